```python
import math
import jax, jax.numpy as jnp
from jax import lax
import numpy as np

D_MODEL = 1024
BATCH = 32
SEQ = 2048
DEPTH = 1

RNN_WIDTH = D_MODEL * 5 // 4
RNN_BLOCKS = 10
RNN_BLOCK = RNN_WIDTH // RNN_BLOCKS
RNN_CONV = 4
LRU_C = 8.0
HEAD_DIM = 128
KV_HEADS = 4
DILATED_CONFIGS = ((128, 1), (512, 4), (2048, 16))
N_GROUPS = len(DILATED_CONFIGS)
Q_HEADS = N_GROUPS * KV_HEADS
ATTN_BLOCK = 128
REL_BUCKETS = 32
REL_MAX_DIST = 2048
FFN_WIDTH = 3 * D_MODEL
FFN_CONV = 3
EPS = 1e-6

IN_SPLITS = (RNN_WIDTH, Q_HEADS * HEAD_DIM, KV_HEADS * HEAD_DIM, KV_HEADS * HEAD_DIM, D_MODEL, D_MODEL)
IN_WIDTH = sum(IN_SPLITS)

kernel_name = "hybrid_rglru_dilated_attn_convffn"


def rms_norm(x, g):
    xf = x.astype(jnp.float32)
    y = xf * lax.rsqrt(jnp.mean(xf * xf, axis=-1, keepdims=True) + EPS)
    return (y * g.astype(jnp.float32)).astype(x.dtype)


def causal_dwconv(x, w, b):
    k = w.shape[0]
    y = lax.conv_general_dilated(x, w[:, None, :].astype(x.dtype), window_strides=(1,),
                                 padding=[(k - 1, 0)], dimension_numbers=("NWC", "WIO", "NWC"),
                                 feature_group_count=x.shape[-1])
    return y + b.astype(x.dtype)


def _t5_bucket(dist):
    max_exact = REL_BUCKETS // 2
    d = np.maximum(dist, 1).astype(np.float32)
    large = max_exact + np.log(d / max_exact) / math.log(REL_MAX_DIST / max_exact) * (REL_BUCKETS - max_exact)
    large = np.minimum(large.astype(np.int32), REL_BUCKETS - 1)
    return np.where(dist < max_exact, dist, large).astype(np.int32)


def _band_structure(n_blocks, dilation, n_back):
    qi = np.arange(ATTN_BLOCK)[None, :, None]
    kj = np.arange(2 * ATTN_BLOCK)[None, None, :]
    nb = np.arange(n_blocks)[:, None, None]
    delta = ATTN_BLOCK + qi - kj
    mask = (delta >= 0) & (delta <= n_back) & ((nb - 1) * ATTN_BLOCK + kj >= 0)
    bucket = _t5_bucket(np.maximum(delta[0], 0) * dilation)
    return mask, bucket


def dilated_group(q, k, v, bias_g, window, dilation):
    B, H, S, hd = q.shape
    r = dilation
    n_back = window // dilation
    M = S // r
    nb = -(-M // ATTN_BLOCK)
    Mp = nb * ATTN_BLOCK

    def to_sub(t):
        return t.reshape(B, H, M, r, hd).transpose(0, 1, 3, 2, 4)

    qs = jnp.pad(to_sub(q), ((0, 0), (0, 0), (0, 0), (0, Mp - M), (0, 0)))
    qs = qs.reshape(B, H, r, nb, ATTN_BLOCK, hd)

    def key_blocks(t):
        ts = jnp.pad(to_sub(t), ((0, 0), (0, 0), (0, 0), (ATTN_BLOCK, Mp - M), (0, 0)))
        ts = ts.reshape(B, H, r, nb + 1, ATTN_BLOCK, hd)
        return jnp.concatenate([ts[:, :, :, :-1], ts[:, :, :, 1:]], axis=4)

    kb = key_blocks(k)
    vb = key_blocks(v)
    mask, bucket = _band_structure(nb, r, n_back)
    bias = bias_g[jnp.asarray(bucket)].astype(jnp.float32).transpose(2, 0, 1)

    logits = jnp.einsum("bhrnqd,bhrnkd->bhrnqk", qs, kb).astype(jnp.float32) * (HEAD_DIM ** -0.5)
    logits = logits + bias[None, :, None, None]
    logits = jnp.where(jnp.asarray(mask)[None, None, None], logits, -jnp.inf)
    mx = jnp.max(logits, axis=-1, keepdims=True)
    p = jnp.exp(logits - mx)
    den = jnp.sum(p, axis=-1, keepdims=True)
    o = jnp.einsum("bhrnqk,bhrnkd->bhrnqd", p, vb.astype(jnp.float32)) / den
    lse = (mx + jnp.log(den))[..., 0]

    o = o.reshape(B, H, r, Mp, hd)[:, :, :, :M].transpose(0, 1, 3, 2, 4).reshape(B, H, S, hd)
    lse = lse.reshape(B, H, r, Mp)[:, :, :, :M].transpose(0, 1, 3, 2).reshape(B, H, S)
    return o, lse


def rg_lru(xc, w_a, b_a, w_x, b_x, lam):
    B, S, C = xc.shape
    xb = xc.reshape(B, S, RNN_BLOCKS, RNN_BLOCK)
    r = jax.nn.sigmoid(jnp.einsum("bsnc,ncd->bsnd", xb, w_a).reshape(B, S, C).astype(jnp.float32)
                       + b_a.astype(jnp.float32))
    i = jax.nn.sigmoid(jnp.einsum("bsnc,ncd->bsnd", xb, w_x).reshape(B, S, C).astype(jnp.float32)
                       + b_x.astype(jnp.float32))
    log_a = -LRU_C * r * jax.nn.softplus(-lam.astype(jnp.float32))
    a = jnp.exp(log_a)
    u = jnp.sqrt(-jnp.expm1(2.0 * log_a)) * (i * xc.astype(jnp.float32))

    def step(h, inp):
        a_t, u_t = inp
        h = a_t * h + u_t
        return h, h

    _, hs = lax.scan(step, jnp.zeros((B, C), jnp.float32), (a.transpose(1, 0, 2), u.transpose(1, 0, 2)))
    return hs.transpose(1, 0, 2)


def _fwd_setup_inputs(seed: int = 0) -> dict:
    key = jax.random.key(seed)
    ks = jax.random.split(key, 24)
    f32 = jnp.float32
    L, D = DEPTH, D_MODEL

    def nrm(k, shape, scale):
        return jax.random.normal(k, shape, f32) * scale

    u = jax.random.uniform(ks[10], (L, RNN_WIDTH), f32, minval=0.9, maxval=0.999)
    p = u ** (1.0 / LRU_C)
    lru_lambda = jnp.log(p) - jnp.log1p(-p)
    return {
        "x": nrm(ks[0], (BATCH, SEQ, D), 1.0),
        "rel_bias": nrm(ks[1], (REL_BUCKETS, Q_HEADS), 0.1),
        "norm_mix_pre": 1.0 + nrm(ks[2], (L, D), 0.05),
        "norm_mix_post": 1.0 + nrm(ks[3], (L, D), 0.05),
        "w_in": nrm(ks[4], (L, D, IN_WIDTH), D ** -0.5),
        "conv_rnn_w": nrm(ks[5], (L, RNN_CONV, RNN_WIDTH), RNN_CONV ** -0.5),
        "conv_rnn_b": nrm(ks[6], (L, RNN_WIDTH), 0.01),
        "w_rg_a": nrm(ks[7], (L, RNN_BLOCKS, RNN_BLOCK, RNN_BLOCK), RNN_BLOCK ** -0.5),
        "b_rg_a": nrm(ks[8], (L, RNN_WIDTH), 0.01),
        "w_rg_x": nrm(ks[9], (L, RNN_BLOCKS, RNN_BLOCK, RNN_BLOCK), RNN_BLOCK ** -0.5),
        "b_rg_x": nrm(ks[11], (L, RNN_WIDTH), 0.01),
        "lru_lambda": lru_lambda,
        "w_branch_rnn": nrm(ks[12], (L, RNN_WIDTH, D), RNN_WIDTH ** -0.5),
        "w_branch_att": nrm(ks[13], (L, KV_HEADS * HEAD_DIM, D), (KV_HEADS * HEAD_DIM) ** -0.5),
        "w_out": nrm(ks[14], (L, D, D), D ** -0.5),
        "norm_ffn_pre": 1.0 + nrm(ks[15], (L, D), 0.05),
        "norm_ffn_post": 1.0 + nrm(ks[16], (L, D), 0.05),
        "w_ffn_gate": nrm(ks[17], (L, D, FFN_WIDTH), D ** -0.5),
        "w_ffn_up": nrm(ks[18], (L, D, FFN_WIDTH), D ** -0.5),
        "conv_ffn_w": nrm(ks[19], (L, FFN_CONV, FFN_WIDTH), FFN_CONV ** -0.5),
        "conv_ffn_b": nrm(ks[20], (L, FFN_WIDTH), 0.01),
        "w_ffn_down": nrm(ks[21], (L, FFN_WIDTH, D), FFN_WIDTH ** -0.5),
    }


def _fwd_reference(x, rel_bias, norm_mix_pre, norm_mix_post, w_in, conv_rnn_w, conv_rnn_b, w_rg_a, b_rg_a,
              w_rg_x, b_rg_x, lru_lambda, w_branch_rnn, w_branch_att, w_out, norm_ffn_pre, norm_ffn_post,
              w_ffn_gate, w_ffn_up, conv_ffn_w, conv_ffn_b, w_ffn_down):
    B, S, D = x.shape
    split_idx = [int(s) for s in np.cumsum(IN_SPLITS)[:-1]]
    h = x
    for l in range(DEPTH):
        hn = rms_norm(h, norm_mix_pre[l])
        proj = hn @ w_in[l]
        xr, q, k, v, g_rnn, g_att = jnp.split(proj, split_idx, axis=-1)

        xc = causal_dwconv(xr, conv_rnn_w[l], conv_rnn_b[l])
        y_rnn = rg_lru(xc, w_rg_a[l], b_rg_a[l], w_rg_x[l], b_rg_x[l], lru_lambda[l]).astype(x.dtype)

        qg = q.reshape(B, S, N_GROUPS, KV_HEADS, HEAD_DIM).transpose(2, 0, 3, 1, 4)
        kh = k.reshape(B, S, KV_HEADS, HEAD_DIM).transpose(0, 2, 1, 3)
        vh = v.reshape(B, S, KV_HEADS, HEAD_DIM).transpose(0, 2, 1, 3)
        outs, lses = [], []
        for g, (window, dilation) in enumerate(DILATED_CONFIGS):
            o_g, lse_g = dilated_group(qg[g], kh, vh, rel_bias[:, g * KV_HEADS:(g + 1) * KV_HEADS],
                                       window, dilation)
            outs.append(o_g)
            lses.append(lse_g)
        alpha = jax.nn.softmax(jnp.stack(lses, axis=0), axis=0)
        o_att = jnp.sum(alpha[..., None] * jnp.stack(outs, axis=0), axis=0)
        o_att = o_att.transpose(0, 2, 1, 3).reshape(B, S, KV_HEADS * HEAD_DIM).astype(x.dtype)

        merged = (jax.nn.sigmoid(g_rnn) * (y_rnn @ w_branch_rnn[l])
                  + jax.nn.sigmoid(g_att) * (o_att @ w_branch_att[l]))
        mix = merged @ w_out[l]
        h = h + rms_norm(mix, norm_mix_post[l])

        hn = rms_norm(h, norm_ffn_pre[l])
        gate = causal_dwconv(hn @ w_ffn_gate[l], conv_ffn_w[l], conv_ffn_b[l])
        ff = (jax.nn.gelu(gate, approximate=True) * (hn @ w_ffn_up[l])) @ w_ffn_down[l]
        h = h + rms_norm(ff, norm_ffn_post[l])
    return h


import jax as _jax
import jax.numpy as _jnp

TWIN_FORMAT = 'train_step'
FWD_PARAMS = ['x', 'rel_bias', 'norm_mix_pre', 'norm_mix_post', 'w_in', 'conv_rnn_w', 'conv_rnn_b', 'w_rg_a', 'b_rg_a', 'w_rg_x', 'b_rg_x', 'lru_lambda', 'w_branch_rnn', 'w_branch_att', 'w_out', 'norm_ffn_pre', 'norm_ffn_post', 'w_ffn_gate', 'w_ffn_up', 'conv_ffn_w', 'conv_ffn_b', 'w_ffn_down']
TWIN_WEIGHTS = ['rel_bias', 'norm_mix_pre', 'norm_mix_post', 'w_in', 'conv_rnn_w', 'conv_rnn_b', 'w_rg_a', 'b_rg_a', 'w_rg_x', 'b_rg_x', 'lru_lambda', 'w_branch_rnn', 'w_branch_att', 'w_out', 'norm_ffn_pre', 'norm_ffn_post', 'w_ffn_gate', 'w_ffn_up', 'conv_ffn_w', 'conv_ffn_b', 'w_ffn_down']
TWIN_DIFF_INPUT = 'x'
TWIN_INPUTS = ['x', 'rel_bias', 'norm_mix_pre', 'norm_mix_post', 'w_in', 'conv_rnn_w', 'conv_rnn_b', 'w_rg_a', 'b_rg_a', 'w_rg_x', 'b_rg_x', 'lru_lambda', 'w_branch_rnn', 'w_branch_att', 'w_out', 'norm_ffn_pre', 'norm_ffn_post', 'w_ffn_gate', 'w_ffn_up', 'conv_ffn_w', 'conv_ffn_b', 'w_ffn_down', 'loss_target', 'm_rel_bias', 'm_norm_mix_pre', 'm_norm_mix_post', 'm_w_in', 'm_conv_rnn_w', 'm_conv_rnn_b', 'm_w_rg_a', 'm_b_rg_a', 'm_w_rg_x', 'm_b_rg_x', 'm_lru_lambda', 'm_w_branch_rnn', 'm_w_branch_att', 'm_w_out', 'm_norm_ffn_pre', 'm_norm_ffn_post', 'm_w_ffn_gate', 'm_w_ffn_up', 'm_conv_ffn_w', 'm_conv_ffn_b', 'm_w_ffn_down', 'v_rel_bias', 'v_norm_mix_pre', 'v_norm_mix_post', 'v_w_in', 'v_conv_rnn_w', 'v_conv_rnn_b', 'v_w_rg_a', 'v_b_rg_a', 'v_w_rg_x', 'v_b_rg_x', 'v_lru_lambda', 'v_w_branch_rnn', 'v_w_branch_att', 'v_w_out', 'v_norm_ffn_pre', 'v_norm_ffn_post', 'v_w_ffn_gate', 'v_w_ffn_up', 'v_conv_ffn_w', 'v_conv_ffn_b', 'v_w_ffn_down']
TWIN_OUTPUTS = ['loss', 'grad_x', 'grad_rel_bias', 'grad_norm_mix_pre', 'grad_norm_mix_post', 'grad_w_in', 'grad_conv_rnn_w', 'grad_conv_rnn_b', 'grad_w_rg_a', 'grad_b_rg_a', 'grad_w_rg_x', 'grad_b_rg_x', 'grad_lru_lambda', 'grad_w_branch_rnn', 'grad_w_branch_att', 'grad_w_out', 'grad_norm_ffn_pre', 'grad_norm_ffn_post', 'grad_w_ffn_gate', 'grad_w_ffn_up', 'grad_conv_ffn_w', 'grad_conv_ffn_b', 'grad_w_ffn_down', 'delta_rel_bias', 'delta_norm_mix_pre', 'delta_norm_mix_post', 'delta_w_in', 'delta_conv_rnn_w', 'delta_conv_rnn_b', 'delta_w_rg_a', 'delta_b_rg_a', 'delta_w_rg_x', 'delta_b_rg_x', 'delta_lru_lambda', 'delta_w_branch_rnn', 'delta_w_branch_att', 'delta_w_out', 'delta_norm_ffn_pre', 'delta_norm_ffn_post', 'delta_w_ffn_gate', 'delta_w_ffn_up', 'delta_conv_ffn_w', 'delta_conv_ffn_b', 'delta_w_ffn_down', 'new_m_rel_bias', 'new_m_norm_mix_pre', 'new_m_norm_mix_post', 'new_m_w_in', 'new_m_conv_rnn_w', 'new_m_conv_rnn_b', 'new_m_w_rg_a', 'new_m_b_rg_a', 'new_m_w_rg_x', 'new_m_b_rg_x', 'new_m_lru_lambda', 'new_m_w_branch_rnn', 'new_m_w_branch_att', 'new_m_w_out', 'new_m_norm_ffn_pre', 'new_m_norm_ffn_post', 'new_m_w_ffn_gate', 'new_m_w_ffn_up', 'new_m_conv_ffn_w', 'new_m_conv_ffn_b', 'new_m_w_ffn_down', 'new_v_rel_bias', 'new_v_norm_mix_pre', 'new_v_norm_mix_post', 'new_v_w_in', 'new_v_conv_rnn_w', 'new_v_conv_rnn_b', 'new_v_w_rg_a', 'new_v_b_rg_a', 'new_v_w_rg_x', 'new_v_b_rg_x', 'new_v_lru_lambda', 'new_v_w_branch_rnn', 'new_v_w_branch_att', 'new_v_w_out', 'new_v_norm_ffn_pre', 'new_v_norm_ffn_post', 'new_v_w_ffn_gate', 'new_v_w_ffn_up', 'new_v_conv_ffn_w', 'new_v_conv_ffn_b', 'new_v_w_ffn_down']
TWIN_LEAF_KINDS = {'loss': 'loss', 'grad_x': 'grad_x', 'grad_rel_bias': 'grad_w', 'grad_norm_mix_pre': 'grad_w', 'grad_norm_mix_post': 'grad_w', 'grad_w_in': 'grad_w', 'grad_conv_rnn_w': 'grad_w', 'grad_conv_rnn_b': 'grad_w', 'grad_w_rg_a': 'grad_w', 'grad_b_rg_a': 'grad_w', 'grad_w_rg_x': 'grad_w', 'grad_b_rg_x': 'grad_w', 'grad_lru_lambda': 'grad_w', 'grad_w_branch_rnn': 'grad_w', 'grad_w_branch_att': 'grad_w', 'grad_w_out': 'grad_w', 'grad_norm_ffn_pre': 'grad_w', 'grad_norm_ffn_post': 'grad_w', 'grad_w_ffn_gate': 'grad_w', 'grad_w_ffn_up': 'grad_w', 'grad_conv_ffn_w': 'grad_w', 'grad_conv_ffn_b': 'grad_w', 'grad_w_ffn_down': 'grad_w', 'delta_rel_bias': 'delta_w', 'delta_norm_mix_pre': 'delta_w', 'delta_norm_mix_post': 'delta_w', 'delta_w_in': 'delta_w', 'delta_conv_rnn_w': 'delta_w', 'delta_conv_rnn_b': 'delta_w', 'delta_w_rg_a': 'delta_w', 'delta_b_rg_a': 'delta_w', 'delta_w_rg_x': 'delta_w', 'delta_b_rg_x': 'delta_w', 'delta_lru_lambda': 'delta_w', 'delta_w_branch_rnn': 'delta_w', 'delta_w_branch_att': 'delta_w', 'delta_w_out': 'delta_w', 'delta_norm_ffn_pre': 'delta_w', 'delta_norm_ffn_post': 'delta_w', 'delta_w_ffn_gate': 'delta_w', 'delta_w_ffn_up': 'delta_w', 'delta_conv_ffn_w': 'delta_w', 'delta_conv_ffn_b': 'delta_w', 'delta_w_ffn_down': 'delta_w', 'new_m_rel_bias': 'new_m', 'new_m_norm_mix_pre': 'new_m', 'new_m_norm_mix_post': 'new_m', 'new_m_w_in': 'new_m', 'new_m_conv_rnn_w': 'new_m', 'new_m_conv_rnn_b': 'new_m', 'new_m_w_rg_a': 'new_m', 'new_m_b_rg_a': 'new_m', 'new_m_w_rg_x': 'new_m', 'new_m_b_rg_x': 'new_m', 'new_m_lru_lambda': 'new_m', 'new_m_w_branch_rnn': 'new_m', 'new_m_w_branch_att': 'new_m', 'new_m_w_out': 'new_m', 'new_m_norm_ffn_pre': 'new_m', 'new_m_norm_ffn_post': 'new_m', 'new_m_w_ffn_gate': 'new_m', 'new_m_w_ffn_up': 'new_m', 'new_m_conv_ffn_w': 'new_m', 'new_m_conv_ffn_b': 'new_m', 'new_m_w_ffn_down': 'new_m', 'new_v_rel_bias': 'new_v', 'new_v_norm_mix_pre': 'new_v', 'new_v_norm_mix_post': 'new_v', 'new_v_w_in': 'new_v', 'new_v_conv_rnn_w': 'new_v', 'new_v_conv_rnn_b': 'new_v', 'new_v_w_rg_a': 'new_v', 'new_v_b_rg_a': 'new_v', 'new_v_w_rg_x': 'new_v', 'new_v_b_rg_x': 'new_v', 'new_v_lru_lambda': 'new_v', 'new_v_w_branch_rnn': 'new_v', 'new_v_w_branch_att': 'new_v', 'new_v_w_out': 'new_v', 'new_v_norm_ffn_pre': 'new_v', 'new_v_norm_ffn_post': 'new_v', 'new_v_w_ffn_gate': 'new_v', 'new_v_w_ffn_up': 'new_v', 'new_v_conv_ffn_w': 'new_v', 'new_v_conv_ffn_b': 'new_v', 'new_v_w_ffn_down': 'new_v'}


def _forward(args):
    return _fwd_reference(*[args[k] for k in FWD_PARAMS])


def _output_shape():
    out = _jax.eval_shape(lambda: _forward(_fwd_setup_inputs(0)))
    return out.shape, out.dtype

N_MICROBATCH = 1
ADAM_LR = 0.001
ADAM_B1 = 0.9
ADAM_B2 = 0.999
ADAM_EPS = 1e-08
ADAM_WD = 0.01
ADAM_STEP = 10
PER_EXAMPLE_BATCH_AXIS = {'x': 0, 'loss_target': 0}
SHARED_INPUTS = []
_WEIGHT_DTYPES = {'rel_bias': _jnp.float32, 'norm_mix_pre': _jnp.float32, 'norm_mix_post': _jnp.float32, 'w_in': _jnp.float32, 'conv_rnn_w': _jnp.float32, 'conv_rnn_b': _jnp.float32, 'w_rg_a': _jnp.float32, 'b_rg_a': _jnp.float32, 'w_rg_x': _jnp.float32, 'b_rg_x': _jnp.float32, 'lru_lambda': _jnp.float32, 'w_branch_rnn': _jnp.float32, 'w_branch_att': _jnp.float32, 'w_out': _jnp.float32, 'norm_ffn_pre': _jnp.float32, 'norm_ffn_post': _jnp.float32, 'w_ffn_gate': _jnp.float32, 'w_ffn_up': _jnp.float32, 'conv_ffn_w': _jnp.float32, 'conv_ffn_b': _jnp.float32, 'w_ffn_down': _jnp.float32}
MOMENT_SCALE = {'rel_bias': 3.101233e-01, 'norm_mix_pre': 1.748940e+00, 'norm_mix_post': 6.675698e+01, 'w_in': 6.842802e-01, 'conv_rnn_w': 2.768603e+00, 'conv_rnn_b': 8.613823e+01, 'w_rg_a': 2.679146e+00, 'b_rg_a': 1.383686e+00, 'w_rg_x': 4.774530e+00, 'b_rg_x': 7.905959e-01, 'lru_lambda': 2.108050e+00, 'w_branch_rnn': 4.593310e+00, 'w_branch_att': 3.200269e-01, 'w_out': 4.314244e+00, 'norm_ffn_pre': 3.197295e+00, 'norm_ffn_post': 6.438177e+01, 'w_ffn_gate': 7.370587e-01, 'w_ffn_up': 1.769639e+00, 'conv_ffn_w': 1.485387e+00, 'conv_ffn_b': 2.842220e+00, 'w_ffn_down': 3.185856e+00}


def _to_microbatches(a, axis):
    t = _jnp.moveaxis(a, axis, 0)
    t = t.reshape((N_MICROBATCH, t.shape[0] // N_MICROBATCH) + t.shape[1:])
    return _jnp.moveaxis(t, 1, axis + 1)


def setup_inputs(seed: int = 0) -> dict:
    inp = _fwd_setup_inputs(seed)
    key = _jax.random.fold_in(_jax.random.key(seed), 7919)
    shape, _ = _output_shape()
    out = dict(inp)
    out["loss_target"] = _jax.random.normal(_jax.random.fold_in(key, 0), shape, _jnp.float32)
    for i, name in enumerate(TWIN_WEIGHTS):
        w = inp[name].astype(_jnp.float32)
        if MOMENT_SCALE is None:
            s = _jnp.sqrt(_jnp.mean(_jnp.square(w)) + 1e-30)
        else:
            s = MOMENT_SCALE[name]
        km, kv = _jax.random.split(_jax.random.fold_in(key, i + 1))
        out[name] = w
        out["m_" + name] = s * _jax.random.normal(km, w.shape, _jnp.float32)
        out["v_" + name] = (s * s) * _jax.random.uniform(kv, w.shape, _jnp.float32, 0.5, 1.5)
    if N_MICROBATCH > 1:
        for name, axis in PER_EXAMPLE_BATCH_AXIS.items():
            out[name] = _to_microbatches(out[name], axis)
    return {'x': out['x'], 'rel_bias': out['rel_bias'], 'norm_mix_pre': out['norm_mix_pre'], 'norm_mix_post': out['norm_mix_post'], 'w_in': out['w_in'], 'conv_rnn_w': out['conv_rnn_w'], 'conv_rnn_b': out['conv_rnn_b'], 'w_rg_a': out['w_rg_a'], 'b_rg_a': out['b_rg_a'], 'w_rg_x': out['w_rg_x'], 'b_rg_x': out['b_rg_x'], 'lru_lambda': out['lru_lambda'], 'w_branch_rnn': out['w_branch_rnn'], 'w_branch_att': out['w_branch_att'], 'w_out': out['w_out'], 'norm_ffn_pre': out['norm_ffn_pre'], 'norm_ffn_post': out['norm_ffn_post'], 'w_ffn_gate': out['w_ffn_gate'], 'w_ffn_up': out['w_ffn_up'], 'conv_ffn_w': out['conv_ffn_w'], 'conv_ffn_b': out['conv_ffn_b'], 'w_ffn_down': out['w_ffn_down'], 'loss_target': out['loss_target'], 'm_rel_bias': out['m_rel_bias'], 'm_norm_mix_pre': out['m_norm_mix_pre'], 'm_norm_mix_post': out['m_norm_mix_post'], 'm_w_in': out['m_w_in'], 'm_conv_rnn_w': out['m_conv_rnn_w'], 'm_conv_rnn_b': out['m_conv_rnn_b'], 'm_w_rg_a': out['m_w_rg_a'], 'm_b_rg_a': out['m_b_rg_a'], 'm_w_rg_x': out['m_w_rg_x'], 'm_b_rg_x': out['m_b_rg_x'], 'm_lru_lambda': out['m_lru_lambda'], 'm_w_branch_rnn': out['m_w_branch_rnn'], 'm_w_branch_att': out['m_w_branch_att'], 'm_w_out': out['m_w_out'], 'm_norm_ffn_pre': out['m_norm_ffn_pre'], 'm_norm_ffn_post': out['m_norm_ffn_post'], 'm_w_ffn_gate': out['m_w_ffn_gate'], 'm_w_ffn_up': out['m_w_ffn_up'], 'm_conv_ffn_w': out['m_conv_ffn_w'], 'm_conv_ffn_b': out['m_conv_ffn_b'], 'm_w_ffn_down': out['m_w_ffn_down'], 'v_rel_bias': out['v_rel_bias'], 'v_norm_mix_pre': out['v_norm_mix_pre'], 'v_norm_mix_post': out['v_norm_mix_post'], 'v_w_in': out['v_w_in'], 'v_conv_rnn_w': out['v_conv_rnn_w'], 'v_conv_rnn_b': out['v_conv_rnn_b'], 'v_w_rg_a': out['v_w_rg_a'], 'v_b_rg_a': out['v_b_rg_a'], 'v_w_rg_x': out['v_w_rg_x'], 'v_b_rg_x': out['v_b_rg_x'], 'v_lru_lambda': out['v_lru_lambda'], 'v_w_branch_rnn': out['v_w_branch_rnn'], 'v_w_branch_att': out['v_w_branch_att'], 'v_w_out': out['v_w_out'], 'v_norm_ffn_pre': out['v_norm_ffn_pre'], 'v_norm_ffn_post': out['v_norm_ffn_post'], 'v_w_ffn_gate': out['v_w_ffn_gate'], 'v_w_ffn_up': out['v_w_ffn_up'], 'v_conv_ffn_w': out['v_conv_ffn_w'], 'v_conv_ffn_b': out['v_conv_ffn_b'], 'v_w_ffn_down': out['v_w_ffn_down']}


def _loss(weights, diff, rest, loss_target):
    with _jax.named_scope("forward"):
        args = {**rest, TWIN_DIFF_INPUT: diff, **{k: w.astype(_WEIGHT_DTYPES[k]) for k, w in weights.items()}}
        y = _forward(args)
    with _jax.named_scope("loss_head"):
        err = _jnp.square(y.astype(_jnp.float32) - loss_target)
        return 0.5 * _jnp.sum(_jnp.mean(err, axis=-1)) if err.ndim else 0.5 * err


def _adamw(w, g, m, v):
    m = ADAM_B1 * m + (1.0 - ADAM_B1) * g
    v = ADAM_B2 * v + (1.0 - ADAM_B2) * _jnp.square(g)
    m_hat = m / (1.0 - ADAM_B1 ** ADAM_STEP)
    v_hat = v / (1.0 - ADAM_B2 ** ADAM_STEP)
    delta = -ADAM_LR * (m_hat / (_jnp.sqrt(v_hat) + ADAM_EPS) + ADAM_WD * w)
    return delta, m, v


def reference(x, rel_bias, norm_mix_pre, norm_mix_post, w_in, conv_rnn_w, conv_rnn_b, w_rg_a, b_rg_a, w_rg_x, b_rg_x, lru_lambda, w_branch_rnn, w_branch_att, w_out, norm_ffn_pre, norm_ffn_post, w_ffn_gate, w_ffn_up, conv_ffn_w, conv_ffn_b, w_ffn_down, loss_target, m_rel_bias, m_norm_mix_pre, m_norm_mix_post, m_w_in, m_conv_rnn_w, m_conv_rnn_b, m_w_rg_a, m_b_rg_a, m_w_rg_x, m_b_rg_x, m_lru_lambda, m_w_branch_rnn, m_w_branch_att, m_w_out, m_norm_ffn_pre, m_norm_ffn_post, m_w_ffn_gate, m_w_ffn_up, m_conv_ffn_w, m_conv_ffn_b, m_w_ffn_down, v_rel_bias, v_norm_mix_pre, v_norm_mix_post, v_w_in, v_conv_rnn_w, v_conv_rnn_b, v_w_rg_a, v_b_rg_a, v_w_rg_x, v_b_rg_x, v_lru_lambda, v_w_branch_rnn, v_w_branch_att, v_w_out, v_norm_ffn_pre, v_norm_ffn_post, v_w_ffn_gate, v_w_ffn_up, v_conv_ffn_w, v_conv_ffn_b, v_w_ffn_down):
    given = dict(x=x, rel_bias=rel_bias, norm_mix_pre=norm_mix_pre, norm_mix_post=norm_mix_post, w_in=w_in, conv_rnn_w=conv_rnn_w, conv_rnn_b=conv_rnn_b, w_rg_a=w_rg_a, b_rg_a=b_rg_a, w_rg_x=w_rg_x, b_rg_x=b_rg_x, lru_lambda=lru_lambda, w_branch_rnn=w_branch_rnn, w_branch_att=w_branch_att, w_out=w_out, norm_ffn_pre=norm_ffn_pre, norm_ffn_post=norm_ffn_post, w_ffn_gate=w_ffn_gate, w_ffn_up=w_ffn_up, conv_ffn_w=conv_ffn_w, conv_ffn_b=conv_ffn_b, w_ffn_down=w_ffn_down, loss_target=loss_target, m_rel_bias=m_rel_bias, m_norm_mix_pre=m_norm_mix_pre, m_norm_mix_post=m_norm_mix_post, m_w_in=m_w_in, m_conv_rnn_w=m_conv_rnn_w, m_conv_rnn_b=m_conv_rnn_b, m_w_rg_a=m_w_rg_a, m_b_rg_a=m_b_rg_a, m_w_rg_x=m_w_rg_x, m_b_rg_x=m_b_rg_x, m_lru_lambda=m_lru_lambda, m_w_branch_rnn=m_w_branch_rnn, m_w_branch_att=m_w_branch_att, m_w_out=m_w_out, m_norm_ffn_pre=m_norm_ffn_pre, m_norm_ffn_post=m_norm_ffn_post, m_w_ffn_gate=m_w_ffn_gate, m_w_ffn_up=m_w_ffn_up, m_conv_ffn_w=m_conv_ffn_w, m_conv_ffn_b=m_conv_ffn_b, m_w_ffn_down=m_w_ffn_down, v_rel_bias=v_rel_bias, v_norm_mix_pre=v_norm_mix_pre, v_norm_mix_post=v_norm_mix_post, v_w_in=v_w_in, v_conv_rnn_w=v_conv_rnn_w, v_conv_rnn_b=v_conv_rnn_b, v_w_rg_a=v_w_rg_a, v_b_rg_a=v_b_rg_a, v_w_rg_x=v_w_rg_x, v_b_rg_x=v_b_rg_x, v_lru_lambda=v_lru_lambda, v_w_branch_rnn=v_w_branch_rnn, v_w_branch_att=v_w_branch_att, v_w_out=v_w_out, v_norm_ffn_pre=v_norm_ffn_pre, v_norm_ffn_post=v_norm_ffn_post, v_w_ffn_gate=v_w_ffn_gate, v_w_ffn_up=v_w_ffn_up, v_conv_ffn_w=v_conv_ffn_w, v_conv_ffn_b=v_conv_ffn_b, v_w_ffn_down=v_w_ffn_down)
    weights = {n: given[n] for n in TWIN_WEIGHTS}
    shared = {n: given[n] for n in SHARED_INPUTS}
    per_example = {n: given[n] for n in ['x']}
    grad_fn = _jax.value_and_grad(_loss, argnums=(0, 1))

    def one_microbatch(ex, loss_target):
        ex = dict(ex)
        diff = ex.pop(TWIN_DIFF_INPUT)
        return grad_fn(weights, diff, {**shared, **ex}, loss_target)

    if N_MICROBATCH == 1:
        loss, (grad_w, grad_x) = one_microbatch(per_example, given["loss_target"])
    else:
        def body(carry, xs):
            loss_sum, grad_sum = carry
            l_k, (gw_k, gx_k) = one_microbatch(xs[0], xs[1])
            with _jax.named_scope("update"):
                return (loss_sum + l_k, _jax.tree.map(_jnp.add, grad_sum, gw_k)), gx_k

        init = (_jnp.zeros((), _jnp.float32), _jax.tree.map(_jnp.zeros_like, weights))
        (loss, grad_w), grad_x = _jax.lax.scan(body, init, (per_example, given["loss_target"]))
    with _jax.named_scope("update"):
        delta_w, new_m, new_v = {}, {}, {}
        for n in TWIN_WEIGHTS:
            delta_w[n], new_m[n], new_v[n] = _adamw(weights[n], grad_w[n], given["m_" + n], given["v_" + n])
    return (loss, grad_x, *[grad_w[n] for n in TWIN_WEIGHTS], *[delta_w[n] for n in TWIN_WEIGHTS],
            *[new_m[n] for n in TWIN_WEIGHTS], *[new_v[n] for n in TWIN_WEIGHTS])
```

```python
import functools
import math

import numpy as np
import jax
import jax.numpy as jnp
from jax import lax
from jax.experimental import pallas as pl
from jax.experimental.pallas import tpu as pltpu

F32 = jnp.float32
BF16 = jnp.bfloat16

EPS = 1e-6
HEAD_DIM = 128
ATTN_BLOCK = 128
DILATED = ((128, 1), (512, 4), (2048, 16))
N_GROUPS = len(DILATED)
REL_BUCKETS = 32
REL_MAX_DIST = 2048
LRU_C = 8.0
NEG = -1e30

ADAM_LR = 0.001
ADAM_B1 = 0.9
ADAM_B2 = 0.999
ADAM_EPS = 1e-08
ADAM_WD = 0.01
ADAM_STEP = 10

N_CHIPS = 4
PACK_W = 1024
PACK_ROWS = 16
VMEM_LIMIT = 56 * 1024 * 1024
MESH = pl.DeviceIdType.MESH


def _params(sem=None):
    return pltpu.CompilerParams(dimension_semantics=sem, vmem_limit_bytes=VMEM_LIMIT)


def _dot(a, b):
    return jnp.dot(a, b, preferred_element_type=F32)


def _dot_nt(a, b):
    return lax.dot_general(a, b, (((1,), (1,)), ((), ())), preferred_element_type=F32)


def _dot_tn(a, b):
    return lax.dot_general(a, b, (((0,), (0,)), ((), ())), preferred_element_type=F32)


def _sig(x):
    return 1.0 / (1.0 + jnp.exp(-x))


def _rows(tm, w):
    return pl.BlockSpec((tm, w), lambda i: (i, 0))


def _whole(shape):
    nd = len(shape)
    return pl.BlockSpec(tuple(shape), lambda *_: (0,) * nd)


def _tile(t, want):
    while t % want:
        want //= 2
    return want


def norm_mm(x, g, ws, out_dtypes, name, tm=256):
    t, d = x.shape
    tm = _tile(t, tm)
    nw = len(ws)

    def body(x_ref, g_ref, *refs):
        w_refs, hn_ref, o_refs = refs[:nw], refs[nw], refs[nw + 1:]
        xv = x_ref[...]
        inv = lax.rsqrt(jnp.mean(xv * xv, axis=-1, keepdims=True) + EPS)
        hn = (xv * inv * g_ref[...]).astype(BF16)
        hn_ref[...] = hn
        for w_ref, o_ref in zip(w_refs, o_refs):
            o_ref[...] = _dot(hn, w_ref[...]).astype(o_ref.dtype)

    outs = pl.pallas_call(
        body, name=name, grid=(t // tm,),
        in_specs=[_rows(tm, d), _whole(g.shape)] + [_whole(w.shape) for w in ws],
        out_specs=[_rows(tm, d)] + [_rows(tm, w.shape[1]) for w in ws],
        out_shape=[jax.ShapeDtypeStruct((t, d), BF16)]
        + [jax.ShapeDtypeStruct((t, w.shape[1]), dt) for w, dt in zip(ws, out_dtypes)],
        compiler_params=_params(("parallel",)),
    )(x, g, *ws)
    return outs[0], outs[1:]


def mm_nt(dys, ws, out_dtype, name, tm=256):
    t = dys[0].shape[0]
    k = ws[0].shape[0]
    tm = _tile(t, tm)
    n = len(dys)

    def body(*refs):
        dy_refs, w_refs, o_ref = refs[:n], refs[n:2 * n], refs[2 * n]
        acc = None
        for dy_ref, w_ref in zip(dy_refs, w_refs):
            part = _dot_nt(dy_ref[...].astype(BF16), w_ref[...])
            acc = part if acc is None else acc + part
        o_ref[...] = acc.astype(o_ref.dtype)

    return pl.pallas_call(
        body, name=name, grid=(t // tm,),
        in_specs=[_rows(tm, dy.shape[1]) for dy in dys] + [_whole(w.shape) for w in ws],
        out_specs=_rows(tm, k),
        out_shape=jax.ShapeDtypeStruct((t, k), out_dtype),
        compiler_params=_params(("parallel",)),
    )(*dys, *ws)


def mm_tn(a, dys, name, tm=512):
    t, k = a.shape
    tm = _tile(t, tm)
    n = len(dys)

    def body(a_ref, *refs):
        dy_refs, o_refs = refs[:n], refs[n:]

        @pl.when(pl.program_id(0) == 0)
        def _():
            for o_ref in o_refs:
                o_ref[...] = jnp.zeros(o_ref.shape, F32)

        av = a_ref[...].astype(BF16)
        for dy_ref, o_ref in zip(dy_refs, o_refs):
            o_ref[...] += _dot_tn(av, dy_ref[...].astype(BF16))

    return pl.pallas_call(
        body, name=name, grid=(t // tm,),
        in_specs=[_rows(tm, k)] + [_rows(tm, dy.shape[1]) for dy in dys],
        out_specs=[_whole((k, dy.shape[1])) for dy in dys],
        out_shape=[jax.ShapeDtypeStruct((k, dy.shape[1]), F32) for dy in dys],
        compiler_params=_params(("arbitrary",)),
    )(a, *dys)


def mm_norm_res(a, w, g, resid, name, tm=256):
    t, k = a.shape
    d = w.shape[1]
    tm = _tile(t, tm)

    def body(a_ref, w_ref, g_ref, r_ref, p_ref, o_ref):
        prod = _dot(a_ref[...], w_ref[...])
        p_ref[...] = prod
        inv = lax.rsqrt(jnp.mean(prod * prod, axis=-1, keepdims=True) + EPS)
        o_ref[...] = r_ref[...] + prod * inv * g_ref[...]

    return pl.pallas_call(
        body, name=name, grid=(t // tm,),
        in_specs=[_rows(tm, k), _whole(w.shape), _whole(g.shape), _rows(tm, d)],
        out_specs=[_rows(tm, d), _rows(tm, d)],
        out_shape=[jax.ShapeDtypeStruct((t, d), F32)] * 2,
        compiler_params=_params(("parallel",)),
    )(a, w, g, resid)


def _rms_bwd(dz, u, g):
    d = u.shape[-1]
    inv = lax.rsqrt(jnp.mean(u * u, axis=-1, keepdims=True) + EPS)
    dzg = dz * g
    proj = jnp.sum(dzg * u, axis=-1, keepdims=True) * (1.0 / d)
    du = inv * (dzg - u * (inv * inv) * proj)
    dg_rows = dz * u * inv
    return du, dg_rows


def norm_bwd(dz, u, g, add, out_dtype, name, tm=256):
    t, d = u.shape
    tm = _tile(t, tm)
    has_add = add is not None

    def body(*refs):
        if has_add:
            dz_ref, u_ref, g_ref, add_ref, du_ref, dg_ref = refs
        else:
            dz_ref, u_ref, g_ref, du_ref, dg_ref = refs

        @pl.when(pl.program_id(0) == 0)
        def _():
            dg_ref[...] = jnp.zeros(dg_ref.shape, F32)

        du, dg_rows = _rms_bwd(dz_ref[...].astype(F32), u_ref[...], g_ref[...])
        if has_add:
            du = du + add_ref[...]
        du_ref[...] = du.astype(du_ref.dtype)
        dg_ref[...] += jnp.sum(dg_rows, axis=0, keepdims=True)

    ins = [dz, u, g] + ([add] if has_add else [])
    return pl.pallas_call(
        body, name=name, grid=(t // tm,),
        in_specs=[_rows(tm, d), _rows(tm, d), _whole(g.shape)] + ([_rows(tm, d)] if has_add else []),
        out_specs=[_rows(tm, d), _whole((1, d))],
        out_shape=[jax.ShapeDtypeStruct((t, d), out_dtype), jax.ShapeDtypeStruct((1, d), F32)],
        compiler_params=_params(("arbitrary",)),
    )(*ins)


def loss_norm_bwd(y, target, ff, g, name, tm=256):
    t, d = y.shape
    tm = _tile(t, tm)

    def body(y_ref, t_ref, ff_ref, g_ref, dy_ref, dff_ref, dg_ref, loss_ref):
        @pl.when(pl.program_id(0) == 0)
        def _():
            dg_ref[...] = jnp.zeros(dg_ref.shape, F32)
            loss_ref[...] = jnp.zeros(loss_ref.shape, F32)

        err = y_ref[...] - t_ref[...]
        loss_ref[...] += jnp.sum(err * err, axis=0, keepdims=True)
        dy = err * (1.0 / d)
        dy_ref[...] = dy
        du, dg_rows = _rms_bwd(dy, ff_ref[...], g_ref[...])
        dff_ref[...] = du.astype(dff_ref.dtype)
        dg_ref[...] += jnp.sum(dg_rows, axis=0, keepdims=True)

    return pl.pallas_call(
        body, name=name, grid=(t // tm,),
        in_specs=[_rows(tm, d), _rows(tm, d), _rows(tm, d), _whole(g.shape)],
        out_specs=[_rows(tm, d), _rows(tm, d), _whole((1, d)), _whole((1, d))],
        out_shape=[jax.ShapeDtypeStruct((t, d), F32), jax.ShapeDtypeStruct((t, d), BF16),
                   jax.ShapeDtypeStruct((1, d), F32), jax.ShapeDtypeStruct((1, d), F32)],
        compiler_params=_params(("arbitrary",)),
    )(y, target, ff, g)


def merge_fwd(y_rnn, o_att, gts, w_br, w_ba, name, tm=256):
    t = y_rnn.shape[0]
    d = w_br.shape[1]
    tm = _tile(t, tm)

    def body(y_ref, o_ref, g_ref, wbr_ref, wba_ref, m_ref, br_ref, ba_ref):
        br = _dot(y_ref[...].astype(BF16), wbr_ref[...])
        ba = _dot(o_ref[...].astype(BF16), wba_ref[...])
        gv = g_ref[...]
        m_ref[...] = (_sig(gv[:, :d]) * br + _sig(gv[:, d:]) * ba).astype(BF16)
        br_ref[...] = br
        ba_ref[...] = ba

    return pl.pallas_call(
        body, name=name, grid=(t // tm,),
        in_specs=[_rows(tm, y_rnn.shape[1]), _rows(tm, o_att.shape[1]), _rows(tm, 2 * d),
                  _whole(w_br.shape), _whole(w_ba.shape)],
        out_specs=[_rows(tm, d)] * 3,
        out_shape=[jax.ShapeDtypeStruct((t, d), BF16), jax.ShapeDtypeStruct((t, d), F32),
                   jax.ShapeDtypeStruct((t, d), F32)],
        compiler_params=_params(("parallel",)),
    )(y_rnn, o_att, gts, w_br, w_ba)


def merge_bwd(dmerged, gts, br, ba, name, tm=256):
    t, d = dmerged.shape
    tm = _tile(t, tm)

    def body(dm_ref, g_ref, br_ref, ba_ref, dbr_ref, dba_ref, dg_ref):
        dm = dm_ref[...]
        gv = g_ref[...]
        sr = _sig(gv[:, :d])
        sa = _sig(gv[:, d:])
        dbr_ref[...] = (dm * sr).astype(BF16)
        dba_ref[...] = (dm * sa).astype(BF16)
        dg_ref[:, :d] = (dm * br_ref[...] * sr * (1.0 - sr)).astype(BF16)
        dg_ref[:, d:] = (dm * ba_ref[...] * sa * (1.0 - sa)).astype(BF16)

    return pl.pallas_call(
        body, name=name, grid=(t // tm,),
        in_specs=[_rows(tm, d), _rows(tm, 2 * d), _rows(tm, d), _rows(tm, d)],
        out_specs=[_rows(tm, d), _rows(tm, d), _rows(tm, 2 * d)],
        out_shape=[jax.ShapeDtypeStruct((t, d), BF16), jax.ShapeDtypeStruct((t, d), BF16),
                   jax.ShapeDtypeStruct((t, 2 * d), BF16)],
        compiler_params=_params(("parallel",)),
    )(dmerged, gts, br, ba)


def _shift_dn(x, d, fill, row):
    return jnp.where(row >= d, pltpu.roll(x, d, 0), fill)


def _shift_up(x, d, fill, row):
    s = x.shape[0]
    return jnp.where(row < s - d, pltpu.roll(x, s - d, 0), fill)


def _conv_fwd(x, w, b, row):
    kk = w.shape[0]
    y = b + w[kk - 1:kk, :] * x
    for j in range(1, kk):
        y = y + w[kk - 1 - j:kk - j, :] * _shift_dn(x, j, 0.0, row)
    return y


def _conv_bwd(dy, x, w, row):
    kk = w.shape[0]
    dx = w[kk - 1:kk, :] * dy
    dws = [None] * kk
    dws[kk - 1] = jnp.sum(dy * x, axis=0, keepdims=True)
    for j in range(1, kk):
        dx = dx + w[kk - 1 - j:kk - j, :] * _shift_up(dy, j, 0.0, row)
        dws[kk - 1 - j] = jnp.sum(dy * _shift_dn(x, j, 0.0, row), axis=0, keepdims=True)
    return dx, jnp.concatenate(dws, axis=0)


def _neg_expm1(x):
    series = x * (1.0 + x * (1.0 / 2 + x * (1.0 / 6 + x * (1.0 / 24 + x * (1.0 / 120 + x * (1.0 / 720 + x * (1.0 / 5040)))))))
    return -jnp.where(x > -0.3, series, jnp.exp(x) - 1.0)


def _softplus(z):
    y = jnp.exp(-jnp.abs(z))
    u = 1.0 + y
    dd = u - 1.0
    log1p = jnp.where(dd == 0.0, y, jnp.log(u) * (y / jnp.where(dd == 0.0, 1.0, dd)))
    return jnp.maximum(z, 0.0) + log1p


def _lru_gates(xc, wa, ba, wx, bx, lam):
    xb = xc.astype(BF16)
    r = _sig(_dot(xb, wa) + ba)
    i = _sig(_dot(xb, wx) + bx)
    sp = _softplus(-lam)
    la = (-LRU_C) * r * sp
    a = jnp.exp(la)
    mult = jnp.sqrt(_neg_expm1(2.0 * la))
    return r, i, sp, la, a, mult


def _scan_fwd(a, u, row):
    s = a.shape[0]
    d = 1
    while d < s:
        u = u + a * _shift_dn(u, d, 0.0, row)
        a = a * _shift_dn(a, d, 1.0, row)
        d *= 2
    return u


def _scan_bwd(b, g, row):
    s = b.shape[0]
    d = 1
    while d < s:
        g = g + b * _shift_up(g, d, 0.0, row)
        b = b * _shift_up(b, d, 1.0, row)
        d *= 2
    return g


def rglru_fwd(xr, cw, cb, wa, ba, wx, bx, lam, name):
    b, s, c = xr.shape
    nb, rb = wa.shape[0], wa.shape[1]
    kk = cw.shape[0]

    def body(x_ref, cw_ref, cb_ref, wa_ref, ba_ref, wx_ref, bx_ref, lam_ref, h_ref):
        row = lax.broadcasted_iota(jnp.int32, (s, rb), 0)
        xc = _conv_fwd(x_ref[...], cw_ref[...], cb_ref[...], row)
        _, i, _, _, a, mult = _lru_gates(xc, wa_ref[...], ba_ref[...], wx_ref[...], bx_ref[...], lam_ref[...])
        h_ref[...] = _scan_fwd(a, mult * (i * xc), row)

    vec = pl.BlockSpec((1, rb), lambda bi, n: (0, n))
    seq = pl.BlockSpec((None, s, rb), lambda bi, n: (bi, 0, n))
    mat = pl.BlockSpec((None, rb, rb), lambda bi, n: (n, 0, 0))
    return pl.pallas_call(
        body, name=name, grid=(b, nb),
        in_specs=[seq, pl.BlockSpec((kk, rb), lambda bi, n: (0, n)), vec, mat, vec, mat, vec, vec],
        out_specs=seq,
        out_shape=jax.ShapeDtypeStruct((b, s, c), F32),
        compiler_params=_params(("parallel", "parallel")),
    )(xr, cw, cb, wa, ba, wx, bx, lam)


def rglru_bwd(xr, h, dh, cw, cb, wa, ba, wx, bx, lam, name):
    b, s, c = xr.shape
    nb, rb = wa.shape[0], wa.shape[1]
    kk = cw.shape[0]

    def body(x_ref, h_ref, dh_ref, cw_ref, cb_ref, wa_ref, ba_ref, wx_ref, bx_ref, lam_ref,
             dx_ref, dcw_ref, dcb_ref, dwa_ref, dba_ref, dwx_ref, dbx_ref, dlam_ref):
        @pl.when(pl.program_id(1) == 0)
        def _():
            for ref in (dcw_ref, dcb_ref, dwa_ref, dba_ref, dwx_ref, dbx_ref, dlam_ref):
                ref[...] = jnp.zeros(ref.shape, F32)

        row = lax.broadcasted_iota(jnp.int32, (s, rb), 0)
        x = x_ref[...]
        cwv = cw_ref[...]
        xc = _conv_fwd(x, cwv, cb_ref[...], row)
        wav, wxv, lamv = wa_ref[...], wx_ref[...], lam_ref[...]
        r, i, sp, la, a, mult = _lru_gates(xc, wav, ba_ref[...], wxv, bx_ref[...], lamv)
        lmb = _scan_bwd(_shift_up(a, 1, 0.0, row), dh_ref[...], row)
        h_prev = _shift_dn(h_ref[...], 1, 0.0, row)
        da = lmb * h_prev
        ixc = i * xc
        dla = da * a - (lmb * ixc) * (a * a) / mult
        di = lmb * mult * xc
        dxc = lmb * mult * i
        dr = dla * ((-LRU_C) * sp)
        dsp = jnp.sum(dla * ((-LRU_C) * r), axis=0, keepdims=True)
        dga = dr * r * (1.0 - r)
        dgx = di * i * (1.0 - i)
        dga_b, dgx_b = dga.astype(BF16), dgx.astype(BF16)
        xb = xc.astype(BF16)
        dwa_ref[...] += _dot_tn(xb, dga_b)
        dwx_ref[...] += _dot_tn(xb, dgx_b)
        dba_ref[...] += jnp.sum(dga, axis=0, keepdims=True)
        dbx_ref[...] += jnp.sum(dgx, axis=0, keepdims=True)
        dlam_ref[...] += dsp * (-_sig(-lamv))
        dxc = dxc + _dot_nt(dga_b, wav) + _dot_nt(dgx_b, wxv)
        dcb_ref[...] += jnp.sum(dxc, axis=0, keepdims=True)
        dx, dcw = _conv_bwd(dxc, x, cwv, row)
        dcw_ref[...] += dcw
        dx_ref[...] = dx.astype(dx_ref.dtype)

    vec = pl.BlockSpec((1, rb), lambda n, bi: (0, n))
    seq = pl.BlockSpec((None, s, rb), lambda n, bi: (bi, 0, n))
    mat = pl.BlockSpec((None, rb, rb), lambda n, bi: (n, 0, 0))
    cws = pl.BlockSpec((kk, rb), lambda n, bi: (0, n))
    sd = jax.ShapeDtypeStruct
    return pl.pallas_call(
        body, name=name, grid=(nb, b),
        in_specs=[seq, seq, seq, cws, vec, mat, vec, mat, vec, vec],
        out_specs=[seq, cws, vec, mat, vec, mat, vec, vec],
        out_shape=[sd((b, s, c), BF16), sd((kk, c), F32), sd((1, c), F32), sd((nb, rb, rb), F32),
                   sd((1, c), F32), sd((nb, rb, rb), F32), sd((1, c), F32), sd((1, c), F32)],
        compiler_params=_params(("parallel", "arbitrary")),
    )(xr, h, dh, cw, cb, wa, ba, wx, bx, lam)


_GELU_C = math.sqrt(2.0 / math.pi)


def _gelu_parts(x):
    th = jnp.tanh(_GELU_C * (x + 0.044715 * x * x * x))
    gel = 0.5 * x * (1.0 + th)
    dgel = 0.5 * (1.0 + th) + 0.5 * x * (1.0 - th * th) * _GELU_C * (1.0 + 3 * 0.044715 * x * x)
    return gel, dgel


def ffn_act(gate_pre, up, cw, cb, name, cbk=256):
    b, s, f = gate_pre.shape
    kk = cw.shape[0]
    cbk = _tile(f, cbk)

    def body(g_ref, u_ref, cw_ref, cb_ref, a_ref):
        row = lax.broadcasted_iota(jnp.int32, (s, cbk), 0)
        gate = _conv_fwd(g_ref[...], cw_ref[...], cb_ref[...], row)
        gel, _ = _gelu_parts(gate)
        a_ref[...] = (gel * u_ref[...]).astype(BF16)

    seq = pl.BlockSpec((None, s, cbk), lambda bi, n: (bi, 0, n))
    return pl.pallas_call(
        body, name=name, grid=(b, f // cbk),
        in_specs=[seq, seq, pl.BlockSpec((kk, cbk), lambda bi, n: (0, n)), pl.BlockSpec((1, cbk), lambda bi, n: (0, n))],
        out_specs=seq,
        out_shape=jax.ShapeDtypeStruct((b, s, f), BF16),
        compiler_params=_params(("parallel", "parallel")),
    )(gate_pre, up, cw, cb)


def ffn_bwd(dact, gate_pre, up, cw, cb, name, cbk=256):
    b, s, f = gate_pre.shape
    kk = cw.shape[0]
    cbk = _tile(f, cbk)

    def body(da_ref, g_ref, u_ref, cw_ref, cb_ref, dg_ref, du_ref, dcw_ref, dcb_ref):
        @pl.when(pl.program_id(1) == 0)
        def _():
            dcw_ref[...] = jnp.zeros(dcw_ref.shape, F32)
            dcb_ref[...] = jnp.zeros(dcb_ref.shape, F32)

        row = lax.broadcasted_iota(jnp.int32, (s, cbk), 0)
        gp = g_ref[...]
        cwv = cw_ref[...]
        gate = _conv_fwd(gp, cwv, cb_ref[...], row)
        gel, dgel = _gelu_parts(gate)
        da = da_ref[...]
        du_ref[...] = (da * gel).astype(BF16)
        dgate = da * u_ref[...] * dgel
        dcb_ref[...] += jnp.sum(dgate, axis=0, keepdims=True)
        dgp, dcw = _conv_bwd(dgate, gp, cwv, row)
        dcw_ref[...] += dcw
        dg_ref[...] = dgp.astype(BF16)

    seq = pl.BlockSpec((None, s, cbk), lambda n, bi: (bi, 0, n))
    cws = pl.BlockSpec((kk, cbk), lambda n, bi: (0, n))
    vec = pl.BlockSpec((1, cbk), lambda n, bi: (0, n))
    sd = jax.ShapeDtypeStruct
    return pl.pallas_call(
        body, name=name, grid=(f // cbk, b),
        in_specs=[seq, seq, seq, cws, vec],
        out_specs=[seq, seq, cws, vec],
        out_shape=[sd((b, s, f), BF16), sd((b, s, f), BF16), sd((kk, f), F32), sd((1, f), F32)],
        compiler_params=_params(("parallel", "arbitrary")),
    )(dact, gate_pre, up, cw, cb)


def _t5_bucket(dist):
    max_exact = REL_BUCKETS // 2
    d = np.maximum(dist, 1).astype(np.float32)
    large = max_exact + np.log(d / max_exact) / math.log(REL_MAX_DIST / max_exact) * (REL_BUCKETS - max_exact)
    large = np.minimum(large.astype(np.int32), REL_BUCKETS - 1)
    return np.where(dist < max_exact, dist, large).astype(np.int32)


def _band(window, dilation):
    qi = np.arange(ATTN_BLOCK)[:, None]
    kj = np.arange(2 * ATTN_BLOCK)[None, :]
    delta = ATTN_BLOCK + qi - kj
    mask = (delta >= 0) & (delta <= window // dilation)
    bucket = _t5_bucket(np.maximum(delta, 0) * dilation)
    return mask, bucket


def _attn_blocks(s, r):
    m = s // r
    assert m % ATTN_BLOCK == 0, "sequence length must be a multiple of dilation * block"
    return m // ATTN_BLOCK


def attn_fwd(qkv, biasm, n_heads, name):
    b, s, _ = qkv.shape
    h = n_heads
    scale = HEAD_DIM ** -0.5
    blk = ATTN_BLOCK

    def body(q1_ref, q2_ref, q3_ref, k_ref, v_ref, bias_ref, o_ref, lse_ref, acc, m_s, l_s):
        m_s[...] = jnp.full(m_s.shape, NEG, F32)
        l_s[...] = jnp.zeros(l_s.shape, F32)
        acc[...] = jnp.zeros(acc.shape, F32)
        for g, q_ref in enumerate((q1_ref, q2_ref, q3_ref)):
            r = DILATED[g][1]
            nb = _attn_blocks(s, r)
            bias_prev = bias_ref[g, :, :blk]
            bias_cur = bias_ref[g, :, blk:]

            def block(idx, carry, q_ref=q_ref, r=r, nb=nb, bias_prev=bias_prev, bias_cur=bias_cur):
                c = idx // nb
                n = idx % nb
                qs = c + r * blk * n
                ks = jnp.where(n > 0, qs - r * blk, qs)
                cur = pl.ds(qs, blk, stride=r)
                prev = pl.ds(ks, blk, stride=r)
                q = q_ref[cur, :].astype(BF16)
                s_cur = _dot_nt(q, k_ref[cur, :].astype(BF16)) * scale + bias_cur
                s_prev = _dot_nt(q, k_ref[prev, :].astype(BF16)) * scale + bias_prev + jnp.where(n > 0, 0.0, NEG)
                m_old = m_s[cur, :]
                m_new = jnp.maximum(m_old, jnp.maximum(jnp.max(s_cur, axis=-1, keepdims=True),
                                                       jnp.max(s_prev, axis=-1, keepdims=True)))
                alpha = jnp.exp(m_old - m_new)
                p_cur = jnp.exp(s_cur - m_new)
                p_prev = jnp.exp(s_prev - m_new)
                l_s[cur, :] = alpha * l_s[cur, :] + jnp.sum(p_cur, axis=-1, keepdims=True) + jnp.sum(p_prev, axis=-1, keepdims=True)
                pv = _dot(p_cur.astype(BF16), v_ref[cur, :].astype(BF16)) + _dot(p_prev.astype(BF16), v_ref[prev, :].astype(BF16))
                acc[cur, :] = alpha * acc[cur, :] + pv
                m_s[cur, :] = m_new
                return carry

            lax.fori_loop(0, r * nb, block, 0)
        l = l_s[...]
        o_ref[...] = acc[...] / l
        lse_ref[...] = m_s[...] + jnp.log(l)

    def col(j):
        return pl.BlockSpec((None, s, HEAD_DIM), lambda bi, hi, j=j: (bi, 0, j * h + hi))

    return pl.pallas_call(
        body, name=name, grid=(b, h),
        in_specs=[col(0), col(1), col(2), col(3), col(4),
                  pl.BlockSpec((N_GROUPS, None, blk, 2 * blk), lambda bi, hi: (0, hi, 0, 0))],
        out_specs=[pl.BlockSpec((None, s, HEAD_DIM), lambda bi, hi: (bi, 0, hi)),
                   pl.BlockSpec((None, None, s, 1), lambda bi, hi: (bi, hi, 0, 0))],
        out_shape=[jax.ShapeDtypeStruct((b, s, h * HEAD_DIM), F32), jax.ShapeDtypeStruct((b, h, s, 1), F32)],
        scratch_shapes=[pltpu.VMEM((s, HEAD_DIM), F32), pltpu.VMEM((s, 1), F32), pltpu.VMEM((s, 1), F32)],
        compiler_params=_params(("parallel", "parallel")),
    )(qkv, qkv, qkv, qkv, qkv, biasm)


def attn_bwd(qkv, biasm, o, lse, do, n_heads, name):
    b, s, _ = qkv.shape
    h = n_heads
    scale = HEAD_DIM ** -0.5
    blk = ATTN_BLOCK

    def body(q1_ref, q2_ref, q3_ref, k_ref, v_ref, bias_ref, o_ref, lse_ref, do_ref,
             dq1_ref, dq2_ref, dq3_ref, dk_ref, dv_ref, ds_ref, dq1_acc, dq2_acc, dq3_acc, dk_acc, dv_acc, delta):
        delta[...] = jnp.sum(do_ref[...] * o_ref[...], axis=-1, keepdims=True)
        dk_acc[...] = jnp.zeros(dk_acc.shape, F32)
        dv_acc[...] = jnp.zeros(dv_acc.shape, F32)
        for g, (q_ref, dq_ref) in enumerate(((q1_ref, dq1_acc), (q2_ref, dq2_acc), (q3_ref, dq3_acc))):
            r = DILATED[g][1]
            nb = _attn_blocks(s, r)
            bias_prev = bias_ref[g, :, :blk]
            bias_cur = bias_ref[g, :, blk:]

            def block(idx, carry, q_ref=q_ref, dq_ref=dq_ref, r=r, nb=nb, bias_prev=bias_prev, bias_cur=bias_cur):
                ds_prev_sum, ds_cur_sum = carry
                c = idx // nb
                n = idx % nb
                qs = c + r * blk * n
                ks = jnp.where(n > 0, qs - r * blk, qs)
                cur = pl.ds(qs, blk, stride=r)
                prev = pl.ds(ks, blk, stride=r)
                q = q_ref[cur, :].astype(BF16)
                k_cur = k_ref[cur, :].astype(BF16)
                k_prev = k_ref[prev, :].astype(BF16)
                v_cur = v_ref[cur, :].astype(BF16)
                v_prev = v_ref[prev, :].astype(BF16)
                dob = do_ref[cur, :].astype(BF16)
                lse_b = lse_ref[cur, :]
                dl_b = delta[cur, :]
                p_cur = jnp.exp(_dot_nt(q, k_cur) * scale + bias_cur - lse_b)
                p_prev = jnp.exp(_dot_nt(q, k_prev) * scale + bias_prev + jnp.where(n > 0, 0.0, NEG) - lse_b)
                ds_cur = p_cur * (_dot_nt(dob, v_cur) - dl_b)
                ds_prev = p_prev * (_dot_nt(dob, v_prev) - dl_b)
                ds_cur_b, ds_prev_b = ds_cur.astype(BF16), ds_prev.astype(BF16)
                dq_ref[cur, :] = (_dot(ds_cur_b, k_cur) + _dot(ds_prev_b, k_prev)) * scale
                dk_acc[prev, :] += _dot_tn(ds_prev_b, q) * scale
                dk_acc[cur, :] += _dot_tn(ds_cur_b, q) * scale
                dv_acc[prev, :] += _dot_tn(p_prev.astype(BF16), dob)
                dv_acc[cur, :] += _dot_tn(p_cur.astype(BF16), dob)
                return ds_prev_sum + ds_prev, ds_cur_sum + ds_cur

            zero = jnp.zeros((blk, blk), F32)
            ds_prev_sum, ds_cur_sum = lax.fori_loop(0, r * nb, block, (zero, zero))
            ds_ref[g, :, :blk] = ds_prev_sum
            ds_ref[g, :, blk:] = ds_cur_sum
        for out_ref, acc_ref in ((dq1_ref, dq1_acc), (dq2_ref, dq2_acc), (dq3_ref, dq3_acc), (dk_ref, dk_acc), (dv_ref, dv_acc)):
            out_ref[...] = acc_ref[...].astype(out_ref.dtype)

    def col(j):
        return pl.BlockSpec((None, s, HEAD_DIM), lambda bi, hi, j=j: (bi, 0, j * h + hi))

    head = pl.BlockSpec((None, s, HEAD_DIM), lambda bi, hi: (bi, 0, hi))
    sd = jax.ShapeDtypeStruct
    return pl.pallas_call(
        body, name=name, grid=(b, h),
        in_specs=[col(0), col(1), col(2), col(3), col(4),
                  pl.BlockSpec((N_GROUPS, None, blk, 2 * blk), lambda bi, hi: (0, hi, 0, 0)),
                  head, pl.BlockSpec((None, None, s, 1), lambda bi, hi: (bi, hi, 0, 0)), head],
        out_specs=[head] * 5 + [pl.BlockSpec((None, None, N_GROUPS, blk, 2 * blk), lambda bi, hi: (bi, hi, 0, 0, 0))],
        out_shape=[sd((b, s, h * HEAD_DIM), BF16)] * 5 + [sd((b, h, N_GROUPS, blk, 2 * blk), F32)],
        scratch_shapes=[pltpu.VMEM((s, HEAD_DIM), F32)] * 5 + [pltpu.VMEM((s, 1), F32)],
        compiler_params=_params(("parallel", "parallel")),
    )(qkv, qkv, qkv, qkv, qkv, biasm, o, lse, do)


def bias_grad(ds_sum, bucket_f, name):
    b, h, g, blk, blk2 = ds_sum.shape

    def body(ds_ref, bk_ref, o_ref):
        tot = jnp.sum(ds_ref[...], axis=0)
        bk = bk_ref[...]
        lane = lax.broadcasted_iota(jnp.int32, (1, 128), 1)
        vec = jnp.zeros((1, 128), F32)
        for bucket in range(REL_BUCKETS):
            val = jnp.sum(jnp.where(bk == float(bucket), tot, 0.0), keepdims=True)
            vec = vec + jnp.where(lane == bucket, val, 0.0)
        o_ref[...] = vec

    return pl.pallas_call(
        body, name=name, grid=(g, h),
        in_specs=[pl.BlockSpec((b, None, None, blk, blk2), lambda gi, hi: (0, hi, gi, 0, 0)),
                  pl.BlockSpec((None, blk, blk2), lambda gi, hi: (gi, 0, 0))],
        out_specs=pl.BlockSpec((None, 1, 128), lambda gi, hi: (gi * h + hi, 0, 0)),
        out_shape=jax.ShapeDtypeStruct((g * h, 1, 128), F32),
        compiler_params=_params(("parallel", "parallel")),
    )(ds_sum, bucket_f)


def _chip_peers():
    x, y, c = lax.axis_index("x"), lax.axis_index("y"), lax.axis_index("c")
    me = 2 * x + y
    peers = [(1 - x, y, c), (x, 1 - y, c), (1 - x, 1 - y, c)]
    peer_chip = [2 * (1 - x) + y, 2 * x + (1 - y), 2 * (1 - x) + (1 - y)]
    return me, peers, peer_chip


def gather_chips(shard, name):
    r, w = shard.shape

    def body(src_ref, out_ref, send_sems, recv_sems, local_sem):
        me, peers, peer_chip = _chip_peers()
        mine = pltpu.make_async_copy(src_ref, out_ref.at[me], local_sem)
        mine.start()
        sends = [pltpu.make_async_remote_copy(src_ref=src_ref, dst_ref=out_ref.at[me], send_sem=send_sems.at[k],
                                              recv_sem=recv_sems.at[k], device_id=peers[k], device_id_type=MESH)
                 for k in range(3)]
        for cp in sends:
            cp.start()
        for k in range(3):
            pltpu.make_async_remote_copy(src_ref=src_ref, dst_ref=out_ref.at[peer_chip[k]], send_sem=send_sems.at[k],
                                         recv_sem=recv_sems.at[k], device_id=peers[k], device_id_type=MESH).wait_recv()
        for cp in sends:
            cp.wait_send()
        mine.wait()

    return pl.pallas_call(
        body, name=name,
        in_specs=[pl.BlockSpec(memory_space=pl.ANY)],
        out_specs=pl.BlockSpec(memory_space=pl.ANY),
        out_shape=jax.ShapeDtypeStruct((N_CHIPS, r, w), shard.dtype),
        scratch_shapes=[pltpu.SemaphoreType.DMA((3,)), pltpu.SemaphoreType.DMA((3,)), pltpu.SemaphoreType.DMA],
    )(shard)


def scatter_chips(slabs, name):
    _, r, w = slabs.shape

    def body(src_ref, out_ref, send_sems, recv_sems, local_sem):
        me, peers, peer_chip = _chip_peers()
        mine = pltpu.make_async_copy(src_ref.at[me], out_ref.at[me], local_sem)
        mine.start()
        sends = [pltpu.make_async_remote_copy(src_ref=src_ref.at[peer_chip[k]], dst_ref=out_ref.at[me],
                                              send_sem=send_sems.at[k], recv_sem=recv_sems.at[k],
                                              device_id=peers[k], device_id_type=MESH)
                 for k in range(3)]
        for cp in sends:
            cp.start()
        for k in range(3):
            pltpu.make_async_remote_copy(src_ref=src_ref.at[me], dst_ref=out_ref.at[peer_chip[k]],
                                         send_sem=send_sems.at[k], recv_sem=recv_sems.at[k],
                                         device_id=peers[k], device_id_type=MESH).wait_recv()
        for cp in sends:
            cp.wait_send()
        mine.wait()

    return pl.pallas_call(
        body, name=name,
        in_specs=[pl.BlockSpec(memory_space=pl.ANY)],
        out_specs=pl.BlockSpec(memory_space=pl.ANY),
        out_shape=jax.ShapeDtypeStruct(slabs.shape, slabs.dtype),
        scratch_shapes=[pltpu.SemaphoreType.DMA((3,)), pltpu.SemaphoreType.DMA((3,)), pltpu.SemaphoreType.DMA],
    )(slabs)


def swap_cores(buf, name):
    def body(src_ref, out_ref, send_sem, recv_sem):
        x, y, c = lax.axis_index("x"), lax.axis_index("y"), lax.axis_index("c")
        cp = pltpu.make_async_remote_copy(src_ref=src_ref, dst_ref=out_ref, send_sem=send_sem, recv_sem=recv_sem,
                                          device_id=(x, y, 1 - c), device_id_type=MESH)
        cp.start()
        cp.wait()

    return pl.pallas_call(
        body, name=name,
        in_specs=[pl.BlockSpec(memory_space=pl.ANY)],
        out_specs=pl.BlockSpec(memory_space=pl.ANY),
        out_shape=jax.ShapeDtypeStruct(buf.shape, buf.dtype),
        scratch_shapes=[pltpu.SemaphoreType.DMA, pltpu.SemaphoreType.DMA],
    )(buf)


def sum_slots(slots, name, tr=512):
    n, r, w = slots.shape
    tr = _tile(r, tr)

    def body(s_ref, o_ref):
        acc = s_ref[0].astype(F32)
        for j in range(1, n):
            acc = acc + s_ref[j].astype(F32)
        o_ref[...] = acc

    return pl.pallas_call(
        body, name=name, grid=(r // tr,),
        in_specs=[pl.BlockSpec((n, tr, w), lambda i: (0, i, 0))],
        out_specs=_rows(tr, w),
        out_shape=jax.ShapeDtypeStruct((r, w), F32),
        compiler_params=_params(("parallel",)),
    )(slots)


def adamw(w, m, v, ga, gb, name, tr=256):
    r, c = w.shape
    tr = r if r % 8 else _tile(r, tr)
    c1 = 1.0 - ADAM_B1 ** ADAM_STEP
    c2 = 1.0 - ADAM_B2 ** ADAM_STEP

    def body(w_ref, m_ref, v_ref, ga_ref, gb_ref, g_ref, d_ref, nm_ref, nv_ref):
        g = ga_ref[...] + gb_ref[...]
        nm = ADAM_B1 * m_ref[...] + (1.0 - ADAM_B1) * g
        nv = ADAM_B2 * v_ref[...] + (1.0 - ADAM_B2) * (g * g)
        g_ref[...] = g
        nm_ref[...] = nm
        nv_ref[...] = nv
        d_ref[...] = (-ADAM_LR) * ((nm / c1) / (jnp.sqrt(nv / c2) + ADAM_EPS) + ADAM_WD * w_ref[...])

    spec = _rows(tr, c)
    return pl.pallas_call(
        body, name=name, grid=(r // tr,),
        in_specs=[spec] * 5, out_specs=[spec] * 4,
        out_shape=[jax.ShapeDtypeStruct((r, c), F32)] * 4,
        compiler_params=_params(("parallel",)),
    )(w, m, v, ga, gb)


_PARAMS = (
    ("rel_bias", None), ("norm_mix_pre", None), ("norm_mix_post", None), ("w_in", 1), ("conv_rnn_w", 1),
    ("conv_rnn_b", None), ("w_rg_a", None), ("b_rg_a", None), ("w_rg_x", None), ("b_rg_x", None),
    ("lru_lambda", None), ("w_branch_rnn", 0), ("w_branch_att", 1), ("w_out", 0), ("norm_ffn_pre", None),
    ("norm_ffn_post", None), ("w_ffn_gate", 1), ("w_ffn_up", 1), ("conv_ffn_w", 1), ("conv_ffn_b", None),
    ("w_ffn_down", 0),
)
_F32_ON_WIRE = ("conv_rnn_w", "conv_ffn_w")


def _as2d(a):
    a = a[0] if a.shape[0] == 1 and a.ndim >= 3 else a
    return a.reshape(-1, a.shape[-1]) if a.ndim == 3 else a


def _pack(pieces, dtype):
    flat = jnp.concatenate([p.astype(dtype).reshape(-1) for p in pieces])
    unit = PACK_W * PACK_ROWS
    pad = (-flat.shape[0]) % unit
    flat = jnp.pad(flat, (0, pad))
    return flat.reshape(-1, PACK_W)


def _unpack(buf, shapes):
    flat = buf.reshape(-1)
    out, off = [], 0
    for shp in shapes:
        n = int(np.prod(shp))
        out.append(flat[off:off + n].reshape(shp))
        off += n
    return out


def kernel(x, rel_bias, norm_mix_pre, norm_mix_post, w_in, conv_rnn_w, conv_rnn_b, w_rg_a, b_rg_a, w_rg_x, b_rg_x, lru_lambda, w_branch_rnn, w_branch_att, w_out, norm_ffn_pre, norm_ffn_post, w_ffn_gate, w_ffn_up, conv_ffn_w, conv_ffn_b, w_ffn_down, loss_target, m_rel_bias, m_norm_mix_pre, m_norm_mix_post, m_w_in, m_conv_rnn_w, m_conv_rnn_b, m_w_rg_a, m_b_rg_a, m_w_rg_x, m_b_rg_x, m_lru_lambda, m_w_branch_rnn, m_w_branch_att, m_w_out, m_norm_ffn_pre, m_norm_ffn_post, m_w_ffn_gate, m_w_ffn_up, m_conv_ffn_w, m_conv_ffn_b, m_w_ffn_down, v_rel_bias, v_norm_mix_pre, v_norm_mix_post, v_w_in, v_conv_rnn_w, v_conv_rnn_b, v_w_rg_a, v_b_rg_a, v_w_rg_x, v_b_rg_x, v_lru_lambda, v_w_branch_rnn, v_w_branch_att, v_w_out, v_norm_ffn_pre, v_norm_ffn_post, v_w_ffn_gate, v_w_ffn_up, v_conv_ffn_w, v_conv_ffn_b, v_w_ffn_down):
    args = dict(locals())
    names = [n for n, _ in _PARAMS]
    axis = dict(_PARAMS)
    w_loc = {n: args[n] for n in names}
    m_loc = {n: args["m_" + n] for n in names}
    v_loc = {n: args["v_" + n] for n in names}
    sharded = [n for n in names if axis[n] is not None]
    replicated = [n for n in names if axis[n] is None]

    pieces = []
    for n in sharded:
        a = _as2d(w_loc[n])
        if n in _F32_ON_WIRE:
            a = lax.bitcast_convert_type(a, BF16)
        pieces.append(a)
    shard_shapes = [p.shape for p in pieces]
    gathered = gather_chips(_pack(pieces, BF16), "gather_weights")
    per_chip = [_unpack(gathered[j], shard_shapes) for j in range(N_CHIPS)]
    full = {}
    for i, n in enumerate(sharded):
        parts = [per_chip[j][i] for j in range(N_CHIPS)]
        if n in _F32_ON_WIRE:
            parts = [lax.bitcast_convert_type(p, F32) for p in parts]
        full[n] = jnp.concatenate(parts, axis=axis[n])
    for n in replicated:
        full[n] = _as2d(w_loc[n])

    grads, loss_part = _local_step(x, loss_target, full)

    slabs = []
    for j in range(N_CHIPS):
        ps = []
        for n in sharded:
            g = grads[n]
            size = g.shape[axis[n]] // N_CHIPS
            ps.append(lax.slice_in_dim(g, j * size, (j + 1) * size, axis=axis[n]))
        ps += [grads[n] for n in replicated]
        slabs.append(_pack(ps, BF16))
    grad_shapes = [_as2d(w_loc[n]).shape for n in sharded + replicated]
    received = scatter_chips(jnp.stack(slabs), "scatter_grads")
    part_mine = sum_slots(received, "sum_grads")
    part_other = swap_cores(part_mine, "swap_grads")
    g_mine = dict(zip(sharded + replicated, _unpack(part_mine, grad_shapes)))
    g_other = dict(zip(sharded + replicated, _unpack(part_other, grad_shapes)))

    out_g, out_d, out_m, out_v = {}, {}, {}, {}
    for n in names:
        shp = w_loc[n].shape
        g, d, nm, nv = adamw(_as2d(w_loc[n]), _as2d(m_loc[n]), _as2d(v_loc[n]), g_mine[n], g_other[n], "adamw_" + n)
        out_g[n], out_d[n], out_m[n], out_v[n] = (t.reshape(shp) for t in (g, d, nm, nv))

    d_model = x.shape[-1]
    loss = lax.psum(0.5 * jnp.sum(loss_part) / d_model, ("x", "y", "c"))
    grad_x = grads["x"]
    return (loss, grad_x, *[out_g[n] for n in names], *[out_d[n] for n in names],
            *[out_m[n] for n in names], *[out_v[n] for n in names])


def _local_step(x, target, p):
    b, s, d = x.shape
    t = b * s
    rnn = p["w_branch_rnn"].shape[0]
    hkv = p["w_branch_att"].shape[0]
    h = hkv // HEAD_DIM
    nq = N_GROUPS * hkv
    ffn = p["w_ffn_down"].shape[0]
    nbk = rnn // p["w_rg_a"].shape[1]

    x2 = x.reshape(t, d)
    tgt = target.reshape(t, d)
    w_in = p["w_in"]
    w_xr, w_qkv, w_g = w_in[:, :rnn], w_in[:, rnn:rnn + nq + 2 * hkv], w_in[:, rnn + nq + 2 * hkv:]
    wa = p["w_rg_a"].reshape(nbk, -1, p["w_rg_a"].shape[1]).astype(BF16)
    wx = p["w_rg_x"].reshape(nbk, -1, p["w_rg_x"].shape[1]).astype(BF16)
    cw_r, cb_r = p["conv_rnn_w"], p["conv_rnn_b"]
    cw_f, cb_f = p["conv_ffn_w"], p["conv_ffn_b"]

    masks, buckets = zip(*[_band(w_, r_) for w_, r_ in DILATED])
    bias = jnp.stack([p["rel_bias"][jnp.asarray(buckets[g])][:, :, g * h:(g + 1) * h] for g in range(N_GROUPS)])
    biasm = jnp.where(jnp.asarray(np.stack(masks))[:, :, :, None], bias, NEG).transpose(0, 3, 1, 2)
    bucket_f = jnp.asarray(np.where(np.stack(masks), np.stack(buckets), -1).astype(np.float32))

    hn1, (xr, qkv, gts) = norm_mm(x2, p["norm_mix_pre"], [w_xr, w_qkv, w_g], [F32, F32, F32], "in_proj")
    xr3 = xr.reshape(b, s, rnn)
    y_rnn = rglru_fwd(xr3, cw_r, cb_r, wa, p["b_rg_a"], wx, p["b_rg_x"], p["lru_lambda"], "rglru_fwd")
    qkv3 = qkv.reshape(b, s, -1)
    o_att, lse = attn_fwd(qkv3, biasm, h, "attn_fwd")
    merged, br, ba = merge_fwd(y_rnn.reshape(t, rnn), o_att.reshape(t, hkv), gts, p["w_branch_rnn"],
                               p["w_branch_att"], "merge_fwd")
    mix, h1 = mm_norm_res(merged, p["w_out"], p["norm_mix_post"], x2, "out_proj")
    hn2, (gate_pre, up) = norm_mm(h1, p["norm_ffn_pre"], [p["w_ffn_gate"], p["w_ffn_up"]], [F32, F32], "ffn_in")
    act = ffn_act(gate_pre.reshape(b, s, ffn), up.reshape(b, s, ffn), cw_f, cb_f, "ffn_act")
    ff, y = mm_norm_res(act.reshape(t, ffn), p["w_ffn_down"], p["norm_ffn_post"], h1, "ffn_down")

    g = {}
    dy, dff, g["norm_ffn_post"], loss_part = loss_norm_bwd(y, tgt, ff, p["norm_ffn_post"], "loss_bwd")
    dact = mm_nt([dff], [p["w_ffn_down"]], F32, "ffn_down_dx")
    (g["w_ffn_down"],) = mm_tn(act.reshape(t, ffn), [dff], "ffn_down_dw")
    dgp, dup, g["conv_ffn_w"], g["conv_ffn_b"] = ffn_bwd(dact.reshape(b, s, ffn), gate_pre.reshape(b, s, ffn),
                                                        up.reshape(b, s, ffn), cw_f, cb_f, "ffn_bwd")
    dgp, dup = dgp.reshape(t, ffn), dup.reshape(t, ffn)
    dhn2 = mm_nt([dgp, dup], [p["w_ffn_gate"], p["w_ffn_up"]], F32, "ffn_in_dx")
    (g["w_ffn_gate"],) = mm_tn(hn2, [dgp], "ffn_gate_dw")
    (g["w_ffn_up"],) = mm_tn(hn2, [dup], "ffn_up_dw")
    dh1, g["norm_ffn_pre"] = norm_bwd(dhn2, h1, p["norm_ffn_pre"], dy, F32, "ffn_norm_bwd")
    dmix, g["norm_mix_post"] = norm_bwd(dh1, mix, p["norm_mix_post"], None, BF16, "mix_norm_bwd")
    dmerged = mm_nt([dmix], [p["w_out"]], F32, "out_proj_dx")
    (g["w_out"],) = mm_tn(merged, [dmix], "out_proj_dw")
    dbr, dba, dgts = merge_bwd(dmerged, gts, br, ba, "merge_bwd")
    dy_rnn = mm_nt([dbr], [p["w_branch_rnn"]], F32, "branch_rnn_dx")
    do_att = mm_nt([dba], [p["w_branch_att"]], F32, "branch_att_dx")
    (g["w_branch_rnn"],) = mm_tn(y_rnn.reshape(t, rnn), [dbr], "branch_rnn_dw")
    (g["w_branch_att"],) = mm_tn(o_att.reshape(t, hkv), [dba], "branch_att_dw")
    (dxr, g["conv_rnn_w"], g["conv_rnn_b"], dwa, g["b_rg_a"], dwx, g["b_rg_x"], g["lru_lambda"]) = rglru_bwd(
        xr3, y_rnn, dy_rnn.reshape(b, s, rnn), cw_r, cb_r, wa, p["b_rg_a"], wx, p["b_rg_x"], p["lru_lambda"], "rglru_bwd")
    g["w_rg_a"] = dwa.reshape(p["w_rg_a"].shape)
    g["w_rg_x"] = dwx.reshape(p["w_rg_x"].shape)
    dq1, dq2, dq3, dk, dv, ds_sum = attn_bwd(qkv3, biasm, o_att, lse, do_att.reshape(b, s, hkv), h, "attn_bwd")
    rows = bias_grad(ds_sum, bucket_f, "bias_grad")
    g["rel_bias"] = rows[:, 0, :REL_BUCKETS].T
    dxr = dxr.reshape(t, rnn)
    dqkv = [a.reshape(t, hkv) for a in (dq1, dq2, dq3, dk, dv)]
    w_qkv_parts = [w_qkv[:, i * hkv:(i + 1) * hkv] for i in range(5)]
    dhn1 = mm_nt([dxr, dgts] + dqkv, [w_xr, w_g] + w_qkv_parts, F32, "in_proj_dx")
    dw_xr, dw_g = mm_tn(hn1, [dxr, dgts], "in_proj_dw_a")
    dw_qkv = mm_tn(hn1, dqkv, "in_proj_dw_b")
    g["w_in"] = jnp.concatenate([dw_xr] + list(dw_qkv) + [dw_g], axis=1)
    dx, g["norm_mix_pre"] = norm_bwd(dhn1, x2, p["norm_mix_pre"], dh1, F32, "in_norm_bwd")
    g["x"] = dx.reshape(b, s, d)
    return g, loss_part
```

```python
import functools
import math

import numpy as np
import jax
import jax.numpy as jnp
from jax import lax
from jax.experimental import pallas as pl
from jax.experimental.pallas import tpu as pltpu

F32 = jnp.float32
BF16 = jnp.bfloat16

EPS = 1e-6
HEAD_DIM = 128
ATTN_BLOCK = 128
DILATED = ((128, 1), (512, 4), (2048, 16))
N_GROUPS = len(DILATED)
REL_BUCKETS = 32
REL_MAX_DIST = 2048
LRU_C = 8.0
NEG = -1e30

ADAM_LR = 0.001
ADAM_B1 = 0.9
ADAM_B2 = 0.999
ADAM_EPS = 1e-08
ADAM_WD = 0.01
ADAM_STEP = 10

N_CHIPS = 4
PACK_W = 1024
PACK_ROWS = 16
VMEM_LIMIT = 56 * 1024 * 1024
MESH = pl.DeviceIdType.MESH


def _params(sem=None):
    return pltpu.CompilerParams(dimension_semantics=sem, vmem_limit_bytes=VMEM_LIMIT)


def _dot(a, b):
    return jnp.dot(a, b, preferred_element_type=F32)


def _dot_nt(a, b):
    return lax.dot_general(a, b, (((1,), (1,)), ((), ())), preferred_element_type=F32)


def _dot_tn(a, b):
    return lax.dot_general(a, b, (((0,), (0,)), ((), ())), preferred_element_type=F32)


def _sig(x):
    return 1.0 / (1.0 + jnp.exp(-x))


def _rows(tm, w):
    return pl.BlockSpec((tm, w), lambda i: (i, 0))


def _whole(shape):
    nd = len(shape)
    return pl.BlockSpec(tuple(shape), lambda *_: (0,) * nd)


def _tile(t, want):
    while t % want:
        want //= 2
    return want


def norm_mm(x, g, ws, splits, name, tm=256):
    t, d = x.shape
    tm = _tile(t, tm)
    nw = len(ws)
    widths = [n for sp in splits for n in sp]

    def body(x_ref, g_ref, *refs):
        w_refs, hn_ref, o_refs = refs[:nw], refs[nw], refs[nw + 1:]
        xv = x_ref[...]
        inv = lax.rsqrt(jnp.mean(xv * xv, axis=-1, keepdims=True) + EPS)
        hn = (xv * inv * g_ref[...]).astype(BF16)
        hn_ref[...] = hn
        o = 0
        for w_ref, sp in zip(w_refs, splits):
            off = 0
            for n in sp:
                o_refs[o][...] = _dot(hn, w_ref[:, off:off + n])
                off += n
                o += 1

    outs = pl.pallas_call(
        body, name=name, grid=(t // tm,),
        in_specs=[_rows(tm, d), _whole(g.shape)] + [_whole(w.shape) for w in ws],
        out_specs=[_rows(tm, d)] + [_rows(tm, n) for n in widths],
        out_shape=[jax.ShapeDtypeStruct((t, d), BF16)] + [jax.ShapeDtypeStruct((t, n), F32) for n in widths],
        compiler_params=_params(("parallel",)),
    )(x, g, *ws)
    return outs[0], outs[1:]


def mm_nt(groups, out_dtype, name, tm=256):
    dys_all = [dy for dys, _ in groups for dy in dys]
    ws = [w for _, w in groups]
    t = dys_all[0].shape[0]
    k = ws[0].shape[0]
    tm = _tile(t, tm)
    n = len(dys_all)

    def body(*refs):
        dy_refs, w_refs, o_ref = refs[:n], refs[n:n + len(ws)], refs[n + len(ws)]
        acc = None
        i = 0
        for (dys, _), w_ref in zip(groups, w_refs):
            off = 0
            for dy in dys:
                width = dy.shape[1]
                part = _dot_nt(dy_refs[i][...].astype(BF16), w_ref[:, off:off + width])
                acc = part if acc is None else acc + part
                off += width
                i += 1
        o_ref[...] = acc.astype(o_ref.dtype)

    return pl.pallas_call(
        body, name=name, grid=(t // tm,),
        in_specs=[_rows(tm, dy.shape[1]) for dy in dys_all] + [_whole(w.shape) for w in ws],
        out_specs=_rows(tm, k),
        out_shape=jax.ShapeDtypeStruct((t, k), out_dtype),
        compiler_params=_params(("parallel",)),
    )(*dys_all, *ws)


def mm_tn(a, dys, name, col_shards=1, tm=512):
    t, k = a.shape
    tm = _tile(t, tm)
    n = len(dys)
    ntot = sum(dy.shape[1] for dy in dys)
    wsh = ntot // col_shards

    def body(a_ref, *refs):
        dy_refs, o_ref, acc = refs[:n], refs[n], refs[n + 1]

        @pl.when(pl.program_id(0) == 0)
        def _():
            acc[...] = jnp.zeros(acc.shape, F32)

        av = a_ref[...].astype(BF16)
        off = 0
        for dy_ref in dy_refs:
            width = dy_ref.shape[1]
            acc[:, off:off + width] += _dot_tn(av, dy_ref[...].astype(BF16))
            off += width

        @pl.when(pl.program_id(0) == pl.num_programs(0) - 1)
        def _():
            for j in range(col_shards):
                o_ref[j] = acc[:, j * wsh:(j + 1) * wsh].astype(o_ref.dtype)

    return pl.pallas_call(
        body, name=name, grid=(t // tm,),
        in_specs=[_rows(tm, k)] + [_rows(tm, dy.shape[1]) for dy in dys],
        out_specs=_whole((col_shards, k, wsh)),
        out_shape=jax.ShapeDtypeStruct((col_shards, k, wsh), BF16),
        scratch_shapes=[pltpu.VMEM((k, ntot), F32)],
        compiler_params=_params(("arbitrary",)),
    )(a, *dys)


def mm_norm_res(a, w, g, resid, name, tm=256):
    t, k = a.shape
    d = w.shape[1]
    tm = _tile(t, tm)

    def body(a_ref, w_ref, g_ref, r_ref, p_ref, o_ref):
        prod = _dot(a_ref[...], w_ref[...])
        p_ref[...] = prod
        inv = lax.rsqrt(jnp.mean(prod * prod, axis=-1, keepdims=True) + EPS)
        o_ref[...] = r_ref[...] + prod * inv * g_ref[...]

    return pl.pallas_call(
        body, name=name, grid=(t // tm,),
        in_specs=[_rows(tm, k), _whole(w.shape), _whole(g.shape), _rows(tm, d)],
        out_specs=[_rows(tm, d), _rows(tm, d)],
        out_shape=[jax.ShapeDtypeStruct((t, d), F32)] * 2,
        compiler_params=_params(("parallel",)),
    )(a, w, g, resid)


def _rms_bwd(dz, u, g):
    d = u.shape[-1]
    inv = lax.rsqrt(jnp.mean(u * u, axis=-1, keepdims=True) + EPS)
    dzg = dz * g
    proj = jnp.sum(dzg * u, axis=-1, keepdims=True) * (1.0 / d)
    du = inv * (dzg - u * (inv * inv) * proj)
    dg_rows = dz * u * inv
    return du, dg_rows


def norm_bwd(dz, u, g, add, out_dtype, name, tm=256):
    t, d = u.shape
    tm = _tile(t, tm)
    has_add = add is not None

    def body(*refs):
        if has_add:
            dz_ref, u_ref, g_ref, add_ref, du_ref, dg_ref = refs
        else:
            dz_ref, u_ref, g_ref, du_ref, dg_ref = refs

        @pl.when(pl.program_id(0) == 0)
        def _():
            dg_ref[...] = jnp.zeros(dg_ref.shape, F32)

        du, dg_rows = _rms_bwd(dz_ref[...].astype(F32), u_ref[...], g_ref[...])
        if has_add:
            du = du + add_ref[...]
        du_ref[...] = du.astype(du_ref.dtype)
        dg_ref[...] += jnp.sum(dg_rows, axis=0, keepdims=True)

    ins = [dz, u, g] + ([add] if has_add else [])
    return pl.pallas_call(
        body, name=name, grid=(t // tm,),
        in_specs=[_rows(tm, d), _rows(tm, d), _whole(g.shape)] + ([_rows(tm, d)] if has_add else []),
        out_specs=[_rows(tm, d), _whole((1, d))],
        out_shape=[jax.ShapeDtypeStruct((t, d), out_dtype), jax.ShapeDtypeStruct((1, d), F32)],
        compiler_params=_params(("arbitrary",)),
    )(*ins)


def loss_norm_bwd(y, target, ff, g, name, tm=256):
    t, d = y.shape
    tm = _tile(t, tm)

    def body(y_ref, t_ref, ff_ref, g_ref, dy_ref, dff_ref, dg_ref, loss_ref):
        @pl.when(pl.program_id(0) == 0)
        def _():
            dg_ref[...] = jnp.zeros(dg_ref.shape, F32)
            loss_ref[...] = jnp.zeros(loss_ref.shape, F32)

        err = y_ref[...] - t_ref[...]
        loss_ref[...] += jnp.sum(err * err, axis=0, keepdims=True)
        dy = err * (1.0 / d)
        dy_ref[...] = dy
        du, dg_rows = _rms_bwd(dy, ff_ref[...], g_ref[...])
        dff_ref[...] = du.astype(dff_ref.dtype)
        dg_ref[...] += jnp.sum(dg_rows, axis=0, keepdims=True)

    return pl.pallas_call(
        body, name=name, grid=(t // tm,),
        in_specs=[_rows(tm, d), _rows(tm, d), _rows(tm, d), _whole(g.shape)],
        out_specs=[_rows(tm, d), _rows(tm, d), _whole((1, d)), _whole((1, d))],
        out_shape=[jax.ShapeDtypeStruct((t, d), F32), jax.ShapeDtypeStruct((t, d), BF16),
                   jax.ShapeDtypeStruct((1, d), F32), jax.ShapeDtypeStruct((1, d), F32)],
        compiler_params=_params(("arbitrary",)),
    )(y, target, ff, g)


def merge_fwd(y_rnn, o_att, gts, w_br, w_ba, name, tm=256):
    t = y_rnn.shape[0]
    d = w_br.shape[1]
    tm = _tile(t, tm)

    def body(y_ref, o_ref, g_ref, wbr_ref, wba_ref, m_ref, br_ref, ba_ref):
        br = _dot(y_ref[...].astype(BF16), wbr_ref[...])
        ba = _dot(o_ref[...].astype(BF16), wba_ref[...])
        gv = g_ref[...]
        m_ref[...] = (_sig(gv[:, :d]) * br + _sig(gv[:, d:]) * ba).astype(BF16)
        br_ref[...] = br
        ba_ref[...] = ba

    return pl.pallas_call(
        body, name=name, grid=(t // tm,),
        in_specs=[_rows(tm, y_rnn.shape[1]), _rows(tm, o_att.shape[1]), _rows(tm, 2 * d),
                  _whole(w_br.shape), _whole(w_ba.shape)],
        out_specs=[_rows(tm, d)] * 3,
        out_shape=[jax.ShapeDtypeStruct((t, d), BF16), jax.ShapeDtypeStruct((t, d), F32),
                   jax.ShapeDtypeStruct((t, d), F32)],
        compiler_params=_params(("parallel",)),
    )(y_rnn, o_att, gts, w_br, w_ba)


def merge_bwd(dmerged, gts, br, ba, name, tm=256):
    t, d = dmerged.shape
    tm = _tile(t, tm)

    def body(dm_ref, g_ref, br_ref, ba_ref, dbr_ref, dba_ref, dg_ref):
        dm = dm_ref[...]
        gv = g_ref[...]
        sr = _sig(gv[:, :d])
        sa = _sig(gv[:, d:])
        dbr_ref[...] = (dm * sr).astype(BF16)
        dba_ref[...] = (dm * sa).astype(BF16)
        dg_ref[:, :d] = (dm * br_ref[...] * sr * (1.0 - sr)).astype(BF16)
        dg_ref[:, d:] = (dm * ba_ref[...] * sa * (1.0 - sa)).astype(BF16)

    return pl.pallas_call(
        body, name=name, grid=(t // tm,),
        in_specs=[_rows(tm, d), _rows(tm, 2 * d), _rows(tm, d), _rows(tm, d)],
        out_specs=[_rows(tm, d), _rows(tm, d), _rows(tm, 2 * d)],
        out_shape=[jax.ShapeDtypeStruct((t, d), BF16), jax.ShapeDtypeStruct((t, d), BF16),
                   jax.ShapeDtypeStruct((t, 2 * d), BF16)],
        compiler_params=_params(("parallel",)),
    )(dmerged, gts, br, ba)


def _shift_dn(x, d, fill, row):
    return jnp.where(row >= d, pltpu.roll(x, d, 0), fill)


def _shift_up(x, d, fill, row):
    s = x.shape[0]
    return jnp.where(row < s - d, pltpu.roll(x, s - d, 0), fill)


def _conv_fwd(x, w, b, row):
    kk = w.shape[0]
    y = b + w[kk - 1:kk, :] * x
    for j in range(1, kk):
        y = y + w[kk - 1 - j:kk - j, :] * _shift_dn(x, j, 0.0, row)
    return y


def _conv_bwd(dy, x, w, row):
    kk = w.shape[0]
    dx = w[kk - 1:kk, :] * dy
    dws = [None] * kk
    dws[kk - 1] = jnp.sum(dy * x, axis=0, keepdims=True)
    for j in range(1, kk):
        dx = dx + w[kk - 1 - j:kk - j, :] * _shift_up(dy, j, 0.0, row)
        dws[kk - 1 - j] = jnp.sum(dy * _shift_dn(x, j, 0.0, row), axis=0, keepdims=True)
    return dx, jnp.concatenate(dws, axis=0)


def _neg_expm1(x):
    series = x * (1.0 + x * (1.0 / 2 + x * (1.0 / 6 + x * (1.0 / 24 + x * (1.0 / 120 + x * (1.0 / 720 + x * (1.0 / 5040)))))))
    return -jnp.where(x > -0.3, series, jnp.exp(x) - 1.0)


def _softplus(z):
    y = jnp.exp(-jnp.abs(z))
    u = 1.0 + y
    dd = u - 1.0
    log1p = jnp.where(dd == 0.0, y, jnp.log(u) * (y / jnp.where(dd == 0.0, 1.0, dd)))
    return jnp.maximum(z, 0.0) + log1p


def _lru_gates(xc, wa, ba, wx, bx, lam):
    xb = xc.astype(BF16)
    r = _sig(_dot(xb, wa) + ba)
    i = _sig(_dot(xb, wx) + bx)
    sp = _softplus(-lam)
    la = (-LRU_C) * r * sp
    a = jnp.exp(la)
    mult = jnp.sqrt(_neg_expm1(2.0 * la))
    return r, i, sp, la, a, mult


def _scan_fwd(a, u, row):
    s = a.shape[0]
    d = 1
    while d < s:
        u = u + a * _shift_dn(u, d, 0.0, row)
        a = a * _shift_dn(a, d, 1.0, row)
        d *= 2
    return u


def _scan_bwd(b, g, row):
    s = b.shape[0]
    d = 1
    while d < s:
        g = g + b * _shift_up(g, d, 0.0, row)
        b = b * _shift_up(b, d, 1.0, row)
        d *= 2
    return g


def rglru_fwd(xr, cw, cb, wa, ba, wx, bx, lam, name):
    b, s, c = xr.shape
    nb, rb = wa.shape[0], wa.shape[1]
    kk = cw.shape[0]

    def body(x_ref, cw_ref, cb_ref, wa_ref, ba_ref, wx_ref, bx_ref, lam_ref, h_ref):
        row = lax.broadcasted_iota(jnp.int32, (s, rb), 0)
        xc = _conv_fwd(x_ref[...], cw_ref[...], cb_ref[...], row)
        _, i, _, _, a, mult = _lru_gates(xc, wa_ref[...], ba_ref[...], wx_ref[...], bx_ref[...], lam_ref[...])
        h_ref[...] = _scan_fwd(a, mult * (i * xc), row)

    vec = pl.BlockSpec((1, rb), lambda bi, n: (0, n))
    seq = pl.BlockSpec((None, s, rb), lambda bi, n: (bi, 0, n))
    mat = pl.BlockSpec((None, rb, rb), lambda bi, n: (n, 0, 0))
    return pl.pallas_call(
        body, name=name, grid=(b, nb),
        in_specs=[seq, pl.BlockSpec((kk, rb), lambda bi, n: (0, n)), vec, mat, vec, mat, vec, vec],
        out_specs=seq,
        out_shape=jax.ShapeDtypeStruct((b, s, c), F32),
        compiler_params=_params(("parallel", "parallel")),
    )(xr, cw, cb, wa, ba, wx, bx, lam)


def rglru_bwd(xr, h, dh, cw, cb, wa, ba, wx, bx, lam, name):
    b, s, c = xr.shape
    nb, rb = wa.shape[0], wa.shape[1]
    kk = cw.shape[0]

    def body(x_ref, h_ref, dh_ref, cw_ref, cb_ref, wa_ref, ba_ref, wx_ref, bx_ref, lam_ref,
             dx_ref, dcw_ref, dcb_ref, dwa_ref, dba_ref, dwx_ref, dbx_ref, dlam_ref):
        @pl.when(pl.program_id(1) == 0)
        def _():
            for ref in (dcw_ref, dcb_ref, dwa_ref, dba_ref, dwx_ref, dbx_ref, dlam_ref):
                ref[...] = jnp.zeros(ref.shape, F32)

        row = lax.broadcasted_iota(jnp.int32, (s, rb), 0)
        x = x_ref[...]
        cwv = cw_ref[...]
        xc = _conv_fwd(x, cwv, cb_ref[...], row)
        wav, wxv, lamv = wa_ref[...], wx_ref[...], lam_ref[...]
        r, i, sp, la, a, mult = _lru_gates(xc, wav, ba_ref[...], wxv, bx_ref[...], lamv)
        lmb = _scan_bwd(_shift_up(a, 1, 0.0, row), dh_ref[...], row)
        h_prev = _shift_dn(h_ref[...], 1, 0.0, row)
        da = lmb * h_prev
        ixc = i * xc
        dla = da * a - (lmb * ixc) * (a * a) / mult
        di = lmb * mult * xc
        dxc = lmb * mult * i
        dr = dla * ((-LRU_C) * sp)
        dsp = jnp.sum(dla * ((-LRU_C) * r), axis=0, keepdims=True)
        dga = dr * r * (1.0 - r)
        dgx = di * i * (1.0 - i)
        dga_b, dgx_b = dga.astype(BF16), dgx.astype(BF16)
        xb = xc.astype(BF16)
        dwa_ref[...] += _dot_tn(xb, dga_b)
        dwx_ref[...] += _dot_tn(xb, dgx_b)
        dba_ref[...] += jnp.sum(dga, axis=0, keepdims=True)
        dbx_ref[...] += jnp.sum(dgx, axis=0, keepdims=True)
        dlam_ref[...] += dsp * (-_sig(-lamv))
        dxc = dxc + _dot_nt(dga_b, wav) + _dot_nt(dgx_b, wxv)
        dcb_ref[...] += jnp.sum(dxc, axis=0, keepdims=True)
        dx, dcw = _conv_bwd(dxc, x, cwv, row)
        dcw_ref[...] += dcw
        dx_ref[...] = dx.astype(dx_ref.dtype)

    vec = pl.BlockSpec((1, rb), lambda n, bi: (0, n))
    seq = pl.BlockSpec((None, s, rb), lambda n, bi: (bi, 0, n))
    mat = pl.BlockSpec((None, rb, rb), lambda n, bi: (n, 0, 0))
    cws = pl.BlockSpec((kk, rb), lambda n, bi: (0, n))
    sd = jax.ShapeDtypeStruct
    return pl.pallas_call(
        body, name=name, grid=(nb, b),
        in_specs=[seq, seq, seq, cws, vec, mat, vec, mat, vec, vec],
        out_specs=[seq, cws, vec, mat, vec, mat, vec, vec],
        out_shape=[sd((b, s, c), BF16), sd((kk, c), F32), sd((1, c), F32), sd((nb, rb, rb), F32),
                   sd((1, c), F32), sd((nb, rb, rb), F32), sd((1, c), F32), sd((1, c), F32)],
        compiler_params=_params(("parallel", "arbitrary")),
    )(xr, h, dh, cw, cb, wa, ba, wx, bx, lam)


_GELU_C = math.sqrt(2.0 / math.pi)


def _gelu_parts(x):
    th = jnp.tanh(_GELU_C * (x + 0.044715 * x * x * x))
    gel = 0.5 * x * (1.0 + th)
    dgel = 0.5 * (1.0 + th) + 0.5 * x * (1.0 - th * th) * _GELU_C * (1.0 + 3 * 0.044715 * x * x)
    return gel, dgel


def ffn_act(gate_pre, up, cw, cb, name, cbk=256):
    b, s, f = gate_pre.shape
    kk = cw.shape[0]
    cbk = _tile(f, cbk)

    def body(g_ref, u_ref, cw_ref, cb_ref, a_ref):
        row = lax.broadcasted_iota(jnp.int32, (s, cbk), 0)
        gate = _conv_fwd(g_ref[...], cw_ref[...], cb_ref[...], row)
        gel, _ = _gelu_parts(gate)
        a_ref[...] = (gel * u_ref[...]).astype(BF16)

    seq = pl.BlockSpec((None, s, cbk), lambda bi, n: (bi, 0, n))
    return pl.pallas_call(
        body, name=name, grid=(b, f // cbk),
        in_specs=[seq, seq, pl.BlockSpec((kk, cbk), lambda bi, n: (0, n)), pl.BlockSpec((1, cbk), lambda bi, n: (0, n))],
        out_specs=seq,
        out_shape=jax.ShapeDtypeStruct((b, s, f), BF16),
        compiler_params=_params(("parallel", "parallel")),
    )(gate_pre, up, cw, cb)


def ffn_bwd(dact, gate_pre, up, cw, cb, name, cbk=256):
    b, s, f = gate_pre.shape
    kk = cw.shape[0]
    cbk = _tile(f, cbk)

    def body(da_ref, g_ref, u_ref, cw_ref, cb_ref, dg_ref, du_ref, dcw_ref, dcb_ref):
        @pl.when(pl.program_id(1) == 0)
        def _():
            dcw_ref[...] = jnp.zeros(dcw_ref.shape, F32)
            dcb_ref[...] = jnp.zeros(dcb_ref.shape, F32)

        row = lax.broadcasted_iota(jnp.int32, (s, cbk), 0)
        gp = g_ref[...]
        cwv = cw_ref[...]
        gate = _conv_fwd(gp, cwv, cb_ref[...], row)
        gel, dgel = _gelu_parts(gate)
        da = da_ref[...]
        du_ref[...] = (da * gel).astype(BF16)
        dgate = da * u_ref[...] * dgel
        dcb_ref[...] += jnp.sum(dgate, axis=0, keepdims=True)
        dgp, dcw = _conv_bwd(dgate, gp, cwv, row)
        dcw_ref[...] += dcw
        dg_ref[...] = dgp.astype(BF16)

    seq = pl.BlockSpec((None, s, cbk), lambda n, bi: (bi, 0, n))
    cws = pl.BlockSpec((kk, cbk), lambda n, bi: (0, n))
    vec = pl.BlockSpec((1, cbk), lambda n, bi: (0, n))
    sd = jax.ShapeDtypeStruct
    return pl.pallas_call(
        body, name=name, grid=(f // cbk, b),
        in_specs=[seq, seq, seq, cws, vec],
        out_specs=[seq, seq, cws, vec],
        out_shape=[sd((b, s, f), BF16), sd((b, s, f), BF16), sd((kk, f), F32), sd((1, f), F32)],
        compiler_params=_params(("parallel", "arbitrary")),
    )(dact, gate_pre, up, cw, cb)


def _t5_bucket(dist):
    max_exact = REL_BUCKETS // 2
    d = np.maximum(dist, 1).astype(np.float32)
    large = max_exact + np.log(d / max_exact) / math.log(REL_MAX_DIST / max_exact) * (REL_BUCKETS - max_exact)
    large = np.minimum(large.astype(np.int32), REL_BUCKETS - 1)
    return np.where(dist < max_exact, dist, large).astype(np.int32)


def _band(window, dilation):
    qi = np.arange(ATTN_BLOCK)[:, None]
    kj = np.arange(2 * ATTN_BLOCK)[None, :]
    delta = ATTN_BLOCK + qi - kj
    mask = (delta >= 0) & (delta <= window // dilation)
    bucket = _t5_bucket(np.maximum(delta, 0) * dilation)
    return mask, bucket


def _attn_blocks(s, r):
    m = s // r
    assert m % ATTN_BLOCK == 0, "sequence length must be a multiple of dilation * block"
    return m // ATTN_BLOCK


def attn_fwd(qkv, biasm, n_heads, name):
    b, s, _ = qkv.shape
    h = n_heads
    scale = HEAD_DIM ** -0.5
    blk = ATTN_BLOCK

    def body(q1_ref, q2_ref, q3_ref, k_ref, v_ref, bias_ref, o_ref, lse_ref, acc, m_s, l_s):
        m_s[...] = jnp.full(m_s.shape, NEG, F32)
        l_s[...] = jnp.zeros(l_s.shape, F32)
        acc[...] = jnp.zeros(acc.shape, F32)
        for g, q_ref in enumerate((q1_ref, q2_ref, q3_ref)):
            r = DILATED[g][1]
            nb = _attn_blocks(s, r)
            bias_prev = bias_ref[g, :, :blk]
            bias_cur = bias_ref[g, :, blk:]

            def block(idx, carry, q_ref=q_ref, r=r, nb=nb, bias_prev=bias_prev, bias_cur=bias_cur):
                c = idx // nb
                n = idx % nb
                qs = c + r * blk * n
                ks = jnp.where(n > 0, qs - r * blk, qs)
                cur = pl.ds(qs, blk, stride=r)
                prev = pl.ds(ks, blk, stride=r)
                q = q_ref[cur, :].astype(BF16)
                s_cur = _dot_nt(q, k_ref[cur, :].astype(BF16)) * scale + bias_cur
                s_prev = _dot_nt(q, k_ref[prev, :].astype(BF16)) * scale + bias_prev + jnp.where(n > 0, 0.0, NEG)
                m_old = m_s[cur, :]
                m_new = jnp.maximum(m_old, jnp.maximum(jnp.max(s_cur, axis=-1, keepdims=True),
                                                       jnp.max(s_prev, axis=-1, keepdims=True)))
                alpha = jnp.exp(m_old - m_new)
                p_cur = jnp.exp(s_cur - m_new)
                p_prev = jnp.exp(s_prev - m_new)
                l_s[cur, :] = alpha * l_s[cur, :] + jnp.sum(p_cur, axis=-1, keepdims=True) + jnp.sum(p_prev, axis=-1, keepdims=True)
                pv = _dot(p_cur.astype(BF16), v_ref[cur, :].astype(BF16)) + _dot(p_prev.astype(BF16), v_ref[prev, :].astype(BF16))
                acc[cur, :] = alpha * acc[cur, :] + pv
                m_s[cur, :] = m_new
                return carry

            lax.fori_loop(0, r * nb, block, 0)
        l = l_s[...]
        o_ref[...] = acc[...] / l
        lse_ref[...] = m_s[...] + jnp.log(l)

    def col(j):
        return pl.BlockSpec((None, s, HEAD_DIM), lambda bi, hi, j=j: (bi, 0, j * h + hi))

    return pl.pallas_call(
        body, name=name, grid=(b, h),
        in_specs=[col(0), col(1), col(2), col(3), col(4),
                  pl.BlockSpec((N_GROUPS, None, blk, 2 * blk), lambda bi, hi: (0, hi, 0, 0))],
        out_specs=[pl.BlockSpec((None, s, HEAD_DIM), lambda bi, hi: (bi, 0, hi)),
                   pl.BlockSpec((None, None, s, 1), lambda bi, hi: (bi, hi, 0, 0))],
        out_shape=[jax.ShapeDtypeStruct((b, s, h * HEAD_DIM), F32), jax.ShapeDtypeStruct((b, h, s, 1), F32)],
        scratch_shapes=[pltpu.VMEM((s, HEAD_DIM), F32), pltpu.VMEM((s, 1), F32), pltpu.VMEM((s, 1), F32)],
        compiler_params=_params(("parallel", "parallel")),
    )(qkv, qkv, qkv, qkv, qkv, biasm)


def attn_bwd(qkv, biasm, o, lse, do, n_heads, name):
    b, s, _ = qkv.shape
    h = n_heads
    scale = HEAD_DIM ** -0.5
    blk = ATTN_BLOCK

    def body(q1_ref, q2_ref, q3_ref, k_ref, v_ref, bias_ref, o_ref, lse_ref, do_ref,
             dq1_ref, dq2_ref, dq3_ref, dk_ref, dv_ref, ds_ref, dq1_acc, dq2_acc, dq3_acc, dk_acc, dv_acc, delta):
        delta[...] = jnp.sum(do_ref[...] * o_ref[...], axis=-1, keepdims=True)
        dk_acc[...] = jnp.zeros(dk_acc.shape, F32)
        dv_acc[...] = jnp.zeros(dv_acc.shape, F32)
        for g, (q_ref, dq_ref) in enumerate(((q1_ref, dq1_acc), (q2_ref, dq2_acc), (q3_ref, dq3_acc))):
            r = DILATED[g][1]
            nb = _attn_blocks(s, r)
            bias_prev = bias_ref[g, :, :blk]
            bias_cur = bias_ref[g, :, blk:]

            def block(idx, carry, q_ref=q_ref, dq_ref=dq_ref, r=r, nb=nb, bias_prev=bias_prev, bias_cur=bias_cur):
                ds_prev_sum, ds_cur_sum = carry
                c = idx // nb
                n = idx % nb
                qs = c + r * blk * n
                ks = jnp.where(n > 0, qs - r * blk, qs)
                cur = pl.ds(qs, blk, stride=r)
                prev = pl.ds(ks, blk, stride=r)
                q = q_ref[cur, :].astype(BF16)
                k_cur = k_ref[cur, :].astype(BF16)
                k_prev = k_ref[prev, :].astype(BF16)
                v_cur = v_ref[cur, :].astype(BF16)
                v_prev = v_ref[prev, :].astype(BF16)
                dob = do_ref[cur, :].astype(BF16)
                lse_b = lse_ref[cur, :]
                dl_b = delta[cur, :]
                p_cur = jnp.exp(_dot_nt(q, k_cur) * scale + bias_cur - lse_b)
                p_prev = jnp.exp(_dot_nt(q, k_prev) * scale + bias_prev + jnp.where(n > 0, 0.0, NEG) - lse_b)
                ds_cur = p_cur * (_dot_nt(dob, v_cur) - dl_b)
                ds_prev = p_prev * (_dot_nt(dob, v_prev) - dl_b)
                ds_cur_b, ds_prev_b = ds_cur.astype(BF16), ds_prev.astype(BF16)
                dq_ref[cur, :] = (_dot(ds_cur_b, k_cur) + _dot(ds_prev_b, k_prev)) * scale
                dk_acc[prev, :] += _dot_tn(ds_prev_b, q) * scale
                dk_acc[cur, :] += _dot_tn(ds_cur_b, q) * scale
                dv_acc[prev, :] += _dot_tn(p_prev.astype(BF16), dob)
                dv_acc[cur, :] += _dot_tn(p_cur.astype(BF16), dob)
                return ds_prev_sum + ds_prev, ds_cur_sum + ds_cur

            zero = jnp.zeros((blk, blk), F32)
            ds_prev_sum, ds_cur_sum = lax.fori_loop(0, r * nb, block, (zero, zero))
            ds_ref[g, :, :blk] = ds_prev_sum
            ds_ref[g, :, blk:] = ds_cur_sum
        for out_ref, acc_ref in ((dq1_ref, dq1_acc), (dq2_ref, dq2_acc), (dq3_ref, dq3_acc), (dk_ref, dk_acc), (dv_ref, dv_acc)):
            out_ref[...] = acc_ref[...].astype(out_ref.dtype)

    def col(j):
        return pl.BlockSpec((None, s, HEAD_DIM), lambda bi, hi, j=j: (bi, 0, j * h + hi))

    head = pl.BlockSpec((None, s, HEAD_DIM), lambda bi, hi: (bi, 0, hi))
    sd = jax.ShapeDtypeStruct
    return pl.pallas_call(
        body, name=name, grid=(b, h),
        in_specs=[col(0), col(1), col(2), col(3), col(4),
                  pl.BlockSpec((N_GROUPS, None, blk, 2 * blk), lambda bi, hi: (0, hi, 0, 0)),
                  head, pl.BlockSpec((None, None, s, 1), lambda bi, hi: (bi, hi, 0, 0)), head],
        out_specs=[head] * 5 + [pl.BlockSpec((None, None, N_GROUPS, blk, 2 * blk), lambda bi, hi: (bi, hi, 0, 0, 0))],
        out_shape=[sd((b, s, h * HEAD_DIM), BF16)] * 5 + [sd((b, h, N_GROUPS, blk, 2 * blk), F32)],
        scratch_shapes=[pltpu.VMEM((s, HEAD_DIM), F32)] * 5 + [pltpu.VMEM((s, 1), F32)],
        compiler_params=_params(("parallel", "parallel")),
    )(qkv, qkv, qkv, qkv, qkv, biasm, o, lse, do)


def bias_table(rel_rows, bucket_f, n_heads, name):
    g, blk, blk2 = bucket_f.shape
    h = n_heads

    def body(rb_ref, bk_ref, o_ref):
        bk = bk_ref[...]
        rb = rb_ref[...]
        acc = jnp.full((blk, blk2), NEG, F32)
        for bucket in range(REL_BUCKETS):
            acc = jnp.where(bk == float(bucket), rb[:, bucket:bucket + 1], acc)
        o_ref[...] = acc

    return pl.pallas_call(
        body, name=name, grid=(g, h),
        in_specs=[pl.BlockSpec((None, 1, 128), lambda gi, hi: (gi * h + hi, 0, 0)),
                  pl.BlockSpec((None, blk, blk2), lambda gi, hi: (gi, 0, 0))],
        out_specs=pl.BlockSpec((None, None, blk, blk2), lambda gi, hi: (gi, hi, 0, 0)),
        out_shape=jax.ShapeDtypeStruct((g, h, blk, blk2), F32),
        compiler_params=_params(("parallel", "parallel")),
    )(rel_rows, bucket_f)


def bias_grad(ds_sum, bucket_f, name):
    b, h, g, blk, blk2 = ds_sum.shape

    def body(ds_ref, bk_ref, o_ref):
        tot = jnp.sum(ds_ref[...], axis=0)
        bk = bk_ref[...]
        lane = lax.broadcasted_iota(jnp.int32, (1, 128), 1)
        vec = jnp.zeros((1, 128), F32)
        for bucket in range(REL_BUCKETS):
            val = jnp.sum(jnp.where(bk == float(bucket), tot, 0.0), keepdims=True)
            vec = vec + jnp.where(lane == bucket, val, 0.0)
        o_ref[...] = vec

    return pl.pallas_call(
        body, name=name, grid=(g, h),
        in_specs=[pl.BlockSpec((b, None, None, blk, blk2), lambda gi, hi: (0, hi, gi, 0, 0)),
                  pl.BlockSpec((None, blk, blk2), lambda gi, hi: (gi, 0, 0))],
        out_specs=pl.BlockSpec((None, 1, 128), lambda gi, hi: (gi * h + hi, 0, 0)),
        out_shape=jax.ShapeDtypeStruct((g * h, 1, 128), F32),
        compiler_params=_params(("parallel", "parallel")),
    )(ds_sum, bucket_f)


def _chip_peers():
    x, y, c = lax.axis_index("x"), lax.axis_index("y"), lax.axis_index("c")
    me = 2 * x + y
    peers = [(1 - x, y, c), (x, 1 - y, c), (1 - x, 1 - y, c)]
    peer_chip = [2 * (1 - x) + y, 2 * x + (1 - y), 2 * (1 - x) + (1 - y)]
    return me, peers, peer_chip


def _any_specs(n):
    return [pl.BlockSpec(memory_space=pl.ANY)] * n


def gather_chips(shards, name):
    n = len(shards)

    def body(*refs):
        src_refs, out_refs = refs[:n], refs[n:2 * n]
        send_sems, recv_sems, local_sems = refs[2 * n:]
        me, peers, peer_chip = _chip_peers()
        local = [pltpu.make_async_copy(src_refs[i], out_refs[i].at[me], local_sems.at[i]) for i in range(n)]
        for cp in local:
            cp.start()

        def remote(i, k, slot):
            return pltpu.make_async_remote_copy(src_ref=src_refs[i], dst_ref=out_refs[i].at[slot],
                                                send_sem=send_sems.at[3 * i + k], recv_sem=recv_sems.at[3 * i + k],
                                                device_id=peers[k], device_id_type=MESH)

        sends = [remote(i, k, me) for i in range(n) for k in range(3)]
        for cp in sends:
            cp.start()
        for i in range(n):
            for k in range(3):
                remote(i, k, peer_chip[k]).wait_recv()
        for cp in sends:
            cp.wait_send()
        for cp in local:
            cp.wait()

    return pl.pallas_call(
        body, name=name,
        in_specs=_any_specs(n), out_specs=_any_specs(n),
        out_shape=[jax.ShapeDtypeStruct((N_CHIPS,) + a.shape, a.dtype) for a in shards],
        scratch_shapes=[pltpu.SemaphoreType.DMA((3 * n,)), pltpu.SemaphoreType.DMA((3 * n,)),
                        pltpu.SemaphoreType.DMA((n,))],
    )(*shards)


def scatter_chips(slabs, whole, name):
    n = len(slabs)
    arrays = list(slabs) + [whole]

    def body(*refs):
        src_refs, out_refs = refs[:n + 1], refs[n + 1:2 * n + 2]
        send_sems, recv_sems, local_sems = refs[2 * n + 2:]
        me, peers, peer_chip = _chip_peers()

        def src(i, chip):
            return src_refs[i].at[chip] if i < n else src_refs[i]

        local = [pltpu.make_async_copy(src(i, me), out_refs[i].at[me], local_sems.at[i]) for i in range(n + 1)]
        for cp in local:
            cp.start()

        def remote(i, k, src_chip, slot):
            return pltpu.make_async_remote_copy(src_ref=src(i, src_chip), dst_ref=out_refs[i].at[slot],
                                                send_sem=send_sems.at[3 * i + k], recv_sem=recv_sems.at[3 * i + k],
                                                device_id=peers[k], device_id_type=MESH)

        sends = [remote(i, k, peer_chip[k], me) for i in range(n + 1) for k in range(3)]
        for cp in sends:
            cp.start()
        for i in range(n + 1):
            for k in range(3):
                remote(i, k, me, peer_chip[k]).wait_recv()
        for cp in sends:
            cp.wait_send()
        for cp in local:
            cp.wait()

    return pl.pallas_call(
        body, name=name,
        in_specs=_any_specs(n + 1), out_specs=_any_specs(n + 1),
        out_shape=[jax.ShapeDtypeStruct(a.shape, a.dtype) for a in slabs]
        + [jax.ShapeDtypeStruct((N_CHIPS,) + whole.shape, whole.dtype)],
        scratch_shapes=[pltpu.SemaphoreType.DMA((3 * (n + 1),)), pltpu.SemaphoreType.DMA((3 * (n + 1),)),
                        pltpu.SemaphoreType.DMA((n + 1,))],
    )(*arrays)


def swap_cores(bufs, name):
    n = len(bufs)

    def body(*refs):
        src_refs, out_refs, send_sems, recv_sems = refs[:n], refs[n:2 * n], refs[2 * n], refs[2 * n + 1]
        x, y, c = lax.axis_index("x"), lax.axis_index("y"), lax.axis_index("c")
        cps = [pltpu.make_async_remote_copy(src_ref=src_refs[i], dst_ref=out_refs[i], send_sem=send_sems.at[i],
                                            recv_sem=recv_sems.at[i], device_id=(x, y, 1 - c), device_id_type=MESH)
               for i in range(n)]
        for cp in cps:
            cp.start()
        for cp in cps:
            cp.wait()

    return pl.pallas_call(
        body, name=name,
        in_specs=_any_specs(n), out_specs=_any_specs(n),
        out_shape=[jax.ShapeDtypeStruct(a.shape, a.dtype) for a in bufs],
        scratch_shapes=[pltpu.SemaphoreType.DMA((n,)), pltpu.SemaphoreType.DMA((n,))],
    )(*bufs)


def _sum_slots(ref):
    acc = ref[0].astype(F32)
    for j in range(1, ref.shape[0]):
        acc = acc + ref[j].astype(F32)
    return acc


def sum_pairs(mine, other, name, tr=176):
    n, r, w = mine.shape
    tr = _tile(r, tr)

    def body(a_ref, b_ref, o_ref):
        o_ref[...] = _sum_slots(a_ref) + _sum_slots(b_ref)

    spec = pl.BlockSpec((n, tr, w), lambda i: (0, i, 0))
    return pl.pallas_call(
        body, name=name, grid=(r // tr,),
        in_specs=[spec, spec], out_specs=_rows(tr, w),
        out_shape=jax.ShapeDtypeStruct((r, w), F32),
        compiler_params=_params(("parallel",)),
    )(mine, other)


def adamw(w, m, v, gs, name, tr=256):
    r, c = w.shape
    tr = r if r % 8 else _tile(r, tr)
    c1 = 1.0 - ADAM_B1 ** ADAM_STEP
    c2 = 1.0 - ADAM_B2 ** ADAM_STEP
    ng = len(gs)

    def body(w_ref, m_ref, v_ref, *refs):
        g_refs, (g_ref, d_ref, nm_ref, nv_ref) = refs[:ng], refs[ng:]
        g = g_refs[0][...] if ng == 1 else _sum_slots(g_refs[0]) + _sum_slots(g_refs[1])
        nm = ADAM_B1 * m_ref[...] + (1.0 - ADAM_B1) * g
        nv = ADAM_B2 * v_ref[...] + (1.0 - ADAM_B2) * (g * g)
        g_ref[...] = g
        nm_ref[...] = nm
        nv_ref[...] = nv
        d_ref[...] = (-ADAM_LR) * ((nm / c1) / (jnp.sqrt(nv / c2) + ADAM_EPS) + ADAM_WD * w_ref[...])

    spec = _rows(tr, c)
    gspec = spec if ng == 1 else pl.BlockSpec((N_CHIPS, tr, c), lambda i: (0, i, 0))
    return pl.pallas_call(
        body, name=name, grid=(r // tr,),
        in_specs=[spec] * 3 + [gspec] * ng, out_specs=[spec] * 4,
        out_shape=[jax.ShapeDtypeStruct((r, c), F32)] * 4,
        compiler_params=_params(("parallel",)),
    )(w, m, v, *gs)


_PARAMS = (
    ("rel_bias", None), ("norm_mix_pre", None), ("norm_mix_post", None), ("w_in", 1), ("conv_rnn_w", 1),
    ("conv_rnn_b", None), ("w_rg_a", None), ("b_rg_a", None), ("w_rg_x", None), ("b_rg_x", None),
    ("lru_lambda", None), ("w_branch_rnn", 0), ("w_branch_att", 1), ("w_out", 0), ("norm_ffn_pre", None),
    ("norm_ffn_post", None), ("w_ffn_gate", 1), ("w_ffn_up", 1), ("conv_ffn_w", 1), ("conv_ffn_b", None),
    ("w_ffn_down", 0),
)
_SMALL = 65536


def _as2d(a):
    a = a[0] if a.shape[0] == 1 and a.ndim >= 3 else a
    return a.reshape(-1, a.shape[-1]) if a.ndim == 3 else a


def _pack(pieces, dtype):
    flat = jnp.concatenate([p.astype(dtype).reshape(-1) for p in pieces])
    unit = PACK_W * PACK_ROWS
    pad = (-flat.shape[0]) % unit
    flat = jnp.pad(flat, (0, pad))
    return flat.reshape(-1, PACK_W)


def _unpack(buf, shapes):
    flat = buf.reshape(-1)
    out, off = [], 0
    for shp in shapes:
        n = int(np.prod(shp))
        out.append(flat[off:off + n].reshape(shp))
        off += n
    return out


def _join(slots, ax):
    if ax == 0:
        return slots.reshape(-1, slots.shape[-1])
    return jnp.transpose(slots, (1, 0, 2)).reshape(slots.shape[1], -1)


def _cut(full, ax):
    if ax == 0:
        return full.reshape(N_CHIPS, -1, full.shape[-1])
    return jnp.transpose(full.reshape(full.shape[0], N_CHIPS, -1), (1, 0, 2))


def kernel(x, rel_bias, norm_mix_pre, norm_mix_post, w_in, conv_rnn_w, conv_rnn_b, w_rg_a, b_rg_a, w_rg_x, b_rg_x, lru_lambda, w_branch_rnn, w_branch_att, w_out, norm_ffn_pre, norm_ffn_post, w_ffn_gate, w_ffn_up, conv_ffn_w, conv_ffn_b, w_ffn_down, loss_target, m_rel_bias, m_norm_mix_pre, m_norm_mix_post, m_w_in, m_conv_rnn_w, m_conv_rnn_b, m_w_rg_a, m_b_rg_a, m_w_rg_x, m_b_rg_x, m_lru_lambda, m_w_branch_rnn, m_w_branch_att, m_w_out, m_norm_ffn_pre, m_norm_ffn_post, m_w_ffn_gate, m_w_ffn_up, m_conv_ffn_w, m_conv_ffn_b, m_w_ffn_down, v_rel_bias, v_norm_mix_pre, v_norm_mix_post, v_w_in, v_conv_rnn_w, v_conv_rnn_b, v_w_rg_a, v_b_rg_a, v_w_rg_x, v_b_rg_x, v_lru_lambda, v_w_branch_rnn, v_w_branch_att, v_w_out, v_norm_ffn_pre, v_norm_ffn_post, v_w_ffn_gate, v_w_ffn_up, v_conv_ffn_w, v_conv_ffn_b, v_w_ffn_down):
    args = dict(locals())
    names = [n for n, _ in _PARAMS]
    axis = dict(_PARAMS)
    w_loc = {n: args[n] for n in names}
    m_loc = {n: args["m_" + n] for n in names}
    v_loc = {n: args["v_" + n] for n in names}
    sharded = [n for n in names if axis[n] is not None]
    replicated = [n for n in names if axis[n] is None]

    big = [n for n in sharded if w_loc[n].size >= _SMALL]
    small_sharded = [n for n in sharded if n not in big]
    small = replicated + small_sharded

    srcs = [_as2d(w_loc[n]).astype(BF16) for n in big] + [_as2d(w_loc[n]) for n in small_sharded]
    gathered = gather_chips(srcs, "gather_weights")
    full = {n: _join(a, axis[n]) for n, a in zip(big + small_sharded, gathered)}
    for n in replicated:
        full[n] = _as2d(w_loc[n])

    g_big, g_small, loss_part = _local_step(x, loss_target, full)

    pack = _pack([g_small[n] for n in small], BF16)
    received = scatter_chips([g_big[n] for n in big], pack, "scatter_grads")
    sibling = swap_cores(received, "swap_grads")
    small_sum = sum_pairs(received[-1], sibling[-1], "sum_small")
    g_tot = dict(zip(small, _unpack(small_sum, [g_small[n].shape for n in small])))
    chip = 2 * lax.axis_index("x") + lax.axis_index("y")
    for n in small_sharded:
        size = g_tot[n].shape[axis[n]] // N_CHIPS
        g_tot[n] = lax.dynamic_slice_in_dim(g_tot[n], chip * size, size, axis=axis[n])

    out_g, out_d, out_m, out_v = {}, {}, {}, {}
    for i, n in enumerate(names):
        shp = w_loc[n].shape
        gs = (received[big.index(n)], sibling[big.index(n)]) if n in big else (g_tot[n],)
        g, d, nm, nv = adamw(_as2d(w_loc[n]), _as2d(m_loc[n]), _as2d(v_loc[n]), gs, "adamw_" + n)
        out_g[n], out_d[n], out_m[n], out_v[n] = (t.reshape(shp) for t in (g, d, nm, nv))

    d_model = x.shape[-1]
    loss = lax.psum(0.5 * jnp.sum(loss_part) / d_model, ("x", "y", "c"))
    grad_x = g_small["x"]
    return (loss, grad_x, *[out_g[n] for n in names], *[out_d[n] for n in names],
            *[out_m[n] for n in names], *[out_v[n] for n in names])


def _local_step(x, target, p):
    b, s, d = x.shape
    t = b * s
    rnn = p["w_branch_rnn"].shape[0]
    hkv = p["w_branch_att"].shape[0]
    h = hkv // HEAD_DIM
    nq = N_GROUPS * hkv
    ffn = p["w_ffn_down"].shape[0]
    nbk = rnn // p["w_rg_a"].shape[1]

    x2 = x.reshape(t, d)
    tgt = target.reshape(t, d)
    w_in = p["w_in"]
    in_splits = (rnn, nq + 2 * hkv, 2 * d)
    wa = p["w_rg_a"].reshape(nbk, -1, p["w_rg_a"].shape[1]).astype(BF16)
    wx = p["w_rg_x"].reshape(nbk, -1, p["w_rg_x"].shape[1]).astype(BF16)
    cw_r, cb_r = p["conv_rnn_w"], p["conv_rnn_b"]
    cw_f, cb_f = p["conv_ffn_w"], p["conv_ffn_b"]

    masks, buckets = zip(*[_band(w_, r_) for w_, r_ in DILATED])
    bucket_f = jnp.asarray(np.where(np.stack(masks), np.stack(buckets), -1).astype(np.float32))
    rel_rows = jnp.pad(p["rel_bias"].T, ((0, 0), (0, 128 - REL_BUCKETS)))[:, None, :]
    biasm = bias_table(rel_rows, bucket_f, h, "bias_table")

    hn1, (xr, qkv, gts) = norm_mm(x2, p["norm_mix_pre"], [w_in], [in_splits], "in_proj")
    xr3 = xr.reshape(b, s, rnn)
    y_rnn = rglru_fwd(xr3, cw_r, cb_r, wa, p["b_rg_a"], wx, p["b_rg_x"], p["lru_lambda"], "rglru_fwd")
    qkv3 = qkv.reshape(b, s, -1)
    o_att, lse = attn_fwd(qkv3, biasm, h, "attn_fwd")
    merged, br, ba = merge_fwd(y_rnn.reshape(t, rnn), o_att.reshape(t, hkv), gts, p["w_branch_rnn"],
                               p["w_branch_att"], "merge_fwd")
    mix, h1 = mm_norm_res(merged, p["w_out"], p["norm_mix_post"], x2, "out_proj")
    hn2, (gate_pre, up) = norm_mm(h1, p["norm_ffn_pre"], [p["w_ffn_gate"], p["w_ffn_up"]], [(ffn,), (ffn,)], "ffn_in")
    act = ffn_act(gate_pre.reshape(b, s, ffn), up.reshape(b, s, ffn), cw_f, cb_f, "ffn_act")
    ff, y = mm_norm_res(act.reshape(t, ffn), p["w_ffn_down"], p["norm_ffn_post"], h1, "ffn_down")

    g, gb = {}, {}

    def rows4(a):
        return a.reshape(N_CHIPS, -1, a.shape[-1])

    dy, dff, g["norm_ffn_post"], loss_part = loss_norm_bwd(y, tgt, ff, p["norm_ffn_post"], "loss_bwd")
    dact = mm_nt([([dff], p["w_ffn_down"])], F32, "ffn_down_dx")
    gb["w_ffn_down"] = rows4(mm_tn(act.reshape(t, ffn), [dff], "ffn_down_dw"))
    dgp, dup, g["conv_ffn_w"], g["conv_ffn_b"] = ffn_bwd(dact.reshape(b, s, ffn), gate_pre.reshape(b, s, ffn),
                                                        up.reshape(b, s, ffn), cw_f, cb_f, "ffn_bwd")
    dgp, dup = dgp.reshape(t, ffn), dup.reshape(t, ffn)
    dhn2 = mm_nt([([dgp], p["w_ffn_gate"]), ([dup], p["w_ffn_up"])], F32, "ffn_in_dx")
    gb["w_ffn_gate"] = mm_tn(hn2, [dgp], "ffn_gate_dw", col_shards=N_CHIPS)
    gb["w_ffn_up"] = mm_tn(hn2, [dup], "ffn_up_dw", col_shards=N_CHIPS)
    dh1, g["norm_ffn_pre"] = norm_bwd(dhn2, h1, p["norm_ffn_pre"], dy, F32, "ffn_norm_bwd")
    dmix, g["norm_mix_post"] = norm_bwd(dh1, mix, p["norm_mix_post"], None, BF16, "mix_norm_bwd")
    dmerged = mm_nt([([dmix], p["w_out"])], F32, "out_proj_dx")
    gb["w_out"] = rows4(mm_tn(merged, [dmix], "out_proj_dw"))
    dbr, dba, dgts = merge_bwd(dmerged, gts, br, ba, "merge_bwd")
    dy_rnn = mm_nt([([dbr], p["w_branch_rnn"])], F32, "branch_rnn_dx")
    do_att = mm_nt([([dba], p["w_branch_att"])], F32, "branch_att_dx")
    gb["w_branch_rnn"] = rows4(mm_tn(y_rnn.reshape(t, rnn), [dbr], "branch_rnn_dw"))
    gb["w_branch_att"] = mm_tn(o_att.reshape(t, hkv), [dba], "branch_att_dw", col_shards=N_CHIPS)
    (dxr, g["conv_rnn_w"], g["conv_rnn_b"], dwa, g["b_rg_a"], dwx, g["b_rg_x"], g["lru_lambda"]) = rglru_bwd(
        xr3, y_rnn, dy_rnn.reshape(b, s, rnn), cw_r, cb_r, wa, p["b_rg_a"], wx, p["b_rg_x"], p["lru_lambda"], "rglru_bwd")
    g["w_rg_a"] = dwa.reshape(p["w_rg_a"].shape)
    g["w_rg_x"] = dwx.reshape(p["w_rg_x"].shape)
    dq1, dq2, dq3, dk, dv, ds_sum = attn_bwd(qkv3, biasm, o_att, lse, do_att.reshape(b, s, hkv), h, "attn_bwd")
    rows = bias_grad(ds_sum, bucket_f, "bias_grad")
    g["rel_bias"] = rows[:, 0, :REL_BUCKETS].T
    dproj = [dxr.reshape(t, rnn)] + [a.reshape(t, hkv) for a in (dq1, dq2, dq3, dk, dv)] + [dgts]
    dhn1 = mm_nt([(dproj, w_in)], F32, "in_proj_dx")
    dw_a = mm_tn(hn1, dproj[:4], "in_proj_dw_a")[0]
    dw_b = mm_tn(hn1, dproj[4:], "in_proj_dw_b")[0]
    gb["w_in"] = _cut(jnp.concatenate([dw_a, dw_b], axis=1), 1)
    dx, g["norm_mix_pre"] = norm_bwd(dhn1, x2, p["norm_mix_pre"], dh1, F32, "in_norm_bwd")
    g["x"] = dx.reshape(b, s, d)
    return gb, g, loss_part
```

```python
import functools
import math

import numpy as np
import jax
import jax.numpy as jnp
from jax import lax
from jax.experimental import pallas as pl
from jax.experimental.pallas import tpu as pltpu

F32 = jnp.float32
BF16 = jnp.bfloat16

EPS = 1e-6
HEAD_DIM = 128
ATTN_BLOCK = 128
DILATED = ((128, 1), (512, 4), (2048, 16))
N_GROUPS = len(DILATED)
REL_BUCKETS = 32
REL_MAX_DIST = 2048
LRU_C = 8.0
NEG = -1e30

ADAM_LR = 0.001
ADAM_B1 = 0.9
ADAM_B2 = 0.999
ADAM_EPS = 1e-08
ADAM_WD = 0.01
ADAM_STEP = 10

N_CHIPS = 4
PACK_W = 1024
PACK_ROWS = 16
VMEM_LIMIT = 56 * 1024 * 1024
MESH = pl.DeviceIdType.MESH


def _params(sem=None):
    return pltpu.CompilerParams(dimension_semantics=sem, vmem_limit_bytes=VMEM_LIMIT)


def _dot(a, b):
    return jnp.dot(a, b, preferred_element_type=F32)


def _dot_nt(a, b):
    return lax.dot_general(a, b, (((1,), (1,)), ((), ())), preferred_element_type=F32)


def _dot_tn(a, b):
    return lax.dot_general(a, b, (((0,), (0,)), ((), ())), preferred_element_type=F32)


def _sig(x):
    return 1.0 / (1.0 + jnp.exp(-x))


def _rows(tm, w):
    return pl.BlockSpec((tm, w), lambda i: (i, 0))


def _whole(shape):
    nd = len(shape)
    return pl.BlockSpec(tuple(shape), lambda *_: (0,) * nd)


def _tile(t, want):
    while t % want:
        want //= 2
    return want


def norm_mm(x, g, ws, splits, name, ride=(), tm=256):
    t, d = x.shape
    tm = _tile(t, tm)
    nw = len(ws)
    widths = [n for sp in splits for n in sp]

    def body(x_ref, g_ref, *refs):
        w_refs, hn_ref, o_refs = refs[:nw], refs[nw], refs[nw + 1:]
        xv = x_ref[...]
        inv = lax.rsqrt(jnp.mean(xv * xv, axis=-1, keepdims=True) + EPS)
        hn = (xv * inv * g_ref[...]).astype(BF16)
        hn_ref[...] = hn
        o = 0
        for w_ref, sp in zip(w_refs, splits):
            off = 0
            for n in sp:
                o_refs[o][...] = _dot(hn, w_ref[:, off:off + n])
                off += n
                o += 1

    r_ins, r_in_specs, r_outs, r_out_specs, r_sems = _ride_args(ride)
    n_out = 1 + len(widths)
    outs = pl.pallas_call(
        _riding(body, 2 + nw, n_out, 0, ride, 1), name=name, grid=(t // tm,),
        in_specs=[_rows(tm, d), _whole(g.shape)] + [_whole(w.shape) for w in ws] + r_in_specs,
        out_specs=[_rows(tm, d)] + [_rows(tm, n) for n in widths] + r_out_specs,
        out_shape=[jax.ShapeDtypeStruct((t, d), BF16)] + [jax.ShapeDtypeStruct((t, n), F32) for n in widths] + r_outs,
        scratch_shapes=r_sems,
        compiler_params=_params(("arbitrary",)),
    )(x, g, *ws, *r_ins)
    return outs[0], outs[1:n_out], _ride_results(ride, outs[n_out:])


def mm_nt(groups, out_dtype, name, ride=(), tm=256):
    dys_all = [dy for dys, _ in groups for dy in dys]
    ws = [w for _, w in groups]
    t = dys_all[0].shape[0]
    k = ws[0].shape[0]
    tm = _tile(t, tm)
    n = len(dys_all)

    def body(*refs):
        dy_refs, w_refs, o_ref = refs[:n], refs[n:n + len(ws)], refs[n + len(ws)]
        acc = None
        i = 0
        for (dys, _), w_ref in zip(groups, w_refs):
            off = 0
            for dy in dys:
                width = dy.shape[1]
                part = _dot_nt(dy_refs[i][...].astype(BF16), w_ref[:, off:off + width])
                acc = part if acc is None else acc + part
                off += width
                i += 1
        o_ref[...] = acc.astype(o_ref.dtype)

    r_ins, r_in_specs, r_outs, r_out_specs, r_sems = _ride_args(ride)
    outs = pl.pallas_call(
        _riding(body, n + len(ws), 1, 0, ride, 1), name=name, grid=(t // tm,),
        in_specs=[_rows(tm, dy.shape[1]) for dy in dys_all] + [_whole(w.shape) for w in ws] + r_in_specs,
        out_specs=[_rows(tm, k)] + r_out_specs,
        out_shape=[jax.ShapeDtypeStruct((t, k), out_dtype)] + r_outs,
        scratch_shapes=r_sems,
        compiler_params=_params(("arbitrary",) if ride else ("parallel",)),
    )(*dys_all, *ws, *r_ins)
    return (outs[0], _ride_results(ride, outs[1:])) if ride else outs[0]


def mm_tn(a, dys, name, col_shards=1, tm=512):
    t, k = a.shape
    tm = _tile(t, tm)
    n = len(dys)
    ntot = sum(dy.shape[1] for dy in dys)
    wsh = ntot // col_shards

    def body(a_ref, *refs):
        dy_refs, o_ref, acc = refs[:n], refs[n], refs[n + 1]

        @pl.when(pl.program_id(0) == 0)
        def _():
            acc[...] = jnp.zeros(acc.shape, F32)

        av = a_ref[...].astype(BF16)
        off = 0
        for dy_ref in dy_refs:
            width = dy_ref.shape[1]
            acc[:, off:off + width] += _dot_tn(av, dy_ref[...].astype(BF16))
            off += width

        @pl.when(pl.program_id(0) == pl.num_programs(0) - 1)
        def _():
            for j in range(col_shards):
                o_ref[j] = acc[:, j * wsh:(j + 1) * wsh].astype(o_ref.dtype)

    return pl.pallas_call(
        body, name=name, grid=(t // tm,),
        in_specs=[_rows(tm, k)] + [_rows(tm, dy.shape[1]) for dy in dys],
        out_specs=_whole((col_shards, k, wsh)),
        out_shape=jax.ShapeDtypeStruct((col_shards, k, wsh), BF16),
        scratch_shapes=[pltpu.VMEM((k, ntot), F32)],
        compiler_params=_params(("arbitrary",)),
    )(a, *dys)


def mm_norm_res(a, w, g, resid, name, tm=256):
    t, k = a.shape
    d = w.shape[1]
    tm = _tile(t, tm)

    def body(a_ref, w_ref, g_ref, r_ref, p_ref, o_ref):
        prod = _dot(a_ref[...], w_ref[...])
        p_ref[...] = prod
        inv = lax.rsqrt(jnp.mean(prod * prod, axis=-1, keepdims=True) + EPS)
        o_ref[...] = r_ref[...] + prod * inv * g_ref[...]

    return pl.pallas_call(
        body, name=name, grid=(t // tm,),
        in_specs=[_rows(tm, k), _whole(w.shape), _whole(g.shape), _rows(tm, d)],
        out_specs=[_rows(tm, d), _rows(tm, d)],
        out_shape=[jax.ShapeDtypeStruct((t, d), F32)] * 2,
        compiler_params=_params(("parallel",)),
    )(a, w, g, resid)


def _rms_bwd(dz, u, g):
    d = u.shape[-1]
    inv = lax.rsqrt(jnp.mean(u * u, axis=-1, keepdims=True) + EPS)
    dzg = dz * g
    proj = jnp.sum(dzg * u, axis=-1, keepdims=True) * (1.0 / d)
    du = inv * (dzg - u * (inv * inv) * proj)
    dg_rows = dz * u * inv
    return du, dg_rows


def norm_bwd(dz, u, g, add, out_dtype, name, tm=256):
    t, d = u.shape
    tm = _tile(t, tm)
    has_add = add is not None

    def body(*refs):
        if has_add:
            dz_ref, u_ref, g_ref, add_ref, du_ref, dg_ref = refs
        else:
            dz_ref, u_ref, g_ref, du_ref, dg_ref = refs

        @pl.when(pl.program_id(0) == 0)
        def _():
            dg_ref[...] = jnp.zeros(dg_ref.shape, F32)

        du, dg_rows = _rms_bwd(dz_ref[...].astype(F32), u_ref[...], g_ref[...])
        if has_add:
            du = du + add_ref[...]
        du_ref[...] = du.astype(du_ref.dtype)
        dg_ref[...] += jnp.sum(dg_rows, axis=0, keepdims=True)

    ins = [dz, u, g] + ([add] if has_add else [])
    return pl.pallas_call(
        body, name=name, grid=(t // tm,),
        in_specs=[_rows(tm, d), _rows(tm, d), _whole(g.shape)] + ([_rows(tm, d)] if has_add else []),
        out_specs=[_rows(tm, d), _whole((1, d))],
        out_shape=[jax.ShapeDtypeStruct((t, d), out_dtype), jax.ShapeDtypeStruct((1, d), F32)],
        compiler_params=_params(("arbitrary",)),
    )(*ins)


def loss_norm_bwd(y, target, ff, g, name, tm=256):
    t, d = y.shape
    tm = _tile(t, tm)

    def body(y_ref, t_ref, ff_ref, g_ref, dy_ref, dff_ref, dg_ref, loss_ref):
        @pl.when(pl.program_id(0) == 0)
        def _():
            dg_ref[...] = jnp.zeros(dg_ref.shape, F32)
            loss_ref[...] = jnp.zeros(loss_ref.shape, F32)

        err = y_ref[...] - t_ref[...]
        loss_ref[...] += jnp.sum(err * err, axis=0, keepdims=True)
        dy = err * (1.0 / d)
        dy_ref[...] = dy
        du, dg_rows = _rms_bwd(dy, ff_ref[...], g_ref[...])
        dff_ref[...] = du.astype(dff_ref.dtype)
        dg_ref[...] += jnp.sum(dg_rows, axis=0, keepdims=True)

    return pl.pallas_call(
        body, name=name, grid=(t // tm,),
        in_specs=[_rows(tm, d), _rows(tm, d), _rows(tm, d), _whole(g.shape)],
        out_specs=[_rows(tm, d), _rows(tm, d), _whole((1, d)), _whole((1, d))],
        out_shape=[jax.ShapeDtypeStruct((t, d), F32), jax.ShapeDtypeStruct((t, d), BF16),
                   jax.ShapeDtypeStruct((1, d), F32), jax.ShapeDtypeStruct((1, d), F32)],
        compiler_params=_params(("arbitrary",)),
    )(y, target, ff, g)


def merge_fwd(y_rnn, o_att, gts, w_br, w_ba, name, tm=256):
    t = y_rnn.shape[0]
    d = w_br.shape[1]
    tm = _tile(t, tm)

    def body(y_ref, o_ref, g_ref, wbr_ref, wba_ref, m_ref, br_ref, ba_ref):
        br = _dot(y_ref[...].astype(BF16), wbr_ref[...])
        ba = _dot(o_ref[...].astype(BF16), wba_ref[...])
        gv = g_ref[...]
        m_ref[...] = (_sig(gv[:, :d]) * br + _sig(gv[:, d:]) * ba).astype(BF16)
        br_ref[...] = br
        ba_ref[...] = ba

    return pl.pallas_call(
        body, name=name, grid=(t // tm,),
        in_specs=[_rows(tm, y_rnn.shape[1]), _rows(tm, o_att.shape[1]), _rows(tm, 2 * d),
                  _whole(w_br.shape), _whole(w_ba.shape)],
        out_specs=[_rows(tm, d)] * 3,
        out_shape=[jax.ShapeDtypeStruct((t, d), BF16), jax.ShapeDtypeStruct((t, d), F32),
                   jax.ShapeDtypeStruct((t, d), F32)],
        compiler_params=_params(("parallel",)),
    )(y_rnn, o_att, gts, w_br, w_ba)


def merge_bwd(dmerged, gts, br, ba, name, tm=256):
    t, d = dmerged.shape
    tm = _tile(t, tm)

    def body(dm_ref, g_ref, br_ref, ba_ref, dbr_ref, dba_ref, dg_ref):
        dm = dm_ref[...]
        gv = g_ref[...]
        sr = _sig(gv[:, :d])
        sa = _sig(gv[:, d:])
        dbr_ref[...] = (dm * sr).astype(BF16)
        dba_ref[...] = (dm * sa).astype(BF16)
        dg_ref[:, :d] = (dm * br_ref[...] * sr * (1.0 - sr)).astype(BF16)
        dg_ref[:, d:] = (dm * ba_ref[...] * sa * (1.0 - sa)).astype(BF16)

    return pl.pallas_call(
        body, name=name, grid=(t // tm,),
        in_specs=[_rows(tm, d), _rows(tm, 2 * d), _rows(tm, d), _rows(tm, d)],
        out_specs=[_rows(tm, d), _rows(tm, d), _rows(tm, 2 * d)],
        out_shape=[jax.ShapeDtypeStruct((t, d), BF16), jax.ShapeDtypeStruct((t, d), BF16),
                   jax.ShapeDtypeStruct((t, 2 * d), BF16)],
        compiler_params=_params(("parallel",)),
    )(dmerged, gts, br, ba)


def _shift_dn(x, d, fill, row):
    return jnp.where(row >= d, pltpu.roll(x, d, 0), fill)


def _shift_up(x, d, fill, row):
    s = x.shape[0]
    return jnp.where(row < s - d, pltpu.roll(x, s - d, 0), fill)


def _conv_fwd(x, w, b, row):
    kk = w.shape[0]
    y = b + w[kk - 1:kk, :] * x
    for j in range(1, kk):
        y = y + w[kk - 1 - j:kk - j, :] * _shift_dn(x, j, 0.0, row)
    return y


def _conv_bwd(dy, x, w, row):
    kk = w.shape[0]
    dx = w[kk - 1:kk, :] * dy
    dws = [None] * kk
    dws[kk - 1] = jnp.sum(dy * x, axis=0, keepdims=True)
    for j in range(1, kk):
        dx = dx + w[kk - 1 - j:kk - j, :] * _shift_up(dy, j, 0.0, row)
        dws[kk - 1 - j] = jnp.sum(dy * _shift_dn(x, j, 0.0, row), axis=0, keepdims=True)
    return dx, jnp.concatenate(dws, axis=0)


def _neg_expm1(x):
    series = x * (1.0 + x * (1.0 / 2 + x * (1.0 / 6 + x * (1.0 / 24 + x * (1.0 / 120 + x * (1.0 / 720 + x * (1.0 / 5040)))))))
    return -jnp.where(x > -0.3, series, jnp.exp(x) - 1.0)


def _softplus(z):
    y = jnp.exp(-jnp.abs(z))
    u = 1.0 + y
    dd = u - 1.0
    log1p = jnp.where(dd == 0.0, y, jnp.log(u) * (y / jnp.where(dd == 0.0, 1.0, dd)))
    return jnp.maximum(z, 0.0) + log1p


def _lru_gates(xc, wa, ba, wx, bx, lam):
    xb = xc.astype(BF16)
    r = _sig(_dot(xb, wa) + ba)
    i = _sig(_dot(xb, wx) + bx)
    sp = _softplus(-lam)
    la = (-LRU_C) * r * sp
    a = jnp.exp(la)
    mult = jnp.sqrt(_neg_expm1(2.0 * la))
    return r, i, sp, la, a, mult


def _scan_fwd(a, u, row):
    s = a.shape[0]
    d = 1
    while d < s:
        u = u + a * _shift_dn(u, d, 0.0, row)
        a = a * _shift_dn(a, d, 1.0, row)
        d *= 2
    return u


def _scan_bwd(b, g, row):
    s = b.shape[0]
    d = 1
    while d < s:
        g = g + b * _shift_up(g, d, 0.0, row)
        b = b * _shift_up(b, d, 1.0, row)
        d *= 2
    return g


def rglru_fwd(xr, cw, cb, wa, ba, wx, bx, lam, name):
    b, s, c = xr.shape
    nb, rb = wa.shape[0], wa.shape[1]
    kk = cw.shape[0]

    def body(x_ref, cw_ref, cb_ref, wa_ref, ba_ref, wx_ref, bx_ref, lam_ref, h_ref):
        row = lax.broadcasted_iota(jnp.int32, (s, rb), 0)
        xc = _conv_fwd(x_ref[...], cw_ref[...], cb_ref[...], row)
        _, i, _, _, a, mult = _lru_gates(xc, wa_ref[...], ba_ref[...], wx_ref[...], bx_ref[...], lam_ref[...])
        h_ref[...] = _scan_fwd(a, mult * (i * xc), row)

    vec = pl.BlockSpec((1, rb), lambda bi, n: (0, n))
    seq = pl.BlockSpec((None, s, rb), lambda bi, n: (bi, 0, n))
    mat = pl.BlockSpec((None, rb, rb), lambda bi, n: (n, 0, 0))
    return pl.pallas_call(
        body, name=name, grid=(b, nb),
        in_specs=[seq, pl.BlockSpec((kk, rb), lambda bi, n: (0, n)), vec, mat, vec, mat, vec, vec],
        out_specs=seq,
        out_shape=jax.ShapeDtypeStruct((b, s, c), F32),
        compiler_params=_params(("parallel", "parallel")),
    )(xr, cw, cb, wa, ba, wx, bx, lam)


def rglru_bwd(xr, h, dh, cw, cb, wa, ba, wx, bx, lam, name):
    b, s, c = xr.shape
    nb, rb = wa.shape[0], wa.shape[1]
    kk = cw.shape[0]

    def body(x_ref, h_ref, dh_ref, cw_ref, cb_ref, wa_ref, ba_ref, wx_ref, bx_ref, lam_ref,
             dx_ref, dcw_ref, dcb_ref, dwa_ref, dba_ref, dwx_ref, dbx_ref, dlam_ref):
        @pl.when(pl.program_id(1) == 0)
        def _():
            for ref in (dcw_ref, dcb_ref, dwa_ref, dba_ref, dwx_ref, dbx_ref, dlam_ref):
                ref[...] = jnp.zeros(ref.shape, F32)

        row = lax.broadcasted_iota(jnp.int32, (s, rb), 0)
        x = x_ref[...]
        cwv = cw_ref[...]
        xc = _conv_fwd(x, cwv, cb_ref[...], row)
        wav, wxv, lamv = wa_ref[...], wx_ref[...], lam_ref[...]
        r, i, sp, la, a, mult = _lru_gates(xc, wav, ba_ref[...], wxv, bx_ref[...], lamv)
        lmb = _scan_bwd(_shift_up(a, 1, 0.0, row), dh_ref[...], row)
        h_prev = _shift_dn(h_ref[...], 1, 0.0, row)
        da = lmb * h_prev
        ixc = i * xc
        dla = da * a - (lmb * ixc) * (a * a) / mult
        di = lmb * mult * xc
        dxc = lmb * mult * i
        dr = dla * ((-LRU_C) * sp)
        dsp = jnp.sum(dla * ((-LRU_C) * r), axis=0, keepdims=True)
        dga = dr * r * (1.0 - r)
        dgx = di * i * (1.0 - i)
        dga_b, dgx_b = dga.astype(BF16), dgx.astype(BF16)
        xb = xc.astype(BF16)
        dwa_ref[...] += _dot_tn(xb, dga_b)
        dwx_ref[...] += _dot_tn(xb, dgx_b)
        dba_ref[...] += jnp.sum(dga, axis=0, keepdims=True)
        dbx_ref[...] += jnp.sum(dgx, axis=0, keepdims=True)
        dlam_ref[...] += dsp * (-_sig(-lamv))
        dxc = dxc + _dot_nt(dga_b, wav) + _dot_nt(dgx_b, wxv)
        dcb_ref[...] += jnp.sum(dxc, axis=0, keepdims=True)
        dx, dcw = _conv_bwd(dxc, x, cwv, row)
        dcw_ref[...] += dcw
        dx_ref[...] = dx.astype(dx_ref.dtype)

    vec = pl.BlockSpec((1, rb), lambda n, bi: (0, n))
    seq = pl.BlockSpec((None, s, rb), lambda n, bi: (bi, 0, n))
    mat = pl.BlockSpec((None, rb, rb), lambda n, bi: (n, 0, 0))
    cws = pl.BlockSpec((kk, rb), lambda n, bi: (0, n))
    sd = jax.ShapeDtypeStruct
    return pl.pallas_call(
        body, name=name, grid=(nb, b),
        in_specs=[seq, seq, seq, cws, vec, mat, vec, mat, vec, vec],
        out_specs=[seq, cws, vec, mat, vec, mat, vec, vec],
        out_shape=[sd((b, s, c), BF16), sd((kk, c), F32), sd((1, c), F32), sd((nb, rb, rb), F32),
                   sd((1, c), F32), sd((nb, rb, rb), F32), sd((1, c), F32), sd((1, c), F32)],
        compiler_params=_params(("parallel", "arbitrary")),
    )(xr, h, dh, cw, cb, wa, ba, wx, bx, lam)


_GELU_C = math.sqrt(2.0 / math.pi)


def _gelu_parts(x):
    th = jnp.tanh(_GELU_C * (x + 0.044715 * x * x * x))
    gel = 0.5 * x * (1.0 + th)
    dgel = 0.5 * (1.0 + th) + 0.5 * x * (1.0 - th * th) * _GELU_C * (1.0 + 3 * 0.044715 * x * x)
    return gel, dgel


def ffn_act(gate_pre, up, cw, cb, name, cbk=256):
    b, s, f = gate_pre.shape
    kk = cw.shape[0]
    cbk = _tile(f, cbk)

    def body(g_ref, u_ref, cw_ref, cb_ref, a_ref):
        row = lax.broadcasted_iota(jnp.int32, (s, cbk), 0)
        gate = _conv_fwd(g_ref[...], cw_ref[...], cb_ref[...], row)
        gel, _ = _gelu_parts(gate)
        a_ref[...] = (gel * u_ref[...]).astype(BF16)

    seq = pl.BlockSpec((None, s, cbk), lambda bi, n: (bi, 0, n))
    return pl.pallas_call(
        body, name=name, grid=(b, f // cbk),
        in_specs=[seq, seq, pl.BlockSpec((kk, cbk), lambda bi, n: (0, n)), pl.BlockSpec((1, cbk), lambda bi, n: (0, n))],
        out_specs=seq,
        out_shape=jax.ShapeDtypeStruct((b, s, f), BF16),
        compiler_params=_params(("parallel", "parallel")),
    )(gate_pre, up, cw, cb)


def ffn_bwd(dact, gate_pre, up, cw, cb, name, cbk=256):
    b, s, f = gate_pre.shape
    kk = cw.shape[0]
    cbk = _tile(f, cbk)

    def body(da_ref, g_ref, u_ref, cw_ref, cb_ref, dg_ref, du_ref, dcw_ref, dcb_ref):
        @pl.when(pl.program_id(1) == 0)
        def _():
            dcw_ref[...] = jnp.zeros(dcw_ref.shape, F32)
            dcb_ref[...] = jnp.zeros(dcb_ref.shape, F32)

        row = lax.broadcasted_iota(jnp.int32, (s, cbk), 0)
        gp = g_ref[...]
        cwv = cw_ref[...]
        gate = _conv_fwd(gp, cwv, cb_ref[...], row)
        gel, dgel = _gelu_parts(gate)
        da = da_ref[...]
        du_ref[...] = (da * gel).astype(BF16)
        dgate = da * u_ref[...] * dgel
        dcb_ref[...] += jnp.sum(dgate, axis=0, keepdims=True)
        dgp, dcw = _conv_bwd(dgate, gp, cwv, row)
        dcw_ref[...] += dcw
        dg_ref[...] = dgp.astype(BF16)

    seq = pl.BlockSpec((None, s, cbk), lambda n, bi: (bi, 0, n))
    cws = pl.BlockSpec((kk, cbk), lambda n, bi: (0, n))
    vec = pl.BlockSpec((1, cbk), lambda n, bi: (0, n))
    sd = jax.ShapeDtypeStruct
    return pl.pallas_call(
        body, name=name, grid=(f // cbk, b),
        in_specs=[seq, seq, seq, cws, vec],
        out_specs=[seq, seq, cws, vec],
        out_shape=[sd((b, s, f), BF16), sd((b, s, f), BF16), sd((kk, f), F32), sd((1, f), F32)],
        compiler_params=_params(("parallel", "arbitrary")),
    )(dact, gate_pre, up, cw, cb)


def _t5_bucket(dist):
    max_exact = REL_BUCKETS // 2
    d = np.maximum(dist, 1).astype(np.float32)
    large = max_exact + np.log(d / max_exact) / math.log(REL_MAX_DIST / max_exact) * (REL_BUCKETS - max_exact)
    large = np.minimum(large.astype(np.int32), REL_BUCKETS - 1)
    return np.where(dist < max_exact, dist, large).astype(np.int32)


def _band(window, dilation):
    qi = np.arange(ATTN_BLOCK)[:, None]
    kj = np.arange(2 * ATTN_BLOCK)[None, :]
    delta = ATTN_BLOCK + qi - kj
    mask = (delta >= 0) & (delta <= window // dilation)
    bucket = _t5_bucket(np.maximum(delta, 0) * dilation)
    return mask, bucket


def _attn_blocks(s, r):
    m = s // r
    assert m % ATTN_BLOCK == 0, "sequence length must be a multiple of dilation * block"
    return m // ATTN_BLOCK


def attn_fwd(qkv, biasm, n_heads, name, ride=()):
    b, s, _ = qkv.shape
    h = n_heads
    scale = HEAD_DIM ** -0.5
    blk = ATTN_BLOCK

    def body(q1_ref, q2_ref, q3_ref, k_ref, v_ref, bias_ref, o_ref, lse_ref, acc, m_s, l_s):
        m_s[...] = jnp.full(m_s.shape, NEG, F32)
        l_s[...] = jnp.zeros(l_s.shape, F32)
        acc[...] = jnp.zeros(acc.shape, F32)
        for g, q_ref in enumerate((q1_ref, q2_ref, q3_ref)):
            r = DILATED[g][1]
            nb = _attn_blocks(s, r)
            bias_prev = bias_ref[g, :, :blk]
            bias_cur = bias_ref[g, :, blk:]

            def block(idx, carry, q_ref=q_ref, r=r, nb=nb, bias_prev=bias_prev, bias_cur=bias_cur):
                c = idx // nb
                n = idx % nb
                qs = c + r * blk * n
                ks = jnp.where(n > 0, qs - r * blk, qs)
                cur = pl.ds(qs, blk, stride=r)
                prev = pl.ds(ks, blk, stride=r)
                q = q_ref[cur, :].astype(BF16)
                s_cur = _dot_nt(q, k_ref[cur, :].astype(BF16)) * scale + bias_cur
                s_prev = _dot_nt(q, k_ref[prev, :].astype(BF16)) * scale + bias_prev + jnp.where(n > 0, 0.0, NEG)
                m_old = m_s[cur, :]
                m_new = jnp.maximum(m_old, jnp.maximum(jnp.max(s_cur, axis=-1, keepdims=True),
                                                       jnp.max(s_prev, axis=-1, keepdims=True)))
                alpha = jnp.exp(m_old - m_new)
                p_cur = jnp.exp(s_cur - m_new)
                p_prev = jnp.exp(s_prev - m_new)
                l_s[cur, :] = alpha * l_s[cur, :] + jnp.sum(p_cur, axis=-1, keepdims=True) + jnp.sum(p_prev, axis=-1, keepdims=True)
                pv = _dot(p_cur.astype(BF16), v_ref[cur, :].astype(BF16)) + _dot(p_prev.astype(BF16), v_ref[prev, :].astype(BF16))
                acc[cur, :] = alpha * acc[cur, :] + pv
                m_s[cur, :] = m_new
                return carry

            lax.fori_loop(0, r * nb, block, 0)
        l = l_s[...]
        o_ref[...] = acc[...] / l
        lse_ref[...] = m_s[...] + jnp.log(l)

    def col(j):
        return pl.BlockSpec((None, s, HEAD_DIM), lambda bi, hi, j=j: (bi, 0, j * h + hi))

    r_ins, r_in_specs, r_outs, r_out_specs, r_sems = _ride_args(ride)
    outs = pl.pallas_call(
        _riding(body, 6, 2, 3, ride, 2), name=name, grid=(b, h),
        in_specs=[col(0), col(1), col(2), col(3), col(4),
                  pl.BlockSpec((N_GROUPS, None, blk, 2 * blk), lambda bi, hi: (0, hi, 0, 0))] + r_in_specs,
        out_specs=[pl.BlockSpec((None, s, HEAD_DIM), lambda bi, hi: (bi, 0, hi)),
                   pl.BlockSpec((None, None, s, 1), lambda bi, hi: (bi, hi, 0, 0))] + r_out_specs,
        out_shape=[jax.ShapeDtypeStruct((b, s, h * HEAD_DIM), F32), jax.ShapeDtypeStruct((b, h, s, 1), F32)] + r_outs,
        scratch_shapes=[pltpu.VMEM((s, HEAD_DIM), F32), pltpu.VMEM((s, 1), F32), pltpu.VMEM((s, 1), F32)] + r_sems,
        compiler_params=_params(("arbitrary", "arbitrary")),
    )(qkv, qkv, qkv, qkv, qkv, biasm, *r_ins)
    return outs[0], outs[1], _ride_results(ride, outs[2:])


def attn_bwd(qkv, biasm, o, lse, do, n_heads, name, ride=()):
    b, s, _ = qkv.shape
    h = n_heads
    scale = HEAD_DIM ** -0.5
    blk = ATTN_BLOCK

    def body(q1_ref, q2_ref, q3_ref, k_ref, v_ref, bias_ref, o_ref, lse_ref, do_ref,
             dq1_ref, dq2_ref, dq3_ref, dk_ref, dv_ref, ds_ref, dq1_acc, dq2_acc, dq3_acc, dk_acc, dv_acc, delta):
        delta[...] = jnp.sum(do_ref[...] * o_ref[...], axis=-1, keepdims=True)
        dk_acc[...] = jnp.zeros(dk_acc.shape, F32)
        dv_acc[...] = jnp.zeros(dv_acc.shape, F32)
        for g, (q_ref, dq_ref) in enumerate(((q1_ref, dq1_acc), (q2_ref, dq2_acc), (q3_ref, dq3_acc))):
            r = DILATED[g][1]
            nb = _attn_blocks(s, r)
            bias_prev = bias_ref[g, :, :blk]
            bias_cur = bias_ref[g, :, blk:]

            def block(idx, carry, q_ref=q_ref, dq_ref=dq_ref, r=r, nb=nb, bias_prev=bias_prev, bias_cur=bias_cur):
                ds_prev_sum, ds_cur_sum = carry
                c = idx // nb
                n = idx % nb
                qs = c + r * blk * n
                ks = jnp.where(n > 0, qs - r * blk, qs)
                cur = pl.ds(qs, blk, stride=r)
                prev = pl.ds(ks, blk, stride=r)
                q = q_ref[cur, :].astype(BF16)
                k_cur = k_ref[cur, :].astype(BF16)
                k_prev = k_ref[prev, :].astype(BF16)
                v_cur = v_ref[cur, :].astype(BF16)
                v_prev = v_ref[prev, :].astype(BF16)
                dob = do_ref[cur, :].astype(BF16)
                lse_b = lse_ref[cur, :]
                dl_b = delta[cur, :]
                p_cur = jnp.exp(_dot_nt(q, k_cur) * scale + bias_cur - lse_b)
                p_prev = jnp.exp(_dot_nt(q, k_prev) * scale + bias_prev + jnp.where(n > 0, 0.0, NEG) - lse_b)
                ds_cur = p_cur * (_dot_nt(dob, v_cur) - dl_b)
                ds_prev = p_prev * (_dot_nt(dob, v_prev) - dl_b)
                ds_cur_b, ds_prev_b = ds_cur.astype(BF16), ds_prev.astype(BF16)
                dq_ref[cur, :] = (_dot(ds_cur_b, k_cur) + _dot(ds_prev_b, k_prev)) * scale
                dk_acc[prev, :] += _dot_tn(ds_prev_b, q) * scale
                dk_acc[cur, :] += _dot_tn(ds_cur_b, q) * scale
                dv_acc[prev, :] += _dot_tn(p_prev.astype(BF16), dob)
                dv_acc[cur, :] += _dot_tn(p_cur.astype(BF16), dob)
                return ds_prev_sum + ds_prev, ds_cur_sum + ds_cur

            zero = jnp.zeros((blk, blk), F32)
            ds_prev_sum, ds_cur_sum = lax.fori_loop(0, r * nb, block, (zero, zero))
            ds_ref[g, :, :blk] = ds_prev_sum
            ds_ref[g, :, blk:] = ds_cur_sum
        for out_ref, acc_ref in ((dq1_ref, dq1_acc), (dq2_ref, dq2_acc), (dq3_ref, dq3_acc), (dk_ref, dk_acc), (dv_ref, dv_acc)):
            out_ref[...] = acc_ref[...].astype(out_ref.dtype)

    def col(j):
        return pl.BlockSpec((None, s, HEAD_DIM), lambda bi, hi, j=j: (bi, 0, j * h + hi))

    head = pl.BlockSpec((None, s, HEAD_DIM), lambda bi, hi: (bi, 0, hi))
    sd = jax.ShapeDtypeStruct
    r_ins, r_in_specs, r_outs, r_out_specs, r_sems = _ride_args(ride)
    outs = pl.pallas_call(
        _riding(body, 9, 6, 6, ride, 2), name=name, grid=(b, h),
        in_specs=[col(0), col(1), col(2), col(3), col(4),
                  pl.BlockSpec((N_GROUPS, None, blk, 2 * blk), lambda bi, hi: (0, hi, 0, 0)),
                  head, pl.BlockSpec((None, None, s, 1), lambda bi, hi: (bi, hi, 0, 0)), head] + r_in_specs,
        out_specs=[head] * 5 + [pl.BlockSpec((None, None, N_GROUPS, blk, 2 * blk), lambda bi, hi: (bi, hi, 0, 0, 0))]
        + r_out_specs,
        out_shape=[sd((b, s, h * HEAD_DIM), BF16)] * 5 + [sd((b, h, N_GROUPS, blk, 2 * blk), F32)] + r_outs,
        scratch_shapes=[pltpu.VMEM((s, HEAD_DIM), F32)] * 5 + [pltpu.VMEM((s, 1), F32)] + r_sems,
        compiler_params=_params(("arbitrary", "arbitrary")),
    )(qkv, qkv, qkv, qkv, qkv, biasm, o, lse, do, *r_ins)
    return outs[:6], _ride_results(ride, outs[6:])


def bias_table(rel_rows, bucket_f, n_heads, name):
    g, blk, blk2 = bucket_f.shape
    h = n_heads

    def body(rb_ref, bk_ref, o_ref):
        bk = bk_ref[...]
        rb = rb_ref[...]
        acc = jnp.full((blk, blk2), NEG, F32)
        for bucket in range(REL_BUCKETS):
            acc = jnp.where(bk == float(bucket), rb[:, bucket:bucket + 1], acc)
        o_ref[...] = acc

    return pl.pallas_call(
        body, name=name, grid=(g, h),
        in_specs=[pl.BlockSpec((None, 1, 128), lambda gi, hi: (gi * h + hi, 0, 0)),
                  pl.BlockSpec((None, blk, blk2), lambda gi, hi: (gi, 0, 0))],
        out_specs=pl.BlockSpec((None, None, blk, blk2), lambda gi, hi: (gi, hi, 0, 0)),
        out_shape=jax.ShapeDtypeStruct((g, h, blk, blk2), F32),
        compiler_params=_params(("parallel", "parallel")),
    )(rel_rows, bucket_f)


def bias_grad(ds_sum, bucket_f, name):
    b, h, g, blk, blk2 = ds_sum.shape

    def body(ds_ref, bk_ref, o_ref):
        tot = jnp.sum(ds_ref[...], axis=0)
        bk = bk_ref[...]
        lane = lax.broadcasted_iota(jnp.int32, (1, 128), 1)
        vec = jnp.zeros((1, 128), F32)
        for bucket in range(REL_BUCKETS):
            val = jnp.sum(jnp.where(bk == float(bucket), tot, 0.0), keepdims=True)
            vec = vec + jnp.where(lane == bucket, val, 0.0)
        o_ref[...] = vec

    return pl.pallas_call(
        body, name=name, grid=(g, h),
        in_specs=[pl.BlockSpec((b, None, None, blk, blk2), lambda gi, hi: (0, hi, gi, 0, 0)),
                  pl.BlockSpec((None, blk, blk2), lambda gi, hi: (gi, 0, 0))],
        out_specs=pl.BlockSpec((None, 1, 128), lambda gi, hi: (gi * h + hi, 0, 0)),
        out_shape=jax.ShapeDtypeStruct((g * h, 1, 128), F32),
        compiler_params=_params(("parallel", "parallel")),
    )(ds_sum, bucket_f)


def _chip_peers():
    x, y, c = lax.axis_index("x"), lax.axis_index("y"), lax.axis_index("c")
    me = 2 * x + y
    peers = [(1 - x, y, c), (x, 1 - y, c), (1 - x, 1 - y, c)]
    peer_chip = [2 * (1 - x) + y, 2 * x + (1 - y), 2 * (1 - x) + (1 - y)]
    return me, peers, peer_chip


def _any_specs(n):
    return [pl.BlockSpec(memory_space=pl.ANY)] * n


class _Exchange:
    def start(self, ins, outs, sems):
        local, sends, _ = self._copies(ins, outs, sems)
        for cp in local + sends:
            cp.start()

    def wait(self, ins, outs, sems):
        local, sends, recvs = self._copies(ins, outs, sems)
        for cp in recvs():
            cp.wait_recv()
        for cp in sends:
            cp.wait_send()
        for cp in local:
            cp.wait()


class _Gather(_Exchange):
    def __init__(self, arrays):
        n = len(arrays)
        self.ins = list(arrays)
        self.out_shape = [jax.ShapeDtypeStruct((N_CHIPS,) + a.shape, a.dtype) for a in arrays]
        self.sems = [pltpu.SemaphoreType.DMA((3 * n,)), pltpu.SemaphoreType.DMA((3 * n,)), pltpu.SemaphoreType.DMA((n,))]

    def _copies(self, ins, outs, sems):
        send_sems, recv_sems, local_sems = sems
        me, peers, peer_chip = _chip_peers()
        n = len(ins)

        def remote(i, k, slot):
            return pltpu.make_async_remote_copy(src_ref=ins[i], dst_ref=outs[i].at[slot],
                                                send_sem=send_sems.at[3 * i + k], recv_sem=recv_sems.at[3 * i + k],
                                                device_id=peers[k], device_id_type=MESH)

        local = [pltpu.make_async_copy(ins[i], outs[i].at[me], local_sems.at[i]) for i in range(n)]
        sends = [remote(i, k, me) for i in range(n) for k in range(3)]
        return local, sends, lambda: [remote(i, k, peer_chip[k]) for i in range(n) for k in range(3)]


class _Scatter(_Exchange):
    def __init__(self, slabs, whole=()):
        self.n_slabs = len(slabs)
        self.ins = list(slabs) + list(whole)
        n = len(self.ins)
        self.out_shape = [jax.ShapeDtypeStruct(a.shape, a.dtype) for a in slabs] \
            + [jax.ShapeDtypeStruct((N_CHIPS,) + a.shape, a.dtype) for a in whole]
        self.sems = [pltpu.SemaphoreType.DMA((3 * n,)), pltpu.SemaphoreType.DMA((3 * n,)), pltpu.SemaphoreType.DMA((n,))]

    def _copies(self, ins, outs, sems):
        send_sems, recv_sems, local_sems = sems
        me, peers, peer_chip = _chip_peers()
        n = len(ins)

        def src(i, chip):
            return ins[i].at[chip] if i < self.n_slabs else ins[i]

        def remote(i, k, src_chip, slot):
            return pltpu.make_async_remote_copy(src_ref=src(i, src_chip), dst_ref=outs[i].at[slot],
                                                send_sem=send_sems.at[3 * i + k], recv_sem=recv_sems.at[3 * i + k],
                                                device_id=peers[k], device_id_type=MESH)

        local = [pltpu.make_async_copy(src(i, me), outs[i].at[me], local_sems.at[i]) for i in range(n)]
        sends = [remote(i, k, peer_chip[k], me) for i in range(n) for k in range(3)]
        return local, sends, lambda: [remote(i, k, me, peer_chip[k]) for i in range(n) for k in range(3)]


class _Swap(_Exchange):
    def __init__(self, arrays):
        n = len(arrays)
        self.ins = list(arrays)
        self.out_shape = [jax.ShapeDtypeStruct(a.shape, a.dtype) for a in arrays]
        self.sems = [pltpu.SemaphoreType.DMA((n,)), pltpu.SemaphoreType.DMA((n,))]

    def _copies(self, ins, outs, sems):
        send_sems, recv_sems = sems
        x, y, c = lax.axis_index("x"), lax.axis_index("y"), lax.axis_index("c")
        cps = [pltpu.make_async_remote_copy(src_ref=ins[i], dst_ref=outs[i], send_sem=send_sems.at[i],
                                            recv_sem=recv_sems.at[i], device_id=(x, y, 1 - c), device_id_type=MESH)
               for i in range(len(ins))]
        return [], cps, lambda: cps


def _riding(body, n_in, n_out, n_scratch, ride, rank):
    if not ride:
        return body
    r_in = sum(len(e.ins) for e in ride)
    r_out = sum(len(e.out_shape) for e in ride)

    def split(refs, sizes):
        out, a = [], 0
        for sz in sizes:
            out.append(refs[a:a + sz])
            a += sz
        return out

    def wrapped(*refs):
        a = 0
        parts = []
        for sz in (n_in, r_in, n_out, r_out, n_scratch):
            parts.append(refs[a:a + sz])
            a += sz
        own_in, ex_in, own_out, ex_out, own_scratch = parts
        ex_sems = refs[a:]
        ins = split(ex_in, [len(e.ins) for e in ride])
        outs = split(ex_out, [len(e.out_shape) for e in ride])
        sems = split(ex_sems, [len(e.sems) for e in ride])
        if rank:
            first = functools.reduce(jnp.logical_and, [pl.program_id(d) == 0 for d in range(rank)])
            last = functools.reduce(jnp.logical_and, [pl.program_id(d) == pl.num_programs(d) - 1 for d in range(rank)])

            @pl.when(first)
            def _():
                for e, i, o, s in zip(ride, ins, outs, sems):
                    e.start(i, o, s)

            body(*own_in, *own_out, *own_scratch)

            @pl.when(last)
            def _():
                for e, i, o, s in zip(ride, ins, outs, sems):
                    e.wait(i, o, s)
        else:
            for e, i, o, s in zip(ride, ins, outs, sems):
                e.start(i, o, s)
            for e, i, o, s in zip(ride, ins, outs, sems):
                e.wait(i, o, s)

    return wrapped


def _ride_args(ride):
    ins = [a for e in ride for a in e.ins]
    outs = [s for e in ride for s in e.out_shape]
    sems = [s for e in ride for s in e.sems]
    return ins, _any_specs(len(ins)), outs, _any_specs(len(outs)), sems


def _ride_results(ride, flat):
    out, a = [], 0
    for e in ride:
        out.append(list(flat[a:a + len(e.out_shape)]))
        a += len(e.out_shape)
    return out


def exchange(ride, name):
    ins, in_specs, outs, out_specs, sems = _ride_args(ride)
    res = pl.pallas_call(
        _riding(lambda: None, 0, 0, 0, ride, 0), name=name,
        in_specs=in_specs, out_specs=out_specs, out_shape=outs, scratch_shapes=sems,
    )(*ins)
    return _ride_results(ride, res)


def _sum_slots(ref):
    acc = ref[0].astype(F32)
    for j in range(1, ref.shape[0]):
        acc = acc + ref[j].astype(F32)
    return acc


def sum_pairs(mine, other, name, tr=176):
    n, r, w = mine.shape
    tr = _tile(r, tr)

    def body(a_ref, b_ref, o_ref):
        o_ref[...] = _sum_slots(a_ref) + _sum_slots(b_ref)

    spec = pl.BlockSpec((n, tr, w), lambda i: (0, i, 0))
    return pl.pallas_call(
        body, name=name, grid=(r // tr,),
        in_specs=[spec, spec], out_specs=_rows(tr, w),
        out_shape=jax.ShapeDtypeStruct((r, w), F32),
        compiler_params=_params(("parallel",)),
    )(mine, other)


def adamw(w, m, v, gs, name, tr=256):
    r, c = w.shape
    tr = r if r % 8 else _tile(r, tr)
    c1 = 1.0 - ADAM_B1 ** ADAM_STEP
    c2 = 1.0 - ADAM_B2 ** ADAM_STEP
    ng = len(gs)

    def body(w_ref, m_ref, v_ref, *refs):
        g_refs, (g_ref, d_ref, nm_ref, nv_ref) = refs[:ng], refs[ng:]
        g = g_refs[0][...] if ng == 1 else _sum_slots(g_refs[0]) + _sum_slots(g_refs[1])
        nm = ADAM_B1 * m_ref[...] + (1.0 - ADAM_B1) * g
        nv = ADAM_B2 * v_ref[...] + (1.0 - ADAM_B2) * (g * g)
        g_ref[...] = g
        nm_ref[...] = nm
        nv_ref[...] = nv
        d_ref[...] = (-ADAM_LR) * ((nm / c1) / (jnp.sqrt(nv / c2) + ADAM_EPS) + ADAM_WD * w_ref[...])

    spec = _rows(tr, c)
    gspec = spec if ng == 1 else pl.BlockSpec((N_CHIPS, tr, c), lambda i: (0, i, 0))
    return pl.pallas_call(
        body, name=name, grid=(r // tr,),
        in_specs=[spec] * 3 + [gspec] * ng, out_specs=[spec] * 4,
        out_shape=[jax.ShapeDtypeStruct((r, c), F32)] * 4,
        compiler_params=_params(("parallel",)),
    )(w, m, v, *gs)


_PARAMS = (
    ("rel_bias", None), ("norm_mix_pre", None), ("norm_mix_post", None), ("w_in", 1), ("conv_rnn_w", 1),
    ("conv_rnn_b", None), ("w_rg_a", None), ("b_rg_a", None), ("w_rg_x", None), ("b_rg_x", None),
    ("lru_lambda", None), ("w_branch_rnn", 0), ("w_branch_att", 1), ("w_out", 0), ("norm_ffn_pre", None),
    ("norm_ffn_post", None), ("w_ffn_gate", 1), ("w_ffn_up", 1), ("conv_ffn_w", 1), ("conv_ffn_b", None),
    ("w_ffn_down", 0),
)
_SMALL = 65536


def _as2d(a):
    a = a[0] if a.shape[0] == 1 and a.ndim >= 3 else a
    return a.reshape(-1, a.shape[-1]) if a.ndim == 3 else a


def _pack(pieces, dtype):
    flat = jnp.concatenate([p.astype(dtype).reshape(-1) for p in pieces])
    unit = PACK_W * PACK_ROWS
    pad = (-flat.shape[0]) % unit
    flat = jnp.pad(flat, (0, pad))
    return flat.reshape(-1, PACK_W)


def _unpack(buf, shapes):
    flat = buf.reshape(-1)
    out, off = [], 0
    for shp in shapes:
        n = int(np.prod(shp))
        out.append(flat[off:off + n].reshape(shp))
        off += n
    return out


def _join(slots, ax):
    if ax == 0:
        return slots.reshape(-1, slots.shape[-1])
    return jnp.transpose(slots, (1, 0, 2)).reshape(slots.shape[1], -1)


def _cut(full, ax):
    if ax == 0:
        return full.reshape(N_CHIPS, -1, full.shape[-1])
    return jnp.transpose(full.reshape(full.shape[0], N_CHIPS, -1), (1, 0, 2))


def kernel(x, rel_bias, norm_mix_pre, norm_mix_post, w_in, conv_rnn_w, conv_rnn_b, w_rg_a, b_rg_a, w_rg_x, b_rg_x, lru_lambda, w_branch_rnn, w_branch_att, w_out, norm_ffn_pre, norm_ffn_post, w_ffn_gate, w_ffn_up, conv_ffn_w, conv_ffn_b, w_ffn_down, loss_target, m_rel_bias, m_norm_mix_pre, m_norm_mix_post, m_w_in, m_conv_rnn_w, m_conv_rnn_b, m_w_rg_a, m_b_rg_a, m_w_rg_x, m_b_rg_x, m_lru_lambda, m_w_branch_rnn, m_w_branch_att, m_w_out, m_norm_ffn_pre, m_norm_ffn_post, m_w_ffn_gate, m_w_ffn_up, m_conv_ffn_w, m_conv_ffn_b, m_w_ffn_down, v_rel_bias, v_norm_mix_pre, v_norm_mix_post, v_w_in, v_conv_rnn_w, v_conv_rnn_b, v_w_rg_a, v_b_rg_a, v_w_rg_x, v_b_rg_x, v_lru_lambda, v_w_branch_rnn, v_w_branch_att, v_w_out, v_norm_ffn_pre, v_norm_ffn_post, v_w_ffn_gate, v_w_ffn_up, v_conv_ffn_w, v_conv_ffn_b, v_w_ffn_down):
    args = dict(locals())
    names = [n for n, _ in _PARAMS]
    axis = dict(_PARAMS)
    w_loc = {n: args[n] for n in names}
    m_loc = {n: args["m_" + n] for n in names}
    v_loc = {n: args["v_" + n] for n in names}
    sharded = [n for n in names if axis[n] is not None]
    replicated = [n for n in names if axis[n] is None]

    big = [n for n in sharded if w_loc[n].size >= _SMALL]
    small_sharded = [n for n in sharded if n not in big]
    small = replicated + small_sharded

    first = ["w_in"] + small_sharded
    srcs = [_as2d(w_loc[n]).astype(BF16) if n in big else _as2d(w_loc[n]) for n in first]
    (gathered,) = exchange([_Gather(srcs)], "gather_first")
    p = {n: _join(a, axis[n]) for n, a in zip(first, gathered)}
    for n in replicated:
        p[n] = _as2d(w_loc[n])
    shards = {n: _as2d(w_loc[n]).astype(BF16) for n in big if n not in first}

    received, sibling, g_small, loss_part = _local_step(x, loss_target, p, shards)

    pack = _pack([g_small[n] for n in small], BF16)
    ((received["small"],),) = exchange([_Scatter([], [pack])], "scatter_small")
    late = [n for n in received if n not in sibling]
    (swapped,) = exchange([_Swap([received[n] for n in late])], "swap_last")
    sibling.update(zip(late, swapped))
    small_sum = sum_pairs(received["small"], sibling["small"], "sum_small")
    g_tot = dict(zip(small, _unpack(small_sum, [g_small[n].shape for n in small])))
    chip = 2 * lax.axis_index("x") + lax.axis_index("y")
    for n in small_sharded:
        size = g_tot[n].shape[axis[n]] // N_CHIPS
        g_tot[n] = lax.dynamic_slice_in_dim(g_tot[n], chip * size, size, axis=axis[n])

    out_g, out_d, out_m, out_v = {}, {}, {}, {}
    for i, n in enumerate(names):
        shp = w_loc[n].shape
        gs = (received[n], sibling[n]) if n in big else (g_tot[n],)
        g, d, nm, nv = adamw(_as2d(w_loc[n]), _as2d(m_loc[n]), _as2d(v_loc[n]), gs, "adamw_" + n)
        out_g[n], out_d[n], out_m[n], out_v[n] = (t.reshape(shp) for t in (g, d, nm, nv))

    d_model = x.shape[-1]
    loss = lax.psum(0.5 * jnp.sum(loss_part) / d_model, ("x", "y", "c"))
    grad_x = g_small["x"]
    return (loss, grad_x, *[out_g[n] for n in names], *[out_d[n] for n in names],
            *[out_m[n] for n in names], *[out_v[n] for n in names])


def _local_step(x, target, p, shards):
    axis = dict(_PARAMS)
    b, s, d = x.shape
    t = b * s
    rnn = p["b_rg_a"].shape[1]
    ffn = p["conv_ffn_b"].shape[1]
    nbk = rnn // p["w_rg_a"].shape[1]
    hkv = (p["w_in"].shape[1] - rnn - 2 * d) // (N_GROUPS + 2)
    h = hkv // HEAD_DIM
    nq = N_GROUPS * hkv

    x2 = x.reshape(t, d)
    tgt = target.reshape(t, d)
    w_in = p["w_in"]
    in_splits = (rnn, nq + 2 * hkv, 2 * d)
    wa = p["w_rg_a"].reshape(nbk, -1, p["w_rg_a"].shape[1]).astype(BF16)
    wx = p["w_rg_x"].reshape(nbk, -1, p["w_rg_x"].shape[1]).astype(BF16)
    cw_r, cb_r = p["conv_rnn_w"], p["conv_rnn_b"]
    cw_f, cb_f = p["conv_ffn_w"], p["conv_ffn_b"]

    masks, buckets = zip(*[_band(w_, r_) for w_, r_ in DILATED])
    bucket_f = jnp.asarray(np.where(np.stack(masks), np.stack(buckets), -1).astype(np.float32))
    rel_rows = jnp.pad(p["rel_bias"].T, ((0, 0), (0, 128 - REL_BUCKETS)))[:, None, :]
    biasm = bias_table(rel_rows, bucket_f, h, "bias_table")

    early = ["w_branch_rnn", "w_branch_att", "w_out"]
    hn1, (xr, qkv, gts), (got,) = norm_mm(x2, p["norm_mix_pre"], [w_in], [in_splits], "in_proj",
                                          ride=[_Gather([shards[n] for n in early])])
    p.update({n: _join(a, axis[n]) for n, a in zip(early, got)})
    xr3 = xr.reshape(b, s, rnn)
    y_rnn = rglru_fwd(xr3, cw_r, cb_r, wa, p["b_rg_a"], wx, p["b_rg_x"], p["lru_lambda"], "rglru_fwd")
    qkv3 = qkv.reshape(b, s, -1)
    ffn_ws = ["w_ffn_gate", "w_ffn_up", "w_ffn_down"]
    o_att, lse, (got,) = attn_fwd(qkv3, biasm, h, "attn_fwd", ride=[_Gather([shards[n] for n in ffn_ws])])
    p.update({n: _join(a, axis[n]) for n, a in zip(ffn_ws, got)})
    merged, br, ba = merge_fwd(y_rnn.reshape(t, rnn), o_att.reshape(t, hkv), gts, p["w_branch_rnn"],
                               p["w_branch_att"], "merge_fwd")
    mix, h1 = mm_norm_res(merged, p["w_out"], p["norm_mix_post"], x2, "out_proj")
    hn2, (gate_pre, up), _ = norm_mm(h1, p["norm_ffn_pre"], [p["w_ffn_gate"], p["w_ffn_up"]], [(ffn,), (ffn,)], "ffn_in")
    act = ffn_act(gate_pre.reshape(b, s, ffn), up.reshape(b, s, ffn), cw_f, cb_f, "ffn_act")
    ff, y = mm_norm_res(act.reshape(t, ffn), p["w_ffn_down"], p["norm_ffn_post"], h1, "ffn_down")

    g, gb = {}, {}
    recv, sib = {}, {}

    def rows4(a):
        return a.reshape(N_CHIPS, -1, a.shape[-1])

    dy, dff, g["norm_ffn_post"], loss_part = loss_norm_bwd(y, tgt, ff, p["norm_ffn_post"], "loss_bwd")
    dact = mm_nt([([dff], p["w_ffn_down"])], F32, "ffn_down_dx")
    gb["w_ffn_down"] = rows4(mm_tn(act.reshape(t, ffn), [dff], "ffn_down_dw"))
    dgp, dup, g["conv_ffn_w"], g["conv_ffn_b"] = ffn_bwd(dact.reshape(b, s, ffn), gate_pre.reshape(b, s, ffn),
                                                        up.reshape(b, s, ffn), cw_f, cb_f, "ffn_bwd")
    dgp, dup = dgp.reshape(t, ffn), dup.reshape(t, ffn)
    dhn2, ((recv["w_ffn_down"],),) = mm_nt([([dgp], p["w_ffn_gate"]), ([dup], p["w_ffn_up"])], F32, "ffn_in_dx",
                                           ride=[_Scatter([gb["w_ffn_down"]])])
    gb["w_ffn_gate"] = mm_tn(hn2, [dgp], "ffn_gate_dw", col_shards=N_CHIPS)
    gb["w_ffn_up"] = mm_tn(hn2, [dup], "ffn_up_dw", col_shards=N_CHIPS)
    dh1, g["norm_ffn_pre"] = norm_bwd(dhn2, h1, p["norm_ffn_pre"], dy, F32, "ffn_norm_bwd")
    dmix, g["norm_mix_post"] = norm_bwd(dh1, mix, p["norm_mix_post"], None, BF16, "mix_norm_bwd")
    dmerged = mm_nt([([dmix], p["w_out"])], F32, "out_proj_dx")
    gb["w_out"] = rows4(mm_tn(merged, [dmix], "out_proj_dw"))
    dbr, dba, dgts = merge_bwd(dmerged, gts, br, ba, "merge_bwd")
    dy_rnn = mm_nt([([dbr], p["w_branch_rnn"])], F32, "branch_rnn_dx")
    do_att = mm_nt([([dba], p["w_branch_att"])], F32, "branch_att_dx")
    gb["w_branch_rnn"] = rows4(mm_tn(y_rnn.reshape(t, rnn), [dbr], "branch_rnn_dw"))
    gb["w_branch_att"] = mm_tn(o_att.reshape(t, hkv), [dba], "branch_att_dw", col_shards=N_CHIPS)
    (dxr, g["conv_rnn_w"], g["conv_rnn_b"], dwa, g["b_rg_a"], dwx, g["b_rg_x"], g["lru_lambda"]) = rglru_bwd(
        xr3, y_rnn, dy_rnn.reshape(b, s, rnn), cw_r, cb_r, wa, p["b_rg_a"], wx, p["b_rg_x"], p["lru_lambda"], "rglru_bwd")
    g["w_rg_a"] = dwa.reshape(p["w_rg_a"].shape)
    g["w_rg_x"] = dwx.reshape(p["w_rg_x"].shape)
    mid = ["w_ffn_gate", "w_ffn_up", "w_out", "w_branch_rnn", "w_branch_att"]
    (dq1, dq2, dq3, dk, dv, ds_sum), (got, (sib["w_ffn_down"],)) = attn_bwd(
        qkv3, biasm, o_att, lse, do_att.reshape(b, s, hkv), h, "attn_bwd",
        ride=[_Scatter([gb[n] for n in mid]), _Swap([recv["w_ffn_down"]])])
    recv.update(zip(mid, got))
    rows = bias_grad(ds_sum, bucket_f, "bias_grad")
    g["rel_bias"] = rows[:, 0, :REL_BUCKETS].T
    dproj = [dxr.reshape(t, rnn)] + [a.reshape(t, hkv) for a in (dq1, dq2, dq3, dk, dv)] + [dgts]
    dw_a = mm_tn(hn1, dproj[:4], "in_proj_dw_a")[0]
    dw_b = mm_tn(hn1, dproj[4:], "in_proj_dw_b")[0]
    gb["w_in"] = _cut(jnp.concatenate([dw_a, dw_b], axis=1), 1)
    dhn1, ((recv["w_in"],), got) = mm_nt([(dproj, w_in)], F32, "in_proj_dx",
                                         ride=[_Scatter([gb["w_in"]]), _Swap([recv[n] for n in mid])])
    sib.update(zip(mid, got))
    dx, g["norm_mix_pre"] = norm_bwd(dhn1, x2, p["norm_mix_pre"], dh1, F32, "in_norm_bwd")
    g["x"] = dx.reshape(b, s, d)
    return recv, sib, g, loss_part
```

```python
import functools
import math

import numpy as np
import jax
import jax.numpy as jnp
from jax import lax
from jax.experimental import pallas as pl
from jax.experimental.pallas import tpu as pltpu

F32 = jnp.float32
BF16 = jnp.bfloat16

EPS = 1e-6
HEAD_DIM = 128
ATTN_BLOCK = 128
DILATED = ((128, 1), (512, 4), (2048, 16))
N_GROUPS = len(DILATED)
REL_BUCKETS = 32
REL_MAX_DIST = 2048
LRU_C = 8.0
NEG = -1e30

ADAM_LR = 0.001
ADAM_B1 = 0.9
ADAM_B2 = 0.999
ADAM_EPS = 1e-08
ADAM_WD = 0.01
ADAM_STEP = 10

N_CHIPS = 4
PACK_W = 1024
PACK_ROWS = 16
VMEM_LIMIT = 56 * 1024 * 1024
MESH = pl.DeviceIdType.MESH


def _params(sem=None):
    return pltpu.CompilerParams(dimension_semantics=sem, vmem_limit_bytes=VMEM_LIMIT)


def _dot(a, b):
    return jnp.dot(a, b, preferred_element_type=F32)


def _dot_nt(a, b):
    return lax.dot_general(a, b, (((1,), (1,)), ((), ())), preferred_element_type=F32)


def _dot_tn(a, b):
    return lax.dot_general(a, b, (((0,), (0,)), ((), ())), preferred_element_type=F32)


def _sig(x):
    return 1.0 / (1.0 + jnp.exp(-x))


def _rows(tm, w):
    return pl.BlockSpec((tm, w), lambda i: (i, 0))


def _whole(shape):
    nd = len(shape)
    return pl.BlockSpec(tuple(shape), lambda *_: (0,) * nd)


def _tile(t, want):
    while t % want:
        want //= 2
    return want


def norm_mm(x, g, ws, splits, name, ride=(), tm=256):
    t, d = x.shape
    tm = _tile(t, tm)
    nw = len(ws)
    widths = [n for sp in splits for n in sp]

    def body(x_ref, g_ref, *refs):
        w_refs, hn_ref, o_refs = refs[:nw], refs[nw], refs[nw + 1:]
        xv = x_ref[...]
        inv = lax.rsqrt(jnp.mean(xv * xv, axis=-1, keepdims=True) + EPS)
        hn = (xv * inv * g_ref[...]).astype(BF16)
        hn_ref[...] = hn
        o = 0
        for w_ref, sp in zip(w_refs, splits):
            off = 0
            for n in sp:
                o_refs[o][...] = _dot(hn, w_ref[:, off:off + n])
                off += n
                o += 1

    r_ins, r_in_specs, r_outs, r_out_specs, r_sems = _ride_args(ride)
    n_out = 1 + len(widths)
    outs = pl.pallas_call(
        _riding(body, 2 + nw, n_out, 0, ride, 1), name=name, grid=(t // tm,),
        in_specs=[_rows(tm, d), _whole(g.shape)] + [_whole(w.shape) for w in ws] + r_in_specs,
        out_specs=[_rows(tm, d)] + [_rows(tm, n) for n in widths] + r_out_specs,
        out_shape=[jax.ShapeDtypeStruct((t, d), BF16)] + [jax.ShapeDtypeStruct((t, n), F32) for n in widths] + r_outs,
        scratch_shapes=r_sems,
        compiler_params=_params(("arbitrary",)),
    )(x, g, *ws, *r_ins)
    return outs[0], outs[1:n_out], _ride_results(ride, outs[n_out:])


def mm_nt(groups, out_dtype, name, ride=(), tm=256):
    dys_all = [dy for dys, _ in groups for dy in dys]
    ws = [w for _, w in groups]
    t = dys_all[0].shape[0]
    k = ws[0].shape[0]
    tm = _tile(t, tm)
    n = len(dys_all)

    def body(*refs):
        dy_refs, w_refs, o_ref = refs[:n], refs[n:n + len(ws)], refs[n + len(ws)]
        acc = None
        i = 0
        for (dys, _), w_ref in zip(groups, w_refs):
            off = 0
            for dy in dys:
                width = dy.shape[1]
                part = _dot_nt(dy_refs[i][...].astype(BF16), w_ref[:, off:off + width])
                acc = part if acc is None else acc + part
                off += width
                i += 1
        o_ref[...] = acc.astype(o_ref.dtype)

    r_ins, r_in_specs, r_outs, r_out_specs, r_sems = _ride_args(ride)
    outs = pl.pallas_call(
        _riding(body, n + len(ws), 1, 0, ride, 1), name=name, grid=(t // tm,),
        in_specs=[_rows(tm, dy.shape[1]) for dy in dys_all] + [_whole(w.shape) for w in ws] + r_in_specs,
        out_specs=[_rows(tm, k)] + r_out_specs,
        out_shape=[jax.ShapeDtypeStruct((t, k), out_dtype)] + r_outs,
        scratch_shapes=r_sems,
        compiler_params=_params(("arbitrary",) if ride else ("parallel",)),
    )(*dys_all, *ws, *r_ins)
    return (outs[0], _ride_results(ride, outs[1:])) if ride else outs[0]


def mm_tn(a, dys, name, col_shards=1, tm=512):
    t, k = a.shape
    tm = _tile(t, tm)
    n = len(dys)
    ntot = sum(dy.shape[1] for dy in dys)
    wsh = ntot // col_shards

    def body(a_ref, *refs):
        dy_refs, o_ref, acc = refs[:n], refs[n], refs[n + 1]

        @pl.when(pl.program_id(0) == 0)
        def _():
            acc[...] = jnp.zeros(acc.shape, F32)

        av = a_ref[...].astype(BF16)
        off = 0
        for dy_ref in dy_refs:
            width = dy_ref.shape[1]
            acc[:, off:off + width] += _dot_tn(av, dy_ref[...].astype(BF16))
            off += width

        @pl.when(pl.program_id(0) == pl.num_programs(0) - 1)
        def _():
            for j in range(col_shards):
                o_ref[j] = acc[:, j * wsh:(j + 1) * wsh].astype(o_ref.dtype)

    return pl.pallas_call(
        body, name=name, grid=(t // tm,),
        in_specs=[_rows(tm, k)] + [_rows(tm, dy.shape[1]) for dy in dys],
        out_specs=_whole((col_shards, k, wsh)),
        out_shape=jax.ShapeDtypeStruct((col_shards, k, wsh), BF16),
        scratch_shapes=[pltpu.VMEM((k, ntot), F32)],
        compiler_params=_params(("arbitrary",)),
    )(a, *dys)


def mm_norm_res(a, w, g, resid, name, tm=256):
    t, k = a.shape
    d = w.shape[1]
    tm = _tile(t, tm)

    def body(a_ref, w_ref, g_ref, r_ref, p_ref, o_ref):
        prod = _dot(a_ref[...], w_ref[...])
        p_ref[...] = prod
        inv = lax.rsqrt(jnp.mean(prod * prod, axis=-1, keepdims=True) + EPS)
        o_ref[...] = r_ref[...] + prod * inv * g_ref[...]

    return pl.pallas_call(
        body, name=name, grid=(t // tm,),
        in_specs=[_rows(tm, k), _whole(w.shape), _whole(g.shape), _rows(tm, d)],
        out_specs=[_rows(tm, d), _rows(tm, d)],
        out_shape=[jax.ShapeDtypeStruct((t, d), F32)] * 2,
        compiler_params=_params(("parallel",)),
    )(a, w, g, resid)


def _rms_bwd(dz, u, g):
    d = u.shape[-1]
    inv = lax.rsqrt(jnp.mean(u * u, axis=-1, keepdims=True) + EPS)
    dzg = dz * g
    proj = jnp.sum(dzg * u, axis=-1, keepdims=True) * (1.0 / d)
    du = inv * (dzg - u * (inv * inv) * proj)
    dg_rows = dz * u * inv
    return du, dg_rows


def norm_bwd(dz, u, g, add, out_dtype, name, tm=256):
    t, d = u.shape
    tm = _tile(t, tm)
    has_add = add is not None

    def body(*refs):
        if has_add:
            dz_ref, u_ref, g_ref, add_ref, du_ref, dg_ref = refs
        else:
            dz_ref, u_ref, g_ref, du_ref, dg_ref = refs

        @pl.when(pl.program_id(0) == 0)
        def _():
            dg_ref[...] = jnp.zeros(dg_ref.shape, F32)

        du, dg_rows = _rms_bwd(dz_ref[...].astype(F32), u_ref[...], g_ref[...])
        if has_add:
            du = du + add_ref[...]
        du_ref[...] = du.astype(du_ref.dtype)
        dg_ref[...] += jnp.sum(dg_rows, axis=0, keepdims=True)

    ins = [dz, u, g] + ([add] if has_add else [])
    return pl.pallas_call(
        body, name=name, grid=(t // tm,),
        in_specs=[_rows(tm, d), _rows(tm, d), _whole(g.shape)] + ([_rows(tm, d)] if has_add else []),
        out_specs=[_rows(tm, d), _whole((1, d))],
        out_shape=[jax.ShapeDtypeStruct((t, d), out_dtype), jax.ShapeDtypeStruct((1, d), F32)],
        compiler_params=_params(("arbitrary",)),
    )(*ins)


def loss_norm_bwd(y, target, ff, g, name, tm=256):
    t, d = y.shape
    tm = _tile(t, tm)

    def body(y_ref, t_ref, ff_ref, g_ref, dy_ref, dff_ref, dg_ref, loss_ref):
        @pl.when(pl.program_id(0) == 0)
        def _():
            dg_ref[...] = jnp.zeros(dg_ref.shape, F32)
            loss_ref[...] = jnp.zeros(loss_ref.shape, F32)

        err = y_ref[...] - t_ref[...]
        loss_ref[...] += jnp.sum(err * err, axis=0, keepdims=True)
        dy = err * (1.0 / d)
        dy_ref[...] = dy
        du, dg_rows = _rms_bwd(dy, ff_ref[...], g_ref[...])
        dff_ref[...] = du.astype(dff_ref.dtype)
        dg_ref[...] += jnp.sum(dg_rows, axis=0, keepdims=True)

    return pl.pallas_call(
        body, name=name, grid=(t // tm,),
        in_specs=[_rows(tm, d), _rows(tm, d), _rows(tm, d), _whole(g.shape)],
        out_specs=[_rows(tm, d), _rows(tm, d), _whole((1, d)), _whole((1, d))],
        out_shape=[jax.ShapeDtypeStruct((t, d), F32), jax.ShapeDtypeStruct((t, d), BF16),
                   jax.ShapeDtypeStruct((1, d), F32), jax.ShapeDtypeStruct((1, d), F32)],
        compiler_params=_params(("arbitrary",)),
    )(y, target, ff, g)


def merge_fwd(y_rnn, o_att, gts, w_br, w_ba, name, tm=256):
    t = y_rnn.shape[0]
    d = w_br.shape[1]
    tm = _tile(t, tm)

    def body(y_ref, o_ref, g_ref, wbr_ref, wba_ref, m_ref, br_ref, ba_ref):
        br = _dot(y_ref[...].astype(BF16), wbr_ref[...])
        ba = _dot(o_ref[...].astype(BF16), wba_ref[...])
        gv = g_ref[...]
        m_ref[...] = (_sig(gv[:, :d]) * br + _sig(gv[:, d:]) * ba).astype(BF16)
        br_ref[...] = br
        ba_ref[...] = ba

    return pl.pallas_call(
        body, name=name, grid=(t // tm,),
        in_specs=[_rows(tm, y_rnn.shape[1]), _rows(tm, o_att.shape[1]), _rows(tm, 2 * d),
                  _whole(w_br.shape), _whole(w_ba.shape)],
        out_specs=[_rows(tm, d)] * 3,
        out_shape=[jax.ShapeDtypeStruct((t, d), BF16), jax.ShapeDtypeStruct((t, d), F32),
                   jax.ShapeDtypeStruct((t, d), F32)],
        compiler_params=_params(("parallel",)),
    )(y_rnn, o_att, gts, w_br, w_ba)


def merge_bwd(dmerged, gts, br, ba, name, tm=256):
    t, d = dmerged.shape
    tm = _tile(t, tm)

    def body(dm_ref, g_ref, br_ref, ba_ref, dbr_ref, dba_ref, dg_ref):
        dm = dm_ref[...]
        gv = g_ref[...]
        sr = _sig(gv[:, :d])
        sa = _sig(gv[:, d:])
        dbr_ref[...] = (dm * sr).astype(BF16)
        dba_ref[...] = (dm * sa).astype(BF16)
        dg_ref[:, :d] = (dm * br_ref[...] * sr * (1.0 - sr)).astype(BF16)
        dg_ref[:, d:] = (dm * ba_ref[...] * sa * (1.0 - sa)).astype(BF16)

    return pl.pallas_call(
        body, name=name, grid=(t // tm,),
        in_specs=[_rows(tm, d), _rows(tm, 2 * d), _rows(tm, d), _rows(tm, d)],
        out_specs=[_rows(tm, d), _rows(tm, d), _rows(tm, 2 * d)],
        out_shape=[jax.ShapeDtypeStruct((t, d), BF16), jax.ShapeDtypeStruct((t, d), BF16),
                   jax.ShapeDtypeStruct((t, 2 * d), BF16)],
        compiler_params=_params(("parallel",)),
    )(dmerged, gts, br, ba)


def _shift_dn(x, d, fill, row):
    return jnp.where(row >= d, pltpu.roll(x, d, 0), fill)


def _shift_up(x, d, fill, row):
    s = x.shape[0]
    return jnp.where(row < s - d, pltpu.roll(x, s - d, 0), fill)


def _conv_fwd(x, w, b, row):
    kk = w.shape[0]
    y = b + w[kk - 1:kk, :] * x
    for j in range(1, kk):
        y = y + w[kk - 1 - j:kk - j, :] * _shift_dn(x, j, 0.0, row)
    return y


def _conv_bwd(dy, x, w, row):
    kk = w.shape[0]
    dx = w[kk - 1:kk, :] * dy
    dws = [None] * kk
    dws[kk - 1] = jnp.sum(dy * x, axis=0, keepdims=True)
    for j in range(1, kk):
        dx = dx + w[kk - 1 - j:kk - j, :] * _shift_up(dy, j, 0.0, row)
        dws[kk - 1 - j] = jnp.sum(dy * _shift_dn(x, j, 0.0, row), axis=0, keepdims=True)
    return dx, jnp.concatenate(dws, axis=0)


def _neg_expm1(x):
    series = x * (1.0 + x * (1.0 / 2 + x * (1.0 / 6 + x * (1.0 / 24 + x * (1.0 / 120 + x * (1.0 / 720 + x * (1.0 / 5040)))))))
    return -jnp.where(x > -0.3, series, jnp.exp(x) - 1.0)


def _softplus(z):
    y = jnp.exp(-jnp.abs(z))
    u = 1.0 + y
    dd = u - 1.0
    log1p = jnp.where(dd == 0.0, y, jnp.log(u) * (y / jnp.where(dd == 0.0, 1.0, dd)))
    return jnp.maximum(z, 0.0) + log1p


def _lru_gates(xc, wa, ba, wx, bx, lam):
    xb = xc.astype(BF16)
    r = _sig(_dot(xb, wa) + ba)
    i = _sig(_dot(xb, wx) + bx)
    sp = _softplus(-lam)
    la = (-LRU_C) * r * sp
    a = jnp.exp(la)
    mult = jnp.sqrt(_neg_expm1(2.0 * la))
    return r, i, sp, la, a, mult


def _scan_fwd(a, u, row):
    s = a.shape[0]
    d = 1
    while d < s:
        u = u + a * _shift_dn(u, d, 0.0, row)
        a = a * _shift_dn(a, d, 1.0, row)
        d *= 2
    return u


def _scan_bwd(b, g, row):
    s = b.shape[0]
    d = 1
    while d < s:
        g = g + b * _shift_up(g, d, 0.0, row)
        b = b * _shift_up(b, d, 1.0, row)
        d *= 2
    return g


def rglru_fwd(xr, cw, cb, wa, ba, wx, bx, lam, name):
    b, s, c = xr.shape
    nb, rb = wa.shape[0], wa.shape[1]
    kk = cw.shape[0]

    def body(x_ref, cw_ref, cb_ref, wa_ref, ba_ref, wx_ref, bx_ref, lam_ref, h_ref):
        row = lax.broadcasted_iota(jnp.int32, (s, rb), 0)
        xc = _conv_fwd(x_ref[...], cw_ref[...], cb_ref[...], row)
        _, i, _, _, a, mult = _lru_gates(xc, wa_ref[...], ba_ref[...], wx_ref[...], bx_ref[...], lam_ref[...])
        h_ref[...] = _scan_fwd(a, mult * (i * xc), row)

    vec = pl.BlockSpec((1, rb), lambda bi, n: (0, n))
    seq = pl.BlockSpec((None, s, rb), lambda bi, n: (bi, 0, n))
    mat = pl.BlockSpec((None, rb, rb), lambda bi, n: (n, 0, 0))
    return pl.pallas_call(
        body, name=name, grid=(b, nb),
        in_specs=[seq, pl.BlockSpec((kk, rb), lambda bi, n: (0, n)), vec, mat, vec, mat, vec, vec],
        out_specs=seq,
        out_shape=jax.ShapeDtypeStruct((b, s, c), F32),
        compiler_params=_params(("parallel", "parallel")),
    )(xr, cw, cb, wa, ba, wx, bx, lam)


def rglru_bwd(xr, h, dh, cw, cb, wa, ba, wx, bx, lam, name):
    b, s, c = xr.shape
    nb, rb = wa.shape[0], wa.shape[1]
    kk = cw.shape[0]

    def body(x_ref, h_ref, dh_ref, cw_ref, cb_ref, wa_ref, ba_ref, wx_ref, bx_ref, lam_ref,
             dx_ref, dcw_ref, dcb_ref, dwa_ref, dba_ref, dwx_ref, dbx_ref, dlam_ref):
        @pl.when(pl.program_id(1) == 0)
        def _():
            for ref in (dcw_ref, dcb_ref, dwa_ref, dba_ref, dwx_ref, dbx_ref, dlam_ref):
                ref[...] = jnp.zeros(ref.shape, F32)

        row = lax.broadcasted_iota(jnp.int32, (s, rb), 0)
        x = x_ref[...]
        cwv = cw_ref[...]
        xc = _conv_fwd(x, cwv, cb_ref[...], row)
        wav, wxv, lamv = wa_ref[...], wx_ref[...], lam_ref[...]
        r, i, sp, la, a, mult = _lru_gates(xc, wav, ba_ref[...], wxv, bx_ref[...], lamv)
        lmb = _scan_bwd(_shift_up(a, 1, 0.0, row), dh_ref[...], row)
        h_prev = _shift_dn(h_ref[...], 1, 0.0, row)
        da = lmb * h_prev
        ixc = i * xc
        dla = da * a - (lmb * ixc) * (a * a) / mult
        di = lmb * mult * xc
        dxc = lmb * mult * i
        dr = dla * ((-LRU_C) * sp)
        dsp = jnp.sum(dla * ((-LRU_C) * r), axis=0, keepdims=True)
        dga = dr * r * (1.0 - r)
        dgx = di * i * (1.0 - i)
        dga_b, dgx_b = dga.astype(BF16), dgx.astype(BF16)
        xb = xc.astype(BF16)
        dwa_ref[...] += _dot_tn(xb, dga_b)
        dwx_ref[...] += _dot_tn(xb, dgx_b)
        dba_ref[...] += jnp.sum(dga, axis=0, keepdims=True)
        dbx_ref[...] += jnp.sum(dgx, axis=0, keepdims=True)
        dlam_ref[...] += dsp * (-_sig(-lamv))
        dxc = dxc + _dot_nt(dga_b, wav) + _dot_nt(dgx_b, wxv)
        dcb_ref[...] += jnp.sum(dxc, axis=0, keepdims=True)
        dx, dcw = _conv_bwd(dxc, x, cwv, row)
        dcw_ref[...] += dcw
        dx_ref[...] = dx.astype(dx_ref.dtype)

    vec = pl.BlockSpec((1, rb), lambda n, bi: (0, n))
    seq = pl.BlockSpec((None, s, rb), lambda n, bi: (bi, 0, n))
    mat = pl.BlockSpec((None, rb, rb), lambda n, bi: (n, 0, 0))
    cws = pl.BlockSpec((kk, rb), lambda n, bi: (0, n))
    sd = jax.ShapeDtypeStruct
    return pl.pallas_call(
        body, name=name, grid=(nb, b),
        in_specs=[seq, seq, seq, cws, vec, mat, vec, mat, vec, vec],
        out_specs=[seq, cws, vec, mat, vec, mat, vec, vec],
        out_shape=[sd((b, s, c), BF16), sd((kk, c), F32), sd((1, c), F32), sd((nb, rb, rb), F32),
                   sd((1, c), F32), sd((nb, rb, rb), F32), sd((1, c), F32), sd((1, c), F32)],
        compiler_params=_params(("parallel", "arbitrary")),
    )(xr, h, dh, cw, cb, wa, ba, wx, bx, lam)


_GELU_C = math.sqrt(2.0 / math.pi)


def _gelu_parts(x):
    th = jnp.tanh(_GELU_C * (x + 0.044715 * x * x * x))
    gel = 0.5 * x * (1.0 + th)
    dgel = 0.5 * (1.0 + th) + 0.5 * x * (1.0 - th * th) * _GELU_C * (1.0 + 3 * 0.044715 * x * x)
    return gel, dgel


def ffn_act(gate_pre, up, cw, cb, name, cbk=256):
    b, s, f = gate_pre.shape
    kk = cw.shape[0]
    cbk = _tile(f, cbk)

    def body(g_ref, u_ref, cw_ref, cb_ref, a_ref):
        row = lax.broadcasted_iota(jnp.int32, (s, cbk), 0)
        gate = _conv_fwd(g_ref[...], cw_ref[...], cb_ref[...], row)
        gel, _ = _gelu_parts(gate)
        a_ref[...] = (gel * u_ref[...]).astype(BF16)

    seq = pl.BlockSpec((None, s, cbk), lambda bi, n: (bi, 0, n))
    return pl.pallas_call(
        body, name=name, grid=(b, f // cbk),
        in_specs=[seq, seq, pl.BlockSpec((kk, cbk), lambda bi, n: (0, n)), pl.BlockSpec((1, cbk), lambda bi, n: (0, n))],
        out_specs=seq,
        out_shape=jax.ShapeDtypeStruct((b, s, f), BF16),
        compiler_params=_params(("parallel", "parallel")),
    )(gate_pre, up, cw, cb)


def ffn_bwd(dact, gate_pre, up, cw, cb, name, cbk=256):
    b, s, f = gate_pre.shape
    kk = cw.shape[0]
    cbk = _tile(f, cbk)

    def body(da_ref, g_ref, u_ref, cw_ref, cb_ref, dg_ref, du_ref, dcw_ref, dcb_ref):
        @pl.when(pl.program_id(1) == 0)
        def _():
            dcw_ref[...] = jnp.zeros(dcw_ref.shape, F32)
            dcb_ref[...] = jnp.zeros(dcb_ref.shape, F32)

        row = lax.broadcasted_iota(jnp.int32, (s, cbk), 0)
        gp = g_ref[...]
        cwv = cw_ref[...]
        gate = _conv_fwd(gp, cwv, cb_ref[...], row)
        gel, dgel = _gelu_parts(gate)
        da = da_ref[...]
        du_ref[...] = (da * gel).astype(BF16)
        dgate = da * u_ref[...] * dgel
        dcb_ref[...] += jnp.sum(dgate, axis=0, keepdims=True)
        dgp, dcw = _conv_bwd(dgate, gp, cwv, row)
        dcw_ref[...] += dcw
        dg_ref[...] = dgp.astype(BF16)

    seq = pl.BlockSpec((None, s, cbk), lambda n, bi: (bi, 0, n))
    cws = pl.BlockSpec((kk, cbk), lambda n, bi: (0, n))
    vec = pl.BlockSpec((1, cbk), lambda n, bi: (0, n))
    sd = jax.ShapeDtypeStruct
    return pl.pallas_call(
        body, name=name, grid=(f // cbk, b),
        in_specs=[seq, seq, seq, cws, vec],
        out_specs=[seq, seq, cws, vec],
        out_shape=[sd((b, s, f), BF16), sd((b, s, f), BF16), sd((kk, f), F32), sd((1, f), F32)],
        compiler_params=_params(("parallel", "arbitrary")),
    )(dact, gate_pre, up, cw, cb)


def _t5_bucket(dist):
    max_exact = REL_BUCKETS // 2
    d = np.maximum(dist, 1).astype(np.float32)
    large = max_exact + np.log(d / max_exact) / math.log(REL_MAX_DIST / max_exact) * (REL_BUCKETS - max_exact)
    large = np.minimum(large.astype(np.int32), REL_BUCKETS - 1)
    return np.where(dist < max_exact, dist, large).astype(np.int32)


def _band(window, dilation):
    qi = np.arange(ATTN_BLOCK)[:, None]
    kj = np.arange(2 * ATTN_BLOCK)[None, :]
    delta = ATTN_BLOCK + qi - kj
    mask = (delta >= 0) & (delta <= window // dilation)
    bucket = _t5_bucket(np.maximum(delta, 0) * dilation)
    return mask, bucket


def _attn_blocks(s, r):
    m = s // r
    assert m % ATTN_BLOCK == 0, "sequence length must be a multiple of dilation * block"
    return m // ATTN_BLOCK


def _perm_load(ref, r):
    if r == 1:
        return ref[...]
    m = ref.shape[0] // r
    return jnp.concatenate([ref[pl.ds(c, m, stride=r), :] for c in range(r)], axis=0)


def _perm_store(ref, g, val, r, add=False):
    if r == 1:
        ref[g] = ref[g] + val if add else val
        return
    m = val.shape[0] // r
    for c in range(r):
        rows = pl.ds(c, m, stride=r)
        part = val[c * m:(c + 1) * m]
        ref[g, rows, :] = ref[g, rows, :] + part if add else part


def _blocks(x):
    return x.reshape(x.shape[0] // ATTN_BLOCK, ATTN_BLOCK, x.shape[1])


def _prev_blocks(x):
    return jnp.concatenate([x[:1], x[:-1]], axis=0)


def _next_blocks(x):
    return jnp.concatenate([x[1:], jnp.zeros_like(x[:1])], axis=0)


def _first_block_neg(s, r):
    nblk = s // ATTN_BLOCK
    idx = lax.broadcasted_iota(jnp.int32, (nblk, 1, 1), 0)
    return jnp.where(idx % _attn_blocks(s, r) == 0, NEG, 0.0)


def _bdot_nt(a, b):
    return lax.dot_general(a, b, (((2,), (2,)), ((0,), (0,))), preferred_element_type=F32)


def _bdot(a, b):
    return lax.dot_general(a, b, (((2,), (1,)), ((0,), (0,))), preferred_element_type=F32)


def _bdot_tn(a, b):
    return lax.dot_general(a, b, (((1,), (1,)), ((0,), (0,))), preferred_element_type=F32)


def attn_fwd(qkv, biasm, n_heads, name, ride=()):
    b, s, _ = qkv.shape
    h = n_heads
    scale = HEAD_DIM ** -0.5
    blk = ATTN_BLOCK

    def body(q1_ref, q2_ref, q3_ref, k_ref, v_ref, bias_ref, o_ref, lse_ref, acc, m_s, l_s):
        for g, q_ref in enumerate((q1_ref, q2_ref, q3_ref)):
            r = DILATED[g][1]
            first = _first_block_neg(s, r)
            q = _blocks(_perm_load(q_ref, r).astype(BF16))
            k = _blocks(_perm_load(k_ref, r).astype(BF16))
            v = _blocks(_perm_load(v_ref, r).astype(BF16))
            s_cur = _bdot_nt(q, k) * scale + bias_ref[g, :, blk:]
            s_prev = _bdot_nt(q, _prev_blocks(k)) * scale + bias_ref[g, :, :blk] + first
            m = jnp.maximum(jnp.max(s_cur, axis=-1, keepdims=True), jnp.max(s_prev, axis=-1, keepdims=True))
            p_cur = jnp.exp(s_cur - m)
            p_prev = jnp.exp(s_prev - m)
            l = jnp.sum(p_cur, axis=-1, keepdims=True) + jnp.sum(p_prev, axis=-1, keepdims=True)
            o = _bdot(p_cur.astype(BF16), v) + _bdot(p_prev.astype(BF16), _prev_blocks(v))
            _perm_store(acc, g, o.reshape(s, HEAD_DIM), r)
            _perm_store(m_s, g, m.reshape(s, 1), r)
            _perm_store(l_s, g, l.reshape(s, 1), r)
        m_all = jnp.maximum(jnp.maximum(m_s[0], m_s[1]), m_s[2])
        w = [jnp.exp(m_s[g] - m_all) for g in range(N_GROUPS)]
        l = w[0] * l_s[0] + w[1] * l_s[1] + w[2] * l_s[2]
        o_ref[...] = (w[0] * acc[0] + w[1] * acc[1] + w[2] * acc[2]) / l
        lse_ref[...] = m_all + jnp.log(l)

    def col(j):
        return pl.BlockSpec((None, s, HEAD_DIM), lambda bi, hi, j=j: (bi, 0, j * h + hi))

    r_ins, r_in_specs, r_outs, r_out_specs, r_sems = _ride_args(ride)
    outs = pl.pallas_call(
        _riding(body, 6, 2, 3, ride, 2), name=name, grid=(b, h),
        in_specs=[col(0), col(1), col(2), col(3), col(4),
                  pl.BlockSpec((N_GROUPS, None, blk, 2 * blk), lambda bi, hi: (0, hi, 0, 0))] + r_in_specs,
        out_specs=[pl.BlockSpec((None, s, HEAD_DIM), lambda bi, hi: (bi, 0, hi)),
                   pl.BlockSpec((None, None, s, 1), lambda bi, hi: (bi, hi, 0, 0))] + r_out_specs,
        out_shape=[jax.ShapeDtypeStruct((b, s, h * HEAD_DIM), F32), jax.ShapeDtypeStruct((b, h, s, 1), F32)] + r_outs,
        scratch_shapes=[pltpu.VMEM((N_GROUPS, s, HEAD_DIM), F32), pltpu.VMEM((N_GROUPS, s, 1), F32),
                        pltpu.VMEM((N_GROUPS, s, 1), F32)] + r_sems,
        compiler_params=_params(("arbitrary", "arbitrary")),
    )(qkv, qkv, qkv, qkv, qkv, biasm, *r_ins)
    return outs[0], outs[1], _ride_results(ride, outs[2:])


def attn_bwd(qkv, biasm, o, lse, do, n_heads, name, ride=()):
    b, s, _ = qkv.shape
    h = n_heads
    scale = HEAD_DIM ** -0.5
    blk = ATTN_BLOCK

    def body(q1_ref, q2_ref, q3_ref, k_ref, v_ref, bias_ref, o_ref, lse_ref, do_ref,
             dq1_ref, dq2_ref, dq3_ref, dk_ref, dv_ref, ds_ref, dq_acc, kv_acc, delta):
        delta[...] = jnp.sum(do_ref[...] * o_ref[...], axis=-1, keepdims=True)
        kv_acc[...] = jnp.zeros(kv_acc.shape, F32)
        for g, q_ref in enumerate((q1_ref, q2_ref, q3_ref)):
            r = DILATED[g][1]
            first = _first_block_neg(s, r)
            q = _blocks(_perm_load(q_ref, r).astype(BF16))
            k = _blocks(_perm_load(k_ref, r).astype(BF16))
            v = _blocks(_perm_load(v_ref, r).astype(BF16))
            dob = _blocks(_perm_load(do_ref, r).astype(BF16))
            lse_b = _blocks(_perm_load(lse_ref, r))
            dl_b = _blocks(_perm_load(delta, r))
            k_prev, v_prev = _prev_blocks(k), _prev_blocks(v)
            p_cur = jnp.exp(_bdot_nt(q, k) * scale + bias_ref[g, :, blk:] - lse_b)
            p_prev = jnp.exp(_bdot_nt(q, k_prev) * scale + bias_ref[g, :, :blk] + first - lse_b)
            ds_cur = p_cur * (_bdot_nt(dob, v) - dl_b)
            ds_prev = p_prev * (_bdot_nt(dob, v_prev) - dl_b)
            ds_ref[g, :, blk:] = jnp.sum(ds_cur, axis=0)
            ds_ref[g, :, :blk] = jnp.sum(ds_prev, axis=0)
            ds_cur_b, ds_prev_b = ds_cur.astype(BF16), ds_prev.astype(BF16)
            dq = (_bdot(ds_cur_b, k) + _bdot(ds_prev_b, k_prev)) * scale
            _perm_store(dq_acc, g, dq.reshape(s, HEAD_DIM), r)
            dk = (_bdot_tn(ds_cur_b, q) + _next_blocks(_bdot_tn(ds_prev_b, q))) * scale
            dv = _bdot_tn(p_cur.astype(BF16), dob) + _next_blocks(_bdot_tn(p_prev.astype(BF16), dob))
            _perm_store(kv_acc, 0, dk.reshape(s, HEAD_DIM), r, add=True)
            _perm_store(kv_acc, 1, dv.reshape(s, HEAD_DIM), r, add=True)
        for g, out_ref in enumerate((dq1_ref, dq2_ref, dq3_ref)):
            out_ref[...] = dq_acc[g].astype(out_ref.dtype)
        dk_ref[...] = kv_acc[0].astype(dk_ref.dtype)
        dv_ref[...] = kv_acc[1].astype(dv_ref.dtype)

    def col(j):
        return pl.BlockSpec((None, s, HEAD_DIM), lambda bi, hi, j=j: (bi, 0, j * h + hi))

    head = pl.BlockSpec((None, s, HEAD_DIM), lambda bi, hi: (bi, 0, hi))
    sd = jax.ShapeDtypeStruct
    r_ins, r_in_specs, r_outs, r_out_specs, r_sems = _ride_args(ride)
    outs = pl.pallas_call(
        _riding(body, 9, 6, 3, ride, 2), name=name, grid=(b, h),
        in_specs=[col(0), col(1), col(2), col(3), col(4),
                  pl.BlockSpec((N_GROUPS, None, blk, 2 * blk), lambda bi, hi: (0, hi, 0, 0)),
                  head, pl.BlockSpec((None, None, s, 1), lambda bi, hi: (bi, hi, 0, 0)), head] + r_in_specs,
        out_specs=[head] * 5 + [pl.BlockSpec((None, None, N_GROUPS, blk, 2 * blk), lambda bi, hi: (bi, hi, 0, 0, 0))]
        + r_out_specs,
        out_shape=[sd((b, s, h * HEAD_DIM), BF16)] * 5 + [sd((b, h, N_GROUPS, blk, 2 * blk), F32)] + r_outs,
        scratch_shapes=[pltpu.VMEM((N_GROUPS, s, HEAD_DIM), F32), pltpu.VMEM((2, s, HEAD_DIM), F32),
                        pltpu.VMEM((s, 1), F32)] + r_sems,
        compiler_params=_params(("arbitrary", "arbitrary")),
    )(qkv, qkv, qkv, qkv, qkv, biasm, o, lse, do, *r_ins)
    return outs[:6], _ride_results(ride, outs[6:])


def bias_table(rel_rows, bucket_f, n_heads, name):
    g, blk, blk2 = bucket_f.shape
    h = n_heads

    def body(rb_ref, bk_ref, o_ref):
        bk = bk_ref[...]
        rb = rb_ref[...]
        acc = jnp.full((blk, blk2), NEG, F32)
        for bucket in range(REL_BUCKETS):
            acc = jnp.where(bk == float(bucket), rb[:, bucket:bucket + 1], acc)
        o_ref[...] = acc

    return pl.pallas_call(
        body, name=name, grid=(g, h),
        in_specs=[pl.BlockSpec((None, 1, 128), lambda gi, hi: (gi * h + hi, 0, 0)),
                  pl.BlockSpec((None, blk, blk2), lambda gi, hi: (gi, 0, 0))],
        out_specs=pl.BlockSpec((None, None, blk, blk2), lambda gi, hi: (gi, hi, 0, 0)),
        out_shape=jax.ShapeDtypeStruct((g, h, blk, blk2), F32),
        compiler_params=_params(("parallel", "parallel")),
    )(rel_rows, bucket_f)


def bias_grad(ds_sum, bucket_f, name):
    b, h, g, blk, blk2 = ds_sum.shape

    def body(ds_ref, bk_ref, o_ref):
        tot = jnp.sum(ds_ref[...], axis=0)
        bk = bk_ref[...]
        lane = lax.broadcasted_iota(jnp.int32, (1, 128), 1)
        vec = jnp.zeros((1, 128), F32)
        for bucket in range(REL_BUCKETS):
            val = jnp.sum(jnp.where(bk == float(bucket), tot, 0.0), keepdims=True)
            vec = vec + jnp.where(lane == bucket, val, 0.0)
        o_ref[...] = vec

    return pl.pallas_call(
        body, name=name, grid=(g, h),
        in_specs=[pl.BlockSpec((b, None, None, blk, blk2), lambda gi, hi: (0, hi, gi, 0, 0)),
                  pl.BlockSpec((None, blk, blk2), lambda gi, hi: (gi, 0, 0))],
        out_specs=pl.BlockSpec((None, 1, 128), lambda gi, hi: (gi * h + hi, 0, 0)),
        out_shape=jax.ShapeDtypeStruct((g * h, 1, 128), F32),
        compiler_params=_params(("parallel", "parallel")),
    )(ds_sum, bucket_f)


def _chip_peers():
    x, y, c = lax.axis_index("x"), lax.axis_index("y"), lax.axis_index("c")
    me = 2 * x + y
    peers = [(1 - x, y, c), (x, 1 - y, c), (1 - x, 1 - y, c)]
    peer_chip = [2 * (1 - x) + y, 2 * x + (1 - y), 2 * (1 - x) + (1 - y)]
    return me, peers, peer_chip


def _any_specs(n):
    return [pl.BlockSpec(memory_space=pl.ANY)] * n


class _Exchange:
    def start(self, ins, outs, sems):
        local, sends, _ = self._copies(ins, outs, sems)
        for cp in local + sends:
            cp.start()

    def wait(self, ins, outs, sems):
        local, sends, recvs = self._copies(ins, outs, sems)
        for cp in recvs():
            cp.wait_recv()
        for cp in sends:
            cp.wait_send()
        for cp in local:
            cp.wait()


class _Gather(_Exchange):
    def __init__(self, arrays):
        n = len(arrays)
        self.ins = list(arrays)
        self.out_shape = [jax.ShapeDtypeStruct((N_CHIPS,) + a.shape, a.dtype) for a in arrays]
        self.sems = [pltpu.SemaphoreType.DMA((3 * n,)), pltpu.SemaphoreType.DMA((3 * n,)), pltpu.SemaphoreType.DMA((n,))]

    def _copies(self, ins, outs, sems):
        send_sems, recv_sems, local_sems = sems
        me, peers, peer_chip = _chip_peers()
        n = len(ins)

        def remote(i, k, slot):
            return pltpu.make_async_remote_copy(src_ref=ins[i], dst_ref=outs[i].at[slot],
                                                send_sem=send_sems.at[3 * i + k], recv_sem=recv_sems.at[3 * i + k],
                                                device_id=peers[k], device_id_type=MESH)

        local = [pltpu.make_async_copy(ins[i], outs[i].at[me], local_sems.at[i]) for i in range(n)]
        sends = [remote(i, k, me) for i in range(n) for k in range(3)]
        return local, sends, lambda: [remote(i, k, peer_chip[k]) for i in range(n) for k in range(3)]


class _Scatter(_Exchange):
    def __init__(self, slabs, whole=()):
        self.n_slabs = len(slabs)
        self.ins = list(slabs) + list(whole)
        n = len(self.ins)
        self.out_shape = [jax.ShapeDtypeStruct(a.shape, a.dtype) for a in slabs] \
            + [jax.ShapeDtypeStruct((N_CHIPS,) + a.shape, a.dtype) for a in whole]
        self.sems = [pltpu.SemaphoreType.DMA((3 * n,)), pltpu.SemaphoreType.DMA((3 * n,)), pltpu.SemaphoreType.DMA((n,))]

    def _copies(self, ins, outs, sems):
        send_sems, recv_sems, local_sems = sems
        me, peers, peer_chip = _chip_peers()
        n = len(ins)

        def src(i, chip):
            return ins[i].at[chip] if i < self.n_slabs else ins[i]

        def remote(i, k, src_chip, slot):
            return pltpu.make_async_remote_copy(src_ref=src(i, src_chip), dst_ref=outs[i].at[slot],
                                                send_sem=send_sems.at[3 * i + k], recv_sem=recv_sems.at[3 * i + k],
                                                device_id=peers[k], device_id_type=MESH)

        local = [pltpu.make_async_copy(src(i, me), outs[i].at[me], local_sems.at[i]) for i in range(n)]
        sends = [remote(i, k, peer_chip[k], me) for i in range(n) for k in range(3)]
        return local, sends, lambda: [remote(i, k, me, peer_chip[k]) for i in range(n) for k in range(3)]


class _Swap(_Exchange):
    def __init__(self, arrays):
        n = len(arrays)
        self.ins = list(arrays)
        self.out_shape = [jax.ShapeDtypeStruct(a.shape, a.dtype) for a in arrays]
        self.sems = [pltpu.SemaphoreType.DMA((n,)), pltpu.SemaphoreType.DMA((n,))]

    def _copies(self, ins, outs, sems):
        send_sems, recv_sems = sems
        x, y, c = lax.axis_index("x"), lax.axis_index("y"), lax.axis_index("c")
        cps = [pltpu.make_async_remote_copy(src_ref=ins[i], dst_ref=outs[i], send_sem=send_sems.at[i],
                                            recv_sem=recv_sems.at[i], device_id=(x, y, 1 - c), device_id_type=MESH)
               for i in range(len(ins))]
        return [], cps, lambda: cps


def _riding(body, n_in, n_out, n_scratch, ride, rank):
    if not ride:
        return body
    r_in = sum(len(e.ins) for e in ride)
    r_out = sum(len(e.out_shape) for e in ride)

    def split(refs, sizes):
        out, a = [], 0
        for sz in sizes:
            out.append(refs[a:a + sz])
            a += sz
        return out

    def wrapped(*refs):
        a = 0
        parts = []
        for sz in (n_in, r_in, n_out, r_out, n_scratch):
            parts.append(refs[a:a + sz])
            a += sz
        own_in, ex_in, own_out, ex_out, own_scratch = parts
        ex_sems = refs[a:]
        ins = split(ex_in, [len(e.ins) for e in ride])
        outs = split(ex_out, [len(e.out_shape) for e in ride])
        sems = split(ex_sems, [len(e.sems) for e in ride])
        if rank:
            first = functools.reduce(jnp.logical_and, [pl.program_id(d) == 0 for d in range(rank)])
            last = functools.reduce(jnp.logical_and, [pl.program_id(d) == pl.num_programs(d) - 1 for d in range(rank)])

            @pl.when(first)
            def _():
                for e, i, o, s in zip(ride, ins, outs, sems):
                    e.start(i, o, s)

            body(*own_in, *own_out, *own_scratch)

            @pl.when(last)
            def _():
                for e, i, o, s in zip(ride, ins, outs, sems):
                    e.wait(i, o, s)
        else:
            for e, i, o, s in zip(ride, ins, outs, sems):
                e.start(i, o, s)
            for e, i, o, s in zip(ride, ins, outs, sems):
                e.wait(i, o, s)

    return wrapped


def _ride_args(ride):
    ins = [a for e in ride for a in e.ins]
    outs = [s for e in ride for s in e.out_shape]
    sems = [s for e in ride for s in e.sems]
    return ins, _any_specs(len(ins)), outs, _any_specs(len(outs)), sems


def _ride_results(ride, flat):
    out, a = [], 0
    for e in ride:
        out.append(list(flat[a:a + len(e.out_shape)]))
        a += len(e.out_shape)
    return out


def exchange(ride, name):
    ins, in_specs, outs, out_specs, sems = _ride_args(ride)
    res = pl.pallas_call(
        _riding(lambda: None, 0, 0, 0, ride, 0), name=name,
        in_specs=in_specs, out_specs=out_specs, out_shape=outs, scratch_shapes=sems,
    )(*ins)
    return _ride_results(ride, res)


def _sum_slots(ref):
    acc = ref[0].astype(F32)
    for j in range(1, ref.shape[0]):
        acc = acc + ref[j].astype(F32)
    return acc


def sum_pairs(mine, other, name, tr=176):
    n, r, w = mine.shape
    tr = _tile(r, tr)

    def body(a_ref, b_ref, o_ref):
        o_ref[...] = _sum_slots(a_ref) + _sum_slots(b_ref)

    spec = pl.BlockSpec((n, tr, w), lambda i: (0, i, 0))
    return pl.pallas_call(
        body, name=name, grid=(r // tr,),
        in_specs=[spec, spec], out_specs=_rows(tr, w),
        out_shape=jax.ShapeDtypeStruct((r, w), F32),
        compiler_params=_params(("parallel",)),
    )(mine, other)


def adamw(w, m, v, gs, name, tr=256):
    r, c = w.shape
    tr = r if r % 8 else _tile(r, tr)
    c1 = 1.0 - ADAM_B1 ** ADAM_STEP
    c2 = 1.0 - ADAM_B2 ** ADAM_STEP
    ng = len(gs)

    def body(w_ref, m_ref, v_ref, *refs):
        g_refs, (g_ref, d_ref, nm_ref, nv_ref) = refs[:ng], refs[ng:]
        g = g_refs[0][...] if ng == 1 else _sum_slots(g_refs[0]) + _sum_slots(g_refs[1])
        nm = ADAM_B1 * m_ref[...] + (1.0 - ADAM_B1) * g
        nv = ADAM_B2 * v_ref[...] + (1.0 - ADAM_B2) * (g * g)
        g_ref[...] = g
        nm_ref[...] = nm
        nv_ref[...] = nv
        d_ref[...] = (-ADAM_LR) * ((nm / c1) / (jnp.sqrt(nv / c2) + ADAM_EPS) + ADAM_WD * w_ref[...])

    spec = _rows(tr, c)
    gspec = spec if ng == 1 else pl.BlockSpec((N_CHIPS, tr, c), lambda i: (0, i, 0))
    return pl.pallas_call(
        body, name=name, grid=(r // tr,),
        in_specs=[spec] * 3 + [gspec] * ng, out_specs=[spec] * 4,
        out_shape=[jax.ShapeDtypeStruct((r, c), F32)] * 4,
        compiler_params=_params(("parallel",)),
    )(w, m, v, *gs)


_PARAMS = (
    ("rel_bias", None), ("norm_mix_pre", None), ("norm_mix_post", None), ("w_in", 1), ("conv_rnn_w", 1),
    ("conv_rnn_b", None), ("w_rg_a", None), ("b_rg_a", None), ("w_rg_x", None), ("b_rg_x", None),
    ("lru_lambda", None), ("w_branch_rnn", 0), ("w_branch_att", 1), ("w_out", 0), ("norm_ffn_pre", None),
    ("norm_ffn_post", None), ("w_ffn_gate", 1), ("w_ffn_up", 1), ("conv_ffn_w", 1), ("conv_ffn_b", None),
    ("w_ffn_down", 0),
)
_SMALL = 65536


def _as2d(a):
    a = a[0] if a.shape[0] == 1 and a.ndim >= 3 else a
    return a.reshape(-1, a.shape[-1]) if a.ndim == 3 else a


def _pack(pieces, dtype):
    flat = jnp.concatenate([p.astype(dtype).reshape(-1) for p in pieces])
    unit = PACK_W * PACK_ROWS
    pad = (-flat.shape[0]) % unit
    flat = jnp.pad(flat, (0, pad))
    return flat.reshape(-1, PACK_W)


def _unpack(buf, shapes):
    flat = buf.reshape(-1)
    out, off = [], 0
    for shp in shapes:
        n = int(np.prod(shp))
        out.append(flat[off:off + n].reshape(shp))
        off += n
    return out


def _join(slots, ax):
    if ax == 0:
        return slots.reshape(-1, slots.shape[-1])
    return jnp.transpose(slots, (1, 0, 2)).reshape(slots.shape[1], -1)


def _cut(full, ax):
    if ax == 0:
        return full.reshape(N_CHIPS, -1, full.shape[-1])
    return jnp.transpose(full.reshape(full.shape[0], N_CHIPS, -1), (1, 0, 2))


def kernel(x, rel_bias, norm_mix_pre, norm_mix_post, w_in, conv_rnn_w, conv_rnn_b, w_rg_a, b_rg_a, w_rg_x, b_rg_x, lru_lambda, w_branch_rnn, w_branch_att, w_out, norm_ffn_pre, norm_ffn_post, w_ffn_gate, w_ffn_up, conv_ffn_w, conv_ffn_b, w_ffn_down, loss_target, m_rel_bias, m_norm_mix_pre, m_norm_mix_post, m_w_in, m_conv_rnn_w, m_conv_rnn_b, m_w_rg_a, m_b_rg_a, m_w_rg_x, m_b_rg_x, m_lru_lambda, m_w_branch_rnn, m_w_branch_att, m_w_out, m_norm_ffn_pre, m_norm_ffn_post, m_w_ffn_gate, m_w_ffn_up, m_conv_ffn_w, m_conv_ffn_b, m_w_ffn_down, v_rel_bias, v_norm_mix_pre, v_norm_mix_post, v_w_in, v_conv_rnn_w, v_conv_rnn_b, v_w_rg_a, v_b_rg_a, v_w_rg_x, v_b_rg_x, v_lru_lambda, v_w_branch_rnn, v_w_branch_att, v_w_out, v_norm_ffn_pre, v_norm_ffn_post, v_w_ffn_gate, v_w_ffn_up, v_conv_ffn_w, v_conv_ffn_b, v_w_ffn_down):
    args = dict(locals())
    names = [n for n, _ in _PARAMS]
    axis = dict(_PARAMS)
    w_loc = {n: args[n] for n in names}
    m_loc = {n: args["m_" + n] for n in names}
    v_loc = {n: args["v_" + n] for n in names}
    sharded = [n for n in names if axis[n] is not None]
    replicated = [n for n in names if axis[n] is None]

    big = [n for n in sharded if w_loc[n].size >= _SMALL]
    small_sharded = [n for n in sharded if n not in big]
    small = replicated + small_sharded

    first = ["w_in"] + small_sharded
    srcs = [_as2d(w_loc[n]).astype(BF16) if n in big else _as2d(w_loc[n]) for n in first]
    (gathered,) = exchange([_Gather(srcs)], "gather_first")
    p = {n: _join(a, axis[n]) for n, a in zip(first, gathered)}
    for n in replicated:
        p[n] = _as2d(w_loc[n])
    shards = {n: _as2d(w_loc[n]).astype(BF16) for n in big if n not in first}

    received, sibling, g_small, loss_part = _local_step(x, loss_target, p, shards)

    pack = _pack([g_small[n] for n in small], BF16)
    ((received["small"],),) = exchange([_Scatter([], [pack])], "scatter_small")
    late = [n for n in received if n not in sibling]
    (swapped,) = exchange([_Swap([received[n] for n in late])], "swap_last")
    sibling.update(zip(late, swapped))
    small_sum = sum_pairs(received["small"], sibling["small"], "sum_small")
    g_tot = dict(zip(small, _unpack(small_sum, [g_small[n].shape for n in small])))
    chip = 2 * lax.axis_index("x") + lax.axis_index("y")
    for n in small_sharded:
        size = g_tot[n].shape[axis[n]] // N_CHIPS
        g_tot[n] = lax.dynamic_slice_in_dim(g_tot[n], chip * size, size, axis=axis[n])

    out_g, out_d, out_m, out_v = {}, {}, {}, {}
    for i, n in enumerate(names):
        shp = w_loc[n].shape
        gs = (received[n], sibling[n]) if n in big else (g_tot[n],)
        g, d, nm, nv = adamw(_as2d(w_loc[n]), _as2d(m_loc[n]), _as2d(v_loc[n]), gs, "adamw_" + n)
        out_g[n], out_d[n], out_m[n], out_v[n] = (t.reshape(shp) for t in (g, d, nm, nv))

    d_model = x.shape[-1]
    loss = lax.psum(0.5 * jnp.sum(loss_part) / d_model, ("x", "y", "c"))
    grad_x = g_small["x"]
    return (loss, grad_x, *[out_g[n] for n in names], *[out_d[n] for n in names],
            *[out_m[n] for n in names], *[out_v[n] for n in names])


def _local_step(x, target, p, shards):
    axis = dict(_PARAMS)
    b, s, d = x.shape
    t = b * s
    rnn = p["b_rg_a"].shape[1]
    ffn = p["conv_ffn_b"].shape[1]
    nbk = rnn // p["w_rg_a"].shape[1]
    hkv = (p["w_in"].shape[1] - rnn - 2 * d) // (N_GROUPS + 2)
    h = hkv // HEAD_DIM
    nq = N_GROUPS * hkv

    x2 = x.reshape(t, d)
    tgt = target.reshape(t, d)
    w_in = p["w_in"]
    in_splits = (rnn, nq + 2 * hkv, 2 * d)
    wa = p["w_rg_a"].reshape(nbk, -1, p["w_rg_a"].shape[1]).astype(BF16)
    wx = p["w_rg_x"].reshape(nbk, -1, p["w_rg_x"].shape[1]).astype(BF16)
    cw_r, cb_r = p["conv_rnn_w"], p["conv_rnn_b"]
    cw_f, cb_f = p["conv_ffn_w"], p["conv_ffn_b"]

    masks, buckets = zip(*[_band(w_, r_) for w_, r_ in DILATED])
    bucket_f = jnp.asarray(np.where(np.stack(masks), np.stack(buckets), -1).astype(np.float32))
    rel_rows = jnp.pad(p["rel_bias"].T, ((0, 0), (0, 128 - REL_BUCKETS)))[:, None, :]
    biasm = bias_table(rel_rows, bucket_f, h, "bias_table")

    early = ["w_branch_rnn", "w_branch_att", "w_out"]
    hn1, (xr, qkv, gts), (got,) = norm_mm(x2, p["norm_mix_pre"], [w_in], [in_splits], "in_proj",
                                          ride=[_Gather([shards[n] for n in early])])
    p.update({n: _join(a, axis[n]) for n, a in zip(early, got)})
    xr3 = xr.reshape(b, s, rnn)
    y_rnn = rglru_fwd(xr3, cw_r, cb_r, wa, p["b_rg_a"], wx, p["b_rg_x"], p["lru_lambda"], "rglru_fwd")
    qkv3 = qkv.reshape(b, s, -1)
    ffn_ws = ["w_ffn_gate", "w_ffn_up", "w_ffn_down"]
    o_att, lse, (got,) = attn_fwd(qkv3, biasm, h, "attn_fwd", ride=[_Gather([shards[n] for n in ffn_ws])])
    p.update({n: _join(a, axis[n]) for n, a in zip(ffn_ws, got)})
    merged, br, ba = merge_fwd(y_rnn.reshape(t, rnn), o_att.reshape(t, hkv), gts, p["w_branch_rnn"],
                               p["w_branch_att"], "merge_fwd")
    mix, h1 = mm_norm_res(merged, p["w_out"], p["norm_mix_post"], x2, "out_proj")
    hn2, (gate_pre, up), _ = norm_mm(h1, p["norm_ffn_pre"], [p["w_ffn_gate"], p["w_ffn_up"]], [(ffn,), (ffn,)], "ffn_in")
    act = ffn_act(gate_pre.reshape(b, s, ffn), up.reshape(b, s, ffn), cw_f, cb_f, "ffn_act")
    ff, y = mm_norm_res(act.reshape(t, ffn), p["w_ffn_down"], p["norm_ffn_post"], h1, "ffn_down")

    g, gb = {}, {}
    recv, sib = {}, {}

    def rows4(a):
        return a.reshape(N_CHIPS, -1, a.shape[-1])

    dy, dff, g["norm_ffn_post"], loss_part = loss_norm_bwd(y, tgt, ff, p["norm_ffn_post"], "loss_bwd")
    dact = mm_nt([([dff], p["w_ffn_down"])], F32, "ffn_down_dx")
    gb["w_ffn_down"] = rows4(mm_tn(act.reshape(t, ffn), [dff], "ffn_down_dw"))
    dgp, dup, g["conv_ffn_w"], g["conv_ffn_b"] = ffn_bwd(dact.reshape(b, s, ffn), gate_pre.reshape(b, s, ffn),
                                                        up.reshape(b, s, ffn), cw_f, cb_f, "ffn_bwd")
    dgp, dup = dgp.reshape(t, ffn), dup.reshape(t, ffn)
    dhn2, ((recv["w_ffn_down"],),) = mm_nt([([dgp], p["w_ffn_gate"]), ([dup], p["w_ffn_up"])], F32, "ffn_in_dx",
                                           ride=[_Scatter([gb["w_ffn_down"]])])
    gb["w_ffn_gate"] = mm_tn(hn2, [dgp], "ffn_gate_dw", col_shards=N_CHIPS)
    gb["w_ffn_up"] = mm_tn(hn2, [dup], "ffn_up_dw", col_shards=N_CHIPS)
    dh1, g["norm_ffn_pre"] = norm_bwd(dhn2, h1, p["norm_ffn_pre"], dy, F32, "ffn_norm_bwd")
    dmix, g["norm_mix_post"] = norm_bwd(dh1, mix, p["norm_mix_post"], None, BF16, "mix_norm_bwd")
    dmerged = mm_nt([([dmix], p["w_out"])], F32, "out_proj_dx")
    gb["w_out"] = rows4(mm_tn(merged, [dmix], "out_proj_dw"))
    dbr, dba, dgts = merge_bwd(dmerged, gts, br, ba, "merge_bwd")
    dy_rnn = mm_nt([([dbr], p["w_branch_rnn"])], F32, "branch_rnn_dx")
    do_att = mm_nt([([dba], p["w_branch_att"])], F32, "branch_att_dx")
    gb["w_branch_rnn"] = rows4(mm_tn(y_rnn.reshape(t, rnn), [dbr], "branch_rnn_dw"))
    gb["w_branch_att"] = mm_tn(o_att.reshape(t, hkv), [dba], "branch_att_dw", col_shards=N_CHIPS)
    (dxr, g["conv_rnn_w"], g["conv_rnn_b"], dwa, g["b_rg_a"], dwx, g["b_rg_x"], g["lru_lambda"]) = rglru_bwd(
        xr3, y_rnn, dy_rnn.reshape(b, s, rnn), cw_r, cb_r, wa, p["b_rg_a"], wx, p["b_rg_x"], p["lru_lambda"], "rglru_bwd")
    g["w_rg_a"] = dwa.reshape(p["w_rg_a"].shape)
    g["w_rg_x"] = dwx.reshape(p["w_rg_x"].shape)
    mid = ["w_ffn_gate", "w_ffn_up", "w_out", "w_branch_rnn", "w_branch_att"]
    (dq1, dq2, dq3, dk, dv, ds_sum), (got, (sib["w_ffn_down"],)) = attn_bwd(
        qkv3, biasm, o_att, lse, do_att.reshape(b, s, hkv), h, "attn_bwd",
        ride=[_Scatter([gb[n] for n in mid]), _Swap([recv["w_ffn_down"]])])
    recv.update(zip(mid, got))
    rows = bias_grad(ds_sum, bucket_f, "bias_grad")
    g["rel_bias"] = rows[:, 0, :REL_BUCKETS].T
    dproj = [dxr.reshape(t, rnn)] + [a.reshape(t, hkv) for a in (dq1, dq2, dq3, dk, dv)] + [dgts]
    dw_a = mm_tn(hn1, dproj[:4], "in_proj_dw_a")[0]
    dw_b = mm_tn(hn1, dproj[4:], "in_proj_dw_b")[0]
    gb["w_in"] = _cut(jnp.concatenate([dw_a, dw_b], axis=1), 1)
    dhn1, ((recv["w_in"],), got) = mm_nt([(dproj, w_in)], F32, "in_proj_dx",
                                         ride=[_Scatter([gb["w_in"]]), _Swap([recv[n] for n in mid])])
    sib.update(zip(mid, got))
    dx, g["norm_mix_pre"] = norm_bwd(dhn1, x2, p["norm_mix_pre"], dh1, F32, "in_norm_bwd")
    g["x"] = dx.reshape(b, s, d)
    return recv, sib, g, loss_part
```

```python
import functools
import math

import numpy as np
import jax
import jax.numpy as jnp
from jax import lax
from jax.experimental import pallas as pl
from jax.experimental.pallas import tpu as pltpu

F32 = jnp.float32
BF16 = jnp.bfloat16

EPS = 1e-6
HEAD_DIM = 128
ATTN_BLOCK = 128
DILATED = ((128, 1), (512, 4), (2048, 16))
N_GROUPS = len(DILATED)
REL_BUCKETS = 32
REL_MAX_DIST = 2048
LRU_C = 8.0
NEG = -1e30

ADAM_LR = 0.001
ADAM_B1 = 0.9
ADAM_B2 = 0.999
ADAM_EPS = 1e-08
ADAM_WD = 0.01
ADAM_STEP = 10

N_CHIPS = 4
PACK_W = 1024
PACK_ROWS = 16
VMEM_LIMIT = 56 * 1024 * 1024
MESH = pl.DeviceIdType.MESH


def _params(sem=None):
    return pltpu.CompilerParams(dimension_semantics=sem, vmem_limit_bytes=VMEM_LIMIT)


def _dot(a, b):
    return jnp.dot(a, b, preferred_element_type=F32)


def _dot_nt(a, b):
    return lax.dot_general(a, b, (((1,), (1,)), ((), ())), preferred_element_type=F32)


def _dot_tn(a, b):
    return lax.dot_general(a, b, (((0,), (0,)), ((), ())), preferred_element_type=F32)


def _sig(x):
    return 0.5 * jnp.tanh(0.5 * x) + 0.5


def _rows(tm, w):
    return pl.BlockSpec((tm, w), lambda i: (i, 0))


def _whole(shape):
    nd = len(shape)
    return pl.BlockSpec(tuple(shape), lambda *_: (0,) * nd)


def _tile(t, want):
    while t % want:
        want //= 2
    return want


def norm_mm(x, g, ws, splits, name, ride=(), tm=256):
    t, d = x.shape
    tm = _tile(t, tm)
    nw = len(ws)
    widths = [n for sp in splits for n in sp]

    def body(x_ref, g_ref, *refs):
        w_refs, hn_ref, o_refs = refs[:nw], refs[nw], refs[nw + 1:]
        xv = x_ref[...]
        inv = lax.rsqrt(jnp.mean(xv * xv, axis=-1, keepdims=True) + EPS)
        hn = (xv * inv * g_ref[...]).astype(BF16)
        hn_ref[...] = hn
        o = 0
        for w_ref, sp in zip(w_refs, splits):
            off = 0
            for n in sp:
                o_refs[o][...] = _dot(hn, w_ref[:, off:off + n])
                off += n
                o += 1

    r_ins, r_in_specs, r_outs, r_out_specs, r_sems = _ride_args(ride)
    n_out = 1 + len(widths)
    outs = pl.pallas_call(
        _riding(body, 2 + nw, n_out, 0, ride, 1), name=name, grid=(t // tm,),
        in_specs=[_rows(tm, d), _whole(g.shape)] + [_whole(w.shape) for w in ws] + r_in_specs,
        out_specs=[_rows(tm, d)] + [_rows(tm, n) for n in widths] + r_out_specs,
        out_shape=[jax.ShapeDtypeStruct((t, d), BF16)] + [jax.ShapeDtypeStruct((t, n), F32) for n in widths] + r_outs,
        scratch_shapes=r_sems,
        compiler_params=_params(("arbitrary",)),
    )(x, g, *ws, *r_ins)
    return outs[0], outs[1:n_out], _ride_results(ride, outs[n_out:])


def mm_nt(groups, out_dtype, name, ride=(), tm=256):
    dys_all = [dy for dys, _ in groups for dy in dys]
    ws = [w for _, w in groups]
    t = dys_all[0].shape[0]
    k = ws[0].shape[0]
    tm = _tile(t, tm)
    n = len(dys_all)

    def body(*refs):
        dy_refs, w_refs, o_ref = refs[:n], refs[n:n + len(ws)], refs[n + len(ws)]
        acc = None
        i = 0
        for (dys, _), w_ref in zip(groups, w_refs):
            off = 0
            for dy in dys:
                width = dy.shape[1]
                part = _dot_nt(dy_refs[i][...].astype(BF16), w_ref[:, off:off + width])
                acc = part if acc is None else acc + part
                off += width
                i += 1
        o_ref[...] = acc.astype(o_ref.dtype)

    r_ins, r_in_specs, r_outs, r_out_specs, r_sems = _ride_args(ride)
    outs = pl.pallas_call(
        _riding(body, n + len(ws), 1, 0, ride, 1), name=name, grid=(t // tm,),
        in_specs=[_rows(tm, dy.shape[1]) for dy in dys_all] + [_whole(w.shape) for w in ws] + r_in_specs,
        out_specs=[_rows(tm, k)] + r_out_specs,
        out_shape=[jax.ShapeDtypeStruct((t, k), out_dtype)] + r_outs,
        scratch_shapes=r_sems,
        compiler_params=_params(("arbitrary",) if ride else ("parallel",)),
    )(*dys_all, *ws, *r_ins)
    return (outs[0], _ride_results(ride, outs[1:])) if ride else outs[0]


def mm_tn(a, dys, name, col_shards=1, tm=512):
    t, k = a.shape
    tm = _tile(t, tm)
    n = len(dys)
    ntot = sum(dy.shape[1] for dy in dys)
    wsh = ntot // col_shards

    def body(a_ref, *refs):
        dy_refs, o_ref, acc = refs[:n], refs[n], refs[n + 1]

        @pl.when(pl.program_id(0) == 0)
        def _():
            acc[...] = jnp.zeros(acc.shape, F32)

        av = a_ref[...].astype(BF16)
        off = 0
        for dy_ref in dy_refs:
            width = dy_ref.shape[1]
            acc[:, off:off + width] += _dot_tn(av, dy_ref[...].astype(BF16))
            off += width

        @pl.when(pl.program_id(0) == pl.num_programs(0) - 1)
        def _():
            for j in range(col_shards):
                o_ref[j] = acc[:, j * wsh:(j + 1) * wsh].astype(o_ref.dtype)

    return pl.pallas_call(
        body, name=name, grid=(t // tm,),
        in_specs=[_rows(tm, k)] + [_rows(tm, dy.shape[1]) for dy in dys],
        out_specs=_whole((col_shards, k, wsh)),
        out_shape=jax.ShapeDtypeStruct((col_shards, k, wsh), BF16),
        scratch_shapes=[pltpu.VMEM((k, ntot), F32)],
        compiler_params=_params(("arbitrary",)),
    )(a, *dys)


def mm_norm_res(a, w, g, resid, name, tm=256):
    t, k = a.shape
    d = w.shape[1]
    tm = _tile(t, tm)

    def body(a_ref, w_ref, g_ref, r_ref, p_ref, o_ref):
        prod = _dot(a_ref[...], w_ref[...])
        p_ref[...] = prod
        inv = lax.rsqrt(jnp.mean(prod * prod, axis=-1, keepdims=True) + EPS)
        o_ref[...] = r_ref[...] + prod * inv * g_ref[...]

    return pl.pallas_call(
        body, name=name, grid=(t // tm,),
        in_specs=[_rows(tm, k), _whole(w.shape), _whole(g.shape), _rows(tm, d)],
        out_specs=[_rows(tm, d), _rows(tm, d)],
        out_shape=[jax.ShapeDtypeStruct((t, d), F32)] * 2,
        compiler_params=_params(("parallel",)),
    )(a, w, g, resid)


def _rms_bwd(dz, u, g):
    d = u.shape[-1]
    inv = lax.rsqrt(jnp.mean(u * u, axis=-1, keepdims=True) + EPS)
    dzg = dz * g
    proj = jnp.sum(dzg * u, axis=-1, keepdims=True) * (1.0 / d)
    du = inv * (dzg - u * (inv * inv) * proj)
    dg_rows = dz * u * inv
    return du, dg_rows


def norm_bwd(dz, u, g, add, out_dtype, name, tm=256):
    t, d = u.shape
    tm = _tile(t, tm)
    has_add = add is not None

    def body(*refs):
        if has_add:
            dz_ref, u_ref, g_ref, add_ref, du_ref, dg_ref = refs
        else:
            dz_ref, u_ref, g_ref, du_ref, dg_ref = refs

        @pl.when(pl.program_id(0) == 0)
        def _():
            dg_ref[...] = jnp.zeros(dg_ref.shape, F32)

        du, dg_rows = _rms_bwd(dz_ref[...].astype(F32), u_ref[...], g_ref[...])
        if has_add:
            du = du + add_ref[...]
        du_ref[...] = du.astype(du_ref.dtype)
        dg_ref[...] += jnp.sum(dg_rows, axis=0, keepdims=True)

    ins = [dz, u, g] + ([add] if has_add else [])
    return pl.pallas_call(
        body, name=name, grid=(t // tm,),
        in_specs=[_rows(tm, d), _rows(tm, d), _whole(g.shape)] + ([_rows(tm, d)] if has_add else []),
        out_specs=[_rows(tm, d), _whole((1, d))],
        out_shape=[jax.ShapeDtypeStruct((t, d), out_dtype), jax.ShapeDtypeStruct((1, d), F32)],
        compiler_params=_params(("arbitrary",)),
    )(*ins)


def loss_norm_bwd(y, target, ff, g, name, tm=256):
    t, d = y.shape
    tm = _tile(t, tm)

    def body(y_ref, t_ref, ff_ref, g_ref, dy_ref, dff_ref, dg_ref, loss_ref):
        @pl.when(pl.program_id(0) == 0)
        def _():
            dg_ref[...] = jnp.zeros(dg_ref.shape, F32)
            loss_ref[...] = jnp.zeros(loss_ref.shape, F32)

        err = y_ref[...] - t_ref[...]
        loss_ref[...] += jnp.sum(err * err, axis=0, keepdims=True)
        dy = err * (1.0 / d)
        dy_ref[...] = dy
        du, dg_rows = _rms_bwd(dy, ff_ref[...], g_ref[...])
        dff_ref[...] = du.astype(dff_ref.dtype)
        dg_ref[...] += jnp.sum(dg_rows, axis=0, keepdims=True)

    return pl.pallas_call(
        body, name=name, grid=(t // tm,),
        in_specs=[_rows(tm, d), _rows(tm, d), _rows(tm, d), _whole(g.shape)],
        out_specs=[_rows(tm, d), _rows(tm, d), _whole((1, d)), _whole((1, d))],
        out_shape=[jax.ShapeDtypeStruct((t, d), F32), jax.ShapeDtypeStruct((t, d), BF16),
                   jax.ShapeDtypeStruct((1, d), F32), jax.ShapeDtypeStruct((1, d), F32)],
        compiler_params=_params(("arbitrary",)),
    )(y, target, ff, g)


def merge_fwd(y_rnn, o_att, gts, w_br, w_ba, name, tm=256):
    t = y_rnn.shape[0]
    d = w_br.shape[1]
    tm = _tile(t, tm)

    def body(y_ref, o_ref, g_ref, wbr_ref, wba_ref, m_ref, br_ref, ba_ref):
        br = _dot(y_ref[...].astype(BF16), wbr_ref[...])
        ba = _dot(o_ref[...].astype(BF16), wba_ref[...])
        gv = g_ref[...]
        m_ref[...] = (_sig(gv[:, :d]) * br + _sig(gv[:, d:]) * ba).astype(BF16)
        br_ref[...] = br
        ba_ref[...] = ba

    return pl.pallas_call(
        body, name=name, grid=(t // tm,),
        in_specs=[_rows(tm, y_rnn.shape[1]), _rows(tm, o_att.shape[1]), _rows(tm, 2 * d),
                  _whole(w_br.shape), _whole(w_ba.shape)],
        out_specs=[_rows(tm, d)] * 3,
        out_shape=[jax.ShapeDtypeStruct((t, d), BF16), jax.ShapeDtypeStruct((t, d), F32),
                   jax.ShapeDtypeStruct((t, d), F32)],
        compiler_params=_params(("parallel",)),
    )(y_rnn, o_att, gts, w_br, w_ba)


def merge_bwd(dmerged, gts, br, ba, name, tm=256):
    t, d = dmerged.shape
    tm = _tile(t, tm)

    def body(dm_ref, g_ref, br_ref, ba_ref, dbr_ref, dba_ref, dg_ref):
        dm = dm_ref[...]
        gv = g_ref[...]
        sr = _sig(gv[:, :d])
        sa = _sig(gv[:, d:])
        dbr_ref[...] = (dm * sr).astype(BF16)
        dba_ref[...] = (dm * sa).astype(BF16)
        dg_ref[:, :d] = (dm * br_ref[...] * sr * (1.0 - sr)).astype(BF16)
        dg_ref[:, d:] = (dm * ba_ref[...] * sa * (1.0 - sa)).astype(BF16)

    return pl.pallas_call(
        body, name=name, grid=(t // tm,),
        in_specs=[_rows(tm, d), _rows(tm, 2 * d), _rows(tm, d), _rows(tm, d)],
        out_specs=[_rows(tm, d), _rows(tm, d), _rows(tm, 2 * d)],
        out_shape=[jax.ShapeDtypeStruct((t, d), BF16), jax.ShapeDtypeStruct((t, d), BF16),
                   jax.ShapeDtypeStruct((t, 2 * d), BF16)],
        compiler_params=_params(("parallel",)),
    )(dmerged, gts, br, ba)


def _shift_dn(x, d, fill, row):
    return jnp.where(row >= d, pltpu.roll(x, d, 0), fill)


def _shift_up(x, d, fill, row):
    s = x.shape[0]
    return jnp.where(row < s - d, pltpu.roll(x, s - d, 0), fill)


def _conv_fwd(x, w, b, row):
    kk = w.shape[0]
    y = b + w[kk - 1:kk, :] * x
    for j in range(1, kk):
        y = y + w[kk - 1 - j:kk - j, :] * _shift_dn(x, j, 0.0, row)
    return y


def _conv_bwd(dy, x, w, row):
    kk = w.shape[0]
    dx = w[kk - 1:kk, :] * dy
    dws = [None] * kk
    dws[kk - 1] = jnp.sum(dy * x, axis=0, keepdims=True)
    for j in range(1, kk):
        dx = dx + w[kk - 1 - j:kk - j, :] * _shift_up(dy, j, 0.0, row)
        dws[kk - 1 - j] = jnp.sum(dy * _shift_dn(x, j, 0.0, row), axis=0, keepdims=True)
    return dx, jnp.concatenate(dws, axis=0)


def _softplus(z):
    y = jnp.exp(-jnp.abs(z))
    u = 1.0 + y
    dd = u - 1.0
    log1p = jnp.where(dd == 0.0, y, jnp.log(u) * (y / jnp.where(dd == 0.0, 1.0, dd)))
    return jnp.maximum(z, 0.0) + log1p


def _lru_decay(xb, wa, ba, lam):
    r = _sig(_dot(xb, wa) + ba)
    sp = _softplus(-lam)
    la = (-LRU_C) * r * sp
    return r, sp, la, jnp.exp(la)


def _lru_gates(xc, wa, ba, wx, bx, lam):
    xb = xc.astype(BF16)
    r, sp, la, a = _lru_decay(xb, wa, ba, lam)
    i = _sig(_dot(xb, wx) + bx)
    one_m_a2 = jnp.tanh(-la) * (1.0 + a * a)
    inv_mult = lax.rsqrt(one_m_a2)
    return r, i, sp, a, one_m_a2 * inv_mult, inv_mult


def _seg_len(s):
    seg = -(-s // 8)
    return seg + (4 - seg % 8) % 8


def _scan_rows(a_pad, u_pad, out_pad, reverse):
    planes, rows8, lanes = a_pad.shape
    seg = rows8 // 8
    sub = lax.broadcasted_iota(jnp.int32, (planes, 8, lanes), 1)

    unroll = 4

    def rows(k, d):
        i = k * unroll + d
        return pl.ds((seg - 1 - i) if reverse else i, 8, stride=seg)

    def ends(k, carry):
        h, p = carry
        for d in range(unroll):
            a = a_pad[:, rows(k, d), :]
            h = a * h + u_pad[:, rows(k, d), :]
            p = a * p
        return h, p

    init = (jnp.zeros((planes, 8, lanes), F32), jnp.ones((planes, 8, lanes), F32))
    h_end, p_end = lax.fori_loop(0, seg // unroll, ends, init)
    start = jnp.zeros((planes, 8, lanes), F32)
    for _ in range(7):
        nxt = h_end + p_end * start
        if reverse:
            start = jnp.where(sub < 7, pltpu.roll(nxt, 7, 1), 0.0)
        else:
            start = jnp.where(sub >= 1, pltpu.roll(nxt, 1, 1), 0.0)

    def redo(k, h):
        for d in range(unroll):
            h = a_pad[:, rows(k, d), :] * h + u_pad[:, rows(k, d), :]
            out_pad[:, rows(k, d), :] = h
        return h

    lax.fori_loop(0, seg // unroll, redo, start)


def _lru_cols(c, rb):
    return 2 * rb if c % (2 * rb) == 0 else rb


def rglru_fwd(xr, cw, cb, wa, ba, wx, bx, lam, name, ride=()):
    b, s, c = xr.shape
    rb = wa.shape[1]
    kk = cw.shape[0]
    cols = _lru_cols(c, rb)
    nj = cols // rb
    seg = _seg_len(s)

    def body(x_ref, cw_ref, cb_ref, wa_ref, ba_ref, wx_ref, bx_ref, lam_ref, h_ref, a_pad, u_pad, h_pad):
        row = lax.broadcasted_iota(jnp.int32, (s, rb), 0)
        for j in range(nj):
            cs = slice(j * rb, (j + 1) * rb)
            xc = _conv_fwd(x_ref[:, cs], cw_ref[:, cs], cb_ref[:, cs], row)
            _, i, _, a, mult, _ = _lru_gates(xc, wa_ref[j], ba_ref[:, cs], wx_ref[j], bx_ref[:, cs], lam_ref[:, cs])
            a_pad[j, 0:s, :] = a
            u_pad[j, 0:s, :] = mult * (i * xc)
        a_pad[:, s:, :] = jnp.ones((nj, 8 * seg - s, rb), F32)
        u_pad[:, s:, :] = jnp.zeros((nj, 8 * seg - s, rb), F32)
        _scan_rows(a_pad, u_pad, h_pad, False)
        for j in range(nj):
            h_ref[:, j * rb:(j + 1) * rb] = h_pad[j, 0:s, :]

    vec = pl.BlockSpec((1, cols), lambda bi, n: (0, n))
    seq = pl.BlockSpec((None, s, cols), lambda bi, n: (bi, 0, n))
    mat = pl.BlockSpec((nj, rb, rb), lambda bi, n: (n, 0, 0))
    r_ins, r_in_specs, r_outs, r_out_specs, r_sems = _ride_args(ride)
    outs = pl.pallas_call(
        _riding(body, 8, 1, 3, ride, 2), name=name, grid=(b, c // cols),
        in_specs=[seq, pl.BlockSpec((kk, cols), lambda bi, n: (0, n)), vec, mat, vec, mat, vec, vec] + r_in_specs,
        out_specs=[seq] + r_out_specs,
        out_shape=[jax.ShapeDtypeStruct((b, s, c), F32)] + r_outs,
        scratch_shapes=[pltpu.VMEM((nj, 8 * seg, rb), F32)] * 3 + r_sems,
        compiler_params=_params(("arbitrary", "arbitrary")),
    )(xr, cw, cb, wa, ba, wx, bx, lam, *r_ins)
    return outs[0], _ride_results(ride, outs[1:])


def rglru_bwd(xr, h, dh, cw, cb, wa, ba, wx, bx, lam, name, ride=()):
    b, s, c = xr.shape
    nb, rb = wa.shape[0], wa.shape[1]
    kk = cw.shape[0]
    cols = _lru_cols(c, rb)
    nj = cols // rb
    seg = _seg_len(s)

    def body(x_ref, h_ref, dh_ref, cw_ref, cb_ref, wa_ref, ba_ref, wx_ref, bx_ref, lam_ref,
             dx_ref, dcw_ref, dcb_ref, dwa_ref, dba_ref, dwx_ref, dbx_ref, dlam_ref, b_pad, g_pad, l_pad):
        @pl.when(pl.program_id(1) == 0)
        def _():
            for ref in (dcw_ref, dcb_ref, dwa_ref, dba_ref, dwx_ref, dbx_ref, dlam_ref):
                ref[...] = jnp.zeros(ref.shape, F32)

        row = lax.broadcasted_iota(jnp.int32, (s, rb), 0)

        def conv(j):
            cs = slice(j * rb, (j + 1) * rb)
            return _conv_fwd(x_ref[:, cs], cw_ref[:, cs], cb_ref[:, cs], row)

        for j in range(nj):
            cs = slice(j * rb, (j + 1) * rb)
            _, _, _, a = _lru_decay(conv(j).astype(BF16), wa_ref[j], ba_ref[:, cs], lam_ref[:, cs])
            b_pad[j, 0:s, :] = _shift_up(a, 1, 0.0, row)
            g_pad[j, 0:s, :] = dh_ref[:, j * rb:(j + 1) * rb]
        b_pad[:, s:, :] = jnp.zeros((nj, 8 * seg - s, rb), F32)
        g_pad[:, s:, :] = jnp.zeros((nj, 8 * seg - s, rb), F32)
        _scan_rows(b_pad, g_pad, l_pad, True)

        for j in range(nj):
            cs = slice(j * rb, (j + 1) * rb)
            x = x_ref[:, cs]
            cwv = cw_ref[:, cs]
            wav, wxv, lamv = wa_ref[j], wx_ref[j], lam_ref[:, cs]
            xc = conv(j)
            r, i, sp, a, mult, inv_mult = _lru_gates(xc, wav, ba_ref[:, cs], wxv, bx_ref[:, cs], lamv)
            lmb = l_pad[j, 0:s, :]
            h_prev = _shift_dn(h_ref[:, cs], 1, 0.0, row)
            da = lmb * h_prev
            ixc = i * xc
            dla = da * a - (lmb * ixc) * (a * a) * inv_mult
            di = lmb * mult * xc
            dxc = lmb * mult * i
            dr = dla * ((-LRU_C) * sp)
            dsp = jnp.sum(dla * ((-LRU_C) * r), axis=0, keepdims=True)
            dga = dr * r * (1.0 - r)
            dgx = di * i * (1.0 - i)
            dga_b, dgx_b = dga.astype(BF16), dgx.astype(BF16)
            xb = xc.astype(BF16)
            dwa_ref[j] += _dot_tn(xb, dga_b)
            dwx_ref[j] += _dot_tn(xb, dgx_b)
            dba_ref[:, cs] += jnp.sum(dga, axis=0, keepdims=True)
            dbx_ref[:, cs] += jnp.sum(dgx, axis=0, keepdims=True)
            dlam_ref[:, cs] += dsp * (-_sig(-lamv))
            dxc = dxc + _dot_nt(dga_b, wav) + _dot_nt(dgx_b, wxv)
            dcb_ref[:, cs] += jnp.sum(dxc, axis=0, keepdims=True)
            dx, dcw = _conv_bwd(dxc, x, cwv, row)
            dcw_ref[:, cs] += dcw
            dx_ref[:, cs] = dx.astype(dx_ref.dtype)

    vec = pl.BlockSpec((1, cols), lambda n, bi: (0, n))
    seq = pl.BlockSpec((None, s, cols), lambda n, bi: (bi, 0, n))
    mat = pl.BlockSpec((nj, rb, rb), lambda n, bi: (n, 0, 0))
    cws = pl.BlockSpec((kk, cols), lambda n, bi: (0, n))
    sd = jax.ShapeDtypeStruct
    r_ins, r_in_specs, r_outs, r_out_specs, r_sems = _ride_args(ride)
    outs = pl.pallas_call(
        _riding(body, 10, 8, 3, ride, 2), name=name, grid=(c // cols, b),
        in_specs=[seq, seq, seq, cws, vec, mat, vec, mat, vec, vec] + r_in_specs,
        out_specs=[seq, cws, vec, mat, vec, mat, vec, vec] + r_out_specs,
        out_shape=[sd((b, s, c), BF16), sd((kk, c), F32), sd((1, c), F32), sd((nb, rb, rb), F32),
                   sd((1, c), F32), sd((nb, rb, rb), F32), sd((1, c), F32), sd((1, c), F32)] + r_outs,
        scratch_shapes=[pltpu.VMEM((nj, 8 * seg, rb), F32)] * 3 + r_sems,
        compiler_params=_params(("arbitrary", "arbitrary")),
    )(xr, h, dh, cw, cb, wa, ba, wx, bx, lam, *r_ins)
    return outs[:8], _ride_results(ride, outs[8:])


_GELU_C = math.sqrt(2.0 / math.pi)


def _gelu_parts(x):
    th = jnp.tanh(_GELU_C * (x + 0.044715 * x * x * x))
    gel = 0.5 * x * (1.0 + th)
    dgel = 0.5 * (1.0 + th) + 0.5 * x * (1.0 - th * th) * _GELU_C * (1.0 + 3 * 0.044715 * x * x)
    return gel, dgel


def ffn_act(gate_pre, up, cw, cb, name, cbk=256):
    b, s, f = gate_pre.shape
    kk = cw.shape[0]
    cbk = _tile(f, cbk)

    def body(g_ref, u_ref, cw_ref, cb_ref, a_ref):
        row = lax.broadcasted_iota(jnp.int32, (s, cbk), 0)
        gate = _conv_fwd(g_ref[...], cw_ref[...], cb_ref[...], row)
        gel, _ = _gelu_parts(gate)
        a_ref[...] = (gel * u_ref[...]).astype(BF16)

    seq = pl.BlockSpec((None, s, cbk), lambda bi, n: (bi, 0, n))
    return pl.pallas_call(
        body, name=name, grid=(b, f // cbk),
        in_specs=[seq, seq, pl.BlockSpec((kk, cbk), lambda bi, n: (0, n)), pl.BlockSpec((1, cbk), lambda bi, n: (0, n))],
        out_specs=seq,
        out_shape=jax.ShapeDtypeStruct((b, s, f), BF16),
        compiler_params=_params(("parallel", "parallel")),
    )(gate_pre, up, cw, cb)


def ffn_bwd(dact, gate_pre, up, cw, cb, name, cbk=256):
    b, s, f = gate_pre.shape
    kk = cw.shape[0]
    cbk = _tile(f, cbk)

    def body(da_ref, g_ref, u_ref, cw_ref, cb_ref, dg_ref, du_ref, dcw_ref, dcb_ref):
        @pl.when(pl.program_id(1) == 0)
        def _():
            dcw_ref[...] = jnp.zeros(dcw_ref.shape, F32)
            dcb_ref[...] = jnp.zeros(dcb_ref.shape, F32)

        row = lax.broadcasted_iota(jnp.int32, (s, cbk), 0)
        gp = g_ref[...]
        cwv = cw_ref[...]
        gate = _conv_fwd(gp, cwv, cb_ref[...], row)
        gel, dgel = _gelu_parts(gate)
        da = da_ref[...]
        du_ref[...] = (da * gel).astype(BF16)
        dgate = da * u_ref[...] * dgel
        dcb_ref[...] += jnp.sum(dgate, axis=0, keepdims=True)
        dgp, dcw = _conv_bwd(dgate, gp, cwv, row)
        dcw_ref[...] += dcw
        dg_ref[...] = dgp.astype(BF16)

    seq = pl.BlockSpec((None, s, cbk), lambda n, bi: (bi, 0, n))
    cws = pl.BlockSpec((kk, cbk), lambda n, bi: (0, n))
    vec = pl.BlockSpec((1, cbk), lambda n, bi: (0, n))
    sd = jax.ShapeDtypeStruct
    return pl.pallas_call(
        body, name=name, grid=(f // cbk, b),
        in_specs=[seq, seq, seq, cws, vec],
        out_specs=[seq, seq, cws, vec],
        out_shape=[sd((b, s, f), BF16), sd((b, s, f), BF16), sd((kk, f), F32), sd((1, f), F32)],
        compiler_params=_params(("parallel", "arbitrary")),
    )(dact, gate_pre, up, cw, cb)


def _t5_bucket(dist):
    max_exact = REL_BUCKETS // 2
    d = np.maximum(dist, 1).astype(np.float32)
    large = max_exact + np.log(d / max_exact) / math.log(REL_MAX_DIST / max_exact) * (REL_BUCKETS - max_exact)
    large = np.minimum(large.astype(np.int32), REL_BUCKETS - 1)
    return np.where(dist < max_exact, dist, large).astype(np.int32)


def _band(window, dilation):
    qi = np.arange(ATTN_BLOCK)[:, None]
    kj = np.arange(2 * ATTN_BLOCK)[None, :]
    delta = ATTN_BLOCK + qi - kj
    mask = (delta >= 0) & (delta <= window // dilation)
    bucket = _t5_bucket(np.maximum(delta, 0) * dilation)
    return mask, bucket


def _attn_blocks(s, r):
    m = s // r
    assert m % ATTN_BLOCK == 0, "sequence length must be a multiple of dilation * block"
    return m // ATTN_BLOCK


def _perm_load(ref, r):
    if r == 1:
        return ref[...]
    m = ref.shape[0] // r
    return jnp.concatenate([ref[pl.ds(c, m, stride=r), :] for c in range(r)], axis=0)


def _perm_store(ref, g, val, r, add=False):
    if r == 1:
        ref[g] = ref[g] + val if add else val
        return
    m = val.shape[0] // r
    for c in range(r):
        rows = pl.ds(c, m, stride=r)
        part = val[c * m:(c + 1) * m]
        ref[g, rows, :] = ref[g, rows, :] + part if add else part


def _blocks(x):
    return x.reshape(x.shape[0] // ATTN_BLOCK, ATTN_BLOCK, x.shape[1])


def _prev_blocks(x):
    return jnp.concatenate([x[:1], x[:-1]], axis=0)


def _next_blocks(x):
    return jnp.concatenate([x[1:], jnp.zeros_like(x[:1])], axis=0)


def _first_block_neg(s, r):
    nblk = s // ATTN_BLOCK
    idx = lax.broadcasted_iota(jnp.int32, (nblk, 1, 1), 0)
    return jnp.where(idx % _attn_blocks(s, r) == 0, NEG, 0.0)


def _bdot_nt(a, b):
    return lax.dot_general(a, b, (((2,), (2,)), ((0,), (0,))), preferred_element_type=F32)


def _bdot(a, b):
    return lax.dot_general(a, b, (((2,), (1,)), ((0,), (0,))), preferred_element_type=F32)


def _bdot_tn(a, b):
    return lax.dot_general(a, b, (((1,), (1,)), ((0,), (0,))), preferred_element_type=F32)


def attn_fwd(qkv, biasm, n_heads, name, ride=()):
    b, s, _ = qkv.shape
    h = n_heads
    scale = HEAD_DIM ** -0.5
    blk = ATTN_BLOCK

    def body(q1_ref, q2_ref, q3_ref, k_ref, v_ref, bias_ref, o_ref, lse_ref, acc, m_s, l_s):
        for g, q_ref in enumerate((q1_ref, q2_ref, q3_ref)):
            r = DILATED[g][1]
            first = _first_block_neg(s, r)
            q = _blocks(_perm_load(q_ref, r).astype(BF16))
            k = _blocks(_perm_load(k_ref, r).astype(BF16))
            v = _blocks(_perm_load(v_ref, r).astype(BF16))
            s_cur = _bdot_nt(q, k) * scale + bias_ref[g, :, blk:]
            s_prev = _bdot_nt(q, _prev_blocks(k)) * scale + bias_ref[g, :, :blk] + first
            m = jnp.maximum(jnp.max(s_cur, axis=-1, keepdims=True), jnp.max(s_prev, axis=-1, keepdims=True))
            p_cur = jnp.exp(s_cur - m)
            p_prev = jnp.exp(s_prev - m)
            l = jnp.sum(p_cur, axis=-1, keepdims=True) + jnp.sum(p_prev, axis=-1, keepdims=True)
            o = _bdot(p_cur.astype(BF16), v) + _bdot(p_prev.astype(BF16), _prev_blocks(v))
            _perm_store(acc, g, o.reshape(s, HEAD_DIM), r)
            _perm_store(m_s, g, m.reshape(s, 1), r)
            _perm_store(l_s, g, l.reshape(s, 1), r)
        m_all = jnp.maximum(jnp.maximum(m_s[0], m_s[1]), m_s[2])
        w = [jnp.exp(m_s[g] - m_all) for g in range(N_GROUPS)]
        l = w[0] * l_s[0] + w[1] * l_s[1] + w[2] * l_s[2]
        o_ref[...] = (w[0] * acc[0] + w[1] * acc[1] + w[2] * acc[2]) / l
        lse_ref[...] = m_all + jnp.log(l)

    def col(j):
        return pl.BlockSpec((None, s, HEAD_DIM), lambda bi, hi, j=j: (bi, 0, j * h + hi))

    r_ins, r_in_specs, r_outs, r_out_specs, r_sems = _ride_args(ride)
    outs = pl.pallas_call(
        _riding(body, 6, 2, 3, ride, 2), name=name, grid=(b, h),
        in_specs=[col(0), col(1), col(2), col(3), col(4),
                  pl.BlockSpec((N_GROUPS, None, blk, 2 * blk), lambda bi, hi: (0, hi, 0, 0))] + r_in_specs,
        out_specs=[pl.BlockSpec((None, s, HEAD_DIM), lambda bi, hi: (bi, 0, hi)),
                   pl.BlockSpec((None, None, s, 1), lambda bi, hi: (bi, hi, 0, 0))] + r_out_specs,
        out_shape=[jax.ShapeDtypeStruct((b, s, h * HEAD_DIM), F32), jax.ShapeDtypeStruct((b, h, s, 1), F32)] + r_outs,
        scratch_shapes=[pltpu.VMEM((N_GROUPS, s, HEAD_DIM), F32), pltpu.VMEM((N_GROUPS, s, 1), F32),
                        pltpu.VMEM((N_GROUPS, s, 1), F32)] + r_sems,
        compiler_params=_params(("arbitrary", "arbitrary")),
    )(qkv, qkv, qkv, qkv, qkv, biasm, *r_ins)
    return outs[0], outs[1], _ride_results(ride, outs[2:])


def attn_bwd(qkv, biasm, o, lse, do, n_heads, name, ride=()):
    b, s, _ = qkv.shape
    h = n_heads
    scale = HEAD_DIM ** -0.5
    blk = ATTN_BLOCK

    def body(q1_ref, q2_ref, q3_ref, k_ref, v_ref, bias_ref, o_ref, lse_ref, do_ref,
             dq1_ref, dq2_ref, dq3_ref, dk_ref, dv_ref, ds_ref, dq_acc, kv_acc, delta):
        delta[...] = jnp.sum(do_ref[...] * o_ref[...], axis=-1, keepdims=True)
        kv_acc[...] = jnp.zeros(kv_acc.shape, F32)
        for g, q_ref in enumerate((q1_ref, q2_ref, q3_ref)):
            r = DILATED[g][1]
            first = _first_block_neg(s, r)
            q = _blocks(_perm_load(q_ref, r).astype(BF16))
            k = _blocks(_perm_load(k_ref, r).astype(BF16))
            v = _blocks(_perm_load(v_ref, r).astype(BF16))
            dob = _blocks(_perm_load(do_ref, r).astype(BF16))
            lse_b = _blocks(_perm_load(lse_ref, r))
            dl_b = _blocks(_perm_load(delta, r))
            k_prev, v_prev = _prev_blocks(k), _prev_blocks(v)
            p_cur = jnp.exp(_bdot_nt(q, k) * scale + bias_ref[g, :, blk:] - lse_b)
            p_prev = jnp.exp(_bdot_nt(q, k_prev) * scale + bias_ref[g, :, :blk] + first - lse_b)
            ds_cur = p_cur * (_bdot_nt(dob, v) - dl_b)
            ds_prev = p_prev * (_bdot_nt(dob, v_prev) - dl_b)
            ds_ref[g, :, blk:] = jnp.sum(ds_cur, axis=0)
            ds_ref[g, :, :blk] = jnp.sum(ds_prev, axis=0)
            ds_cur_b, ds_prev_b = ds_cur.astype(BF16), ds_prev.astype(BF16)
            dq = (_bdot(ds_cur_b, k) + _bdot(ds_prev_b, k_prev)) * scale
            _perm_store(dq_acc, g, dq.reshape(s, HEAD_DIM), r)
            dk = (_bdot_tn(ds_cur_b, q) + _next_blocks(_bdot_tn(ds_prev_b, q))) * scale
            dv = _bdot_tn(p_cur.astype(BF16), dob) + _next_blocks(_bdot_tn(p_prev.astype(BF16), dob))
            _perm_store(kv_acc, 0, dk.reshape(s, HEAD_DIM), r, add=True)
            _perm_store(kv_acc, 1, dv.reshape(s, HEAD_DIM), r, add=True)
        for g, out_ref in enumerate((dq1_ref, dq2_ref, dq3_ref)):
            out_ref[...] = dq_acc[g].astype(out_ref.dtype)
        dk_ref[...] = kv_acc[0].astype(dk_ref.dtype)
        dv_ref[...] = kv_acc[1].astype(dv_ref.dtype)

    def col(j):
        return pl.BlockSpec((None, s, HEAD_DIM), lambda bi, hi, j=j: (bi, 0, j * h + hi))

    head = pl.BlockSpec((None, s, HEAD_DIM), lambda bi, hi: (bi, 0, hi))
    sd = jax.ShapeDtypeStruct
    r_ins, r_in_specs, r_outs, r_out_specs, r_sems = _ride_args(ride)
    outs = pl.pallas_call(
        _riding(body, 9, 6, 3, ride, 2), name=name, grid=(b, h),
        in_specs=[col(0), col(1), col(2), col(3), col(4),
                  pl.BlockSpec((N_GROUPS, None, blk, 2 * blk), lambda bi, hi: (0, hi, 0, 0)),
                  head, pl.BlockSpec((None, None, s, 1), lambda bi, hi: (bi, hi, 0, 0)), head] + r_in_specs,
        out_specs=[head] * 5 + [pl.BlockSpec((None, None, N_GROUPS, blk, 2 * blk), lambda bi, hi: (bi, hi, 0, 0, 0))]
        + r_out_specs,
        out_shape=[sd((b, s, h * HEAD_DIM), BF16)] * 5 + [sd((b, h, N_GROUPS, blk, 2 * blk), F32)] + r_outs,
        scratch_shapes=[pltpu.VMEM((N_GROUPS, s, HEAD_DIM), F32), pltpu.VMEM((2, s, HEAD_DIM), F32),
                        pltpu.VMEM((s, 1), F32)] + r_sems,
        compiler_params=_params(("arbitrary", "arbitrary")),
    )(qkv, qkv, qkv, qkv, qkv, biasm, o, lse, do, *r_ins)
    return outs[:6], _ride_results(ride, outs[6:])


def bias_table(rel_rows, bucket_f, n_heads, name):
    g, blk, blk2 = bucket_f.shape
    h = n_heads

    def body(rb_ref, bk_ref, o_ref):
        bk = bk_ref[...]
        rb = rb_ref[...]
        acc = jnp.full((blk, blk2), NEG, F32)
        for bucket in range(REL_BUCKETS):
            acc = jnp.where(bk == float(bucket), rb[:, bucket:bucket + 1], acc)
        o_ref[...] = acc

    return pl.pallas_call(
        body, name=name, grid=(g, h),
        in_specs=[pl.BlockSpec((None, 1, 128), lambda gi, hi: (gi * h + hi, 0, 0)),
                  pl.BlockSpec((None, blk, blk2), lambda gi, hi: (gi, 0, 0))],
        out_specs=pl.BlockSpec((None, None, blk, blk2), lambda gi, hi: (gi, hi, 0, 0)),
        out_shape=jax.ShapeDtypeStruct((g, h, blk, blk2), F32),
        compiler_params=_params(("parallel", "parallel")),
    )(rel_rows, bucket_f)


def bias_grad(ds_sum, bucket_f, name):
    b, h, g, blk, blk2 = ds_sum.shape

    def body(ds_ref, bk_ref, o_ref):
        tot = jnp.sum(ds_ref[...], axis=0)
        bk = bk_ref[...]
        lane = lax.broadcasted_iota(jnp.int32, (1, 128), 1)
        vec = jnp.zeros((1, 128), F32)
        for bucket in range(REL_BUCKETS):
            val = jnp.sum(jnp.where(bk == float(bucket), tot, 0.0), keepdims=True)
            vec = vec + jnp.where(lane == bucket, val, 0.0)
        o_ref[...] = vec

    return pl.pallas_call(
        body, name=name, grid=(g, h),
        in_specs=[pl.BlockSpec((b, None, None, blk, blk2), lambda gi, hi: (0, hi, gi, 0, 0)),
                  pl.BlockSpec((None, blk, blk2), lambda gi, hi: (gi, 0, 0))],
        out_specs=pl.BlockSpec((None, 1, 128), lambda gi, hi: (gi * h + hi, 0, 0)),
        out_shape=jax.ShapeDtypeStruct((g * h, 1, 128), F32),
        compiler_params=_params(("parallel", "parallel")),
    )(ds_sum, bucket_f)


def _chip_peers():
    x, y, c = lax.axis_index("x"), lax.axis_index("y"), lax.axis_index("c")
    me = 2 * x + y
    peers = [(1 - x, y, c), (x, 1 - y, c), (1 - x, 1 - y, c)]
    peer_chip = [2 * (1 - x) + y, 2 * x + (1 - y), 2 * (1 - x) + (1 - y)]
    return me, peers, peer_chip


def _any_specs(n):
    return [pl.BlockSpec(memory_space=pl.ANY)] * n


class _Exchange:
    def start(self, ins, outs, sems):
        local, sends, _ = self._copies(ins, outs, sems)
        for cp in local + sends:
            cp.start()

    def wait(self, ins, outs, sems):
        local, sends, recvs = self._copies(ins, outs, sems)
        for cp in recvs():
            cp.wait_recv()
        for cp in sends:
            cp.wait_send()
        for cp in local:
            cp.wait()


class _Gather(_Exchange):
    def __init__(self, arrays):
        n = len(arrays)
        self.ins = list(arrays)
        self.out_shape = [jax.ShapeDtypeStruct((N_CHIPS,) + a.shape, a.dtype) for a in arrays]
        self.sems = [pltpu.SemaphoreType.DMA((3 * n,)), pltpu.SemaphoreType.DMA((3 * n,)), pltpu.SemaphoreType.DMA((n,))]

    def _copies(self, ins, outs, sems):
        send_sems, recv_sems, local_sems = sems
        me, peers, peer_chip = _chip_peers()
        n = len(ins)

        def remote(i, k, slot):
            return pltpu.make_async_remote_copy(src_ref=ins[i], dst_ref=outs[i].at[slot],
                                                send_sem=send_sems.at[3 * i + k], recv_sem=recv_sems.at[3 * i + k],
                                                device_id=peers[k], device_id_type=MESH)

        local = [pltpu.make_async_copy(ins[i], outs[i].at[me], local_sems.at[i]) for i in range(n)]
        sends = [remote(i, k, me) for i in range(n) for k in range(3)]
        return local, sends, lambda: [remote(i, k, peer_chip[k]) for i in range(n) for k in range(3)]


class _Scatter(_Exchange):
    def __init__(self, slabs, whole=()):
        self.n_slabs = len(slabs)
        self.ins = list(slabs) + list(whole)
        n = len(self.ins)
        self.out_shape = [jax.ShapeDtypeStruct(a.shape, a.dtype) for a in slabs] \
            + [jax.ShapeDtypeStruct((N_CHIPS,) + a.shape, a.dtype) for a in whole]
        self.sems = [pltpu.SemaphoreType.DMA((3 * n,)), pltpu.SemaphoreType.DMA((3 * n,)), pltpu.SemaphoreType.DMA((n,))]

    def _copies(self, ins, outs, sems):
        send_sems, recv_sems, local_sems = sems
        me, peers, peer_chip = _chip_peers()
        n = len(ins)

        def src(i, chip):
            return ins[i].at[chip] if i < self.n_slabs else ins[i]

        def remote(i, k, src_chip, slot):
            return pltpu.make_async_remote_copy(src_ref=src(i, src_chip), dst_ref=outs[i].at[slot],
                                                send_sem=send_sems.at[3 * i + k], recv_sem=recv_sems.at[3 * i + k],
                                                device_id=peers[k], device_id_type=MESH)

        local = [pltpu.make_async_copy(src(i, me), outs[i].at[me], local_sems.at[i]) for i in range(n)]
        sends = [remote(i, k, peer_chip[k], me) for i in range(n) for k in range(3)]
        return local, sends, lambda: [remote(i, k, me, peer_chip[k]) for i in range(n) for k in range(3)]


class _Swap(_Exchange):
    def __init__(self, arrays):
        n = len(arrays)
        self.ins = list(arrays)
        self.out_shape = [jax.ShapeDtypeStruct(a.shape, a.dtype) for a in arrays]
        self.sems = [pltpu.SemaphoreType.DMA((n,)), pltpu.SemaphoreType.DMA((n,))]

    def _copies(self, ins, outs, sems):
        send_sems, recv_sems = sems
        x, y, c = lax.axis_index("x"), lax.axis_index("y"), lax.axis_index("c")
        cps = [pltpu.make_async_remote_copy(src_ref=ins[i], dst_ref=outs[i], send_sem=send_sems.at[i],
                                            recv_sem=recv_sems.at[i], device_id=(x, y, 1 - c), device_id_type=MESH)
               for i in range(len(ins))]
        return [], cps, lambda: cps


def _riding(body, n_in, n_out, n_scratch, ride, rank):
    if not ride:
        return body
    r_in = sum(len(e.ins) for e in ride)
    r_out = sum(len(e.out_shape) for e in ride)

    def split(refs, sizes):
        out, a = [], 0
        for sz in sizes:
            out.append(refs[a:a + sz])
            a += sz
        return out

    def wrapped(*refs):
        a = 0
        parts = []
        for sz in (n_in, r_in, n_out, r_out, n_scratch):
            parts.append(refs[a:a + sz])
            a += sz
        own_in, ex_in, own_out, ex_out, own_scratch = parts
        ex_sems = refs[a:]
        ins = split(ex_in, [len(e.ins) for e in ride])
        outs = split(ex_out, [len(e.out_shape) for e in ride])
        sems = split(ex_sems, [len(e.sems) for e in ride])
        if rank:
            first = functools.reduce(jnp.logical_and, [pl.program_id(d) == 0 for d in range(rank)])
            last = functools.reduce(jnp.logical_and, [pl.program_id(d) == pl.num_programs(d) - 1 for d in range(rank)])

            @pl.when(first)
            def _():
                for e, i, o, s in zip(ride, ins, outs, sems):
                    e.start(i, o, s)

            body(*own_in, *own_out, *own_scratch)

            @pl.when(last)
            def _():
                for e, i, o, s in zip(ride, ins, outs, sems):
                    e.wait(i, o, s)
        else:
            for e, i, o, s in zip(ride, ins, outs, sems):
                e.start(i, o, s)
            for e, i, o, s in zip(ride, ins, outs, sems):
                e.wait(i, o, s)

    return wrapped


def _ride_args(ride):
    ins = [a for e in ride for a in e.ins]
    outs = [s for e in ride for s in e.out_shape]
    sems = [s for e in ride for s in e.sems]
    return ins, _any_specs(len(ins)), outs, _any_specs(len(outs)), sems


def _ride_results(ride, flat):
    out, a = [], 0
    for e in ride:
        out.append(list(flat[a:a + len(e.out_shape)]))
        a += len(e.out_shape)
    return out


def exchange(ride, name):
    ins, in_specs, outs, out_specs, sems = _ride_args(ride)
    res = pl.pallas_call(
        _riding(lambda: None, 0, 0, 0, ride, 0), name=name,
        in_specs=in_specs, out_specs=out_specs, out_shape=outs, scratch_shapes=sems,
    )(*ins)
    return _ride_results(ride, res)


def _sum_slots(ref):
    acc = ref[0].astype(F32)
    for j in range(1, ref.shape[0]):
        acc = acc + ref[j].astype(F32)
    return acc


def sum_pairs(mine, other, name, tr=176):
    n, r, w = mine.shape
    tr = _tile(r, tr)

    def body(a_ref, b_ref, o_ref):
        o_ref[...] = _sum_slots(a_ref) + _sum_slots(b_ref)

    spec = pl.BlockSpec((n, tr, w), lambda i: (0, i, 0))
    return pl.pallas_call(
        body, name=name, grid=(r // tr,),
        in_specs=[spec, spec], out_specs=_rows(tr, w),
        out_shape=jax.ShapeDtypeStruct((r, w), F32),
        compiler_params=_params(("parallel",)),
    )(mine, other)


def adamw(w, m, v, gs, name, tr=256):
    r, c = w.shape
    tr = r if r % 8 else _tile(r, tr)
    c1 = 1.0 - ADAM_B1 ** ADAM_STEP
    c2 = 1.0 - ADAM_B2 ** ADAM_STEP
    ng = len(gs)

    def body(w_ref, m_ref, v_ref, *refs):
        g_refs, (g_ref, d_ref, nm_ref, nv_ref) = refs[:ng], refs[ng:]
        g = g_refs[0][...] if ng == 1 else _sum_slots(g_refs[0]) + _sum_slots(g_refs[1])
        nm = ADAM_B1 * m_ref[...] + (1.0 - ADAM_B1) * g
        nv = ADAM_B2 * v_ref[...] + (1.0 - ADAM_B2) * (g * g)
        g_ref[...] = g
        nm_ref[...] = nm
        nv_ref[...] = nv
        d_ref[...] = (-ADAM_LR) * ((nm / c1) / (jnp.sqrt(nv / c2) + ADAM_EPS) + ADAM_WD * w_ref[...])

    spec = _rows(tr, c)
    gspec = spec if ng == 1 else pl.BlockSpec((N_CHIPS, tr, c), lambda i: (0, i, 0))
    return pl.pallas_call(
        body, name=name, grid=(r // tr,),
        in_specs=[spec] * 3 + [gspec] * ng, out_specs=[spec] * 4,
        out_shape=[jax.ShapeDtypeStruct((r, c), F32)] * 4,
        compiler_params=_params(("parallel",)),
    )(w, m, v, *gs)


_PARAMS = (
    ("rel_bias", None), ("norm_mix_pre", None), ("norm_mix_post", None), ("w_in", 1), ("conv_rnn_w", 1),
    ("conv_rnn_b", None), ("w_rg_a", None), ("b_rg_a", None), ("w_rg_x", None), ("b_rg_x", None),
    ("lru_lambda", None), ("w_branch_rnn", 0), ("w_branch_att", 1), ("w_out", 0), ("norm_ffn_pre", None),
    ("norm_ffn_post", None), ("w_ffn_gate", 1), ("w_ffn_up", 1), ("conv_ffn_w", 1), ("conv_ffn_b", None),
    ("w_ffn_down", 0),
)
_SMALL = 65536


def _as2d(a):
    a = a[0] if a.shape[0] == 1 and a.ndim >= 3 else a
    return a.reshape(-1, a.shape[-1]) if a.ndim == 3 else a


def _pack(pieces, dtype):
    flat = jnp.concatenate([p.astype(dtype).reshape(-1) for p in pieces])
    unit = PACK_W * PACK_ROWS
    pad = (-flat.shape[0]) % unit
    flat = jnp.pad(flat, (0, pad))
    return flat.reshape(-1, PACK_W)


def _unpack(buf, shapes):
    flat = buf.reshape(-1)
    out, off = [], 0
    for shp in shapes:
        n = int(np.prod(shp))
        out.append(flat[off:off + n].reshape(shp))
        off += n
    return out


def _join(slots, ax):
    if ax == 0:
        return slots.reshape(-1, slots.shape[-1])
    return jnp.transpose(slots, (1, 0, 2)).reshape(slots.shape[1], -1)


def _cut(full, ax):
    if ax == 0:
        return full.reshape(N_CHIPS, -1, full.shape[-1])
    return jnp.transpose(full.reshape(full.shape[0], N_CHIPS, -1), (1, 0, 2))


def kernel(x, rel_bias, norm_mix_pre, norm_mix_post, w_in, conv_rnn_w, conv_rnn_b, w_rg_a, b_rg_a, w_rg_x, b_rg_x, lru_lambda, w_branch_rnn, w_branch_att, w_out, norm_ffn_pre, norm_ffn_post, w_ffn_gate, w_ffn_up, conv_ffn_w, conv_ffn_b, w_ffn_down, loss_target, m_rel_bias, m_norm_mix_pre, m_norm_mix_post, m_w_in, m_conv_rnn_w, m_conv_rnn_b, m_w_rg_a, m_b_rg_a, m_w_rg_x, m_b_rg_x, m_lru_lambda, m_w_branch_rnn, m_w_branch_att, m_w_out, m_norm_ffn_pre, m_norm_ffn_post, m_w_ffn_gate, m_w_ffn_up, m_conv_ffn_w, m_conv_ffn_b, m_w_ffn_down, v_rel_bias, v_norm_mix_pre, v_norm_mix_post, v_w_in, v_conv_rnn_w, v_conv_rnn_b, v_w_rg_a, v_b_rg_a, v_w_rg_x, v_b_rg_x, v_lru_lambda, v_w_branch_rnn, v_w_branch_att, v_w_out, v_norm_ffn_pre, v_norm_ffn_post, v_w_ffn_gate, v_w_ffn_up, v_conv_ffn_w, v_conv_ffn_b, v_w_ffn_down):
    args = dict(locals())
    names = [n for n, _ in _PARAMS]
    axis = dict(_PARAMS)
    w_loc = {n: args[n] for n in names}
    m_loc = {n: args["m_" + n] for n in names}
    v_loc = {n: args["v_" + n] for n in names}
    sharded = [n for n in names if axis[n] is not None]
    replicated = [n for n in names if axis[n] is None]

    big = [n for n in sharded if w_loc[n].size >= _SMALL]
    small_sharded = [n for n in sharded if n not in big]
    small = replicated + small_sharded

    first = ["w_in"] + small_sharded
    srcs = [_as2d(w_loc[n]).astype(BF16) if n in big else _as2d(w_loc[n]) for n in first]
    (gathered,) = exchange([_Gather(srcs)], "gather_first")
    p = {n: _join(a, axis[n]) for n, a in zip(first, gathered)}
    for n in replicated:
        p[n] = _as2d(w_loc[n])
    shards = {n: _as2d(w_loc[n]).astype(BF16) for n in big if n not in first}

    received, sibling, g_small, loss_part = _local_step(x, loss_target, p, shards)

    pack = _pack([g_small[n] for n in small], BF16)
    ((received["small"],),) = exchange([_Scatter([], [pack])], "scatter_small")
    late = [n for n in received if n not in sibling]
    (swapped,) = exchange([_Swap([received[n] for n in late])], "swap_last")
    sibling.update(zip(late, swapped))
    small_sum = sum_pairs(received["small"], sibling["small"], "sum_small")
    g_tot = dict(zip(small, _unpack(small_sum, [g_small[n].shape for n in small])))
    chip = 2 * lax.axis_index("x") + lax.axis_index("y")
    for n in small_sharded:
        size = g_tot[n].shape[axis[n]] // N_CHIPS
        g_tot[n] = lax.dynamic_slice_in_dim(g_tot[n], chip * size, size, axis=axis[n])

    out_g, out_d, out_m, out_v = {}, {}, {}, {}
    for i, n in enumerate(names):
        shp = w_loc[n].shape
        gs = (received[n], sibling[n]) if n in big else (g_tot[n],)
        g, d, nm, nv = adamw(_as2d(w_loc[n]), _as2d(m_loc[n]), _as2d(v_loc[n]), gs, "adamw_" + n)
        out_g[n], out_d[n], out_m[n], out_v[n] = (t.reshape(shp) for t in (g, d, nm, nv))

    d_model = x.shape[-1]
    loss = lax.psum(0.5 * jnp.sum(loss_part) / d_model, ("x", "y", "c"))
    grad_x = g_small["x"]
    return (loss, grad_x, *[out_g[n] for n in names], *[out_d[n] for n in names],
            *[out_m[n] for n in names], *[out_v[n] for n in names])


def _local_step(x, target, p, shards):
    axis = dict(_PARAMS)
    b, s, d = x.shape
    t = b * s
    rnn = p["b_rg_a"].shape[1]
    ffn = p["conv_ffn_b"].shape[1]
    nbk = rnn // p["w_rg_a"].shape[1]
    hkv = (p["w_in"].shape[1] - rnn - 2 * d) // (N_GROUPS + 2)
    h = hkv // HEAD_DIM
    nq = N_GROUPS * hkv

    x2 = x.reshape(t, d)
    tgt = target.reshape(t, d)
    w_in = p["w_in"]
    in_splits = (rnn, nq + 2 * hkv, 2 * d)
    wa = p["w_rg_a"].reshape(nbk, -1, p["w_rg_a"].shape[1]).astype(BF16)
    wx = p["w_rg_x"].reshape(nbk, -1, p["w_rg_x"].shape[1]).astype(BF16)
    cw_r, cb_r = p["conv_rnn_w"], p["conv_rnn_b"]
    cw_f, cb_f = p["conv_ffn_w"], p["conv_ffn_b"]

    masks, buckets = zip(*[_band(w_, r_) for w_, r_ in DILATED])
    bucket_f = jnp.asarray(np.where(np.stack(masks), np.stack(buckets), -1).astype(np.float32))
    rel_rows = jnp.pad(p["rel_bias"].T, ((0, 0), (0, 128 - REL_BUCKETS)))[:, None, :]
    biasm = bias_table(rel_rows, bucket_f, h, "bias_table")

    early = ["w_branch_rnn", "w_branch_att", "w_out"]
    hn1, (xr, qkv, gts), (got,) = norm_mm(x2, p["norm_mix_pre"], [w_in], [in_splits], "in_proj",
                                          ride=[_Gather([shards[n] for n in early])])
    p.update({n: _join(a, axis[n]) for n, a in zip(early, got)})
    xr3 = xr.reshape(b, s, rnn)
    y_rnn, (got,) = rglru_fwd(xr3, cw_r, cb_r, wa, p["b_rg_a"], wx, p["b_rg_x"], p["lru_lambda"], "rglru_fwd",
                              ride=[_Gather([shards[n] for n in ("w_ffn_gate", "w_ffn_up")])])
    p.update({n: _join(a, axis[n]) for n, a in zip(("w_ffn_gate", "w_ffn_up"), got)})
    qkv3 = qkv.reshape(b, s, -1)
    o_att, lse, ((got,),) = attn_fwd(qkv3, biasm, h, "attn_fwd", ride=[_Gather([shards["w_ffn_down"]])])
    p["w_ffn_down"] = _join(got, axis["w_ffn_down"])
    merged, br, ba = merge_fwd(y_rnn.reshape(t, rnn), o_att.reshape(t, hkv), gts, p["w_branch_rnn"],
                               p["w_branch_att"], "merge_fwd")
    mix, h1 = mm_norm_res(merged, p["w_out"], p["norm_mix_post"], x2, "out_proj")
    hn2, (gate_pre, up), _ = norm_mm(h1, p["norm_ffn_pre"], [p["w_ffn_gate"], p["w_ffn_up"]], [(ffn,), (ffn,)], "ffn_in")
    act = ffn_act(gate_pre.reshape(b, s, ffn), up.reshape(b, s, ffn), cw_f, cb_f, "ffn_act")
    ff, y = mm_norm_res(act.reshape(t, ffn), p["w_ffn_down"], p["norm_ffn_post"], h1, "ffn_down")

    g, gb = {}, {}
    recv, sib = {}, {}

    def rows4(a):
        return a.reshape(N_CHIPS, -1, a.shape[-1])

    dy, dff, g["norm_ffn_post"], loss_part = loss_norm_bwd(y, tgt, ff, p["norm_ffn_post"], "loss_bwd")
    dact = mm_nt([([dff], p["w_ffn_down"])], F32, "ffn_down_dx")
    gb["w_ffn_down"] = rows4(mm_tn(act.reshape(t, ffn), [dff], "ffn_down_dw"))
    dgp, dup, g["conv_ffn_w"], g["conv_ffn_b"] = ffn_bwd(dact.reshape(b, s, ffn), gate_pre.reshape(b, s, ffn),
                                                        up.reshape(b, s, ffn), cw_f, cb_f, "ffn_bwd")
    dgp, dup = dgp.reshape(t, ffn), dup.reshape(t, ffn)
    dhn2, ((recv["w_ffn_down"],),) = mm_nt([([dgp], p["w_ffn_gate"]), ([dup], p["w_ffn_up"])], F32, "ffn_in_dx",
                                           ride=[_Scatter([gb["w_ffn_down"]])])
    gb["w_ffn_gate"] = mm_tn(hn2, [dgp], "ffn_gate_dw", col_shards=N_CHIPS)
    gb["w_ffn_up"] = mm_tn(hn2, [dup], "ffn_up_dw", col_shards=N_CHIPS)
    dh1, g["norm_ffn_pre"] = norm_bwd(dhn2, h1, p["norm_ffn_pre"], dy, F32, "ffn_norm_bwd")
    dmix, g["norm_mix_post"] = norm_bwd(dh1, mix, p["norm_mix_post"], None, BF16, "mix_norm_bwd")
    dmerged = mm_nt([([dmix], p["w_out"])], F32, "out_proj_dx")
    gb["w_out"] = rows4(mm_tn(merged, [dmix], "out_proj_dw"))
    dbr, dba, dgts = merge_bwd(dmerged, gts, br, ba, "merge_bwd")
    dy_rnn = mm_nt([([dbr], p["w_branch_rnn"])], F32, "branch_rnn_dx")
    do_att = mm_nt([([dba], p["w_branch_att"])], F32, "branch_att_dx")
    gb["w_branch_rnn"] = rows4(mm_tn(y_rnn.reshape(t, rnn), [dbr], "branch_rnn_dw"))
    gb["w_branch_att"] = mm_tn(o_att.reshape(t, hkv), [dba], "branch_att_dw", col_shards=N_CHIPS)
    ffn_in = ["w_ffn_gate", "w_ffn_up"]
    (dxr, g["conv_rnn_w"], g["conv_rnn_b"], dwa, g["b_rg_a"], dwx, g["b_rg_x"], g["lru_lambda"]), (got,) = rglru_bwd(
        xr3, y_rnn, dy_rnn.reshape(b, s, rnn), cw_r, cb_r, wa, p["b_rg_a"], wx, p["b_rg_x"], p["lru_lambda"], "rglru_bwd",
        ride=[_Scatter([gb[n] for n in ffn_in])])
    recv.update(zip(ffn_in, got))
    g["w_rg_a"] = dwa.reshape(p["w_rg_a"].shape)
    g["w_rg_x"] = dwx.reshape(p["w_rg_x"].shape)
    mid = ["w_out", "w_branch_rnn", "w_branch_att"]
    early_recv = ["w_ffn_down"] + ffn_in
    (dq1, dq2, dq3, dk, dv, ds_sum), (got, swapped) = attn_bwd(
        qkv3, biasm, o_att, lse, do_att.reshape(b, s, hkv), h, "attn_bwd",
        ride=[_Scatter([gb[n] for n in mid]), _Swap([recv[n] for n in early_recv])])
    recv.update(zip(mid, got))
    sib.update(zip(early_recv, swapped))
    rows = bias_grad(ds_sum, bucket_f, "bias_grad")
    g["rel_bias"] = rows[:, 0, :REL_BUCKETS].T
    dproj = [dxr.reshape(t, rnn)] + [a.reshape(t, hkv) for a in (dq1, dq2, dq3, dk, dv)] + [dgts]
    dw_a = mm_tn(hn1, dproj[:4], "in_proj_dw_a")[0]
    dw_b = mm_tn(hn1, dproj[4:], "in_proj_dw_b")[0]
    gb["w_in"] = _cut(jnp.concatenate([dw_a, dw_b], axis=1), 1)
    dhn1, ((recv["w_in"],), got) = mm_nt([(dproj, w_in)], F32, "in_proj_dx",
                                         ride=[_Scatter([gb["w_in"]]), _Swap([recv[n] for n in mid])])
    sib.update(zip(mid, got))
    dx, g["norm_mix_pre"] = norm_bwd(dhn1, x2, p["norm_mix_pre"], dh1, F32, "in_norm_bwd")
    g["x"] = dx.reshape(b, s, d)
    return recv, sib, g, loss_part
```

```python
import functools
import math

import numpy as np
import jax
import jax.numpy as jnp
from jax import lax
from jax.experimental import pallas as pl
from jax.experimental.pallas import tpu as pltpu

F32 = jnp.float32
BF16 = jnp.bfloat16

EPS = 1e-6
HEAD_DIM = 128
ATTN_BLOCK = 128
DILATED = ((128, 1), (512, 4), (2048, 16))
N_GROUPS = len(DILATED)
REL_BUCKETS = 32
REL_MAX_DIST = 2048
LRU_C = 8.0
NEG = -1e30

ADAM_LR = 0.001
ADAM_B1 = 0.9
ADAM_B2 = 0.999
ADAM_EPS = 1e-08
ADAM_WD = 0.01
ADAM_STEP = 10

N_CHIPS = 4
PACK_W = 1024
PACK_ROWS = 16
VMEM_LIMIT = 56 * 1024 * 1024
MESH = pl.DeviceIdType.MESH


def _params(sem=None):
    return pltpu.CompilerParams(dimension_semantics=sem, vmem_limit_bytes=VMEM_LIMIT)


def _dot(a, b):
    return jnp.dot(a, b, preferred_element_type=F32)


def _dot_nt(a, b):
    return lax.dot_general(a, b, (((1,), (1,)), ((), ())), preferred_element_type=F32)


def _dot_tn(a, b):
    return lax.dot_general(a, b, (((0,), (0,)), ((), ())), preferred_element_type=F32)


def _sig(x):
    return 0.5 * jnp.tanh(0.5 * x) + 0.5


def _rows(tm, w):
    return pl.BlockSpec((tm, w), lambda i: (i, 0))


def _whole(shape):
    nd = len(shape)
    return pl.BlockSpec(tuple(shape), lambda *_: (0,) * nd)


def _tile(t, want):
    while t % want:
        want //= 2
    return want


def norm_mm(x, g, ws, splits, name, ride=(), tm=256):
    t, d = x.shape
    tm = _tile(t, tm)
    nw = len(ws)
    widths = [n for sp in splits for n in sp]

    def body(x_ref, g_ref, *refs):
        w_refs, hn_ref, o_refs = refs[:nw], refs[nw], refs[nw + 1:]
        xv = x_ref[...]
        inv = lax.rsqrt(jnp.mean(xv * xv, axis=-1, keepdims=True) + EPS)
        hn = (xv * inv * g_ref[...]).astype(BF16)
        hn_ref[...] = hn
        o = 0
        for w_ref, sp in zip(w_refs, splits):
            off = 0
            for n in sp:
                o_refs[o][...] = _dot(hn, w_ref[:, off:off + n])
                off += n
                o += 1

    r_ins, r_in_specs, r_outs, r_out_specs, r_sems = _ride_args(ride)
    n_out = 1 + len(widths)
    outs = pl.pallas_call(
        _riding(body, 2 + nw, n_out, 0, ride, 1), name=name, grid=(t // tm,),
        in_specs=[_rows(tm, d), _whole(g.shape)] + [_whole(w.shape) for w in ws] + r_in_specs,
        out_specs=[_rows(tm, d)] + [_rows(tm, n) for n in widths] + r_out_specs,
        out_shape=[jax.ShapeDtypeStruct((t, d), BF16)] + [jax.ShapeDtypeStruct((t, n), F32) for n in widths] + r_outs,
        scratch_shapes=r_sems,
        compiler_params=_params(("arbitrary",)),
    )(x, g, *ws, *r_ins)
    return outs[0], outs[1:n_out], _ride_results(ride, outs[n_out:])


def mm_nt(groups, out_dtype, name, ride=(), tm=256):
    dys_all = [dy for dys, _ in groups for dy in dys]
    ws = [w for _, w in groups]
    t = dys_all[0].shape[0]
    k = ws[0].shape[0]
    tm = _tile(t, tm)
    n = len(dys_all)

    def body(*refs):
        dy_refs, w_refs, o_ref = refs[:n], refs[n:n + len(ws)], refs[n + len(ws)]
        acc = None
        i = 0
        for (dys, _), w_ref in zip(groups, w_refs):
            off = 0
            for dy in dys:
                width = dy.shape[1]
                part = _dot_nt(dy_refs[i][...].astype(BF16), w_ref[:, off:off + width])
                acc = part if acc is None else acc + part
                off += width
                i += 1
        o_ref[...] = acc.astype(o_ref.dtype)

    r_ins, r_in_specs, r_outs, r_out_specs, r_sems = _ride_args(ride)
    outs = pl.pallas_call(
        _riding(body, n + len(ws), 1, 0, ride, 1), name=name, grid=(t // tm,),
        in_specs=[_rows(tm, dy.shape[1]) for dy in dys_all] + [_whole(w.shape) for w in ws] + r_in_specs,
        out_specs=[_rows(tm, k)] + r_out_specs,
        out_shape=[jax.ShapeDtypeStruct((t, k), out_dtype)] + r_outs,
        scratch_shapes=r_sems,
        compiler_params=_params(("arbitrary",) if ride else ("parallel",)),
    )(*dys_all, *ws, *r_ins)
    return (outs[0], _ride_results(ride, outs[1:])) if ride else outs[0]


def mm_tn(a, dys, name, col_shards=1, tm=512):
    t, k = a.shape
    tm = _tile(t, tm)
    n = len(dys)
    ntot = sum(dy.shape[1] for dy in dys)
    wsh = ntot // col_shards

    def body(a_ref, *refs):
        dy_refs, o_ref, acc = refs[:n], refs[n], refs[n + 1]

        @pl.when(pl.program_id(0) == 0)
        def _():
            acc[...] = jnp.zeros(acc.shape, F32)

        av = a_ref[...].astype(BF16)
        off = 0
        for dy_ref in dy_refs:
            width = dy_ref.shape[1]
            acc[:, off:off + width] += _dot_tn(av, dy_ref[...].astype(BF16))
            off += width

        @pl.when(pl.program_id(0) == pl.num_programs(0) - 1)
        def _():
            for j in range(col_shards):
                o_ref[j] = acc[:, j * wsh:(j + 1) * wsh].astype(o_ref.dtype)

    return pl.pallas_call(
        body, name=name, grid=(t // tm,),
        in_specs=[_rows(tm, k)] + [_rows(tm, dy.shape[1]) for dy in dys],
        out_specs=_whole((col_shards, k, wsh)),
        out_shape=jax.ShapeDtypeStruct((col_shards, k, wsh), BF16),
        scratch_shapes=[pltpu.VMEM((k, ntot), F32)],
        compiler_params=_params(("arbitrary",)),
    )(a, *dys)


def mm_norm_res(a, w, g, resid, name, tm=256):
    t, k = a.shape
    d = w.shape[1]
    tm = _tile(t, tm)

    def body(a_ref, w_ref, g_ref, r_ref, p_ref, o_ref):
        prod = _dot(a_ref[...], w_ref[...])
        p_ref[...] = prod
        inv = lax.rsqrt(jnp.mean(prod * prod, axis=-1, keepdims=True) + EPS)
        o_ref[...] = r_ref[...] + prod * inv * g_ref[...]

    return pl.pallas_call(
        body, name=name, grid=(t // tm,),
        in_specs=[_rows(tm, k), _whole(w.shape), _whole(g.shape), _rows(tm, d)],
        out_specs=[_rows(tm, d), _rows(tm, d)],
        out_shape=[jax.ShapeDtypeStruct((t, d), F32)] * 2,
        compiler_params=_params(("parallel",)),
    )(a, w, g, resid)


def _rms_bwd(dz, u, g):
    d = u.shape[-1]
    inv = lax.rsqrt(jnp.mean(u * u, axis=-1, keepdims=True) + EPS)
    dzg = dz * g
    proj = jnp.sum(dzg * u, axis=-1, keepdims=True) * (1.0 / d)
    du = inv * (dzg - u * (inv * inv) * proj)
    dg_rows = dz * u * inv
    return du, dg_rows


def norm_bwd(dz, u, g, add, out_dtype, name, tm=256):
    t, d = u.shape
    tm = _tile(t, tm)
    has_add = add is not None

    def body(*refs):
        if has_add:
            dz_ref, u_ref, g_ref, add_ref, du_ref, dg_ref = refs
        else:
            dz_ref, u_ref, g_ref, du_ref, dg_ref = refs

        @pl.when(pl.program_id(0) == 0)
        def _():
            dg_ref[...] = jnp.zeros(dg_ref.shape, F32)

        du, dg_rows = _rms_bwd(dz_ref[...].astype(F32), u_ref[...], g_ref[...])
        if has_add:
            du = du + add_ref[...]
        du_ref[...] = du.astype(du_ref.dtype)
        dg_ref[...] += jnp.sum(dg_rows, axis=0, keepdims=True)

    ins = [dz, u, g] + ([add] if has_add else [])
    return pl.pallas_call(
        body, name=name, grid=(t // tm,),
        in_specs=[_rows(tm, d), _rows(tm, d), _whole(g.shape)] + ([_rows(tm, d)] if has_add else []),
        out_specs=[_rows(tm, d), _whole((1, d))],
        out_shape=[jax.ShapeDtypeStruct((t, d), out_dtype), jax.ShapeDtypeStruct((1, d), F32)],
        compiler_params=_params(("arbitrary",)),
    )(*ins)


def loss_norm_bwd(y, target, ff, g, name, tm=256):
    t, d = y.shape
    tm = _tile(t, tm)

    def body(y_ref, t_ref, ff_ref, g_ref, dy_ref, dff_ref, dg_ref, loss_ref):
        @pl.when(pl.program_id(0) == 0)
        def _():
            dg_ref[...] = jnp.zeros(dg_ref.shape, F32)
            loss_ref[...] = jnp.zeros(loss_ref.shape, F32)

        err = y_ref[...] - t_ref[...]
        loss_ref[...] += jnp.sum(err * err, axis=0, keepdims=True)
        dy = err * (1.0 / d)
        dy_ref[...] = dy
        du, dg_rows = _rms_bwd(dy, ff_ref[...], g_ref[...])
        dff_ref[...] = du.astype(dff_ref.dtype)
        dg_ref[...] += jnp.sum(dg_rows, axis=0, keepdims=True)

    return pl.pallas_call(
        body, name=name, grid=(t // tm,),
        in_specs=[_rows(tm, d), _rows(tm, d), _rows(tm, d), _whole(g.shape)],
        out_specs=[_rows(tm, d), _rows(tm, d), _whole((1, d)), _whole((1, d))],
        out_shape=[jax.ShapeDtypeStruct((t, d), F32), jax.ShapeDtypeStruct((t, d), BF16),
                   jax.ShapeDtypeStruct((1, d), F32), jax.ShapeDtypeStruct((1, d), F32)],
        compiler_params=_params(("arbitrary",)),
    )(y, target, ff, g)


def merge_fwd(y_rnn, o_att, gts, w_br, w_ba, name, tm=256):
    t = y_rnn.shape[0]
    d = w_br.shape[1]
    tm = _tile(t, tm)

    def body(y_ref, o_ref, g_ref, wbr_ref, wba_ref, m_ref, br_ref, ba_ref):
        br = _dot(y_ref[...].astype(BF16), wbr_ref[...])
        ba = _dot(o_ref[...].astype(BF16), wba_ref[...])
        gv = g_ref[...]
        m_ref[...] = (_sig(gv[:, :d]) * br + _sig(gv[:, d:]) * ba).astype(BF16)
        br_ref[...] = br
        ba_ref[...] = ba

    return pl.pallas_call(
        body, name=name, grid=(t // tm,),
        in_specs=[_rows(tm, y_rnn.shape[1]), _rows(tm, o_att.shape[1]), _rows(tm, 2 * d),
                  _whole(w_br.shape), _whole(w_ba.shape)],
        out_specs=[_rows(tm, d)] * 3,
        out_shape=[jax.ShapeDtypeStruct((t, d), BF16), jax.ShapeDtypeStruct((t, d), F32),
                   jax.ShapeDtypeStruct((t, d), F32)],
        compiler_params=_params(("parallel",)),
    )(y_rnn, o_att, gts, w_br, w_ba)


def merge_bwd(dmerged, gts, br, ba, name, tm=256):
    t, d = dmerged.shape
    tm = _tile(t, tm)

    def body(dm_ref, g_ref, br_ref, ba_ref, dbr_ref, dba_ref, dg_ref):
        dm = dm_ref[...]
        gv = g_ref[...]
        sr = _sig(gv[:, :d])
        sa = _sig(gv[:, d:])
        dbr_ref[...] = (dm * sr).astype(BF16)
        dba_ref[...] = (dm * sa).astype(BF16)
        dg_ref[:, :d] = (dm * br_ref[...] * sr * (1.0 - sr)).astype(BF16)
        dg_ref[:, d:] = (dm * ba_ref[...] * sa * (1.0 - sa)).astype(BF16)

    return pl.pallas_call(
        body, name=name, grid=(t // tm,),
        in_specs=[_rows(tm, d), _rows(tm, 2 * d), _rows(tm, d), _rows(tm, d)],
        out_specs=[_rows(tm, d), _rows(tm, d), _rows(tm, 2 * d)],
        out_shape=[jax.ShapeDtypeStruct((t, d), BF16), jax.ShapeDtypeStruct((t, d), BF16),
                   jax.ShapeDtypeStruct((t, 2 * d), BF16)],
        compiler_params=_params(("parallel",)),
    )(dmerged, gts, br, ba)


def _shift_dn(x, d, fill, row):
    return jnp.where(row >= d, pltpu.roll(x, d, 0), fill)


def _shift_up(x, d, fill, row):
    s = x.shape[0]
    return jnp.where(row < s - d, pltpu.roll(x, s - d, 0), fill)


def _conv_fwd(x, w, b, row):
    kk = w.shape[0]
    y = b + w[kk - 1:kk, :] * x
    for j in range(1, kk):
        y = y + w[kk - 1 - j:kk - j, :] * _shift_dn(x, j, 0.0, row)
    return y


def _conv_bwd(dy, x, w, row):
    kk = w.shape[0]
    dx = w[kk - 1:kk, :] * dy
    dws = [None] * kk
    dws[kk - 1] = jnp.sum(dy * x, axis=0, keepdims=True)
    for j in range(1, kk):
        dx = dx + w[kk - 1 - j:kk - j, :] * _shift_up(dy, j, 0.0, row)
        dws[kk - 1 - j] = jnp.sum(dy * _shift_dn(x, j, 0.0, row), axis=0, keepdims=True)
    return dx, jnp.concatenate(dws, axis=0)


def _softplus(z):
    y = jnp.exp(-jnp.abs(z))
    u = 1.0 + y
    dd = u - 1.0
    log1p = jnp.where(dd == 0.0, y, jnp.log(u) * (y / jnp.where(dd == 0.0, 1.0, dd)))
    return jnp.maximum(z, 0.0) + log1p


def _lru_decay(xb, wa, ba, lam):
    r = _sig(_dot(xb, wa) + ba)
    sp = _softplus(-lam)
    la = (-LRU_C) * r * sp
    return r, sp, la, jnp.exp(la)


def _lru_gates(xc, wa, ba, wx, bx, lam):
    xb = xc.astype(BF16)
    r, sp, la, a = _lru_decay(xb, wa, ba, lam)
    i = _sig(_dot(xb, wx) + bx)
    one_m_a2 = jnp.tanh(-la) * (1.0 + a * a)
    inv_mult = lax.rsqrt(one_m_a2)
    return r, i, sp, a, one_m_a2 * inv_mult, inv_mult


def _seg_len(s):
    seg = -(-s // 8)
    return seg + (4 - seg % 8) % 8


def _scan_rows(a_pad, u_pad, out_pad, reverse):
    planes, rows8, lanes = a_pad.shape
    seg = rows8 // 8
    sub = lax.broadcasted_iota(jnp.int32, (planes, 8, lanes), 1)

    unroll = 4

    def rows(k, d):
        i = k * unroll + d
        return pl.ds((seg - 1 - i) if reverse else i, 8, stride=seg)

    def ends(k, carry):
        h, p = carry
        for d in range(unroll):
            a = a_pad[:, rows(k, d), :]
            h = a * h + u_pad[:, rows(k, d), :]
            p = a * p
        return h, p

    init = (jnp.zeros((planes, 8, lanes), F32), jnp.ones((planes, 8, lanes), F32))
    h_end, p_end = lax.fori_loop(0, seg // unroll, ends, init)
    start = jnp.zeros((planes, 8, lanes), F32)
    for _ in range(7):
        nxt = h_end + p_end * start
        if reverse:
            start = jnp.where(sub < 7, pltpu.roll(nxt, 7, 1), 0.0)
        else:
            start = jnp.where(sub >= 1, pltpu.roll(nxt, 1, 1), 0.0)

    def redo(k, h):
        for d in range(unroll):
            h = a_pad[:, rows(k, d), :] * h + u_pad[:, rows(k, d), :]
            out_pad[:, rows(k, d), :] = h
        return h

    lax.fori_loop(0, seg // unroll, redo, start)


def _lru_cols(c, rb):
    return 2 * rb if c % (2 * rb) == 0 else rb


def rglru_fwd(xr, cw, cb, wa, ba, wx, bx, lam, name, ride=()):
    b, s, c = xr.shape
    rb = wa.shape[1]
    kk = cw.shape[0]
    cols = _lru_cols(c, rb)
    nj = cols // rb
    seg = _seg_len(s)

    def body(x_ref, cw_ref, cb_ref, wa_ref, ba_ref, wx_ref, bx_ref, lam_ref, h_ref, a_pad, u_pad, h_pad):
        row = lax.broadcasted_iota(jnp.int32, (s, rb), 0)
        for j in range(nj):
            cs = slice(j * rb, (j + 1) * rb)
            xc = _conv_fwd(x_ref[:, cs], cw_ref[:, cs], cb_ref[:, cs], row)
            _, i, _, a, mult, _ = _lru_gates(xc, wa_ref[j], ba_ref[:, cs], wx_ref[j], bx_ref[:, cs], lam_ref[:, cs])
            a_pad[j, 0:s, :] = a
            u_pad[j, 0:s, :] = mult * (i * xc)
        a_pad[:, s:, :] = jnp.ones((nj, 8 * seg - s, rb), F32)
        u_pad[:, s:, :] = jnp.zeros((nj, 8 * seg - s, rb), F32)
        _scan_rows(a_pad, u_pad, h_pad, False)
        for j in range(nj):
            h_ref[:, j * rb:(j + 1) * rb] = h_pad[j, 0:s, :]

    vec = pl.BlockSpec((1, cols), lambda bi, n: (0, n))
    seq = pl.BlockSpec((None, s, cols), lambda bi, n: (bi, 0, n))
    mat = pl.BlockSpec((nj, rb, rb), lambda bi, n: (n, 0, 0))
    r_ins, r_in_specs, r_outs, r_out_specs, r_sems = _ride_args(ride)
    outs = pl.pallas_call(
        _riding(body, 8, 1, 3, ride, 2), name=name, grid=(b, c // cols),
        in_specs=[seq, pl.BlockSpec((kk, cols), lambda bi, n: (0, n)), vec, mat, vec, mat, vec, vec] + r_in_specs,
        out_specs=[seq] + r_out_specs,
        out_shape=[jax.ShapeDtypeStruct((b, s, c), F32)] + r_outs,
        scratch_shapes=[pltpu.VMEM((nj, 8 * seg, rb), F32)] * 3 + r_sems,
        compiler_params=_params(("arbitrary", "arbitrary")),
    )(xr, cw, cb, wa, ba, wx, bx, lam, *r_ins)
    return outs[0], _ride_results(ride, outs[1:])


def rglru_bwd(xr, h, dh, cw, cb, wa, ba, wx, bx, lam, name, ride=()):
    b, s, c = xr.shape
    nb, rb = wa.shape[0], wa.shape[1]
    kk = cw.shape[0]
    cols = _lru_cols(c, rb)
    nj = cols // rb
    seg = _seg_len(s)

    def body(x_ref, h_ref, dh_ref, cw_ref, cb_ref, wa_ref, ba_ref, wx_ref, bx_ref, lam_ref,
             dx_ref, dcw_ref, dcb_ref, dwa_ref, dba_ref, dwx_ref, dbx_ref, dlam_ref, b_pad, g_pad, l_pad):
        @pl.when(pl.program_id(1) == 0)
        def _():
            for ref in (dcw_ref, dcb_ref, dwa_ref, dba_ref, dwx_ref, dbx_ref, dlam_ref):
                ref[...] = jnp.zeros(ref.shape, F32)

        row = lax.broadcasted_iota(jnp.int32, (s, rb), 0)

        def conv(j):
            cs = slice(j * rb, (j + 1) * rb)
            return _conv_fwd(x_ref[:, cs], cw_ref[:, cs], cb_ref[:, cs], row)

        for j in range(nj):
            cs = slice(j * rb, (j + 1) * rb)
            _, _, _, a = _lru_decay(conv(j).astype(BF16), wa_ref[j], ba_ref[:, cs], lam_ref[:, cs])
            b_pad[j, 0:s, :] = _shift_up(a, 1, 0.0, row)
            g_pad[j, 0:s, :] = dh_ref[:, j * rb:(j + 1) * rb]
        b_pad[:, s:, :] = jnp.zeros((nj, 8 * seg - s, rb), F32)
        g_pad[:, s:, :] = jnp.zeros((nj, 8 * seg - s, rb), F32)
        _scan_rows(b_pad, g_pad, l_pad, True)

        for j in range(nj):
            cs = slice(j * rb, (j + 1) * rb)
            x = x_ref[:, cs]
            cwv = cw_ref[:, cs]
            wav, wxv, lamv = wa_ref[j], wx_ref[j], lam_ref[:, cs]
            xc = conv(j)
            r, i, sp, a, mult, inv_mult = _lru_gates(xc, wav, ba_ref[:, cs], wxv, bx_ref[:, cs], lamv)
            lmb = l_pad[j, 0:s, :]
            h_prev = _shift_dn(h_ref[:, cs], 1, 0.0, row)
            da = lmb * h_prev
            ixc = i * xc
            dla = da * a - (lmb * ixc) * (a * a) * inv_mult
            di = lmb * mult * xc
            dxc = lmb * mult * i
            dr = dla * ((-LRU_C) * sp)
            dsp = jnp.sum(dla * ((-LRU_C) * r), axis=0, keepdims=True)
            dga = dr * r * (1.0 - r)
            dgx = di * i * (1.0 - i)
            dga_b, dgx_b = dga.astype(BF16), dgx.astype(BF16)
            xb = xc.astype(BF16)
            dwa_ref[j] += _dot_tn(xb, dga_b)
            dwx_ref[j] += _dot_tn(xb, dgx_b)
            dba_ref[:, cs] += jnp.sum(dga, axis=0, keepdims=True)
            dbx_ref[:, cs] += jnp.sum(dgx, axis=0, keepdims=True)
            dlam_ref[:, cs] += dsp * (-_sig(-lamv))
            dxc = dxc + _dot_nt(dga_b, wav) + _dot_nt(dgx_b, wxv)
            dcb_ref[:, cs] += jnp.sum(dxc, axis=0, keepdims=True)
            dx, dcw = _conv_bwd(dxc, x, cwv, row)
            dcw_ref[:, cs] += dcw
            dx_ref[:, cs] = dx.astype(dx_ref.dtype)

    vec = pl.BlockSpec((1, cols), lambda n, bi: (0, n))
    seq = pl.BlockSpec((None, s, cols), lambda n, bi: (bi, 0, n))
    mat = pl.BlockSpec((nj, rb, rb), lambda n, bi: (n, 0, 0))
    cws = pl.BlockSpec((kk, cols), lambda n, bi: (0, n))
    sd = jax.ShapeDtypeStruct
    r_ins, r_in_specs, r_outs, r_out_specs, r_sems = _ride_args(ride)
    outs = pl.pallas_call(
        _riding(body, 10, 8, 3, ride, 2), name=name, grid=(c // cols, b),
        in_specs=[seq, seq, seq, cws, vec, mat, vec, mat, vec, vec] + r_in_specs,
        out_specs=[seq, cws, vec, mat, vec, mat, vec, vec] + r_out_specs,
        out_shape=[sd((b, s, c), BF16), sd((kk, c), F32), sd((1, c), F32), sd((nb, rb, rb), F32),
                   sd((1, c), F32), sd((nb, rb, rb), F32), sd((1, c), F32), sd((1, c), F32)] + r_outs,
        scratch_shapes=[pltpu.VMEM((nj, 8 * seg, rb), F32)] * 3 + r_sems,
        compiler_params=_params(("arbitrary", "arbitrary")),
    )(xr, h, dh, cw, cb, wa, ba, wx, bx, lam, *r_ins)
    return outs[:8], _ride_results(ride, outs[8:])


_GELU_C = math.sqrt(2.0 / math.pi)


def _gelu_parts(x):
    th = jnp.tanh(_GELU_C * (x + 0.044715 * x * x * x))
    gel = 0.5 * x * (1.0 + th)
    dgel = 0.5 * (1.0 + th) + 0.5 * x * (1.0 - th * th) * _GELU_C * (1.0 + 3 * 0.044715 * x * x)
    return gel, dgel


def ffn_act(gate_pre, up, cw, cb, name, cbk=256):
    b, s, f = gate_pre.shape
    kk = cw.shape[0]
    cbk = _tile(f, cbk)

    def body(g_ref, u_ref, cw_ref, cb_ref, a_ref):
        row = lax.broadcasted_iota(jnp.int32, (s, cbk), 0)
        gate = _conv_fwd(g_ref[...], cw_ref[...], cb_ref[...], row)
        gel, _ = _gelu_parts(gate)
        a_ref[...] = (gel * u_ref[...]).astype(BF16)

    seq = pl.BlockSpec((None, s, cbk), lambda bi, n: (bi, 0, n))
    return pl.pallas_call(
        body, name=name, grid=(b, f // cbk),
        in_specs=[seq, seq, pl.BlockSpec((kk, cbk), lambda bi, n: (0, n)), pl.BlockSpec((1, cbk), lambda bi, n: (0, n))],
        out_specs=seq,
        out_shape=jax.ShapeDtypeStruct((b, s, f), BF16),
        compiler_params=_params(("parallel", "parallel")),
    )(gate_pre, up, cw, cb)


def ffn_bwd(dact, gate_pre, up, cw, cb, name, cbk=256):
    b, s, f = gate_pre.shape
    kk = cw.shape[0]
    cbk = _tile(f, cbk)

    def body(da_ref, g_ref, u_ref, cw_ref, cb_ref, dg_ref, du_ref, dcw_ref, dcb_ref):
        @pl.when(pl.program_id(1) == 0)
        def _():
            dcw_ref[...] = jnp.zeros(dcw_ref.shape, F32)
            dcb_ref[...] = jnp.zeros(dcb_ref.shape, F32)

        row = lax.broadcasted_iota(jnp.int32, (s, cbk), 0)
        gp = g_ref[...]
        cwv = cw_ref[...]
        gate = _conv_fwd(gp, cwv, cb_ref[...], row)
        gel, dgel = _gelu_parts(gate)
        da = da_ref[...]
        du_ref[...] = (da * gel).astype(BF16)
        dgate = da * u_ref[...] * dgel
        dcb_ref[...] += jnp.sum(dgate, axis=0, keepdims=True)
        dgp, dcw = _conv_bwd(dgate, gp, cwv, row)
        dcw_ref[...] += dcw
        dg_ref[...] = dgp.astype(BF16)

    seq = pl.BlockSpec((None, s, cbk), lambda n, bi: (bi, 0, n))
    cws = pl.BlockSpec((kk, cbk), lambda n, bi: (0, n))
    vec = pl.BlockSpec((1, cbk), lambda n, bi: (0, n))
    sd = jax.ShapeDtypeStruct
    return pl.pallas_call(
        body, name=name, grid=(f // cbk, b),
        in_specs=[seq, seq, seq, cws, vec],
        out_specs=[seq, seq, cws, vec],
        out_shape=[sd((b, s, f), BF16), sd((b, s, f), BF16), sd((kk, f), F32), sd((1, f), F32)],
        compiler_params=_params(("parallel", "arbitrary")),
    )(dact, gate_pre, up, cw, cb)


def _t5_bucket(dist):
    max_exact = REL_BUCKETS // 2
    d = np.maximum(dist, 1).astype(np.float32)
    large = max_exact + np.log(d / max_exact) / math.log(REL_MAX_DIST / max_exact) * (REL_BUCKETS - max_exact)
    large = np.minimum(large.astype(np.int32), REL_BUCKETS - 1)
    return np.where(dist < max_exact, dist, large).astype(np.int32)


def _band(window, dilation):
    qi = np.arange(ATTN_BLOCK)[:, None]
    kj = np.arange(2 * ATTN_BLOCK)[None, :]
    delta = ATTN_BLOCK + qi - kj
    mask = (delta >= 0) & (delta <= window // dilation)
    bucket = _t5_bucket(np.maximum(delta, 0) * dilation)
    return mask, bucket


def _attn_blocks(s, r):
    m = s // r
    assert m % ATTN_BLOCK == 0, "sequence length must be a multiple of dilation * block"
    return m // ATTN_BLOCK


def _perm_load(ref, r):
    if r == 1:
        return ref[...]
    m = ref.shape[0] // r
    return jnp.concatenate([ref[pl.ds(c, m, stride=r), :] for c in range(r)], axis=0)


def _perm_store(ref, g, val, r, add=False):
    if r == 1:
        ref[g] = ref[g] + val if add else val
        return
    m = val.shape[0] // r
    for c in range(r):
        rows = pl.ds(c, m, stride=r)
        part = val[c * m:(c + 1) * m]
        ref[g, rows, :] = ref[g, rows, :] + part if add else part


def _blocks(x):
    return x.reshape(x.shape[0] // ATTN_BLOCK, ATTN_BLOCK, x.shape[1])


def _prev_blocks(x):
    return jnp.concatenate([x[:1], x[:-1]], axis=0)


def _next_blocks(x):
    return jnp.concatenate([x[1:], jnp.zeros_like(x[:1])], axis=0)


def _first_block_neg(s, r):
    nblk = s // ATTN_BLOCK
    idx = lax.broadcasted_iota(jnp.int32, (nblk, 1, 1), 0)
    return jnp.where(idx % _attn_blocks(s, r) == 0, NEG, 0.0)


def _bdot_nt(a, b):
    return lax.dot_general(a, b, (((2,), (2,)), ((0,), (0,))), preferred_element_type=F32)


def _bdot(a, b):
    return lax.dot_general(a, b, (((2,), (1,)), ((0,), (0,))), preferred_element_type=F32)


def _bdot_tn(a, b):
    return lax.dot_general(a, b, (((1,), (1,)), ((0,), (0,))), preferred_element_type=F32)


def attn_fwd(qkv, biasm, n_heads, name, ride=()):
    b, s, _ = qkv.shape
    h = n_heads
    scale = HEAD_DIM ** -0.5
    blk = ATTN_BLOCK

    def body(q1_ref, q2_ref, q3_ref, k_ref, v_ref, bias_ref, o_ref, lse_ref, acc, m_s, l_s):
        for g, q_ref in enumerate((q1_ref, q2_ref, q3_ref)):
            r = DILATED[g][1]
            first = _first_block_neg(s, r)
            q = _blocks(_perm_load(q_ref, r).astype(BF16))
            k = _blocks(_perm_load(k_ref, r).astype(BF16))
            v = _blocks(_perm_load(v_ref, r).astype(BF16))
            s_cur = _bdot_nt(q, k) * scale + bias_ref[g, :, blk:]
            s_prev = _bdot_nt(q, _prev_blocks(k)) * scale + bias_ref[g, :, :blk] + first
            m = jnp.maximum(jnp.max(s_cur, axis=-1, keepdims=True), jnp.max(s_prev, axis=-1, keepdims=True))
            p_cur = jnp.exp(s_cur - m)
            p_prev = jnp.exp(s_prev - m)
            l = jnp.sum(p_cur, axis=-1, keepdims=True) + jnp.sum(p_prev, axis=-1, keepdims=True)
            o = _bdot(p_cur.astype(BF16), v) + _bdot(p_prev.astype(BF16), _prev_blocks(v))
            _perm_store(acc, g, o.reshape(s, HEAD_DIM), r)
            _perm_store(m_s, g, m.reshape(s, 1), r)
            _perm_store(l_s, g, l.reshape(s, 1), r)
        m_all = jnp.maximum(jnp.maximum(m_s[0], m_s[1]), m_s[2])
        w = [jnp.exp(m_s[g] - m_all) for g in range(N_GROUPS)]
        l = w[0] * l_s[0] + w[1] * l_s[1] + w[2] * l_s[2]
        o_ref[...] = (w[0] * acc[0] + w[1] * acc[1] + w[2] * acc[2]) / l
        lse_ref[...] = m_all + jnp.log(l)

    def col(j):
        return pl.BlockSpec((None, s, HEAD_DIM), lambda bi, hi, j=j: (bi, 0, j * h + hi))

    r_ins, r_in_specs, r_outs, r_out_specs, r_sems = _ride_args(ride)
    outs = pl.pallas_call(
        _riding(body, 6, 2, 3, ride, 2), name=name, grid=(b, h),
        in_specs=[col(0), col(1), col(2), col(3), col(4),
                  pl.BlockSpec((N_GROUPS, None, blk, 2 * blk), lambda bi, hi: (0, hi, 0, 0))] + r_in_specs,
        out_specs=[pl.BlockSpec((None, s, HEAD_DIM), lambda bi, hi: (bi, 0, hi)),
                   pl.BlockSpec((None, None, s, 1), lambda bi, hi: (bi, hi, 0, 0))] + r_out_specs,
        out_shape=[jax.ShapeDtypeStruct((b, s, h * HEAD_DIM), F32), jax.ShapeDtypeStruct((b, h, s, 1), F32)] + r_outs,
        scratch_shapes=[pltpu.VMEM((N_GROUPS, s, HEAD_DIM), F32), pltpu.VMEM((N_GROUPS, s, 1), F32),
                        pltpu.VMEM((N_GROUPS, s, 1), F32)] + r_sems,
        compiler_params=_params(("arbitrary", "arbitrary")),
    )(qkv, qkv, qkv, qkv, qkv, biasm, *r_ins)
    return outs[0], outs[1], _ride_results(ride, outs[2:])


def attn_bwd(qkv, biasm, o, lse, do, n_heads, name, ride=()):
    b, s, _ = qkv.shape
    h = n_heads
    scale = HEAD_DIM ** -0.5
    blk = ATTN_BLOCK

    def body(q1_ref, q2_ref, q3_ref, k_ref, v_ref, bias_ref, o_ref, lse_ref, do_ref,
             dq1_ref, dq2_ref, dq3_ref, dk_ref, dv_ref, ds_ref, dq_acc, kv_acc, delta):
        delta[...] = jnp.sum(do_ref[...] * o_ref[...], axis=-1, keepdims=True)
        kv_acc[...] = jnp.zeros(kv_acc.shape, F32)
        for g, q_ref in enumerate((q1_ref, q2_ref, q3_ref)):
            r = DILATED[g][1]
            first = _first_block_neg(s, r)
            q = _blocks(_perm_load(q_ref, r).astype(BF16))
            k = _blocks(_perm_load(k_ref, r).astype(BF16))
            v = _blocks(_perm_load(v_ref, r).astype(BF16))
            dob = _blocks(_perm_load(do_ref, r).astype(BF16))
            lse_b = _blocks(_perm_load(lse_ref, r))
            dl_b = _blocks(_perm_load(delta, r))
            k_prev, v_prev = _prev_blocks(k), _prev_blocks(v)
            p_cur = jnp.exp(_bdot_nt(q, k) * scale + bias_ref[g, :, blk:] - lse_b)
            p_prev = jnp.exp(_bdot_nt(q, k_prev) * scale + bias_ref[g, :, :blk] + first - lse_b)
            ds_cur = p_cur * (_bdot_nt(dob, v) - dl_b)
            ds_prev = p_prev * (_bdot_nt(dob, v_prev) - dl_b)
            ds_ref[g, :, blk:] = jnp.sum(ds_cur, axis=0)
            ds_ref[g, :, :blk] = jnp.sum(ds_prev, axis=0)
            ds_cur_b, ds_prev_b = ds_cur.astype(BF16), ds_prev.astype(BF16)
            dq = (_bdot(ds_cur_b, k) + _bdot(ds_prev_b, k_prev)) * scale
            _perm_store(dq_acc, g, dq.reshape(s, HEAD_DIM), r)
            dk = (_bdot_tn(ds_cur_b, q) + _next_blocks(_bdot_tn(ds_prev_b, q))) * scale
            dv = _bdot_tn(p_cur.astype(BF16), dob) + _next_blocks(_bdot_tn(p_prev.astype(BF16), dob))
            _perm_store(kv_acc, 0, dk.reshape(s, HEAD_DIM), r, add=True)
            _perm_store(kv_acc, 1, dv.reshape(s, HEAD_DIM), r, add=True)
        for g, out_ref in enumerate((dq1_ref, dq2_ref, dq3_ref)):
            out_ref[...] = dq_acc[g].astype(out_ref.dtype)
        dk_ref[...] = kv_acc[0].astype(dk_ref.dtype)
        dv_ref[...] = kv_acc[1].astype(dv_ref.dtype)

    def col(j):
        return pl.BlockSpec((None, s, HEAD_DIM), lambda bi, hi, j=j: (bi, 0, j * h + hi))

    head = pl.BlockSpec((None, s, HEAD_DIM), lambda bi, hi: (bi, 0, hi))
    sd = jax.ShapeDtypeStruct
    r_ins, r_in_specs, r_outs, r_out_specs, r_sems = _ride_args(ride)
    outs = pl.pallas_call(
        _riding(body, 9, 6, 3, ride, 2), name=name, grid=(b, h),
        in_specs=[col(0), col(1), col(2), col(3), col(4),
                  pl.BlockSpec((N_GROUPS, None, blk, 2 * blk), lambda bi, hi: (0, hi, 0, 0)),
                  head, pl.BlockSpec((None, None, s, 1), lambda bi, hi: (bi, hi, 0, 0)), head] + r_in_specs,
        out_specs=[head] * 5 + [pl.BlockSpec((None, None, N_GROUPS, blk, 2 * blk), lambda bi, hi: (bi, hi, 0, 0, 0))]
        + r_out_specs,
        out_shape=[sd((b, s, h * HEAD_DIM), BF16)] * 5 + [sd((b, h, N_GROUPS, blk, 2 * blk), F32)] + r_outs,
        scratch_shapes=[pltpu.VMEM((N_GROUPS, s, HEAD_DIM), F32), pltpu.VMEM((2, s, HEAD_DIM), F32),
                        pltpu.VMEM((s, 1), F32)] + r_sems,
        compiler_params=_params(("arbitrary", "arbitrary")),
    )(qkv, qkv, qkv, qkv, qkv, biasm, o, lse, do, *r_ins)
    return outs[:6], _ride_results(ride, outs[6:])


def bias_table(rel_rows, bucket_f, n_heads, name):
    g, blk, blk2 = bucket_f.shape
    h = n_heads

    def body(rb_ref, bk_ref, o_ref):
        bk = bk_ref[...]
        rb = rb_ref[...]
        acc = jnp.full((blk, blk2), NEG, F32)
        for bucket in range(REL_BUCKETS):
            acc = jnp.where(bk == float(bucket), rb[:, bucket:bucket + 1], acc)
        o_ref[...] = acc

    return pl.pallas_call(
        body, name=name, grid=(g, h),
        in_specs=[pl.BlockSpec((None, 1, 128), lambda gi, hi: (gi * h + hi, 0, 0)),
                  pl.BlockSpec((None, blk, blk2), lambda gi, hi: (gi, 0, 0))],
        out_specs=pl.BlockSpec((None, None, blk, blk2), lambda gi, hi: (gi, hi, 0, 0)),
        out_shape=jax.ShapeDtypeStruct((g, h, blk, blk2), F32),
        compiler_params=_params(("parallel", "parallel")),
    )(rel_rows, bucket_f)


def bias_grad(ds_sum, bucket_f, name):
    b, h, g, blk, blk2 = ds_sum.shape

    def body(ds_ref, bk_ref, o_ref):
        tot = jnp.sum(ds_ref[...], axis=0)
        bk = bk_ref[...]
        lane = lax.broadcasted_iota(jnp.int32, (1, 128), 1)
        vec = jnp.zeros((1, 128), F32)
        for bucket in range(REL_BUCKETS):
            val = jnp.sum(jnp.where(bk == float(bucket), tot, 0.0), keepdims=True)
            vec = vec + jnp.where(lane == bucket, val, 0.0)
        o_ref[...] = vec

    return pl.pallas_call(
        body, name=name, grid=(g, h),
        in_specs=[pl.BlockSpec((b, None, None, blk, blk2), lambda gi, hi: (0, hi, gi, 0, 0)),
                  pl.BlockSpec((None, blk, blk2), lambda gi, hi: (gi, 0, 0))],
        out_specs=pl.BlockSpec((None, 1, 128), lambda gi, hi: (gi * h + hi, 0, 0)),
        out_shape=jax.ShapeDtypeStruct((g * h, 1, 128), F32),
        compiler_params=_params(("parallel", "parallel")),
    )(ds_sum, bucket_f)


def _chip_peers():
    x, y, c = lax.axis_index("x"), lax.axis_index("y"), lax.axis_index("c")
    me = 2 * x + y
    peers = [(1 - x, y, c), (x, 1 - y, c), (1 - x, 1 - y, c)]
    peer_chip = [2 * (1 - x) + y, 2 * x + (1 - y), 2 * (1 - x) + (1 - y)]
    return me, peers, peer_chip


def _any_specs(n):
    return [pl.BlockSpec(memory_space=pl.ANY)] * n


_MID_NUM, _MID_DEN = 3, 4


class _Exchange:
    def start(self, ins, outs, sems):
        local, sends, _ = self._copies(ins, outs, sems)
        for cp in local + sends:
            cp.start()

    def mid(self, ins, outs, sems):
        pass

    def wait(self, ins, outs, sems):
        local, sends, recvs = self._copies(ins, outs, sems)
        for cp in recvs():
            cp.wait_recv()
        for cp in sends:
            cp.wait_send()
        for cp in local:
            cp.wait()


class _Gather(_Exchange):
    HALF_ROWS = 16

    def __init__(self, arrays):
        n = len(arrays)
        self.ins = list(arrays)
        self.split = [a.shape[0] % (2 * self.HALF_ROWS) == 0 for a in arrays]
        self.out_shape = [jax.ShapeDtypeStruct((N_CHIPS,) + a.shape, a.dtype) for a in arrays]
        dma = pltpu.SemaphoreType.DMA
        self.sems = [dma((3 * n,)), dma((3 * n,)), dma((n,)), dma((3 * n,)), dma((3 * n,))]

    def _half(self, i, ref, sibling=False):
        if not self.split[i]:
            return ref
        half = self.ins[i].shape[0] // 2
        c = lax.axis_index("c")
        c = 1 - c if sibling else c
        return ref.at[pl.ds(pl.multiple_of(c * half, self.HALF_ROWS), half)]

    def _plan(self, ins, outs, sems):
        send1, recv1, local_sems, send2, recv2 = sems
        me, peers, peer_chip = _chip_peers()
        x, y, c = lax.axis_index("x"), lax.axis_index("y"), lax.axis_index("c")
        n = len(ins)
        pairs = [(i, k) for i in range(n) for k in range(3)]

        def fetch(i, k, slot):
            return pltpu.make_async_remote_copy(src_ref=self._half(i, ins[i]), dst_ref=self._half(i, outs[i].at[slot]),
                                                send_sem=send1.at[3 * i + k], recv_sem=recv1.at[3 * i + k],
                                                device_id=peers[k], device_id_type=MESH)

        def share(i, k, sibling):
            part = self._half(i, outs[i].at[peer_chip[k]], sibling)
            return pltpu.make_async_remote_copy(src_ref=part, dst_ref=part, send_sem=send2.at[3 * i + k],
                                                recv_sem=recv2.at[3 * i + k], device_id=(x, y, 1 - c),
                                                device_id_type=MESH)

        split_pairs = [(i, k) for i, k in pairs if self.split[i]]
        return dict(
            local=lambda: [pltpu.make_async_copy(ins[i], outs[i].at[me], local_sems.at[i]) for i in range(n)],
            fetch_out=lambda: [fetch(i, k, me) for i, k in pairs],
            fetch_in=lambda: [fetch(i, k, peer_chip[k]) for i, k in pairs],
            share_out=lambda: [share(i, k, False) for i, k in split_pairs],
            share_in=lambda: [share(i, k, True) for i, k in split_pairs])

    def start(self, ins, outs, sems):
        plan = self._plan(ins, outs, sems)
        for cp in plan["local"]() + plan["fetch_out"]():
            cp.start()

    def mid(self, ins, outs, sems):
        plan = self._plan(ins, outs, sems)
        for cp in plan["fetch_in"]():
            cp.wait_recv()
        for cp in plan["share_out"]():
            cp.start()

    def wait(self, ins, outs, sems):
        plan = self._plan(ins, outs, sems)
        for cp in plan["share_in"]():
            cp.wait_recv()
        for cp in plan["fetch_out"]() + plan["share_out"]():
            cp.wait_send()
        for cp in plan["local"]():
            cp.wait()


class _Scatter(_Exchange):
    def __init__(self, slabs, whole=()):
        self.n_slabs = len(slabs)
        self.ins = list(slabs) + list(whole)
        n = len(self.ins)
        self.out_shape = [jax.ShapeDtypeStruct(a.shape, a.dtype) for a in slabs] \
            + [jax.ShapeDtypeStruct((N_CHIPS,) + a.shape, a.dtype) for a in whole]
        self.sems = [pltpu.SemaphoreType.DMA((3 * n,)), pltpu.SemaphoreType.DMA((3 * n,)), pltpu.SemaphoreType.DMA((n,))]

    def _copies(self, ins, outs, sems):
        send_sems, recv_sems, local_sems = sems
        me, peers, peer_chip = _chip_peers()
        n = len(ins)

        def src(i, chip):
            return ins[i].at[chip] if i < self.n_slabs else ins[i]

        def remote(i, k, src_chip, slot):
            return pltpu.make_async_remote_copy(src_ref=src(i, src_chip), dst_ref=outs[i].at[slot],
                                                send_sem=send_sems.at[3 * i + k], recv_sem=recv_sems.at[3 * i + k],
                                                device_id=peers[k], device_id_type=MESH)

        local = [pltpu.make_async_copy(src(i, me), outs[i].at[me], local_sems.at[i]) for i in range(n)]
        sends = [remote(i, k, peer_chip[k], me) for i in range(n) for k in range(3)]
        return local, sends, lambda: [remote(i, k, me, peer_chip[k]) for i in range(n) for k in range(3)]


class _Swap(_Exchange):
    def __init__(self, arrays):
        n = len(arrays)
        self.ins = list(arrays)
        self.out_shape = [jax.ShapeDtypeStruct(a.shape, a.dtype) for a in arrays]
        self.sems = [pltpu.SemaphoreType.DMA((n,)), pltpu.SemaphoreType.DMA((n,))]

    def _copies(self, ins, outs, sems):
        send_sems, recv_sems = sems
        x, y, c = lax.axis_index("x"), lax.axis_index("y"), lax.axis_index("c")
        cps = [pltpu.make_async_remote_copy(src_ref=ins[i], dst_ref=outs[i], send_sem=send_sems.at[i],
                                            recv_sem=recv_sems.at[i], device_id=(x, y, 1 - c), device_id_type=MESH)
               for i in range(len(ins))]
        return [], cps, lambda: cps


def _riding(body, n_in, n_out, n_scratch, ride, rank):
    if not ride:
        return body
    r_in = sum(len(e.ins) for e in ride)
    r_out = sum(len(e.out_shape) for e in ride)

    def split(refs, sizes):
        out, a = [], 0
        for sz in sizes:
            out.append(refs[a:a + sz])
            a += sz
        return out

    def wrapped(*refs):
        a = 0
        parts = []
        for sz in (n_in, r_in, n_out, r_out, n_scratch):
            parts.append(refs[a:a + sz])
            a += sz
        own_in, ex_in, own_out, ex_out, own_scratch = parts
        ex_sems = refs[a:]
        ins = split(ex_in, [len(e.ins) for e in ride])
        outs = split(ex_out, [len(e.out_shape) for e in ride])
        sems = split(ex_sems, [len(e.sems) for e in ride])
        if rank:
            step, total = 0, 1
            for d in range(rank):
                step = step * pl.num_programs(d) + pl.program_id(d)
                total = total * pl.num_programs(d)

            @pl.when(step == 0)
            def _():
                for e, i, o, s in zip(ride, ins, outs, sems):
                    e.start(i, o, s)

            body(*own_in, *own_out, *own_scratch)

            @pl.when(step == (total * _MID_NUM) // _MID_DEN)
            def _():
                for e, i, o, s in zip(ride, ins, outs, sems):
                    e.mid(i, o, s)

            @pl.when(step == total - 1)
            def _():
                for e, i, o, s in zip(ride, ins, outs, sems):
                    e.wait(i, o, s)
        else:
            for phase in ("start", "mid", "wait"):
                for e, i, o, s in zip(ride, ins, outs, sems):
                    getattr(e, phase)(i, o, s)

    return wrapped


def _ride_args(ride):
    ins = [a for e in ride for a in e.ins]
    outs = [s for e in ride for s in e.out_shape]
    sems = [s for e in ride for s in e.sems]
    return ins, _any_specs(len(ins)), outs, _any_specs(len(outs)), sems


def _ride_results(ride, flat):
    out, a = [], 0
    for e in ride:
        out.append(list(flat[a:a + len(e.out_shape)]))
        a += len(e.out_shape)
    return out


def exchange(ride, name):
    ins, in_specs, outs, out_specs, sems = _ride_args(ride)
    res = pl.pallas_call(
        _riding(lambda: None, 0, 0, 0, ride, 0), name=name,
        in_specs=in_specs, out_specs=out_specs, out_shape=outs, scratch_shapes=sems,
    )(*ins)
    return _ride_results(ride, res)


def _sum_slots(ref):
    acc = ref[0].astype(F32)
    for j in range(1, ref.shape[0]):
        acc = acc + ref[j].astype(F32)
    return acc


def sum_pairs(mine, other, name, tr=176):
    n, r, w = mine.shape
    tr = _tile(r, tr)

    def body(a_ref, b_ref, o_ref):
        o_ref[...] = _sum_slots(a_ref) + _sum_slots(b_ref)

    spec = pl.BlockSpec((n, tr, w), lambda i: (0, i, 0))
    return pl.pallas_call(
        body, name=name, grid=(r // tr,),
        in_specs=[spec, spec], out_specs=_rows(tr, w),
        out_shape=jax.ShapeDtypeStruct((r, w), F32),
        compiler_params=_params(("parallel",)),
    )(mine, other)


def adamw(w, m, v, gs, name, tr=256):
    r, c = w.shape
    tr = r if r % 8 else _tile(r, tr)
    c1 = 1.0 - ADAM_B1 ** ADAM_STEP
    c2 = 1.0 - ADAM_B2 ** ADAM_STEP
    ng = len(gs)

    def body(w_ref, m_ref, v_ref, *refs):
        g_refs, (g_ref, d_ref, nm_ref, nv_ref) = refs[:ng], refs[ng:]
        g = g_refs[0][...] if ng == 1 else _sum_slots(g_refs[0]) + _sum_slots(g_refs[1])
        nm = ADAM_B1 * m_ref[...] + (1.0 - ADAM_B1) * g
        nv = ADAM_B2 * v_ref[...] + (1.0 - ADAM_B2) * (g * g)
        g_ref[...] = g
        nm_ref[...] = nm
        nv_ref[...] = nv
        d_ref[...] = (-ADAM_LR) * ((nm / c1) / (jnp.sqrt(nv / c2) + ADAM_EPS) + ADAM_WD * w_ref[...])

    spec = _rows(tr, c)
    gspec = spec if ng == 1 else pl.BlockSpec((N_CHIPS, tr, c), lambda i: (0, i, 0))
    return pl.pallas_call(
        body, name=name, grid=(r // tr,),
        in_specs=[spec] * 3 + [gspec] * ng, out_specs=[spec] * 4,
        out_shape=[jax.ShapeDtypeStruct((r, c), F32)] * 4,
        compiler_params=_params(("parallel",)),
    )(w, m, v, *gs)


_PARAMS = (
    ("rel_bias", None), ("norm_mix_pre", None), ("norm_mix_post", None), ("w_in", 1), ("conv_rnn_w", 1),
    ("conv_rnn_b", None), ("w_rg_a", None), ("b_rg_a", None), ("w_rg_x", None), ("b_rg_x", None),
    ("lru_lambda", None), ("w_branch_rnn", 0), ("w_branch_att", 1), ("w_out", 0), ("norm_ffn_pre", None),
    ("norm_ffn_post", None), ("w_ffn_gate", 1), ("w_ffn_up", 1), ("conv_ffn_w", 1), ("conv_ffn_b", None),
    ("w_ffn_down", 0),
)
_SMALL = 65536


def _as2d(a):
    a = a[0] if a.shape[0] == 1 and a.ndim >= 3 else a
    return a.reshape(-1, a.shape[-1]) if a.ndim == 3 else a


def _pack(pieces, dtype):
    flat = jnp.concatenate([p.astype(dtype).reshape(-1) for p in pieces])
    unit = PACK_W * PACK_ROWS
    pad = (-flat.shape[0]) % unit
    flat = jnp.pad(flat, (0, pad))
    return flat.reshape(-1, PACK_W)


def _unpack(buf, shapes):
    flat = buf.reshape(-1)
    out, off = [], 0
    for shp in shapes:
        n = int(np.prod(shp))
        out.append(flat[off:off + n].reshape(shp))
        off += n
    return out


def _join(slots, ax):
    if ax == 0:
        return slots.reshape(-1, slots.shape[-1])
    return jnp.transpose(slots, (1, 0, 2)).reshape(slots.shape[1], -1)


def _cut(full, ax):
    if ax == 0:
        return full.reshape(N_CHIPS, -1, full.shape[-1])
    return jnp.transpose(full.reshape(full.shape[0], N_CHIPS, -1), (1, 0, 2))


def kernel(x, rel_bias, norm_mix_pre, norm_mix_post, w_in, conv_rnn_w, conv_rnn_b, w_rg_a, b_rg_a, w_rg_x, b_rg_x, lru_lambda, w_branch_rnn, w_branch_att, w_out, norm_ffn_pre, norm_ffn_post, w_ffn_gate, w_ffn_up, conv_ffn_w, conv_ffn_b, w_ffn_down, loss_target, m_rel_bias, m_norm_mix_pre, m_norm_mix_post, m_w_in, m_conv_rnn_w, m_conv_rnn_b, m_w_rg_a, m_b_rg_a, m_w_rg_x, m_b_rg_x, m_lru_lambda, m_w_branch_rnn, m_w_branch_att, m_w_out, m_norm_ffn_pre, m_norm_ffn_post, m_w_ffn_gate, m_w_ffn_up, m_conv_ffn_w, m_conv_ffn_b, m_w_ffn_down, v_rel_bias, v_norm_mix_pre, v_norm_mix_post, v_w_in, v_conv_rnn_w, v_conv_rnn_b, v_w_rg_a, v_b_rg_a, v_w_rg_x, v_b_rg_x, v_lru_lambda, v_w_branch_rnn, v_w_branch_att, v_w_out, v_norm_ffn_pre, v_norm_ffn_post, v_w_ffn_gate, v_w_ffn_up, v_conv_ffn_w, v_conv_ffn_b, v_w_ffn_down):
    args = dict(locals())
    names = [n for n, _ in _PARAMS]
    axis = dict(_PARAMS)
    w_loc = {n: args[n] for n in names}
    m_loc = {n: args["m_" + n] for n in names}
    v_loc = {n: args["v_" + n] for n in names}
    sharded = [n for n in names if axis[n] is not None]
    replicated = [n for n in names if axis[n] is None]

    big = [n for n in sharded if w_loc[n].size >= _SMALL]
    small_sharded = [n for n in sharded if n not in big]
    small = replicated + small_sharded

    first = ["w_in"] + small_sharded
    srcs = [_as2d(w_loc[n]).astype(BF16) if n in big else _as2d(w_loc[n]) for n in first]
    (gathered,) = exchange([_Gather(srcs)], "gather_first")
    p = {n: _join(a, axis[n]) for n, a in zip(first, gathered)}
    for n in replicated:
        p[n] = _as2d(w_loc[n])
    shards = {n: _as2d(w_loc[n]).astype(BF16) for n in big if n not in first}

    received, sibling, g_small, loss_part = _local_step(x, loss_target, p, shards)

    pack = _pack([g_small[n] for n in small], BF16)
    ((received["small"],),) = exchange([_Scatter([], [pack])], "scatter_small")
    late = [n for n in received if n not in sibling]
    (swapped,) = exchange([_Swap([received[n] for n in late])], "swap_last")
    sibling.update(zip(late, swapped))
    small_sum = sum_pairs(received["small"], sibling["small"], "sum_small")
    g_tot = dict(zip(small, _unpack(small_sum, [g_small[n].shape for n in small])))
    chip = 2 * lax.axis_index("x") + lax.axis_index("y")
    for n in small_sharded:
        size = g_tot[n].shape[axis[n]] // N_CHIPS
        g_tot[n] = lax.dynamic_slice_in_dim(g_tot[n], chip * size, size, axis=axis[n])

    out_g, out_d, out_m, out_v = {}, {}, {}, {}
    for i, n in enumerate(names):
        shp = w_loc[n].shape
        gs = (received[n], sibling[n]) if n in big else (g_tot[n],)
        g, d, nm, nv = adamw(_as2d(w_loc[n]), _as2d(m_loc[n]), _as2d(v_loc[n]), gs, "adamw_" + n)
        out_g[n], out_d[n], out_m[n], out_v[n] = (t.reshape(shp) for t in (g, d, nm, nv))

    d_model = x.shape[-1]
    loss = lax.psum(0.5 * jnp.sum(loss_part) / d_model, ("x", "y", "c"))
    grad_x = g_small["x"]
    return (loss, grad_x, *[out_g[n] for n in names], *[out_d[n] for n in names],
            *[out_m[n] for n in names], *[out_v[n] for n in names])


def _local_step(x, target, p, shards):
    axis = dict(_PARAMS)
    b, s, d = x.shape
    t = b * s
    rnn = p["b_rg_a"].shape[1]
    ffn = p["conv_ffn_b"].shape[1]
    nbk = rnn // p["w_rg_a"].shape[1]
    hkv = (p["w_in"].shape[1] - rnn - 2 * d) // (N_GROUPS + 2)
    h = hkv // HEAD_DIM
    nq = N_GROUPS * hkv

    x2 = x.reshape(t, d)
    tgt = target.reshape(t, d)
    w_in = p["w_in"]
    in_splits = (rnn, nq + 2 * hkv, 2 * d)
    wa = p["w_rg_a"].reshape(nbk, -1, p["w_rg_a"].shape[1]).astype(BF16)
    wx = p["w_rg_x"].reshape(nbk, -1, p["w_rg_x"].shape[1]).astype(BF16)
    cw_r, cb_r = p["conv_rnn_w"], p["conv_rnn_b"]
    cw_f, cb_f = p["conv_ffn_w"], p["conv_ffn_b"]

    masks, buckets = zip(*[_band(w_, r_) for w_, r_ in DILATED])
    bucket_f = jnp.asarray(np.where(np.stack(masks), np.stack(buckets), -1).astype(np.float32))
    rel_rows = jnp.pad(p["rel_bias"].T, ((0, 0), (0, 128 - REL_BUCKETS)))[:, None, :]
    biasm = bias_table(rel_rows, bucket_f, h, "bias_table")

    early = ["w_branch_rnn", "w_branch_att", "w_out"]
    hn1, (xr, qkv, gts), (got,) = norm_mm(x2, p["norm_mix_pre"], [w_in], [in_splits], "in_proj",
                                          ride=[_Gather([shards[n] for n in early])])
    p.update({n: _join(a, axis[n]) for n, a in zip(early, got)})
    xr3 = xr.reshape(b, s, rnn)
    y_rnn, (got,) = rglru_fwd(xr3, cw_r, cb_r, wa, p["b_rg_a"], wx, p["b_rg_x"], p["lru_lambda"], "rglru_fwd",
                              ride=[_Gather([shards[n] for n in ("w_ffn_gate", "w_ffn_up")])])
    p.update({n: _join(a, axis[n]) for n, a in zip(("w_ffn_gate", "w_ffn_up"), got)})
    qkv3 = qkv.reshape(b, s, -1)
    o_att, lse, ((got,),) = attn_fwd(qkv3, biasm, h, "attn_fwd", ride=[_Gather([shards["w_ffn_down"]])])
    p["w_ffn_down"] = _join(got, axis["w_ffn_down"])
    merged, br, ba = merge_fwd(y_rnn.reshape(t, rnn), o_att.reshape(t, hkv), gts, p["w_branch_rnn"],
                               p["w_branch_att"], "merge_fwd")
    mix, h1 = mm_norm_res(merged, p["w_out"], p["norm_mix_post"], x2, "out_proj")
    hn2, (gate_pre, up), _ = norm_mm(h1, p["norm_ffn_pre"], [p["w_ffn_gate"], p["w_ffn_up"]], [(ffn,), (ffn,)], "ffn_in")
    act = ffn_act(gate_pre.reshape(b, s, ffn), up.reshape(b, s, ffn), cw_f, cb_f, "ffn_act")
    ff, y = mm_norm_res(act.reshape(t, ffn), p["w_ffn_down"], p["norm_ffn_post"], h1, "ffn_down")

    g, gb = {}, {}
    recv, sib = {}, {}

    def rows4(a):
        return a.reshape(N_CHIPS, -1, a.shape[-1])

    dy, dff, g["norm_ffn_post"], loss_part = loss_norm_bwd(y, tgt, ff, p["norm_ffn_post"], "loss_bwd")
    dact = mm_nt([([dff], p["w_ffn_down"])], F32, "ffn_down_dx")
    gb["w_ffn_down"] = rows4(mm_tn(act.reshape(t, ffn), [dff], "ffn_down_dw"))
    dgp, dup, g["conv_ffn_w"], g["conv_ffn_b"] = ffn_bwd(dact.reshape(b, s, ffn), gate_pre.reshape(b, s, ffn),
                                                        up.reshape(b, s, ffn), cw_f, cb_f, "ffn_bwd")
    dgp, dup = dgp.reshape(t, ffn), dup.reshape(t, ffn)
    dhn2, ((recv["w_ffn_down"],),) = mm_nt([([dgp], p["w_ffn_gate"]), ([dup], p["w_ffn_up"])], F32, "ffn_in_dx",
                                           ride=[_Scatter([gb["w_ffn_down"]])])
    gb["w_ffn_gate"] = mm_tn(hn2, [dgp], "ffn_gate_dw", col_shards=N_CHIPS)
    gb["w_ffn_up"] = mm_tn(hn2, [dup], "ffn_up_dw", col_shards=N_CHIPS)
    dh1, g["norm_ffn_pre"] = norm_bwd(dhn2, h1, p["norm_ffn_pre"], dy, F32, "ffn_norm_bwd")
    dmix, g["norm_mix_post"] = norm_bwd(dh1, mix, p["norm_mix_post"], None, BF16, "mix_norm_bwd")
    dmerged = mm_nt([([dmix], p["w_out"])], F32, "out_proj_dx")
    gb["w_out"] = rows4(mm_tn(merged, [dmix], "out_proj_dw"))
    dbr, dba, dgts = merge_bwd(dmerged, gts, br, ba, "merge_bwd")
    dy_rnn = mm_nt([([dbr], p["w_branch_rnn"])], F32, "branch_rnn_dx")
    do_att = mm_nt([([dba], p["w_branch_att"])], F32, "branch_att_dx")
    gb["w_branch_rnn"] = rows4(mm_tn(y_rnn.reshape(t, rnn), [dbr], "branch_rnn_dw"))
    gb["w_branch_att"] = mm_tn(o_att.reshape(t, hkv), [dba], "branch_att_dw", col_shards=N_CHIPS)
    ffn_in = ["w_ffn_gate", "w_ffn_up"]
    (dxr, g["conv_rnn_w"], g["conv_rnn_b"], dwa, g["b_rg_a"], dwx, g["b_rg_x"], g["lru_lambda"]), (got,) = rglru_bwd(
        xr3, y_rnn, dy_rnn.reshape(b, s, rnn), cw_r, cb_r, wa, p["b_rg_a"], wx, p["b_rg_x"], p["lru_lambda"], "rglru_bwd",
        ride=[_Scatter([gb[n] for n in ffn_in])])
    recv.update(zip(ffn_in, got))
    g["w_rg_a"] = dwa.reshape(p["w_rg_a"].shape)
    g["w_rg_x"] = dwx.reshape(p["w_rg_x"].shape)
    mid = ["w_out", "w_branch_rnn", "w_branch_att"]
    early_recv = ["w_ffn_down"] + ffn_in
    (dq1, dq2, dq3, dk, dv, ds_sum), (got, swapped) = attn_bwd(
        qkv3, biasm, o_att, lse, do_att.reshape(b, s, hkv), h, "attn_bwd",
        ride=[_Scatter([gb[n] for n in mid]), _Swap([recv[n] for n in early_recv])])
    recv.update(zip(mid, got))
    sib.update(zip(early_recv, swapped))
    rows = bias_grad(ds_sum, bucket_f, "bias_grad")
    g["rel_bias"] = rows[:, 0, :REL_BUCKETS].T
    dproj = [dxr.reshape(t, rnn)] + [a.reshape(t, hkv) for a in (dq1, dq2, dq3, dk, dv)] + [dgts]
    dw_a = mm_tn(hn1, dproj[:4], "in_proj_dw_a")[0]
    dw_b = mm_tn(hn1, dproj[4:], "in_proj_dw_b")[0]
    gb["w_in"] = _cut(jnp.concatenate([dw_a, dw_b], axis=1), 1)
    dhn1, ((recv["w_in"],), got) = mm_nt([(dproj, w_in)], F32, "in_proj_dx",
                                         ride=[_Scatter([gb["w_in"]]), _Swap([recv[n] for n in mid])])
    sib.update(zip(mid, got))
    dx, g["norm_mix_pre"] = norm_bwd(dhn1, x2, p["norm_mix_pre"], dh1, F32, "in_norm_bwd")
    g["x"] = dx.reshape(b, s, d)
    return recv, sib, g, loss_part
```

```python
import functools
import math

import numpy as np
import jax
import jax.numpy as jnp
from jax import lax
from jax.experimental import pallas as pl
from jax.experimental.pallas import tpu as pltpu

F32 = jnp.float32
BF16 = jnp.bfloat16

EPS = 1e-6
HEAD_DIM = 128
ATTN_BLOCK = 128
DILATED = ((128, 1), (512, 4), (2048, 16))
N_GROUPS = len(DILATED)
REL_BUCKETS = 32
REL_MAX_DIST = 2048
LRU_C = 8.0
NEG = -1e30

ADAM_LR = 0.001
ADAM_B1 = 0.9
ADAM_B2 = 0.999
ADAM_EPS = 1e-08
ADAM_WD = 0.01
ADAM_STEP = 10

N_CHIPS = 4
PACK_W = 1024
PACK_ROWS = 16
VMEM_LIMIT = 56 * 1024 * 1024
MESH = pl.DeviceIdType.MESH


def _params(sem=None):
    return pltpu.CompilerParams(dimension_semantics=sem, vmem_limit_bytes=VMEM_LIMIT)


def _dot(a, b):
    return jnp.dot(a, b, preferred_element_type=F32)


def _dot_nt(a, b):
    return lax.dot_general(a, b, (((1,), (1,)), ((), ())), preferred_element_type=F32)


def _dot_tn(a, b):
    return lax.dot_general(a, b, (((0,), (0,)), ((), ())), preferred_element_type=F32)


def _sig(x):
    return 0.5 * jnp.tanh(0.5 * x) + 0.5


def _rows(tm, w):
    return pl.BlockSpec((tm, w), lambda i: (i, 0))


def _whole(shape):
    nd = len(shape)
    return pl.BlockSpec(tuple(shape), lambda *_: (0,) * nd)


def _tile(t, want):
    while t % want:
        want //= 2
    return want


def norm_mm(x, g, ws, splits, name, ride=(), tm=256):
    t, d = x.shape
    tm = _tile(t, tm)
    nw = len(ws)
    widths = [n for sp in splits for n in sp]

    def body(x_ref, g_ref, *refs):
        w_refs, hn_ref, o_refs = refs[:nw], refs[nw], refs[nw + 1:]
        xv = x_ref[...]
        inv = lax.rsqrt(jnp.mean(xv * xv, axis=-1, keepdims=True) + EPS)
        hn = (xv * inv * g_ref[...]).astype(BF16)
        hn_ref[...] = hn
        o = 0
        for w_ref, sp in zip(w_refs, splits):
            off = 0
            for n in sp:
                o_refs[o][...] = _dot(hn, w_ref[:, off:off + n])
                off += n
                o += 1

    r_ins, r_in_specs, r_outs, r_out_specs, r_sems = _ride_args(ride)
    n_out = 1 + len(widths)
    outs = pl.pallas_call(
        _riding(body, 2 + nw, n_out, 0, ride, 1), name=name, grid=(t // tm,),
        in_specs=[_rows(tm, d), _whole(g.shape)] + [_whole(w.shape) for w in ws] + r_in_specs,
        out_specs=[_rows(tm, d)] + [_rows(tm, n) for n in widths] + r_out_specs,
        out_shape=[jax.ShapeDtypeStruct((t, d), BF16)] + [jax.ShapeDtypeStruct((t, n), F32) for n in widths] + r_outs,
        scratch_shapes=r_sems,
        compiler_params=_params(("arbitrary",)),
    )(x, g, *ws, *r_ins)
    return outs[0], outs[1:n_out], _ride_results(ride, outs[n_out:])


def mm_nt(groups, out_dtype, name, ride=(), tm=256):
    dys_all = [dy for dys, _ in groups for dy in dys]
    ws = [w for _, w in groups]
    t = dys_all[0].shape[0]
    k = ws[0].shape[0]
    tm = _tile(t, tm)
    n = len(dys_all)

    def body(*refs):
        dy_refs, w_refs, o_ref = refs[:n], refs[n:n + len(ws)], refs[n + len(ws)]
        acc = None
        i = 0
        for (dys, _), w_ref in zip(groups, w_refs):
            off = 0
            for dy in dys:
                width = dy.shape[1]
                part = _dot_nt(dy_refs[i][...].astype(BF16), w_ref[:, off:off + width])
                acc = part if acc is None else acc + part
                off += width
                i += 1
        o_ref[...] = acc.astype(o_ref.dtype)

    r_ins, r_in_specs, r_outs, r_out_specs, r_sems = _ride_args(ride)
    outs = pl.pallas_call(
        _riding(body, n + len(ws), 1, 0, ride, 1), name=name, grid=(t // tm,),
        in_specs=[_rows(tm, dy.shape[1]) for dy in dys_all] + [_whole(w.shape) for w in ws] + r_in_specs,
        out_specs=[_rows(tm, k)] + r_out_specs,
        out_shape=[jax.ShapeDtypeStruct((t, k), out_dtype)] + r_outs,
        scratch_shapes=r_sems,
        compiler_params=_params(("arbitrary",) if ride else ("parallel",)),
    )(*dys_all, *ws, *r_ins)
    return (outs[0], _ride_results(ride, outs[1:])) if ride else outs[0]


def mm_tn(a, dys, name, col_shards=1, tm=512):
    t, k = a.shape
    tm = _tile(t, tm)
    n = len(dys)
    ntot = sum(dy.shape[1] for dy in dys)
    wsh = ntot // col_shards

    def body(a_ref, *refs):
        dy_refs, o_ref, acc = refs[:n], refs[n], refs[n + 1]

        @pl.when(pl.program_id(0) == 0)
        def _():
            acc[...] = jnp.zeros(acc.shape, F32)

        av = a_ref[...].astype(BF16)
        off = 0
        for dy_ref in dy_refs:
            width = dy_ref.shape[1]
            acc[:, off:off + width] += _dot_tn(av, dy_ref[...].astype(BF16))
            off += width

        @pl.when(pl.program_id(0) == pl.num_programs(0) - 1)
        def _():
            for j in range(col_shards):
                o_ref[j] = acc[:, j * wsh:(j + 1) * wsh].astype(o_ref.dtype)

    return pl.pallas_call(
        body, name=name, grid=(t // tm,),
        in_specs=[_rows(tm, k)] + [_rows(tm, dy.shape[1]) for dy in dys],
        out_specs=_whole((col_shards, k, wsh)),
        out_shape=jax.ShapeDtypeStruct((col_shards, k, wsh), BF16),
        scratch_shapes=[pltpu.VMEM((k, ntot), F32)],
        compiler_params=_params(("arbitrary",)),
    )(a, *dys)


def mm_norm_res(a, w, g, resid, name, tm=256):
    t, k = a.shape
    d = w.shape[1]
    tm = _tile(t, tm)

    def body(a_ref, w_ref, g_ref, r_ref, p_ref, o_ref):
        prod = _dot(a_ref[...], w_ref[...])
        p_ref[...] = prod
        inv = lax.rsqrt(jnp.mean(prod * prod, axis=-1, keepdims=True) + EPS)
        o_ref[...] = r_ref[...] + prod * inv * g_ref[...]

    return pl.pallas_call(
        body, name=name, grid=(t // tm,),
        in_specs=[_rows(tm, k), _whole(w.shape), _whole(g.shape), _rows(tm, d)],
        out_specs=[_rows(tm, d), _rows(tm, d)],
        out_shape=[jax.ShapeDtypeStruct((t, d), F32)] * 2,
        compiler_params=_params(("parallel",)),
    )(a, w, g, resid)


def _rms_bwd(dz, u, g):
    d = u.shape[-1]
    inv = lax.rsqrt(jnp.mean(u * u, axis=-1, keepdims=True) + EPS)
    dzg = dz * g
    proj = jnp.sum(dzg * u, axis=-1, keepdims=True) * (1.0 / d)
    du = inv * (dzg - u * (inv * inv) * proj)
    dg_rows = dz * u * inv
    return du, dg_rows


def norm_bwd(dz, u, g, add, out_dtype, name, tm=256):
    t, d = u.shape
    tm = _tile(t, tm)
    has_add = add is not None

    def body(*refs):
        if has_add:
            dz_ref, u_ref, g_ref, add_ref, du_ref, dg_ref = refs
        else:
            dz_ref, u_ref, g_ref, du_ref, dg_ref = refs

        @pl.when(pl.program_id(0) == 0)
        def _():
            dg_ref[...] = jnp.zeros(dg_ref.shape, F32)

        du, dg_rows = _rms_bwd(dz_ref[...].astype(F32), u_ref[...], g_ref[...])
        if has_add:
            du = du + add_ref[...]
        du_ref[...] = du.astype(du_ref.dtype)
        dg_ref[...] += jnp.sum(dg_rows, axis=0, keepdims=True)

    ins = [dz, u, g] + ([add] if has_add else [])
    return pl.pallas_call(
        body, name=name, grid=(t // tm,),
        in_specs=[_rows(tm, d), _rows(tm, d), _whole(g.shape)] + ([_rows(tm, d)] if has_add else []),
        out_specs=[_rows(tm, d), _whole((1, d))],
        out_shape=[jax.ShapeDtypeStruct((t, d), out_dtype), jax.ShapeDtypeStruct((1, d), F32)],
        compiler_params=_params(("arbitrary",)),
    )(*ins)


def ffn_down_loss(act, wd, g, h1, target, name, tm=256):
    t, f = act.shape
    d = wd.shape[1]
    tm = _tile(t, tm)

    def body(a_ref, w_ref, g_ref, h_ref, t_ref, dy_ref, dff_ref, dact_ref, dg_ref, loss_ref):
        @pl.when(pl.program_id(0) == 0)
        def _():
            dg_ref[...] = jnp.zeros(dg_ref.shape, F32)
            loss_ref[...] = jnp.zeros(loss_ref.shape, F32)

        wv = w_ref[...]
        gv = g_ref[...]
        ff = _dot(a_ref[...], wv)
        inv = lax.rsqrt(jnp.mean(ff * ff, axis=-1, keepdims=True) + EPS)
        err = h_ref[...] + ff * inv * gv - t_ref[...]
        loss_ref[...] += jnp.sum(err * err, axis=0, keepdims=True)
        dy = err * (1.0 / d)
        dy_ref[...] = dy
        du, dg_rows = _rms_bwd(dy, ff, gv)
        dff = du.astype(BF16)
        dff_ref[...] = dff
        dg_ref[...] += jnp.sum(dg_rows, axis=0, keepdims=True)
        dact_ref[...] = _dot_nt(dff, wv)

    return pl.pallas_call(
        body, name=name, grid=(t // tm,),
        in_specs=[_rows(tm, f), _whole(wd.shape), _whole(g.shape), _rows(tm, d), _rows(tm, d)],
        out_specs=[_rows(tm, d), _rows(tm, d), _rows(tm, f), _whole((1, d)), _whole((1, d))],
        out_shape=[jax.ShapeDtypeStruct((t, d), F32), jax.ShapeDtypeStruct((t, d), BF16),
                   jax.ShapeDtypeStruct((t, f), F32), jax.ShapeDtypeStruct((1, d), F32),
                   jax.ShapeDtypeStruct((1, d), F32)],
        compiler_params=_params(("arbitrary",)),
    )(act, wd, g, h1, target)


def merge_fwd(y_rnn, o_att, gts, w_br, w_ba, name, tm=256):
    t = y_rnn.shape[0]
    d = w_br.shape[1]
    tm = _tile(t, tm)

    def body(y_ref, o_ref, g_ref, wbr_ref, wba_ref, m_ref, br_ref, ba_ref):
        br = _dot(y_ref[...].astype(BF16), wbr_ref[...])
        ba = _dot(o_ref[...].astype(BF16), wba_ref[...])
        gv = g_ref[...]
        m_ref[...] = (_sig(gv[:, :d]) * br + _sig(gv[:, d:]) * ba).astype(BF16)
        br_ref[...] = br
        ba_ref[...] = ba

    return pl.pallas_call(
        body, name=name, grid=(t // tm,),
        in_specs=[_rows(tm, y_rnn.shape[1]), _rows(tm, o_att.shape[1]), _rows(tm, 2 * d),
                  _whole(w_br.shape), _whole(w_ba.shape)],
        out_specs=[_rows(tm, d)] * 3,
        out_shape=[jax.ShapeDtypeStruct((t, d), BF16), jax.ShapeDtypeStruct((t, d), F32),
                   jax.ShapeDtypeStruct((t, d), F32)],
        compiler_params=_params(("parallel",)),
    )(y_rnn, o_att, gts, w_br, w_ba)


def merge_bwd(dmerged, gts, br, ba, name, tm=256):
    t, d = dmerged.shape
    tm = _tile(t, tm)

    def body(dm_ref, g_ref, br_ref, ba_ref, dbr_ref, dba_ref, dg_ref):
        dm = dm_ref[...]
        gv = g_ref[...]
        sr = _sig(gv[:, :d])
        sa = _sig(gv[:, d:])
        dbr_ref[...] = (dm * sr).astype(BF16)
        dba_ref[...] = (dm * sa).astype(BF16)
        dg_ref[:, :d] = (dm * br_ref[...] * sr * (1.0 - sr)).astype(BF16)
        dg_ref[:, d:] = (dm * ba_ref[...] * sa * (1.0 - sa)).astype(BF16)

    return pl.pallas_call(
        body, name=name, grid=(t // tm,),
        in_specs=[_rows(tm, d), _rows(tm, 2 * d), _rows(tm, d), _rows(tm, d)],
        out_specs=[_rows(tm, d), _rows(tm, d), _rows(tm, 2 * d)],
        out_shape=[jax.ShapeDtypeStruct((t, d), BF16), jax.ShapeDtypeStruct((t, d), BF16),
                   jax.ShapeDtypeStruct((t, 2 * d), BF16)],
        compiler_params=_params(("parallel",)),
    )(dmerged, gts, br, ba)


def _shift_dn(x, d, fill, row):
    return jnp.where(row >= d, pltpu.roll(x, d, 0), fill)


def _shift_up(x, d, fill, row):
    s = x.shape[0]
    return jnp.where(row < s - d, pltpu.roll(x, s - d, 0), fill)


def _conv_fwd(x, w, b, row):
    kk = w.shape[0]
    y = b + w[kk - 1:kk, :] * x
    for j in range(1, kk):
        y = y + w[kk - 1 - j:kk - j, :] * _shift_dn(x, j, 0.0, row)
    return y


def _conv_bwd(dy, x, w, row):
    kk = w.shape[0]
    dx = w[kk - 1:kk, :] * dy
    dws = [None] * kk
    dws[kk - 1] = jnp.sum(dy * x, axis=0, keepdims=True)
    for j in range(1, kk):
        dx = dx + w[kk - 1 - j:kk - j, :] * _shift_up(dy, j, 0.0, row)
        dws[kk - 1 - j] = jnp.sum(dy * _shift_dn(x, j, 0.0, row), axis=0, keepdims=True)
    return dx, jnp.concatenate(dws, axis=0)


def _softplus(z):
    y = jnp.exp(-jnp.abs(z))
    u = 1.0 + y
    dd = u - 1.0
    log1p = jnp.where(dd == 0.0, y, jnp.log(u) * (y / jnp.where(dd == 0.0, 1.0, dd)))
    return jnp.maximum(z, 0.0) + log1p


def _lru_decay(xb, wa, ba, lam):
    r = _sig(_dot(xb, wa) + ba)
    sp = _softplus(-lam)
    la = (-LRU_C) * r * sp
    return r, sp, la, jnp.exp(la)


def _lru_gates(xc, wa, ba, wx, bx, lam):
    xb = xc.astype(BF16)
    r, sp, la, a = _lru_decay(xb, wa, ba, lam)
    i = _sig(_dot(xb, wx) + bx)
    one_m_a2 = jnp.tanh(-la) * (1.0 + a * a)
    inv_mult = lax.rsqrt(one_m_a2)
    return r, i, sp, a, one_m_a2 * inv_mult, inv_mult


def _seg_len(s):
    seg = -(-s // 8)
    return seg + (4 - seg % 8) % 8


def _scan_rows(a_pad, u_pad, out_pad, reverse):
    planes, rows8, lanes = a_pad.shape
    seg = rows8 // 8
    sub = lax.broadcasted_iota(jnp.int32, (planes, 8, lanes), 1)

    unroll = 4

    def rows(k, d):
        i = k * unroll + d
        return pl.ds((seg - 1 - i) if reverse else i, 8, stride=seg)

    def ends(k, carry):
        h, p = carry
        for d in range(unroll):
            a = a_pad[:, rows(k, d), :]
            h = a * h + u_pad[:, rows(k, d), :]
            p = a * p
        return h, p

    init = (jnp.zeros((planes, 8, lanes), F32), jnp.ones((planes, 8, lanes), F32))
    h_end, p_end = lax.fori_loop(0, seg // unroll, ends, init)
    start = jnp.zeros((planes, 8, lanes), F32)
    for _ in range(7):
        nxt = h_end + p_end * start
        if reverse:
            start = jnp.where(sub < 7, pltpu.roll(nxt, 7, 1), 0.0)
        else:
            start = jnp.where(sub >= 1, pltpu.roll(nxt, 1, 1), 0.0)

    def redo(k, h):
        for d in range(unroll):
            h = a_pad[:, rows(k, d), :] * h + u_pad[:, rows(k, d), :]
            out_pad[:, rows(k, d), :] = h
        return h

    lax.fori_loop(0, seg // unroll, redo, start)


def _lru_cols(c, rb):
    return 2 * rb if c % (2 * rb) == 0 else rb


def rglru_fwd(xr, cw, cb, wa, ba, wx, bx, lam, name, ride=()):
    b, s, c = xr.shape
    rb = wa.shape[1]
    kk = cw.shape[0]
    cols = _lru_cols(c, rb)
    nj = cols // rb
    seg = _seg_len(s)

    def body(x_ref, cw_ref, cb_ref, wa_ref, ba_ref, wx_ref, bx_ref, lam_ref, h_ref, a_pad, u_pad, h_pad):
        row = lax.broadcasted_iota(jnp.int32, (s, rb), 0)
        for j in range(nj):
            cs = slice(j * rb, (j + 1) * rb)
            xc = _conv_fwd(x_ref[:, cs], cw_ref[:, cs], cb_ref[:, cs], row)
            _, i, _, a, mult, _ = _lru_gates(xc, wa_ref[j], ba_ref[:, cs], wx_ref[j], bx_ref[:, cs], lam_ref[:, cs])
            a_pad[j, 0:s, :] = a
            u_pad[j, 0:s, :] = mult * (i * xc)
        a_pad[:, s:, :] = jnp.ones((nj, 8 * seg - s, rb), F32)
        u_pad[:, s:, :] = jnp.zeros((nj, 8 * seg - s, rb), F32)
        _scan_rows(a_pad, u_pad, h_pad, False)
        for j in range(nj):
            h_ref[:, j * rb:(j + 1) * rb] = h_pad[j, 0:s, :]

    vec = pl.BlockSpec((1, cols), lambda bi, n: (0, n))
    seq = pl.BlockSpec((None, s, cols), lambda bi, n: (bi, 0, n))
    mat = pl.BlockSpec((nj, rb, rb), lambda bi, n: (n, 0, 0))
    r_ins, r_in_specs, r_outs, r_out_specs, r_sems = _ride_args(ride)
    outs = pl.pallas_call(
        _riding(body, 8, 1, 3, ride, 2), name=name, grid=(b, c // cols),
        in_specs=[seq, pl.BlockSpec((kk, cols), lambda bi, n: (0, n)), vec, mat, vec, mat, vec, vec] + r_in_specs,
        out_specs=[seq] + r_out_specs,
        out_shape=[jax.ShapeDtypeStruct((b, s, c), F32)] + r_outs,
        scratch_shapes=[pltpu.VMEM((nj, 8 * seg, rb), F32)] * 3 + r_sems,
        compiler_params=_params(("arbitrary", "arbitrary")),
    )(xr, cw, cb, wa, ba, wx, bx, lam, *r_ins)
    return outs[0], _ride_results(ride, outs[1:])


def rglru_bwd(xr, h, dh, cw, cb, wa, ba, wx, bx, lam, name, ride=()):
    b, s, c = xr.shape
    nb, rb = wa.shape[0], wa.shape[1]
    kk = cw.shape[0]
    cols = _lru_cols(c, rb)
    nj = cols // rb
    seg = _seg_len(s)

    def body(x_ref, h_ref, dh_ref, cw_ref, cb_ref, wa_ref, ba_ref, wx_ref, bx_ref, lam_ref,
             dx_ref, dcw_ref, dcb_ref, dwa_ref, dba_ref, dwx_ref, dbx_ref, dlam_ref, b_pad, g_pad, l_pad):
        @pl.when(pl.program_id(1) == 0)
        def _():
            for ref in (dcw_ref, dcb_ref, dwa_ref, dba_ref, dwx_ref, dbx_ref, dlam_ref):
                ref[...] = jnp.zeros(ref.shape, F32)

        row = lax.broadcasted_iota(jnp.int32, (s, rb), 0)

        def conv(j):
            cs = slice(j * rb, (j + 1) * rb)
            return _conv_fwd(x_ref[:, cs], cw_ref[:, cs], cb_ref[:, cs], row)

        for j in range(nj):
            cs = slice(j * rb, (j + 1) * rb)
            _, _, _, a = _lru_decay(conv(j).astype(BF16), wa_ref[j], ba_ref[:, cs], lam_ref[:, cs])
            b_pad[j, 0:s, :] = _shift_up(a, 1, 0.0, row)
            g_pad[j, 0:s, :] = dh_ref[:, j * rb:(j + 1) * rb]
        b_pad[:, s:, :] = jnp.zeros((nj, 8 * seg - s, rb), F32)
        g_pad[:, s:, :] = jnp.zeros((nj, 8 * seg - s, rb), F32)
        _scan_rows(b_pad, g_pad, l_pad, True)

        for j in range(nj):
            cs = slice(j * rb, (j + 1) * rb)
            x = x_ref[:, cs]
            cwv = cw_ref[:, cs]
            wav, wxv, lamv = wa_ref[j], wx_ref[j], lam_ref[:, cs]
            xc = conv(j)
            r, i, sp, a, mult, inv_mult = _lru_gates(xc, wav, ba_ref[:, cs], wxv, bx_ref[:, cs], lamv)
            lmb = l_pad[j, 0:s, :]
            h_prev = _shift_dn(h_ref[:, cs], 1, 0.0, row)
            da = lmb * h_prev
            ixc = i * xc
            dla = da * a - (lmb * ixc) * (a * a) * inv_mult
            di = lmb * mult * xc
            dxc = lmb * mult * i
            dr = dla * ((-LRU_C) * sp)
            dsp = jnp.sum(dla * ((-LRU_C) * r), axis=0, keepdims=True)
            dga = dr * r * (1.0 - r)
            dgx = di * i * (1.0 - i)
            dga_b, dgx_b = dga.astype(BF16), dgx.astype(BF16)
            xb = xc.astype(BF16)
            dwa_ref[j] += _dot_tn(xb, dga_b)
            dwx_ref[j] += _dot_tn(xb, dgx_b)
            dba_ref[:, cs] += jnp.sum(dga, axis=0, keepdims=True)
            dbx_ref[:, cs] += jnp.sum(dgx, axis=0, keepdims=True)
            dlam_ref[:, cs] += dsp * (-_sig(-lamv))
            dxc = dxc + _dot_nt(dga_b, wav) + _dot_nt(dgx_b, wxv)
            dcb_ref[:, cs] += jnp.sum(dxc, axis=0, keepdims=True)
            dx, dcw = _conv_bwd(dxc, x, cwv, row)
            dcw_ref[:, cs] += dcw
            dx_ref[:, cs] = dx.astype(dx_ref.dtype)

    vec = pl.BlockSpec((1, cols), lambda n, bi: (0, n))
    seq = pl.BlockSpec((None, s, cols), lambda n, bi: (bi, 0, n))
    mat = pl.BlockSpec((nj, rb, rb), lambda n, bi: (n, 0, 0))
    cws = pl.BlockSpec((kk, cols), lambda n, bi: (0, n))
    sd = jax.ShapeDtypeStruct
    r_ins, r_in_specs, r_outs, r_out_specs, r_sems = _ride_args(ride)
    outs = pl.pallas_call(
        _riding(body, 10, 8, 3, ride, 2), name=name, grid=(c // cols, b),
        in_specs=[seq, seq, seq, cws, vec, mat, vec, mat, vec, vec] + r_in_specs,
        out_specs=[seq, cws, vec, mat, vec, mat, vec, vec] + r_out_specs,
        out_shape=[sd((b, s, c), BF16), sd((kk, c), F32), sd((1, c), F32), sd((nb, rb, rb), F32),
                   sd((1, c), F32), sd((nb, rb, rb), F32), sd((1, c), F32), sd((1, c), F32)] + r_outs,
        scratch_shapes=[pltpu.VMEM((nj, 8 * seg, rb), F32)] * 3 + r_sems,
        compiler_params=_params(("arbitrary", "arbitrary")),
    )(xr, h, dh, cw, cb, wa, ba, wx, bx, lam, *r_ins)
    return outs[:8], _ride_results(ride, outs[8:])


_GELU_C = math.sqrt(2.0 / math.pi)


def _gelu_parts(x):
    th = jnp.tanh(_GELU_C * (x + 0.044715 * x * x * x))
    gel = 0.5 * x * (1.0 + th)
    dgel = 0.5 * (1.0 + th) + 0.5 * x * (1.0 - th * th) * _GELU_C * (1.0 + 3 * 0.044715 * x * x)
    return gel, dgel


def ffn_in_act(x, g, wg, wu, cw, cb, seq_len, name, tm=256):
    t, d = x.shape
    f = wg.shape[1]
    kk = cw.shape[0]
    tm = _tile(seq_len, tm)
    tiles_per_seq = seq_len // tm
    keep = 8
    assert kk - 1 <= keep

    def body(x_ref, g_ref, wg_ref, wu_ref, cw_ref, cb_ref, hn_ref, gp_ref, up_ref, act_ref, tail):
        @pl.when(pl.program_id(0) % tiles_per_seq == 0)
        def _():
            tail[...] = jnp.zeros(tail.shape, F32)

        xv = x_ref[...]
        inv = lax.rsqrt(jnp.mean(xv * xv, axis=-1, keepdims=True) + EPS)
        hn = (xv * inv * g_ref[...]).astype(BF16)
        hn_ref[...] = hn
        gp = _dot(hn, wg_ref[...])
        up = _dot(hn, wu_ref[...])
        gp_ref[...] = gp
        up_ref[...] = up
        cwv = cw_ref[...]
        row = lax.broadcasted_iota(jnp.int32, (tm, 1), 0)
        gate = _conv_fwd(gp, cwv, cb_ref[...], row)
        row8 = lax.broadcasted_iota(jnp.int32, (keep, 1), 0)
        prev = tail[...]
        fix = jnp.zeros((keep, f), F32)
        for j in range(1, kk):
            fix = fix + cwv[kk - 1 - j:kk - j, :] * jnp.where(row8 < j, pltpu.roll(prev, j, 0), 0.0)
        gate = jnp.concatenate([gate[:keep] + fix, gate[keep:]], axis=0)
        tail[...] = gp[tm - keep:, :]
        gel, _ = _gelu_parts(gate)
        act_ref[...] = (gel * up).astype(BF16)

    sd = jax.ShapeDtypeStruct
    return pl.pallas_call(
        body, name=name, grid=(t // tm,),
        in_specs=[_rows(tm, d), _whole(g.shape), _whole(wg.shape), _whole(wu.shape), _whole(cw.shape), _whole(cb.shape)],
        out_specs=[_rows(tm, d), _rows(tm, f), _rows(tm, f), _rows(tm, f)],
        out_shape=[sd((t, d), BF16), sd((t, f), F32), sd((t, f), F32), sd((t, f), BF16)],
        scratch_shapes=[pltpu.VMEM((keep, f), F32)],
        compiler_params=_params(("arbitrary",)),
    )(x, g, wg, wu, cw, cb)


def ffn_bwd(dact, gate_pre, up, cw, cb, name, cbk=256):
    b, s, f = gate_pre.shape
    kk = cw.shape[0]
    cbk = _tile(f, cbk)

    def body(da_ref, g_ref, u_ref, cw_ref, cb_ref, dg_ref, du_ref, dcw_ref, dcb_ref):
        @pl.when(pl.program_id(1) == 0)
        def _():
            dcw_ref[...] = jnp.zeros(dcw_ref.shape, F32)
            dcb_ref[...] = jnp.zeros(dcb_ref.shape, F32)

        row = lax.broadcasted_iota(jnp.int32, (s, cbk), 0)
        gp = g_ref[...]
        cwv = cw_ref[...]
        gate = _conv_fwd(gp, cwv, cb_ref[...], row)
        gel, dgel = _gelu_parts(gate)
        da = da_ref[...]
        du_ref[...] = (da * gel).astype(BF16)
        dgate = da * u_ref[...] * dgel
        dcb_ref[...] += jnp.sum(dgate, axis=0, keepdims=True)
        dgp, dcw = _conv_bwd(dgate, gp, cwv, row)
        dcw_ref[...] += dcw
        dg_ref[...] = dgp.astype(BF16)

    seq = pl.BlockSpec((None, s, cbk), lambda n, bi: (bi, 0, n))
    cws = pl.BlockSpec((kk, cbk), lambda n, bi: (0, n))
    vec = pl.BlockSpec((1, cbk), lambda n, bi: (0, n))
    sd = jax.ShapeDtypeStruct
    return pl.pallas_call(
        body, name=name, grid=(f // cbk, b),
        in_specs=[seq, seq, seq, cws, vec],
        out_specs=[seq, seq, cws, vec],
        out_shape=[sd((b, s, f), BF16), sd((b, s, f), BF16), sd((kk, f), F32), sd((1, f), F32)],
        compiler_params=_params(("parallel", "arbitrary")),
    )(dact, gate_pre, up, cw, cb)


def _t5_bucket(dist):
    max_exact = REL_BUCKETS // 2
    d = np.maximum(dist, 1).astype(np.float32)
    large = max_exact + np.log(d / max_exact) / math.log(REL_MAX_DIST / max_exact) * (REL_BUCKETS - max_exact)
    large = np.minimum(large.astype(np.int32), REL_BUCKETS - 1)
    return np.where(dist < max_exact, dist, large).astype(np.int32)


def _band(window, dilation):
    qi = np.arange(ATTN_BLOCK)[:, None]
    kj = np.arange(2 * ATTN_BLOCK)[None, :]
    delta = ATTN_BLOCK + qi - kj
    mask = (delta >= 0) & (delta <= window // dilation)
    bucket = _t5_bucket(np.maximum(delta, 0) * dilation)
    return mask, bucket


def _attn_blocks(s, r):
    m = s // r
    assert m % ATTN_BLOCK == 0, "sequence length must be a multiple of dilation * block"
    return m // ATTN_BLOCK


def _perm_load(ref, r):
    if r == 1:
        return ref[...]
    m = ref.shape[0] // r
    return jnp.concatenate([ref[pl.ds(c, m, stride=r), :] for c in range(r)], axis=0)


def _perm_store(ref, g, val, r, add=False):
    if r == 1:
        ref[g] = ref[g] + val if add else val
        return
    m = val.shape[0] // r
    for c in range(r):
        rows = pl.ds(c, m, stride=r)
        part = val[c * m:(c + 1) * m]
        ref[g, rows, :] = ref[g, rows, :] + part if add else part


def _blocks(x):
    return x.reshape(x.shape[0] // ATTN_BLOCK, ATTN_BLOCK, x.shape[1])


def _prev_blocks(x):
    return jnp.concatenate([x[:1], x[:-1]], axis=0)


def _next_blocks(x):
    return jnp.concatenate([x[1:], jnp.zeros_like(x[:1])], axis=0)


def _first_block_neg(s, r):
    nblk = s // ATTN_BLOCK
    idx = lax.broadcasted_iota(jnp.int32, (nblk, 1, 1), 0)
    return jnp.where(idx % _attn_blocks(s, r) == 0, NEG, 0.0)


def _bdot_nt(a, b):
    return lax.dot_general(a, b, (((2,), (2,)), ((0,), (0,))), preferred_element_type=F32)


def _bdot(a, b):
    return lax.dot_general(a, b, (((2,), (1,)), ((0,), (0,))), preferred_element_type=F32)


def _bdot_tn(a, b):
    return lax.dot_general(a, b, (((1,), (1,)), ((0,), (0,))), preferred_element_type=F32)


def attn_fwd(qkv, biasm, n_heads, name, ride=()):
    b, s, _ = qkv.shape
    h = n_heads
    scale = HEAD_DIM ** -0.5
    blk = ATTN_BLOCK

    def body(q1_ref, q2_ref, q3_ref, k_ref, v_ref, bias_ref, o_ref, lse_ref, acc, m_s, l_s):
        for g, q_ref in enumerate((q1_ref, q2_ref, q3_ref)):
            r = DILATED[g][1]
            first = _first_block_neg(s, r)
            q = _blocks(_perm_load(q_ref, r).astype(BF16))
            k = _blocks(_perm_load(k_ref, r).astype(BF16))
            v = _blocks(_perm_load(v_ref, r).astype(BF16))
            s_cur = _bdot_nt(q, k) * scale + bias_ref[g, :, blk:]
            s_prev = _bdot_nt(q, _prev_blocks(k)) * scale + bias_ref[g, :, :blk] + first
            m = jnp.maximum(jnp.max(s_cur, axis=-1, keepdims=True), jnp.max(s_prev, axis=-1, keepdims=True))
            p_cur = jnp.exp(s_cur - m)
            p_prev = jnp.exp(s_prev - m)
            l = jnp.sum(p_cur, axis=-1, keepdims=True) + jnp.sum(p_prev, axis=-1, keepdims=True)
            o = _bdot(p_cur.astype(BF16), v) + _bdot(p_prev.astype(BF16), _prev_blocks(v))
            _perm_store(acc, g, o.reshape(s, HEAD_DIM), r)
            _perm_store(m_s, g, m.reshape(s, 1), r)
            _perm_store(l_s, g, l.reshape(s, 1), r)
        m_all = jnp.maximum(jnp.maximum(m_s[0], m_s[1]), m_s[2])
        w = [jnp.exp(m_s[g] - m_all) for g in range(N_GROUPS)]
        l = w[0] * l_s[0] + w[1] * l_s[1] + w[2] * l_s[2]
        o_ref[...] = (w[0] * acc[0] + w[1] * acc[1] + w[2] * acc[2]) / l
        lse_ref[...] = m_all + jnp.log(l)

    def col(j):
        return pl.BlockSpec((None, s, HEAD_DIM), lambda bi, hi, j=j: (bi, 0, j * h + hi))

    r_ins, r_in_specs, r_outs, r_out_specs, r_sems = _ride_args(ride)
    outs = pl.pallas_call(
        _riding(body, 6, 2, 3, ride, 2), name=name, grid=(b, h),
        in_specs=[col(0), col(1), col(2), col(3), col(4),
                  pl.BlockSpec((N_GROUPS, None, blk, 2 * blk), lambda bi, hi: (0, hi, 0, 0))] + r_in_specs,
        out_specs=[pl.BlockSpec((None, s, HEAD_DIM), lambda bi, hi: (bi, 0, hi)),
                   pl.BlockSpec((None, None, s, 1), lambda bi, hi: (bi, hi, 0, 0))] + r_out_specs,
        out_shape=[jax.ShapeDtypeStruct((b, s, h * HEAD_DIM), F32), jax.ShapeDtypeStruct((b, h, s, 1), F32)] + r_outs,
        scratch_shapes=[pltpu.VMEM((N_GROUPS, s, HEAD_DIM), F32), pltpu.VMEM((N_GROUPS, s, 1), F32),
                        pltpu.VMEM((N_GROUPS, s, 1), F32)] + r_sems,
        compiler_params=_params(("arbitrary", "arbitrary")),
    )(qkv, qkv, qkv, qkv, qkv, biasm, *r_ins)
    return outs[0], outs[1], _ride_results(ride, outs[2:])


def attn_bwd(qkv, biasm, o, lse, do, n_heads, name, ride=()):
    b, s, _ = qkv.shape
    h = n_heads
    scale = HEAD_DIM ** -0.5
    blk = ATTN_BLOCK

    def body(q1_ref, q2_ref, q3_ref, k_ref, v_ref, bias_ref, o_ref, lse_ref, do_ref,
             dq1_ref, dq2_ref, dq3_ref, dk_ref, dv_ref, ds_ref, dq_acc, kv_acc, delta):
        delta[...] = jnp.sum(do_ref[...] * o_ref[...], axis=-1, keepdims=True)
        kv_acc[...] = jnp.zeros(kv_acc.shape, F32)
        for g, q_ref in enumerate((q1_ref, q2_ref, q3_ref)):
            r = DILATED[g][1]
            first = _first_block_neg(s, r)
            q = _blocks(_perm_load(q_ref, r).astype(BF16))
            k = _blocks(_perm_load(k_ref, r).astype(BF16))
            v = _blocks(_perm_load(v_ref, r).astype(BF16))
            dob = _blocks(_perm_load(do_ref, r).astype(BF16))
            lse_b = _blocks(_perm_load(lse_ref, r))
            dl_b = _blocks(_perm_load(delta, r))
            k_prev, v_prev = _prev_blocks(k), _prev_blocks(v)
            p_cur = jnp.exp(_bdot_nt(q, k) * scale + bias_ref[g, :, blk:] - lse_b)
            p_prev = jnp.exp(_bdot_nt(q, k_prev) * scale + bias_ref[g, :, :blk] + first - lse_b)
            ds_cur = p_cur * (_bdot_nt(dob, v) - dl_b)
            ds_prev = p_prev * (_bdot_nt(dob, v_prev) - dl_b)
            ds_ref[g, :, blk:] = jnp.sum(ds_cur, axis=0)
            ds_ref[g, :, :blk] = jnp.sum(ds_prev, axis=0)
            ds_cur_b, ds_prev_b = ds_cur.astype(BF16), ds_prev.astype(BF16)
            dq = (_bdot(ds_cur_b, k) + _bdot(ds_prev_b, k_prev)) * scale
            _perm_store(dq_acc, g, dq.reshape(s, HEAD_DIM), r)
            dk = (_bdot_tn(ds_cur_b, q) + _next_blocks(_bdot_tn(ds_prev_b, q))) * scale
            dv = _bdot_tn(p_cur.astype(BF16), dob) + _next_blocks(_bdot_tn(p_prev.astype(BF16), dob))
            _perm_store(kv_acc, 0, dk.reshape(s, HEAD_DIM), r, add=True)
            _perm_store(kv_acc, 1, dv.reshape(s, HEAD_DIM), r, add=True)
        for g, out_ref in enumerate((dq1_ref, dq2_ref, dq3_ref)):
            out_ref[...] = dq_acc[g].astype(out_ref.dtype)
        dk_ref[...] = kv_acc[0].astype(dk_ref.dtype)
        dv_ref[...] = kv_acc[1].astype(dv_ref.dtype)

    def col(j):
        return pl.BlockSpec((None, s, HEAD_DIM), lambda bi, hi, j=j: (bi, 0, j * h + hi))

    head = pl.BlockSpec((None, s, HEAD_DIM), lambda bi, hi: (bi, 0, hi))
    sd = jax.ShapeDtypeStruct
    r_ins, r_in_specs, r_outs, r_out_specs, r_sems = _ride_args(ride)
    outs = pl.pallas_call(
        _riding(body, 9, 6, 3, ride, 2), name=name, grid=(b, h),
        in_specs=[col(0), col(1), col(2), col(3), col(4),
                  pl.BlockSpec((N_GROUPS, None, blk, 2 * blk), lambda bi, hi: (0, hi, 0, 0)),
                  head, pl.BlockSpec((None, None, s, 1), lambda bi, hi: (bi, hi, 0, 0)), head] + r_in_specs,
        out_specs=[head] * 5 + [pl.BlockSpec((None, None, N_GROUPS, blk, 2 * blk), lambda bi, hi: (bi, hi, 0, 0, 0))]
        + r_out_specs,
        out_shape=[sd((b, s, h * HEAD_DIM), BF16)] * 5 + [sd((b, h, N_GROUPS, blk, 2 * blk), F32)] + r_outs,
        scratch_shapes=[pltpu.VMEM((N_GROUPS, s, HEAD_DIM), F32), pltpu.VMEM((2, s, HEAD_DIM), F32),
                        pltpu.VMEM((s, 1), F32)] + r_sems,
        compiler_params=_params(("arbitrary", "arbitrary")),
    )(qkv, qkv, qkv, qkv, qkv, biasm, o, lse, do, *r_ins)
    return outs[:6], _ride_results(ride, outs[6:])


def bias_table(rel_rows, bucket_f, n_heads, name):
    g, blk, blk2 = bucket_f.shape
    h = n_heads

    def body(rb_ref, bk_ref, o_ref):
        bk = bk_ref[...]
        rb = rb_ref[...]
        acc = jnp.full((blk, blk2), NEG, F32)
        for bucket in range(REL_BUCKETS):
            acc = jnp.where(bk == float(bucket), rb[:, bucket:bucket + 1], acc)
        o_ref[...] = acc

    return pl.pallas_call(
        body, name=name, grid=(g, h),
        in_specs=[pl.BlockSpec((None, 1, 128), lambda gi, hi: (gi * h + hi, 0, 0)),
                  pl.BlockSpec((None, blk, blk2), lambda gi, hi: (gi, 0, 0))],
        out_specs=pl.BlockSpec((None, None, blk, blk2), lambda gi, hi: (gi, hi, 0, 0)),
        out_shape=jax.ShapeDtypeStruct((g, h, blk, blk2), F32),
        compiler_params=_params(("parallel", "parallel")),
    )(rel_rows, bucket_f)


def bias_grad(ds_sum, bucket_f, name):
    b, h, g, blk, blk2 = ds_sum.shape

    def body(ds_ref, bk_ref, o_ref):
        tot = jnp.sum(ds_ref[...], axis=0)
        bk = bk_ref[...]
        lane = lax.broadcasted_iota(jnp.int32, (1, 128), 1)
        vec = jnp.zeros((1, 128), F32)
        for bucket in range(REL_BUCKETS):
            val = jnp.sum(jnp.where(bk == float(bucket), tot, 0.0), keepdims=True)
            vec = vec + jnp.where(lane == bucket, val, 0.0)
        o_ref[...] = vec

    return pl.pallas_call(
        body, name=name, grid=(g, h),
        in_specs=[pl.BlockSpec((b, None, None, blk, blk2), lambda gi, hi: (0, hi, gi, 0, 0)),
                  pl.BlockSpec((None, blk, blk2), lambda gi, hi: (gi, 0, 0))],
        out_specs=pl.BlockSpec((None, 1, 128), lambda gi, hi: (gi * h + hi, 0, 0)),
        out_shape=jax.ShapeDtypeStruct((g * h, 1, 128), F32),
        compiler_params=_params(("parallel", "parallel")),
    )(ds_sum, bucket_f)


def _chip_peers():
    x, y, c = lax.axis_index("x"), lax.axis_index("y"), lax.axis_index("c")
    me = 2 * x + y
    peers = [(1 - x, y, c), (x, 1 - y, c), (1 - x, 1 - y, c)]
    peer_chip = [2 * (1 - x) + y, 2 * x + (1 - y), 2 * (1 - x) + (1 - y)]
    return me, peers, peer_chip


def _any_specs(n):
    return [pl.BlockSpec(memory_space=pl.ANY)] * n


_MID_NUM, _MID_DEN = 3, 4


class _Exchange:
    def start(self, ins, outs, sems):
        local, sends, _ = self._copies(ins, outs, sems)
        for cp in local + sends:
            cp.start()

    def mid(self, ins, outs, sems):
        pass

    def wait(self, ins, outs, sems):
        local, sends, recvs = self._copies(ins, outs, sems)
        for cp in recvs():
            cp.wait_recv()
        for cp in sends:
            cp.wait_send()
        for cp in local:
            cp.wait()


class _Gather(_Exchange):
    HALF_ROWS = 16

    def __init__(self, arrays):
        n = len(arrays)
        self.ins = list(arrays)
        self.split = [a.shape[0] % (2 * self.HALF_ROWS) == 0 for a in arrays]
        self.out_shape = [jax.ShapeDtypeStruct((N_CHIPS,) + a.shape, a.dtype) for a in arrays]
        dma = pltpu.SemaphoreType.DMA
        self.sems = [dma((3 * n,)), dma((3 * n,)), dma((n,)), dma((3 * n,)), dma((3 * n,))]

    def _half(self, i, ref, sibling=False):
        if not self.split[i]:
            return ref
        half = self.ins[i].shape[0] // 2
        c = lax.axis_index("c")
        c = 1 - c if sibling else c
        return ref.at[pl.ds(pl.multiple_of(c * half, self.HALF_ROWS), half)]

    def _plan(self, ins, outs, sems):
        send1, recv1, local_sems, send2, recv2 = sems
        me, peers, peer_chip = _chip_peers()
        x, y, c = lax.axis_index("x"), lax.axis_index("y"), lax.axis_index("c")
        n = len(ins)
        pairs = [(i, k) for i in range(n) for k in range(3)]

        def fetch(i, k, slot):
            return pltpu.make_async_remote_copy(src_ref=self._half(i, ins[i]), dst_ref=self._half(i, outs[i].at[slot]),
                                                send_sem=send1.at[3 * i + k], recv_sem=recv1.at[3 * i + k],
                                                device_id=peers[k], device_id_type=MESH)

        def share(i, k, sibling):
            part = self._half(i, outs[i].at[peer_chip[k]], sibling)
            return pltpu.make_async_remote_copy(src_ref=part, dst_ref=part, send_sem=send2.at[3 * i + k],
                                                recv_sem=recv2.at[3 * i + k], device_id=(x, y, 1 - c),
                                                device_id_type=MESH)

        split_pairs = [(i, k) for i, k in pairs if self.split[i]]
        return dict(
            local=lambda: [pltpu.make_async_copy(ins[i], outs[i].at[me], local_sems.at[i]) for i in range(n)],
            fetch_out=lambda: [fetch(i, k, me) for i, k in pairs],
            fetch_in=lambda: [fetch(i, k, peer_chip[k]) for i, k in pairs],
            share_out=lambda: [share(i, k, False) for i, k in split_pairs],
            share_in=lambda: [share(i, k, True) for i, k in split_pairs])

    def start(self, ins, outs, sems):
        plan = self._plan(ins, outs, sems)
        for cp in plan["local"]() + plan["fetch_out"]():
            cp.start()

    def mid(self, ins, outs, sems):
        plan = self._plan(ins, outs, sems)
        for cp in plan["fetch_in"]():
            cp.wait_recv()
        for cp in plan["share_out"]():
            cp.start()

    def wait(self, ins, outs, sems):
        plan = self._plan(ins, outs, sems)
        for cp in plan["share_in"]():
            cp.wait_recv()
        for cp in plan["fetch_out"]() + plan["share_out"]():
            cp.wait_send()
        for cp in plan["local"]():
            cp.wait()


class _Scatter(_Exchange):
    def __init__(self, slabs, whole=()):
        self.n_slabs = len(slabs)
        self.ins = list(slabs) + list(whole)
        n = len(self.ins)
        self.out_shape = [jax.ShapeDtypeStruct(a.shape, a.dtype) for a in slabs] \
            + [jax.ShapeDtypeStruct((N_CHIPS,) + a.shape, a.dtype) for a in whole]
        self.sems = [pltpu.SemaphoreType.DMA((3 * n,)), pltpu.SemaphoreType.DMA((3 * n,)), pltpu.SemaphoreType.DMA((n,))]

    def _copies(self, ins, outs, sems):
        send_sems, recv_sems, local_sems = sems
        me, peers, peer_chip = _chip_peers()
        n = len(ins)

        def src(i, chip):
            return ins[i].at[chip] if i < self.n_slabs else ins[i]

        def remote(i, k, src_chip, slot):
            return pltpu.make_async_remote_copy(src_ref=src(i, src_chip), dst_ref=outs[i].at[slot],
                                                send_sem=send_sems.at[3 * i + k], recv_sem=recv_sems.at[3 * i + k],
                                                device_id=peers[k], device_id_type=MESH)

        local = [pltpu.make_async_copy(src(i, me), outs[i].at[me], local_sems.at[i]) for i in range(n)]
        sends = [remote(i, k, peer_chip[k], me) for i in range(n) for k in range(3)]
        return local, sends, lambda: [remote(i, k, me, peer_chip[k]) for i in range(n) for k in range(3)]


class _Swap(_Exchange):
    def __init__(self, arrays):
        n = len(arrays)
        self.ins = list(arrays)
        self.out_shape = [jax.ShapeDtypeStruct(a.shape, a.dtype) for a in arrays]
        self.sems = [pltpu.SemaphoreType.DMA((n,)), pltpu.SemaphoreType.DMA((n,))]

    def _copies(self, ins, outs, sems):
        send_sems, recv_sems = sems
        x, y, c = lax.axis_index("x"), lax.axis_index("y"), lax.axis_index("c")
        cps = [pltpu.make_async_remote_copy(src_ref=ins[i], dst_ref=outs[i], send_sem=send_sems.at[i],
                                            recv_sem=recv_sems.at[i], device_id=(x, y, 1 - c), device_id_type=MESH)
               for i in range(len(ins))]
        return [], cps, lambda: cps


def _riding(body, n_in, n_out, n_scratch, ride, rank):
    if not ride:
        return body
    r_in = sum(len(e.ins) for e in ride)
    r_out = sum(len(e.out_shape) for e in ride)

    def split(refs, sizes):
        out, a = [], 0
        for sz in sizes:
            out.append(refs[a:a + sz])
            a += sz
        return out

    def wrapped(*refs):
        a = 0
        parts = []
        for sz in (n_in, r_in, n_out, r_out, n_scratch):
            parts.append(refs[a:a + sz])
            a += sz
        own_in, ex_in, own_out, ex_out, own_scratch = parts
        ex_sems = refs[a:]
        ins = split(ex_in, [len(e.ins) for e in ride])
        outs = split(ex_out, [len(e.out_shape) for e in ride])
        sems = split(ex_sems, [len(e.sems) for e in ride])
        if rank:
            step, total = 0, 1
            for d in range(rank):
                step = step * pl.num_programs(d) + pl.program_id(d)
                total = total * pl.num_programs(d)

            @pl.when(step == 0)
            def _():
                for e, i, o, s in zip(ride, ins, outs, sems):
                    e.start(i, o, s)

            body(*own_in, *own_out, *own_scratch)

            @pl.when(step == (total * _MID_NUM) // _MID_DEN)
            def _():
                for e, i, o, s in zip(ride, ins, outs, sems):
                    e.mid(i, o, s)

            @pl.when(step == total - 1)
            def _():
                for e, i, o, s in zip(ride, ins, outs, sems):
                    e.wait(i, o, s)
        else:
            for phase in ("start", "mid", "wait"):
                for e, i, o, s in zip(ride, ins, outs, sems):
                    getattr(e, phase)(i, o, s)

    return wrapped


def _ride_args(ride):
    ins = [a for e in ride for a in e.ins]
    outs = [s for e in ride for s in e.out_shape]
    sems = [s for e in ride for s in e.sems]
    return ins, _any_specs(len(ins)), outs, _any_specs(len(outs)), sems


def _ride_results(ride, flat):
    out, a = [], 0
    for e in ride:
        out.append(list(flat[a:a + len(e.out_shape)]))
        a += len(e.out_shape)
    return out


def exchange(ride, name):
    ins, in_specs, outs, out_specs, sems = _ride_args(ride)
    res = pl.pallas_call(
        _riding(lambda: None, 0, 0, 0, ride, 0), name=name,
        in_specs=in_specs, out_specs=out_specs, out_shape=outs, scratch_shapes=sems,
    )(*ins)
    return _ride_results(ride, res)


def _sum_slots(ref):
    acc = ref[0].astype(F32)
    for j in range(1, ref.shape[0]):
        acc = acc + ref[j].astype(F32)
    return acc


def sum_pairs(mine, other, name, tr=176):
    n, r, w = mine.shape
    tr = _tile(r, tr)

    def body(a_ref, b_ref, o_ref):
        o_ref[...] = _sum_slots(a_ref) + _sum_slots(b_ref)

    spec = pl.BlockSpec((n, tr, w), lambda i: (0, i, 0))
    return pl.pallas_call(
        body, name=name, grid=(r // tr,),
        in_specs=[spec, spec], out_specs=_rows(tr, w),
        out_shape=jax.ShapeDtypeStruct((r, w), F32),
        compiler_params=_params(("parallel",)),
    )(mine, other)


def adamw(w, m, v, gs, name, tr=256):
    r, c = w.shape
    tr = r if r % 8 else _tile(r, tr)
    c1 = 1.0 - ADAM_B1 ** ADAM_STEP
    c2 = 1.0 - ADAM_B2 ** ADAM_STEP
    ng = len(gs)

    def body(w_ref, m_ref, v_ref, *refs):
        g_refs, (g_ref, d_ref, nm_ref, nv_ref) = refs[:ng], refs[ng:]
        g = g_refs[0][...] if ng == 1 else _sum_slots(g_refs[0]) + _sum_slots(g_refs[1])
        nm = ADAM_B1 * m_ref[...] + (1.0 - ADAM_B1) * g
        nv = ADAM_B2 * v_ref[...] + (1.0 - ADAM_B2) * (g * g)
        g_ref[...] = g
        nm_ref[...] = nm
        nv_ref[...] = nv
        d_ref[...] = (-ADAM_LR) * ((nm / c1) / (jnp.sqrt(nv / c2) + ADAM_EPS) + ADAM_WD * w_ref[...])

    spec = _rows(tr, c)
    gspec = spec if ng == 1 else pl.BlockSpec((N_CHIPS, tr, c), lambda i: (0, i, 0))
    return pl.pallas_call(
        body, name=name, grid=(r // tr,),
        in_specs=[spec] * 3 + [gspec] * ng, out_specs=[spec] * 4,
        out_shape=[jax.ShapeDtypeStruct((r, c), F32)] * 4,
        compiler_params=_params(("parallel",)),
    )(w, m, v, *gs)


_PARAMS = (
    ("rel_bias", None), ("norm_mix_pre", None), ("norm_mix_post", None), ("w_in", 1), ("conv_rnn_w", 1),
    ("conv_rnn_b", None), ("w_rg_a", None), ("b_rg_a", None), ("w_rg_x", None), ("b_rg_x", None),
    ("lru_lambda", None), ("w_branch_rnn", 0), ("w_branch_att", 1), ("w_out", 0), ("norm_ffn_pre", None),
    ("norm_ffn_post", None), ("w_ffn_gate", 1), ("w_ffn_up", 1), ("conv_ffn_w", 1), ("conv_ffn_b", None),
    ("w_ffn_down", 0),
)
_SMALL = 65536


def _as2d(a):
    a = a[0] if a.shape[0] == 1 and a.ndim >= 3 else a
    return a.reshape(-1, a.shape[-1]) if a.ndim == 3 else a


def _pack(pieces, dtype):
    flat = jnp.concatenate([p.astype(dtype).reshape(-1) for p in pieces])
    unit = PACK_W * PACK_ROWS
    pad = (-flat.shape[0]) % unit
    flat = jnp.pad(flat, (0, pad))
    return flat.reshape(-1, PACK_W)


def _unpack(buf, shapes):
    flat = buf.reshape(-1)
    out, off = [], 0
    for shp in shapes:
        n = int(np.prod(shp))
        out.append(flat[off:off + n].reshape(shp))
        off += n
    return out


def _join(slots, ax):
    if ax == 0:
        return slots.reshape(-1, slots.shape[-1])
    return jnp.transpose(slots, (1, 0, 2)).reshape(slots.shape[1], -1)


def _cut(full, ax):
    if ax == 0:
        return full.reshape(N_CHIPS, -1, full.shape[-1])
    return jnp.transpose(full.reshape(full.shape[0], N_CHIPS, -1), (1, 0, 2))


def kernel(x, rel_bias, norm_mix_pre, norm_mix_post, w_in, conv_rnn_w, conv_rnn_b, w_rg_a, b_rg_a, w_rg_x, b_rg_x, lru_lambda, w_branch_rnn, w_branch_att, w_out, norm_ffn_pre, norm_ffn_post, w_ffn_gate, w_ffn_up, conv_ffn_w, conv_ffn_b, w_ffn_down, loss_target, m_rel_bias, m_norm_mix_pre, m_norm_mix_post, m_w_in, m_conv_rnn_w, m_conv_rnn_b, m_w_rg_a, m_b_rg_a, m_w_rg_x, m_b_rg_x, m_lru_lambda, m_w_branch_rnn, m_w_branch_att, m_w_out, m_norm_ffn_pre, m_norm_ffn_post, m_w_ffn_gate, m_w_ffn_up, m_conv_ffn_w, m_conv_ffn_b, m_w_ffn_down, v_rel_bias, v_norm_mix_pre, v_norm_mix_post, v_w_in, v_conv_rnn_w, v_conv_rnn_b, v_w_rg_a, v_b_rg_a, v_w_rg_x, v_b_rg_x, v_lru_lambda, v_w_branch_rnn, v_w_branch_att, v_w_out, v_norm_ffn_pre, v_norm_ffn_post, v_w_ffn_gate, v_w_ffn_up, v_conv_ffn_w, v_conv_ffn_b, v_w_ffn_down):
    args = dict(locals())
    names = [n for n, _ in _PARAMS]
    axis = dict(_PARAMS)
    w_loc = {n: args[n] for n in names}
    m_loc = {n: args["m_" + n] for n in names}
    v_loc = {n: args["v_" + n] for n in names}
    sharded = [n for n in names if axis[n] is not None]
    replicated = [n for n in names if axis[n] is None]

    big = [n for n in sharded if w_loc[n].size >= _SMALL]
    small_sharded = [n for n in sharded if n not in big]
    small = replicated + small_sharded

    first = ["w_in"] + small_sharded
    srcs = [_as2d(w_loc[n]).astype(BF16) if n in big else _as2d(w_loc[n]) for n in first]
    (gathered,) = exchange([_Gather(srcs)], "gather_first")
    p = {n: _join(a, axis[n]) for n, a in zip(first, gathered)}
    for n in replicated:
        p[n] = _as2d(w_loc[n])
    shards = {n: _as2d(w_loc[n]).astype(BF16) for n in big if n not in first}

    received, sibling, g_small, loss_part = _local_step(x, loss_target, p, shards)

    pack = _pack([g_small[n] for n in small], BF16)
    ((received["small"],),) = exchange([_Scatter([], [pack])], "scatter_small")
    late = [n for n in received if n not in sibling]
    (swapped,) = exchange([_Swap([received[n] for n in late])], "swap_last")
    sibling.update(zip(late, swapped))
    small_sum = sum_pairs(received["small"], sibling["small"], "sum_small")
    g_tot = dict(zip(small, _unpack(small_sum, [g_small[n].shape for n in small])))
    chip = 2 * lax.axis_index("x") + lax.axis_index("y")
    for n in small_sharded:
        size = g_tot[n].shape[axis[n]] // N_CHIPS
        g_tot[n] = lax.dynamic_slice_in_dim(g_tot[n], chip * size, size, axis=axis[n])

    out_g, out_d, out_m, out_v = {}, {}, {}, {}
    for i, n in enumerate(names):
        shp = w_loc[n].shape
        gs = (received[n], sibling[n]) if n in big else (g_tot[n],)
        g, d, nm, nv = adamw(_as2d(w_loc[n]), _as2d(m_loc[n]), _as2d(v_loc[n]), gs, "adamw_" + n)
        out_g[n], out_d[n], out_m[n], out_v[n] = (t.reshape(shp) for t in (g, d, nm, nv))

    d_model = x.shape[-1]
    loss = lax.psum(0.5 * jnp.sum(loss_part) / d_model, ("x", "y", "c"))
    grad_x = g_small["x"]
    return (loss, grad_x, *[out_g[n] for n in names], *[out_d[n] for n in names],
            *[out_m[n] for n in names], *[out_v[n] for n in names])


def _local_step(x, target, p, shards):
    axis = dict(_PARAMS)
    b, s, d = x.shape
    t = b * s
    rnn = p["b_rg_a"].shape[1]
    ffn = p["conv_ffn_b"].shape[1]
    nbk = rnn // p["w_rg_a"].shape[1]
    hkv = (p["w_in"].shape[1] - rnn - 2 * d) // (N_GROUPS + 2)
    h = hkv // HEAD_DIM
    nq = N_GROUPS * hkv

    x2 = x.reshape(t, d)
    tgt = target.reshape(t, d)
    w_in = p["w_in"]
    in_splits = (rnn, nq + 2 * hkv, 2 * d)
    wa = p["w_rg_a"].reshape(nbk, -1, p["w_rg_a"].shape[1]).astype(BF16)
    wx = p["w_rg_x"].reshape(nbk, -1, p["w_rg_x"].shape[1]).astype(BF16)
    cw_r, cb_r = p["conv_rnn_w"], p["conv_rnn_b"]
    cw_f, cb_f = p["conv_ffn_w"], p["conv_ffn_b"]

    masks, buckets = zip(*[_band(w_, r_) for w_, r_ in DILATED])
    bucket_f = jnp.asarray(np.where(np.stack(masks), np.stack(buckets), -1).astype(np.float32))
    rel_rows = jnp.pad(p["rel_bias"].T, ((0, 0), (0, 128 - REL_BUCKETS)))[:, None, :]
    biasm = bias_table(rel_rows, bucket_f, h, "bias_table")

    early = ["w_branch_rnn", "w_branch_att", "w_out"]
    hn1, (xr, qkv, gts), (got,) = norm_mm(x2, p["norm_mix_pre"], [w_in], [in_splits], "in_proj",
                                          ride=[_Gather([shards[n] for n in early])])
    p.update({n: _join(a, axis[n]) for n, a in zip(early, got)})
    xr3 = xr.reshape(b, s, rnn)
    y_rnn, (got,) = rglru_fwd(xr3, cw_r, cb_r, wa, p["b_rg_a"], wx, p["b_rg_x"], p["lru_lambda"], "rglru_fwd",
                              ride=[_Gather([shards[n] for n in ("w_ffn_gate", "w_ffn_up")])])
    p.update({n: _join(a, axis[n]) for n, a in zip(("w_ffn_gate", "w_ffn_up"), got)})
    qkv3 = qkv.reshape(b, s, -1)
    o_att, lse, ((got,),) = attn_fwd(qkv3, biasm, h, "attn_fwd", ride=[_Gather([shards["w_ffn_down"]])])
    p["w_ffn_down"] = _join(got, axis["w_ffn_down"])
    merged, br, ba = merge_fwd(y_rnn.reshape(t, rnn), o_att.reshape(t, hkv), gts, p["w_branch_rnn"],
                               p["w_branch_att"], "merge_fwd")
    mix, h1 = mm_norm_res(merged, p["w_out"], p["norm_mix_post"], x2, "out_proj")
    hn2, gate_pre, up, act = ffn_in_act(h1, p["norm_ffn_pre"], p["w_ffn_gate"], p["w_ffn_up"], cw_f, cb_f, s, "ffn_in")

    g, gb = {}, {}
    recv, sib = {}, {}

    def rows4(a):
        return a.reshape(N_CHIPS, -1, a.shape[-1])

    dy, dff, dact, g["norm_ffn_post"], loss_part = ffn_down_loss(act, p["w_ffn_down"], p["norm_ffn_post"], h1, tgt,
                                                                  "ffn_down")
    gb["w_ffn_down"] = rows4(mm_tn(act, [dff], "ffn_down_dw"))
    dgp, dup, g["conv_ffn_w"], g["conv_ffn_b"] = ffn_bwd(dact.reshape(b, s, ffn), gate_pre.reshape(b, s, ffn),
                                                        up.reshape(b, s, ffn), cw_f, cb_f, "ffn_bwd")
    dgp, dup = dgp.reshape(t, ffn), dup.reshape(t, ffn)
    dhn2, ((recv["w_ffn_down"],),) = mm_nt([([dgp], p["w_ffn_gate"]), ([dup], p["w_ffn_up"])], F32, "ffn_in_dx",
                                           ride=[_Scatter([gb["w_ffn_down"]])])
    gb["w_ffn_gate"] = mm_tn(hn2, [dgp], "ffn_gate_dw", col_shards=N_CHIPS)
    gb["w_ffn_up"] = mm_tn(hn2, [dup], "ffn_up_dw", col_shards=N_CHIPS)
    dh1, g["norm_ffn_pre"] = norm_bwd(dhn2, h1, p["norm_ffn_pre"], dy, F32, "ffn_norm_bwd")
    dmix, g["norm_mix_post"] = norm_bwd(dh1, mix, p["norm_mix_post"], None, BF16, "mix_norm_bwd")
    dmerged = mm_nt([([dmix], p["w_out"])], F32, "out_proj_dx")
    gb["w_out"] = rows4(mm_tn(merged, [dmix], "out_proj_dw"))
    dbr, dba, dgts = merge_bwd(dmerged, gts, br, ba, "merge_bwd")
    dy_rnn = mm_nt([([dbr], p["w_branch_rnn"])], F32, "branch_rnn_dx")
    do_att = mm_nt([([dba], p["w_branch_att"])], F32, "branch_att_dx")
    gb["w_branch_rnn"] = rows4(mm_tn(y_rnn.reshape(t, rnn), [dbr], "branch_rnn_dw"))
    gb["w_branch_att"] = mm_tn(o_att.reshape(t, hkv), [dba], "branch_att_dw", col_shards=N_CHIPS)
    ffn_in = ["w_ffn_gate", "w_ffn_up"]
    (dxr, g["conv_rnn_w"], g["conv_rnn_b"], dwa, g["b_rg_a"], dwx, g["b_rg_x"], g["lru_lambda"]), (got,) = rglru_bwd(
        xr3, y_rnn, dy_rnn.reshape(b, s, rnn), cw_r, cb_r, wa, p["b_rg_a"], wx, p["b_rg_x"], p["lru_lambda"], "rglru_bwd",
        ride=[_Scatter([gb[n] for n in ffn_in])])
    recv.update(zip(ffn_in, got))
    g["w_rg_a"] = dwa.reshape(p["w_rg_a"].shape)
    g["w_rg_x"] = dwx.reshape(p["w_rg_x"].shape)
    mid = ["w_out", "w_branch_rnn", "w_branch_att"]
    early_recv = ["w_ffn_down"] + ffn_in
    (dq1, dq2, dq3, dk, dv, ds_sum), (got, swapped) = attn_bwd(
        qkv3, biasm, o_att, lse, do_att.reshape(b, s, hkv), h, "attn_bwd",
        ride=[_Scatter([gb[n] for n in mid]), _Swap([recv[n] for n in early_recv])])
    recv.update(zip(mid, got))
    sib.update(zip(early_recv, swapped))
    rows = bias_grad(ds_sum, bucket_f, "bias_grad")
    g["rel_bias"] = rows[:, 0, :REL_BUCKETS].T
    dproj = [dxr.reshape(t, rnn)] + [a.reshape(t, hkv) for a in (dq1, dq2, dq3, dk, dv)] + [dgts]
    dw_a = mm_tn(hn1, dproj[:4], "in_proj_dw_a")[0]
    dw_b = mm_tn(hn1, dproj[4:], "in_proj_dw_b")[0]
    gb["w_in"] = _cut(jnp.concatenate([dw_a, dw_b], axis=1), 1)
    dhn1, ((recv["w_in"],), got) = mm_nt([(dproj, w_in)], F32, "in_proj_dx",
                                         ride=[_Scatter([gb["w_in"]]), _Swap([recv[n] for n in mid])])
    sib.update(zip(mid, got))
    dx, g["norm_mix_pre"] = norm_bwd(dhn1, x2, p["norm_mix_pre"], dh1, F32, "in_norm_bwd")
    g["x"] = dx.reshape(b, s, d)
    return recv, sib, g, loss_part
```

```python
import functools
import math

import numpy as np
import jax
import jax.numpy as jnp
from jax import lax
from jax.experimental import pallas as pl
from jax.experimental.pallas import tpu as pltpu

F32 = jnp.float32
BF16 = jnp.bfloat16

EPS = 1e-6
HEAD_DIM = 128
ATTN_BLOCK = 128
DILATED = ((128, 1), (512, 4), (2048, 16))
N_GROUPS = len(DILATED)
REL_BUCKETS = 32
REL_MAX_DIST = 2048
LRU_C = 8.0
NEG = -1e30

ADAM_LR = 0.001
ADAM_B1 = 0.9
ADAM_B2 = 0.999
ADAM_EPS = 1e-08
ADAM_WD = 0.01
ADAM_STEP = 10

N_CHIPS = 4
PACK_W = 1024
PACK_ROWS = 16
VMEM_LIMIT = 56 * 1024 * 1024
MESH = pl.DeviceIdType.MESH


def _params(sem=None):
    return pltpu.CompilerParams(dimension_semantics=sem, vmem_limit_bytes=VMEM_LIMIT)


def _dot(a, b):
    return jnp.dot(a, b, preferred_element_type=F32)


def _dot_nt(a, b):
    return lax.dot_general(a, b, (((1,), (1,)), ((), ())), preferred_element_type=F32)


def _dot_tn(a, b):
    return lax.dot_general(a, b, (((0,), (0,)), ((), ())), preferred_element_type=F32)


def _sig(x):
    return 0.5 * jnp.tanh(0.5 * x) + 0.5


def _rows(tm, w):
    return pl.BlockSpec((tm, w), lambda i: (i, 0))


def _whole(shape):
    nd = len(shape)
    return pl.BlockSpec(tuple(shape), lambda *_: (0,) * nd)


def _tile(t, want):
    while t % want:
        want //= 2
    return want


def norm_mm(x, g, ws, splits, name, ride=(), tm=256):
    t, d = x.shape
    tm = _tile(t, tm)
    nw = len(ws)
    widths = [n for sp in splits for n in sp]

    def body(x_ref, g_ref, *refs):
        w_refs, hn_ref, o_refs = refs[:nw], refs[nw], refs[nw + 1:]
        xv = x_ref[...]
        inv = lax.rsqrt(jnp.mean(xv * xv, axis=-1, keepdims=True) + EPS)
        hn = (xv * inv * g_ref[...]).astype(BF16)
        hn_ref[...] = hn
        o = 0
        for w_ref, sp in zip(w_refs, splits):
            off = 0
            for n in sp:
                o_refs[o][...] = _dot(hn, w_ref[:, off:off + n])
                off += n
                o += 1

    r_ins, r_in_specs, r_outs, r_out_specs, r_sems = _ride_args(ride)
    n_out = 1 + len(widths)
    outs = pl.pallas_call(
        _riding(body, 2 + nw, n_out, 0, ride, 1), name=name, grid=(t // tm,),
        in_specs=[_rows(tm, d), _whole(g.shape)] + [_whole(w.shape) for w in ws] + r_in_specs,
        out_specs=[_rows(tm, d)] + [_rows(tm, n) for n in widths] + r_out_specs,
        out_shape=[jax.ShapeDtypeStruct((t, d), BF16)] + [jax.ShapeDtypeStruct((t, n), F32) for n in widths] + r_outs,
        scratch_shapes=r_sems,
        compiler_params=_params(("arbitrary",)),
    )(x, g, *ws, *r_ins)
    return outs[0], outs[1:n_out], _ride_results(ride, outs[n_out:])


def mm_nt(groups, name, ride=(), norm=None, tm=256):
    dys_all = [dy for dys, _ in groups for dy in dys]
    ws = [w for _, w in groups]
    t = dys_all[0].shape[0]
    k = ws[0].shape[0]
    tm = _tile(t, tm)
    n = len(dys_all)
    extra = list(norm) if norm else []

    def body(*refs):
        dy_refs, w_refs = refs[:n], refs[n:n + len(ws)]
        rest = refs[n + len(ws):]
        acc = None
        i = 0
        for (dys, _), w_ref in zip(groups, w_refs):
            off = 0
            for dy in dys:
                width = dy.shape[1]
                part = _dot_nt(dy_refs[i][...].astype(BF16), w_ref[:, off:off + width])
                acc = part if acc is None else acc + part
                off += width
                i += 1
        if norm:
            u_ref, g_ref, add_ref, o_ref, dg_ref = rest

            @pl.when(pl.program_id(0) == 0)
            def _():
                dg_ref[...] = jnp.zeros(dg_ref.shape, F32)

            du, dg_rows = _rms_bwd(acc, u_ref[...], g_ref[...])
            o_ref[...] = du + add_ref[...]
            dg_ref[...] += jnp.sum(dg_rows, axis=0, keepdims=True)
        else:
            rest[0][...] = acc

    n_out = 2 if norm else 1
    r_ins, r_in_specs, r_outs, r_out_specs, r_sems = _ride_args(ride)
    outs = pl.pallas_call(
        _riding(body, n + len(ws) + len(extra), n_out, 0, ride, 1), name=name, grid=(t // tm,),
        in_specs=[_rows(tm, dy.shape[1]) for dy in dys_all] + [_whole(w.shape) for w in ws]
        + ([_rows(tm, k), _whole((1, k)), _rows(tm, k)] if norm else []) + r_in_specs,
        out_specs=[_rows(tm, k)] + ([_whole((1, k))] if norm else []) + r_out_specs,
        out_shape=[jax.ShapeDtypeStruct((t, k), F32)] + ([jax.ShapeDtypeStruct((1, k), F32)] if norm else []) + r_outs,
        scratch_shapes=r_sems,
        compiler_params=_params(("arbitrary",)),
    )(*dys_all, *ws, *extra, *r_ins)
    return tuple(outs[:n_out]) + (_ride_results(ride, outs[n_out:]),)


def mm_tn(a, dys, name, col_shards=1, tm=512):
    t, k = a.shape
    tm = _tile(t, tm)
    n = len(dys)
    ntot = sum(dy.shape[1] for dy in dys)
    wsh = ntot // col_shards

    def body(a_ref, *refs):
        dy_refs, o_ref, acc = refs[:n], refs[n], refs[n + 1]

        @pl.when(pl.program_id(0) == 0)
        def _():
            acc[...] = jnp.zeros(acc.shape, F32)

        av = a_ref[...].astype(BF16)
        off = 0
        for dy_ref in dy_refs:
            width = dy_ref.shape[1]
            acc[:, off:off + width] += _dot_tn(av, dy_ref[...].astype(BF16))
            off += width

        @pl.when(pl.program_id(0) == pl.num_programs(0) - 1)
        def _():
            for j in range(col_shards):
                o_ref[j] = acc[:, j * wsh:(j + 1) * wsh].astype(o_ref.dtype)

    return pl.pallas_call(
        body, name=name, grid=(t // tm,),
        in_specs=[_rows(tm, k)] + [_rows(tm, dy.shape[1]) for dy in dys],
        out_specs=_whole((col_shards, k, wsh)),
        out_shape=jax.ShapeDtypeStruct((col_shards, k, wsh), BF16),
        scratch_shapes=[pltpu.VMEM((k, ntot), F32)],
        compiler_params=_params(("arbitrary",)),
    )(a, *dys)


def _rms_bwd(dz, u, g):
    d = u.shape[-1]
    inv = lax.rsqrt(jnp.mean(u * u, axis=-1, keepdims=True) + EPS)
    dzg = dz * g
    proj = jnp.sum(dzg * u, axis=-1, keepdims=True) * (1.0 / d)
    du = inv * (dzg - u * (inv * inv) * proj)
    dg_rows = dz * u * inv
    return du, dg_rows


def ffn_down_loss(act, wd, g, h1, target, name, tm=256):
    t, f = act.shape
    d = wd.shape[1]
    tm = _tile(t, tm)

    def body(a_ref, w_ref, g_ref, h_ref, t_ref, dy_ref, dff_ref, dact_ref, dg_ref, loss_ref):
        @pl.when(pl.program_id(0) == 0)
        def _():
            dg_ref[...] = jnp.zeros(dg_ref.shape, F32)
            loss_ref[...] = jnp.zeros(loss_ref.shape, F32)

        wv = w_ref[...]
        gv = g_ref[...]
        ff = _dot(a_ref[...], wv)
        inv = lax.rsqrt(jnp.mean(ff * ff, axis=-1, keepdims=True) + EPS)
        err = h_ref[...] + ff * inv * gv - t_ref[...]
        loss_ref[...] += jnp.sum(err * err, axis=0, keepdims=True)
        dy = err * (1.0 / d)
        dy_ref[...] = dy
        du, dg_rows = _rms_bwd(dy, ff, gv)
        dff = du.astype(BF16)
        dff_ref[...] = dff
        dg_ref[...] += jnp.sum(dg_rows, axis=0, keepdims=True)
        dact_ref[...] = _dot_nt(dff, wv)

    return pl.pallas_call(
        body, name=name, grid=(t // tm,),
        in_specs=[_rows(tm, f), _whole(wd.shape), _whole(g.shape), _rows(tm, d), _rows(tm, d)],
        out_specs=[_rows(tm, d), _rows(tm, d), _rows(tm, f), _whole((1, d)), _whole((1, d))],
        out_shape=[jax.ShapeDtypeStruct((t, d), F32), jax.ShapeDtypeStruct((t, d), BF16),
                   jax.ShapeDtypeStruct((t, f), F32), jax.ShapeDtypeStruct((1, d), F32),
                   jax.ShapeDtypeStruct((1, d), F32)],
        compiler_params=_params(("arbitrary",)),
    )(act, wd, g, h1, target)


def merge_out(y_rnn, o_att, gts, w_br, w_ba, w_out, g, x, name, tm=256):
    t = y_rnn.shape[0]
    d = w_br.shape[1]
    tm = _tile(t, tm)

    def body(y_ref, o_ref, g_ref, wbr_ref, wba_ref, wo_ref, gn_ref, x_ref, m_ref, br_ref, ba_ref, mix_ref, h_ref):
        br = _dot(y_ref[...].astype(BF16), wbr_ref[...])
        ba = _dot(o_ref[...].astype(BF16), wba_ref[...])
        gv = g_ref[...]
        merged = (_sig(gv[:, :d]) * br + _sig(gv[:, d:]) * ba).astype(BF16)
        m_ref[...] = merged
        br_ref[...] = br
        ba_ref[...] = ba
        mix = _dot(merged, wo_ref[...])
        mix_ref[...] = mix
        inv = lax.rsqrt(jnp.mean(mix * mix, axis=-1, keepdims=True) + EPS)
        h_ref[...] = x_ref[...] + mix * inv * gn_ref[...]

    sd = jax.ShapeDtypeStruct
    return pl.pallas_call(
        body, name=name, grid=(t // tm,),
        in_specs=[_rows(tm, y_rnn.shape[1]), _rows(tm, o_att.shape[1]), _rows(tm, 2 * d),
                  _whole(w_br.shape), _whole(w_ba.shape), _whole(w_out.shape), _whole(g.shape), _rows(tm, d)],
        out_specs=[_rows(tm, d)] * 5,
        out_shape=[sd((t, d), BF16), sd((t, d), F32), sd((t, d), F32), sd((t, d), F32), sd((t, d), F32)],
        compiler_params=_params(("parallel",)),
    )(y_rnn, o_att, gts, w_br, w_ba, w_out, g, x)


def mid_bwd(dhn2, h1, g_ffn, dy, mix, g_mix, w_out, gts, br, ba, w_br, w_ba, name, tm=256):
    t, d = h1.shape
    tm = _tile(t, tm)
    rnn, hkv = w_br.shape[0], w_ba.shape[0]

    def body(dhn_ref, h_ref, gf_ref, dy_ref, mix_ref, gm_ref, wo_ref, g_ref, br_ref, ba_ref, wbr_ref, wba_ref,
             dh_ref, dmix_ref, dbr_ref, dba_ref, dg_ref, dyr_ref, doa_ref, dgf_ref, dgm_ref):
        @pl.when(pl.program_id(0) == 0)
        def _():
            dgf_ref[...] = jnp.zeros(dgf_ref.shape, F32)
            dgm_ref[...] = jnp.zeros(dgm_ref.shape, F32)

        du, rows_f = _rms_bwd(dhn_ref[...], h_ref[...], gf_ref[...])
        dh1 = du + dy_ref[...]
        dh_ref[...] = dh1
        dgf_ref[...] += jnp.sum(rows_f, axis=0, keepdims=True)
        dmx, rows_m = _rms_bwd(dh1, mix_ref[...], gm_ref[...])
        dmix = dmx.astype(BF16)
        dmix_ref[...] = dmix
        dgm_ref[...] += jnp.sum(rows_m, axis=0, keepdims=True)
        dm = _dot_nt(dmix, wo_ref[...])
        gv = g_ref[...]
        sr = _sig(gv[:, :d])
        sa = _sig(gv[:, d:])
        dbr = (dm * sr).astype(BF16)
        dba = (dm * sa).astype(BF16)
        dbr_ref[...] = dbr
        dba_ref[...] = dba
        dg_ref[:, :d] = (dm * br_ref[...] * sr * (1.0 - sr)).astype(BF16)
        dg_ref[:, d:] = (dm * ba_ref[...] * sa * (1.0 - sa)).astype(BF16)
        dyr_ref[...] = _dot_nt(dbr, wbr_ref[...])
        doa_ref[...] = _dot_nt(dba, wba_ref[...])

    sd = jax.ShapeDtypeStruct
    row, vec = _rows(tm, d), _whole((1, d))
    return pl.pallas_call(
        body, name=name, grid=(t // tm,),
        in_specs=[row, row, vec, row, row, vec, _whole(w_out.shape), _rows(tm, 2 * d), row, row,
                  _whole(w_br.shape), _whole(w_ba.shape)],
        out_specs=[row, row, row, row, _rows(tm, 2 * d), _rows(tm, rnn), _rows(tm, hkv), vec, vec],
        out_shape=[sd((t, d), F32), sd((t, d), BF16), sd((t, d), BF16), sd((t, d), BF16), sd((t, 2 * d), BF16),
                   sd((t, rnn), F32), sd((t, hkv), F32), sd((1, d), F32), sd((1, d), F32)],
        compiler_params=_params(("arbitrary",)),
    )(dhn2, h1, g_ffn, dy, mix, g_mix, w_out, gts, br, ba, w_br, w_ba)


def _shift_dn(x, d, fill, row):
    return jnp.where(row >= d, pltpu.roll(x, d, 0), fill)


def _shift_up(x, d, fill, row):
    s = x.shape[0]
    return jnp.where(row < s - d, pltpu.roll(x, s - d, 0), fill)


def _conv_fwd(x, w, b, row):
    kk = w.shape[0]
    y = b + w[kk - 1:kk, :] * x
    for j in range(1, kk):
        y = y + w[kk - 1 - j:kk - j, :] * _shift_dn(x, j, 0.0, row)
    return y


def _conv_bwd(dy, x, w, row):
    kk = w.shape[0]
    dx = w[kk - 1:kk, :] * dy
    dws = [None] * kk
    dws[kk - 1] = jnp.sum(dy * x, axis=0, keepdims=True)
    for j in range(1, kk):
        dx = dx + w[kk - 1 - j:kk - j, :] * _shift_up(dy, j, 0.0, row)
        dws[kk - 1 - j] = jnp.sum(dy * _shift_dn(x, j, 0.0, row), axis=0, keepdims=True)
    return dx, jnp.concatenate(dws, axis=0)


def _softplus(z):
    y = jnp.exp(-jnp.abs(z))
    u = 1.0 + y
    dd = u - 1.0
    log1p = jnp.where(dd == 0.0, y, jnp.log(u) * (y / jnp.where(dd == 0.0, 1.0, dd)))
    return jnp.maximum(z, 0.0) + log1p


def _lru_decay(xb, wa, ba, lam):
    r = _sig(_dot(xb, wa) + ba)
    sp = _softplus(-lam)
    la = (-LRU_C) * r * sp
    return r, sp, la, jnp.exp(la)


def _lru_gates(xc, wa, ba, wx, bx, lam):
    xb = xc.astype(BF16)
    r, sp, la, a = _lru_decay(xb, wa, ba, lam)
    i = _sig(_dot(xb, wx) + bx)
    one_m_a2 = jnp.tanh(-la) * (1.0 + a * a)
    inv_mult = lax.rsqrt(one_m_a2)
    return r, i, sp, a, one_m_a2 * inv_mult, inv_mult


def _seg_len(s):
    seg = -(-s // 8)
    return seg + (4 - seg % 8) % 8


def _scan_rows(a_pad, u_pad, out_pad, reverse):
    planes, rows8, lanes = a_pad.shape
    seg = rows8 // 8
    sub = lax.broadcasted_iota(jnp.int32, (planes, 8, lanes), 1)

    unroll = 4

    def rows(k, d):
        i = k * unroll + d
        return pl.ds((seg - 1 - i) if reverse else i, 8, stride=seg)

    def ends(k, carry):
        h, p = carry
        for d in range(unroll):
            a = a_pad[:, rows(k, d), :]
            h = a * h + u_pad[:, rows(k, d), :]
            p = a * p
        return h, p

    init = (jnp.zeros((planes, 8, lanes), F32), jnp.ones((planes, 8, lanes), F32))
    h_end, p_end = lax.fori_loop(0, seg // unroll, ends, init)
    start = jnp.zeros((planes, 8, lanes), F32)
    for _ in range(7):
        nxt = h_end + p_end * start
        if reverse:
            start = jnp.where(sub < 7, pltpu.roll(nxt, 7, 1), 0.0)
        else:
            start = jnp.where(sub >= 1, pltpu.roll(nxt, 1, 1), 0.0)

    def redo(k, h):
        for d in range(unroll):
            h = a_pad[:, rows(k, d), :] * h + u_pad[:, rows(k, d), :]
            out_pad[:, rows(k, d), :] = h
        return h

    lax.fori_loop(0, seg // unroll, redo, start)


def _lru_cols(c, rb):
    return 2 * rb if c % (2 * rb) == 0 else rb


def rglru_fwd(xr, cw, cb, wa, ba, wx, bx, lam, name, ride=()):
    b, s, c = xr.shape
    rb = wa.shape[1]
    kk = cw.shape[0]
    cols = _lru_cols(c, rb)
    nj = cols // rb
    seg = _seg_len(s)

    def body(x_ref, cw_ref, cb_ref, wa_ref, ba_ref, wx_ref, bx_ref, lam_ref, h_ref, a_pad, u_pad, h_pad):
        row = lax.broadcasted_iota(jnp.int32, (s, rb), 0)
        for j in range(nj):
            cs = slice(j * rb, (j + 1) * rb)
            xc = _conv_fwd(x_ref[:, cs], cw_ref[:, cs], cb_ref[:, cs], row)
            _, i, _, a, mult, _ = _lru_gates(xc, wa_ref[j], ba_ref[:, cs], wx_ref[j], bx_ref[:, cs], lam_ref[:, cs])
            a_pad[j, 0:s, :] = a
            u_pad[j, 0:s, :] = mult * (i * xc)
        a_pad[:, s:, :] = jnp.ones((nj, 8 * seg - s, rb), F32)
        u_pad[:, s:, :] = jnp.zeros((nj, 8 * seg - s, rb), F32)
        _scan_rows(a_pad, u_pad, h_pad, False)
        for j in range(nj):
            h_ref[:, j * rb:(j + 1) * rb] = h_pad[j, 0:s, :]

    vec = pl.BlockSpec((1, cols), lambda bi, n: (0, n))
    seq = pl.BlockSpec((None, s, cols), lambda bi, n: (bi, 0, n))
    mat = pl.BlockSpec((nj, rb, rb), lambda bi, n: (n, 0, 0))
    r_ins, r_in_specs, r_outs, r_out_specs, r_sems = _ride_args(ride)
    outs = pl.pallas_call(
        _riding(body, 8, 1, 3, ride, 2), name=name, grid=(b, c // cols),
        in_specs=[seq, pl.BlockSpec((kk, cols), lambda bi, n: (0, n)), vec, mat, vec, mat, vec, vec] + r_in_specs,
        out_specs=[seq] + r_out_specs,
        out_shape=[jax.ShapeDtypeStruct((b, s, c), F32)] + r_outs,
        scratch_shapes=[pltpu.VMEM((nj, 8 * seg, rb), F32)] * 3 + r_sems,
        compiler_params=_params(("arbitrary", "arbitrary")),
    )(xr, cw, cb, wa, ba, wx, bx, lam, *r_ins)
    return outs[0], _ride_results(ride, outs[1:])


def rglru_bwd(xr, h, dh, cw, cb, wa, ba, wx, bx, lam, name, ride=()):
    b, s, c = xr.shape
    nb, rb = wa.shape[0], wa.shape[1]
    kk = cw.shape[0]
    cols = _lru_cols(c, rb)
    nj = cols // rb
    seg = _seg_len(s)

    def body(x_ref, h_ref, dh_ref, cw_ref, cb_ref, wa_ref, ba_ref, wx_ref, bx_ref, lam_ref,
             dx_ref, dcw_ref, dcb_ref, dwa_ref, dba_ref, dwx_ref, dbx_ref, dlam_ref, b_pad, g_pad, l_pad):
        @pl.when(pl.program_id(1) == 0)
        def _():
            for ref in (dcw_ref, dcb_ref, dwa_ref, dba_ref, dwx_ref, dbx_ref, dlam_ref):
                ref[...] = jnp.zeros(ref.shape, F32)

        row = lax.broadcasted_iota(jnp.int32, (s, rb), 0)

        def conv(j):
            cs = slice(j * rb, (j + 1) * rb)
            return _conv_fwd(x_ref[:, cs], cw_ref[:, cs], cb_ref[:, cs], row)

        for j in range(nj):
            cs = slice(j * rb, (j + 1) * rb)
            _, _, _, a = _lru_decay(conv(j).astype(BF16), wa_ref[j], ba_ref[:, cs], lam_ref[:, cs])
            b_pad[j, 0:s, :] = _shift_up(a, 1, 0.0, row)
            g_pad[j, 0:s, :] = dh_ref[:, j * rb:(j + 1) * rb]
        b_pad[:, s:, :] = jnp.zeros((nj, 8 * seg - s, rb), F32)
        g_pad[:, s:, :] = jnp.zeros((nj, 8 * seg - s, rb), F32)
        _scan_rows(b_pad, g_pad, l_pad, True)

        for j in range(nj):
            cs = slice(j * rb, (j + 1) * rb)
            x = x_ref[:, cs]
            cwv = cw_ref[:, cs]
            wav, wxv, lamv = wa_ref[j], wx_ref[j], lam_ref[:, cs]
            xc = conv(j)
            r, i, sp, a, mult, inv_mult = _lru_gates(xc, wav, ba_ref[:, cs], wxv, bx_ref[:, cs], lamv)
            lmb = l_pad[j, 0:s, :]
            h_prev = _shift_dn(h_ref[:, cs], 1, 0.0, row)
            da = lmb * h_prev
            ixc = i * xc
            dla = da * a - (lmb * ixc) * (a * a) * inv_mult
            di = lmb * mult * xc
            dxc = lmb * mult * i
            dr = dla * ((-LRU_C) * sp)
            dsp = jnp.sum(dla * ((-LRU_C) * r), axis=0, keepdims=True)
            dga = dr * r * (1.0 - r)
            dgx = di * i * (1.0 - i)
            dga_b, dgx_b = dga.astype(BF16), dgx.astype(BF16)
            xb = xc.astype(BF16)
            dwa_ref[j] += _dot_tn(xb, dga_b)
            dwx_ref[j] += _dot_tn(xb, dgx_b)
            dba_ref[:, cs] += jnp.sum(dga, axis=0, keepdims=True)
            dbx_ref[:, cs] += jnp.sum(dgx, axis=0, keepdims=True)
            dlam_ref[:, cs] += dsp * (-_sig(-lamv))
            dxc = dxc + _dot_nt(dga_b, wav) + _dot_nt(dgx_b, wxv)
            dcb_ref[:, cs] += jnp.sum(dxc, axis=0, keepdims=True)
            dx, dcw = _conv_bwd(dxc, x, cwv, row)
            dcw_ref[:, cs] += dcw
            dx_ref[:, cs] = dx.astype(dx_ref.dtype)

    vec = pl.BlockSpec((1, cols), lambda n, bi: (0, n))
    seq = pl.BlockSpec((None, s, cols), lambda n, bi: (bi, 0, n))
    mat = pl.BlockSpec((nj, rb, rb), lambda n, bi: (n, 0, 0))
    cws = pl.BlockSpec((kk, cols), lambda n, bi: (0, n))
    sd = jax.ShapeDtypeStruct
    r_ins, r_in_specs, r_outs, r_out_specs, r_sems = _ride_args(ride)
    outs = pl.pallas_call(
        _riding(body, 10, 8, 3, ride, 2), name=name, grid=(c // cols, b),
        in_specs=[seq, seq, seq, cws, vec, mat, vec, mat, vec, vec] + r_in_specs,
        out_specs=[seq, cws, vec, mat, vec, mat, vec, vec] + r_out_specs,
        out_shape=[sd((b, s, c), BF16), sd((kk, c), F32), sd((1, c), F32), sd((nb, rb, rb), F32),
                   sd((1, c), F32), sd((nb, rb, rb), F32), sd((1, c), F32), sd((1, c), F32)] + r_outs,
        scratch_shapes=[pltpu.VMEM((nj, 8 * seg, rb), F32)] * 3 + r_sems,
        compiler_params=_params(("arbitrary", "arbitrary")),
    )(xr, h, dh, cw, cb, wa, ba, wx, bx, lam, *r_ins)
    return outs[:8], _ride_results(ride, outs[8:])


_GELU_C = math.sqrt(2.0 / math.pi)


def _gelu_parts(x):
    th = jnp.tanh(_GELU_C * (x + 0.044715 * x * x * x))
    gel = 0.5 * x * (1.0 + th)
    dgel = 0.5 * (1.0 + th) + 0.5 * x * (1.0 - th * th) * _GELU_C * (1.0 + 3 * 0.044715 * x * x)
    return gel, dgel


def ffn_in_act(x, g, wg, wu, cw, cb, seq_len, name, tm=256):
    t, d = x.shape
    f = wg.shape[1]
    kk = cw.shape[0]
    tm = _tile(seq_len, tm)
    tiles_per_seq = seq_len // tm
    keep = 8
    assert kk - 1 <= keep

    def body(x_ref, g_ref, wg_ref, wu_ref, cw_ref, cb_ref, hn_ref, gp_ref, up_ref, act_ref, tail):
        @pl.when(pl.program_id(0) % tiles_per_seq == 0)
        def _():
            tail[...] = jnp.zeros(tail.shape, F32)

        xv = x_ref[...]
        inv = lax.rsqrt(jnp.mean(xv * xv, axis=-1, keepdims=True) + EPS)
        hn = (xv * inv * g_ref[...]).astype(BF16)
        hn_ref[...] = hn
        gp = _dot(hn, wg_ref[...])
        up = _dot(hn, wu_ref[...])
        gp_ref[...] = gp
        up_ref[...] = up
        cwv = cw_ref[...]
        row = lax.broadcasted_iota(jnp.int32, (tm, 1), 0)
        gate = _conv_fwd(gp, cwv, cb_ref[...], row)
        row8 = lax.broadcasted_iota(jnp.int32, (keep, 1), 0)
        prev = tail[...]
        fix = jnp.zeros((keep, f), F32)
        for j in range(1, kk):
            fix = fix + cwv[kk - 1 - j:kk - j, :] * jnp.where(row8 < j, pltpu.roll(prev, j, 0), 0.0)
        gate = jnp.concatenate([gate[:keep] + fix, gate[keep:]], axis=0)
        tail[...] = gp[tm - keep:, :]
        gel, _ = _gelu_parts(gate)
        act_ref[...] = (gel * up).astype(BF16)

    sd = jax.ShapeDtypeStruct
    return pl.pallas_call(
        body, name=name, grid=(t // tm,),
        in_specs=[_rows(tm, d), _whole(g.shape), _whole(wg.shape), _whole(wu.shape), _whole(cw.shape), _whole(cb.shape)],
        out_specs=[_rows(tm, d), _rows(tm, f), _rows(tm, f), _rows(tm, f)],
        out_shape=[sd((t, d), BF16), sd((t, f), F32), sd((t, f), F32), sd((t, f), BF16)],
        scratch_shapes=[pltpu.VMEM((keep, f), F32)],
        compiler_params=_params(("arbitrary",)),
    )(x, g, wg, wu, cw, cb)


def ffn_bwd(dact, gate_pre, up, cw, cb, name, cbk=256):
    b, s, f = gate_pre.shape
    kk = cw.shape[0]
    cbk = _tile(f, cbk)

    def body(da_ref, g_ref, u_ref, cw_ref, cb_ref, dg_ref, du_ref, dcw_ref, dcb_ref):
        @pl.when(pl.program_id(1) == 0)
        def _():
            dcw_ref[...] = jnp.zeros(dcw_ref.shape, F32)
            dcb_ref[...] = jnp.zeros(dcb_ref.shape, F32)

        row = lax.broadcasted_iota(jnp.int32, (s, cbk), 0)
        gp = g_ref[...]
        cwv = cw_ref[...]
        gate = _conv_fwd(gp, cwv, cb_ref[...], row)
        gel, dgel = _gelu_parts(gate)
        da = da_ref[...]
        du_ref[...] = (da * gel).astype(BF16)
        dgate = da * u_ref[...] * dgel
        dcb_ref[...] += jnp.sum(dgate, axis=0, keepdims=True)
        dgp, dcw = _conv_bwd(dgate, gp, cwv, row)
        dcw_ref[...] += dcw
        dg_ref[...] = dgp.astype(BF16)

    seq = pl.BlockSpec((None, s, cbk), lambda n, bi: (bi, 0, n))
    cws = pl.BlockSpec((kk, cbk), lambda n, bi: (0, n))
    vec = pl.BlockSpec((1, cbk), lambda n, bi: (0, n))
    sd = jax.ShapeDtypeStruct
    return pl.pallas_call(
        body, name=name, grid=(f // cbk, b),
        in_specs=[seq, seq, seq, cws, vec],
        out_specs=[seq, seq, cws, vec],
        out_shape=[sd((b, s, f), BF16), sd((b, s, f), BF16), sd((kk, f), F32), sd((1, f), F32)],
        compiler_params=_params(("parallel", "arbitrary")),
    )(dact, gate_pre, up, cw, cb)


def _t5_bucket(dist):
    max_exact = REL_BUCKETS // 2
    d = np.maximum(dist, 1).astype(np.float32)
    large = max_exact + np.log(d / max_exact) / math.log(REL_MAX_DIST / max_exact) * (REL_BUCKETS - max_exact)
    large = np.minimum(large.astype(np.int32), REL_BUCKETS - 1)
    return np.where(dist < max_exact, dist, large).astype(np.int32)


def _band(window, dilation):
    qi = np.arange(ATTN_BLOCK)[:, None]
    kj = np.arange(2 * ATTN_BLOCK)[None, :]
    delta = ATTN_BLOCK + qi - kj
    mask = (delta >= 0) & (delta <= window // dilation)
    bucket = _t5_bucket(np.maximum(delta, 0) * dilation)
    return mask, bucket


def _attn_blocks(s, r):
    m = s // r
    assert m % ATTN_BLOCK == 0, "sequence length must be a multiple of dilation * block"
    return m // ATTN_BLOCK


def _perm_load(ref, r):
    if r == 1:
        return ref[...]
    m = ref.shape[0] // r
    return jnp.concatenate([ref[pl.ds(c, m, stride=r), :] for c in range(r)], axis=0)


def _perm_store(ref, g, val, r, add=False):
    if r == 1:
        ref[g] = ref[g] + val if add else val
        return
    m = val.shape[0] // r
    for c in range(r):
        rows = pl.ds(c, m, stride=r)
        part = val[c * m:(c + 1) * m]
        ref[g, rows, :] = ref[g, rows, :] + part if add else part


def _blocks(x):
    return x.reshape(x.shape[0] // ATTN_BLOCK, ATTN_BLOCK, x.shape[1])


def _prev_blocks(x):
    return jnp.concatenate([x[:1], x[:-1]], axis=0)


def _next_blocks(x):
    return jnp.concatenate([x[1:], jnp.zeros_like(x[:1])], axis=0)


def _first_block_neg(s, r):
    nblk = s // ATTN_BLOCK
    idx = lax.broadcasted_iota(jnp.int32, (nblk, 1, 1), 0)
    return jnp.where(idx % _attn_blocks(s, r) == 0, NEG, 0.0)


def _bdot_nt(a, b):
    return lax.dot_general(a, b, (((2,), (2,)), ((0,), (0,))), preferred_element_type=F32)


def _bdot(a, b):
    return lax.dot_general(a, b, (((2,), (1,)), ((0,), (0,))), preferred_element_type=F32)


def _bdot_tn(a, b):
    return lax.dot_general(a, b, (((1,), (1,)), ((0,), (0,))), preferred_element_type=F32)


def attn_fwd(qkv, biasm, n_heads, name, ride=()):
    b, s, _ = qkv.shape
    h = n_heads
    scale = HEAD_DIM ** -0.5
    blk = ATTN_BLOCK

    def body(q1_ref, q2_ref, q3_ref, k_ref, v_ref, bias_ref, o_ref, lse_ref, acc, m_s, l_s):
        for g, q_ref in enumerate((q1_ref, q2_ref, q3_ref)):
            r = DILATED[g][1]
            first = _first_block_neg(s, r)
            q = _blocks(_perm_load(q_ref, r).astype(BF16))
            k = _blocks(_perm_load(k_ref, r).astype(BF16))
            v = _blocks(_perm_load(v_ref, r).astype(BF16))
            s_cur = _bdot_nt(q, k) * scale + bias_ref[g, :, blk:]
            s_prev = _bdot_nt(q, _prev_blocks(k)) * scale + bias_ref[g, :, :blk] + first
            m = jnp.maximum(jnp.max(s_cur, axis=-1, keepdims=True), jnp.max(s_prev, axis=-1, keepdims=True))
            p_cur = jnp.exp(s_cur - m)
            p_prev = jnp.exp(s_prev - m)
            l = jnp.sum(p_cur, axis=-1, keepdims=True) + jnp.sum(p_prev, axis=-1, keepdims=True)
            o = _bdot(p_cur.astype(BF16), v) + _bdot(p_prev.astype(BF16), _prev_blocks(v))
            _perm_store(acc, g, o.reshape(s, HEAD_DIM), r)
            _perm_store(m_s, g, m.reshape(s, 1), r)
            _perm_store(l_s, g, l.reshape(s, 1), r)
        m_all = jnp.maximum(jnp.maximum(m_s[0], m_s[1]), m_s[2])
        w = [jnp.exp(m_s[g] - m_all) for g in range(N_GROUPS)]
        l = w[0] * l_s[0] + w[1] * l_s[1] + w[2] * l_s[2]
        o_ref[...] = (w[0] * acc[0] + w[1] * acc[1] + w[2] * acc[2]) / l
        lse_ref[...] = m_all + jnp.log(l)

    def col(j):
        return pl.BlockSpec((None, s, HEAD_DIM), lambda bi, hi, j=j: (bi, 0, j * h + hi))

    r_ins, r_in_specs, r_outs, r_out_specs, r_sems = _ride_args(ride)
    outs = pl.pallas_call(
        _riding(body, 6, 2, 3, ride, 2), name=name, grid=(b, h),
        in_specs=[col(0), col(1), col(2), col(3), col(4),
                  pl.BlockSpec((N_GROUPS, None, blk, 2 * blk), lambda bi, hi: (0, hi, 0, 0))] + r_in_specs,
        out_specs=[pl.BlockSpec((None, s, HEAD_DIM), lambda bi, hi: (bi, 0, hi)),
                   pl.BlockSpec((None, None, s, 1), lambda bi, hi: (bi, hi, 0, 0))] + r_out_specs,
        out_shape=[jax.ShapeDtypeStruct((b, s, h * HEAD_DIM), F32), jax.ShapeDtypeStruct((b, h, s, 1), F32)] + r_outs,
        scratch_shapes=[pltpu.VMEM((N_GROUPS, s, HEAD_DIM), F32), pltpu.VMEM((N_GROUPS, s, 1), F32),
                        pltpu.VMEM((N_GROUPS, s, 1), F32)] + r_sems,
        compiler_params=_params(("arbitrary", "arbitrary")),
    )(qkv, qkv, qkv, qkv, qkv, biasm, *r_ins)
    return outs[0], outs[1], _ride_results(ride, outs[2:])


def attn_bwd(qkv, biasm, o, lse, do, n_heads, name, ride=()):
    b, s, _ = qkv.shape
    h = n_heads
    scale = HEAD_DIM ** -0.5
    blk = ATTN_BLOCK

    def body(q1_ref, q2_ref, q3_ref, k_ref, v_ref, bias_ref, o_ref, lse_ref, do_ref,
             dq1_ref, dq2_ref, dq3_ref, dk_ref, dv_ref, ds_ref, dq_acc, kv_acc, delta):
        delta[...] = jnp.sum(do_ref[...] * o_ref[...], axis=-1, keepdims=True)
        kv_acc[...] = jnp.zeros(kv_acc.shape, F32)
        for g, q_ref in enumerate((q1_ref, q2_ref, q3_ref)):
            r = DILATED[g][1]
            first = _first_block_neg(s, r)
            q = _blocks(_perm_load(q_ref, r).astype(BF16))
            k = _blocks(_perm_load(k_ref, r).astype(BF16))
            v = _blocks(_perm_load(v_ref, r).astype(BF16))
            dob = _blocks(_perm_load(do_ref, r).astype(BF16))
            lse_b = _blocks(_perm_load(lse_ref, r))
            dl_b = _blocks(_perm_load(delta, r))
            k_prev, v_prev = _prev_blocks(k), _prev_blocks(v)
            p_cur = jnp.exp(_bdot_nt(q, k) * scale + bias_ref[g, :, blk:] - lse_b)
            p_prev = jnp.exp(_bdot_nt(q, k_prev) * scale + bias_ref[g, :, :blk] + first - lse_b)
            ds_cur = p_cur * (_bdot_nt(dob, v) - dl_b)
            ds_prev = p_prev * (_bdot_nt(dob, v_prev) - dl_b)
            ds_ref[g, :, blk:] = jnp.sum(ds_cur, axis=0)
            ds_ref[g, :, :blk] = jnp.sum(ds_prev, axis=0)
            ds_cur_b, ds_prev_b = ds_cur.astype(BF16), ds_prev.astype(BF16)
            dq = (_bdot(ds_cur_b, k) + _bdot(ds_prev_b, k_prev)) * scale
            _perm_store(dq_acc, g, dq.reshape(s, HEAD_DIM), r)
            dk = (_bdot_tn(ds_cur_b, q) + _next_blocks(_bdot_tn(ds_prev_b, q))) * scale
            dv = _bdot_tn(p_cur.astype(BF16), dob) + _next_blocks(_bdot_tn(p_prev.astype(BF16), dob))
            _perm_store(kv_acc, 0, dk.reshape(s, HEAD_DIM), r, add=True)
            _perm_store(kv_acc, 1, dv.reshape(s, HEAD_DIM), r, add=True)
        for g, out_ref in enumerate((dq1_ref, dq2_ref, dq3_ref)):
            out_ref[...] = dq_acc[g].astype(out_ref.dtype)
        dk_ref[...] = kv_acc[0].astype(dk_ref.dtype)
        dv_ref[...] = kv_acc[1].astype(dv_ref.dtype)

    def col(j):
        return pl.BlockSpec((None, s, HEAD_DIM), lambda bi, hi, j=j: (bi, 0, j * h + hi))

    head = pl.BlockSpec((None, s, HEAD_DIM), lambda bi, hi: (bi, 0, hi))
    sd = jax.ShapeDtypeStruct
    r_ins, r_in_specs, r_outs, r_out_specs, r_sems = _ride_args(ride)
    outs = pl.pallas_call(
        _riding(body, 9, 6, 3, ride, 2), name=name, grid=(b, h),
        in_specs=[col(0), col(1), col(2), col(3), col(4),
                  pl.BlockSpec((N_GROUPS, None, blk, 2 * blk), lambda bi, hi: (0, hi, 0, 0)),
                  head, pl.BlockSpec((None, None, s, 1), lambda bi, hi: (bi, hi, 0, 0)), head] + r_in_specs,
        out_specs=[head] * 5 + [pl.BlockSpec((None, None, N_GROUPS, blk, 2 * blk), lambda bi, hi: (bi, hi, 0, 0, 0))]
        + r_out_specs,
        out_shape=[sd((b, s, h * HEAD_DIM), BF16)] * 5 + [sd((b, h, N_GROUPS, blk, 2 * blk), F32)] + r_outs,
        scratch_shapes=[pltpu.VMEM((N_GROUPS, s, HEAD_DIM), F32), pltpu.VMEM((2, s, HEAD_DIM), F32),
                        pltpu.VMEM((s, 1), F32)] + r_sems,
        compiler_params=_params(("arbitrary", "arbitrary")),
    )(qkv, qkv, qkv, qkv, qkv, biasm, o, lse, do, *r_ins)
    return outs[:6], _ride_results(ride, outs[6:])


def bias_table(rel_rows, bucket_f, n_heads, name):
    g, blk, blk2 = bucket_f.shape
    h = n_heads

    def body(rb_ref, bk_ref, o_ref):
        bk = bk_ref[...]
        rb = rb_ref[...]
        acc = jnp.full((blk, blk2), NEG, F32)
        for bucket in range(REL_BUCKETS):
            acc = jnp.where(bk == float(bucket), rb[:, bucket:bucket + 1], acc)
        o_ref[...] = acc

    return pl.pallas_call(
        body, name=name, grid=(g, h),
        in_specs=[pl.BlockSpec((None, 1, 128), lambda gi, hi: (gi * h + hi, 0, 0)),
                  pl.BlockSpec((None, blk, blk2), lambda gi, hi: (gi, 0, 0))],
        out_specs=pl.BlockSpec((None, None, blk, blk2), lambda gi, hi: (gi, hi, 0, 0)),
        out_shape=jax.ShapeDtypeStruct((g, h, blk, blk2), F32),
        compiler_params=_params(("parallel", "parallel")),
    )(rel_rows, bucket_f)


def bias_grad(ds_sum, bucket_f, name):
    b, h, g, blk, blk2 = ds_sum.shape

    def body(ds_ref, bk_ref, o_ref):
        tot = jnp.sum(ds_ref[...], axis=0)
        bk = bk_ref[...]
        lane = lax.broadcasted_iota(jnp.int32, (1, 128), 1)
        vec = jnp.zeros((1, 128), F32)
        for bucket in range(REL_BUCKETS):
            val = jnp.sum(jnp.where(bk == float(bucket), tot, 0.0), keepdims=True)
            vec = vec + jnp.where(lane == bucket, val, 0.0)
        o_ref[...] = vec

    return pl.pallas_call(
        body, name=name, grid=(g, h),
        in_specs=[pl.BlockSpec((b, None, None, blk, blk2), lambda gi, hi: (0, hi, gi, 0, 0)),
                  pl.BlockSpec((None, blk, blk2), lambda gi, hi: (gi, 0, 0))],
        out_specs=pl.BlockSpec((None, 1, 128), lambda gi, hi: (gi * h + hi, 0, 0)),
        out_shape=jax.ShapeDtypeStruct((g * h, 1, 128), F32),
        compiler_params=_params(("parallel", "parallel")),
    )(ds_sum, bucket_f)


def _chip_peers():
    x, y, c = lax.axis_index("x"), lax.axis_index("y"), lax.axis_index("c")
    me = 2 * x + y
    peers = [(1 - x, y, c), (x, 1 - y, c), (1 - x, 1 - y, c)]
    peer_chip = [2 * (1 - x) + y, 2 * x + (1 - y), 2 * (1 - x) + (1 - y)]
    return me, peers, peer_chip


def _any_specs(n):
    return [pl.BlockSpec(memory_space=pl.ANY)] * n


_MID_NUM, _MID_DEN = 3, 4


class _Exchange:
    def start(self, ins, outs, sems):
        local, sends, _ = self._copies(ins, outs, sems)
        for cp in local + sends:
            cp.start()

    def mid(self, ins, outs, sems):
        pass

    def wait(self, ins, outs, sems):
        local, sends, recvs = self._copies(ins, outs, sems)
        for cp in recvs():
            cp.wait_recv()
        for cp in sends:
            cp.wait_send()
        for cp in local:
            cp.wait()


class _Gather(_Exchange):
    HALF_ROWS = 16

    def __init__(self, arrays):
        n = len(arrays)
        self.ins = list(arrays)
        self.split = [a.shape[0] % (2 * self.HALF_ROWS) == 0 for a in arrays]
        self.out_shape = [jax.ShapeDtypeStruct((N_CHIPS,) + a.shape, a.dtype) for a in arrays]
        dma = pltpu.SemaphoreType.DMA
        self.sems = [dma((3 * n,)), dma((3 * n,)), dma((n,)), dma((3 * n,)), dma((3 * n,))]

    def _half(self, i, ref, sibling=False):
        if not self.split[i]:
            return ref
        half = self.ins[i].shape[0] // 2
        c = lax.axis_index("c")
        c = 1 - c if sibling else c
        return ref.at[pl.ds(pl.multiple_of(c * half, self.HALF_ROWS), half)]

    def _plan(self, ins, outs, sems):
        send1, recv1, local_sems, send2, recv2 = sems
        me, peers, peer_chip = _chip_peers()
        x, y, c = lax.axis_index("x"), lax.axis_index("y"), lax.axis_index("c")
        n = len(ins)
        pairs = [(i, k) for i in range(n) for k in range(3)]

        def fetch(i, k, slot):
            return pltpu.make_async_remote_copy(src_ref=self._half(i, ins[i]), dst_ref=self._half(i, outs[i].at[slot]),
                                                send_sem=send1.at[3 * i + k], recv_sem=recv1.at[3 * i + k],
                                                device_id=peers[k], device_id_type=MESH)

        def share(i, k, sibling):
            part = self._half(i, outs[i].at[peer_chip[k]], sibling)
            return pltpu.make_async_remote_copy(src_ref=part, dst_ref=part, send_sem=send2.at[3 * i + k],
                                                recv_sem=recv2.at[3 * i + k], device_id=(x, y, 1 - c),
                                                device_id_type=MESH)

        split_pairs = [(i, k) for i, k in pairs if self.split[i]]
        return dict(
            local=lambda: [pltpu.make_async_copy(ins[i], outs[i].at[me], local_sems.at[i]) for i in range(n)],
            fetch_out=lambda: [fetch(i, k, me) for i, k in pairs],
            fetch_in=lambda: [fetch(i, k, peer_chip[k]) for i, k in pairs],
            share_out=lambda: [share(i, k, False) for i, k in split_pairs],
            share_in=lambda: [share(i, k, True) for i, k in split_pairs])

    def start(self, ins, outs, sems):
        plan = self._plan(ins, outs, sems)
        for cp in plan["local"]() + plan["fetch_out"]():
            cp.start()

    def mid(self, ins, outs, sems):
        plan = self._plan(ins, outs, sems)
        for cp in plan["fetch_in"]():
            cp.wait_recv()
        for cp in plan["share_out"]():
            cp.start()

    def wait(self, ins, outs, sems):
        plan = self._plan(ins, outs, sems)
        for cp in plan["share_in"]():
            cp.wait_recv()
        for cp in plan["fetch_out"]() + plan["share_out"]():
            cp.wait_send()
        for cp in plan["local"]():
            cp.wait()


class _Scatter(_Exchange):
    def __init__(self, slabs, whole=()):
        self.n_slabs = len(slabs)
        self.ins = list(slabs) + list(whole)
        n = len(self.ins)
        self.out_shape = [jax.ShapeDtypeStruct(a.shape, a.dtype) for a in slabs] \
            + [jax.ShapeDtypeStruct((N_CHIPS,) + a.shape, a.dtype) for a in whole]
        self.sems = [pltpu.SemaphoreType.DMA((3 * n,)), pltpu.SemaphoreType.DMA((3 * n,)), pltpu.SemaphoreType.DMA((n,))]

    def _copies(self, ins, outs, sems):
        send_sems, recv_sems, local_sems = sems
        me, peers, peer_chip = _chip_peers()
        n = len(ins)

        def src(i, chip):
            return ins[i].at[chip] if i < self.n_slabs else ins[i]

        def remote(i, k, src_chip, slot):
            return pltpu.make_async_remote_copy(src_ref=src(i, src_chip), dst_ref=outs[i].at[slot],
                                                send_sem=send_sems.at[3 * i + k], recv_sem=recv_sems.at[3 * i + k],
                                                device_id=peers[k], device_id_type=MESH)

        local = [pltpu.make_async_copy(src(i, me), outs[i].at[me], local_sems.at[i]) for i in range(n)]
        sends = [remote(i, k, peer_chip[k], me) for i in range(n) for k in range(3)]
        return local, sends, lambda: [remote(i, k, me, peer_chip[k]) for i in range(n) for k in range(3)]


class _Swap(_Exchange):
    def __init__(self, arrays):
        n = len(arrays)
        self.ins = list(arrays)
        self.out_shape = [jax.ShapeDtypeStruct(a.shape, a.dtype) for a in arrays]
        self.sems = [pltpu.SemaphoreType.DMA((n,)), pltpu.SemaphoreType.DMA((n,))]

    def _copies(self, ins, outs, sems):
        send_sems, recv_sems = sems
        x, y, c = lax.axis_index("x"), lax.axis_index("y"), lax.axis_index("c")
        cps = [pltpu.make_async_remote_copy(src_ref=ins[i], dst_ref=outs[i], send_sem=send_sems.at[i],
                                            recv_sem=recv_sems.at[i], device_id=(x, y, 1 - c), device_id_type=MESH)
               for i in range(len(ins))]
        return [], cps, lambda: cps


def _riding(body, n_in, n_out, n_scratch, ride, rank):
    if not ride:
        return body
    r_in = sum(len(e.ins) for e in ride)
    r_out = sum(len(e.out_shape) for e in ride)

    def split(refs, sizes):
        out, a = [], 0
        for sz in sizes:
            out.append(refs[a:a + sz])
            a += sz
        return out

    def wrapped(*refs):
        a = 0
        parts = []
        for sz in (n_in, r_in, n_out, r_out, n_scratch):
            parts.append(refs[a:a + sz])
            a += sz
        own_in, ex_in, own_out, ex_out, own_scratch = parts
        ex_sems = refs[a:]
        ins = split(ex_in, [len(e.ins) for e in ride])
        outs = split(ex_out, [len(e.out_shape) for e in ride])
        sems = split(ex_sems, [len(e.sems) for e in ride])
        if rank:
            step, total = 0, 1
            for d in range(rank):
                step = step * pl.num_programs(d) + pl.program_id(d)
                total = total * pl.num_programs(d)

            @pl.when(step == 0)
            def _():
                for e, i, o, s in zip(ride, ins, outs, sems):
                    e.start(i, o, s)

            body(*own_in, *own_out, *own_scratch)

            @pl.when(step == (total * _MID_NUM) // _MID_DEN)
            def _():
                for e, i, o, s in zip(ride, ins, outs, sems):
                    e.mid(i, o, s)

            @pl.when(step == total - 1)
            def _():
                for e, i, o, s in zip(ride, ins, outs, sems):
                    e.wait(i, o, s)
        else:
            for phase in ("start", "mid", "wait"):
                for e, i, o, s in zip(ride, ins, outs, sems):
                    getattr(e, phase)(i, o, s)

    return wrapped


def _ride_args(ride):
    ins = [a for e in ride for a in e.ins]
    outs = [s for e in ride for s in e.out_shape]
    sems = [s for e in ride for s in e.sems]
    return ins, _any_specs(len(ins)), outs, _any_specs(len(outs)), sems


def _ride_results(ride, flat):
    out, a = [], 0
    for e in ride:
        out.append(list(flat[a:a + len(e.out_shape)]))
        a += len(e.out_shape)
    return out


def exchange(ride, name):
    ins, in_specs, outs, out_specs, sems = _ride_args(ride)
    res = pl.pallas_call(
        _riding(lambda: None, 0, 0, 0, ride, 0), name=name,
        in_specs=in_specs, out_specs=out_specs, out_shape=outs, scratch_shapes=sems,
    )(*ins)
    return _ride_results(ride, res)


def _sum_slots(ref):
    acc = ref[0].astype(F32)
    for j in range(1, ref.shape[0]):
        acc = acc + ref[j].astype(F32)
    return acc


def sum_pairs(mine, other, name, tr=176):
    n, r, w = mine.shape
    tr = _tile(r, tr)

    def body(a_ref, b_ref, o_ref):
        o_ref[...] = _sum_slots(a_ref) + _sum_slots(b_ref)

    spec = pl.BlockSpec((n, tr, w), lambda i: (0, i, 0))
    return pl.pallas_call(
        body, name=name, grid=(r // tr,),
        in_specs=[spec, spec], out_specs=_rows(tr, w),
        out_shape=jax.ShapeDtypeStruct((r, w), F32),
        compiler_params=_params(("parallel",)),
    )(mine, other)


def adamw(w, m, v, gs, name, tr=256):
    r, c = w.shape
    tr = r if r % 8 else _tile(r, tr)
    c1 = 1.0 - ADAM_B1 ** ADAM_STEP
    c2 = 1.0 - ADAM_B2 ** ADAM_STEP
    ng = len(gs)

    def body(w_ref, m_ref, v_ref, *refs):
        g_refs, (g_ref, d_ref, nm_ref, nv_ref) = refs[:ng], refs[ng:]
        g = g_refs[0][...] if ng == 1 else _sum_slots(g_refs[0]) + _sum_slots(g_refs[1])
        nm = ADAM_B1 * m_ref[...] + (1.0 - ADAM_B1) * g
        nv = ADAM_B2 * v_ref[...] + (1.0 - ADAM_B2) * (g * g)
        g_ref[...] = g
        nm_ref[...] = nm
        nv_ref[...] = nv
        d_ref[...] = (-ADAM_LR) * ((nm / c1) / (jnp.sqrt(nv / c2) + ADAM_EPS) + ADAM_WD * w_ref[...])

    spec = _rows(tr, c)
    gspec = spec if ng == 1 else pl.BlockSpec((N_CHIPS, tr, c), lambda i: (0, i, 0))
    return pl.pallas_call(
        body, name=name, grid=(r // tr,),
        in_specs=[spec] * 3 + [gspec] * ng, out_specs=[spec] * 4,
        out_shape=[jax.ShapeDtypeStruct((r, c), F32)] * 4,
        compiler_params=_params(("parallel",)),
    )(w, m, v, *gs)


_PARAMS = (
    ("rel_bias", None), ("norm_mix_pre", None), ("norm_mix_post", None), ("w_in", 1), ("conv_rnn_w", 1),
    ("conv_rnn_b", None), ("w_rg_a", None), ("b_rg_a", None), ("w_rg_x", None), ("b_rg_x", None),
    ("lru_lambda", None), ("w_branch_rnn", 0), ("w_branch_att", 1), ("w_out", 0), ("norm_ffn_pre", None),
    ("norm_ffn_post", None), ("w_ffn_gate", 1), ("w_ffn_up", 1), ("conv_ffn_w", 1), ("conv_ffn_b", None),
    ("w_ffn_down", 0),
)
_SMALL = 65536


def _as2d(a):
    a = a[0] if a.shape[0] == 1 and a.ndim >= 3 else a
    return a.reshape(-1, a.shape[-1]) if a.ndim == 3 else a


def _pack(pieces, dtype):
    flat = jnp.concatenate([p.astype(dtype).reshape(-1) for p in pieces])
    unit = PACK_W * PACK_ROWS
    pad = (-flat.shape[0]) % unit
    flat = jnp.pad(flat, (0, pad))
    return flat.reshape(-1, PACK_W)


def _unpack(buf, shapes):
    flat = buf.reshape(-1)
    out, off = [], 0
    for shp in shapes:
        n = int(np.prod(shp))
        out.append(flat[off:off + n].reshape(shp))
        off += n
    return out


def _join(slots, ax):
    if ax == 0:
        return slots.reshape(-1, slots.shape[-1])
    return jnp.transpose(slots, (1, 0, 2)).reshape(slots.shape[1], -1)


def _cut(full, ax):
    if ax == 0:
        return full.reshape(N_CHIPS, -1, full.shape[-1])
    return jnp.transpose(full.reshape(full.shape[0], N_CHIPS, -1), (1, 0, 2))


def kernel(x, rel_bias, norm_mix_pre, norm_mix_post, w_in, conv_rnn_w, conv_rnn_b, w_rg_a, b_rg_a, w_rg_x, b_rg_x, lru_lambda, w_branch_rnn, w_branch_att, w_out, norm_ffn_pre, norm_ffn_post, w_ffn_gate, w_ffn_up, conv_ffn_w, conv_ffn_b, w_ffn_down, loss_target, m_rel_bias, m_norm_mix_pre, m_norm_mix_post, m_w_in, m_conv_rnn_w, m_conv_rnn_b, m_w_rg_a, m_b_rg_a, m_w_rg_x, m_b_rg_x, m_lru_lambda, m_w_branch_rnn, m_w_branch_att, m_w_out, m_norm_ffn_pre, m_norm_ffn_post, m_w_ffn_gate, m_w_ffn_up, m_conv_ffn_w, m_conv_ffn_b, m_w_ffn_down, v_rel_bias, v_norm_mix_pre, v_norm_mix_post, v_w_in, v_conv_rnn_w, v_conv_rnn_b, v_w_rg_a, v_b_rg_a, v_w_rg_x, v_b_rg_x, v_lru_lambda, v_w_branch_rnn, v_w_branch_att, v_w_out, v_norm_ffn_pre, v_norm_ffn_post, v_w_ffn_gate, v_w_ffn_up, v_conv_ffn_w, v_conv_ffn_b, v_w_ffn_down):
    args = dict(locals())
    names = [n for n, _ in _PARAMS]
    axis = dict(_PARAMS)
    w_loc = {n: args[n] for n in names}
    m_loc = {n: args["m_" + n] for n in names}
    v_loc = {n: args["v_" + n] for n in names}
    sharded = [n for n in names if axis[n] is not None]
    replicated = [n for n in names if axis[n] is None]

    big = [n for n in sharded if w_loc[n].size >= _SMALL]
    small_sharded = [n for n in sharded if n not in big]
    small = replicated + small_sharded

    first = ["w_in"] + small_sharded
    srcs = [_as2d(w_loc[n]).astype(BF16) if n in big else _as2d(w_loc[n]) for n in first]
    (gathered,) = exchange([_Gather(srcs)], "gather_first")
    p = {n: _join(a, axis[n]) for n, a in zip(first, gathered)}
    for n in replicated:
        p[n] = _as2d(w_loc[n])
    shards = {n: _as2d(w_loc[n]).astype(BF16) for n in big if n not in first}

    received, sibling, g_small, loss_part = _local_step(x, loss_target, p, shards)

    pack = _pack([g_small[n] for n in small], BF16)
    ((received["small"],),) = exchange([_Scatter([], [pack])], "scatter_small")
    late = [n for n in received if n not in sibling]
    (swapped,) = exchange([_Swap([received[n] for n in late])], "swap_last")
    sibling.update(zip(late, swapped))
    small_sum = sum_pairs(received["small"], sibling["small"], "sum_small")
    g_tot = dict(zip(small, _unpack(small_sum, [g_small[n].shape for n in small])))
    chip = 2 * lax.axis_index("x") + lax.axis_index("y")
    for n in small_sharded:
        size = g_tot[n].shape[axis[n]] // N_CHIPS
        g_tot[n] = lax.dynamic_slice_in_dim(g_tot[n], chip * size, size, axis=axis[n])

    out_g, out_d, out_m, out_v = {}, {}, {}, {}
    for i, n in enumerate(names):
        shp = w_loc[n].shape
        gs = (received[n], sibling[n]) if n in big else (g_tot[n],)
        g, d, nm, nv = adamw(_as2d(w_loc[n]), _as2d(m_loc[n]), _as2d(v_loc[n]), gs, "adamw_" + n)
        out_g[n], out_d[n], out_m[n], out_v[n] = (t.reshape(shp) for t in (g, d, nm, nv))

    d_model = x.shape[-1]
    loss = lax.psum(0.5 * jnp.sum(loss_part) / d_model, ("x", "y", "c"))
    grad_x = g_small["x"]
    return (loss, grad_x, *[out_g[n] for n in names], *[out_d[n] for n in names],
            *[out_m[n] for n in names], *[out_v[n] for n in names])


def _local_step(x, target, p, shards):
    axis = dict(_PARAMS)
    b, s, d = x.shape
    t = b * s
    rnn = p["b_rg_a"].shape[1]
    ffn = p["conv_ffn_b"].shape[1]
    nbk = rnn // p["w_rg_a"].shape[1]
    hkv = (p["w_in"].shape[1] - rnn - 2 * d) // (N_GROUPS + 2)
    h = hkv // HEAD_DIM
    nq = N_GROUPS * hkv

    x2 = x.reshape(t, d)
    tgt = target.reshape(t, d)
    w_in = p["w_in"]
    in_splits = (rnn, nq + 2 * hkv, 2 * d)
    wa = p["w_rg_a"].reshape(nbk, -1, p["w_rg_a"].shape[1]).astype(BF16)
    wx = p["w_rg_x"].reshape(nbk, -1, p["w_rg_x"].shape[1]).astype(BF16)
    cw_r, cb_r = p["conv_rnn_w"], p["conv_rnn_b"]
    cw_f, cb_f = p["conv_ffn_w"], p["conv_ffn_b"]

    masks, buckets = zip(*[_band(w_, r_) for w_, r_ in DILATED])
    bucket_f = jnp.asarray(np.where(np.stack(masks), np.stack(buckets), -1).astype(np.float32))
    rel_rows = jnp.pad(p["rel_bias"].T, ((0, 0), (0, 128 - REL_BUCKETS)))[:, None, :]
    biasm = bias_table(rel_rows, bucket_f, h, "bias_table")

    early = ["w_branch_rnn", "w_branch_att", "w_out"]
    hn1, (xr, qkv, gts), (got,) = norm_mm(x2, p["norm_mix_pre"], [w_in], [in_splits], "in_proj",
                                          ride=[_Gather([shards[n] for n in early])])
    p.update({n: _join(a, axis[n]) for n, a in zip(early, got)})
    xr3 = xr.reshape(b, s, rnn)
    y_rnn, (got,) = rglru_fwd(xr3, cw_r, cb_r, wa, p["b_rg_a"], wx, p["b_rg_x"], p["lru_lambda"], "rglru_fwd",
                              ride=[_Gather([shards[n] for n in ("w_ffn_gate", "w_ffn_up")])])
    p.update({n: _join(a, axis[n]) for n, a in zip(("w_ffn_gate", "w_ffn_up"), got)})
    qkv3 = qkv.reshape(b, s, -1)
    o_att, lse, ((got,),) = attn_fwd(qkv3, biasm, h, "attn_fwd", ride=[_Gather([shards["w_ffn_down"]])])
    p["w_ffn_down"] = _join(got, axis["w_ffn_down"])
    merged, br, ba, mix, h1 = merge_out(y_rnn.reshape(t, rnn), o_att.reshape(t, hkv), gts, p["w_branch_rnn"],
                                        p["w_branch_att"], p["w_out"], p["norm_mix_post"], x2, "merge_out")
    hn2, gate_pre, up, act = ffn_in_act(h1, p["norm_ffn_pre"], p["w_ffn_gate"], p["w_ffn_up"], cw_f, cb_f, s, "ffn_in")

    g, gb = {}, {}
    recv, sib = {}, {}

    def rows4(a):
        return a.reshape(N_CHIPS, -1, a.shape[-1])

    dy, dff, dact, g["norm_ffn_post"], loss_part = ffn_down_loss(act, p["w_ffn_down"], p["norm_ffn_post"], h1, tgt,
                                                                  "ffn_down")
    gb["w_ffn_down"] = rows4(mm_tn(act, [dff], "ffn_down_dw"))
    dgp, dup, g["conv_ffn_w"], g["conv_ffn_b"] = ffn_bwd(dact.reshape(b, s, ffn), gate_pre.reshape(b, s, ffn),
                                                        up.reshape(b, s, ffn), cw_f, cb_f, "ffn_bwd")
    dgp, dup = dgp.reshape(t, ffn), dup.reshape(t, ffn)
    dhn2, ((recv["w_ffn_down"],),) = mm_nt([([dgp], p["w_ffn_gate"]), ([dup], p["w_ffn_up"])], "ffn_in_dx",
                                           ride=[_Scatter([gb["w_ffn_down"]])])
    gb["w_ffn_gate"] = mm_tn(hn2, [dgp], "ffn_gate_dw", col_shards=N_CHIPS)
    gb["w_ffn_up"] = mm_tn(hn2, [dup], "ffn_up_dw", col_shards=N_CHIPS)
    dh1, dmix, dbr, dba, dgts, dy_rnn, do_att, g["norm_ffn_pre"], g["norm_mix_post"] = mid_bwd(
        dhn2, h1, p["norm_ffn_pre"], dy, mix, p["norm_mix_post"], p["w_out"], gts, br, ba,
        p["w_branch_rnn"], p["w_branch_att"], "mid_bwd")
    gb["w_out"] = rows4(mm_tn(merged, [dmix], "out_proj_dw"))
    gb["w_branch_rnn"] = rows4(mm_tn(y_rnn.reshape(t, rnn), [dbr], "branch_rnn_dw"))
    gb["w_branch_att"] = mm_tn(o_att.reshape(t, hkv), [dba], "branch_att_dw", col_shards=N_CHIPS)
    ffn_in = ["w_ffn_gate", "w_ffn_up"]
    (dxr, g["conv_rnn_w"], g["conv_rnn_b"], dwa, g["b_rg_a"], dwx, g["b_rg_x"], g["lru_lambda"]), (got,) = rglru_bwd(
        xr3, y_rnn, dy_rnn.reshape(b, s, rnn), cw_r, cb_r, wa, p["b_rg_a"], wx, p["b_rg_x"], p["lru_lambda"], "rglru_bwd",
        ride=[_Scatter([gb[n] for n in ffn_in])])
    recv.update(zip(ffn_in, got))
    g["w_rg_a"] = dwa.reshape(p["w_rg_a"].shape)
    g["w_rg_x"] = dwx.reshape(p["w_rg_x"].shape)
    mid = ["w_out", "w_branch_rnn", "w_branch_att"]
    early_recv = ["w_ffn_down"] + ffn_in
    (dq1, dq2, dq3, dk, dv, ds_sum), (got, swapped) = attn_bwd(
        qkv3, biasm, o_att, lse, do_att.reshape(b, s, hkv), h, "attn_bwd",
        ride=[_Scatter([gb[n] for n in mid]), _Swap([recv[n] for n in early_recv])])
    recv.update(zip(mid, got))
    sib.update(zip(early_recv, swapped))
    rows = bias_grad(ds_sum, bucket_f, "bias_grad")
    g["rel_bias"] = rows[:, 0, :REL_BUCKETS].T
    dproj = [dxr.reshape(t, rnn)] + [a.reshape(t, hkv) for a in (dq1, dq2, dq3, dk, dv)] + [dgts]
    dw_a = mm_tn(hn1, dproj[:4], "in_proj_dw_a")[0]
    dw_b = mm_tn(hn1, dproj[4:], "in_proj_dw_b")[0]
    gb["w_in"] = _cut(jnp.concatenate([dw_a, dw_b], axis=1), 1)
    dx, g["norm_mix_pre"], ((recv["w_in"],), got) = mm_nt(
        [(dproj, w_in)], "in_proj_dx", norm=(x2, p["norm_mix_pre"], dh1),
        ride=[_Scatter([gb["w_in"]]), _Swap([recv[n] for n in mid])])
    sib.update(zip(mid, got))
    g["x"] = dx.reshape(b, s, d)
    return recv, sib, g, loss_part
```

```python
import functools
import math

import numpy as np
import jax
import jax.numpy as jnp
from jax import lax
from jax.experimental import pallas as pl
from jax.experimental.pallas import tpu as pltpu

F32 = jnp.float32
BF16 = jnp.bfloat16

EPS = 1e-6
HEAD_DIM = 128
ATTN_BLOCK = 128
DILATED = ((128, 1), (512, 4), (2048, 16))
N_GROUPS = len(DILATED)
REL_BUCKETS = 32
REL_MAX_DIST = 2048
LRU_C = 8.0
NEG = -1e30

ADAM_LR = 0.001
ADAM_B1 = 0.9
ADAM_B2 = 0.999
ADAM_EPS = 1e-08
ADAM_WD = 0.01
ADAM_STEP = 10

N_CHIPS = 4
PACK_W = 1024
PACK_ROWS = 16
VMEM_LIMIT = 56 * 1024 * 1024
MESH = pl.DeviceIdType.MESH


def _params(sem=None):
    return pltpu.CompilerParams(dimension_semantics=sem, vmem_limit_bytes=VMEM_LIMIT)


def _dot(a, b):
    return jnp.dot(a, b, preferred_element_type=F32)


def _dot_nt(a, b):
    return lax.dot_general(a, b, (((1,), (1,)), ((), ())), preferred_element_type=F32)


def _dot_tn(a, b):
    return lax.dot_general(a, b, (((0,), (0,)), ((), ())), preferred_element_type=F32)


def _sig(x):
    return 0.5 * jnp.tanh(0.5 * x) + 0.5


def _rows(tm, w):
    return pl.BlockSpec((tm, w), lambda i: (i, 0))


def _whole(shape):
    nd = len(shape)
    return pl.BlockSpec(tuple(shape), lambda *_: (0,) * nd)


def _tile(t, want):
    while t % want:
        want //= 2
    return want


def norm_mm(x, g, ws, splits, name, ride=(), tm=256):
    t, d = x.shape
    tm = _tile(t, tm)
    nw = len(ws)
    widths = [n for sp in splits for n in sp]

    def body(x_ref, g_ref, *refs):
        w_refs, hn_ref, o_refs = refs[:nw], refs[nw], refs[nw + 1:]
        xv = x_ref[...]
        inv = lax.rsqrt(jnp.mean(xv * xv, axis=-1, keepdims=True) + EPS)
        hn = (xv * inv * g_ref[...]).astype(BF16)
        hn_ref[...] = hn
        o = 0
        for w_ref, sp in zip(w_refs, splits):
            off = 0
            for n in sp:
                o_refs[o][...] = _dot(hn, w_ref[:, off:off + n])
                off += n
                o += 1

    r_ins, r_in_specs, r_outs, r_out_specs, r_sems = _ride_args(ride)
    n_out = 1 + len(widths)
    outs = pl.pallas_call(
        _riding(body, 2 + nw, n_out, 0, ride, 1), name=name, grid=(t // tm,),
        in_specs=[_rows(tm, d), _whole(g.shape)] + [_whole(w.shape) for w in ws] + r_in_specs,
        out_specs=[_rows(tm, d)] + [_rows(tm, n) for n in widths] + r_out_specs,
        out_shape=[jax.ShapeDtypeStruct((t, d), BF16)] + [jax.ShapeDtypeStruct((t, n), F32) for n in widths] + r_outs,
        scratch_shapes=r_sems,
        compiler_params=_params(("arbitrary",)),
    )(x, g, *ws, *r_ins)
    return outs[0], outs[1:n_out], _ride_results(ride, outs[n_out:])


def mm_nt(groups, name, ride=(), norm=None, tm=256):
    dys_all = [dy for dys, _ in groups for dy in dys]
    ws = [w for _, w in groups]
    t = dys_all[0].shape[0]
    k = ws[0].shape[0]
    tm = _tile(t, tm)
    n = len(dys_all)
    extra = list(norm) if norm else []

    def body(*refs):
        dy_refs, w_refs = refs[:n], refs[n:n + len(ws)]
        rest = refs[n + len(ws):]
        acc = None
        i = 0
        for (dys, _), w_ref in zip(groups, w_refs):
            off = 0
            for dy in dys:
                width = dy.shape[1]
                part = _dot_nt(dy_refs[i][...].astype(BF16), w_ref[:, off:off + width])
                acc = part if acc is None else acc + part
                off += width
                i += 1
        if norm:
            u_ref, g_ref, add_ref, o_ref, dg_ref = rest

            @pl.when(pl.program_id(0) == 0)
            def _():
                dg_ref[...] = jnp.zeros(dg_ref.shape, F32)

            du, dg_rows = _rms_bwd(acc, u_ref[...], g_ref[...])
            o_ref[...] = du + add_ref[...]
            dg_ref[...] += jnp.sum(dg_rows, axis=0, keepdims=True)
        else:
            rest[0][...] = acc

    n_out = 2 if norm else 1
    r_ins, r_in_specs, r_outs, r_out_specs, r_sems = _ride_args(ride)
    outs = pl.pallas_call(
        _riding(body, n + len(ws) + len(extra), n_out, 0, ride, 1), name=name, grid=(t // tm,),
        in_specs=[_rows(tm, dy.shape[1]) for dy in dys_all] + [_whole(w.shape) for w in ws]
        + ([_rows(tm, k), _whole((1, k)), _rows(tm, k)] if norm else []) + r_in_specs,
        out_specs=[_rows(tm, k)] + ([_whole((1, k))] if norm else []) + r_out_specs,
        out_shape=[jax.ShapeDtypeStruct((t, k), F32)] + ([jax.ShapeDtypeStruct((1, k), F32)] if norm else []) + r_outs,
        scratch_shapes=r_sems,
        compiler_params=_params(("arbitrary",)),
    )(*dys_all, *ws, *extra, *r_ins)
    return tuple(outs[:n_out]) + (_ride_results(ride, outs[n_out:]),)


def mm_tn(a, dys, name, col_shards=1, tm=1024):
    t, k = a.shape
    tm = _tile(t, tm)
    n = len(dys)
    ntot = sum(dy.shape[1] for dy in dys)
    wsh = ntot // col_shards

    def body(a_ref, *refs):
        dy_refs, o_ref, acc = refs[:n], refs[n], refs[n + 1]

        @pl.when(pl.program_id(0) == 0)
        def _():
            acc[...] = jnp.zeros(acc.shape, F32)

        av = a_ref[...].astype(BF16)
        off = 0
        for dy_ref in dy_refs:
            width = dy_ref.shape[1]
            acc[:, off:off + width] += _dot_tn(av, dy_ref[...].astype(BF16))
            off += width

        @pl.when(pl.program_id(0) == pl.num_programs(0) - 1)
        def _():
            for j in range(col_shards):
                o_ref[j] = acc[:, j * wsh:(j + 1) * wsh].astype(o_ref.dtype)

    return pl.pallas_call(
        body, name=name, grid=(t // tm,),
        in_specs=[_rows(tm, k)] + [_rows(tm, dy.shape[1]) for dy in dys],
        out_specs=_whole((col_shards, k, wsh)),
        out_shape=jax.ShapeDtypeStruct((col_shards, k, wsh), BF16),
        scratch_shapes=[pltpu.VMEM((k, ntot), F32)],
        compiler_params=_params(("arbitrary",)),
    )(a, *dys)


def _rms_bwd(dz, u, g):
    d = u.shape[-1]
    inv = lax.rsqrt(jnp.mean(u * u, axis=-1, keepdims=True) + EPS)
    dzg = dz * g
    proj = jnp.sum(dzg * u, axis=-1, keepdims=True) * (1.0 / d)
    du = inv * (dzg - u * (inv * inv) * proj)
    dg_rows = dz * u * inv
    return du, dg_rows


def ffn_down_loss(act, wd, g, h1, target, name, tm=256):
    t, f = act.shape
    d = wd.shape[1]
    tm = _tile(t, tm)

    def body(a_ref, w_ref, g_ref, h_ref, t_ref, dy_ref, dff_ref, dact_ref, dg_ref, loss_ref):
        @pl.when(pl.program_id(0) == 0)
        def _():
            dg_ref[...] = jnp.zeros(dg_ref.shape, F32)
            loss_ref[...] = jnp.zeros(loss_ref.shape, F32)

        wv = w_ref[...]
        gv = g_ref[...]
        ff = _dot(a_ref[...], wv)
        inv = lax.rsqrt(jnp.mean(ff * ff, axis=-1, keepdims=True) + EPS)
        err = h_ref[...] + ff * inv * gv - t_ref[...]
        loss_ref[...] += jnp.sum(err * err, axis=0, keepdims=True)
        dy = err * (1.0 / d)
        dy_ref[...] = dy
        du, dg_rows = _rms_bwd(dy, ff, gv)
        dff = du.astype(BF16)
        dff_ref[...] = dff
        dg_ref[...] += jnp.sum(dg_rows, axis=0, keepdims=True)
        dact_ref[...] = _dot_nt(dff, wv)

    return pl.pallas_call(
        body, name=name, grid=(t // tm,),
        in_specs=[_rows(tm, f), _whole(wd.shape), _whole(g.shape), _rows(tm, d), _rows(tm, d)],
        out_specs=[_rows(tm, d), _rows(tm, d), _rows(tm, f), _whole((1, d)), _whole((1, d))],
        out_shape=[jax.ShapeDtypeStruct((t, d), F32), jax.ShapeDtypeStruct((t, d), BF16),
                   jax.ShapeDtypeStruct((t, f), F32), jax.ShapeDtypeStruct((1, d), F32),
                   jax.ShapeDtypeStruct((1, d), F32)],
        compiler_params=_params(("arbitrary",)),
    )(act, wd, g, h1, target)


def merge_out(y_rnn, o_att, gts, w_br, w_ba, w_out, g, x, name, tm=256):
    t = y_rnn.shape[0]
    d = w_br.shape[1]
    tm = _tile(t, tm)

    def body(y_ref, o_ref, g_ref, wbr_ref, wba_ref, wo_ref, gn_ref, x_ref, m_ref, br_ref, ba_ref, mix_ref, h_ref):
        br = _dot(y_ref[...].astype(BF16), wbr_ref[...])
        ba = _dot(o_ref[...].astype(BF16), wba_ref[...])
        gv = g_ref[...]
        merged = (_sig(gv[:, :d]) * br + _sig(gv[:, d:]) * ba).astype(BF16)
        m_ref[...] = merged
        br_ref[...] = br
        ba_ref[...] = ba
        mix = _dot(merged, wo_ref[...])
        mix_ref[...] = mix
        inv = lax.rsqrt(jnp.mean(mix * mix, axis=-1, keepdims=True) + EPS)
        h_ref[...] = x_ref[...] + mix * inv * gn_ref[...]

    sd = jax.ShapeDtypeStruct
    return pl.pallas_call(
        body, name=name, grid=(t // tm,),
        in_specs=[_rows(tm, y_rnn.shape[1]), _rows(tm, o_att.shape[1]), _rows(tm, 2 * d),
                  _whole(w_br.shape), _whole(w_ba.shape), _whole(w_out.shape), _whole(g.shape), _rows(tm, d)],
        out_specs=[_rows(tm, d)] * 5,
        out_shape=[sd((t, d), BF16), sd((t, d), F32), sd((t, d), F32), sd((t, d), F32), sd((t, d), F32)],
        compiler_params=_params(("parallel",)),
    )(y_rnn, o_att, gts, w_br, w_ba, w_out, g, x)


def mid_bwd(dhn2, h1, g_ffn, dy, mix, g_mix, w_out, gts, br, ba, w_br, w_ba, name, tm=256):
    t, d = h1.shape
    tm = _tile(t, tm)
    rnn, hkv = w_br.shape[0], w_ba.shape[0]

    def body(dhn_ref, h_ref, gf_ref, dy_ref, mix_ref, gm_ref, wo_ref, g_ref, br_ref, ba_ref, wbr_ref, wba_ref,
             dh_ref, dmix_ref, dbr_ref, dba_ref, dg_ref, dyr_ref, doa_ref, dgf_ref, dgm_ref):
        @pl.when(pl.program_id(0) == 0)
        def _():
            dgf_ref[...] = jnp.zeros(dgf_ref.shape, F32)
            dgm_ref[...] = jnp.zeros(dgm_ref.shape, F32)

        du, rows_f = _rms_bwd(dhn_ref[...], h_ref[...], gf_ref[...])
        dh1 = du + dy_ref[...]
        dh_ref[...] = dh1
        dgf_ref[...] += jnp.sum(rows_f, axis=0, keepdims=True)
        dmx, rows_m = _rms_bwd(dh1, mix_ref[...], gm_ref[...])
        dmix = dmx.astype(BF16)
        dmix_ref[...] = dmix
        dgm_ref[...] += jnp.sum(rows_m, axis=0, keepdims=True)
        dm = _dot_nt(dmix, wo_ref[...])
        gv = g_ref[...]
        sr = _sig(gv[:, :d])
        sa = _sig(gv[:, d:])
        dbr = (dm * sr).astype(BF16)
        dba = (dm * sa).astype(BF16)
        dbr_ref[...] = dbr
        dba_ref[...] = dba
        dg_ref[:, :d] = (dm * br_ref[...] * sr * (1.0 - sr)).astype(BF16)
        dg_ref[:, d:] = (dm * ba_ref[...] * sa * (1.0 - sa)).astype(BF16)
        dyr_ref[...] = _dot_nt(dbr, wbr_ref[...])
        doa_ref[...] = _dot_nt(dba, wba_ref[...])

    sd = jax.ShapeDtypeStruct
    row, vec = _rows(tm, d), _whole((1, d))
    return pl.pallas_call(
        body, name=name, grid=(t // tm,),
        in_specs=[row, row, vec, row, row, vec, _whole(w_out.shape), _rows(tm, 2 * d), row, row,
                  _whole(w_br.shape), _whole(w_ba.shape)],
        out_specs=[row, row, row, row, _rows(tm, 2 * d), _rows(tm, rnn), _rows(tm, hkv), vec, vec],
        out_shape=[sd((t, d), F32), sd((t, d), BF16), sd((t, d), BF16), sd((t, d), BF16), sd((t, 2 * d), BF16),
                   sd((t, rnn), F32), sd((t, hkv), F32), sd((1, d), F32), sd((1, d), F32)],
        compiler_params=_params(("arbitrary",)),
    )(dhn2, h1, g_ffn, dy, mix, g_mix, w_out, gts, br, ba, w_br, w_ba)


def _shift_dn(x, d, fill, row):
    return jnp.where(row >= d, pltpu.roll(x, d, 0), fill)


def _shift_up(x, d, fill, row):
    s = x.shape[0]
    return jnp.where(row < s - d, pltpu.roll(x, s - d, 0), fill)


def _conv_fwd(x, w, b, row):
    kk = w.shape[0]
    y = b + w[kk - 1:kk, :] * x
    for j in range(1, kk):
        y = y + w[kk - 1 - j:kk - j, :] * _shift_dn(x, j, 0.0, row)
    return y


def _conv_bwd(dy, x, w, row):
    kk = w.shape[0]
    dx = w[kk - 1:kk, :] * dy
    dws = [None] * kk
    dws[kk - 1] = jnp.sum(dy * x, axis=0, keepdims=True)
    for j in range(1, kk):
        dx = dx + w[kk - 1 - j:kk - j, :] * _shift_up(dy, j, 0.0, row)
        dws[kk - 1 - j] = jnp.sum(dy * _shift_dn(x, j, 0.0, row), axis=0, keepdims=True)
    return dx, jnp.concatenate(dws, axis=0)


def _softplus(z):
    y = jnp.exp(-jnp.abs(z))
    u = 1.0 + y
    dd = u - 1.0
    log1p = jnp.where(dd == 0.0, y, jnp.log(u) * (y / jnp.where(dd == 0.0, 1.0, dd)))
    return jnp.maximum(z, 0.0) + log1p


def _lru_decay(xb, wa, ba, lam):
    r = _sig(_dot(xb, wa) + ba)
    sp = _softplus(-lam)
    la = (-LRU_C) * r * sp
    return r, sp, la, jnp.exp(la)


def _lru_gates(xc, wa, ba, wx, bx, lam):
    xb = xc.astype(BF16)
    r, sp, la, a = _lru_decay(xb, wa, ba, lam)
    i = _sig(_dot(xb, wx) + bx)
    one_m_a2 = jnp.tanh(-la) * (1.0 + a * a)
    inv_mult = lax.rsqrt(one_m_a2)
    return r, i, sp, a, one_m_a2 * inv_mult, inv_mult


def _seg_len(s):
    seg = -(-s // 8)
    return seg + (4 - seg % 8) % 8


def _scan_rows(a_pad, u_pad, out_pad, reverse):
    planes, rows8, lanes = a_pad.shape
    seg = rows8 // 8
    sub = lax.broadcasted_iota(jnp.int32, (planes, 8, lanes), 1)

    unroll = 4

    def rows(k, d):
        i = k * unroll + d
        return pl.ds((seg - 1 - i) if reverse else i, 8, stride=seg)

    def ends(k, carry):
        h, p = carry
        for d in range(unroll):
            a = a_pad[:, rows(k, d), :]
            h = a * h + u_pad[:, rows(k, d), :]
            p = a * p
        return h, p

    init = (jnp.zeros((planes, 8, lanes), F32), jnp.ones((planes, 8, lanes), F32))
    h_end, p_end = lax.fori_loop(0, seg // unroll, ends, init)
    start = jnp.zeros((planes, 8, lanes), F32)
    for _ in range(7):
        nxt = h_end + p_end * start
        if reverse:
            start = jnp.where(sub < 7, pltpu.roll(nxt, 7, 1), 0.0)
        else:
            start = jnp.where(sub >= 1, pltpu.roll(nxt, 1, 1), 0.0)

    def redo(k, h):
        for d in range(unroll):
            h = a_pad[:, rows(k, d), :] * h + u_pad[:, rows(k, d), :]
            out_pad[:, rows(k, d), :] = h
        return h

    lax.fori_loop(0, seg // unroll, redo, start)


def _lru_cols(c, rb):
    return 2 * rb if c % (2 * rb) == 0 else rb


def rglru_fwd(xr, cw, cb, wa, ba, wx, bx, lam, name, ride=()):
    b, s, c = xr.shape
    rb = wa.shape[1]
    kk = cw.shape[0]
    cols = _lru_cols(c, rb)
    nj = cols // rb
    seg = _seg_len(s)

    def body(x_ref, cw_ref, cb_ref, wa_ref, ba_ref, wx_ref, bx_ref, lam_ref, h_ref, a_ref, xc_ref, a_pad, u_pad, h_pad):
        row = lax.broadcasted_iota(jnp.int32, (s, rb), 0)
        for j in range(nj):
            cs = slice(j * rb, (j + 1) * rb)
            xc = _conv_fwd(x_ref[:, cs], cw_ref[:, cs], cb_ref[:, cs], row)
            _, i, _, a, mult, _ = _lru_gates(xc, wa_ref[j], ba_ref[:, cs], wx_ref[j], bx_ref[:, cs], lam_ref[:, cs])
            xc_ref[:, cs] = xc
            a_ref[:, cs] = a
            a_pad[j, 0:s, :] = a
            u_pad[j, 0:s, :] = mult * (i * xc)
        a_pad[:, s:, :] = jnp.ones((nj, 8 * seg - s, rb), F32)
        u_pad[:, s:, :] = jnp.zeros((nj, 8 * seg - s, rb), F32)
        _scan_rows(a_pad, u_pad, h_pad, False)
        for j in range(nj):
            h_ref[:, j * rb:(j + 1) * rb] = h_pad[j, 0:s, :]

    vec = pl.BlockSpec((1, cols), lambda bi, n: (0, n))
    seq = pl.BlockSpec((None, s, cols), lambda bi, n: (bi, 0, n))
    mat = pl.BlockSpec((nj, rb, rb), lambda bi, n: (n, 0, 0))
    r_ins, r_in_specs, r_outs, r_out_specs, r_sems = _ride_args(ride)
    outs = pl.pallas_call(
        _riding(body, 8, 3, 3, ride, 2), name=name, grid=(b, c // cols),
        in_specs=[seq, pl.BlockSpec((kk, cols), lambda bi, n: (0, n)), vec, mat, vec, mat, vec, vec] + r_in_specs,
        out_specs=[seq] * 3 + r_out_specs,
        out_shape=[jax.ShapeDtypeStruct((b, s, c), F32)] * 3 + r_outs,
        scratch_shapes=[pltpu.VMEM((nj, 8 * seg, rb), F32)] * 3 + r_sems,
        compiler_params=_params(("arbitrary", "arbitrary")),
    )(xr, cw, cb, wa, ba, wx, bx, lam, *r_ins)
    return outs[:3], _ride_results(ride, outs[3:])


def rglru_bwd(xr, h, dh, a_fwd, xc_fwd, cw, cb, wa, ba, wx, bx, lam, name, ride=()):
    b, s, c = xr.shape
    nb, rb = wa.shape[0], wa.shape[1]
    kk = cw.shape[0]
    cols = _lru_cols(c, rb)
    nj = cols // rb
    seg = _seg_len(s)

    def body(x_ref, h_ref, dh_ref, a_ref, xc_ref, cw_ref, cb_ref, wa_ref, ba_ref, wx_ref, bx_ref, lam_ref,
             dx_ref, dcw_ref, dcb_ref, dwa_ref, dba_ref, dwx_ref, dbx_ref, dlam_ref, b_pad, g_pad, l_pad):
        @pl.when(pl.program_id(1) == 0)
        def _():
            for ref in (dcw_ref, dcb_ref, dwa_ref, dba_ref, dwx_ref, dbx_ref, dlam_ref):
                ref[...] = jnp.zeros(ref.shape, F32)

        row = lax.broadcasted_iota(jnp.int32, (s, rb), 0)

        for j in range(nj):
            b_pad[j, 0:s, :] = _shift_up(a_ref[:, j * rb:(j + 1) * rb], 1, 0.0, row)
            g_pad[j, 0:s, :] = dh_ref[:, j * rb:(j + 1) * rb]
        b_pad[:, s:, :] = jnp.zeros((nj, 8 * seg - s, rb), F32)
        g_pad[:, s:, :] = jnp.zeros((nj, 8 * seg - s, rb), F32)
        _scan_rows(b_pad, g_pad, l_pad, True)

        for j in range(nj):
            cs = slice(j * rb, (j + 1) * rb)
            x = x_ref[:, cs]
            cwv = cw_ref[:, cs]
            wav, wxv, lamv = wa_ref[j], wx_ref[j], lam_ref[:, cs]
            xc = xc_ref[:, cs]
            r, i, sp, a, mult, inv_mult = _lru_gates(xc, wav, ba_ref[:, cs], wxv, bx_ref[:, cs], lamv)
            lmb = l_pad[j, 0:s, :]
            h_prev = _shift_dn(h_ref[:, cs], 1, 0.0, row)
            da = lmb * h_prev
            ixc = i * xc
            dla = da * a - (lmb * ixc) * (a * a) * inv_mult
            di = lmb * mult * xc
            dxc = lmb * mult * i
            dr = dla * ((-LRU_C) * sp)
            dsp = jnp.sum(dla * ((-LRU_C) * r), axis=0, keepdims=True)
            dga = dr * r * (1.0 - r)
            dgx = di * i * (1.0 - i)
            dga_b, dgx_b = dga.astype(BF16), dgx.astype(BF16)
            xb = xc.astype(BF16)
            dwa_ref[j] += _dot_tn(xb, dga_b)
            dwx_ref[j] += _dot_tn(xb, dgx_b)
            dba_ref[:, cs] += jnp.sum(dga, axis=0, keepdims=True)
            dbx_ref[:, cs] += jnp.sum(dgx, axis=0, keepdims=True)
            dlam_ref[:, cs] += dsp * (-_sig(-lamv))
            dxc = dxc + _dot_nt(dga_b, wav) + _dot_nt(dgx_b, wxv)
            dcb_ref[:, cs] += jnp.sum(dxc, axis=0, keepdims=True)
            dx, dcw = _conv_bwd(dxc, x, cwv, row)
            dcw_ref[:, cs] += dcw
            dx_ref[:, cs] = dx.astype(dx_ref.dtype)

    vec = pl.BlockSpec((1, cols), lambda n, bi: (0, n))
    seq = pl.BlockSpec((None, s, cols), lambda n, bi: (bi, 0, n))
    mat = pl.BlockSpec((nj, rb, rb), lambda n, bi: (n, 0, 0))
    cws = pl.BlockSpec((kk, cols), lambda n, bi: (0, n))
    sd = jax.ShapeDtypeStruct
    r_ins, r_in_specs, r_outs, r_out_specs, r_sems = _ride_args(ride)
    outs = pl.pallas_call(
        _riding(body, 12, 8, 3, ride, 2), name=name, grid=(c // cols, b),
        in_specs=[seq, seq, seq, seq, seq, cws, vec, mat, vec, mat, vec, vec] + r_in_specs,
        out_specs=[seq, cws, vec, mat, vec, mat, vec, vec] + r_out_specs,
        out_shape=[sd((b, s, c), BF16), sd((kk, c), F32), sd((1, c), F32), sd((nb, rb, rb), F32),
                   sd((1, c), F32), sd((nb, rb, rb), F32), sd((1, c), F32), sd((1, c), F32)] + r_outs,
        scratch_shapes=[pltpu.VMEM((nj, 8 * seg, rb), F32)] * 3 + r_sems,
        compiler_params=_params(("arbitrary", "arbitrary")),
    )(xr, h, dh, a_fwd, xc_fwd, cw, cb, wa, ba, wx, bx, lam, *r_ins)
    return outs[:8], _ride_results(ride, outs[8:])


_GELU_C = math.sqrt(2.0 / math.pi)


def _gelu_parts(x):
    th = jnp.tanh(_GELU_C * (x + 0.044715 * x * x * x))
    gel = 0.5 * x * (1.0 + th)
    dgel = 0.5 * (1.0 + th) + 0.5 * x * (1.0 - th * th) * _GELU_C * (1.0 + 3 * 0.044715 * x * x)
    return gel, dgel


def ffn_in_act(x, g, wg, wu, cw, cb, seq_len, name, tm=256):
    t, d = x.shape
    f = wg.shape[1]
    kk = cw.shape[0]
    tm = _tile(seq_len, tm)
    tiles_per_seq = seq_len // tm
    keep = 8
    assert kk - 1 <= keep

    def body(x_ref, g_ref, wg_ref, wu_ref, cw_ref, cb_ref, hn_ref, gp_ref, up_ref, act_ref, tail):
        @pl.when(pl.program_id(0) % tiles_per_seq == 0)
        def _():
            tail[...] = jnp.zeros(tail.shape, F32)

        xv = x_ref[...]
        inv = lax.rsqrt(jnp.mean(xv * xv, axis=-1, keepdims=True) + EPS)
        hn = (xv * inv * g_ref[...]).astype(BF16)
        hn_ref[...] = hn
        gp = _dot(hn, wg_ref[...])
        up = _dot(hn, wu_ref[...])
        gp_ref[...] = gp
        up_ref[...] = up
        cwv = cw_ref[...]
        row = lax.broadcasted_iota(jnp.int32, (tm, 1), 0)
        gate = _conv_fwd(gp, cwv, cb_ref[...], row)
        row8 = lax.broadcasted_iota(jnp.int32, (keep, 1), 0)
        prev = tail[...]
        fix = jnp.zeros((keep, f), F32)
        for j in range(1, kk):
            fix = fix + cwv[kk - 1 - j:kk - j, :] * jnp.where(row8 < j, pltpu.roll(prev, j, 0), 0.0)
        gate = jnp.concatenate([gate[:keep] + fix, gate[keep:]], axis=0)
        tail[...] = gp[tm - keep:, :]
        gel, _ = _gelu_parts(gate)
        act_ref[...] = (gel * up).astype(BF16)

    sd = jax.ShapeDtypeStruct
    return pl.pallas_call(
        body, name=name, grid=(t // tm,),
        in_specs=[_rows(tm, d), _whole(g.shape), _whole(wg.shape), _whole(wu.shape), _whole(cw.shape), _whole(cb.shape)],
        out_specs=[_rows(tm, d), _rows(tm, f), _rows(tm, f), _rows(tm, f)],
        out_shape=[sd((t, d), BF16), sd((t, f), F32), sd((t, f), F32), sd((t, f), BF16)],
        scratch_shapes=[pltpu.VMEM((keep, f), F32)],
        compiler_params=_params(("arbitrary",)),
    )(x, g, wg, wu, cw, cb)


def ffn_bwd(dact, gate_pre, up, cw, cb, name, cbk=256):
    b, s, f = gate_pre.shape
    kk = cw.shape[0]
    cbk = _tile(f, cbk)

    def body(da_ref, g_ref, u_ref, cw_ref, cb_ref, dg_ref, du_ref, dcw_ref, dcb_ref):
        @pl.when(pl.program_id(1) == 0)
        def _():
            dcw_ref[...] = jnp.zeros(dcw_ref.shape, F32)
            dcb_ref[...] = jnp.zeros(dcb_ref.shape, F32)

        row = lax.broadcasted_iota(jnp.int32, (s, cbk), 0)
        gp = g_ref[...]
        cwv = cw_ref[...]
        gate = _conv_fwd(gp, cwv, cb_ref[...], row)
        gel, dgel = _gelu_parts(gate)
        da = da_ref[...]
        du_ref[...] = (da * gel).astype(BF16)
        dgate = da * u_ref[...] * dgel
        dcb_ref[...] += jnp.sum(dgate, axis=0, keepdims=True)
        dgp, dcw = _conv_bwd(dgate, gp, cwv, row)
        dcw_ref[...] += dcw
        dg_ref[...] = dgp.astype(BF16)

    seq = pl.BlockSpec((None, s, cbk), lambda n, bi: (bi, 0, n))
    cws = pl.BlockSpec((kk, cbk), lambda n, bi: (0, n))
    vec = pl.BlockSpec((1, cbk), lambda n, bi: (0, n))
    sd = jax.ShapeDtypeStruct
    return pl.pallas_call(
        body, name=name, grid=(f // cbk, b),
        in_specs=[seq, seq, seq, cws, vec],
        out_specs=[seq, seq, cws, vec],
        out_shape=[sd((b, s, f), BF16), sd((b, s, f), BF16), sd((kk, f), F32), sd((1, f), F32)],
        compiler_params=_params(("parallel", "arbitrary")),
    )(dact, gate_pre, up, cw, cb)


def _t5_bucket(dist):
    max_exact = REL_BUCKETS // 2
    d = np.maximum(dist, 1).astype(np.float32)
    large = max_exact + np.log(d / max_exact) / math.log(REL_MAX_DIST / max_exact) * (REL_BUCKETS - max_exact)
    large = np.minimum(large.astype(np.int32), REL_BUCKETS - 1)
    return np.where(dist < max_exact, dist, large).astype(np.int32)


def _band(window, dilation):
    qi = np.arange(ATTN_BLOCK)[:, None]
    kj = np.arange(2 * ATTN_BLOCK)[None, :]
    delta = ATTN_BLOCK + qi - kj
    mask = (delta >= 0) & (delta <= window // dilation)
    bucket = _t5_bucket(np.maximum(delta, 0) * dilation)
    return mask, bucket


def _attn_blocks(s, r):
    m = s // r
    assert m % ATTN_BLOCK == 0, "sequence length must be a multiple of dilation * block"
    return m // ATTN_BLOCK


def _perm_load(ref, r):
    if r == 1:
        return ref[...]
    m = ref.shape[0] // r
    return jnp.concatenate([ref[pl.ds(c, m, stride=r), :] for c in range(r)], axis=0)


def _perm_store(ref, g, val, r, add=False):
    if r == 1:
        ref[g] = ref[g] + val if add else val
        return
    m = val.shape[0] // r
    for c in range(r):
        rows = pl.ds(c, m, stride=r)
        part = val[c * m:(c + 1) * m]
        ref[g, rows, :] = ref[g, rows, :] + part if add else part


def _blocks(x):
    return x.reshape(x.shape[0] // ATTN_BLOCK, ATTN_BLOCK, x.shape[1])


def _prev_blocks(x):
    return jnp.concatenate([x[:1], x[:-1]], axis=0)


def _next_blocks(x):
    return jnp.concatenate([x[1:], jnp.zeros_like(x[:1])], axis=0)


def _first_block_neg(s, r):
    nblk = s // ATTN_BLOCK
    idx = lax.broadcasted_iota(jnp.int32, (nblk, 1, 1), 0)
    return jnp.where(idx % _attn_blocks(s, r) == 0, NEG, 0.0)


def _bdot_nt(a, b):
    return lax.dot_general(a, b, (((2,), (2,)), ((0,), (0,))), preferred_element_type=F32)


def _bdot(a, b):
    return lax.dot_general(a, b, (((2,), (1,)), ((0,), (0,))), preferred_element_type=F32)


def _bdot_tn(a, b):
    return lax.dot_general(a, b, (((1,), (1,)), ((0,), (0,))), preferred_element_type=F32)


def attn_fwd(qkv, biasm, n_heads, name, ride=()):
    b, s, _ = qkv.shape
    h = n_heads
    scale = HEAD_DIM ** -0.5
    blk = ATTN_BLOCK

    def body(q1_ref, q2_ref, q3_ref, k_ref, v_ref, bias_ref, o_ref, lse_ref, acc, m_s, l_s):
        for g, q_ref in enumerate((q1_ref, q2_ref, q3_ref)):
            r = DILATED[g][1]
            first = _first_block_neg(s, r)
            q = _blocks(_perm_load(q_ref, r).astype(BF16))
            k = _blocks(_perm_load(k_ref, r).astype(BF16))
            v = _blocks(_perm_load(v_ref, r).astype(BF16))
            s_cur = _bdot_nt(q, k) * scale + bias_ref[g, :, blk:]
            s_prev = _bdot_nt(q, _prev_blocks(k)) * scale + bias_ref[g, :, :blk] + first
            m = jnp.maximum(jnp.max(s_cur, axis=-1, keepdims=True), jnp.max(s_prev, axis=-1, keepdims=True))
            p_cur = jnp.exp(s_cur - m)
            p_prev = jnp.exp(s_prev - m)
            l = jnp.sum(p_cur, axis=-1, keepdims=True) + jnp.sum(p_prev, axis=-1, keepdims=True)
            o = _bdot(p_cur.astype(BF16), v) + _bdot(p_prev.astype(BF16), _prev_blocks(v))
            _perm_store(acc, g, o.reshape(s, HEAD_DIM), r)
            _perm_store(m_s, g, m.reshape(s, 1), r)
            _perm_store(l_s, g, l.reshape(s, 1), r)
        m_all = jnp.maximum(jnp.maximum(m_s[0], m_s[1]), m_s[2])
        w = [jnp.exp(m_s[g] - m_all) for g in range(N_GROUPS)]
        l = w[0] * l_s[0] + w[1] * l_s[1] + w[2] * l_s[2]
        o_ref[...] = (w[0] * acc[0] + w[1] * acc[1] + w[2] * acc[2]) / l
        lse_ref[...] = m_all + jnp.log(l)

    def col(j):
        return pl.BlockSpec((None, s, HEAD_DIM), lambda bi, hi, j=j: (bi, 0, j * h + hi))

    r_ins, r_in_specs, r_outs, r_out_specs, r_sems = _ride_args(ride)
    outs = pl.pallas_call(
        _riding(body, 6, 2, 3, ride, 2), name=name, grid=(b, h),
        in_specs=[col(0), col(1), col(2), col(3), col(4),
                  pl.BlockSpec((N_GROUPS, None, blk, 2 * blk), lambda bi, hi: (0, hi, 0, 0))] + r_in_specs,
        out_specs=[pl.BlockSpec((None, s, HEAD_DIM), lambda bi, hi: (bi, 0, hi)),
                   pl.BlockSpec((None, None, s, 1), lambda bi, hi: (bi, hi, 0, 0))] + r_out_specs,
        out_shape=[jax.ShapeDtypeStruct((b, s, h * HEAD_DIM), F32), jax.ShapeDtypeStruct((b, h, s, 1), F32)] + r_outs,
        scratch_shapes=[pltpu.VMEM((N_GROUPS, s, HEAD_DIM), F32), pltpu.VMEM((N_GROUPS, s, 1), F32),
                        pltpu.VMEM((N_GROUPS, s, 1), F32)] + r_sems,
        compiler_params=_params(("arbitrary", "arbitrary")),
    )(qkv, qkv, qkv, qkv, qkv, biasm, *r_ins)
    return outs[0], outs[1], _ride_results(ride, outs[2:])


def attn_bwd(qkv, biasm, o, lse, do, n_heads, name, ride=()):
    b, s, _ = qkv.shape
    h = n_heads
    scale = HEAD_DIM ** -0.5
    blk = ATTN_BLOCK

    def body(q1_ref, q2_ref, q3_ref, k_ref, v_ref, bias_ref, o_ref, lse_ref, do_ref,
             dq1_ref, dq2_ref, dq3_ref, dk_ref, dv_ref, ds_ref, dq_acc, kv_acc, delta):
        delta[...] = jnp.sum(do_ref[...] * o_ref[...], axis=-1, keepdims=True)
        kv_acc[...] = jnp.zeros(kv_acc.shape, F32)
        for g, q_ref in enumerate((q1_ref, q2_ref, q3_ref)):
            r = DILATED[g][1]
            first = _first_block_neg(s, r)
            q = _blocks(_perm_load(q_ref, r).astype(BF16))
            k = _blocks(_perm_load(k_ref, r).astype(BF16))
            v = _blocks(_perm_load(v_ref, r).astype(BF16))
            dob = _blocks(_perm_load(do_ref, r).astype(BF16))
            lse_b = _blocks(_perm_load(lse_ref, r))
            dl_b = _blocks(_perm_load(delta, r))
            k_prev, v_prev = _prev_blocks(k), _prev_blocks(v)
            p_cur = jnp.exp(_bdot_nt(q, k) * scale + bias_ref[g, :, blk:] - lse_b)
            p_prev = jnp.exp(_bdot_nt(q, k_prev) * scale + bias_ref[g, :, :blk] + first - lse_b)
            ds_cur = p_cur * (_bdot_nt(dob, v) - dl_b)
            ds_prev = p_prev * (_bdot_nt(dob, v_prev) - dl_b)
            ds_ref[g, :, blk:] = jnp.sum(ds_cur, axis=0)
            ds_ref[g, :, :blk] = jnp.sum(ds_prev, axis=0)
            ds_cur_b, ds_prev_b = ds_cur.astype(BF16), ds_prev.astype(BF16)
            dq = (_bdot(ds_cur_b, k) + _bdot(ds_prev_b, k_prev)) * scale
            _perm_store(dq_acc, g, dq.reshape(s, HEAD_DIM), r)
            dk = (_bdot_tn(ds_cur_b, q) + _next_blocks(_bdot_tn(ds_prev_b, q))) * scale
            dv = _bdot_tn(p_cur.astype(BF16), dob) + _next_blocks(_bdot_tn(p_prev.astype(BF16), dob))
            _perm_store(kv_acc, 0, dk.reshape(s, HEAD_DIM), r, add=True)
            _perm_store(kv_acc, 1, dv.reshape(s, HEAD_DIM), r, add=True)
        for g, out_ref in enumerate((dq1_ref, dq2_ref, dq3_ref)):
            out_ref[...] = dq_acc[g].astype(out_ref.dtype)
        dk_ref[...] = kv_acc[0].astype(dk_ref.dtype)
        dv_ref[...] = kv_acc[1].astype(dv_ref.dtype)

    def col(j):
        return pl.BlockSpec((None, s, HEAD_DIM), lambda bi, hi, j=j: (bi, 0, j * h + hi))

    head = pl.BlockSpec((None, s, HEAD_DIM), lambda bi, hi: (bi, 0, hi))
    sd = jax.ShapeDtypeStruct
    r_ins, r_in_specs, r_outs, r_out_specs, r_sems = _ride_args(ride)
    outs = pl.pallas_call(
        _riding(body, 9, 6, 3, ride, 2), name=name, grid=(b, h),
        in_specs=[col(0), col(1), col(2), col(3), col(4),
                  pl.BlockSpec((N_GROUPS, None, blk, 2 * blk), lambda bi, hi: (0, hi, 0, 0)),
                  head, pl.BlockSpec((None, None, s, 1), lambda bi, hi: (bi, hi, 0, 0)), head] + r_in_specs,
        out_specs=[head] * 5 + [pl.BlockSpec((None, None, N_GROUPS, blk, 2 * blk), lambda bi, hi: (bi, hi, 0, 0, 0))]
        + r_out_specs,
        out_shape=[sd((b, s, h * HEAD_DIM), BF16)] * 5 + [sd((b, h, N_GROUPS, blk, 2 * blk), F32)] + r_outs,
        scratch_shapes=[pltpu.VMEM((N_GROUPS, s, HEAD_DIM), F32), pltpu.VMEM((2, s, HEAD_DIM), F32),
                        pltpu.VMEM((s, 1), F32)] + r_sems,
        compiler_params=_params(("arbitrary", "arbitrary")),
    )(qkv, qkv, qkv, qkv, qkv, biasm, o, lse, do, *r_ins)
    return outs[:6], _ride_results(ride, outs[6:])


def bias_table(rel_rows, bucket_f, n_heads, name):
    g, blk, blk2 = bucket_f.shape
    h = n_heads

    def body(rb_ref, bk_ref, o_ref):
        bk = bk_ref[...]
        rb = rb_ref[...]
        acc = jnp.full((blk, blk2), NEG, F32)
        for bucket in range(REL_BUCKETS):
            acc = jnp.where(bk == float(bucket), rb[:, bucket:bucket + 1], acc)
        o_ref[...] = acc

    return pl.pallas_call(
        body, name=name, grid=(g, h),
        in_specs=[pl.BlockSpec((None, 1, 128), lambda gi, hi: (gi * h + hi, 0, 0)),
                  pl.BlockSpec((None, blk, blk2), lambda gi, hi: (gi, 0, 0))],
        out_specs=pl.BlockSpec((None, None, blk, blk2), lambda gi, hi: (gi, hi, 0, 0)),
        out_shape=jax.ShapeDtypeStruct((g, h, blk, blk2), F32),
        compiler_params=_params(("parallel", "parallel")),
    )(rel_rows, bucket_f)


def bias_grad(ds_sum, bucket_f, name):
    b, h, g, blk, blk2 = ds_sum.shape

    def body(ds_ref, bk_ref, o_ref):
        tot = jnp.sum(ds_ref[...], axis=0)
        bk = bk_ref[...]
        lane = lax.broadcasted_iota(jnp.int32, (1, 128), 1)
        vec = jnp.zeros((1, 128), F32)
        for bucket in range(REL_BUCKETS):
            val = jnp.sum(jnp.where(bk == float(bucket), tot, 0.0), keepdims=True)
            vec = vec + jnp.where(lane == bucket, val, 0.0)
        o_ref[...] = vec

    return pl.pallas_call(
        body, name=name, grid=(g, h),
        in_specs=[pl.BlockSpec((b, None, None, blk, blk2), lambda gi, hi: (0, hi, gi, 0, 0)),
                  pl.BlockSpec((None, blk, blk2), lambda gi, hi: (gi, 0, 0))],
        out_specs=pl.BlockSpec((None, 1, 128), lambda gi, hi: (gi * h + hi, 0, 0)),
        out_shape=jax.ShapeDtypeStruct((g * h, 1, 128), F32),
        compiler_params=_params(("parallel", "parallel")),
    )(ds_sum, bucket_f)


def _chip_peers():
    x, y, c = lax.axis_index("x"), lax.axis_index("y"), lax.axis_index("c")
    me = 2 * x + y
    peers = [(1 - x, y, c), (x, 1 - y, c), (1 - x, 1 - y, c)]
    peer_chip = [2 * (1 - x) + y, 2 * x + (1 - y), 2 * (1 - x) + (1 - y)]
    return me, peers, peer_chip


def _any_specs(n):
    return [pl.BlockSpec(memory_space=pl.ANY)] * n


_MID_NUM, _MID_DEN = 3, 4


class _Exchange:
    def start(self, ins, outs, sems):
        local, sends, _ = self._copies(ins, outs, sems)
        for cp in local + sends:
            cp.start()

    def mid(self, ins, outs, sems):
        pass

    def wait(self, ins, outs, sems):
        local, sends, recvs = self._copies(ins, outs, sems)
        for cp in recvs():
            cp.wait_recv()
        for cp in sends:
            cp.wait_send()
        for cp in local:
            cp.wait()


class _Gather(_Exchange):
    HALF_ROWS = 16

    def __init__(self, arrays):
        n = len(arrays)
        self.ins = list(arrays)
        self.split = [a.shape[0] % (2 * self.HALF_ROWS) == 0 for a in arrays]
        self.out_shape = [jax.ShapeDtypeStruct((N_CHIPS,) + a.shape, a.dtype) for a in arrays]
        dma = pltpu.SemaphoreType.DMA
        self.sems = [dma((3 * n,)), dma((3 * n,)), dma((n,)), dma((3 * n,)), dma((3 * n,))]

    def _half(self, i, ref, sibling=False):
        if not self.split[i]:
            return ref
        half = self.ins[i].shape[0] // 2
        c = lax.axis_index("c")
        c = 1 - c if sibling else c
        return ref.at[pl.ds(pl.multiple_of(c * half, self.HALF_ROWS), half)]

    def _plan(self, ins, outs, sems):
        send1, recv1, local_sems, send2, recv2 = sems
        me, peers, peer_chip = _chip_peers()
        x, y, c = lax.axis_index("x"), lax.axis_index("y"), lax.axis_index("c")
        n = len(ins)
        pairs = [(i, k) for i in range(n) for k in range(3)]

        def fetch(i, k, slot):
            return pltpu.make_async_remote_copy(src_ref=self._half(i, ins[i]), dst_ref=self._half(i, outs[i].at[slot]),
                                                send_sem=send1.at[3 * i + k], recv_sem=recv1.at[3 * i + k],
                                                device_id=peers[k], device_id_type=MESH)

        def share(i, k, sibling):
            part = self._half(i, outs[i].at[peer_chip[k]], sibling)
            return pltpu.make_async_remote_copy(src_ref=part, dst_ref=part, send_sem=send2.at[3 * i + k],
                                                recv_sem=recv2.at[3 * i + k], device_id=(x, y, 1 - c),
                                                device_id_type=MESH)

        split_pairs = [(i, k) for i, k in pairs if self.split[i]]
        return dict(
            local=lambda: [pltpu.make_async_copy(ins[i], outs[i].at[me], local_sems.at[i]) for i in range(n)],
            fetch_out=lambda: [fetch(i, k, me) for i, k in pairs],
            fetch_in=lambda: [fetch(i, k, peer_chip[k]) for i, k in pairs],
            share_out=lambda: [share(i, k, False) for i, k in split_pairs],
            share_in=lambda: [share(i, k, True) for i, k in split_pairs])

    def start(self, ins, outs, sems):
        plan = self._plan(ins, outs, sems)
        for cp in plan["local"]() + plan["fetch_out"]():
            cp.start()

    def mid(self, ins, outs, sems):
        plan = self._plan(ins, outs, sems)
        for cp in plan["fetch_in"]():
            cp.wait_recv()
        for cp in plan["share_out"]():
            cp.start()

    def wait(self, ins, outs, sems):
        plan = self._plan(ins, outs, sems)
        for cp in plan["share_in"]():
            cp.wait_recv()
        for cp in plan["fetch_out"]() + plan["share_out"]():
            cp.wait_send()
        for cp in plan["local"]():
            cp.wait()


class _Scatter(_Exchange):
    def __init__(self, slabs, whole=()):
        self.n_slabs = len(slabs)
        self.ins = list(slabs) + list(whole)
        n = len(self.ins)
        self.out_shape = [jax.ShapeDtypeStruct(a.shape, a.dtype) for a in slabs] \
            + [jax.ShapeDtypeStruct((N_CHIPS,) + a.shape, a.dtype) for a in whole]
        self.sems = [pltpu.SemaphoreType.DMA((3 * n,)), pltpu.SemaphoreType.DMA((3 * n,)), pltpu.SemaphoreType.DMA((n,))]

    def _copies(self, ins, outs, sems):
        send_sems, recv_sems, local_sems = sems
        me, peers, peer_chip = _chip_peers()
        n = len(ins)

        def src(i, chip):
            return ins[i].at[chip] if i < self.n_slabs else ins[i]

        def remote(i, k, src_chip, slot):
            return pltpu.make_async_remote_copy(src_ref=src(i, src_chip), dst_ref=outs[i].at[slot],
                                                send_sem=send_sems.at[3 * i + k], recv_sem=recv_sems.at[3 * i + k],
                                                device_id=peers[k], device_id_type=MESH)

        local = [pltpu.make_async_copy(src(i, me), outs[i].at[me], local_sems.at[i]) for i in range(n)]
        sends = [remote(i, k, peer_chip[k], me) for i in range(n) for k in range(3)]
        return local, sends, lambda: [remote(i, k, me, peer_chip[k]) for i in range(n) for k in range(3)]


class _Swap(_Exchange):
    def __init__(self, arrays):
        n = len(arrays)
        self.ins = list(arrays)
        self.out_shape = [jax.ShapeDtypeStruct(a.shape, a.dtype) for a in arrays]
        self.sems = [pltpu.SemaphoreType.DMA((n,)), pltpu.SemaphoreType.DMA((n,))]

    def _copies(self, ins, outs, sems):
        send_sems, recv_sems = sems
        x, y, c = lax.axis_index("x"), lax.axis_index("y"), lax.axis_index("c")
        cps = [pltpu.make_async_remote_copy(src_ref=ins[i], dst_ref=outs[i], send_sem=send_sems.at[i],
                                            recv_sem=recv_sems.at[i], device_id=(x, y, 1 - c), device_id_type=MESH)
               for i in range(len(ins))]
        return [], cps, lambda: cps


def _riding(body, n_in, n_out, n_scratch, ride, rank):
    if not ride:
        return body
    r_in = sum(len(e.ins) for e in ride)
    r_out = sum(len(e.out_shape) for e in ride)

    def split(refs, sizes):
        out, a = [], 0
        for sz in sizes:
            out.append(refs[a:a + sz])
            a += sz
        return out

    def wrapped(*refs):
        a = 0
        parts = []
        for sz in (n_in, r_in, n_out, r_out, n_scratch):
            parts.append(refs[a:a + sz])
            a += sz
        own_in, ex_in, own_out, ex_out, own_scratch = parts
        ex_sems = refs[a:]
        ins = split(ex_in, [len(e.ins) for e in ride])
        outs = split(ex_out, [len(e.out_shape) for e in ride])
        sems = split(ex_sems, [len(e.sems) for e in ride])
        if rank:
            step, total = 0, 1
            for d in range(rank):
                step = step * pl.num_programs(d) + pl.program_id(d)
                total = total * pl.num_programs(d)

            @pl.when(step == 0)
            def _():
                for e, i, o, s in zip(ride, ins, outs, sems):
                    e.start(i, o, s)

            body(*own_in, *own_out, *own_scratch)

            @pl.when(step == (total * _MID_NUM) // _MID_DEN)
            def _():
                for e, i, o, s in zip(ride, ins, outs, sems):
                    e.mid(i, o, s)

            @pl.when(step == total - 1)
            def _():
                for e, i, o, s in zip(ride, ins, outs, sems):
                    e.wait(i, o, s)
        else:
            for phase in ("start", "mid", "wait"):
                for e, i, o, s in zip(ride, ins, outs, sems):
                    getattr(e, phase)(i, o, s)

    return wrapped


def _ride_args(ride):
    ins = [a for e in ride for a in e.ins]
    outs = [s for e in ride for s in e.out_shape]
    sems = [s for e in ride for s in e.sems]
    return ins, _any_specs(len(ins)), outs, _any_specs(len(outs)), sems


def _ride_results(ride, flat):
    out, a = [], 0
    for e in ride:
        out.append(list(flat[a:a + len(e.out_shape)]))
        a += len(e.out_shape)
    return out


def exchange(ride, name):
    ins, in_specs, outs, out_specs, sems = _ride_args(ride)
    res = pl.pallas_call(
        _riding(lambda: None, 0, 0, 0, ride, 0), name=name,
        in_specs=in_specs, out_specs=out_specs, out_shape=outs, scratch_shapes=sems,
    )(*ins)
    return _ride_results(ride, res)


def _sum_slots(ref):
    acc = ref[0].astype(F32)
    for j in range(1, ref.shape[0]):
        acc = acc + ref[j].astype(F32)
    return acc


def sum_pairs(mine, other, name, tr=176):
    n, r, w = mine.shape
    tr = _tile(r, tr)

    def body(a_ref, b_ref, o_ref):
        o_ref[...] = _sum_slots(a_ref) + _sum_slots(b_ref)

    spec = pl.BlockSpec((n, tr, w), lambda i: (0, i, 0))
    return pl.pallas_call(
        body, name=name, grid=(r // tr,),
        in_specs=[spec, spec], out_specs=_rows(tr, w),
        out_shape=jax.ShapeDtypeStruct((r, w), F32),
        compiler_params=_params(("parallel",)),
    )(mine, other)


def adamw(w, m, v, gs, name, tr=256):
    r, c = w.shape
    tr = r if r % 8 else _tile(r, tr)
    c1 = 1.0 - ADAM_B1 ** ADAM_STEP
    c2 = 1.0 - ADAM_B2 ** ADAM_STEP
    ng = len(gs)

    def body(w_ref, m_ref, v_ref, *refs):
        g_refs, (g_ref, d_ref, nm_ref, nv_ref) = refs[:ng], refs[ng:]
        g = g_refs[0][...] if ng == 1 else _sum_slots(g_refs[0]) + _sum_slots(g_refs[1])
        nm = ADAM_B1 * m_ref[...] + (1.0 - ADAM_B1) * g
        nv = ADAM_B2 * v_ref[...] + (1.0 - ADAM_B2) * (g * g)
        g_ref[...] = g
        nm_ref[...] = nm
        nv_ref[...] = nv
        d_ref[...] = (-ADAM_LR) * ((nm / c1) / (jnp.sqrt(nv / c2) + ADAM_EPS) + ADAM_WD * w_ref[...])

    spec = _rows(tr, c)
    gspec = spec if ng == 1 else pl.BlockSpec((N_CHIPS, tr, c), lambda i: (0, i, 0))
    return pl.pallas_call(
        body, name=name, grid=(r // tr,),
        in_specs=[spec] * 3 + [gspec] * ng, out_specs=[spec] * 4,
        out_shape=[jax.ShapeDtypeStruct((r, c), F32)] * 4,
        compiler_params=_params(("parallel",)),
    )(w, m, v, *gs)


_PARAMS = (
    ("rel_bias", None), ("norm_mix_pre", None), ("norm_mix_post", None), ("w_in", 1), ("conv_rnn_w", 1),
    ("conv_rnn_b", None), ("w_rg_a", None), ("b_rg_a", None), ("w_rg_x", None), ("b_rg_x", None),
    ("lru_lambda", None), ("w_branch_rnn", 0), ("w_branch_att", 1), ("w_out", 0), ("norm_ffn_pre", None),
    ("norm_ffn_post", None), ("w_ffn_gate", 1), ("w_ffn_up", 1), ("conv_ffn_w", 1), ("conv_ffn_b", None),
    ("w_ffn_down", 0),
)
_SMALL = 65536


def _as2d(a):
    a = a[0] if a.shape[0] == 1 and a.ndim >= 3 else a
    return a.reshape(-1, a.shape[-1]) if a.ndim == 3 else a


def _pack(pieces, dtype):
    flat = jnp.concatenate([p.astype(dtype).reshape(-1) for p in pieces])
    unit = PACK_W * PACK_ROWS
    pad = (-flat.shape[0]) % unit
    flat = jnp.pad(flat, (0, pad))
    return flat.reshape(-1, PACK_W)


def _unpack(buf, shapes):
    flat = buf.reshape(-1)
    out, off = [], 0
    for shp in shapes:
        n = int(np.prod(shp))
        out.append(flat[off:off + n].reshape(shp))
        off += n
    return out


def _join(slots, ax):
    if ax == 0:
        return slots.reshape(-1, slots.shape[-1])
    return jnp.transpose(slots, (1, 0, 2)).reshape(slots.shape[1], -1)


def _cut(full, ax):
    if ax == 0:
        return full.reshape(N_CHIPS, -1, full.shape[-1])
    return jnp.transpose(full.reshape(full.shape[0], N_CHIPS, -1), (1, 0, 2))


def kernel(x, rel_bias, norm_mix_pre, norm_mix_post, w_in, conv_rnn_w, conv_rnn_b, w_rg_a, b_rg_a, w_rg_x, b_rg_x, lru_lambda, w_branch_rnn, w_branch_att, w_out, norm_ffn_pre, norm_ffn_post, w_ffn_gate, w_ffn_up, conv_ffn_w, conv_ffn_b, w_ffn_down, loss_target, m_rel_bias, m_norm_mix_pre, m_norm_mix_post, m_w_in, m_conv_rnn_w, m_conv_rnn_b, m_w_rg_a, m_b_rg_a, m_w_rg_x, m_b_rg_x, m_lru_lambda, m_w_branch_rnn, m_w_branch_att, m_w_out, m_norm_ffn_pre, m_norm_ffn_post, m_w_ffn_gate, m_w_ffn_up, m_conv_ffn_w, m_conv_ffn_b, m_w_ffn_down, v_rel_bias, v_norm_mix_pre, v_norm_mix_post, v_w_in, v_conv_rnn_w, v_conv_rnn_b, v_w_rg_a, v_b_rg_a, v_w_rg_x, v_b_rg_x, v_lru_lambda, v_w_branch_rnn, v_w_branch_att, v_w_out, v_norm_ffn_pre, v_norm_ffn_post, v_w_ffn_gate, v_w_ffn_up, v_conv_ffn_w, v_conv_ffn_b, v_w_ffn_down):
    args = dict(locals())
    names = [n for n, _ in _PARAMS]
    axis = dict(_PARAMS)
    w_loc = {n: args[n] for n in names}
    m_loc = {n: args["m_" + n] for n in names}
    v_loc = {n: args["v_" + n] for n in names}
    sharded = [n for n in names if axis[n] is not None]
    replicated = [n for n in names if axis[n] is None]

    big = [n for n in sharded if w_loc[n].size >= _SMALL]
    small_sharded = [n for n in sharded if n not in big]
    small = replicated + small_sharded

    first = ["w_in"] + small_sharded
    srcs = [_as2d(w_loc[n]).astype(BF16) if n in big else _as2d(w_loc[n]) for n in first]
    (gathered,) = exchange([_Gather(srcs)], "gather_first")
    p = {n: _join(a, axis[n]) for n, a in zip(first, gathered)}
    for n in replicated:
        p[n] = _as2d(w_loc[n])
    shards = {n: _as2d(w_loc[n]).astype(BF16) for n in big if n not in first}

    received, sibling, g_small, loss_part = _local_step(x, loss_target, p, shards)

    pack = _pack([g_small[n] for n in small], BF16)
    ((received["small"],),) = exchange([_Scatter([], [pack])], "scatter_small")
    late = [n for n in received if n not in sibling]
    (swapped,) = exchange([_Swap([received[n] for n in late])], "swap_last")
    sibling.update(zip(late, swapped))
    small_sum = sum_pairs(received["small"], sibling["small"], "sum_small")
    g_tot = dict(zip(small, _unpack(small_sum, [g_small[n].shape for n in small])))
    chip = 2 * lax.axis_index("x") + lax.axis_index("y")
    for n in small_sharded:
        size = g_tot[n].shape[axis[n]] // N_CHIPS
        g_tot[n] = lax.dynamic_slice_in_dim(g_tot[n], chip * size, size, axis=axis[n])

    out_g, out_d, out_m, out_v = {}, {}, {}, {}
    for i, n in enumerate(names):
        shp = w_loc[n].shape
        gs = (received[n], sibling[n]) if n in big else (g_tot[n],)
        g, d, nm, nv = adamw(_as2d(w_loc[n]), _as2d(m_loc[n]), _as2d(v_loc[n]), gs, "adamw_" + n)
        out_g[n], out_d[n], out_m[n], out_v[n] = (t.reshape(shp) for t in (g, d, nm, nv))

    d_model = x.shape[-1]
    loss = lax.psum(0.5 * jnp.sum(loss_part) / d_model, ("x", "y", "c"))
    grad_x = g_small["x"]
    return (loss, grad_x, *[out_g[n] for n in names], *[out_d[n] for n in names],
            *[out_m[n] for n in names], *[out_v[n] for n in names])


def _local_step(x, target, p, shards):
    axis = dict(_PARAMS)
    b, s, d = x.shape
    t = b * s
    rnn = p["b_rg_a"].shape[1]
    ffn = p["conv_ffn_b"].shape[1]
    nbk = rnn // p["w_rg_a"].shape[1]
    hkv = (p["w_in"].shape[1] - rnn - 2 * d) // (N_GROUPS + 2)
    h = hkv // HEAD_DIM
    nq = N_GROUPS * hkv

    x2 = x.reshape(t, d)
    tgt = target.reshape(t, d)
    w_in = p["w_in"]
    in_splits = (rnn, nq + 2 * hkv, 2 * d)
    wa = p["w_rg_a"].reshape(nbk, -1, p["w_rg_a"].shape[1]).astype(BF16)
    wx = p["w_rg_x"].reshape(nbk, -1, p["w_rg_x"].shape[1]).astype(BF16)
    cw_r, cb_r = p["conv_rnn_w"], p["conv_rnn_b"]
    cw_f, cb_f = p["conv_ffn_w"], p["conv_ffn_b"]

    masks, buckets = zip(*[_band(w_, r_) for w_, r_ in DILATED])
    bucket_f = jnp.asarray(np.where(np.stack(masks), np.stack(buckets), -1).astype(np.float32))
    rel_rows = jnp.pad(p["rel_bias"].T, ((0, 0), (0, 128 - REL_BUCKETS)))[:, None, :]
    biasm = bias_table(rel_rows, bucket_f, h, "bias_table")

    early = ["w_branch_rnn", "w_branch_att", "w_out"]
    hn1, (xr, qkv, gts), (got,) = norm_mm(x2, p["norm_mix_pre"], [w_in], [in_splits], "in_proj",
                                          ride=[_Gather([shards[n] for n in early])])
    p.update({n: _join(a, axis[n]) for n, a in zip(early, got)})
    xr3 = xr.reshape(b, s, rnn)
    (y_rnn, a_rnn, xc_rnn), (got,) = rglru_fwd(xr3, cw_r, cb_r, wa, p["b_rg_a"], wx, p["b_rg_x"], p["lru_lambda"], "rglru_fwd",
                              ride=[_Gather([shards[n] for n in ("w_ffn_gate", "w_ffn_up")])])
    p.update({n: _join(a, axis[n]) for n, a in zip(("w_ffn_gate", "w_ffn_up"), got)})
    qkv3 = qkv.reshape(b, s, -1)
    o_att, lse, ((got,),) = attn_fwd(qkv3, biasm, h, "attn_fwd", ride=[_Gather([shards["w_ffn_down"]])])
    p["w_ffn_down"] = _join(got, axis["w_ffn_down"])
    merged, br, ba, mix, h1 = merge_out(y_rnn.reshape(t, rnn), o_att.reshape(t, hkv), gts, p["w_branch_rnn"],
                                        p["w_branch_att"], p["w_out"], p["norm_mix_post"], x2, "merge_out")
    hn2, gate_pre, up, act = ffn_in_act(h1, p["norm_ffn_pre"], p["w_ffn_gate"], p["w_ffn_up"], cw_f, cb_f, s, "ffn_in")

    g, gb = {}, {}
    recv, sib = {}, {}

    def rows4(a):
        return a.reshape(N_CHIPS, -1, a.shape[-1])

    dy, dff, dact, g["norm_ffn_post"], loss_part = ffn_down_loss(act, p["w_ffn_down"], p["norm_ffn_post"], h1, tgt,
                                                                  "ffn_down")
    gb["w_ffn_down"] = rows4(mm_tn(act, [dff], "ffn_down_dw"))
    dgp, dup, g["conv_ffn_w"], g["conv_ffn_b"] = ffn_bwd(dact.reshape(b, s, ffn), gate_pre.reshape(b, s, ffn),
                                                        up.reshape(b, s, ffn), cw_f, cb_f, "ffn_bwd")
    dgp, dup = dgp.reshape(t, ffn), dup.reshape(t, ffn)
    dhn2, ((recv["w_ffn_down"],),) = mm_nt([([dgp], p["w_ffn_gate"]), ([dup], p["w_ffn_up"])], "ffn_in_dx",
                                           ride=[_Scatter([gb["w_ffn_down"]])])
    gb["w_ffn_gate"] = mm_tn(hn2, [dgp], "ffn_gate_dw", col_shards=N_CHIPS)
    gb["w_ffn_up"] = mm_tn(hn2, [dup], "ffn_up_dw", col_shards=N_CHIPS)
    dh1, dmix, dbr, dba, dgts, dy_rnn, do_att, g["norm_ffn_pre"], g["norm_mix_post"] = mid_bwd(
        dhn2, h1, p["norm_ffn_pre"], dy, mix, p["norm_mix_post"], p["w_out"], gts, br, ba,
        p["w_branch_rnn"], p["w_branch_att"], "mid_bwd")
    gb["w_out"] = rows4(mm_tn(merged, [dmix], "out_proj_dw"))
    gb["w_branch_rnn"] = rows4(mm_tn(y_rnn.reshape(t, rnn), [dbr], "branch_rnn_dw"))
    gb["w_branch_att"] = mm_tn(o_att.reshape(t, hkv), [dba], "branch_att_dw", col_shards=N_CHIPS)
    ffn_in = ["w_ffn_gate", "w_ffn_up"]
    (dxr, g["conv_rnn_w"], g["conv_rnn_b"], dwa, g["b_rg_a"], dwx, g["b_rg_x"], g["lru_lambda"]), (got,) = rglru_bwd(
        xr3, y_rnn, dy_rnn.reshape(b, s, rnn), a_rnn, xc_rnn, cw_r, cb_r, wa, p["b_rg_a"], wx, p["b_rg_x"], p["lru_lambda"], "rglru_bwd",
        ride=[_Scatter([gb[n] for n in ffn_in])])
    recv.update(zip(ffn_in, got))
    g["w_rg_a"] = dwa.reshape(p["w_rg_a"].shape)
    g["w_rg_x"] = dwx.reshape(p["w_rg_x"].shape)
    mid = ["w_out", "w_branch_rnn", "w_branch_att"]
    early_recv = ["w_ffn_down"] + ffn_in
    (dq1, dq2, dq3, dk, dv, ds_sum), (got, swapped) = attn_bwd(
        qkv3, biasm, o_att, lse, do_att.reshape(b, s, hkv), h, "attn_bwd",
        ride=[_Scatter([gb[n] for n in mid]), _Swap([recv[n] for n in early_recv])])
    recv.update(zip(mid, got))
    sib.update(zip(early_recv, swapped))
    rows = bias_grad(ds_sum, bucket_f, "bias_grad")
    g["rel_bias"] = rows[:, 0, :REL_BUCKETS].T
    dproj = [dxr.reshape(t, rnn)] + [a.reshape(t, hkv) for a in (dq1, dq2, dq3, dk, dv)] + [dgts]
    dw_a = mm_tn(hn1, dproj[:4], "in_proj_dw_a")[0]
    dw_b = mm_tn(hn1, dproj[4:], "in_proj_dw_b")[0]
    gb["w_in"] = _cut(jnp.concatenate([dw_a, dw_b], axis=1), 1)
    dx, g["norm_mix_pre"], ((recv["w_in"],), got) = mm_nt(
        [(dproj, w_in)], "in_proj_dx", norm=(x2, p["norm_mix_pre"], dh1),
        ride=[_Scatter([gb["w_in"]]), _Swap([recv[n] for n in mid])])
    sib.update(zip(mid, got))
    g["x"] = dx.reshape(b, s, d)
    return recv, sib, g, loss_part
```

```python
import functools
import math

import numpy as np
import jax
import jax.numpy as jnp
from jax import lax
from jax.experimental import pallas as pl
from jax.experimental.pallas import tpu as pltpu

F32 = jnp.float32
BF16 = jnp.bfloat16

EPS = 1e-6
HEAD_DIM = 128
ATTN_BLOCK = 128
DILATED = ((128, 1), (512, 4), (2048, 16))
N_GROUPS = len(DILATED)
REL_BUCKETS = 32
REL_MAX_DIST = 2048
LRU_C = 8.0
NEG = -1e30

ADAM_LR = 0.001
ADAM_B1 = 0.9
ADAM_B2 = 0.999
ADAM_EPS = 1e-08
ADAM_WD = 0.01
ADAM_STEP = 10

N_CHIPS = 4
PACK_W = 1024
PACK_ROWS = 16
VMEM_LIMIT = 56 * 1024 * 1024
MESH = pl.DeviceIdType.MESH


def _params(sem=None):
    return pltpu.CompilerParams(dimension_semantics=sem, vmem_limit_bytes=VMEM_LIMIT)


def _dot(a, b):
    return jnp.dot(a, b, preferred_element_type=F32)


def _dot_nt(a, b):
    return lax.dot_general(a, b, (((1,), (1,)), ((), ())), preferred_element_type=F32)


def _dot_tn(a, b):
    return lax.dot_general(a, b, (((0,), (0,)), ((), ())), preferred_element_type=F32)


def _sig(x):
    return 0.5 * jnp.tanh(0.5 * x) + 0.5


def _rows(tm, w):
    return pl.BlockSpec((tm, w), lambda i: (i, 0))


def _whole(shape):
    nd = len(shape)
    return pl.BlockSpec(tuple(shape), lambda *_: (0,) * nd)


def _tile(t, want):
    while t % want:
        want //= 2
    return want


def norm_mm(x, g, ws, splits, name, ride=(), tm=256):
    t, d = x.shape
    tm = _tile(t, tm)
    nw = len(ws)
    widths = [n for sp in splits for n in sp]

    def body(x_ref, g_ref, *refs):
        w_refs, hn_ref, o_refs = refs[:nw], refs[nw], refs[nw + 1:]
        xv = x_ref[...]
        inv = lax.rsqrt(jnp.mean(xv * xv, axis=-1, keepdims=True) + EPS)
        hn = (xv * inv * g_ref[...]).astype(BF16)
        hn_ref[...] = hn
        o = 0
        for w_ref, sp in zip(w_refs, splits):
            off = 0
            for n in sp:
                o_refs[o][...] = _dot(hn, w_ref[:, off:off + n])
                off += n
                o += 1

    r_ins, r_in_specs, r_outs, r_out_specs, r_sems = _ride_args(ride)
    n_out = 1 + len(widths)
    outs = pl.pallas_call(
        _riding(body, 2 + nw, n_out, 0, ride, 1), name=name, grid=(t // tm,),
        in_specs=[_rows(tm, d), _whole(g.shape)] + [_whole(w.shape) for w in ws] + r_in_specs,
        out_specs=[_rows(tm, d)] + [_rows(tm, n) for n in widths] + r_out_specs,
        out_shape=[jax.ShapeDtypeStruct((t, d), BF16)] + [jax.ShapeDtypeStruct((t, n), F32) for n in widths] + r_outs,
        scratch_shapes=r_sems,
        compiler_params=_params(("arbitrary",)),
    )(x, g, *ws, *r_ins)
    return outs[0], outs[1:n_out], _ride_results(ride, outs[n_out:])


def mm_nt(groups, name, ride=(), norm=None, tm=256):
    dys_all = [dy for dys, _ in groups for dy in dys]
    ws = [w for _, w in groups]
    t = dys_all[0].shape[0]
    k = ws[0].shape[0]
    tm = _tile(t, tm)
    n = len(dys_all)
    extra = list(norm) if norm else []

    def body(*refs):
        dy_refs, w_refs = refs[:n], refs[n:n + len(ws)]
        rest = refs[n + len(ws):]
        acc = None
        i = 0
        for (dys, _), w_ref in zip(groups, w_refs):
            off = 0
            for dy in dys:
                width = dy.shape[1]
                part = _dot_nt(dy_refs[i][...].astype(BF16), w_ref[:, off:off + width])
                acc = part if acc is None else acc + part
                off += width
                i += 1
        if norm:
            u_ref, g_ref, add_ref, o_ref, dg_ref = rest

            @pl.when(pl.program_id(0) == 0)
            def _():
                dg_ref[...] = jnp.zeros(dg_ref.shape, F32)

            du, dg_rows = _rms_bwd(acc, u_ref[...], g_ref[...])
            o_ref[...] = du + add_ref[...]
            dg_ref[...] += jnp.sum(dg_rows, axis=0, keepdims=True)
        else:
            rest[0][...] = acc

    n_out = 2 if norm else 1
    r_ins, r_in_specs, r_outs, r_out_specs, r_sems = _ride_args(ride)
    outs = pl.pallas_call(
        _riding(body, n + len(ws) + len(extra), n_out, 0, ride, 1), name=name, grid=(t // tm,),
        in_specs=[_rows(tm, dy.shape[1]) for dy in dys_all] + [_whole(w.shape) for w in ws]
        + ([_rows(tm, k), _whole((1, k)), _rows(tm, k)] if norm else []) + r_in_specs,
        out_specs=[_rows(tm, k)] + ([_whole((1, k))] if norm else []) + r_out_specs,
        out_shape=[jax.ShapeDtypeStruct((t, k), F32)] + ([jax.ShapeDtypeStruct((1, k), F32)] if norm else []) + r_outs,
        scratch_shapes=r_sems,
        compiler_params=_params(("arbitrary",)),
    )(*dys_all, *ws, *extra, *r_ins)
    return tuple(outs[:n_out]) + (_ride_results(ride, outs[n_out:]),)


def mm_tn(a, dys, name, col_shards=1, tm=1024):
    t, k = a.shape
    tm = _tile(t, tm)
    n = len(dys)
    ntot = sum(dy.shape[1] for dy in dys)
    wsh = ntot // col_shards

    def body(a_ref, *refs):
        dy_refs, o_ref, acc = refs[:n], refs[n], refs[n + 1]

        @pl.when(pl.program_id(0) == 0)
        def _():
            acc[...] = jnp.zeros(acc.shape, F32)

        av = a_ref[...].astype(BF16)
        off = 0
        for dy_ref in dy_refs:
            width = dy_ref.shape[1]
            acc[:, off:off + width] += _dot_tn(av, dy_ref[...].astype(BF16))
            off += width

        @pl.when(pl.program_id(0) == pl.num_programs(0) - 1)
        def _():
            for j in range(col_shards):
                o_ref[j] = acc[:, j * wsh:(j + 1) * wsh].astype(o_ref.dtype)

    return pl.pallas_call(
        body, name=name, grid=(t // tm,),
        in_specs=[_rows(tm, k)] + [_rows(tm, dy.shape[1]) for dy in dys],
        out_specs=_whole((col_shards, k, wsh)),
        out_shape=jax.ShapeDtypeStruct((col_shards, k, wsh), BF16),
        scratch_shapes=[pltpu.VMEM((k, ntot), F32)],
        compiler_params=_params(("arbitrary",)),
    )(a, *dys)


def _rms_bwd(dz, u, g):
    d = u.shape[-1]
    inv = lax.rsqrt(jnp.mean(u * u, axis=-1, keepdims=True) + EPS)
    dzg = dz * g
    proj = jnp.sum(dzg * u, axis=-1, keepdims=True) * (1.0 / d)
    du = inv * (dzg - u * (inv * inv) * proj)
    dg_rows = dz * u * inv
    return du, dg_rows


def ffn_down_loss(act, wd, g, h1, target, name, tm=256):
    t, f = act.shape
    d = wd.shape[1]
    tm = _tile(t, tm)

    def body(a_ref, w_ref, g_ref, h_ref, t_ref, dy_ref, dff_ref, dact_ref, dg_ref, loss_ref):
        @pl.when(pl.program_id(0) == 0)
        def _():
            dg_ref[...] = jnp.zeros(dg_ref.shape, F32)
            loss_ref[...] = jnp.zeros(loss_ref.shape, F32)

        wv = w_ref[...]
        gv = g_ref[...]
        ff = _dot(a_ref[...], wv)
        inv = lax.rsqrt(jnp.mean(ff * ff, axis=-1, keepdims=True) + EPS)
        err = h_ref[...] + ff * inv * gv - t_ref[...]
        loss_ref[...] += jnp.sum(err * err, axis=0, keepdims=True)
        dy = err * (1.0 / d)
        dy_ref[...] = dy
        du, dg_rows = _rms_bwd(dy, ff, gv)
        dff = du.astype(BF16)
        dff_ref[...] = dff
        dg_ref[...] += jnp.sum(dg_rows, axis=0, keepdims=True)
        dact_ref[...] = _dot_nt(dff, wv)

    return pl.pallas_call(
        body, name=name, grid=(t // tm,),
        in_specs=[_rows(tm, f), _whole(wd.shape), _whole(g.shape), _rows(tm, d), _rows(tm, d)],
        out_specs=[_rows(tm, d), _rows(tm, d), _rows(tm, f), _whole((1, d)), _whole((1, d))],
        out_shape=[jax.ShapeDtypeStruct((t, d), F32), jax.ShapeDtypeStruct((t, d), BF16),
                   jax.ShapeDtypeStruct((t, f), F32), jax.ShapeDtypeStruct((1, d), F32),
                   jax.ShapeDtypeStruct((1, d), F32)],
        compiler_params=_params(("arbitrary",)),
    )(act, wd, g, h1, target)


def merge_out(y_rnn, o_att, gts, w_br, w_ba, w_out, g, x, name, tm=256):
    t = y_rnn.shape[0]
    d = w_br.shape[1]
    tm = _tile(t, tm)

    def body(y_ref, o_ref, g_ref, wbr_ref, wba_ref, wo_ref, gn_ref, x_ref, m_ref, br_ref, ba_ref, mix_ref, h_ref):
        br = _dot(y_ref[...].astype(BF16), wbr_ref[...])
        ba = _dot(o_ref[...].astype(BF16), wba_ref[...])
        gv = g_ref[...]
        merged = (_sig(gv[:, :d]) * br + _sig(gv[:, d:]) * ba).astype(BF16)
        m_ref[...] = merged
        br_ref[...] = br
        ba_ref[...] = ba
        mix = _dot(merged, wo_ref[...])
        mix_ref[...] = mix
        inv = lax.rsqrt(jnp.mean(mix * mix, axis=-1, keepdims=True) + EPS)
        h_ref[...] = x_ref[...] + mix * inv * gn_ref[...]

    sd = jax.ShapeDtypeStruct
    return pl.pallas_call(
        body, name=name, grid=(t // tm,),
        in_specs=[_rows(tm, y_rnn.shape[1]), _rows(tm, o_att.shape[1]), _rows(tm, 2 * d),
                  _whole(w_br.shape), _whole(w_ba.shape), _whole(w_out.shape), _whole(g.shape), _rows(tm, d)],
        out_specs=[_rows(tm, d)] * 5,
        out_shape=[sd((t, d), BF16), sd((t, d), F32), sd((t, d), F32), sd((t, d), F32), sd((t, d), F32)],
        compiler_params=_params(("parallel",)),
    )(y_rnn, o_att, gts, w_br, w_ba, w_out, g, x)


def mid_bwd(dhn2, h1, g_ffn, dy, mix, g_mix, w_out, gts, br, ba, w_br, w_ba, name, tm=256):
    t, d = h1.shape
    tm = _tile(t, tm)
    rnn, hkv = w_br.shape[0], w_ba.shape[0]

    def body(dhn_ref, h_ref, gf_ref, dy_ref, mix_ref, gm_ref, wo_ref, g_ref, br_ref, ba_ref, wbr_ref, wba_ref,
             dh_ref, dmix_ref, dbr_ref, dba_ref, dg_ref, dyr_ref, doa_ref, dgf_ref, dgm_ref):
        @pl.when(pl.program_id(0) == 0)
        def _():
            dgf_ref[...] = jnp.zeros(dgf_ref.shape, F32)
            dgm_ref[...] = jnp.zeros(dgm_ref.shape, F32)

        du, rows_f = _rms_bwd(dhn_ref[...], h_ref[...], gf_ref[...])
        dh1 = du + dy_ref[...]
        dh_ref[...] = dh1
        dgf_ref[...] += jnp.sum(rows_f, axis=0, keepdims=True)
        dmx, rows_m = _rms_bwd(dh1, mix_ref[...], gm_ref[...])
        dmix = dmx.astype(BF16)
        dmix_ref[...] = dmix
        dgm_ref[...] += jnp.sum(rows_m, axis=0, keepdims=True)
        dm = _dot_nt(dmix, wo_ref[...])
        gv = g_ref[...]
        sr = _sig(gv[:, :d])
        sa = _sig(gv[:, d:])
        dbr = (dm * sr).astype(BF16)
        dba = (dm * sa).astype(BF16)
        dbr_ref[...] = dbr
        dba_ref[...] = dba
        dg_ref[:, :d] = (dm * br_ref[...] * sr * (1.0 - sr)).astype(BF16)
        dg_ref[:, d:] = (dm * ba_ref[...] * sa * (1.0 - sa)).astype(BF16)
        dyr_ref[...] = _dot_nt(dbr, wbr_ref[...])
        doa_ref[...] = _dot_nt(dba, wba_ref[...])

    sd = jax.ShapeDtypeStruct
    row, vec = _rows(tm, d), _whole((1, d))
    return pl.pallas_call(
        body, name=name, grid=(t // tm,),
        in_specs=[row, row, vec, row, row, vec, _whole(w_out.shape), _rows(tm, 2 * d), row, row,
                  _whole(w_br.shape), _whole(w_ba.shape)],
        out_specs=[row, row, row, row, _rows(tm, 2 * d), _rows(tm, rnn), _rows(tm, hkv), vec, vec],
        out_shape=[sd((t, d), F32), sd((t, d), BF16), sd((t, d), BF16), sd((t, d), BF16), sd((t, 2 * d), BF16),
                   sd((t, rnn), F32), sd((t, hkv), F32), sd((1, d), F32), sd((1, d), F32)],
        compiler_params=_params(("arbitrary",)),
    )(dhn2, h1, g_ffn, dy, mix, g_mix, w_out, gts, br, ba, w_br, w_ba)


def _shift_dn(x, d, fill, row):
    return jnp.where(row >= d, pltpu.roll(x, d, 0), fill)


def _shift_up(x, d, fill, row):
    s = x.shape[0]
    return jnp.where(row < s - d, pltpu.roll(x, s - d, 0), fill)


def _conv_fwd(x, w, b, row):
    kk = w.shape[0]
    y = b + w[kk - 1:kk, :] * x
    for j in range(1, kk):
        y = y + w[kk - 1 - j:kk - j, :] * _shift_dn(x, j, 0.0, row)
    return y


def _conv_bwd(dy, x, w, row):
    kk = w.shape[0]
    dx = w[kk - 1:kk, :] * dy
    dws = [None] * kk
    dws[kk - 1] = jnp.sum(dy * x, axis=0, keepdims=True)
    for j in range(1, kk):
        dx = dx + w[kk - 1 - j:kk - j, :] * _shift_up(dy, j, 0.0, row)
        dws[kk - 1 - j] = jnp.sum(dy * _shift_dn(x, j, 0.0, row), axis=0, keepdims=True)
    return dx, jnp.concatenate(dws, axis=0)


def _softplus(z):
    y = jnp.exp(-jnp.abs(z))
    u = 1.0 + y
    dd = u - 1.0
    log1p = jnp.where(dd == 0.0, y, jnp.log(u) * (y / jnp.where(dd == 0.0, 1.0, dd)))
    return jnp.maximum(z, 0.0) + log1p


def _lru_decay(xb, wa, ba, lam):
    r = _sig(_dot(xb, wa) + ba)
    sp = _softplus(-lam)
    la = (-LRU_C) * r * sp
    return r, sp, la, jnp.exp(la)


def _lru_gates(xc, wa, ba, wx, bx, lam):
    xb = xc.astype(BF16)
    r, sp, la, a = _lru_decay(xb, wa, ba, lam)
    i = _sig(_dot(xb, wx) + bx)
    one_m_a2 = jnp.tanh(-la) * (1.0 + a * a)
    inv_mult = lax.rsqrt(one_m_a2)
    return r, i, sp, a, one_m_a2 * inv_mult, inv_mult


def _seg_len(s):
    seg = -(-s // 8)
    return seg + (4 - seg % 8) % 8


def _scan_rows(a_pad, u_pad, out_pad, reverse):
    planes, rows8, lanes = a_pad.shape
    seg = rows8 // 8
    sub = lax.broadcasted_iota(jnp.int32, (planes, 8, lanes), 1)

    unroll = 4

    def rows(k, d):
        i = k * unroll + d
        return pl.ds((seg - 1 - i) if reverse else i, 8, stride=seg)

    def ends(k, carry):
        h, p = carry
        for d in range(unroll):
            a = a_pad[:, rows(k, d), :]
            h = a * h + u_pad[:, rows(k, d), :]
            p = a * p
        return h, p

    init = (jnp.zeros((planes, 8, lanes), F32), jnp.ones((planes, 8, lanes), F32))
    h_end, p_end = lax.fori_loop(0, seg // unroll, ends, init)
    start = jnp.zeros((planes, 8, lanes), F32)
    for _ in range(7):
        nxt = h_end + p_end * start
        if reverse:
            start = jnp.where(sub < 7, pltpu.roll(nxt, 7, 1), 0.0)
        else:
            start = jnp.where(sub >= 1, pltpu.roll(nxt, 1, 1), 0.0)

    def redo(k, h):
        for d in range(unroll):
            h = a_pad[:, rows(k, d), :] * h + u_pad[:, rows(k, d), :]
            out_pad[:, rows(k, d), :] = h
        return h

    lax.fori_loop(0, seg // unroll, redo, start)


def _lru_cols(c, rb):
    return 2 * rb if c % (2 * rb) == 0 else rb


def rglru_fwd(xr, cw, cb, wa, ba, wx, bx, lam, name, ride=()):
    b, s, c = xr.shape
    rb = wa.shape[1]
    kk = cw.shape[0]
    cols = _lru_cols(c, rb)
    nj = cols // rb
    seg = _seg_len(s)

    def body(x_ref, cw_ref, cb_ref, wa_ref, ba_ref, wx_ref, bx_ref, lam_ref, h_ref, a_ref, xc_ref, a_pad, u_pad, h_pad):
        row = lax.broadcasted_iota(jnp.int32, (s, rb), 0)
        for j in range(nj):
            cs = slice(j * rb, (j + 1) * rb)
            xc = _conv_fwd(x_ref[:, cs], cw_ref[:, cs], cb_ref[:, cs], row)
            _, i, _, a, mult, _ = _lru_gates(xc, wa_ref[j], ba_ref[:, cs], wx_ref[j], bx_ref[:, cs], lam_ref[:, cs])
            xc_ref[:, cs] = xc
            a_ref[:, cs] = a
            a_pad[j, 0:s, :] = a
            u_pad[j, 0:s, :] = mult * (i * xc)
        a_pad[:, s:, :] = jnp.ones((nj, 8 * seg - s, rb), F32)
        u_pad[:, s:, :] = jnp.zeros((nj, 8 * seg - s, rb), F32)
        _scan_rows(a_pad, u_pad, h_pad, False)
        for j in range(nj):
            h_ref[:, j * rb:(j + 1) * rb] = h_pad[j, 0:s, :]

    vec = pl.BlockSpec((1, cols), lambda bi, n: (0, n))
    seq = pl.BlockSpec((None, s, cols), lambda bi, n: (bi, 0, n))
    mat = pl.BlockSpec((nj, rb, rb), lambda bi, n: (n, 0, 0))
    r_ins, r_in_specs, r_outs, r_out_specs, r_sems = _ride_args(ride)
    outs = pl.pallas_call(
        _riding(body, 8, 3, 3, ride, 2), name=name, grid=(b, c // cols),
        in_specs=[seq, pl.BlockSpec((kk, cols), lambda bi, n: (0, n)), vec, mat, vec, mat, vec, vec] + r_in_specs,
        out_specs=[seq] * 3 + r_out_specs,
        out_shape=[jax.ShapeDtypeStruct((b, s, c), F32)] * 3 + r_outs,
        scratch_shapes=[pltpu.VMEM((nj, 8 * seg, rb), F32)] * 3 + r_sems,
        compiler_params=_params(("arbitrary", "arbitrary")),
    )(xr, cw, cb, wa, ba, wx, bx, lam, *r_ins)
    return outs[:3], _ride_results(ride, outs[3:])


def rglru_bwd(xr, h, dh, a_fwd, xc_fwd, cw, cb, wa, ba, wx, bx, lam, name, ride=()):
    b, s, c = xr.shape
    nb, rb = wa.shape[0], wa.shape[1]
    kk = cw.shape[0]
    cols = _lru_cols(c, rb)
    nj = cols // rb
    seg = _seg_len(s)

    def body(x_ref, h_ref, dh_ref, a_ref, xc_ref, cw_ref, cb_ref, wa_ref, ba_ref, wx_ref, bx_ref, lam_ref,
             dx_ref, dcw_ref, dcb_ref, dwa_ref, dba_ref, dwx_ref, dbx_ref, dlam_ref, b_pad, g_pad, l_pad):
        @pl.when(pl.program_id(1) == 0)
        def _():
            for ref in (dcw_ref, dcb_ref, dwa_ref, dba_ref, dwx_ref, dbx_ref, dlam_ref):
                ref[...] = jnp.zeros(ref.shape, F32)

        row = lax.broadcasted_iota(jnp.int32, (s, rb), 0)

        for j in range(nj):
            b_pad[j, 0:s, :] = _shift_up(a_ref[:, j * rb:(j + 1) * rb], 1, 0.0, row)
            g_pad[j, 0:s, :] = dh_ref[:, j * rb:(j + 1) * rb]
        b_pad[:, s:, :] = jnp.zeros((nj, 8 * seg - s, rb), F32)
        g_pad[:, s:, :] = jnp.zeros((nj, 8 * seg - s, rb), F32)
        _scan_rows(b_pad, g_pad, l_pad, True)

        for j in range(nj):
            cs = slice(j * rb, (j + 1) * rb)
            x = x_ref[:, cs]
            cwv = cw_ref[:, cs]
            wav, wxv, lamv = wa_ref[j], wx_ref[j], lam_ref[:, cs]
            xc = xc_ref[:, cs]
            r, i, sp, a, mult, inv_mult = _lru_gates(xc, wav, ba_ref[:, cs], wxv, bx_ref[:, cs], lamv)
            lmb = l_pad[j, 0:s, :]
            h_prev = _shift_dn(h_ref[:, cs], 1, 0.0, row)
            da = lmb * h_prev
            ixc = i * xc
            dla = da * a - (lmb * ixc) * (a * a) * inv_mult
            di = lmb * mult * xc
            dxc = lmb * mult * i
            dr = dla * ((-LRU_C) * sp)
            dsp = jnp.sum(dla * ((-LRU_C) * r), axis=0, keepdims=True)
            dga = dr * r * (1.0 - r)
            dgx = di * i * (1.0 - i)
            dga_b, dgx_b = dga.astype(BF16), dgx.astype(BF16)
            xb = xc.astype(BF16)
            dwa_ref[j] += _dot_tn(xb, dga_b)
            dwx_ref[j] += _dot_tn(xb, dgx_b)
            dba_ref[:, cs] += jnp.sum(dga, axis=0, keepdims=True)
            dbx_ref[:, cs] += jnp.sum(dgx, axis=0, keepdims=True)
            dlam_ref[:, cs] += dsp * (-_sig(-lamv))
            dxc = dxc + _dot_nt(dga_b, wav) + _dot_nt(dgx_b, wxv)
            dcb_ref[:, cs] += jnp.sum(dxc, axis=0, keepdims=True)
            dx, dcw = _conv_bwd(dxc, x, cwv, row)
            dcw_ref[:, cs] += dcw
            dx_ref[:, cs] = dx.astype(dx_ref.dtype)

    vec = pl.BlockSpec((1, cols), lambda n, bi: (0, n))
    seq = pl.BlockSpec((None, s, cols), lambda n, bi: (bi, 0, n))
    mat = pl.BlockSpec((nj, rb, rb), lambda n, bi: (n, 0, 0))
    cws = pl.BlockSpec((kk, cols), lambda n, bi: (0, n))
    sd = jax.ShapeDtypeStruct
    r_ins, r_in_specs, r_outs, r_out_specs, r_sems = _ride_args(ride)
    outs = pl.pallas_call(
        _riding(body, 12, 8, 3, ride, 2), name=name, grid=(c // cols, b),
        in_specs=[seq, seq, seq, seq, seq, cws, vec, mat, vec, mat, vec, vec] + r_in_specs,
        out_specs=[seq, cws, vec, mat, vec, mat, vec, vec] + r_out_specs,
        out_shape=[sd((b, s, c), BF16), sd((kk, c), F32), sd((1, c), F32), sd((nb, rb, rb), F32),
                   sd((1, c), F32), sd((nb, rb, rb), F32), sd((1, c), F32), sd((1, c), F32)] + r_outs,
        scratch_shapes=[pltpu.VMEM((nj, 8 * seg, rb), F32)] * 3 + r_sems,
        compiler_params=_params(("arbitrary", "arbitrary")),
    )(xr, h, dh, a_fwd, xc_fwd, cw, cb, wa, ba, wx, bx, lam, *r_ins)
    return outs[:8], _ride_results(ride, outs[8:])


_GELU_C = math.sqrt(2.0 / math.pi)


def _gelu_parts(x):
    th = jnp.tanh(_GELU_C * (x + 0.044715 * x * x * x))
    gel = 0.5 * x * (1.0 + th)
    dgel = 0.5 * (1.0 + th) + 0.5 * x * (1.0 - th * th) * _GELU_C * (1.0 + 3 * 0.044715 * x * x)
    return gel, dgel


def ffn_in_act(x, g, wg, wu, cw, cb, seq_len, name, tm=256):
    t, d = x.shape
    f = wg.shape[1]
    kk = cw.shape[0]
    tm = _tile(seq_len, tm)
    tiles_per_seq = seq_len // tm
    keep = 8
    assert kk - 1 <= keep

    def body(x_ref, g_ref, wg_ref, wu_ref, cw_ref, cb_ref, hn_ref, gp_ref, up_ref, act_ref, tail):
        @pl.when(pl.program_id(0) % tiles_per_seq == 0)
        def _():
            tail[...] = jnp.zeros(tail.shape, F32)

        xv = x_ref[...]
        inv = lax.rsqrt(jnp.mean(xv * xv, axis=-1, keepdims=True) + EPS)
        hn = (xv * inv * g_ref[...]).astype(BF16)
        hn_ref[...] = hn
        gp = _dot(hn, wg_ref[...])
        up = _dot(hn, wu_ref[...])
        gp_ref[...] = gp
        up_ref[...] = up
        cwv = cw_ref[...]
        row = lax.broadcasted_iota(jnp.int32, (tm, 1), 0)
        gate = _conv_fwd(gp, cwv, cb_ref[...], row)
        row8 = lax.broadcasted_iota(jnp.int32, (keep, 1), 0)
        prev = tail[...]
        fix = jnp.zeros((keep, f), F32)
        for j in range(1, kk):
            fix = fix + cwv[kk - 1 - j:kk - j, :] * jnp.where(row8 < j, pltpu.roll(prev, j, 0), 0.0)
        gate = jnp.concatenate([gate[:keep] + fix, gate[keep:]], axis=0)
        tail[...] = gp[tm - keep:, :]
        gel, _ = _gelu_parts(gate)
        act_ref[...] = (gel * up).astype(BF16)

    sd = jax.ShapeDtypeStruct
    return pl.pallas_call(
        body, name=name, grid=(t // tm,),
        in_specs=[_rows(tm, d), _whole(g.shape), _whole(wg.shape), _whole(wu.shape), _whole(cw.shape), _whole(cb.shape)],
        out_specs=[_rows(tm, d), _rows(tm, f), _rows(tm, f), _rows(tm, f)],
        out_shape=[sd((t, d), BF16), sd((t, f), F32), sd((t, f), F32), sd((t, f), BF16)],
        scratch_shapes=[pltpu.VMEM((keep, f), F32)],
        compiler_params=_params(("arbitrary",)),
    )(x, g, wg, wu, cw, cb)


def ffn_in_bwd(dact, gate_pre, up, cw, cb, wg, wu, seq_len, name, ride=(), tm=256):
    t, f = gate_pre.shape
    d = wg.shape[0]
    kk = cw.shape[0]
    tm = _tile(seq_len, tm)
    nt = t // tm
    tiles_per_seq = seq_len // tm
    keep = 8
    assert kk - 1 <= keep

    def body(da_ref, g_ref, halo_ref, u_ref, cw_ref, cb_ref, wg_ref, wu_ref,
             dg_ref, du_ref, dhn_ref, dcw_ref, dcb_ref, nxt):
        tile = (nt - 1 - pl.program_id(0)) % tiles_per_seq

        @pl.when(pl.program_id(0) == 0)
        def _():
            dcw_ref[...] = jnp.zeros(dcw_ref.shape, F32)
            dcb_ref[...] = jnp.zeros(dcb_ref.shape, F32)

        @pl.when(tile == tiles_per_seq - 1)
        def _():
            nxt[...] = jnp.zeros(nxt.shape, F32)

        row = lax.broadcasted_iota(jnp.int32, (tm, 1), 0)
        row8 = lax.broadcasted_iota(jnp.int32, (keep, 1), 0)
        gp = g_ref[...]
        cwv = cw_ref[...]
        prev = jnp.where(tile > 0, halo_ref[...], 0.0)
        gate = _conv_fwd(gp, cwv, cb_ref[...], row)
        fix = jnp.zeros((keep, f), F32)
        for j in range(1, kk):
            fix = fix + cwv[kk - 1 - j:kk - j, :] * jnp.where(row8 < j, pltpu.roll(prev, j, 0), 0.0)
        gate = jnp.concatenate([gate[:keep] + fix, gate[keep:]], axis=0)
        gel, dgel = _gelu_parts(gate)
        da = da_ref[...]
        dup = (da * gel).astype(BF16)
        du_ref[...] = dup
        dgate = da * u_ref[...] * dgel
        dcb_ref[...] += jnp.sum(dgate, axis=0, keepdims=True)
        after = nxt[...]
        dgp = cwv[kk - 1:kk, :] * dgate
        tail_fix = jnp.zeros((keep, f), F32)
        dws = [None] * kk
        dws[kk - 1] = jnp.sum(dgate * gp, axis=0, keepdims=True)
        for j in range(1, kk):
            wj = cwv[kk - 1 - j:kk - j, :]
            dgp = dgp + wj * _shift_up(dgate, j, 0.0, row)
            tail_fix = tail_fix + wj * jnp.where(row8 >= keep - j, pltpu.roll(after, keep - j, 0), 0.0)
            dws[kk - 1 - j] = (jnp.sum(dgate * _shift_dn(gp, j, 0.0, row), axis=0, keepdims=True)
                               + jnp.sum(dgate[:keep] * jnp.where(row8 < j, pltpu.roll(prev, j, 0), 0.0),
                                         axis=0, keepdims=True))
        dgp = jnp.concatenate([dgp[:tm - keep], dgp[tm - keep:] + tail_fix], axis=0).astype(BF16)
        nxt[...] = dgate[:keep]
        dcw_ref[...] += jnp.concatenate(dws, axis=0)
        dg_ref[...] = dgp
        dhn_ref[...] = _dot_nt(dgp, wg_ref[...]) + _dot_nt(dup, wu_ref[...])

    def rev(i):
        return nt - 1 - i

    rows_f = pl.BlockSpec((tm, f), lambda i: (rev(i), 0))
    halo = pl.BlockSpec((None, keep, f), lambda i: (jnp.maximum(rev(i) * (tm // keep) - 1, 0), 0, 0))
    once = pl.Buffered(1)
    sd = jax.ShapeDtypeStruct
    r_ins, r_in_specs, r_outs, r_out_specs, r_sems = _ride_args(ride)
    outs = pl.pallas_call(
        _riding(body, 8, 5, 1, ride, 1), name=name, grid=(nt,),
        in_specs=[rows_f, rows_f, halo, rows_f, _whole(cw.shape), _whole(cb.shape),
                  pl.BlockSpec(wg.shape, lambda i: (0, 0), pipeline_mode=once),
                  pl.BlockSpec(wu.shape, lambda i: (0, 0), pipeline_mode=once)] + r_in_specs,
        out_specs=[rows_f, rows_f, pl.BlockSpec((tm, d), lambda i: (rev(i), 0)), _whole((kk, f)), _whole((1, f))]
        + r_out_specs,
        out_shape=[sd((t, f), BF16), sd((t, f), BF16), sd((t, d), F32), sd((kk, f), F32), sd((1, f), F32)] + r_outs,
        scratch_shapes=[pltpu.VMEM((keep, f), F32)] + r_sems,
        compiler_params=_params(("arbitrary",)),
    )(dact, gate_pre, gate_pre.reshape(t // keep, keep, f), up, cw, cb, wg, wu, *r_ins)
    return outs[:5], _ride_results(ride, outs[5:])


def _t5_bucket(dist):
    max_exact = REL_BUCKETS // 2
    d = np.maximum(dist, 1).astype(np.float32)
    large = max_exact + np.log(d / max_exact) / math.log(REL_MAX_DIST / max_exact) * (REL_BUCKETS - max_exact)
    large = np.minimum(large.astype(np.int32), REL_BUCKETS - 1)
    return np.where(dist < max_exact, dist, large).astype(np.int32)


def _band(window, dilation):
    qi = np.arange(ATTN_BLOCK)[:, None]
    kj = np.arange(2 * ATTN_BLOCK)[None, :]
    delta = ATTN_BLOCK + qi - kj
    mask = (delta >= 0) & (delta <= window // dilation)
    bucket = _t5_bucket(np.maximum(delta, 0) * dilation)
    return mask, bucket


def _attn_blocks(s, r):
    m = s // r
    assert m % ATTN_BLOCK == 0, "sequence length must be a multiple of dilation * block"
    return m // ATTN_BLOCK


def _perm_load(ref, r):
    if r == 1:
        return ref[...]
    m = ref.shape[0] // r
    return jnp.concatenate([ref[pl.ds(c, m, stride=r), :] for c in range(r)], axis=0)


def _perm_store(ref, g, val, r, add=False):
    if r == 1:
        ref[g] = ref[g] + val if add else val
        return
    m = val.shape[0] // r
    for c in range(r):
        rows = pl.ds(c, m, stride=r)
        part = val[c * m:(c + 1) * m]
        ref[g, rows, :] = ref[g, rows, :] + part if add else part


def _blocks(x):
    return x.reshape(x.shape[0] // ATTN_BLOCK, ATTN_BLOCK, x.shape[1])


def _prev_blocks(x):
    return jnp.concatenate([x[:1], x[:-1]], axis=0)


def _next_blocks(x):
    return jnp.concatenate([x[1:], jnp.zeros_like(x[:1])], axis=0)


def _first_block_neg(s, r):
    nblk = s // ATTN_BLOCK
    idx = lax.broadcasted_iota(jnp.int32, (nblk, 1, 1), 0)
    return jnp.where(idx % _attn_blocks(s, r) == 0, NEG, 0.0)


def _bdot_nt(a, b):
    return lax.dot_general(a, b, (((2,), (2,)), ((0,), (0,))), preferred_element_type=F32)


def _bdot(a, b):
    return lax.dot_general(a, b, (((2,), (1,)), ((0,), (0,))), preferred_element_type=F32)


def _bdot_tn(a, b):
    return lax.dot_general(a, b, (((1,), (1,)), ((0,), (0,))), preferred_element_type=F32)


def attn_fwd(qkv, biasm, n_heads, name, ride=()):
    b, s, _ = qkv.shape
    h = n_heads
    scale = HEAD_DIM ** -0.5
    blk = ATTN_BLOCK

    def body(q1_ref, q2_ref, q3_ref, k_ref, v_ref, bias_ref, o_ref, lse_ref, acc, m_s, l_s):
        for g, q_ref in enumerate((q1_ref, q2_ref, q3_ref)):
            r = DILATED[g][1]
            first = _first_block_neg(s, r)
            q = _blocks(_perm_load(q_ref, r).astype(BF16))
            k = _blocks(_perm_load(k_ref, r).astype(BF16))
            v = _blocks(_perm_load(v_ref, r).astype(BF16))
            s_cur = _bdot_nt(q, k) * scale + bias_ref[g, :, blk:]
            s_prev = _bdot_nt(q, _prev_blocks(k)) * scale + bias_ref[g, :, :blk] + first
            m = jnp.max(jnp.maximum(s_cur, s_prev), axis=-1, keepdims=True)
            p_cur = jnp.exp(s_cur - m)
            p_prev = jnp.exp(s_prev - m)
            l = jnp.sum(p_cur + p_prev, axis=-1, keepdims=True)
            o = _bdot(p_cur.astype(BF16), v) + _bdot(p_prev.astype(BF16), _prev_blocks(v))
            _perm_store(acc, g, o.reshape(s, HEAD_DIM), r)
            _perm_store(m_s, g, m.reshape(s, 1), r)
            _perm_store(l_s, g, l.reshape(s, 1), r)
        m_all = jnp.maximum(jnp.maximum(m_s[0], m_s[1]), m_s[2])
        w = [jnp.exp(m_s[g] - m_all) for g in range(N_GROUPS)]
        l = w[0] * l_s[0] + w[1] * l_s[1] + w[2] * l_s[2]
        o_ref[...] = (w[0] * acc[0] + w[1] * acc[1] + w[2] * acc[2]) / l
        lse_ref[...] = m_all + jnp.log(l)

    def col(j):
        return pl.BlockSpec((None, s, HEAD_DIM), lambda bi, hi, j=j: (bi, 0, j * h + hi))

    r_ins, r_in_specs, r_outs, r_out_specs, r_sems = _ride_args(ride)
    outs = pl.pallas_call(
        _riding(body, 6, 2, 3, ride, 2), name=name, grid=(b, h),
        in_specs=[col(0), col(1), col(2), col(3), col(4),
                  pl.BlockSpec((N_GROUPS, None, blk, 2 * blk), lambda bi, hi: (0, hi, 0, 0))] + r_in_specs,
        out_specs=[pl.BlockSpec((None, s, HEAD_DIM), lambda bi, hi: (bi, 0, hi)),
                   pl.BlockSpec((None, None, s, 1), lambda bi, hi: (bi, hi, 0, 0))] + r_out_specs,
        out_shape=[jax.ShapeDtypeStruct((b, s, h * HEAD_DIM), F32), jax.ShapeDtypeStruct((b, h, s, 1), F32)] + r_outs,
        scratch_shapes=[pltpu.VMEM((N_GROUPS, s, HEAD_DIM), F32), pltpu.VMEM((N_GROUPS, s, 1), F32),
                        pltpu.VMEM((N_GROUPS, s, 1), F32)] + r_sems,
        compiler_params=_params(("arbitrary", "arbitrary")),
    )(qkv, qkv, qkv, qkv, qkv, biasm, *r_ins)
    return outs[0], outs[1], _ride_results(ride, outs[2:])


def attn_bwd(qkv, biasm, o, lse, do, n_heads, name, ride=()):
    b, s, _ = qkv.shape
    h = n_heads
    scale = HEAD_DIM ** -0.5
    blk = ATTN_BLOCK

    def body(q1_ref, q2_ref, q3_ref, k_ref, v_ref, bias_ref, o_ref, lse_ref, do_ref,
             dq1_ref, dq2_ref, dq3_ref, dk_ref, dv_ref, ds_ref, dq_acc, kv_acc, delta):
        delta[...] = jnp.sum(do_ref[...] * o_ref[...], axis=-1, keepdims=True)
        kv_acc[...] = jnp.zeros(kv_acc.shape, F32)
        for g, q_ref in enumerate((q1_ref, q2_ref, q3_ref)):
            r = DILATED[g][1]
            first = _first_block_neg(s, r)
            q = _blocks(_perm_load(q_ref, r).astype(BF16))
            k = _blocks(_perm_load(k_ref, r).astype(BF16))
            v = _blocks(_perm_load(v_ref, r).astype(BF16))
            dob = _blocks(_perm_load(do_ref, r).astype(BF16))
            lse_b = _blocks(_perm_load(lse_ref, r))
            dl_b = _blocks(_perm_load(delta, r))
            k_prev, v_prev = _prev_blocks(k), _prev_blocks(v)
            p_cur = jnp.exp(_bdot_nt(q, k) * scale + bias_ref[g, :, blk:] - lse_b)
            p_prev = jnp.exp(_bdot_nt(q, k_prev) * scale + bias_ref[g, :, :blk] + first - lse_b)
            ds_cur = p_cur * (_bdot_nt(dob, v) - dl_b)
            ds_prev = p_prev * (_bdot_nt(dob, v_prev) - dl_b)
            ds_ref[g, :, blk:] = jnp.sum(ds_cur, axis=0)
            ds_ref[g, :, :blk] = jnp.sum(ds_prev, axis=0)
            ds_cur_b, ds_prev_b = ds_cur.astype(BF16), ds_prev.astype(BF16)
            dq = (_bdot(ds_cur_b, k) + _bdot(ds_prev_b, k_prev)) * scale
            _perm_store(dq_acc, g, dq.reshape(s, HEAD_DIM), r)
            dk = (_bdot_tn(ds_cur_b, q) + _next_blocks(_bdot_tn(ds_prev_b, q))) * scale
            dv = _bdot_tn(p_cur.astype(BF16), dob) + _next_blocks(_bdot_tn(p_prev.astype(BF16), dob))
            _perm_store(kv_acc, 0, dk.reshape(s, HEAD_DIM), r, add=True)
            _perm_store(kv_acc, 1, dv.reshape(s, HEAD_DIM), r, add=True)
        for g, out_ref in enumerate((dq1_ref, dq2_ref, dq3_ref)):
            out_ref[...] = dq_acc[g].astype(out_ref.dtype)
        dk_ref[...] = kv_acc[0].astype(dk_ref.dtype)
        dv_ref[...] = kv_acc[1].astype(dv_ref.dtype)

    def col(j):
        return pl.BlockSpec((None, s, HEAD_DIM), lambda bi, hi, j=j: (bi, 0, j * h + hi))

    head = pl.BlockSpec((None, s, HEAD_DIM), lambda bi, hi: (bi, 0, hi))
    sd = jax.ShapeDtypeStruct
    r_ins, r_in_specs, r_outs, r_out_specs, r_sems = _ride_args(ride)
    outs = pl.pallas_call(
        _riding(body, 9, 6, 3, ride, 2), name=name, grid=(b, h),
        in_specs=[col(0), col(1), col(2), col(3), col(4),
                  pl.BlockSpec((N_GROUPS, None, blk, 2 * blk), lambda bi, hi: (0, hi, 0, 0)),
                  head, pl.BlockSpec((None, None, s, 1), lambda bi, hi: (bi, hi, 0, 0)), head] + r_in_specs,
        out_specs=[head] * 5 + [pl.BlockSpec((None, None, N_GROUPS, blk, 2 * blk), lambda bi, hi: (bi, hi, 0, 0, 0))]
        + r_out_specs,
        out_shape=[sd((b, s, h * HEAD_DIM), BF16)] * 5 + [sd((b, h, N_GROUPS, blk, 2 * blk), F32)] + r_outs,
        scratch_shapes=[pltpu.VMEM((N_GROUPS, s, HEAD_DIM), F32), pltpu.VMEM((2, s, HEAD_DIM), F32),
                        pltpu.VMEM((s, 1), F32)] + r_sems,
        compiler_params=_params(("arbitrary", "arbitrary")),
    )(qkv, qkv, qkv, qkv, qkv, biasm, o, lse, do, *r_ins)
    return outs[:6], _ride_results(ride, outs[6:])


def bias_table(rel_rows, bucket_f, n_heads, name):
    g, blk, blk2 = bucket_f.shape
    h = n_heads

    def body(rb_ref, bk_ref, o_ref):
        bk = bk_ref[...]
        rb = rb_ref[...]
        acc = jnp.full((blk, blk2), NEG, F32)
        for bucket in range(REL_BUCKETS):
            acc = jnp.where(bk == float(bucket), rb[:, bucket:bucket + 1], acc)
        o_ref[...] = acc

    return pl.pallas_call(
        body, name=name, grid=(g, h),
        in_specs=[pl.BlockSpec((None, 1, 128), lambda gi, hi: (gi * h + hi, 0, 0)),
                  pl.BlockSpec((None, blk, blk2), lambda gi, hi: (gi, 0, 0))],
        out_specs=pl.BlockSpec((None, None, blk, blk2), lambda gi, hi: (gi, hi, 0, 0)),
        out_shape=jax.ShapeDtypeStruct((g, h, blk, blk2), F32),
        compiler_params=_params(("parallel", "parallel")),
    )(rel_rows, bucket_f)


def bias_grad(ds_sum, bucket_f, name):
    b, h, g, blk, blk2 = ds_sum.shape

    def body(ds_ref, bk_ref, o_ref):
        tot = jnp.sum(ds_ref[...], axis=0)
        bk = bk_ref[...]
        lane = lax.broadcasted_iota(jnp.int32, (1, 128), 1)
        vec = jnp.zeros((1, 128), F32)
        for bucket in range(REL_BUCKETS):
            val = jnp.sum(jnp.where(bk == float(bucket), tot, 0.0), keepdims=True)
            vec = vec + jnp.where(lane == bucket, val, 0.0)
        o_ref[...] = vec

    return pl.pallas_call(
        body, name=name, grid=(g, h),
        in_specs=[pl.BlockSpec((b, None, None, blk, blk2), lambda gi, hi: (0, hi, gi, 0, 0)),
                  pl.BlockSpec((None, blk, blk2), lambda gi, hi: (gi, 0, 0))],
        out_specs=pl.BlockSpec((None, 1, 128), lambda gi, hi: (gi * h + hi, 0, 0)),
        out_shape=jax.ShapeDtypeStruct((g * h, 1, 128), F32),
        compiler_params=_params(("parallel", "parallel")),
    )(ds_sum, bucket_f)


def _chip_peers():
    x, y, c = lax.axis_index("x"), lax.axis_index("y"), lax.axis_index("c")
    me = 2 * x + y
    peers = [(1 - x, y, c), (x, 1 - y, c), (1 - x, 1 - y, c)]
    peer_chip = [2 * (1 - x) + y, 2 * x + (1 - y), 2 * (1 - x) + (1 - y)]
    return me, peers, peer_chip


def _any_specs(n):
    return [pl.BlockSpec(memory_space=pl.ANY)] * n


_MID_NUM, _MID_DEN = 3, 4


class _Exchange:
    def start(self, ins, outs, sems):
        local, sends, _ = self._copies(ins, outs, sems)
        for cp in local + sends:
            cp.start()

    def mid(self, ins, outs, sems):
        pass

    def wait(self, ins, outs, sems):
        local, sends, recvs = self._copies(ins, outs, sems)
        for cp in recvs():
            cp.wait_recv()
        for cp in sends:
            cp.wait_send()
        for cp in local:
            cp.wait()


class _Gather(_Exchange):
    HALF_ROWS = 16

    def __init__(self, arrays):
        n = len(arrays)
        self.ins = list(arrays)
        self.split = [a.shape[0] % (2 * self.HALF_ROWS) == 0 for a in arrays]
        self.out_shape = [jax.ShapeDtypeStruct((N_CHIPS,) + a.shape, a.dtype) for a in arrays]
        dma = pltpu.SemaphoreType.DMA
        self.sems = [dma((3 * n,)), dma((3 * n,)), dma((n,)), dma((3 * n,)), dma((3 * n,))]

    def _half(self, i, ref, sibling=False):
        if not self.split[i]:
            return ref
        half = self.ins[i].shape[0] // 2
        c = lax.axis_index("c")
        c = 1 - c if sibling else c
        return ref.at[pl.ds(pl.multiple_of(c * half, self.HALF_ROWS), half)]

    def _plan(self, ins, outs, sems):
        send1, recv1, local_sems, send2, recv2 = sems
        me, peers, peer_chip = _chip_peers()
        x, y, c = lax.axis_index("x"), lax.axis_index("y"), lax.axis_index("c")
        n = len(ins)
        pairs = [(i, k) for i in range(n) for k in range(3)]

        def fetch(i, k, slot):
            return pltpu.make_async_remote_copy(src_ref=self._half(i, ins[i]), dst_ref=self._half(i, outs[i].at[slot]),
                                                send_sem=send1.at[3 * i + k], recv_sem=recv1.at[3 * i + k],
                                                device_id=peers[k], device_id_type=MESH)

        def share(i, k, sibling):
            part = self._half(i, outs[i].at[peer_chip[k]], sibling)
            return pltpu.make_async_remote_copy(src_ref=part, dst_ref=part, send_sem=send2.at[3 * i + k],
                                                recv_sem=recv2.at[3 * i + k], device_id=(x, y, 1 - c),
                                                device_id_type=MESH)

        split_pairs = [(i, k) for i, k in pairs if self.split[i]]
        return dict(
            local=lambda: [pltpu.make_async_copy(ins[i], outs[i].at[me], local_sems.at[i]) for i in range(n)],
            fetch_out=lambda: [fetch(i, k, me) for i, k in pairs],
            fetch_in=lambda: [fetch(i, k, peer_chip[k]) for i, k in pairs],
            share_out=lambda: [share(i, k, False) for i, k in split_pairs],
            share_in=lambda: [share(i, k, True) for i, k in split_pairs])

    def start(self, ins, outs, sems):
        plan = self._plan(ins, outs, sems)
        for cp in plan["local"]() + plan["fetch_out"]():
            cp.start()

    def mid(self, ins, outs, sems):
        plan = self._plan(ins, outs, sems)
        for cp in plan["fetch_in"]():
            cp.wait_recv()
        for cp in plan["share_out"]():
            cp.start()

    def wait(self, ins, outs, sems):
        plan = self._plan(ins, outs, sems)
        for cp in plan["share_in"]():
            cp.wait_recv()
        for cp in plan["fetch_out"]() + plan["share_out"]():
            cp.wait_send()
        for cp in plan["local"]():
            cp.wait()


class _Scatter(_Exchange):
    def __init__(self, slabs, whole=()):
        self.n_slabs = len(slabs)
        self.ins = list(slabs) + list(whole)
        n = len(self.ins)
        self.out_shape = [jax.ShapeDtypeStruct(a.shape, a.dtype) for a in slabs] \
            + [jax.ShapeDtypeStruct((N_CHIPS,) + a.shape, a.dtype) for a in whole]
        self.sems = [pltpu.SemaphoreType.DMA((3 * n,)), pltpu.SemaphoreType.DMA((3 * n,)), pltpu.SemaphoreType.DMA((n,))]

    def _copies(self, ins, outs, sems):
        send_sems, recv_sems, local_sems = sems
        me, peers, peer_chip = _chip_peers()
        n = len(ins)

        def src(i, chip):
            return ins[i].at[chip] if i < self.n_slabs else ins[i]

        def remote(i, k, src_chip, slot):
            return pltpu.make_async_remote_copy(src_ref=src(i, src_chip), dst_ref=outs[i].at[slot],
                                                send_sem=send_sems.at[3 * i + k], recv_sem=recv_sems.at[3 * i + k],
                                                device_id=peers[k], device_id_type=MESH)

        local = [pltpu.make_async_copy(src(i, me), outs[i].at[me], local_sems.at[i]) for i in range(n)]
        sends = [remote(i, k, peer_chip[k], me) for i in range(n) for k in range(3)]
        return local, sends, lambda: [remote(i, k, me, peer_chip[k]) for i in range(n) for k in range(3)]


class _Swap(_Exchange):
    def __init__(self, arrays):
        n = len(arrays)
        self.ins = list(arrays)
        self.out_shape = [jax.ShapeDtypeStruct(a.shape, a.dtype) for a in arrays]
        self.sems = [pltpu.SemaphoreType.DMA((n,)), pltpu.SemaphoreType.DMA((n,))]

    def _copies(self, ins, outs, sems):
        send_sems, recv_sems = sems
        x, y, c = lax.axis_index("x"), lax.axis_index("y"), lax.axis_index("c")
        cps = [pltpu.make_async_remote_copy(src_ref=ins[i], dst_ref=outs[i], send_sem=send_sems.at[i],
                                            recv_sem=recv_sems.at[i], device_id=(x, y, 1 - c), device_id_type=MESH)
               for i in range(len(ins))]
        return [], cps, lambda: cps


def _riding(body, n_in, n_out, n_scratch, ride, rank):
    if not ride:
        return body
    r_in = sum(len(e.ins) for e in ride)
    r_out = sum(len(e.out_shape) for e in ride)

    def split(refs, sizes):
        out, a = [], 0
        for sz in sizes:
            out.append(refs[a:a + sz])
            a += sz
        return out

    def wrapped(*refs):
        a = 0
        parts = []
        for sz in (n_in, r_in, n_out, r_out, n_scratch):
            parts.append(refs[a:a + sz])
            a += sz
        own_in, ex_in, own_out, ex_out, own_scratch = parts
        ex_sems = refs[a:]
        ins = split(ex_in, [len(e.ins) for e in ride])
        outs = split(ex_out, [len(e.out_shape) for e in ride])
        sems = split(ex_sems, [len(e.sems) for e in ride])
        if rank:
            step, total = 0, 1
            for d in range(rank):
                step = step * pl.num_programs(d) + pl.program_id(d)
                total = total * pl.num_programs(d)

            @pl.when(step == 0)
            def _():
                for e, i, o, s in zip(ride, ins, outs, sems):
                    e.start(i, o, s)

            body(*own_in, *own_out, *own_scratch)

            @pl.when(step == (total * _MID_NUM) // _MID_DEN)
            def _():
                for e, i, o, s in zip(ride, ins, outs, sems):
                    e.mid(i, o, s)

            @pl.when(step == total - 1)
            def _():
                for e, i, o, s in zip(ride, ins, outs, sems):
                    e.wait(i, o, s)
        else:
            for phase in ("start", "mid", "wait"):
                for e, i, o, s in zip(ride, ins, outs, sems):
                    getattr(e, phase)(i, o, s)

    return wrapped


def _ride_args(ride):
    ins = [a for e in ride for a in e.ins]
    outs = [s for e in ride for s in e.out_shape]
    sems = [s for e in ride for s in e.sems]
    return ins, _any_specs(len(ins)), outs, _any_specs(len(outs)), sems


def _ride_results(ride, flat):
    out, a = [], 0
    for e in ride:
        out.append(list(flat[a:a + len(e.out_shape)]))
        a += len(e.out_shape)
    return out


def exchange(ride, name):
    ins, in_specs, outs, out_specs, sems = _ride_args(ride)
    res = pl.pallas_call(
        _riding(lambda: None, 0, 0, 0, ride, 0), name=name,
        in_specs=in_specs, out_specs=out_specs, out_shape=outs, scratch_shapes=sems,
    )(*ins)
    return _ride_results(ride, res)


def _sum_slots(ref):
    acc = ref[0].astype(F32)
    for j in range(1, ref.shape[0]):
        acc = acc + ref[j].astype(F32)
    return acc


def sum_pairs(mine, other, name, tr=176):
    n, r, w = mine.shape
    tr = _tile(r, tr)

    def body(a_ref, b_ref, o_ref):
        o_ref[...] = _sum_slots(a_ref) + _sum_slots(b_ref)

    spec = pl.BlockSpec((n, tr, w), lambda i: (0, i, 0))
    return pl.pallas_call(
        body, name=name, grid=(r // tr,),
        in_specs=[spec, spec], out_specs=_rows(tr, w),
        out_shape=jax.ShapeDtypeStruct((r, w), F32),
        compiler_params=_params(("parallel",)),
    )(mine, other)


def adamw(w, m, v, gs, name, tr=256):
    r, c = w.shape
    tr = r if r % 8 else _tile(r, tr)
    c1 = 1.0 - ADAM_B1 ** ADAM_STEP
    c2 = 1.0 - ADAM_B2 ** ADAM_STEP
    ng = len(gs)

    def body(w_ref, m_ref, v_ref, *refs):
        g_refs, (g_ref, d_ref, nm_ref, nv_ref) = refs[:ng], refs[ng:]
        g = g_refs[0][...] if ng == 1 else _sum_slots(g_refs[0]) + _sum_slots(g_refs[1])
        nm = ADAM_B1 * m_ref[...] + (1.0 - ADAM_B1) * g
        nv = ADAM_B2 * v_ref[...] + (1.0 - ADAM_B2) * (g * g)
        g_ref[...] = g
        nm_ref[...] = nm
        nv_ref[...] = nv
        d_ref[...] = (-ADAM_LR) * ((nm / c1) / (jnp.sqrt(nv / c2) + ADAM_EPS) + ADAM_WD * w_ref[...])

    spec = _rows(tr, c)
    gspec = spec if ng == 1 else pl.BlockSpec((N_CHIPS, tr, c), lambda i: (0, i, 0))
    return pl.pallas_call(
        body, name=name, grid=(r // tr,),
        in_specs=[spec] * 3 + [gspec] * ng, out_specs=[spec] * 4,
        out_shape=[jax.ShapeDtypeStruct((r, c), F32)] * 4,
        compiler_params=_params(("parallel",)),
    )(w, m, v, *gs)


_PARAMS = (
    ("rel_bias", None), ("norm_mix_pre", None), ("norm_mix_post", None), ("w_in", 1), ("conv_rnn_w", 1),
    ("conv_rnn_b", None), ("w_rg_a", None), ("b_rg_a", None), ("w_rg_x", None), ("b_rg_x", None),
    ("lru_lambda", None), ("w_branch_rnn", 0), ("w_branch_att", 1), ("w_out", 0), ("norm_ffn_pre", None),
    ("norm_ffn_post", None), ("w_ffn_gate", 1), ("w_ffn_up", 1), ("conv_ffn_w", 1), ("conv_ffn_b", None),
    ("w_ffn_down", 0),
)
_SMALL = 65536


def _as2d(a):
    a = a[0] if a.shape[0] == 1 and a.ndim >= 3 else a
    return a.reshape(-1, a.shape[-1]) if a.ndim == 3 else a


def _pack(pieces, dtype):
    flat = jnp.concatenate([p.astype(dtype).reshape(-1) for p in pieces])
    unit = PACK_W * PACK_ROWS
    pad = (-flat.shape[0]) % unit
    flat = jnp.pad(flat, (0, pad))
    return flat.reshape(-1, PACK_W)


def _unpack(buf, shapes):
    flat = buf.reshape(-1)
    out, off = [], 0
    for shp in shapes:
        n = int(np.prod(shp))
        out.append(flat[off:off + n].reshape(shp))
        off += n
    return out


def _join(slots, ax):
    if ax == 0:
        return slots.reshape(-1, slots.shape[-1])
    return jnp.transpose(slots, (1, 0, 2)).reshape(slots.shape[1], -1)


def _cut(full, ax):
    if ax == 0:
        return full.reshape(N_CHIPS, -1, full.shape[-1])
    return jnp.transpose(full.reshape(full.shape[0], N_CHIPS, -1), (1, 0, 2))


def kernel(x, rel_bias, norm_mix_pre, norm_mix_post, w_in, conv_rnn_w, conv_rnn_b, w_rg_a, b_rg_a, w_rg_x, b_rg_x, lru_lambda, w_branch_rnn, w_branch_att, w_out, norm_ffn_pre, norm_ffn_post, w_ffn_gate, w_ffn_up, conv_ffn_w, conv_ffn_b, w_ffn_down, loss_target, m_rel_bias, m_norm_mix_pre, m_norm_mix_post, m_w_in, m_conv_rnn_w, m_conv_rnn_b, m_w_rg_a, m_b_rg_a, m_w_rg_x, m_b_rg_x, m_lru_lambda, m_w_branch_rnn, m_w_branch_att, m_w_out, m_norm_ffn_pre, m_norm_ffn_post, m_w_ffn_gate, m_w_ffn_up, m_conv_ffn_w, m_conv_ffn_b, m_w_ffn_down, v_rel_bias, v_norm_mix_pre, v_norm_mix_post, v_w_in, v_conv_rnn_w, v_conv_rnn_b, v_w_rg_a, v_b_rg_a, v_w_rg_x, v_b_rg_x, v_lru_lambda, v_w_branch_rnn, v_w_branch_att, v_w_out, v_norm_ffn_pre, v_norm_ffn_post, v_w_ffn_gate, v_w_ffn_up, v_conv_ffn_w, v_conv_ffn_b, v_w_ffn_down):
    args = dict(locals())
    names = [n for n, _ in _PARAMS]
    axis = dict(_PARAMS)
    w_loc = {n: args[n] for n in names}
    m_loc = {n: args["m_" + n] for n in names}
    v_loc = {n: args["v_" + n] for n in names}
    sharded = [n for n in names if axis[n] is not None]
    replicated = [n for n in names if axis[n] is None]

    big = [n for n in sharded if w_loc[n].size >= _SMALL]
    small_sharded = [n for n in sharded if n not in big]
    small = replicated + small_sharded

    first = ["w_in"] + small_sharded
    srcs = [_as2d(w_loc[n]).astype(BF16) if n in big else _as2d(w_loc[n]) for n in first]
    (gathered,) = exchange([_Gather(srcs)], "gather_first")
    p = {n: _join(a, axis[n]) for n, a in zip(first, gathered)}
    for n in replicated:
        p[n] = _as2d(w_loc[n])
    shards = {n: _as2d(w_loc[n]).astype(BF16) for n in big if n not in first}

    received, sibling, g_small, loss_part = _local_step(x, loss_target, p, shards)

    pack = _pack([g_small[n] for n in small], BF16)
    ((received["small"],),) = exchange([_Scatter([], [pack])], "scatter_small")
    late = [n for n in received if n not in sibling]
    (swapped,) = exchange([_Swap([received[n] for n in late])], "swap_last")
    sibling.update(zip(late, swapped))
    small_sum = sum_pairs(received["small"], sibling["small"], "sum_small")
    g_tot = dict(zip(small, _unpack(small_sum, [g_small[n].shape for n in small])))
    chip = 2 * lax.axis_index("x") + lax.axis_index("y")
    for n in small_sharded:
        size = g_tot[n].shape[axis[n]] // N_CHIPS
        g_tot[n] = lax.dynamic_slice_in_dim(g_tot[n], chip * size, size, axis=axis[n])

    out_g, out_d, out_m, out_v = {}, {}, {}, {}
    for i, n in enumerate(names):
        shp = w_loc[n].shape
        gs = (received[n], sibling[n]) if n in big else (g_tot[n],)
        g, d, nm, nv = adamw(_as2d(w_loc[n]), _as2d(m_loc[n]), _as2d(v_loc[n]), gs, "adamw_" + n)
        out_g[n], out_d[n], out_m[n], out_v[n] = (t.reshape(shp) for t in (g, d, nm, nv))

    d_model = x.shape[-1]
    loss = lax.psum(0.5 * jnp.sum(loss_part) / d_model, ("x", "y", "c"))
    grad_x = g_small["x"]
    return (loss, grad_x, *[out_g[n] for n in names], *[out_d[n] for n in names],
            *[out_m[n] for n in names], *[out_v[n] for n in names])


def _local_step(x, target, p, shards):
    axis = dict(_PARAMS)
    b, s, d = x.shape
    t = b * s
    rnn = p["b_rg_a"].shape[1]
    ffn = p["conv_ffn_b"].shape[1]
    nbk = rnn // p["w_rg_a"].shape[1]
    hkv = (p["w_in"].shape[1] - rnn - 2 * d) // (N_GROUPS + 2)
    h = hkv // HEAD_DIM
    nq = N_GROUPS * hkv

    x2 = x.reshape(t, d)
    tgt = target.reshape(t, d)
    w_in = p["w_in"]
    in_splits = (rnn, nq + 2 * hkv, 2 * d)
    wa = p["w_rg_a"].reshape(nbk, -1, p["w_rg_a"].shape[1]).astype(BF16)
    wx = p["w_rg_x"].reshape(nbk, -1, p["w_rg_x"].shape[1]).astype(BF16)
    cw_r, cb_r = p["conv_rnn_w"], p["conv_rnn_b"]
    cw_f, cb_f = p["conv_ffn_w"], p["conv_ffn_b"]

    masks, buckets = zip(*[_band(w_, r_) for w_, r_ in DILATED])
    bucket_f = jnp.asarray(np.where(np.stack(masks), np.stack(buckets), -1).astype(np.float32))
    rel_rows = jnp.pad(p["rel_bias"].T, ((0, 0), (0, 128 - REL_BUCKETS)))[:, None, :]
    biasm = bias_table(rel_rows, bucket_f, h, "bias_table")

    early = ["w_branch_rnn", "w_branch_att", "w_out"]
    hn1, (xr, qkv, gts), (got,) = norm_mm(x2, p["norm_mix_pre"], [w_in], [in_splits], "in_proj",
                                          ride=[_Gather([shards[n] for n in early])])
    p.update({n: _join(a, axis[n]) for n, a in zip(early, got)})
    xr3 = xr.reshape(b, s, rnn)
    (y_rnn, a_rnn, xc_rnn), (got,) = rglru_fwd(xr3, cw_r, cb_r, wa, p["b_rg_a"], wx, p["b_rg_x"], p["lru_lambda"], "rglru_fwd",
                              ride=[_Gather([shards[n] for n in ("w_ffn_gate", "w_ffn_up")])])
    p.update({n: _join(a, axis[n]) for n, a in zip(("w_ffn_gate", "w_ffn_up"), got)})
    qkv3 = qkv.reshape(b, s, -1)
    o_att, lse, ((got,),) = attn_fwd(qkv3, biasm, h, "attn_fwd", ride=[_Gather([shards["w_ffn_down"]])])
    p["w_ffn_down"] = _join(got, axis["w_ffn_down"])
    merged, br, ba, mix, h1 = merge_out(y_rnn.reshape(t, rnn), o_att.reshape(t, hkv), gts, p["w_branch_rnn"],
                                        p["w_branch_att"], p["w_out"], p["norm_mix_post"], x2, "merge_out")
    hn2, gate_pre, up, act = ffn_in_act(h1, p["norm_ffn_pre"], p["w_ffn_gate"], p["w_ffn_up"], cw_f, cb_f, s, "ffn_in")

    g, gb = {}, {}
    recv, sib = {}, {}

    def rows4(a):
        return a.reshape(N_CHIPS, -1, a.shape[-1])

    dy, dff, dact, g["norm_ffn_post"], loss_part = ffn_down_loss(act, p["w_ffn_down"], p["norm_ffn_post"], h1, tgt,
                                                                  "ffn_down")
    gb["w_ffn_down"] = rows4(mm_tn(act, [dff], "ffn_down_dw"))
    (dgp, dup, dhn2, g["conv_ffn_w"], g["conv_ffn_b"]), ((recv["w_ffn_down"],),) = ffn_in_bwd(
        dact, gate_pre, up, cw_f, cb_f, p["w_ffn_gate"], p["w_ffn_up"], s, "ffn_in_bwd",
        ride=[_Scatter([gb["w_ffn_down"]])])
    gb["w_ffn_gate"] = mm_tn(hn2, [dgp], "ffn_gate_dw", col_shards=N_CHIPS)
    gb["w_ffn_up"] = mm_tn(hn2, [dup], "ffn_up_dw", col_shards=N_CHIPS)
    dh1, dmix, dbr, dba, dgts, dy_rnn, do_att, g["norm_ffn_pre"], g["norm_mix_post"] = mid_bwd(
        dhn2, h1, p["norm_ffn_pre"], dy, mix, p["norm_mix_post"], p["w_out"], gts, br, ba,
        p["w_branch_rnn"], p["w_branch_att"], "mid_bwd")
    gb["w_out"] = rows4(mm_tn(merged, [dmix], "out_proj_dw"))
    gb["w_branch_rnn"] = rows4(mm_tn(y_rnn.reshape(t, rnn), [dbr], "branch_rnn_dw"))
    gb["w_branch_att"] = mm_tn(o_att.reshape(t, hkv), [dba], "branch_att_dw", col_shards=N_CHIPS)
    ffn_in = ["w_ffn_gate", "w_ffn_up"]
    (dxr, g["conv_rnn_w"], g["conv_rnn_b"], dwa, g["b_rg_a"], dwx, g["b_rg_x"], g["lru_lambda"]), (got,) = rglru_bwd(
        xr3, y_rnn, dy_rnn.reshape(b, s, rnn), a_rnn, xc_rnn, cw_r, cb_r, wa, p["b_rg_a"], wx, p["b_rg_x"], p["lru_lambda"], "rglru_bwd",
        ride=[_Scatter([gb[n] for n in ffn_in])])
    recv.update(zip(ffn_in, got))
    g["w_rg_a"] = dwa.reshape(p["w_rg_a"].shape)
    g["w_rg_x"] = dwx.reshape(p["w_rg_x"].shape)
    mid = ["w_out", "w_branch_rnn", "w_branch_att"]
    early_recv = ["w_ffn_down"] + ffn_in
    (dq1, dq2, dq3, dk, dv, ds_sum), (got, swapped) = attn_bwd(
        qkv3, biasm, o_att, lse, do_att.reshape(b, s, hkv), h, "attn_bwd",
        ride=[_Scatter([gb[n] for n in mid]), _Swap([recv[n] for n in early_recv])])
    recv.update(zip(mid, got))
    sib.update(zip(early_recv, swapped))
    rows = bias_grad(ds_sum, bucket_f, "bias_grad")
    g["rel_bias"] = rows[:, 0, :REL_BUCKETS].T
    dproj = [dxr.reshape(t, rnn)] + [a.reshape(t, hkv) for a in (dq1, dq2, dq3, dk, dv)] + [dgts]
    dw_a = mm_tn(hn1, dproj[:4], "in_proj_dw_a")[0]
    dw_b = mm_tn(hn1, dproj[4:], "in_proj_dw_b")[0]
    gb["w_in"] = _cut(jnp.concatenate([dw_a, dw_b], axis=1), 1)
    dx, g["norm_mix_pre"], ((recv["w_in"],), got) = mm_nt(
        [(dproj, w_in)], "in_proj_dx", norm=(x2, p["norm_mix_pre"], dh1),
        ride=[_Scatter([gb["w_in"]]), _Swap([recv[n] for n in mid])])
    sib.update(zip(mid, got))
    g["x"] = dx.reshape(b, s, d)
    return recv, sib, g, loss_part
```

```python
import functools
import math

import numpy as np
import jax
import jax.numpy as jnp
from jax import lax
from jax.experimental import pallas as pl
from jax.experimental.pallas import tpu as pltpu

F32 = jnp.float32
BF16 = jnp.bfloat16

EPS = 1e-6
HEAD_DIM = 128
ATTN_BLOCK = 128
DILATED = ((128, 1), (512, 4), (2048, 16))
N_GROUPS = len(DILATED)
REL_BUCKETS = 32
REL_MAX_DIST = 2048
LRU_C = 8.0
NEG = -1e30

ADAM_LR = 0.001
ADAM_B1 = 0.9
ADAM_B2 = 0.999
ADAM_EPS = 1e-08
ADAM_WD = 0.01
ADAM_STEP = 10

N_CHIPS = 4
PACK_W = 1024
PACK_ROWS = 16
VMEM_LIMIT = 56 * 1024 * 1024
MESH = pl.DeviceIdType.MESH


def _params(sem=None):
    return pltpu.CompilerParams(dimension_semantics=sem, vmem_limit_bytes=VMEM_LIMIT)


def _dot(a, b):
    return jnp.dot(a, b, preferred_element_type=F32)


def _dot_nt(a, b):
    return lax.dot_general(a, b, (((1,), (1,)), ((), ())), preferred_element_type=F32)


def _dot_tn(a, b):
    return lax.dot_general(a, b, (((0,), (0,)), ((), ())), preferred_element_type=F32)


def _sig(x):
    return 0.5 * jnp.tanh(0.5 * x) + 0.5


def _rows(tm, w):
    return pl.BlockSpec((tm, w), lambda i: (i, 0))


def _whole(shape):
    nd = len(shape)
    return pl.BlockSpec(tuple(shape), lambda *_: (0,) * nd)


def _tile(t, want):
    while t % want:
        want //= 2
    return want


def norm_mm(x, g, ws, splits, name, ride=(), tm=256):
    t, d = x.shape
    tm = _tile(t, tm)
    nw = len(ws)
    widths = [n for sp in splits for n in sp]

    def body(x_ref, g_ref, *refs):
        w_refs, hn_ref, o_refs = refs[:nw], refs[nw], refs[nw + 1:]
        xv = x_ref[...]
        inv = lax.rsqrt(jnp.mean(xv * xv, axis=-1, keepdims=True) + EPS)
        hn = (xv * inv * g_ref[...]).astype(BF16)
        hn_ref[...] = hn
        o = 0
        for w_ref, sp in zip(w_refs, splits):
            off = 0
            for n in sp:
                o_refs[o][...] = _dot(hn, w_ref[:, off:off + n])
                off += n
                o += 1

    r_ins, r_in_specs, r_outs, r_out_specs, r_sems = _ride_args(ride)
    n_out = 1 + len(widths)
    outs = pl.pallas_call(
        _riding(body, 2 + nw, n_out, 0, ride, 1), name=name, grid=(t // tm,),
        in_specs=[_rows(tm, d), _whole(g.shape)] + [_whole(w.shape) for w in ws] + r_in_specs,
        out_specs=[_rows(tm, d)] + [_rows(tm, n) for n in widths] + r_out_specs,
        out_shape=[jax.ShapeDtypeStruct((t, d), BF16)] + [jax.ShapeDtypeStruct((t, n), F32) for n in widths] + r_outs,
        scratch_shapes=r_sems,
        compiler_params=_params(("arbitrary",)),
    )(x, g, *ws, *r_ins)
    return outs[0], outs[1:n_out], _ride_results(ride, outs[n_out:])


def mm_nt(groups, name, ride=(), norm=None, tm=256):
    dys_all = [dy for dys, _ in groups for dy in dys]
    ws = [w for _, w in groups]
    t = dys_all[0].shape[0]
    k = ws[0].shape[0]
    tm = _tile(t, tm)
    n = len(dys_all)
    extra = list(norm) if norm else []

    def body(*refs):
        dy_refs, w_refs = refs[:n], refs[n:n + len(ws)]
        rest = refs[n + len(ws):]
        acc = None
        i = 0
        for (dys, _), w_ref in zip(groups, w_refs):
            off = 0
            for dy in dys:
                width = dy.shape[1]
                part = _dot_nt(dy_refs[i][...].astype(BF16), w_ref[:, off:off + width])
                acc = part if acc is None else acc + part
                off += width
                i += 1
        if norm:
            u_ref, g_ref, add_ref, o_ref, dg_ref = rest

            @pl.when(pl.program_id(0) == 0)
            def _():
                dg_ref[...] = jnp.zeros(dg_ref.shape, F32)

            du, dg_rows = _rms_bwd(acc, u_ref[...], g_ref[...])
            o_ref[...] = du + add_ref[...]
            dg_ref[...] += jnp.sum(dg_rows, axis=0, keepdims=True)
        else:
            rest[0][...] = acc

    n_out = 2 if norm else 1
    r_ins, r_in_specs, r_outs, r_out_specs, r_sems = _ride_args(ride)
    outs = pl.pallas_call(
        _riding(body, n + len(ws) + len(extra), n_out, 0, ride, 1), name=name, grid=(t // tm,),
        in_specs=[_rows(tm, dy.shape[1]) for dy in dys_all] + [_whole(w.shape) for w in ws]
        + ([_rows(tm, k), _whole((1, k)), _rows(tm, k)] if norm else []) + r_in_specs,
        out_specs=[_rows(tm, k)] + ([_whole((1, k))] if norm else []) + r_out_specs,
        out_shape=[jax.ShapeDtypeStruct((t, k), F32)] + ([jax.ShapeDtypeStruct((1, k), F32)] if norm else []) + r_outs,
        scratch_shapes=r_sems,
        compiler_params=_params(("arbitrary",)),
    )(*dys_all, *ws, *extra, *r_ins)
    return tuple(outs[:n_out]) + (_ride_results(ride, outs[n_out:]),)


def mm_tn(a, dys, name, col_shards=1, tm=1024):
    t, k = a.shape
    tm = _tile(t, tm)
    n = len(dys)
    ntot = sum(dy.shape[1] for dy in dys)
    wsh = ntot // col_shards

    def body(a_ref, *refs):
        dy_refs, o_ref, acc = refs[:n], refs[n], refs[n + 1]

        @pl.when(pl.program_id(0) == 0)
        def _():
            acc[...] = jnp.zeros(acc.shape, F32)

        av = a_ref[...].astype(BF16)
        off = 0
        for dy_ref in dy_refs:
            width = dy_ref.shape[1]
            acc[:, off:off + width] += _dot_tn(av, dy_ref[...].astype(BF16))
            off += width

        @pl.when(pl.program_id(0) == pl.num_programs(0) - 1)
        def _():
            for j in range(col_shards):
                o_ref[j] = acc[:, j * wsh:(j + 1) * wsh].astype(o_ref.dtype)

    return pl.pallas_call(
        body, name=name, grid=(t // tm,),
        in_specs=[_rows(tm, k)] + [_rows(tm, dy.shape[1]) for dy in dys],
        out_specs=_whole((col_shards, k, wsh)),
        out_shape=jax.ShapeDtypeStruct((col_shards, k, wsh), BF16),
        scratch_shapes=[pltpu.VMEM((k, ntot), F32)],
        compiler_params=_params(("arbitrary",)),
    )(a, *dys)


def _rms_bwd(dz, u, g):
    d = u.shape[-1]
    inv = lax.rsqrt(jnp.mean(u * u, axis=-1, keepdims=True) + EPS)
    dzg = dz * g
    proj = jnp.sum(dzg * u, axis=-1, keepdims=True) * (1.0 / d)
    du = inv * (dzg - u * (inv * inv) * proj)
    dg_rows = dz * u * inv
    return du, dg_rows


def ffn_down_loss(act, wd, g, h1, target, name, tm=256):
    t, f = act.shape
    d = wd.shape[1]
    tm = _tile(t, tm)

    def body(a_ref, w_ref, g_ref, h_ref, t_ref, dy_ref, dff_ref, dact_ref, dg_ref, loss_ref):
        @pl.when(pl.program_id(0) == 0)
        def _():
            dg_ref[...] = jnp.zeros(dg_ref.shape, F32)
            loss_ref[...] = jnp.zeros(loss_ref.shape, F32)

        wv = w_ref[...]
        gv = g_ref[...]
        ff = _dot(a_ref[...], wv)
        inv = lax.rsqrt(jnp.mean(ff * ff, axis=-1, keepdims=True) + EPS)
        err = h_ref[...] + ff * inv * gv - t_ref[...]
        loss_ref[...] += jnp.sum(err * err, axis=0, keepdims=True)
        dy = err * (1.0 / d)
        dy_ref[...] = dy
        du, dg_rows = _rms_bwd(dy, ff, gv)
        dff = du.astype(BF16)
        dff_ref[...] = dff
        dg_ref[...] += jnp.sum(dg_rows, axis=0, keepdims=True)
        dact_ref[...] = _dot_nt(dff, wv)

    return pl.pallas_call(
        body, name=name, grid=(t // tm,),
        in_specs=[_rows(tm, f), _whole(wd.shape), _whole(g.shape), _rows(tm, d), _rows(tm, d)],
        out_specs=[_rows(tm, d), _rows(tm, d), _rows(tm, f), _whole((1, d)), _whole((1, d))],
        out_shape=[jax.ShapeDtypeStruct((t, d), F32), jax.ShapeDtypeStruct((t, d), BF16),
                   jax.ShapeDtypeStruct((t, f), F32), jax.ShapeDtypeStruct((1, d), F32),
                   jax.ShapeDtypeStruct((1, d), F32)],
        compiler_params=_params(("arbitrary",)),
    )(act, wd, g, h1, target)


def merge_out(y_rnn, o_att, gts, w_br, w_ba, w_out, g, x, name, tm=256):
    t = y_rnn.shape[0]
    d = w_br.shape[1]
    tm = _tile(t, tm)

    def body(y_ref, o_ref, g_ref, wbr_ref, wba_ref, wo_ref, gn_ref, x_ref, m_ref, br_ref, ba_ref, mix_ref, h_ref):
        br = _dot(y_ref[...].astype(BF16), wbr_ref[...])
        ba = _dot(o_ref[...].astype(BF16), wba_ref[...])
        gv = g_ref[...]
        merged = (_sig(gv[:, :d]) * br + _sig(gv[:, d:]) * ba).astype(BF16)
        m_ref[...] = merged
        br_ref[...] = br
        ba_ref[...] = ba
        mix = _dot(merged, wo_ref[...])
        mix_ref[...] = mix
        inv = lax.rsqrt(jnp.mean(mix * mix, axis=-1, keepdims=True) + EPS)
        h_ref[...] = x_ref[...] + mix * inv * gn_ref[...]

    sd = jax.ShapeDtypeStruct
    return pl.pallas_call(
        body, name=name, grid=(t // tm,),
        in_specs=[_rows(tm, y_rnn.shape[1]), _rows(tm, o_att.shape[1]), _rows(tm, 2 * d),
                  _whole(w_br.shape), _whole(w_ba.shape), _whole(w_out.shape), _whole(g.shape), _rows(tm, d)],
        out_specs=[_rows(tm, d)] * 5,
        out_shape=[sd((t, d), BF16), sd((t, d), F32), sd((t, d), F32), sd((t, d), F32), sd((t, d), F32)],
        compiler_params=_params(("parallel",)),
    )(y_rnn, o_att, gts, w_br, w_ba, w_out, g, x)


def mid_bwd(dhn2, h1, g_ffn, dy, mix, g_mix, w_out, gts, br, ba, w_br, w_ba, merged, y_rnn, o_att, name, tm=256):
    t, d = h1.shape
    tm = _tile(t, tm)
    rnn, hkv = w_br.shape[0], w_ba.shape[0]
    wsh = d // N_CHIPS

    def body(dhn_ref, h_ref, gf_ref, dy_ref, mix_ref, gm_ref, wo_ref, g_ref, br_ref, ba_ref, wbr_ref, wba_ref,
             m_ref, y_ref, o_ref, dh_ref, dg_ref, dyr_ref, doa_ref, dgf_ref, dgm_ref, dwo_ref, dwbr_ref, dwba_ref,
             acc_o, acc_br, acc_ba):
        @pl.when(pl.program_id(0) == 0)
        def _():
            dgf_ref[...] = jnp.zeros(dgf_ref.shape, F32)
            dgm_ref[...] = jnp.zeros(dgm_ref.shape, F32)
            acc_o[...] = jnp.zeros(acc_o.shape, F32)
            acc_br[...] = jnp.zeros(acc_br.shape, F32)
            acc_ba[...] = jnp.zeros(acc_ba.shape, F32)

        du, rows_f = _rms_bwd(dhn_ref[...], h_ref[...], gf_ref[...])
        dh1 = du + dy_ref[...]
        dh_ref[...] = dh1
        dgf_ref[...] += jnp.sum(rows_f, axis=0, keepdims=True)
        dmx, rows_m = _rms_bwd(dh1, mix_ref[...], gm_ref[...])
        dmix = dmx.astype(BF16)
        acc_o[...] += _dot_tn(m_ref[...], dmix)
        dgm_ref[...] += jnp.sum(rows_m, axis=0, keepdims=True)
        dm = _dot_nt(dmix, wo_ref[...])
        gv = g_ref[...]
        sr = _sig(gv[:, :d])
        sa = _sig(gv[:, d:])
        dbr = (dm * sr).astype(BF16)
        dba = (dm * sa).astype(BF16)
        acc_br[...] += _dot_tn(y_ref[...].astype(BF16), dbr)
        acc_ba[...] += _dot_tn(o_ref[...].astype(BF16), dba)
        dg_ref[:, :d] = (dm * br_ref[...] * sr * (1.0 - sr)).astype(BF16)
        dg_ref[:, d:] = (dm * ba_ref[...] * sa * (1.0 - sa)).astype(BF16)
        dyr_ref[...] = _dot_nt(dbr, wbr_ref[...])
        doa_ref[...] = _dot_nt(dba, wba_ref[...])

        @pl.when(pl.program_id(0) == pl.num_programs(0) - 1)
        def _():
            dwo_ref[...] = acc_o[...].astype(BF16)
            dwbr_ref[...] = acc_br[...].astype(BF16)
            for j in range(N_CHIPS):
                dwba_ref[j] = acc_ba[:, j * wsh:(j + 1) * wsh].astype(BF16)

    sd = jax.ShapeDtypeStruct
    row, vec = _rows(tm, d), _whole((1, d))
    once = pl.Buffered(1)

    def resident(shape):
        return pl.BlockSpec(shape, lambda i: (0,) * len(shape), pipeline_mode=once)

    return pl.pallas_call(
        body, name=name, grid=(t // tm,),
        in_specs=[row, row, vec, row, row, vec, resident(w_out.shape), _rows(tm, 2 * d), row, row,
                  resident(w_br.shape), resident(w_ba.shape), row, _rows(tm, rnn), _rows(tm, hkv)],
        out_specs=[row, _rows(tm, 2 * d), _rows(tm, rnn), _rows(tm, hkv), vec, vec,
                   resident((d, d)), resident((rnn, d)), resident((N_CHIPS, hkv, wsh))],
        out_shape=[sd((t, d), F32), sd((t, 2 * d), BF16), sd((t, rnn), F32), sd((t, hkv), F32), sd((1, d), F32),
                   sd((1, d), F32), sd((d, d), BF16), sd((rnn, d), BF16), sd((N_CHIPS, hkv, wsh), BF16)],
        scratch_shapes=[pltpu.VMEM((d, d), F32), pltpu.VMEM((rnn, d), F32), pltpu.VMEM((hkv, d), F32)],
        compiler_params=_params(("arbitrary",)),
    )(dhn2, h1, g_ffn, dy, mix, g_mix, w_out, gts, br, ba, w_br, w_ba, merged, y_rnn, o_att)


def _shift_dn(x, d, fill, row):
    return jnp.where(row >= d, pltpu.roll(x, d, 0), fill)


def _shift_up(x, d, fill, row):
    s = x.shape[0]
    return jnp.where(row < s - d, pltpu.roll(x, s - d, 0), fill)


def _conv_fwd(x, w, b, row):
    kk = w.shape[0]
    y = b + w[kk - 1:kk, :] * x
    for j in range(1, kk):
        y = y + w[kk - 1 - j:kk - j, :] * _shift_dn(x, j, 0.0, row)
    return y


def _conv_bwd(dy, x, w, row):
    kk = w.shape[0]
    dx = w[kk - 1:kk, :] * dy
    dws = [None] * kk
    dws[kk - 1] = jnp.sum(dy * x, axis=0, keepdims=True)
    for j in range(1, kk):
        dx = dx + w[kk - 1 - j:kk - j, :] * _shift_up(dy, j, 0.0, row)
        dws[kk - 1 - j] = jnp.sum(dy * _shift_dn(x, j, 0.0, row), axis=0, keepdims=True)
    return dx, jnp.concatenate(dws, axis=0)


def _softplus(z):
    y = jnp.exp(-jnp.abs(z))
    u = 1.0 + y
    dd = u - 1.0
    log1p = jnp.where(dd == 0.0, y, jnp.log(u) * (y / jnp.where(dd == 0.0, 1.0, dd)))
    return jnp.maximum(z, 0.0) + log1p


def _lru_decay(xb, wa, ba, lam):
    r = _sig(_dot(xb, wa) + ba)
    sp = _softplus(-lam)
    la = (-LRU_C) * r * sp
    return r, sp, la, jnp.exp(la)


def _lru_gates(xc, wa, ba, wx, bx, lam):
    xb = xc.astype(BF16)
    r, sp, la, a = _lru_decay(xb, wa, ba, lam)
    i = _sig(_dot(xb, wx) + bx)
    one_m_a2 = jnp.tanh(-la) * (1.0 + a * a)
    inv_mult = lax.rsqrt(one_m_a2)
    return r, i, sp, a, one_m_a2 * inv_mult, inv_mult


def _seg_len(s):
    seg = -(-s // 8)
    return seg + (4 - seg % 8) % 8


def _scan_rows(a_pad, u_pad, out_pad, reverse):
    planes, rows8, lanes = a_pad.shape
    seg = rows8 // 8
    sub = lax.broadcasted_iota(jnp.int32, (planes, 8, lanes), 1)

    unroll = 4

    def rows(k, d):
        i = k * unroll + d
        return pl.ds((seg - 1 - i) if reverse else i, 8, stride=seg)

    def ends(k, carry):
        h, p = carry
        for d in range(unroll):
            a = a_pad[:, rows(k, d), :]
            h = a * h + u_pad[:, rows(k, d), :]
            p = a * p
        return h, p

    init = (jnp.zeros((planes, 8, lanes), F32), jnp.ones((planes, 8, lanes), F32))
    h_end, p_end = lax.fori_loop(0, seg // unroll, ends, init)
    start = jnp.zeros((planes, 8, lanes), F32)
    for _ in range(7):
        nxt = h_end + p_end * start
        if reverse:
            start = jnp.where(sub < 7, pltpu.roll(nxt, 7, 1), 0.0)
        else:
            start = jnp.where(sub >= 1, pltpu.roll(nxt, 1, 1), 0.0)

    def redo(k, h):
        for d in range(unroll):
            h = a_pad[:, rows(k, d), :] * h + u_pad[:, rows(k, d), :]
            out_pad[:, rows(k, d), :] = h
        return h

    lax.fori_loop(0, seg // unroll, redo, start)


def _lru_cols(c, rb):
    return 2 * rb if c % (2 * rb) == 0 else rb


def rglru_fwd(xr, cw, cb, wa, ba, wx, bx, lam, name, ride=()):
    b, s, c = xr.shape
    rb = wa.shape[1]
    kk = cw.shape[0]
    cols = _lru_cols(c, rb)
    nj = cols // rb
    seg = _seg_len(s)

    def body(x_ref, cw_ref, cb_ref, wa_ref, ba_ref, wx_ref, bx_ref, lam_ref, h_ref, a_ref, xc_ref, a_pad, u_pad, h_pad):
        row = lax.broadcasted_iota(jnp.int32, (s, rb), 0)
        for j in range(nj):
            cs = slice(j * rb, (j + 1) * rb)
            xc = _conv_fwd(x_ref[:, cs], cw_ref[:, cs], cb_ref[:, cs], row)
            _, i, _, a, mult, _ = _lru_gates(xc, wa_ref[j], ba_ref[:, cs], wx_ref[j], bx_ref[:, cs], lam_ref[:, cs])
            xc_ref[:, cs] = xc
            a_ref[:, cs] = a
            a_pad[j, 0:s, :] = a
            u_pad[j, 0:s, :] = mult * (i * xc)
        a_pad[:, s:, :] = jnp.ones((nj, 8 * seg - s, rb), F32)
        u_pad[:, s:, :] = jnp.zeros((nj, 8 * seg - s, rb), F32)
        _scan_rows(a_pad, u_pad, h_pad, False)
        for j in range(nj):
            h_ref[:, j * rb:(j + 1) * rb] = h_pad[j, 0:s, :]

    vec = pl.BlockSpec((1, cols), lambda bi, n: (0, n))
    seq = pl.BlockSpec((None, s, cols), lambda bi, n: (bi, 0, n))
    mat = pl.BlockSpec((nj, rb, rb), lambda bi, n: (n, 0, 0))
    r_ins, r_in_specs, r_outs, r_out_specs, r_sems = _ride_args(ride)
    outs = pl.pallas_call(
        _riding(body, 8, 3, 3, ride, 2), name=name, grid=(b, c // cols),
        in_specs=[seq, pl.BlockSpec((kk, cols), lambda bi, n: (0, n)), vec, mat, vec, mat, vec, vec] + r_in_specs,
        out_specs=[seq] * 3 + r_out_specs,
        out_shape=[jax.ShapeDtypeStruct((b, s, c), F32)] * 3 + r_outs,
        scratch_shapes=[pltpu.VMEM((nj, 8 * seg, rb), F32)] * 3 + r_sems,
        compiler_params=_params(("arbitrary", "arbitrary")),
    )(xr, cw, cb, wa, ba, wx, bx, lam, *r_ins)
    return outs[:3], _ride_results(ride, outs[3:])


def rglru_bwd(xr, h, dh, a_fwd, xc_fwd, cw, cb, wa, ba, wx, bx, lam, name, ride=()):
    b, s, c = xr.shape
    nb, rb = wa.shape[0], wa.shape[1]
    kk = cw.shape[0]
    cols = _lru_cols(c, rb)
    nj = cols // rb
    seg = _seg_len(s)

    def body(x_ref, h_ref, dh_ref, a_ref, xc_ref, cw_ref, cb_ref, wa_ref, ba_ref, wx_ref, bx_ref, lam_ref,
             dx_ref, dcw_ref, dcb_ref, dwa_ref, dba_ref, dwx_ref, dbx_ref, dlam_ref, b_pad, g_pad, l_pad):
        @pl.when(pl.program_id(1) == 0)
        def _():
            for ref in (dcw_ref, dcb_ref, dwa_ref, dba_ref, dwx_ref, dbx_ref, dlam_ref):
                ref[...] = jnp.zeros(ref.shape, F32)

        row = lax.broadcasted_iota(jnp.int32, (s, rb), 0)

        for j in range(nj):
            b_pad[j, 0:s, :] = _shift_up(a_ref[:, j * rb:(j + 1) * rb], 1, 0.0, row)
            g_pad[j, 0:s, :] = dh_ref[:, j * rb:(j + 1) * rb]
        b_pad[:, s:, :] = jnp.zeros((nj, 8 * seg - s, rb), F32)
        g_pad[:, s:, :] = jnp.zeros((nj, 8 * seg - s, rb), F32)
        _scan_rows(b_pad, g_pad, l_pad, True)

        for j in range(nj):
            cs = slice(j * rb, (j + 1) * rb)
            x = x_ref[:, cs]
            cwv = cw_ref[:, cs]
            wav, wxv, lamv = wa_ref[j], wx_ref[j], lam_ref[:, cs]
            xc = xc_ref[:, cs]
            r, i, sp, a, mult, inv_mult = _lru_gates(xc, wav, ba_ref[:, cs], wxv, bx_ref[:, cs], lamv)
            lmb = l_pad[j, 0:s, :]
            h_prev = _shift_dn(h_ref[:, cs], 1, 0.0, row)
            da = lmb * h_prev
            ixc = i * xc
            dla = da * a - (lmb * ixc) * (a * a) * inv_mult
            di = lmb * mult * xc
            dxc = lmb * mult * i
            dr = dla * ((-LRU_C) * sp)
            dsp = jnp.sum(dla * ((-LRU_C) * r), axis=0, keepdims=True)
            dga = dr * r * (1.0 - r)
            dgx = di * i * (1.0 - i)
            dga_b, dgx_b = dga.astype(BF16), dgx.astype(BF16)
            xb = xc.astype(BF16)
            dwa_ref[j] += _dot_tn(xb, dga_b)
            dwx_ref[j] += _dot_tn(xb, dgx_b)
            dba_ref[:, cs] += jnp.sum(dga, axis=0, keepdims=True)
            dbx_ref[:, cs] += jnp.sum(dgx, axis=0, keepdims=True)
            dlam_ref[:, cs] += dsp * (-_sig(-lamv))
            dxc = dxc + _dot_nt(dga_b, wav) + _dot_nt(dgx_b, wxv)
            dcb_ref[:, cs] += jnp.sum(dxc, axis=0, keepdims=True)
            dx, dcw = _conv_bwd(dxc, x, cwv, row)
            dcw_ref[:, cs] += dcw
            dx_ref[:, cs] = dx.astype(dx_ref.dtype)

    vec = pl.BlockSpec((1, cols), lambda n, bi: (0, n))
    seq = pl.BlockSpec((None, s, cols), lambda n, bi: (bi, 0, n))
    mat = pl.BlockSpec((nj, rb, rb), lambda n, bi: (n, 0, 0))
    cws = pl.BlockSpec((kk, cols), lambda n, bi: (0, n))
    sd = jax.ShapeDtypeStruct
    r_ins, r_in_specs, r_outs, r_out_specs, r_sems = _ride_args(ride)
    outs = pl.pallas_call(
        _riding(body, 12, 8, 3, ride, 2), name=name, grid=(c // cols, b),
        in_specs=[seq, seq, seq, seq, seq, cws, vec, mat, vec, mat, vec, vec] + r_in_specs,
        out_specs=[seq, cws, vec, mat, vec, mat, vec, vec] + r_out_specs,
        out_shape=[sd((b, s, c), BF16), sd((kk, c), F32), sd((1, c), F32), sd((nb, rb, rb), F32),
                   sd((1, c), F32), sd((nb, rb, rb), F32), sd((1, c), F32), sd((1, c), F32)] + r_outs,
        scratch_shapes=[pltpu.VMEM((nj, 8 * seg, rb), F32)] * 3 + r_sems,
        compiler_params=_params(("arbitrary", "arbitrary")),
    )(xr, h, dh, a_fwd, xc_fwd, cw, cb, wa, ba, wx, bx, lam, *r_ins)
    return outs[:8], _ride_results(ride, outs[8:])


_GELU_C = math.sqrt(2.0 / math.pi)


def _gelu_parts(x):
    th = jnp.tanh(_GELU_C * (x + 0.044715 * x * x * x))
    gel = 0.5 * x * (1.0 + th)
    dgel = 0.5 * (1.0 + th) + 0.5 * x * (1.0 - th * th) * _GELU_C * (1.0 + 3 * 0.044715 * x * x)
    return gel, dgel


def ffn_in_act(x, g, wg, wu, cw, cb, seq_len, name, tm=256):
    t, d = x.shape
    f = wg.shape[1]
    kk = cw.shape[0]
    tm = _tile(seq_len, tm)
    tiles_per_seq = seq_len // tm
    keep = 8
    assert kk - 1 <= keep

    def body(x_ref, g_ref, wg_ref, wu_ref, cw_ref, cb_ref, hn_ref, gp_ref, up_ref, act_ref, tail):
        @pl.when(pl.program_id(0) % tiles_per_seq == 0)
        def _():
            tail[...] = jnp.zeros(tail.shape, F32)

        xv = x_ref[...]
        inv = lax.rsqrt(jnp.mean(xv * xv, axis=-1, keepdims=True) + EPS)
        hn = (xv * inv * g_ref[...]).astype(BF16)
        hn_ref[...] = hn
        gp = _dot(hn, wg_ref[...])
        up = _dot(hn, wu_ref[...])
        gp_ref[...] = gp
        up_ref[...] = up
        cwv = cw_ref[...]
        row = lax.broadcasted_iota(jnp.int32, (tm, 1), 0)
        gate = _conv_fwd(gp, cwv, cb_ref[...], row)
        row8 = lax.broadcasted_iota(jnp.int32, (keep, 1), 0)
        prev = tail[...]
        fix = jnp.zeros((keep, f), F32)
        for j in range(1, kk):
            fix = fix + cwv[kk - 1 - j:kk - j, :] * jnp.where(row8 < j, pltpu.roll(prev, j, 0), 0.0)
        gate = jnp.concatenate([gate[:keep] + fix, gate[keep:]], axis=0)
        tail[...] = gp[tm - keep:, :]
        gel, _ = _gelu_parts(gate)
        act_ref[...] = (gel * up).astype(BF16)

    sd = jax.ShapeDtypeStruct
    return pl.pallas_call(
        body, name=name, grid=(t // tm,),
        in_specs=[_rows(tm, d), _whole(g.shape), _whole(wg.shape), _whole(wu.shape), _whole(cw.shape), _whole(cb.shape)],
        out_specs=[_rows(tm, d), _rows(tm, f), _rows(tm, f), _rows(tm, f)],
        out_shape=[sd((t, d), BF16), sd((t, f), F32), sd((t, f), F32), sd((t, f), BF16)],
        scratch_shapes=[pltpu.VMEM((keep, f), F32)],
        compiler_params=_params(("arbitrary",)),
    )(x, g, wg, wu, cw, cb)


def ffn_in_bwd(dact, gate_pre, up, cw, cb, wg, wu, seq_len, name, ride=(), tm=256):
    t, f = gate_pre.shape
    d = wg.shape[0]
    kk = cw.shape[0]
    tm = _tile(seq_len, tm)
    nt = t // tm
    tiles_per_seq = seq_len // tm
    keep = 8
    assert kk - 1 <= keep

    def body(da_ref, g_ref, halo_ref, u_ref, cw_ref, cb_ref, wg_ref, wu_ref,
             dg_ref, du_ref, dhn_ref, dcw_ref, dcb_ref, nxt):
        tile = (nt - 1 - pl.program_id(0)) % tiles_per_seq

        @pl.when(pl.program_id(0) == 0)
        def _():
            dcw_ref[...] = jnp.zeros(dcw_ref.shape, F32)
            dcb_ref[...] = jnp.zeros(dcb_ref.shape, F32)

        @pl.when(tile == tiles_per_seq - 1)
        def _():
            nxt[...] = jnp.zeros(nxt.shape, F32)

        row = lax.broadcasted_iota(jnp.int32, (tm, 1), 0)
        row8 = lax.broadcasted_iota(jnp.int32, (keep, 1), 0)
        gp = g_ref[...]
        cwv = cw_ref[...]
        prev = jnp.where(tile > 0, halo_ref[...], 0.0)
        gate = _conv_fwd(gp, cwv, cb_ref[...], row)
        fix = jnp.zeros((keep, f), F32)
        for j in range(1, kk):
            fix = fix + cwv[kk - 1 - j:kk - j, :] * jnp.where(row8 < j, pltpu.roll(prev, j, 0), 0.0)
        gate = jnp.concatenate([gate[:keep] + fix, gate[keep:]], axis=0)
        gel, dgel = _gelu_parts(gate)
        da = da_ref[...]
        dup = (da * gel).astype(BF16)
        du_ref[...] = dup
        dgate = da * u_ref[...] * dgel
        dcb_ref[...] += jnp.sum(dgate, axis=0, keepdims=True)
        after = nxt[...]
        dgp = cwv[kk - 1:kk, :] * dgate
        tail_fix = jnp.zeros((keep, f), F32)
        dws = [None] * kk
        dws[kk - 1] = jnp.sum(dgate * gp, axis=0, keepdims=True)
        for j in range(1, kk):
            wj = cwv[kk - 1 - j:kk - j, :]
            dgp = dgp + wj * _shift_up(dgate, j, 0.0, row)
            tail_fix = tail_fix + wj * jnp.where(row8 >= keep - j, pltpu.roll(after, keep - j, 0), 0.0)
            dws[kk - 1 - j] = (jnp.sum(dgate * _shift_dn(gp, j, 0.0, row), axis=0, keepdims=True)
                               + jnp.sum(dgate[:keep] * jnp.where(row8 < j, pltpu.roll(prev, j, 0), 0.0),
                                         axis=0, keepdims=True))
        dgp = jnp.concatenate([dgp[:tm - keep], dgp[tm - keep:] + tail_fix], axis=0).astype(BF16)
        nxt[...] = dgate[:keep]
        dcw_ref[...] += jnp.concatenate(dws, axis=0)
        dg_ref[...] = dgp
        dhn_ref[...] = _dot_nt(dgp, wg_ref[...]) + _dot_nt(dup, wu_ref[...])

    def rev(i):
        return nt - 1 - i

    rows_f = pl.BlockSpec((tm, f), lambda i: (rev(i), 0))
    halo = pl.BlockSpec((None, keep, f), lambda i: (jnp.maximum(rev(i) * (tm // keep) - 1, 0), 0, 0))
    once = pl.Buffered(1)
    sd = jax.ShapeDtypeStruct
    r_ins, r_in_specs, r_outs, r_out_specs, r_sems = _ride_args(ride)
    outs = pl.pallas_call(
        _riding(body, 8, 5, 1, ride, 1), name=name, grid=(nt,),
        in_specs=[rows_f, rows_f, halo, rows_f, _whole(cw.shape), _whole(cb.shape),
                  pl.BlockSpec(wg.shape, lambda i: (0, 0), pipeline_mode=once),
                  pl.BlockSpec(wu.shape, lambda i: (0, 0), pipeline_mode=once)] + r_in_specs,
        out_specs=[rows_f, rows_f, pl.BlockSpec((tm, d), lambda i: (rev(i), 0)), _whole((kk, f)), _whole((1, f))]
        + r_out_specs,
        out_shape=[sd((t, f), BF16), sd((t, f), BF16), sd((t, d), F32), sd((kk, f), F32), sd((1, f), F32)] + r_outs,
        scratch_shapes=[pltpu.VMEM((keep, f), F32)] + r_sems,
        compiler_params=_params(("arbitrary",)),
    )(dact, gate_pre, gate_pre.reshape(t // keep, keep, f), up, cw, cb, wg, wu, *r_ins)
    return outs[:5], _ride_results(ride, outs[5:])


def _t5_bucket(dist):
    max_exact = REL_BUCKETS // 2
    d = np.maximum(dist, 1).astype(np.float32)
    large = max_exact + np.log(d / max_exact) / math.log(REL_MAX_DIST / max_exact) * (REL_BUCKETS - max_exact)
    large = np.minimum(large.astype(np.int32), REL_BUCKETS - 1)
    return np.where(dist < max_exact, dist, large).astype(np.int32)


def _band(window, dilation):
    qi = np.arange(ATTN_BLOCK)[:, None]
    kj = np.arange(2 * ATTN_BLOCK)[None, :]
    delta = ATTN_BLOCK + qi - kj
    mask = (delta >= 0) & (delta <= window // dilation)
    bucket = _t5_bucket(np.maximum(delta, 0) * dilation)
    return mask, bucket


def _attn_blocks(s, r):
    m = s // r
    assert m % ATTN_BLOCK == 0, "sequence length must be a multiple of dilation * block"
    return m // ATTN_BLOCK


def _perm_load(ref, r):
    if r == 1:
        return ref[...]
    m = ref.shape[0] // r
    return jnp.concatenate([ref[pl.ds(c, m, stride=r), :] for c in range(r)], axis=0)


def _perm_store(ref, g, val, r, add=False):
    if r == 1:
        ref[g] = ref[g] + val if add else val
        return
    m = val.shape[0] // r
    for c in range(r):
        rows = pl.ds(c, m, stride=r)
        part = val[c * m:(c + 1) * m]
        ref[g, rows, :] = ref[g, rows, :] + part if add else part


def _blocks(x):
    return x.reshape(x.shape[0] // ATTN_BLOCK, ATTN_BLOCK, x.shape[1])


def _prev_blocks(x):
    return jnp.concatenate([x[:1], x[:-1]], axis=0)


def _next_blocks(x):
    return jnp.concatenate([x[1:], jnp.zeros_like(x[:1])], axis=0)


def _first_block_neg(s, r):
    nblk = s // ATTN_BLOCK
    idx = lax.broadcasted_iota(jnp.int32, (nblk, 1, 1), 0)
    return jnp.where(idx % _attn_blocks(s, r) == 0, NEG, 0.0)


def _bdot_nt(a, b):
    return lax.dot_general(a, b, (((2,), (2,)), ((0,), (0,))), preferred_element_type=F32)


def _bdot(a, b):
    return lax.dot_general(a, b, (((2,), (1,)), ((0,), (0,))), preferred_element_type=F32)


def _bdot_tn(a, b):
    return lax.dot_general(a, b, (((1,), (1,)), ((0,), (0,))), preferred_element_type=F32)


def attn_fwd(qkv, biasm, n_heads, name, ride=()):
    b, s, _ = qkv.shape
    h = n_heads
    scale = HEAD_DIM ** -0.5
    blk = ATTN_BLOCK

    def body(q1_ref, q2_ref, q3_ref, k_ref, v_ref, bias_ref, o_ref, lse_ref, acc, m_s, l_s):
        for g, q_ref in enumerate((q1_ref, q2_ref, q3_ref)):
            r = DILATED[g][1]
            first = _first_block_neg(s, r)
            q = _blocks(_perm_load(q_ref, r).astype(BF16))
            k = _blocks(_perm_load(k_ref, r).astype(BF16))
            v = _blocks(_perm_load(v_ref, r).astype(BF16))
            s_cur = _bdot_nt(q, k) * scale + bias_ref[g, :, blk:]
            s_prev = _bdot_nt(q, _prev_blocks(k)) * scale + bias_ref[g, :, :blk] + first
            m = jnp.max(jnp.maximum(s_cur, s_prev), axis=-1, keepdims=True)
            p_cur = jnp.exp(s_cur - m)
            p_prev = jnp.exp(s_prev - m)
            l = jnp.sum(p_cur + p_prev, axis=-1, keepdims=True)
            o = _bdot(p_cur.astype(BF16), v) + _bdot(p_prev.astype(BF16), _prev_blocks(v))
            _perm_store(acc, g, o.reshape(s, HEAD_DIM), r)
            _perm_store(m_s, g, m.reshape(s, 1), r)
            _perm_store(l_s, g, l.reshape(s, 1), r)
        m_all = jnp.maximum(jnp.maximum(m_s[0], m_s[1]), m_s[2])
        w = [jnp.exp(m_s[g] - m_all) for g in range(N_GROUPS)]
        l = w[0] * l_s[0] + w[1] * l_s[1] + w[2] * l_s[2]
        o_ref[...] = (w[0] * acc[0] + w[1] * acc[1] + w[2] * acc[2]) / l
        lse_ref[...] = m_all + jnp.log(l)

    def col(j):
        return pl.BlockSpec((None, s, HEAD_DIM), lambda bi, hi, j=j: (bi, 0, j * h + hi))

    r_ins, r_in_specs, r_outs, r_out_specs, r_sems = _ride_args(ride)
    outs = pl.pallas_call(
        _riding(body, 6, 2, 3, ride, 2), name=name, grid=(b, h),
        in_specs=[col(0), col(1), col(2), col(3), col(4),
                  pl.BlockSpec((N_GROUPS, None, blk, 2 * blk), lambda bi, hi: (0, hi, 0, 0))] + r_in_specs,
        out_specs=[pl.BlockSpec((None, s, HEAD_DIM), lambda bi, hi: (bi, 0, hi)),
                   pl.BlockSpec((None, None, s, 1), lambda bi, hi: (bi, hi, 0, 0))] + r_out_specs,
        out_shape=[jax.ShapeDtypeStruct((b, s, h * HEAD_DIM), F32), jax.ShapeDtypeStruct((b, h, s, 1), F32)] + r_outs,
        scratch_shapes=[pltpu.VMEM((N_GROUPS, s, HEAD_DIM), F32), pltpu.VMEM((N_GROUPS, s, 1), F32),
                        pltpu.VMEM((N_GROUPS, s, 1), F32)] + r_sems,
        compiler_params=_params(("arbitrary", "arbitrary")),
    )(qkv, qkv, qkv, qkv, qkv, biasm, *r_ins)
    return outs[0], outs[1], _ride_results(ride, outs[2:])


def attn_bwd(qkv, biasm, o, lse, do, n_heads, name, ride=()):
    b, s, _ = qkv.shape
    h = n_heads
    scale = HEAD_DIM ** -0.5
    blk = ATTN_BLOCK

    def body(q1_ref, q2_ref, q3_ref, k_ref, v_ref, bias_ref, o_ref, lse_ref, do_ref,
             dq1_ref, dq2_ref, dq3_ref, dk_ref, dv_ref, ds_ref, dq_acc, kv_acc, delta):
        delta[...] = jnp.sum(do_ref[...] * o_ref[...], axis=-1, keepdims=True)
        kv_acc[...] = jnp.zeros(kv_acc.shape, F32)
        for g, q_ref in enumerate((q1_ref, q2_ref, q3_ref)):
            r = DILATED[g][1]
            first = _first_block_neg(s, r)
            q = _blocks(_perm_load(q_ref, r).astype(BF16))
            k = _blocks(_perm_load(k_ref, r).astype(BF16))
            v = _blocks(_perm_load(v_ref, r).astype(BF16))
            dob = _blocks(_perm_load(do_ref, r).astype(BF16))
            lse_b = _blocks(_perm_load(lse_ref, r))
            dl_b = _blocks(_perm_load(delta, r))
            k_prev, v_prev = _prev_blocks(k), _prev_blocks(v)
            p_cur = jnp.exp(_bdot_nt(q, k) * scale + bias_ref[g, :, blk:] - lse_b)
            p_prev = jnp.exp(_bdot_nt(q, k_prev) * scale + bias_ref[g, :, :blk] + first - lse_b)
            ds_cur = p_cur * (_bdot_nt(dob, v) - dl_b)
            ds_prev = p_prev * (_bdot_nt(dob, v_prev) - dl_b)
            ds_ref[g, :, blk:] = jnp.sum(ds_cur, axis=0)
            ds_ref[g, :, :blk] = jnp.sum(ds_prev, axis=0)
            ds_cur_b, ds_prev_b = ds_cur.astype(BF16), ds_prev.astype(BF16)
            dq = (_bdot(ds_cur_b, k) + _bdot(ds_prev_b, k_prev)) * scale
            _perm_store(dq_acc, g, dq.reshape(s, HEAD_DIM), r)
            dk = (_bdot_tn(ds_cur_b, q) + _next_blocks(_bdot_tn(ds_prev_b, q))) * scale
            dv = _bdot_tn(p_cur.astype(BF16), dob) + _next_blocks(_bdot_tn(p_prev.astype(BF16), dob))
            _perm_store(kv_acc, 0, dk.reshape(s, HEAD_DIM), r, add=True)
            _perm_store(kv_acc, 1, dv.reshape(s, HEAD_DIM), r, add=True)
        for g, out_ref in enumerate((dq1_ref, dq2_ref, dq3_ref)):
            out_ref[...] = dq_acc[g].astype(out_ref.dtype)
        dk_ref[...] = kv_acc[0].astype(dk_ref.dtype)
        dv_ref[...] = kv_acc[1].astype(dv_ref.dtype)

    def col(j):
        return pl.BlockSpec((None, s, HEAD_DIM), lambda bi, hi, j=j: (bi, 0, j * h + hi))

    head = pl.BlockSpec((None, s, HEAD_DIM), lambda bi, hi: (bi, 0, hi))
    sd = jax.ShapeDtypeStruct
    r_ins, r_in_specs, r_outs, r_out_specs, r_sems = _ride_args(ride)
    outs = pl.pallas_call(
        _riding(body, 9, 6, 3, ride, 2), name=name, grid=(b, h),
        in_specs=[col(0), col(1), col(2), col(3), col(4),
                  pl.BlockSpec((N_GROUPS, None, blk, 2 * blk), lambda bi, hi: (0, hi, 0, 0)),
                  head, pl.BlockSpec((None, None, s, 1), lambda bi, hi: (bi, hi, 0, 0)), head] + r_in_specs,
        out_specs=[head] * 5 + [pl.BlockSpec((None, None, N_GROUPS, blk, 2 * blk), lambda bi, hi: (bi, hi, 0, 0, 0))]
        + r_out_specs,
        out_shape=[sd((b, s, h * HEAD_DIM), BF16)] * 5 + [sd((b, h, N_GROUPS, blk, 2 * blk), F32)] + r_outs,
        scratch_shapes=[pltpu.VMEM((N_GROUPS, s, HEAD_DIM), F32), pltpu.VMEM((2, s, HEAD_DIM), F32),
                        pltpu.VMEM((s, 1), F32)] + r_sems,
        compiler_params=_params(("arbitrary", "arbitrary")),
    )(qkv, qkv, qkv, qkv, qkv, biasm, o, lse, do, *r_ins)
    return outs[:6], _ride_results(ride, outs[6:])


def bias_table(rel_rows, bucket_f, n_heads, name):
    g, blk, blk2 = bucket_f.shape
    h = n_heads

    def body(rb_ref, bk_ref, o_ref):
        bk = bk_ref[...]
        rb = rb_ref[...]
        acc = jnp.full((blk, blk2), NEG, F32)
        for bucket in range(REL_BUCKETS):
            acc = jnp.where(bk == float(bucket), rb[:, bucket:bucket + 1], acc)
        o_ref[...] = acc

    return pl.pallas_call(
        body, name=name, grid=(g, h),
        in_specs=[pl.BlockSpec((None, 1, 128), lambda gi, hi: (gi * h + hi, 0, 0)),
                  pl.BlockSpec((None, blk, blk2), lambda gi, hi: (gi, 0, 0))],
        out_specs=pl.BlockSpec((None, None, blk, blk2), lambda gi, hi: (gi, hi, 0, 0)),
        out_shape=jax.ShapeDtypeStruct((g, h, blk, blk2), F32),
        compiler_params=_params(("parallel", "parallel")),
    )(rel_rows, bucket_f)


def bias_grad(ds_sum, bucket_f, name):
    b, h, g, blk, blk2 = ds_sum.shape

    def body(ds_ref, bk_ref, o_ref):
        tot = jnp.sum(ds_ref[...], axis=0)
        bk = bk_ref[...]
        lane = lax.broadcasted_iota(jnp.int32, (1, 128), 1)
        vec = jnp.zeros((1, 128), F32)
        for bucket in range(REL_BUCKETS):
            val = jnp.sum(jnp.where(bk == float(bucket), tot, 0.0), keepdims=True)
            vec = vec + jnp.where(lane == bucket, val, 0.0)
        o_ref[...] = vec

    return pl.pallas_call(
        body, name=name, grid=(g, h),
        in_specs=[pl.BlockSpec((b, None, None, blk, blk2), lambda gi, hi: (0, hi, gi, 0, 0)),
                  pl.BlockSpec((None, blk, blk2), lambda gi, hi: (gi, 0, 0))],
        out_specs=pl.BlockSpec((None, 1, 128), lambda gi, hi: (gi * h + hi, 0, 0)),
        out_shape=jax.ShapeDtypeStruct((g * h, 1, 128), F32),
        compiler_params=_params(("parallel", "parallel")),
    )(ds_sum, bucket_f)


def _chip_peers():
    x, y, c = lax.axis_index("x"), lax.axis_index("y"), lax.axis_index("c")
    me = 2 * x + y
    peers = [(1 - x, y, c), (x, 1 - y, c), (1 - x, 1 - y, c)]
    peer_chip = [2 * (1 - x) + y, 2 * x + (1 - y), 2 * (1 - x) + (1 - y)]
    return me, peers, peer_chip


def _any_specs(n):
    return [pl.BlockSpec(memory_space=pl.ANY)] * n


_MID_NUM, _MID_DEN = 3, 4


class _Exchange:
    def start(self, ins, outs, sems):
        local, sends, _ = self._copies(ins, outs, sems)
        for cp in local + sends:
            cp.start()

    def mid(self, ins, outs, sems):
        pass

    def wait(self, ins, outs, sems):
        local, sends, recvs = self._copies(ins, outs, sems)
        for cp in recvs():
            cp.wait_recv()
        for cp in sends:
            cp.wait_send()
        for cp in local:
            cp.wait()


class _Gather(_Exchange):
    HALF_ROWS = 16

    def __init__(self, arrays):
        n = len(arrays)
        self.ins = list(arrays)
        self.split = [a.shape[0] % (2 * self.HALF_ROWS) == 0 for a in arrays]
        self.out_shape = [jax.ShapeDtypeStruct((N_CHIPS,) + a.shape, a.dtype) for a in arrays]
        dma = pltpu.SemaphoreType.DMA
        self.sems = [dma((3 * n,)), dma((3 * n,)), dma((n,)), dma((3 * n,)), dma((3 * n,))]

    def _half(self, i, ref, sibling=False):
        if not self.split[i]:
            return ref
        half = self.ins[i].shape[0] // 2
        c = lax.axis_index("c")
        c = 1 - c if sibling else c
        return ref.at[pl.ds(pl.multiple_of(c * half, self.HALF_ROWS), half)]

    def _plan(self, ins, outs, sems):
        send1, recv1, local_sems, send2, recv2 = sems
        me, peers, peer_chip = _chip_peers()
        x, y, c = lax.axis_index("x"), lax.axis_index("y"), lax.axis_index("c")
        n = len(ins)
        pairs = [(i, k) for i in range(n) for k in range(3)]

        def fetch(i, k, slot):
            return pltpu.make_async_remote_copy(src_ref=self._half(i, ins[i]), dst_ref=self._half(i, outs[i].at[slot]),
                                                send_sem=send1.at[3 * i + k], recv_sem=recv1.at[3 * i + k],
                                                device_id=peers[k], device_id_type=MESH)

        def share(i, k, sibling):
            part = self._half(i, outs[i].at[peer_chip[k]], sibling)
            return pltpu.make_async_remote_copy(src_ref=part, dst_ref=part, send_sem=send2.at[3 * i + k],
                                                recv_sem=recv2.at[3 * i + k], device_id=(x, y, 1 - c),
                                                device_id_type=MESH)

        split_pairs = [(i, k) for i, k in pairs if self.split[i]]
        return dict(
            local=lambda: [pltpu.make_async_copy(ins[i], outs[i].at[me], local_sems.at[i]) for i in range(n)],
            fetch_out=lambda: [fetch(i, k, me) for i, k in pairs],
            fetch_in=lambda: [fetch(i, k, peer_chip[k]) for i, k in pairs],
            share_out=lambda: [share(i, k, False) for i, k in split_pairs],
            share_in=lambda: [share(i, k, True) for i, k in split_pairs])

    def start(self, ins, outs, sems):
        plan = self._plan(ins, outs, sems)
        for cp in plan["local"]() + plan["fetch_out"]():
            cp.start()

    def mid(self, ins, outs, sems):
        plan = self._plan(ins, outs, sems)
        for cp in plan["fetch_in"]():
            cp.wait_recv()
        for cp in plan["share_out"]():
            cp.start()

    def wait(self, ins, outs, sems):
        plan = self._plan(ins, outs, sems)
        for cp in plan["share_in"]():
            cp.wait_recv()
        for cp in plan["fetch_out"]() + plan["share_out"]():
            cp.wait_send()
        for cp in plan["local"]():
            cp.wait()


class _Scatter(_Exchange):
    def __init__(self, slabs, whole=()):
        self.n_slabs = len(slabs)
        self.ins = list(slabs) + list(whole)
        n = len(self.ins)
        self.out_shape = [jax.ShapeDtypeStruct(a.shape, a.dtype) for a in slabs] \
            + [jax.ShapeDtypeStruct((N_CHIPS,) + a.shape, a.dtype) for a in whole]
        self.sems = [pltpu.SemaphoreType.DMA((3 * n,)), pltpu.SemaphoreType.DMA((3 * n,)), pltpu.SemaphoreType.DMA((n,))]

    def _copies(self, ins, outs, sems):
        send_sems, recv_sems, local_sems = sems
        me, peers, peer_chip = _chip_peers()
        n = len(ins)

        def src(i, chip):
            return ins[i].at[chip] if i < self.n_slabs else ins[i]

        def remote(i, k, src_chip, slot):
            return pltpu.make_async_remote_copy(src_ref=src(i, src_chip), dst_ref=outs[i].at[slot],
                                                send_sem=send_sems.at[3 * i + k], recv_sem=recv_sems.at[3 * i + k],
                                                device_id=peers[k], device_id_type=MESH)

        local = [pltpu.make_async_copy(src(i, me), outs[i].at[me], local_sems.at[i]) for i in range(n)]
        sends = [remote(i, k, peer_chip[k], me) for i in range(n) for k in range(3)]
        return local, sends, lambda: [remote(i, k, me, peer_chip[k]) for i in range(n) for k in range(3)]


class _Swap(_Exchange):
    def __init__(self, arrays):
        n = len(arrays)
        self.ins = list(arrays)
        self.out_shape = [jax.ShapeDtypeStruct(a.shape, a.dtype) for a in arrays]
        self.sems = [pltpu.SemaphoreType.DMA((n,)), pltpu.SemaphoreType.DMA((n,))]

    def _copies(self, ins, outs, sems):
        send_sems, recv_sems = sems
        x, y, c = lax.axis_index("x"), lax.axis_index("y"), lax.axis_index("c")
        cps = [pltpu.make_async_remote_copy(src_ref=ins[i], dst_ref=outs[i], send_sem=send_sems.at[i],
                                            recv_sem=recv_sems.at[i], device_id=(x, y, 1 - c), device_id_type=MESH)
               for i in range(len(ins))]
        return [], cps, lambda: cps


def _riding(body, n_in, n_out, n_scratch, ride, rank):
    if not ride:
        return body
    r_in = sum(len(e.ins) for e in ride)
    r_out = sum(len(e.out_shape) for e in ride)

    def split(refs, sizes):
        out, a = [], 0
        for sz in sizes:
            out.append(refs[a:a + sz])
            a += sz
        return out

    def wrapped(*refs):
        a = 0
        parts = []
        for sz in (n_in, r_in, n_out, r_out, n_scratch):
            parts.append(refs[a:a + sz])
            a += sz
        own_in, ex_in, own_out, ex_out, own_scratch = parts
        ex_sems = refs[a:]
        ins = split(ex_in, [len(e.ins) for e in ride])
        outs = split(ex_out, [len(e.out_shape) for e in ride])
        sems = split(ex_sems, [len(e.sems) for e in ride])
        if rank:
            step, total = 0, 1
            for d in range(rank):
                step = step * pl.num_programs(d) + pl.program_id(d)
                total = total * pl.num_programs(d)

            @pl.when(step == 0)
            def _():
                for e, i, o, s in zip(ride, ins, outs, sems):
                    e.start(i, o, s)

            body(*own_in, *own_out, *own_scratch)

            @pl.when(step == (total * _MID_NUM) // _MID_DEN)
            def _():
                for e, i, o, s in zip(ride, ins, outs, sems):
                    e.mid(i, o, s)

            @pl.when(step == total - 1)
            def _():
                for e, i, o, s in zip(ride, ins, outs, sems):
                    e.wait(i, o, s)
        else:
            for phase in ("start", "mid", "wait"):
                for e, i, o, s in zip(ride, ins, outs, sems):
                    getattr(e, phase)(i, o, s)

    return wrapped


def _ride_args(ride):
    ins = [a for e in ride for a in e.ins]
    outs = [s for e in ride for s in e.out_shape]
    sems = [s for e in ride for s in e.sems]
    return ins, _any_specs(len(ins)), outs, _any_specs(len(outs)), sems


def _ride_results(ride, flat):
    out, a = [], 0
    for e in ride:
        out.append(list(flat[a:a + len(e.out_shape)]))
        a += len(e.out_shape)
    return out


def exchange(ride, name):
    ins, in_specs, outs, out_specs, sems = _ride_args(ride)
    res = pl.pallas_call(
        _riding(lambda: None, 0, 0, 0, ride, 0), name=name,
        in_specs=in_specs, out_specs=out_specs, out_shape=outs, scratch_shapes=sems,
    )(*ins)
    return _ride_results(ride, res)


def _sum_slots(ref):
    acc = ref[0].astype(F32)
    for j in range(1, ref.shape[0]):
        acc = acc + ref[j].astype(F32)
    return acc


def sum_pairs(mine, other, name, tr=176):
    n, r, w = mine.shape
    tr = _tile(r, tr)

    def body(a_ref, b_ref, o_ref):
        o_ref[...] = _sum_slots(a_ref) + _sum_slots(b_ref)

    spec = pl.BlockSpec((n, tr, w), lambda i: (0, i, 0))
    return pl.pallas_call(
        body, name=name, grid=(r // tr,),
        in_specs=[spec, spec], out_specs=_rows(tr, w),
        out_shape=jax.ShapeDtypeStruct((r, w), F32),
        compiler_params=_params(("parallel",)),
    )(mine, other)


def adamw(w, m, v, gs, name, tr=256):
    r, c = w.shape
    tr = r if r % 8 else _tile(r, tr)
    c1 = 1.0 - ADAM_B1 ** ADAM_STEP
    c2 = 1.0 - ADAM_B2 ** ADAM_STEP
    ng = len(gs)

    def body(w_ref, m_ref, v_ref, *refs):
        g_refs, (g_ref, d_ref, nm_ref, nv_ref) = refs[:ng], refs[ng:]
        g = g_refs[0][...] if ng == 1 else _sum_slots(g_refs[0]) + _sum_slots(g_refs[1])
        nm = ADAM_B1 * m_ref[...] + (1.0 - ADAM_B1) * g
        nv = ADAM_B2 * v_ref[...] + (1.0 - ADAM_B2) * (g * g)
        g_ref[...] = g
        nm_ref[...] = nm
        nv_ref[...] = nv
        d_ref[...] = (-ADAM_LR) * ((nm / c1) / (jnp.sqrt(nv / c2) + ADAM_EPS) + ADAM_WD * w_ref[...])

    spec = _rows(tr, c)
    gspec = spec if ng == 1 else pl.BlockSpec((N_CHIPS, tr, c), lambda i: (0, i, 0))
    return pl.pallas_call(
        body, name=name, grid=(r // tr,),
        in_specs=[spec] * 3 + [gspec] * ng, out_specs=[spec] * 4,
        out_shape=[jax.ShapeDtypeStruct((r, c), F32)] * 4,
        compiler_params=_params(("parallel",)),
    )(w, m, v, *gs)


_PARAMS = (
    ("rel_bias", None), ("norm_mix_pre", None), ("norm_mix_post", None), ("w_in", 1), ("conv_rnn_w", 1),
    ("conv_rnn_b", None), ("w_rg_a", None), ("b_rg_a", None), ("w_rg_x", None), ("b_rg_x", None),
    ("lru_lambda", None), ("w_branch_rnn", 0), ("w_branch_att", 1), ("w_out", 0), ("norm_ffn_pre", None),
    ("norm_ffn_post", None), ("w_ffn_gate", 1), ("w_ffn_up", 1), ("conv_ffn_w", 1), ("conv_ffn_b", None),
    ("w_ffn_down", 0),
)
_SMALL = 65536


def _as2d(a):
    a = a[0] if a.shape[0] == 1 and a.ndim >= 3 else a
    return a.reshape(-1, a.shape[-1]) if a.ndim == 3 else a


def _pack(pieces, dtype):
    flat = jnp.concatenate([p.astype(dtype).reshape(-1) for p in pieces])
    unit = PACK_W * PACK_ROWS
    pad = (-flat.shape[0]) % unit
    flat = jnp.pad(flat, (0, pad))
    return flat.reshape(-1, PACK_W)


def _unpack(buf, shapes):
    flat = buf.reshape(-1)
    out, off = [], 0
    for shp in shapes:
        n = int(np.prod(shp))
        out.append(flat[off:off + n].reshape(shp))
        off += n
    return out


def _join(slots, ax):
    if ax == 0:
        return slots.reshape(-1, slots.shape[-1])
    return jnp.transpose(slots, (1, 0, 2)).reshape(slots.shape[1], -1)


def _cut(full, ax):
    if ax == 0:
        return full.reshape(N_CHIPS, -1, full.shape[-1])
    return jnp.transpose(full.reshape(full.shape[0], N_CHIPS, -1), (1, 0, 2))


def kernel(x, rel_bias, norm_mix_pre, norm_mix_post, w_in, conv_rnn_w, conv_rnn_b, w_rg_a, b_rg_a, w_rg_x, b_rg_x, lru_lambda, w_branch_rnn, w_branch_att, w_out, norm_ffn_pre, norm_ffn_post, w_ffn_gate, w_ffn_up, conv_ffn_w, conv_ffn_b, w_ffn_down, loss_target, m_rel_bias, m_norm_mix_pre, m_norm_mix_post, m_w_in, m_conv_rnn_w, m_conv_rnn_b, m_w_rg_a, m_b_rg_a, m_w_rg_x, m_b_rg_x, m_lru_lambda, m_w_branch_rnn, m_w_branch_att, m_w_out, m_norm_ffn_pre, m_norm_ffn_post, m_w_ffn_gate, m_w_ffn_up, m_conv_ffn_w, m_conv_ffn_b, m_w_ffn_down, v_rel_bias, v_norm_mix_pre, v_norm_mix_post, v_w_in, v_conv_rnn_w, v_conv_rnn_b, v_w_rg_a, v_b_rg_a, v_w_rg_x, v_b_rg_x, v_lru_lambda, v_w_branch_rnn, v_w_branch_att, v_w_out, v_norm_ffn_pre, v_norm_ffn_post, v_w_ffn_gate, v_w_ffn_up, v_conv_ffn_w, v_conv_ffn_b, v_w_ffn_down):
    args = dict(locals())
    names = [n for n, _ in _PARAMS]
    axis = dict(_PARAMS)
    w_loc = {n: args[n] for n in names}
    m_loc = {n: args["m_" + n] for n in names}
    v_loc = {n: args["v_" + n] for n in names}
    sharded = [n for n in names if axis[n] is not None]
    replicated = [n for n in names if axis[n] is None]

    big = [n for n in sharded if w_loc[n].size >= _SMALL]
    small_sharded = [n for n in sharded if n not in big]
    small = replicated + small_sharded

    first = ["w_in"] + small_sharded
    srcs = [_as2d(w_loc[n]).astype(BF16) if n in big else _as2d(w_loc[n]) for n in first]
    (gathered,) = exchange([_Gather(srcs)], "gather_first")
    p = {n: _join(a, axis[n]) for n, a in zip(first, gathered)}
    for n in replicated:
        p[n] = _as2d(w_loc[n])
    shards = {n: _as2d(w_loc[n]).astype(BF16) for n in big if n not in first}

    received, sibling, g_small, loss_part = _local_step(x, loss_target, p, shards)

    pack = _pack([g_small[n] for n in small], BF16)
    ((received["small"],),) = exchange([_Scatter([], [pack])], "scatter_small")
    late = [n for n in received if n not in sibling]
    (swapped,) = exchange([_Swap([received[n] for n in late])], "swap_last")
    sibling.update(zip(late, swapped))
    small_sum = sum_pairs(received["small"], sibling["small"], "sum_small")
    g_tot = dict(zip(small, _unpack(small_sum, [g_small[n].shape for n in small])))
    chip = 2 * lax.axis_index("x") + lax.axis_index("y")
    for n in small_sharded:
        size = g_tot[n].shape[axis[n]] // N_CHIPS
        g_tot[n] = lax.dynamic_slice_in_dim(g_tot[n], chip * size, size, axis=axis[n])

    out_g, out_d, out_m, out_v = {}, {}, {}, {}
    for i, n in enumerate(names):
        shp = w_loc[n].shape
        gs = (received[n], sibling[n]) if n in big else (g_tot[n],)
        g, d, nm, nv = adamw(_as2d(w_loc[n]), _as2d(m_loc[n]), _as2d(v_loc[n]), gs, "adamw_" + n)
        out_g[n], out_d[n], out_m[n], out_v[n] = (t.reshape(shp) for t in (g, d, nm, nv))

    d_model = x.shape[-1]
    loss = lax.psum(0.5 * jnp.sum(loss_part) / d_model, ("x", "y", "c"))
    grad_x = g_small["x"]
    return (loss, grad_x, *[out_g[n] for n in names], *[out_d[n] for n in names],
            *[out_m[n] for n in names], *[out_v[n] for n in names])


def _local_step(x, target, p, shards):
    axis = dict(_PARAMS)
    b, s, d = x.shape
    t = b * s
    rnn = p["b_rg_a"].shape[1]
    ffn = p["conv_ffn_b"].shape[1]
    nbk = rnn // p["w_rg_a"].shape[1]
    hkv = (p["w_in"].shape[1] - rnn - 2 * d) // (N_GROUPS + 2)
    h = hkv // HEAD_DIM
    nq = N_GROUPS * hkv

    x2 = x.reshape(t, d)
    tgt = target.reshape(t, d)
    w_in = p["w_in"]
    in_splits = (rnn, nq + 2 * hkv, 2 * d)
    wa = p["w_rg_a"].reshape(nbk, -1, p["w_rg_a"].shape[1]).astype(BF16)
    wx = p["w_rg_x"].reshape(nbk, -1, p["w_rg_x"].shape[1]).astype(BF16)
    cw_r, cb_r = p["conv_rnn_w"], p["conv_rnn_b"]
    cw_f, cb_f = p["conv_ffn_w"], p["conv_ffn_b"]

    masks, buckets = zip(*[_band(w_, r_) for w_, r_ in DILATED])
    bucket_f = jnp.asarray(np.where(np.stack(masks), np.stack(buckets), -1).astype(np.float32))
    rel_rows = jnp.pad(p["rel_bias"].T, ((0, 0), (0, 128 - REL_BUCKETS)))[:, None, :]
    biasm = bias_table(rel_rows, bucket_f, h, "bias_table")

    early = ["w_branch_rnn", "w_branch_att", "w_out"]
    hn1, (xr, qkv, gts), (got,) = norm_mm(x2, p["norm_mix_pre"], [w_in], [in_splits], "in_proj",
                                          ride=[_Gather([shards[n] for n in early])])
    p.update({n: _join(a, axis[n]) for n, a in zip(early, got)})
    xr3 = xr.reshape(b, s, rnn)
    (y_rnn, a_rnn, xc_rnn), (got,) = rglru_fwd(xr3, cw_r, cb_r, wa, p["b_rg_a"], wx, p["b_rg_x"], p["lru_lambda"], "rglru_fwd",
                              ride=[_Gather([shards[n] for n in ("w_ffn_gate", "w_ffn_up")])])
    p.update({n: _join(a, axis[n]) for n, a in zip(("w_ffn_gate", "w_ffn_up"), got)})
    qkv3 = qkv.reshape(b, s, -1)
    o_att, lse, ((got,),) = attn_fwd(qkv3, biasm, h, "attn_fwd", ride=[_Gather([shards["w_ffn_down"]])])
    p["w_ffn_down"] = _join(got, axis["w_ffn_down"])
    merged, br, ba, mix, h1 = merge_out(y_rnn.reshape(t, rnn), o_att.reshape(t, hkv), gts, p["w_branch_rnn"],
                                        p["w_branch_att"], p["w_out"], p["norm_mix_post"], x2, "merge_out")
    hn2, gate_pre, up, act = ffn_in_act(h1, p["norm_ffn_pre"], p["w_ffn_gate"], p["w_ffn_up"], cw_f, cb_f, s, "ffn_in")

    g, gb = {}, {}
    recv, sib = {}, {}

    def rows4(a):
        return a.reshape(N_CHIPS, -1, a.shape[-1])

    dy, dff, dact, g["norm_ffn_post"], loss_part = ffn_down_loss(act, p["w_ffn_down"], p["norm_ffn_post"], h1, tgt,
                                                                  "ffn_down")
    gb["w_ffn_down"] = rows4(mm_tn(act, [dff], "ffn_down_dw"))
    (dgp, dup, dhn2, g["conv_ffn_w"], g["conv_ffn_b"]), ((recv["w_ffn_down"],),) = ffn_in_bwd(
        dact, gate_pre, up, cw_f, cb_f, p["w_ffn_gate"], p["w_ffn_up"], s, "ffn_in_bwd",
        ride=[_Scatter([gb["w_ffn_down"]])])
    gb["w_ffn_gate"] = mm_tn(hn2, [dgp], "ffn_gate_dw", col_shards=N_CHIPS)
    gb["w_ffn_up"] = mm_tn(hn2, [dup], "ffn_up_dw", col_shards=N_CHIPS)
    (dh1, dgts, dy_rnn, do_att, g["norm_ffn_pre"], g["norm_mix_post"], dw_out, dw_br,
     gb["w_branch_att"]) = mid_bwd(dhn2, h1, p["norm_ffn_pre"], dy, mix, p["norm_mix_post"], p["w_out"], gts, br, ba,
                                   p["w_branch_rnn"], p["w_branch_att"], merged, y_rnn.reshape(t, rnn),
                                   o_att.reshape(t, hkv), "mid_bwd")
    gb["w_out"], gb["w_branch_rnn"] = rows4(dw_out), rows4(dw_br)
    ffn_in = ["w_ffn_gate", "w_ffn_up"]
    (dxr, g["conv_rnn_w"], g["conv_rnn_b"], dwa, g["b_rg_a"], dwx, g["b_rg_x"], g["lru_lambda"]), (got,) = rglru_bwd(
        xr3, y_rnn, dy_rnn.reshape(b, s, rnn), a_rnn, xc_rnn, cw_r, cb_r, wa, p["b_rg_a"], wx, p["b_rg_x"], p["lru_lambda"], "rglru_bwd",
        ride=[_Scatter([gb[n] for n in ffn_in])])
    recv.update(zip(ffn_in, got))
    g["w_rg_a"] = dwa.reshape(p["w_rg_a"].shape)
    g["w_rg_x"] = dwx.reshape(p["w_rg_x"].shape)
    mid = ["w_out", "w_branch_rnn", "w_branch_att"]
    early_recv = ["w_ffn_down"] + ffn_in
    (dq1, dq2, dq3, dk, dv, ds_sum), (got, swapped) = attn_bwd(
        qkv3, biasm, o_att, lse, do_att.reshape(b, s, hkv), h, "attn_bwd",
        ride=[_Scatter([gb[n] for n in mid]), _Swap([recv[n] for n in early_recv])])
    recv.update(zip(mid, got))
    sib.update(zip(early_recv, swapped))
    rows = bias_grad(ds_sum, bucket_f, "bias_grad")
    g["rel_bias"] = rows[:, 0, :REL_BUCKETS].T
    dproj = [dxr.reshape(t, rnn)] + [a.reshape(t, hkv) for a in (dq1, dq2, dq3, dk, dv)] + [dgts]
    dw_a = mm_tn(hn1, dproj[:4], "in_proj_dw_a")[0]
    dw_b = mm_tn(hn1, dproj[4:], "in_proj_dw_b")[0]
    gb["w_in"] = _cut(jnp.concatenate([dw_a, dw_b], axis=1), 1)
    dx, g["norm_mix_pre"], ((recv["w_in"],), got) = mm_nt(
        [(dproj, w_in)], "in_proj_dx", norm=(x2, p["norm_mix_pre"], dh1),
        ride=[_Scatter([gb["w_in"]]), _Swap([recv[n] for n in mid])])
    sib.update(zip(mid, got))
    g["x"] = dx.reshape(b, s, d)
    return recv, sib, g, loss_part
```

```python
import functools
import math

import numpy as np
import jax
import jax.numpy as jnp
from jax import lax
from jax.experimental import pallas as pl
from jax.experimental.pallas import tpu as pltpu

F32 = jnp.float32
BF16 = jnp.bfloat16

EPS = 1e-6
HEAD_DIM = 128
ATTN_BLOCK = 128
DILATED = ((128, 1), (512, 4), (2048, 16))
N_GROUPS = len(DILATED)
REL_BUCKETS = 32
REL_MAX_DIST = 2048
LRU_C = 8.0
NEG = -1e30

ADAM_LR = 0.001
ADAM_B1 = 0.9
ADAM_B2 = 0.999
ADAM_EPS = 1e-08
ADAM_WD = 0.01
ADAM_STEP = 10

N_CHIPS = 4
PACK_W = 1024
PACK_ROWS = 16
VMEM_LIMIT = 56 * 1024 * 1024
MESH = pl.DeviceIdType.MESH


def _params(sem=None):
    return pltpu.CompilerParams(dimension_semantics=sem, vmem_limit_bytes=VMEM_LIMIT)


def _dot(a, b):
    return jnp.dot(a, b, preferred_element_type=F32)


def _dot_nt(a, b):
    return lax.dot_general(a, b, (((1,), (1,)), ((), ())), preferred_element_type=F32)


def _dot_tn(a, b):
    return lax.dot_general(a, b, (((0,), (0,)), ((), ())), preferred_element_type=F32)


def _sig(x):
    return 0.5 * jnp.tanh(0.5 * x) + 0.5


def _rows(tm, w):
    return pl.BlockSpec((tm, w), lambda i: (i, 0))


def _whole(shape):
    nd = len(shape)
    return pl.BlockSpec(tuple(shape), lambda *_: (0,) * nd)


def _resident(shape):
    nd = len(shape)
    return pl.BlockSpec(tuple(shape), lambda *_: (0,) * nd, pipeline_mode=pl.Buffered(1))


def _tile(t, want):
    while t % want:
        want //= 2
    return want


def norm_mm(x, g, ws, splits, name, ride=(), tm=512):
    t, d = x.shape
    tm = _tile(t, tm)
    nw = len(ws)
    widths = [n for sp in splits for n in sp]

    def body(x_ref, g_ref, *refs):
        w_refs, hn_ref, o_refs = refs[:nw], refs[nw], refs[nw + 1:]
        xv = x_ref[...]
        inv = lax.rsqrt(jnp.mean(xv * xv, axis=-1, keepdims=True) + EPS)
        hn = (xv * inv * g_ref[...]).astype(BF16)
        hn_ref[...] = hn
        o = 0
        for w_ref, sp in zip(w_refs, splits):
            off = 0
            for n in sp:
                o_refs[o][...] = _dot(hn, w_ref[:, off:off + n])
                off += n
                o += 1

    r_ins, r_in_specs, r_outs, r_out_specs, r_sems = _ride_args(ride)
    n_out = 1 + len(widths)
    outs = pl.pallas_call(
        _riding(body, 2 + nw, n_out, 0, ride, 1), name=name, grid=(t // tm,),
        in_specs=[_rows(tm, d), _whole(g.shape)] + [_resident(w.shape) for w in ws] + r_in_specs,
        out_specs=[_rows(tm, d)] + [_rows(tm, n) for n in widths] + r_out_specs,
        out_shape=[jax.ShapeDtypeStruct((t, d), BF16)] + [jax.ShapeDtypeStruct((t, n), F32) for n in widths] + r_outs,
        scratch_shapes=r_sems,
        compiler_params=_params(("arbitrary",)),
    )(x, g, *ws, *r_ins)
    return outs[0], outs[1:n_out], _ride_results(ride, outs[n_out:])


def mm_nt(groups, name, ride=(), norm=None, tm=512):
    dys_all = [dy for dys, _ in groups for dy in dys]
    ws = [w for _, w in groups]
    t = dys_all[0].shape[0]
    k = ws[0].shape[0]
    tm = _tile(t, tm)
    n = len(dys_all)
    extra = list(norm) if norm else []

    def body(*refs):
        dy_refs, w_refs = refs[:n], refs[n:n + len(ws)]
        rest = refs[n + len(ws):]
        acc = None
        i = 0
        for (dys, _), w_ref in zip(groups, w_refs):
            off = 0
            for dy in dys:
                width = dy.shape[1]
                part = _dot_nt(dy_refs[i][...].astype(BF16), w_ref[:, off:off + width])
                acc = part if acc is None else acc + part
                off += width
                i += 1
        if norm:
            u_ref, g_ref, add_ref, o_ref, dg_ref = rest

            @pl.when(pl.program_id(0) == 0)
            def _():
                dg_ref[...] = jnp.zeros(dg_ref.shape, F32)

            du, dg_rows = _rms_bwd(acc, u_ref[...], g_ref[...])
            o_ref[...] = du + add_ref[...]
            dg_ref[...] += jnp.sum(dg_rows, axis=0, keepdims=True)
        else:
            rest[0][...] = acc

    n_out = 2 if norm else 1
    r_ins, r_in_specs, r_outs, r_out_specs, r_sems = _ride_args(ride)
    outs = pl.pallas_call(
        _riding(body, n + len(ws) + len(extra), n_out, 0, ride, 1), name=name, grid=(t // tm,),
        in_specs=[_rows(tm, dy.shape[1]) for dy in dys_all] + [_resident(w.shape) for w in ws]
        + ([_rows(tm, k), _whole((1, k)), _rows(tm, k)] if norm else []) + r_in_specs,
        out_specs=[_rows(tm, k)] + ([_whole((1, k))] if norm else []) + r_out_specs,
        out_shape=[jax.ShapeDtypeStruct((t, k), F32)] + ([jax.ShapeDtypeStruct((1, k), F32)] if norm else []) + r_outs,
        scratch_shapes=r_sems,
        compiler_params=_params(("arbitrary",)),
    )(*dys_all, *ws, *extra, *r_ins)
    return tuple(outs[:n_out]) + (_ride_results(ride, outs[n_out:]),)


def mm_tn(a, dys, name, col_shards=1, tm=1024):
    t, k = a.shape
    tm = _tile(t, tm)
    n = len(dys)
    ntot = sum(dy.shape[1] for dy in dys)
    wsh = ntot // col_shards

    def body(a_ref, *refs):
        dy_refs, o_ref, acc = refs[:n], refs[n], refs[n + 1]

        @pl.when(pl.program_id(0) == 0)
        def _():
            acc[...] = jnp.zeros(acc.shape, F32)

        av = a_ref[...].astype(BF16)
        off = 0
        for dy_ref in dy_refs:
            width = dy_ref.shape[1]
            acc[:, off:off + width] += _dot_tn(av, dy_ref[...].astype(BF16))
            off += width

        @pl.when(pl.program_id(0) == pl.num_programs(0) - 1)
        def _():
            for j in range(col_shards):
                o_ref[j] = acc[:, j * wsh:(j + 1) * wsh].astype(o_ref.dtype)

    return pl.pallas_call(
        body, name=name, grid=(t // tm,),
        in_specs=[_rows(tm, k)] + [_rows(tm, dy.shape[1]) for dy in dys],
        out_specs=_whole((col_shards, k, wsh)),
        out_shape=jax.ShapeDtypeStruct((col_shards, k, wsh), BF16),
        scratch_shapes=[pltpu.VMEM((k, ntot), F32)],
        compiler_params=_params(("arbitrary",)),
    )(a, *dys)


def _rms_bwd(dz, u, g):
    d = u.shape[-1]
    inv = lax.rsqrt(jnp.mean(u * u, axis=-1, keepdims=True) + EPS)
    dzg = dz * g
    proj = jnp.sum(dzg * u, axis=-1, keepdims=True) * (1.0 / d)
    du = inv * (dzg - u * (inv * inv) * proj)
    dg_rows = dz * u * inv
    return du, dg_rows


def ffn_down_loss(act, wd, g, h1, target, name, tm=512):
    t, f = act.shape
    d = wd.shape[1]
    tm = _tile(t, tm)

    def body(a_ref, w_ref, g_ref, h_ref, t_ref, dy_ref, dff_ref, dact_ref, dg_ref, loss_ref):
        @pl.when(pl.program_id(0) == 0)
        def _():
            dg_ref[...] = jnp.zeros(dg_ref.shape, F32)
            loss_ref[...] = jnp.zeros(loss_ref.shape, F32)

        wv = w_ref[...]
        gv = g_ref[...]
        ff = _dot(a_ref[...], wv)
        inv = lax.rsqrt(jnp.mean(ff * ff, axis=-1, keepdims=True) + EPS)
        err = h_ref[...] + ff * inv * gv - t_ref[...]
        loss_ref[...] += jnp.sum(err * err, axis=0, keepdims=True)
        dy = err * (1.0 / d)
        dy_ref[...] = dy
        du, dg_rows = _rms_bwd(dy, ff, gv)
        dff = du.astype(BF16)
        dff_ref[...] = dff
        dg_ref[...] += jnp.sum(dg_rows, axis=0, keepdims=True)
        dact_ref[...] = _dot_nt(dff, wv)

    return pl.pallas_call(
        body, name=name, grid=(t // tm,),
        in_specs=[_rows(tm, f), _resident(wd.shape), _whole(g.shape), _rows(tm, d), _rows(tm, d)],
        out_specs=[_rows(tm, d), _rows(tm, d), _rows(tm, f), _whole((1, d)), _whole((1, d))],
        out_shape=[jax.ShapeDtypeStruct((t, d), F32), jax.ShapeDtypeStruct((t, d), BF16),
                   jax.ShapeDtypeStruct((t, f), F32), jax.ShapeDtypeStruct((1, d), F32),
                   jax.ShapeDtypeStruct((1, d), F32)],
        compiler_params=_params(("arbitrary",)),
    )(act, wd, g, h1, target)


def merge_out(y_rnn, o_att, gts, w_br, w_ba, w_out, g, x, name, tm=256):
    t = y_rnn.shape[0]
    d = w_br.shape[1]
    tm = _tile(t, tm)

    def body(y_ref, o_ref, g_ref, wbr_ref, wba_ref, wo_ref, gn_ref, x_ref, m_ref, br_ref, ba_ref, mix_ref, h_ref):
        br = _dot(y_ref[...].astype(BF16), wbr_ref[...])
        ba = _dot(o_ref[...].astype(BF16), wba_ref[...])
        gv = g_ref[...]
        merged = (_sig(gv[:, :d]) * br + _sig(gv[:, d:]) * ba).astype(BF16)
        m_ref[...] = merged
        br_ref[...] = br
        ba_ref[...] = ba
        mix = _dot(merged, wo_ref[...])
        mix_ref[...] = mix
        inv = lax.rsqrt(jnp.mean(mix * mix, axis=-1, keepdims=True) + EPS)
        h_ref[...] = x_ref[...] + mix * inv * gn_ref[...]

    sd = jax.ShapeDtypeStruct
    return pl.pallas_call(
        body, name=name, grid=(t // tm,),
        in_specs=[_rows(tm, y_rnn.shape[1]), _rows(tm, o_att.shape[1]), _rows(tm, 2 * d),
                  _whole(w_br.shape), _whole(w_ba.shape), _whole(w_out.shape), _whole(g.shape), _rows(tm, d)],
        out_specs=[_rows(tm, d)] * 5,
        out_shape=[sd((t, d), BF16), sd((t, d), F32), sd((t, d), F32), sd((t, d), F32), sd((t, d), F32)],
        compiler_params=_params(("parallel",)),
    )(y_rnn, o_att, gts, w_br, w_ba, w_out, g, x)


def mid_bwd(dhn2, h1, g_ffn, dy, mix, g_mix, w_out, gts, br, ba, w_br, w_ba, merged, y_rnn, o_att, name, tm=256):
    t, d = h1.shape
    tm = _tile(t, tm)
    rnn, hkv = w_br.shape[0], w_ba.shape[0]
    wsh = d // N_CHIPS

    def body(dhn_ref, h_ref, gf_ref, dy_ref, mix_ref, gm_ref, wo_ref, g_ref, br_ref, ba_ref, wbr_ref, wba_ref,
             m_ref, y_ref, o_ref, dh_ref, dg_ref, dyr_ref, doa_ref, dgf_ref, dgm_ref, dwo_ref, dwbr_ref, dwba_ref,
             acc_o, acc_br, acc_ba):
        @pl.when(pl.program_id(0) == 0)
        def _():
            dgf_ref[...] = jnp.zeros(dgf_ref.shape, F32)
            dgm_ref[...] = jnp.zeros(dgm_ref.shape, F32)
            acc_o[...] = jnp.zeros(acc_o.shape, F32)
            acc_br[...] = jnp.zeros(acc_br.shape, F32)
            acc_ba[...] = jnp.zeros(acc_ba.shape, F32)

        du, rows_f = _rms_bwd(dhn_ref[...], h_ref[...], gf_ref[...])
        dh1 = du + dy_ref[...]
        dh_ref[...] = dh1
        dgf_ref[...] += jnp.sum(rows_f, axis=0, keepdims=True)
        dmx, rows_m = _rms_bwd(dh1, mix_ref[...], gm_ref[...])
        dmix = dmx.astype(BF16)
        acc_o[...] += _dot_tn(m_ref[...], dmix)
        dgm_ref[...] += jnp.sum(rows_m, axis=0, keepdims=True)
        dm = _dot_nt(dmix, wo_ref[...])
        gv = g_ref[...]
        sr = _sig(gv[:, :d])
        sa = _sig(gv[:, d:])
        dbr = (dm * sr).astype(BF16)
        dba = (dm * sa).astype(BF16)
        acc_br[...] += _dot_tn(y_ref[...].astype(BF16), dbr)
        acc_ba[...] += _dot_tn(o_ref[...].astype(BF16), dba)
        dg_ref[:, :d] = (dm * br_ref[...] * sr * (1.0 - sr)).astype(BF16)
        dg_ref[:, d:] = (dm * ba_ref[...] * sa * (1.0 - sa)).astype(BF16)
        dyr_ref[...] = _dot_nt(dbr, wbr_ref[...])
        doa_ref[...] = _dot_nt(dba, wba_ref[...])

        @pl.when(pl.program_id(0) == pl.num_programs(0) - 1)
        def _():
            dwo_ref[...] = acc_o[...].astype(BF16)
            dwbr_ref[...] = acc_br[...].astype(BF16)
            for j in range(N_CHIPS):
                dwba_ref[j] = acc_ba[:, j * wsh:(j + 1) * wsh].astype(BF16)

    sd = jax.ShapeDtypeStruct
    row, vec = _rows(tm, d), _whole((1, d))
    once = pl.Buffered(1)

    def resident(shape):
        return pl.BlockSpec(shape, lambda i: (0,) * len(shape), pipeline_mode=once)

    return pl.pallas_call(
        body, name=name, grid=(t // tm,),
        in_specs=[row, row, vec, row, row, vec, resident(w_out.shape), _rows(tm, 2 * d), row, row,
                  resident(w_br.shape), resident(w_ba.shape), row, _rows(tm, rnn), _rows(tm, hkv)],
        out_specs=[row, _rows(tm, 2 * d), _rows(tm, rnn), _rows(tm, hkv), vec, vec,
                   resident((d, d)), resident((rnn, d)), resident((N_CHIPS, hkv, wsh))],
        out_shape=[sd((t, d), F32), sd((t, 2 * d), BF16), sd((t, rnn), F32), sd((t, hkv), F32), sd((1, d), F32),
                   sd((1, d), F32), sd((d, d), BF16), sd((rnn, d), BF16), sd((N_CHIPS, hkv, wsh), BF16)],
        scratch_shapes=[pltpu.VMEM((d, d), F32), pltpu.VMEM((rnn, d), F32), pltpu.VMEM((hkv, d), F32)],
        compiler_params=_params(("arbitrary",)),
    )(dhn2, h1, g_ffn, dy, mix, g_mix, w_out, gts, br, ba, w_br, w_ba, merged, y_rnn, o_att)


def _shift_dn(x, d, fill, row):
    return jnp.where(row >= d, pltpu.roll(x, d, 0), fill)


def _shift_up(x, d, fill, row):
    s = x.shape[0]
    return jnp.where(row < s - d, pltpu.roll(x, s - d, 0), fill)


def _conv_fwd(x, w, b, row):
    kk = w.shape[0]
    y = b + w[kk - 1:kk, :] * x
    for j in range(1, kk):
        y = y + w[kk - 1 - j:kk - j, :] * _shift_dn(x, j, 0.0, row)
    return y


def _conv_bwd(dy, x, w, row):
    kk = w.shape[0]
    dx = w[kk - 1:kk, :] * dy
    dws = [None] * kk
    dws[kk - 1] = jnp.sum(dy * x, axis=0, keepdims=True)
    for j in range(1, kk):
        dx = dx + w[kk - 1 - j:kk - j, :] * _shift_up(dy, j, 0.0, row)
        dws[kk - 1 - j] = jnp.sum(dy * _shift_dn(x, j, 0.0, row), axis=0, keepdims=True)
    return dx, jnp.concatenate(dws, axis=0)


def _softplus(z):
    y = jnp.exp(-jnp.abs(z))
    u = 1.0 + y
    dd = u - 1.0
    log1p = jnp.where(dd == 0.0, y, jnp.log(u) * (y / jnp.where(dd == 0.0, 1.0, dd)))
    return jnp.maximum(z, 0.0) + log1p


def _lru_decay(xb, wa, ba, lam):
    r = _sig(_dot(xb, wa) + ba)
    sp = _softplus(-lam)
    la = (-LRU_C) * r * sp
    return r, sp, la, jnp.exp(la)


def _lru_gates(xc, wa, ba, wx, bx, lam):
    xb = xc.astype(BF16)
    r, sp, la, a = _lru_decay(xb, wa, ba, lam)
    i = _sig(_dot(xb, wx) + bx)
    one_m_a2 = jnp.tanh(-la) * (1.0 + a * a)
    inv_mult = lax.rsqrt(one_m_a2)
    return r, i, sp, a, one_m_a2 * inv_mult, inv_mult


def _seg_len(s):
    seg = -(-s // 8)
    return seg + (4 - seg % 8) % 8


def _scan_rows(a_pad, u_pad, out_pad, reverse):
    planes, rows8, lanes = a_pad.shape
    seg = rows8 // 8
    sub = lax.broadcasted_iota(jnp.int32, (planes, 8, lanes), 1)

    unroll = 4

    def rows(k, d):
        i = k * unroll + d
        return pl.ds((seg - 1 - i) if reverse else i, 8, stride=seg)

    def ends(k, carry):
        h, p = carry
        for d in range(unroll):
            a = a_pad[:, rows(k, d), :]
            h = a * h + u_pad[:, rows(k, d), :]
            p = a * p
        return h, p

    init = (jnp.zeros((planes, 8, lanes), F32), jnp.ones((planes, 8, lanes), F32))
    h_end, p_end = lax.fori_loop(0, seg // unroll, ends, init)
    start = jnp.zeros((planes, 8, lanes), F32)
    for _ in range(7):
        nxt = h_end + p_end * start
        if reverse:
            start = jnp.where(sub < 7, pltpu.roll(nxt, 7, 1), 0.0)
        else:
            start = jnp.where(sub >= 1, pltpu.roll(nxt, 1, 1), 0.0)

    def redo(k, h):
        for d in range(unroll):
            h = a_pad[:, rows(k, d), :] * h + u_pad[:, rows(k, d), :]
            out_pad[:, rows(k, d), :] = h
        return h

    lax.fori_loop(0, seg // unroll, redo, start)


def _lru_cols(c, rb):
    return 2 * rb if c % (2 * rb) == 0 else rb


def rglru_fwd(xr, cw, cb, wa, ba, wx, bx, lam, name, ride=()):
    b, s, c = xr.shape
    rb = wa.shape[1]
    kk = cw.shape[0]
    cols = _lru_cols(c, rb)
    nj = cols // rb
    seg = _seg_len(s)

    def body(x_ref, cw_ref, cb_ref, wa_ref, ba_ref, wx_ref, bx_ref, lam_ref, h_ref, a_ref, xc_ref, a_pad, u_pad, h_pad):
        row = lax.broadcasted_iota(jnp.int32, (s, rb), 0)
        for j in range(nj):
            cs = slice(j * rb, (j + 1) * rb)
            xc = _conv_fwd(x_ref[:, cs], cw_ref[:, cs], cb_ref[:, cs], row)
            _, i, _, a, mult, _ = _lru_gates(xc, wa_ref[j], ba_ref[:, cs], wx_ref[j], bx_ref[:, cs], lam_ref[:, cs])
            xc_ref[:, cs] = xc
            a_ref[:, cs] = a
            a_pad[j, 0:s, :] = a
            u_pad[j, 0:s, :] = mult * (i * xc)
        a_pad[:, s:, :] = jnp.ones((nj, 8 * seg - s, rb), F32)
        u_pad[:, s:, :] = jnp.zeros((nj, 8 * seg - s, rb), F32)
        _scan_rows(a_pad, u_pad, h_pad, False)
        for j in range(nj):
            h_ref[:, j * rb:(j + 1) * rb] = h_pad[j, 0:s, :]

    vec = pl.BlockSpec((1, cols), lambda bi, n: (0, n))
    seq = pl.BlockSpec((None, s, cols), lambda bi, n: (bi, 0, n))
    mat = pl.BlockSpec((nj, rb, rb), lambda bi, n: (n, 0, 0))
    r_ins, r_in_specs, r_outs, r_out_specs, r_sems = _ride_args(ride)
    outs = pl.pallas_call(
        _riding(body, 8, 3, 3, ride, 2), name=name, grid=(b, c // cols),
        in_specs=[seq, pl.BlockSpec((kk, cols), lambda bi, n: (0, n)), vec, mat, vec, mat, vec, vec] + r_in_specs,
        out_specs=[seq] * 3 + r_out_specs,
        out_shape=[jax.ShapeDtypeStruct((b, s, c), F32)] * 3 + r_outs,
        scratch_shapes=[pltpu.VMEM((nj, 8 * seg, rb), F32)] * 3 + r_sems,
        compiler_params=_params(("arbitrary", "arbitrary")),
    )(xr, cw, cb, wa, ba, wx, bx, lam, *r_ins)
    return outs[:3], _ride_results(ride, outs[3:])


def rglru_bwd(xr, h, dh, a_fwd, xc_fwd, cw, cb, wa, ba, wx, bx, lam, name, ride=()):
    b, s, c = xr.shape
    nb, rb = wa.shape[0], wa.shape[1]
    kk = cw.shape[0]
    cols = _lru_cols(c, rb)
    nj = cols // rb
    seg = _seg_len(s)

    def body(x_ref, h_ref, dh_ref, a_ref, xc_ref, cw_ref, cb_ref, wa_ref, ba_ref, wx_ref, bx_ref, lam_ref,
             dx_ref, dcw_ref, dcb_ref, dwa_ref, dba_ref, dwx_ref, dbx_ref, dlam_ref, b_pad, g_pad, l_pad):
        @pl.when(pl.program_id(1) == 0)
        def _():
            for ref in (dcw_ref, dcb_ref, dwa_ref, dba_ref, dwx_ref, dbx_ref, dlam_ref):
                ref[...] = jnp.zeros(ref.shape, F32)

        row = lax.broadcasted_iota(jnp.int32, (s, rb), 0)

        for j in range(nj):
            b_pad[j, 0:s, :] = _shift_up(a_ref[:, j * rb:(j + 1) * rb], 1, 0.0, row)
            g_pad[j, 0:s, :] = dh_ref[:, j * rb:(j + 1) * rb]
        b_pad[:, s:, :] = jnp.zeros((nj, 8 * seg - s, rb), F32)
        g_pad[:, s:, :] = jnp.zeros((nj, 8 * seg - s, rb), F32)
        _scan_rows(b_pad, g_pad, l_pad, True)

        for j in range(nj):
            cs = slice(j * rb, (j + 1) * rb)
            x = x_ref[:, cs]
            cwv = cw_ref[:, cs]
            wav, wxv, lamv = wa_ref[j], wx_ref[j], lam_ref[:, cs]
            xc = xc_ref[:, cs]
            r, i, sp, a, mult, inv_mult = _lru_gates(xc, wav, ba_ref[:, cs], wxv, bx_ref[:, cs], lamv)
            lmb = l_pad[j, 0:s, :]
            h_prev = _shift_dn(h_ref[:, cs], 1, 0.0, row)
            da = lmb * h_prev
            ixc = i * xc
            dla = da * a - (lmb * ixc) * (a * a) * inv_mult
            di = lmb * mult * xc
            dxc = lmb * mult * i
            dr = dla * ((-LRU_C) * sp)
            dsp = jnp.sum(dla * ((-LRU_C) * r), axis=0, keepdims=True)
            dga = dr * r * (1.0 - r)
            dgx = di * i * (1.0 - i)
            dga_b, dgx_b = dga.astype(BF16), dgx.astype(BF16)
            xb = xc.astype(BF16)
            dwa_ref[j] += _dot_tn(xb, dga_b)
            dwx_ref[j] += _dot_tn(xb, dgx_b)
            dba_ref[:, cs] += jnp.sum(dga, axis=0, keepdims=True)
            dbx_ref[:, cs] += jnp.sum(dgx, axis=0, keepdims=True)
            dlam_ref[:, cs] += dsp * (-_sig(-lamv))
            dxc = dxc + _dot_nt(dga_b, wav) + _dot_nt(dgx_b, wxv)
            dcb_ref[:, cs] += jnp.sum(dxc, axis=0, keepdims=True)
            dx, dcw = _conv_bwd(dxc, x, cwv, row)
            dcw_ref[:, cs] += dcw
            dx_ref[:, cs] = dx.astype(dx_ref.dtype)

    vec = pl.BlockSpec((1, cols), lambda n, bi: (0, n))
    seq = pl.BlockSpec((None, s, cols), lambda n, bi: (bi, 0, n))
    mat = pl.BlockSpec((nj, rb, rb), lambda n, bi: (n, 0, 0))
    cws = pl.BlockSpec((kk, cols), lambda n, bi: (0, n))
    sd = jax.ShapeDtypeStruct
    r_ins, r_in_specs, r_outs, r_out_specs, r_sems = _ride_args(ride)
    outs = pl.pallas_call(
        _riding(body, 12, 8, 3, ride, 2), name=name, grid=(c // cols, b),
        in_specs=[seq, seq, seq, seq, seq, cws, vec, mat, vec, mat, vec, vec] + r_in_specs,
        out_specs=[seq, cws, vec, mat, vec, mat, vec, vec] + r_out_specs,
        out_shape=[sd((b, s, c), BF16), sd((kk, c), F32), sd((1, c), F32), sd((nb, rb, rb), F32),
                   sd((1, c), F32), sd((nb, rb, rb), F32), sd((1, c), F32), sd((1, c), F32)] + r_outs,
        scratch_shapes=[pltpu.VMEM((nj, 8 * seg, rb), F32)] * 3 + r_sems,
        compiler_params=_params(("arbitrary", "arbitrary")),
    )(xr, h, dh, a_fwd, xc_fwd, cw, cb, wa, ba, wx, bx, lam, *r_ins)
    return outs[:8], _ride_results(ride, outs[8:])


_GELU_C = math.sqrt(2.0 / math.pi)


def _gelu_parts(x):
    th = jnp.tanh(_GELU_C * (x + 0.044715 * x * x * x))
    gel = 0.5 * x * (1.0 + th)
    dgel = 0.5 * (1.0 + th) + 0.5 * x * (1.0 - th * th) * _GELU_C * (1.0 + 3 * 0.044715 * x * x)
    return gel, dgel


def ffn_in_act(x, g, wg, wu, cw, cb, seq_len, name, tm=256):
    t, d = x.shape
    f = wg.shape[1]
    kk = cw.shape[0]
    tm = _tile(seq_len, tm)
    tiles_per_seq = seq_len // tm
    keep = 8
    assert kk - 1 <= keep

    def body(x_ref, g_ref, wg_ref, wu_ref, cw_ref, cb_ref, hn_ref, gp_ref, up_ref, act_ref, tail):
        @pl.when(pl.program_id(0) % tiles_per_seq == 0)
        def _():
            tail[...] = jnp.zeros(tail.shape, F32)

        xv = x_ref[...]
        inv = lax.rsqrt(jnp.mean(xv * xv, axis=-1, keepdims=True) + EPS)
        hn = (xv * inv * g_ref[...]).astype(BF16)
        hn_ref[...] = hn
        gp = _dot(hn, wg_ref[...])
        up = _dot(hn, wu_ref[...])
        gp_ref[...] = gp
        up_ref[...] = up
        cwv = cw_ref[...]
        row = lax.broadcasted_iota(jnp.int32, (tm, 1), 0)
        gate = _conv_fwd(gp, cwv, cb_ref[...], row)
        row8 = lax.broadcasted_iota(jnp.int32, (keep, 1), 0)
        prev = tail[...]
        fix = jnp.zeros((keep, f), F32)
        for j in range(1, kk):
            fix = fix + cwv[kk - 1 - j:kk - j, :] * jnp.where(row8 < j, pltpu.roll(prev, j, 0), 0.0)
        gate = jnp.concatenate([gate[:keep] + fix, gate[keep:]], axis=0)
        tail[...] = gp[tm - keep:, :]
        gel, _ = _gelu_parts(gate)
        act_ref[...] = (gel * up).astype(BF16)

    sd = jax.ShapeDtypeStruct
    return pl.pallas_call(
        body, name=name, grid=(t // tm,),
        in_specs=[_rows(tm, d), _whole(g.shape), _whole(wg.shape), _whole(wu.shape), _whole(cw.shape), _whole(cb.shape)],
        out_specs=[_rows(tm, d), _rows(tm, f), _rows(tm, f), _rows(tm, f)],
        out_shape=[sd((t, d), BF16), sd((t, f), F32), sd((t, f), F32), sd((t, f), BF16)],
        scratch_shapes=[pltpu.VMEM((keep, f), F32)],
        compiler_params=_params(("arbitrary",)),
    )(x, g, wg, wu, cw, cb)


def ffn_in_bwd(dact, gate_pre, up, cw, cb, wg, wu, seq_len, name, ride=(), tm=256):
    t, f = gate_pre.shape
    d = wg.shape[0]
    kk = cw.shape[0]
    tm = _tile(seq_len, tm)
    nt = t // tm
    tiles_per_seq = seq_len // tm
    keep = 8
    assert kk - 1 <= keep

    def body(da_ref, g_ref, halo_ref, u_ref, cw_ref, cb_ref, wg_ref, wu_ref,
             dg_ref, du_ref, dhn_ref, dcw_ref, dcb_ref, nxt):
        tile = (nt - 1 - pl.program_id(0)) % tiles_per_seq

        @pl.when(pl.program_id(0) == 0)
        def _():
            dcw_ref[...] = jnp.zeros(dcw_ref.shape, F32)
            dcb_ref[...] = jnp.zeros(dcb_ref.shape, F32)

        @pl.when(tile == tiles_per_seq - 1)
        def _():
            nxt[...] = jnp.zeros(nxt.shape, F32)

        row = lax.broadcasted_iota(jnp.int32, (tm, 1), 0)
        row8 = lax.broadcasted_iota(jnp.int32, (keep, 1), 0)
        gp = g_ref[...]
        cwv = cw_ref[...]
        prev = jnp.where(tile > 0, halo_ref[...], 0.0)
        gate = _conv_fwd(gp, cwv, cb_ref[...], row)
        fix = jnp.zeros((keep, f), F32)
        for j in range(1, kk):
            fix = fix + cwv[kk - 1 - j:kk - j, :] * jnp.where(row8 < j, pltpu.roll(prev, j, 0), 0.0)
        gate = jnp.concatenate([gate[:keep] + fix, gate[keep:]], axis=0)
        gel, dgel = _gelu_parts(gate)
        da = da_ref[...]
        dup = (da * gel).astype(BF16)
        du_ref[...] = dup
        dgate = da * u_ref[...] * dgel
        dcb_ref[...] += jnp.sum(dgate, axis=0, keepdims=True)
        after = nxt[...]
        dgp = cwv[kk - 1:kk, :] * dgate
        tail_fix = jnp.zeros((keep, f), F32)
        dws = [None] * kk
        dws[kk - 1] = jnp.sum(dgate * gp, axis=0, keepdims=True)
        for j in range(1, kk):
            wj = cwv[kk - 1 - j:kk - j, :]
            dgp = dgp + wj * _shift_up(dgate, j, 0.0, row)
            tail_fix = tail_fix + wj * jnp.where(row8 >= keep - j, pltpu.roll(after, keep - j, 0), 0.0)
            dws[kk - 1 - j] = (jnp.sum(dgate * _shift_dn(gp, j, 0.0, row), axis=0, keepdims=True)
                               + jnp.sum(dgate[:keep] * jnp.where(row8 < j, pltpu.roll(prev, j, 0), 0.0),
                                         axis=0, keepdims=True))
        dgp = jnp.concatenate([dgp[:tm - keep], dgp[tm - keep:] + tail_fix], axis=0).astype(BF16)
        nxt[...] = dgate[:keep]
        dcw_ref[...] += jnp.concatenate(dws, axis=0)
        dg_ref[...] = dgp
        dhn_ref[...] = _dot_nt(dgp, wg_ref[...]) + _dot_nt(dup, wu_ref[...])

    def rev(i):
        return nt - 1 - i

    rows_f = pl.BlockSpec((tm, f), lambda i: (rev(i), 0))
    halo = pl.BlockSpec((None, keep, f), lambda i: (jnp.maximum(rev(i) * (tm // keep) - 1, 0), 0, 0))
    once = pl.Buffered(1)
    sd = jax.ShapeDtypeStruct
    r_ins, r_in_specs, r_outs, r_out_specs, r_sems = _ride_args(ride)
    outs = pl.pallas_call(
        _riding(body, 8, 5, 1, ride, 1), name=name, grid=(nt,),
        in_specs=[rows_f, rows_f, halo, rows_f, _whole(cw.shape), _whole(cb.shape),
                  pl.BlockSpec(wg.shape, lambda i: (0, 0), pipeline_mode=once),
                  pl.BlockSpec(wu.shape, lambda i: (0, 0), pipeline_mode=once)] + r_in_specs,
        out_specs=[rows_f, rows_f, pl.BlockSpec((tm, d), lambda i: (rev(i), 0)), _whole((kk, f)), _whole((1, f))]
        + r_out_specs,
        out_shape=[sd((t, f), BF16), sd((t, f), BF16), sd((t, d), F32), sd((kk, f), F32), sd((1, f), F32)] + r_outs,
        scratch_shapes=[pltpu.VMEM((keep, f), F32)] + r_sems,
        compiler_params=_params(("arbitrary",)),
    )(dact, gate_pre, gate_pre.reshape(t // keep, keep, f), up, cw, cb, wg, wu, *r_ins)
    return outs[:5], _ride_results(ride, outs[5:])


def _t5_bucket(dist):
    max_exact = REL_BUCKETS // 2
    d = np.maximum(dist, 1).astype(np.float32)
    large = max_exact + np.log(d / max_exact) / math.log(REL_MAX_DIST / max_exact) * (REL_BUCKETS - max_exact)
    large = np.minimum(large.astype(np.int32), REL_BUCKETS - 1)
    return np.where(dist < max_exact, dist, large).astype(np.int32)


def _band(window, dilation):
    qi = np.arange(ATTN_BLOCK)[:, None]
    kj = np.arange(2 * ATTN_BLOCK)[None, :]
    delta = ATTN_BLOCK + qi - kj
    mask = (delta >= 0) & (delta <= window // dilation)
    bucket = _t5_bucket(np.maximum(delta, 0) * dilation)
    return mask, bucket


def _attn_blocks(s, r):
    m = s // r
    assert m % ATTN_BLOCK == 0, "sequence length must be a multiple of dilation * block"
    return m // ATTN_BLOCK


def _perm_load(ref, r):
    if r == 1:
        return ref[...]
    m = ref.shape[0] // r
    return jnp.concatenate([ref[pl.ds(c, m, stride=r), :] for c in range(r)], axis=0)


def _perm_store(ref, g, val, r, add=False):
    if r == 1:
        ref[g] = ref[g] + val if add else val
        return
    m = val.shape[0] // r
    for c in range(r):
        rows = pl.ds(c, m, stride=r)
        part = val[c * m:(c + 1) * m]
        ref[g, rows, :] = ref[g, rows, :] + part if add else part


def _blocks(x):
    return x.reshape(x.shape[0] // ATTN_BLOCK, ATTN_BLOCK, x.shape[1])


def _prev_blocks(x):
    return jnp.concatenate([x[:1], x[:-1]], axis=0)


def _next_blocks(x):
    return jnp.concatenate([x[1:], jnp.zeros_like(x[:1])], axis=0)


def _first_block_neg(s, r):
    nblk = s // ATTN_BLOCK
    idx = lax.broadcasted_iota(jnp.int32, (nblk, 1, 1), 0)
    return jnp.where(idx % _attn_blocks(s, r) == 0, NEG, 0.0)


def _bdot_nt(a, b):
    return lax.dot_general(a, b, (((2,), (2,)), ((0,), (0,))), preferred_element_type=F32)


def _bdot(a, b):
    return lax.dot_general(a, b, (((2,), (1,)), ((0,), (0,))), preferred_element_type=F32)


def _bdot_tn(a, b):
    return lax.dot_general(a, b, (((1,), (1,)), ((0,), (0,))), preferred_element_type=F32)


def attn_fwd(qkv, biasm, n_heads, name, ride=()):
    b, s, _ = qkv.shape
    h = n_heads
    scale = HEAD_DIM ** -0.5
    blk = ATTN_BLOCK

    def body(q1_ref, q2_ref, q3_ref, k_ref, v_ref, bias_ref, o_ref, lse_ref, acc, m_s, l_s):
        for g, q_ref in enumerate((q1_ref, q2_ref, q3_ref)):
            r = DILATED[g][1]
            first = _first_block_neg(s, r)
            q = _blocks(_perm_load(q_ref, r).astype(BF16))
            k = _blocks(_perm_load(k_ref, r).astype(BF16))
            v = _blocks(_perm_load(v_ref, r).astype(BF16))
            s_cur = _bdot_nt(q, k) * scale + bias_ref[g, :, blk:]
            s_prev = _bdot_nt(q, _prev_blocks(k)) * scale + bias_ref[g, :, :blk] + first
            m = jnp.max(jnp.maximum(s_cur, s_prev), axis=-1, keepdims=True)
            p_cur = jnp.exp(s_cur - m)
            p_prev = jnp.exp(s_prev - m)
            l = jnp.sum(p_cur + p_prev, axis=-1, keepdims=True)
            o = _bdot(p_cur.astype(BF16), v) + _bdot(p_prev.astype(BF16), _prev_blocks(v))
            _perm_store(acc, g, o.reshape(s, HEAD_DIM), r)
            _perm_store(m_s, g, m.reshape(s, 1), r)
            _perm_store(l_s, g, l.reshape(s, 1), r)
        m_all = jnp.maximum(jnp.maximum(m_s[0], m_s[1]), m_s[2])
        w = [jnp.exp(m_s[g] - m_all) for g in range(N_GROUPS)]
        l = w[0] * l_s[0] + w[1] * l_s[1] + w[2] * l_s[2]
        o_ref[...] = (w[0] * acc[0] + w[1] * acc[1] + w[2] * acc[2]) * (1.0 / l)
        lse_ref[...] = m_all + jnp.log(l)

    def col(j):
        return pl.BlockSpec((None, s, HEAD_DIM), lambda bi, hi, j=j: (bi, 0, j * h + hi))

    r_ins, r_in_specs, r_outs, r_out_specs, r_sems = _ride_args(ride)
    outs = pl.pallas_call(
        _riding(body, 6, 2, 3, ride, 2), name=name, grid=(b, h),
        in_specs=[col(0), col(1), col(2), col(3), col(4),
                  pl.BlockSpec((N_GROUPS, None, blk, 2 * blk), lambda bi, hi: (0, hi, 0, 0))] + r_in_specs,
        out_specs=[pl.BlockSpec((None, s, HEAD_DIM), lambda bi, hi: (bi, 0, hi)),
                   pl.BlockSpec((None, None, s, 1), lambda bi, hi: (bi, hi, 0, 0))] + r_out_specs,
        out_shape=[jax.ShapeDtypeStruct((b, s, h * HEAD_DIM), F32), jax.ShapeDtypeStruct((b, h, s, 1), F32)] + r_outs,
        scratch_shapes=[pltpu.VMEM((N_GROUPS, s, HEAD_DIM), F32), pltpu.VMEM((N_GROUPS, s, 1), F32),
                        pltpu.VMEM((N_GROUPS, s, 1), F32)] + r_sems,
        compiler_params=_params(("arbitrary", "arbitrary")),
    )(qkv, qkv, qkv, qkv, qkv, biasm, *r_ins)
    return outs[0], outs[1], _ride_results(ride, outs[2:])


def attn_bwd(qkv, biasm, o, lse, do, n_heads, name, ride=()):
    b, s, _ = qkv.shape
    h = n_heads
    scale = HEAD_DIM ** -0.5
    blk = ATTN_BLOCK

    def body(q1_ref, q2_ref, q3_ref, k_ref, v_ref, bias_ref, o_ref, lse_ref, do_ref,
             dq1_ref, dq2_ref, dq3_ref, dk_ref, dv_ref, ds_ref, dq_acc, kv_acc, delta):
        delta[...] = jnp.sum(do_ref[...] * o_ref[...], axis=-1, keepdims=True)
        kv_acc[...] = jnp.zeros(kv_acc.shape, F32)
        for g, q_ref in enumerate((q1_ref, q2_ref, q3_ref)):
            r = DILATED[g][1]
            first = _first_block_neg(s, r)
            q = _blocks(_perm_load(q_ref, r).astype(BF16))
            k = _blocks(_perm_load(k_ref, r).astype(BF16))
            v = _blocks(_perm_load(v_ref, r).astype(BF16))
            dob = _blocks(_perm_load(do_ref, r).astype(BF16))
            lse_b = _blocks(_perm_load(lse_ref, r))
            dl_b = _blocks(_perm_load(delta, r))
            k_prev, v_prev = _prev_blocks(k), _prev_blocks(v)
            p_cur = jnp.exp(_bdot_nt(q, k) * scale + bias_ref[g, :, blk:] - lse_b)
            p_prev = jnp.exp(_bdot_nt(q, k_prev) * scale + bias_ref[g, :, :blk] + first - lse_b)
            ds_cur = p_cur * (_bdot_nt(dob, v) - dl_b)
            ds_prev = p_prev * (_bdot_nt(dob, v_prev) - dl_b)
            ds_ref[g, :, blk:] = jnp.sum(ds_cur, axis=0)
            ds_ref[g, :, :blk] = jnp.sum(ds_prev, axis=0)
            ds_cur_b, ds_prev_b = ds_cur.astype(BF16), ds_prev.astype(BF16)
            dq = (_bdot(ds_cur_b, k) + _bdot(ds_prev_b, k_prev)) * scale
            _perm_store(dq_acc, g, dq.reshape(s, HEAD_DIM), r)
            dk = (_bdot_tn(ds_cur_b, q) + _next_blocks(_bdot_tn(ds_prev_b, q))) * scale
            dv = _bdot_tn(p_cur.astype(BF16), dob) + _next_blocks(_bdot_tn(p_prev.astype(BF16), dob))
            _perm_store(kv_acc, 0, dk.reshape(s, HEAD_DIM), r, add=True)
            _perm_store(kv_acc, 1, dv.reshape(s, HEAD_DIM), r, add=True)
        for g, out_ref in enumerate((dq1_ref, dq2_ref, dq3_ref)):
            out_ref[...] = dq_acc[g].astype(out_ref.dtype)
        dk_ref[...] = kv_acc[0].astype(dk_ref.dtype)
        dv_ref[...] = kv_acc[1].astype(dv_ref.dtype)

    def col(j):
        return pl.BlockSpec((None, s, HEAD_DIM), lambda bi, hi, j=j: (bi, 0, j * h + hi))

    head = pl.BlockSpec((None, s, HEAD_DIM), lambda bi, hi: (bi, 0, hi))
    sd = jax.ShapeDtypeStruct
    r_ins, r_in_specs, r_outs, r_out_specs, r_sems = _ride_args(ride)
    outs = pl.pallas_call(
        _riding(body, 9, 6, 3, ride, 2), name=name, grid=(b, h),
        in_specs=[col(0), col(1), col(2), col(3), col(4),
                  pl.BlockSpec((N_GROUPS, None, blk, 2 * blk), lambda bi, hi: (0, hi, 0, 0)),
                  head, pl.BlockSpec((None, None, s, 1), lambda bi, hi: (bi, hi, 0, 0)), head] + r_in_specs,
        out_specs=[head] * 5 + [pl.BlockSpec((None, None, N_GROUPS, blk, 2 * blk), lambda bi, hi: (bi, hi, 0, 0, 0))]
        + r_out_specs,
        out_shape=[sd((b, s, h * HEAD_DIM), BF16)] * 5 + [sd((b, h, N_GROUPS, blk, 2 * blk), F32)] + r_outs,
        scratch_shapes=[pltpu.VMEM((N_GROUPS, s, HEAD_DIM), F32), pltpu.VMEM((2, s, HEAD_DIM), F32),
                        pltpu.VMEM((s, 1), F32)] + r_sems,
        compiler_params=_params(("arbitrary", "arbitrary")),
    )(qkv, qkv, qkv, qkv, qkv, biasm, o, lse, do, *r_ins)
    return outs[:6], _ride_results(ride, outs[6:])


def bias_table(rel_rows, bucket_f, n_heads, name):
    g, blk, blk2 = bucket_f.shape
    h = n_heads

    def body(rb_ref, bk_ref, o_ref):
        bk = bk_ref[...]
        rb = rb_ref[...]
        acc = jnp.full((blk, blk2), NEG, F32)
        for bucket in range(REL_BUCKETS):
            acc = jnp.where(bk == float(bucket), rb[:, bucket:bucket + 1], acc)
        o_ref[...] = acc

    return pl.pallas_call(
        body, name=name, grid=(g, h),
        in_specs=[pl.BlockSpec((None, 1, 128), lambda gi, hi: (gi * h + hi, 0, 0)),
                  pl.BlockSpec((None, blk, blk2), lambda gi, hi: (gi, 0, 0))],
        out_specs=pl.BlockSpec((None, None, blk, blk2), lambda gi, hi: (gi, hi, 0, 0)),
        out_shape=jax.ShapeDtypeStruct((g, h, blk, blk2), F32),
        compiler_params=_params(("parallel", "parallel")),
    )(rel_rows, bucket_f)


def bias_grad(ds_sum, bucket_f, name):
    b, h, g, blk, blk2 = ds_sum.shape

    def body(ds_ref, bk_ref, o_ref):
        tot = jnp.sum(ds_ref[...], axis=0)
        bk = bk_ref[...]
        lane = lax.broadcasted_iota(jnp.int32, (1, 128), 1)
        vec = jnp.zeros((1, 128), F32)
        for bucket in range(REL_BUCKETS):
            val = jnp.sum(jnp.where(bk == float(bucket), tot, 0.0), keepdims=True)
            vec = vec + jnp.where(lane == bucket, val, 0.0)
        o_ref[...] = vec

    return pl.pallas_call(
        body, name=name, grid=(g, h),
        in_specs=[pl.BlockSpec((b, None, None, blk, blk2), lambda gi, hi: (0, hi, gi, 0, 0)),
                  pl.BlockSpec((None, blk, blk2), lambda gi, hi: (gi, 0, 0))],
        out_specs=pl.BlockSpec((None, 1, 128), lambda gi, hi: (gi * h + hi, 0, 0)),
        out_shape=jax.ShapeDtypeStruct((g * h, 1, 128), F32),
        compiler_params=_params(("parallel", "parallel")),
    )(ds_sum, bucket_f)


def _chip_peers():
    x, y, c = lax.axis_index("x"), lax.axis_index("y"), lax.axis_index("c")
    me = 2 * x + y
    peers = [(1 - x, y, c), (x, 1 - y, c), (1 - x, 1 - y, c)]
    peer_chip = [2 * (1 - x) + y, 2 * x + (1 - y), 2 * (1 - x) + (1 - y)]
    return me, peers, peer_chip


def _any_specs(n):
    return [pl.BlockSpec(memory_space=pl.ANY)] * n


_MID_NUM, _MID_DEN = 3, 4


class _Exchange:
    def start(self, ins, outs, sems):
        local, sends, _ = self._copies(ins, outs, sems)
        for cp in local + sends:
            cp.start()

    def mid(self, ins, outs, sems):
        pass

    def wait(self, ins, outs, sems):
        local, sends, recvs = self._copies(ins, outs, sems)
        for cp in recvs():
            cp.wait_recv()
        for cp in sends:
            cp.wait_send()
        for cp in local:
            cp.wait()


class _Gather(_Exchange):
    HALF_ROWS = 16

    def __init__(self, arrays):
        n = len(arrays)
        self.ins = list(arrays)
        self.split = [a.shape[0] % (2 * self.HALF_ROWS) == 0 for a in arrays]
        self.out_shape = [jax.ShapeDtypeStruct((N_CHIPS,) + a.shape, a.dtype) for a in arrays]
        dma = pltpu.SemaphoreType.DMA
        self.sems = [dma((3 * n,)), dma((3 * n,)), dma((n,)), dma((3 * n,)), dma((3 * n,))]

    def _half(self, i, ref, sibling=False):
        if not self.split[i]:
            return ref
        half = self.ins[i].shape[0] // 2
        c = lax.axis_index("c")
        c = 1 - c if sibling else c
        return ref.at[pl.ds(pl.multiple_of(c * half, self.HALF_ROWS), half)]

    def _plan(self, ins, outs, sems):
        send1, recv1, local_sems, send2, recv2 = sems
        me, peers, peer_chip = _chip_peers()
        x, y, c = lax.axis_index("x"), lax.axis_index("y"), lax.axis_index("c")
        n = len(ins)
        pairs = [(i, k) for i in range(n) for k in range(3)]

        def fetch(i, k, slot):
            return pltpu.make_async_remote_copy(src_ref=self._half(i, ins[i]), dst_ref=self._half(i, outs[i].at[slot]),
                                                send_sem=send1.at[3 * i + k], recv_sem=recv1.at[3 * i + k],
                                                device_id=peers[k], device_id_type=MESH)

        def share(i, k, sibling):
            part = self._half(i, outs[i].at[peer_chip[k]], sibling)
            return pltpu.make_async_remote_copy(src_ref=part, dst_ref=part, send_sem=send2.at[3 * i + k],
                                                recv_sem=recv2.at[3 * i + k], device_id=(x, y, 1 - c),
                                                device_id_type=MESH)

        split_pairs = [(i, k) for i, k in pairs if self.split[i]]
        return dict(
            local=lambda: [pltpu.make_async_copy(ins[i], outs[i].at[me], local_sems.at[i]) for i in range(n)],
            fetch_out=lambda: [fetch(i, k, me) for i, k in pairs],
            fetch_in=lambda: [fetch(i, k, peer_chip[k]) for i, k in pairs],
            share_out=lambda: [share(i, k, False) for i, k in split_pairs],
            share_in=lambda: [share(i, k, True) for i, k in split_pairs])

    def start(self, ins, outs, sems):
        plan = self._plan(ins, outs, sems)
        for cp in plan["local"]() + plan["fetch_out"]():
            cp.start()

    def mid(self, ins, outs, sems):
        plan = self._plan(ins, outs, sems)
        for cp in plan["fetch_in"]():
            cp.wait_recv()
        for cp in plan["share_out"]():
            cp.start()

    def wait(self, ins, outs, sems):
        plan = self._plan(ins, outs, sems)
        for cp in plan["share_in"]():
            cp.wait_recv()
        for cp in plan["fetch_out"]() + plan["share_out"]():
            cp.wait_send()
        for cp in plan["local"]():
            cp.wait()


class _Scatter(_Exchange):
    def __init__(self, slabs, whole=()):
        self.n_slabs = len(slabs)
        self.ins = list(slabs) + list(whole)
        n = len(self.ins)
        self.out_shape = [jax.ShapeDtypeStruct(a.shape, a.dtype) for a in slabs] \
            + [jax.ShapeDtypeStruct((N_CHIPS,) + a.shape, a.dtype) for a in whole]
        self.sems = [pltpu.SemaphoreType.DMA((3 * n,)), pltpu.SemaphoreType.DMA((3 * n,)), pltpu.SemaphoreType.DMA((n,))]

    def _copies(self, ins, outs, sems):
        send_sems, recv_sems, local_sems = sems
        me, peers, peer_chip = _chip_peers()
        n = len(ins)

        def src(i, chip):
            return ins[i].at[chip] if i < self.n_slabs else ins[i]

        def remote(i, k, src_chip, slot):
            return pltpu.make_async_remote_copy(src_ref=src(i, src_chip), dst_ref=outs[i].at[slot],
                                                send_sem=send_sems.at[3 * i + k], recv_sem=recv_sems.at[3 * i + k],
                                                device_id=peers[k], device_id_type=MESH)

        local = [pltpu.make_async_copy(src(i, me), outs[i].at[me], local_sems.at[i]) for i in range(n)]
        sends = [remote(i, k, peer_chip[k], me) for i in range(n) for k in range(3)]
        return local, sends, lambda: [remote(i, k, me, peer_chip[k]) for i in range(n) for k in range(3)]


class _Swap(_Exchange):
    def __init__(self, arrays):
        n = len(arrays)
        self.ins = list(arrays)
        self.out_shape = [jax.ShapeDtypeStruct(a.shape, a.dtype) for a in arrays]
        self.sems = [pltpu.SemaphoreType.DMA((n,)), pltpu.SemaphoreType.DMA((n,))]

    def _copies(self, ins, outs, sems):
        send_sems, recv_sems = sems
        x, y, c = lax.axis_index("x"), lax.axis_index("y"), lax.axis_index("c")
        cps = [pltpu.make_async_remote_copy(src_ref=ins[i], dst_ref=outs[i], send_sem=send_sems.at[i],
                                            recv_sem=recv_sems.at[i], device_id=(x, y, 1 - c), device_id_type=MESH)
               for i in range(len(ins))]
        return [], cps, lambda: cps


def _riding(body, n_in, n_out, n_scratch, ride, rank):
    if not ride:
        return body
    r_in = sum(len(e.ins) for e in ride)
    r_out = sum(len(e.out_shape) for e in ride)

    def split(refs, sizes):
        out, a = [], 0
        for sz in sizes:
            out.append(refs[a:a + sz])
            a += sz
        return out

    def wrapped(*refs):
        a = 0
        parts = []
        for sz in (n_in, r_in, n_out, r_out, n_scratch):
            parts.append(refs[a:a + sz])
            a += sz
        own_in, ex_in, own_out, ex_out, own_scratch = parts
        ex_sems = refs[a:]
        ins = split(ex_in, [len(e.ins) for e in ride])
        outs = split(ex_out, [len(e.out_shape) for e in ride])
        sems = split(ex_sems, [len(e.sems) for e in ride])
        if rank:
            step, total = 0, 1
            for d in range(rank):
                step = step * pl.num_programs(d) + pl.program_id(d)
                total = total * pl.num_programs(d)

            @pl.when(step == 0)
            def _():
                for e, i, o, s in zip(ride, ins, outs, sems):
                    e.start(i, o, s)

            body(*own_in, *own_out, *own_scratch)

            @pl.when(step == (total * _MID_NUM) // _MID_DEN)
            def _():
                for e, i, o, s in zip(ride, ins, outs, sems):
                    e.mid(i, o, s)

            @pl.when(step == total - 1)
            def _():
                for e, i, o, s in zip(ride, ins, outs, sems):
                    e.wait(i, o, s)
        else:
            for phase in ("start", "mid", "wait"):
                for e, i, o, s in zip(ride, ins, outs, sems):
                    getattr(e, phase)(i, o, s)

    return wrapped


def _ride_args(ride):
    ins = [a for e in ride for a in e.ins]
    outs = [s for e in ride for s in e.out_shape]
    sems = [s for e in ride for s in e.sems]
    return ins, _any_specs(len(ins)), outs, _any_specs(len(outs)), sems


def _ride_results(ride, flat):
    out, a = [], 0
    for e in ride:
        out.append(list(flat[a:a + len(e.out_shape)]))
        a += len(e.out_shape)
    return out


def exchange(ride, name):
    ins, in_specs, outs, out_specs, sems = _ride_args(ride)
    res = pl.pallas_call(
        _riding(lambda: None, 0, 0, 0, ride, 0), name=name,
        in_specs=in_specs, out_specs=out_specs, out_shape=outs, scratch_shapes=sems,
    )(*ins)
    return _ride_results(ride, res)


def _sum_slots(ref):
    acc = ref[0].astype(F32)
    for j in range(1, ref.shape[0]):
        acc = acc + ref[j].astype(F32)
    return acc


def sum_pairs(mine, other, name, tr=176):
    n, r, w = mine.shape
    tr = _tile(r, tr)

    def body(a_ref, b_ref, o_ref):
        o_ref[...] = _sum_slots(a_ref) + _sum_slots(b_ref)

    spec = pl.BlockSpec((n, tr, w), lambda i: (0, i, 0))
    return pl.pallas_call(
        body, name=name, grid=(r // tr,),
        in_specs=[spec, spec], out_specs=_rows(tr, w),
        out_shape=jax.ShapeDtypeStruct((r, w), F32),
        compiler_params=_params(("parallel",)),
    )(mine, other)


def adamw(w, m, v, gs, name, tr=256):
    r, c = w.shape
    tr = r if r % 8 else _tile(r, tr)
    c1 = 1.0 - ADAM_B1 ** ADAM_STEP
    c2 = 1.0 - ADAM_B2 ** ADAM_STEP
    ng = len(gs)

    def body(w_ref, m_ref, v_ref, *refs):
        g_refs, (g_ref, d_ref, nm_ref, nv_ref) = refs[:ng], refs[ng:]
        g = g_refs[0][...] if ng == 1 else _sum_slots(g_refs[0]) + _sum_slots(g_refs[1])
        nm = ADAM_B1 * m_ref[...] + (1.0 - ADAM_B1) * g
        nv = ADAM_B2 * v_ref[...] + (1.0 - ADAM_B2) * (g * g)
        g_ref[...] = g
        nm_ref[...] = nm
        nv_ref[...] = nv
        d_ref[...] = (-ADAM_LR) * ((nm / c1) / (jnp.sqrt(nv / c2) + ADAM_EPS) + ADAM_WD * w_ref[...])

    spec = _rows(tr, c)
    gspec = spec if ng == 1 else pl.BlockSpec((N_CHIPS, tr, c), lambda i: (0, i, 0))
    return pl.pallas_call(
        body, name=name, grid=(r // tr,),
        in_specs=[spec] * 3 + [gspec] * ng, out_specs=[spec] * 4,
        out_shape=[jax.ShapeDtypeStruct((r, c), F32)] * 4,
        compiler_params=_params(("parallel",)),
    )(w, m, v, *gs)


_PARAMS = (
    ("rel_bias", None), ("norm_mix_pre", None), ("norm_mix_post", None), ("w_in", 1), ("conv_rnn_w", 1),
    ("conv_rnn_b", None), ("w_rg_a", None), ("b_rg_a", None), ("w_rg_x", None), ("b_rg_x", None),
    ("lru_lambda", None), ("w_branch_rnn", 0), ("w_branch_att", 1), ("w_out", 0), ("norm_ffn_pre", None),
    ("norm_ffn_post", None), ("w_ffn_gate", 1), ("w_ffn_up", 1), ("conv_ffn_w", 1), ("conv_ffn_b", None),
    ("w_ffn_down", 0),
)
_SMALL = 65536


def _as2d(a):
    a = a[0] if a.shape[0] == 1 and a.ndim >= 3 else a
    return a.reshape(-1, a.shape[-1]) if a.ndim == 3 else a


def _pack(pieces, dtype):
    flat = jnp.concatenate([p.astype(dtype).reshape(-1) for p in pieces])
    unit = PACK_W * PACK_ROWS
    pad = (-flat.shape[0]) % unit
    flat = jnp.pad(flat, (0, pad))
    return flat.reshape(-1, PACK_W)


def _unpack(buf, shapes):
    flat = buf.reshape(-1)
    out, off = [], 0
    for shp in shapes:
        n = int(np.prod(shp))
        out.append(flat[off:off + n].reshape(shp))
        off += n
    return out


def _join(slots, ax):
    if ax == 0:
        return slots.reshape(-1, slots.shape[-1])
    return jnp.transpose(slots, (1, 0, 2)).reshape(slots.shape[1], -1)


def _cut(full, ax):
    if ax == 0:
        return full.reshape(N_CHIPS, -1, full.shape[-1])
    return jnp.transpose(full.reshape(full.shape[0], N_CHIPS, -1), (1, 0, 2))


def kernel(x, rel_bias, norm_mix_pre, norm_mix_post, w_in, conv_rnn_w, conv_rnn_b, w_rg_a, b_rg_a, w_rg_x, b_rg_x, lru_lambda, w_branch_rnn, w_branch_att, w_out, norm_ffn_pre, norm_ffn_post, w_ffn_gate, w_ffn_up, conv_ffn_w, conv_ffn_b, w_ffn_down, loss_target, m_rel_bias, m_norm_mix_pre, m_norm_mix_post, m_w_in, m_conv_rnn_w, m_conv_rnn_b, m_w_rg_a, m_b_rg_a, m_w_rg_x, m_b_rg_x, m_lru_lambda, m_w_branch_rnn, m_w_branch_att, m_w_out, m_norm_ffn_pre, m_norm_ffn_post, m_w_ffn_gate, m_w_ffn_up, m_conv_ffn_w, m_conv_ffn_b, m_w_ffn_down, v_rel_bias, v_norm_mix_pre, v_norm_mix_post, v_w_in, v_conv_rnn_w, v_conv_rnn_b, v_w_rg_a, v_b_rg_a, v_w_rg_x, v_b_rg_x, v_lru_lambda, v_w_branch_rnn, v_w_branch_att, v_w_out, v_norm_ffn_pre, v_norm_ffn_post, v_w_ffn_gate, v_w_ffn_up, v_conv_ffn_w, v_conv_ffn_b, v_w_ffn_down):
    args = dict(locals())
    names = [n for n, _ in _PARAMS]
    axis = dict(_PARAMS)
    w_loc = {n: args[n] for n in names}
    m_loc = {n: args["m_" + n] for n in names}
    v_loc = {n: args["v_" + n] for n in names}
    sharded = [n for n in names if axis[n] is not None]
    replicated = [n for n in names if axis[n] is None]

    big = [n for n in sharded if w_loc[n].size >= _SMALL]
    small_sharded = [n for n in sharded if n not in big]
    small = replicated + small_sharded

    first = ["w_in"] + small_sharded
    srcs = [_as2d(w_loc[n]).astype(BF16) if n in big else _as2d(w_loc[n]) for n in first]
    (gathered,) = exchange([_Gather(srcs)], "gather_first")
    p = {n: _join(a, axis[n]) for n, a in zip(first, gathered)}
    for n in replicated:
        p[n] = _as2d(w_loc[n])
    shards = {n: _as2d(w_loc[n]).astype(BF16) for n in big if n not in first}

    received, sibling, g_small, loss_part = _local_step(x, loss_target, p, shards)

    pack = _pack([g_small[n] for n in small], BF16)
    ((received["small"],),) = exchange([_Scatter([], [pack])], "scatter_small")
    late = [n for n in received if n not in sibling]
    (swapped,) = exchange([_Swap([received[n] for n in late])], "swap_last")
    sibling.update(zip(late, swapped))
    small_sum = sum_pairs(received["small"], sibling["small"], "sum_small")
    g_tot = dict(zip(small, _unpack(small_sum, [g_small[n].shape for n in small])))
    chip = 2 * lax.axis_index("x") + lax.axis_index("y")
    for n in small_sharded:
        size = g_tot[n].shape[axis[n]] // N_CHIPS
        g_tot[n] = lax.dynamic_slice_in_dim(g_tot[n], chip * size, size, axis=axis[n])

    out_g, out_d, out_m, out_v = {}, {}, {}, {}
    for i, n in enumerate(names):
        shp = w_loc[n].shape
        gs = (received[n], sibling[n]) if n in big else (g_tot[n],)
        g, d, nm, nv = adamw(_as2d(w_loc[n]), _as2d(m_loc[n]), _as2d(v_loc[n]), gs, "adamw_" + n)
        out_g[n], out_d[n], out_m[n], out_v[n] = (t.reshape(shp) for t in (g, d, nm, nv))

    d_model = x.shape[-1]
    loss = lax.psum(0.5 * jnp.sum(loss_part) / d_model, ("x", "y", "c"))
    grad_x = g_small["x"]
    return (loss, grad_x, *[out_g[n] for n in names], *[out_d[n] for n in names],
            *[out_m[n] for n in names], *[out_v[n] for n in names])


def _local_step(x, target, p, shards):
    axis = dict(_PARAMS)
    b, s, d = x.shape
    t = b * s
    rnn = p["b_rg_a"].shape[1]
    ffn = p["conv_ffn_b"].shape[1]
    nbk = rnn // p["w_rg_a"].shape[1]
    hkv = (p["w_in"].shape[1] - rnn - 2 * d) // (N_GROUPS + 2)
    h = hkv // HEAD_DIM
    nq = N_GROUPS * hkv

    x2 = x.reshape(t, d)
    tgt = target.reshape(t, d)
    w_in = p["w_in"]
    in_splits = (rnn, nq + 2 * hkv, 2 * d)
    wa = p["w_rg_a"].reshape(nbk, -1, p["w_rg_a"].shape[1]).astype(BF16)
    wx = p["w_rg_x"].reshape(nbk, -1, p["w_rg_x"].shape[1]).astype(BF16)
    cw_r, cb_r = p["conv_rnn_w"], p["conv_rnn_b"]
    cw_f, cb_f = p["conv_ffn_w"], p["conv_ffn_b"]

    masks, buckets = zip(*[_band(w_, r_) for w_, r_ in DILATED])
    bucket_f = jnp.asarray(np.where(np.stack(masks), np.stack(buckets), -1).astype(np.float32))
    rel_rows = jnp.pad(p["rel_bias"].T, ((0, 0), (0, 128 - REL_BUCKETS)))[:, None, :]
    biasm = bias_table(rel_rows, bucket_f, h, "bias_table")

    early = ["w_branch_rnn", "w_branch_att", "w_out"]
    hn1, (xr, qkv, gts), (got,) = norm_mm(x2, p["norm_mix_pre"], [w_in], [in_splits], "in_proj",
                                          ride=[_Gather([shards[n] for n in early])])
    p.update({n: _join(a, axis[n]) for n, a in zip(early, got)})
    xr3 = xr.reshape(b, s, rnn)
    (y_rnn, a_rnn, xc_rnn), (got,) = rglru_fwd(xr3, cw_r, cb_r, wa, p["b_rg_a"], wx, p["b_rg_x"], p["lru_lambda"], "rglru_fwd",
                              ride=[_Gather([shards[n] for n in ("w_ffn_gate", "w_ffn_up")])])
    p.update({n: _join(a, axis[n]) for n, a in zip(("w_ffn_gate", "w_ffn_up"), got)})
    qkv3 = qkv.reshape(b, s, -1)
    o_att, lse, ((got,),) = attn_fwd(qkv3, biasm, h, "attn_fwd", ride=[_Gather([shards["w_ffn_down"]])])
    p["w_ffn_down"] = _join(got, axis["w_ffn_down"])
    merged, br, ba, mix, h1 = merge_out(y_rnn.reshape(t, rnn), o_att.reshape(t, hkv), gts, p["w_branch_rnn"],
                                        p["w_branch_att"], p["w_out"], p["norm_mix_post"], x2, "merge_out")
    hn2, gate_pre, up, act = ffn_in_act(h1, p["norm_ffn_pre"], p["w_ffn_gate"], p["w_ffn_up"], cw_f, cb_f, s, "ffn_in")

    g, gb = {}, {}
    recv, sib = {}, {}

    def rows4(a):
        return a.reshape(N_CHIPS, -1, a.shape[-1])

    dy, dff, dact, g["norm_ffn_post"], loss_part = ffn_down_loss(act, p["w_ffn_down"], p["norm_ffn_post"], h1, tgt,
                                                                  "ffn_down")
    gb["w_ffn_down"] = rows4(mm_tn(act, [dff], "ffn_down_dw"))
    (dgp, dup, dhn2, g["conv_ffn_w"], g["conv_ffn_b"]), ((recv["w_ffn_down"],),) = ffn_in_bwd(
        dact, gate_pre, up, cw_f, cb_f, p["w_ffn_gate"], p["w_ffn_up"], s, "ffn_in_bwd",
        ride=[_Scatter([gb["w_ffn_down"]])])
    gb["w_ffn_gate"] = mm_tn(hn2, [dgp], "ffn_gate_dw", col_shards=N_CHIPS)
    gb["w_ffn_up"] = mm_tn(hn2, [dup], "ffn_up_dw", col_shards=N_CHIPS)
    (dh1, dgts, dy_rnn, do_att, g["norm_ffn_pre"], g["norm_mix_post"], dw_out, dw_br,
     gb["w_branch_att"]) = mid_bwd(dhn2, h1, p["norm_ffn_pre"], dy, mix, p["norm_mix_post"], p["w_out"], gts, br, ba,
                                   p["w_branch_rnn"], p["w_branch_att"], merged, y_rnn.reshape(t, rnn),
                                   o_att.reshape(t, hkv), "mid_bwd")
    gb["w_out"], gb["w_branch_rnn"] = rows4(dw_out), rows4(dw_br)
    ffn_in = ["w_ffn_gate", "w_ffn_up"]
    (dxr, g["conv_rnn_w"], g["conv_rnn_b"], dwa, g["b_rg_a"], dwx, g["b_rg_x"], g["lru_lambda"]), (got,) = rglru_bwd(
        xr3, y_rnn, dy_rnn.reshape(b, s, rnn), a_rnn, xc_rnn, cw_r, cb_r, wa, p["b_rg_a"], wx, p["b_rg_x"], p["lru_lambda"], "rglru_bwd",
        ride=[_Scatter([gb[n] for n in ffn_in])])
    recv.update(zip(ffn_in, got))
    g["w_rg_a"] = dwa.reshape(p["w_rg_a"].shape)
    g["w_rg_x"] = dwx.reshape(p["w_rg_x"].shape)
    mid = ["w_out", "w_branch_rnn", "w_branch_att"]
    early_recv = ["w_ffn_down"] + ffn_in
    (dq1, dq2, dq3, dk, dv, ds_sum), (got, swapped) = attn_bwd(
        qkv3, biasm, o_att, lse, do_att.reshape(b, s, hkv), h, "attn_bwd",
        ride=[_Scatter([gb[n] for n in mid]), _Swap([recv[n] for n in early_recv])])
    recv.update(zip(mid, got))
    sib.update(zip(early_recv, swapped))
    rows = bias_grad(ds_sum, bucket_f, "bias_grad")
    g["rel_bias"] = rows[:, 0, :REL_BUCKETS].T
    dproj = [dxr.reshape(t, rnn)] + [a.reshape(t, hkv) for a in (dq1, dq2, dq3, dk, dv)] + [dgts]
    dw_a = mm_tn(hn1, dproj[:4], "in_proj_dw_a")[0]
    dw_b = mm_tn(hn1, dproj[4:], "in_proj_dw_b")[0]
    gb["w_in"] = _cut(jnp.concatenate([dw_a, dw_b], axis=1), 1)
    dx, g["norm_mix_pre"], ((recv["w_in"],), got) = mm_nt(
        [(dproj, w_in)], "in_proj_dx", norm=(x2, p["norm_mix_pre"], dh1),
        ride=[_Scatter([gb["w_in"]]), _Swap([recv[n] for n in mid])])
    sib.update(zip(mid, got))
    g["x"] = dx.reshape(b, s, d)
    return recv, sib, g, loss_part
```

```python
import functools
import math

import numpy as np
import jax
import jax.numpy as jnp
from jax import lax
from jax.experimental import pallas as pl
from jax.experimental.pallas import tpu as pltpu

F32 = jnp.float32
BF16 = jnp.bfloat16

EPS = 1e-6
HEAD_DIM = 128
ATTN_BLOCK = 128
DILATED = ((128, 1), (512, 4), (2048, 16))
N_GROUPS = len(DILATED)
REL_BUCKETS = 32
REL_MAX_DIST = 2048
LRU_C = 8.0
NEG = -1e30

ADAM_LR = 0.001
ADAM_B1 = 0.9
ADAM_B2 = 0.999
ADAM_EPS = 1e-08
ADAM_WD = 0.01
ADAM_STEP = 10

N_CHIPS = 4
PACK_W = 1024
PACK_ROWS = 16
VMEM_LIMIT = 56 * 1024 * 1024
MESH = pl.DeviceIdType.MESH


def _params(sem=None):
    return pltpu.CompilerParams(dimension_semantics=sem, vmem_limit_bytes=VMEM_LIMIT)


def _dot(a, b):
    return jnp.dot(a, b, preferred_element_type=F32)


def _dot_nt(a, b):
    return lax.dot_general(a, b, (((1,), (1,)), ((), ())), preferred_element_type=F32)


def _dot_tn(a, b):
    return lax.dot_general(a, b, (((0,), (0,)), ((), ())), preferred_element_type=F32)


def _sig(x):
    return 0.5 * jnp.tanh(0.5 * x) + 0.5


def _rows(tm, w):
    return pl.BlockSpec((tm, w), lambda i: (i, 0))


def _whole(shape):
    nd = len(shape)
    return pl.BlockSpec(tuple(shape), lambda *_: (0,) * nd)


def _resident(shape):
    nd = len(shape)
    return pl.BlockSpec(tuple(shape), lambda *_: (0,) * nd, pipeline_mode=pl.Buffered(1))


def _tile(t, want):
    while t % want:
        want //= 2
    return want


def norm_mm(x, g, ws, splits, name, ride=(), tm=512):
    t, d = x.shape
    tm = _tile(t, tm)
    nw = len(ws)
    widths = [n for sp in splits for n in sp]

    def body(x_ref, g_ref, *refs):
        w_refs, hn_ref, o_refs = refs[:nw], refs[nw], refs[nw + 1:]
        xv = x_ref[...]
        inv = lax.rsqrt(jnp.mean(xv * xv, axis=-1, keepdims=True) + EPS)
        hn = (xv * inv * g_ref[...]).astype(BF16)
        hn_ref[...] = hn
        o = 0
        for w_ref, sp in zip(w_refs, splits):
            off = 0
            for n in sp:
                o_refs[o][...] = _dot(hn, w_ref[:, off:off + n])
                off += n
                o += 1

    r_ins, r_in_specs, r_outs, r_out_specs, r_sems = _ride_args(ride)
    n_out = 1 + len(widths)
    outs = pl.pallas_call(
        _riding(body, 2 + nw, n_out, 0, ride, 1), name=name, grid=(t // tm,),
        in_specs=[_rows(tm, d), _whole(g.shape)] + [_resident(w.shape) for w in ws] + r_in_specs,
        out_specs=[_rows(tm, d)] + [_rows(tm, n) for n in widths] + r_out_specs,
        out_shape=[jax.ShapeDtypeStruct((t, d), BF16)] + [jax.ShapeDtypeStruct((t, n), F32) for n in widths] + r_outs,
        scratch_shapes=r_sems,
        compiler_params=_params(("arbitrary",)),
    )(x, g, *ws, *r_ins)
    return outs[0], outs[1:n_out], _ride_results(ride, outs[n_out:])


def mm_nt(groups, name, ride=(), norm=None, tm=512):
    dys_all = [dy for dys, _ in groups for dy in dys]
    ws = [w for _, w in groups]
    t = dys_all[0].shape[0]
    k = ws[0].shape[0]
    tm = _tile(t, tm)
    n = len(dys_all)
    extra = list(norm) if norm else []

    def body(*refs):
        dy_refs, w_refs = refs[:n], refs[n:n + len(ws)]
        rest = refs[n + len(ws):]
        acc = None
        i = 0
        for (dys, _), w_ref in zip(groups, w_refs):
            off = 0
            for dy in dys:
                width = dy.shape[1]
                part = _dot_nt(dy_refs[i][...].astype(BF16), w_ref[:, off:off + width])
                acc = part if acc is None else acc + part
                off += width
                i += 1
        if norm:
            u_ref, g_ref, add_ref, o_ref, dg_ref = rest

            @pl.when(pl.program_id(0) == 0)
            def _():
                dg_ref[...] = jnp.zeros(dg_ref.shape, F32)

            du, dg_rows = _rms_bwd(acc, u_ref[...], g_ref[...])
            o_ref[...] = du + add_ref[...]
            dg_ref[...] += jnp.sum(dg_rows, axis=0, keepdims=True)
        else:
            rest[0][...] = acc

    n_out = 2 if norm else 1
    r_ins, r_in_specs, r_outs, r_out_specs, r_sems = _ride_args(ride)
    outs = pl.pallas_call(
        _riding(body, n + len(ws) + len(extra), n_out, 0, ride, 1), name=name, grid=(t // tm,),
        in_specs=[_rows(tm, dy.shape[1]) for dy in dys_all] + [_resident(w.shape) for w in ws]
        + ([_rows(tm, k), _whole((1, k)), _rows(tm, k)] if norm else []) + r_in_specs,
        out_specs=[_rows(tm, k)] + ([_whole((1, k))] if norm else []) + r_out_specs,
        out_shape=[jax.ShapeDtypeStruct((t, k), F32)] + ([jax.ShapeDtypeStruct((1, k), F32)] if norm else []) + r_outs,
        scratch_shapes=r_sems,
        compiler_params=_params(("arbitrary",)),
    )(*dys_all, *ws, *extra, *r_ins)
    return tuple(outs[:n_out]) + (_ride_results(ride, outs[n_out:]),)


def mm_tn(a, dys, name, col_shards=1, tm=1024):
    t, k = a.shape
    tm = _tile(t, tm)
    n = len(dys)
    ntot = sum(dy.shape[1] for dy in dys)
    wsh = ntot // col_shards

    def body(a_ref, *refs):
        dy_refs, o_ref, acc = refs[:n], refs[n], refs[n + 1]

        @pl.when(pl.program_id(0) == 0)
        def _():
            acc[...] = jnp.zeros(acc.shape, F32)

        av = a_ref[...].astype(BF16)
        off = 0
        for dy_ref in dy_refs:
            width = dy_ref.shape[1]
            acc[:, off:off + width] += _dot_tn(av, dy_ref[...].astype(BF16))
            off += width

        @pl.when(pl.program_id(0) == pl.num_programs(0) - 1)
        def _():
            for j in range(col_shards):
                o_ref[j] = acc[:, j * wsh:(j + 1) * wsh].astype(o_ref.dtype)

    return pl.pallas_call(
        body, name=name, grid=(t // tm,),
        in_specs=[_rows(tm, k)] + [_rows(tm, dy.shape[1]) for dy in dys],
        out_specs=_whole((col_shards, k, wsh)),
        out_shape=jax.ShapeDtypeStruct((col_shards, k, wsh), BF16),
        scratch_shapes=[pltpu.VMEM((k, ntot), F32)],
        compiler_params=_params(("arbitrary",)),
    )(a, *dys)


def _rms_bwd(dz, u, g):
    d = u.shape[-1]
    inv = lax.rsqrt(jnp.mean(u * u, axis=-1, keepdims=True) + EPS)
    dzg = dz * g
    proj = jnp.sum(dzg * u, axis=-1, keepdims=True) * (1.0 / d)
    du = inv * (dzg - u * (inv * inv) * proj)
    dg_rows = dz * u * inv
    return du, dg_rows


def ffn_down_loss(act, wd, g, h1, target, name, tm=512):
    t, f = act.shape
    d = wd.shape[1]
    tm = _tile(t, tm)

    def body(a_ref, w_ref, g_ref, h_ref, t_ref, dy_ref, dff_ref, dact_ref, dg_ref, loss_ref):
        @pl.when(pl.program_id(0) == 0)
        def _():
            dg_ref[...] = jnp.zeros(dg_ref.shape, F32)
            loss_ref[...] = jnp.zeros(loss_ref.shape, F32)

        wv = w_ref[...]
        gv = g_ref[...]
        ff = _dot(a_ref[...], wv)
        inv = lax.rsqrt(jnp.mean(ff * ff, axis=-1, keepdims=True) + EPS)
        err = h_ref[...] + ff * inv * gv - t_ref[...]
        loss_ref[...] += jnp.sum(err * err, axis=0, keepdims=True)
        dy = err * (1.0 / d)
        dy_ref[...] = dy
        du, dg_rows = _rms_bwd(dy, ff, gv)
        dff = du.astype(BF16)
        dff_ref[...] = dff
        dg_ref[...] += jnp.sum(dg_rows, axis=0, keepdims=True)
        dact_ref[...] = _dot_nt(dff, wv)

    return pl.pallas_call(
        body, name=name, grid=(t // tm,),
        in_specs=[_rows(tm, f), _resident(wd.shape), _whole(g.shape), _rows(tm, d), _rows(tm, d)],
        out_specs=[_rows(tm, d), _rows(tm, d), _rows(tm, f), _whole((1, d)), _whole((1, d))],
        out_shape=[jax.ShapeDtypeStruct((t, d), F32), jax.ShapeDtypeStruct((t, d), BF16),
                   jax.ShapeDtypeStruct((t, f), F32), jax.ShapeDtypeStruct((1, d), F32),
                   jax.ShapeDtypeStruct((1, d), F32)],
        compiler_params=_params(("arbitrary",)),
    )(act, wd, g, h1, target)


def merge_out(y_rnn, o_att, gts, w_br, w_ba, w_out, g, x, name, tm=256):
    t = y_rnn.shape[0]
    d = w_br.shape[1]
    tm = _tile(t, tm)

    def body(y_ref, o_ref, g_ref, wbr_ref, wba_ref, wo_ref, gn_ref, x_ref, m_ref, br_ref, ba_ref, mix_ref, h_ref):
        br = _dot(y_ref[...].astype(BF16), wbr_ref[...])
        ba = _dot(o_ref[...].astype(BF16), wba_ref[...])
        gv = g_ref[...]
        merged = (_sig(gv[:, :d]) * br + _sig(gv[:, d:]) * ba).astype(BF16)
        m_ref[...] = merged
        br_ref[...] = br
        ba_ref[...] = ba
        mix = _dot(merged, wo_ref[...])
        mix_ref[...] = mix
        inv = lax.rsqrt(jnp.mean(mix * mix, axis=-1, keepdims=True) + EPS)
        h_ref[...] = x_ref[...] + mix * inv * gn_ref[...]

    sd = jax.ShapeDtypeStruct
    return pl.pallas_call(
        body, name=name, grid=(t // tm,),
        in_specs=[_rows(tm, y_rnn.shape[1]), _rows(tm, o_att.shape[1]), _rows(tm, 2 * d),
                  _whole(w_br.shape), _whole(w_ba.shape), _whole(w_out.shape), _whole(g.shape), _rows(tm, d)],
        out_specs=[_rows(tm, d)] * 5,
        out_shape=[sd((t, d), BF16), sd((t, d), F32), sd((t, d), F32), sd((t, d), F32), sd((t, d), F32)],
        compiler_params=_params(("parallel",)),
    )(y_rnn, o_att, gts, w_br, w_ba, w_out, g, x)


def mid_bwd(dhn2, h1, g_ffn, dy, mix, g_mix, w_out, gts, br, ba, w_br, w_ba, merged, y_rnn, o_att, name, tm=256):
    t, d = h1.shape
    tm = _tile(t, tm)
    rnn, hkv = w_br.shape[0], w_ba.shape[0]
    wsh = d // N_CHIPS

    def body(dhn_ref, h_ref, gf_ref, dy_ref, mix_ref, gm_ref, wo_ref, g_ref, br_ref, ba_ref, wbr_ref, wba_ref,
             m_ref, y_ref, o_ref, dh_ref, dg_ref, dyr_ref, doa_ref, dgf_ref, dgm_ref, dwo_ref, dwbr_ref, dwba_ref,
             acc_o, acc_br, acc_ba):
        @pl.when(pl.program_id(0) == 0)
        def _():
            dgf_ref[...] = jnp.zeros(dgf_ref.shape, F32)
            dgm_ref[...] = jnp.zeros(dgm_ref.shape, F32)
            acc_o[...] = jnp.zeros(acc_o.shape, F32)
            acc_br[...] = jnp.zeros(acc_br.shape, F32)
            acc_ba[...] = jnp.zeros(acc_ba.shape, F32)

        du, rows_f = _rms_bwd(dhn_ref[...], h_ref[...], gf_ref[...])
        dh1 = du + dy_ref[...]
        dh_ref[...] = dh1
        dgf_ref[...] += jnp.sum(rows_f, axis=0, keepdims=True)
        dmx, rows_m = _rms_bwd(dh1, mix_ref[...], gm_ref[...])
        dmix = dmx.astype(BF16)
        acc_o[...] += _dot_tn(m_ref[...], dmix)
        dgm_ref[...] += jnp.sum(rows_m, axis=0, keepdims=True)
        dm = _dot_nt(dmix, wo_ref[...])
        gv = g_ref[...]
        sr = _sig(gv[:, :d])
        sa = _sig(gv[:, d:])
        dbr = (dm * sr).astype(BF16)
        dba = (dm * sa).astype(BF16)
        acc_br[...] += _dot_tn(y_ref[...].astype(BF16), dbr)
        acc_ba[...] += _dot_tn(o_ref[...].astype(BF16), dba)
        dg_ref[:, :d] = (dm * br_ref[...] * sr * (1.0 - sr)).astype(BF16)
        dg_ref[:, d:] = (dm * ba_ref[...] * sa * (1.0 - sa)).astype(BF16)
        dyr_ref[...] = _dot_nt(dbr, wbr_ref[...])
        doa_ref[...] = _dot_nt(dba, wba_ref[...])

        @pl.when(pl.program_id(0) == pl.num_programs(0) - 1)
        def _():
            dwo_ref[...] = acc_o[...].astype(BF16)
            dwbr_ref[...] = acc_br[...].astype(BF16)
            for j in range(N_CHIPS):
                dwba_ref[j] = acc_ba[:, j * wsh:(j + 1) * wsh].astype(BF16)

    sd = jax.ShapeDtypeStruct
    row, vec = _rows(tm, d), _whole((1, d))
    once = pl.Buffered(1)

    def resident(shape):
        return pl.BlockSpec(shape, lambda i: (0,) * len(shape), pipeline_mode=once)

    return pl.pallas_call(
        body, name=name, grid=(t // tm,),
        in_specs=[row, row, vec, row, row, vec, resident(w_out.shape), _rows(tm, 2 * d), row, row,
                  resident(w_br.shape), resident(w_ba.shape), row, _rows(tm, rnn), _rows(tm, hkv)],
        out_specs=[row, _rows(tm, 2 * d), _rows(tm, rnn), _rows(tm, hkv), vec, vec,
                   resident((d, d)), resident((rnn, d)), resident((N_CHIPS, hkv, wsh))],
        out_shape=[sd((t, d), F32), sd((t, 2 * d), BF16), sd((t, rnn), F32), sd((t, hkv), F32), sd((1, d), F32),
                   sd((1, d), F32), sd((d, d), BF16), sd((rnn, d), BF16), sd((N_CHIPS, hkv, wsh), BF16)],
        scratch_shapes=[pltpu.VMEM((d, d), F32), pltpu.VMEM((rnn, d), F32), pltpu.VMEM((hkv, d), F32)],
        compiler_params=_params(("arbitrary",)),
    )(dhn2, h1, g_ffn, dy, mix, g_mix, w_out, gts, br, ba, w_br, w_ba, merged, y_rnn, o_att)


def _shift_dn(x, d, fill, row):
    return jnp.where(row >= d, pltpu.roll(x, d, 0), fill)


def _shift_up(x, d, fill, row):
    s = x.shape[0]
    return jnp.where(row < s - d, pltpu.roll(x, s - d, 0), fill)


def _conv_fwd(x, w, b, row):
    kk = w.shape[0]
    y = b + w[kk - 1:kk, :] * x
    for j in range(1, kk):
        y = y + w[kk - 1 - j:kk - j, :] * _shift_dn(x, j, 0.0, row)
    return y


def _conv_bwd(dy, x, w, row):
    kk = w.shape[0]
    dx = w[kk - 1:kk, :] * dy
    dws = [None] * kk
    dws[kk - 1] = jnp.sum(dy * x, axis=0, keepdims=True)
    for j in range(1, kk):
        dx = dx + w[kk - 1 - j:kk - j, :] * _shift_up(dy, j, 0.0, row)
        dws[kk - 1 - j] = jnp.sum(dy * _shift_dn(x, j, 0.0, row), axis=0, keepdims=True)
    return dx, jnp.concatenate(dws, axis=0)


def _softplus(z):
    y = jnp.exp(-jnp.abs(z))
    u = 1.0 + y
    dd = u - 1.0
    log1p = jnp.where(dd == 0.0, y, jnp.log(u) * (y / jnp.where(dd == 0.0, 1.0, dd)))
    return jnp.maximum(z, 0.0) + log1p


def _lru_decay(xb, wa, ba, lam):
    r = _sig(_dot(xb, wa) + ba)
    sp = _softplus(-lam)
    la = (-LRU_C) * r * sp
    return r, sp, la, jnp.exp(la)


def _lru_gates(xc, wa, ba, wx, bx, lam):
    xb = xc.astype(BF16)
    r, sp, la, a = _lru_decay(xb, wa, ba, lam)
    i = _sig(_dot(xb, wx) + bx)
    one_m_a2 = jnp.tanh(-la) * (1.0 + a * a)
    inv_mult = lax.rsqrt(one_m_a2)
    return r, i, sp, a, one_m_a2 * inv_mult, inv_mult


def _seg_len(s):
    seg = -(-s // 8)
    return seg + (4 - seg % 8) % 8


def _scan_rows(a_pad, u_pad, out_pad, reverse):
    planes, rows8, lanes = a_pad.shape
    seg = rows8 // 8
    sub = lax.broadcasted_iota(jnp.int32, (planes, 8, lanes), 1)

    unroll = 4

    def rows(k, d):
        i = k * unroll + d
        return pl.ds((seg - 1 - i) if reverse else i, 8, stride=seg)

    def ends(k, carry):
        h, p = carry
        for d in range(unroll):
            a = a_pad[:, rows(k, d), :]
            h = a * h + u_pad[:, rows(k, d), :]
            p = a * p
        return h, p

    init = (jnp.zeros((planes, 8, lanes), F32), jnp.ones((planes, 8, lanes), F32))
    h_end, p_end = lax.fori_loop(0, seg // unroll, ends, init)
    start = jnp.zeros((planes, 8, lanes), F32)
    for _ in range(7):
        nxt = h_end + p_end * start
        if reverse:
            start = jnp.where(sub < 7, pltpu.roll(nxt, 7, 1), 0.0)
        else:
            start = jnp.where(sub >= 1, pltpu.roll(nxt, 1, 1), 0.0)

    def redo(k, h):
        for d in range(unroll):
            h = a_pad[:, rows(k, d), :] * h + u_pad[:, rows(k, d), :]
            out_pad[:, rows(k, d), :] = h
        return h

    lax.fori_loop(0, seg // unroll, redo, start)


def _lru_cols(c, rb):
    return 2 * rb if c % (2 * rb) == 0 else rb


def rglru_fwd(xr, cw, cb, wa, ba, wx, bx, lam, name, ride=()):
    b, s, c = xr.shape
    rb = wa.shape[1]
    kk = cw.shape[0]
    cols = _lru_cols(c, rb)
    nj = cols // rb
    seg = _seg_len(s)

    def body(x_ref, cw_ref, cb_ref, wa_ref, ba_ref, wx_ref, bx_ref, lam_ref, h_ref, a_ref, xc_ref, a_pad, u_pad, h_pad):
        row = lax.broadcasted_iota(jnp.int32, (s, rb), 0)
        for j in range(nj):
            cs = slice(j * rb, (j + 1) * rb)
            xc = _conv_fwd(x_ref[:, cs], cw_ref[:, cs], cb_ref[:, cs], row)
            _, i, _, a, mult, _ = _lru_gates(xc, wa_ref[j], ba_ref[:, cs], wx_ref[j], bx_ref[:, cs], lam_ref[:, cs])
            xc_ref[:, cs] = xc
            a_ref[:, cs] = a
            a_pad[j, 0:s, :] = a
            u_pad[j, 0:s, :] = mult * (i * xc)
        a_pad[:, s:, :] = jnp.ones((nj, 8 * seg - s, rb), F32)
        u_pad[:, s:, :] = jnp.zeros((nj, 8 * seg - s, rb), F32)
        _scan_rows(a_pad, u_pad, h_pad, False)
        for j in range(nj):
            h_ref[:, j * rb:(j + 1) * rb] = h_pad[j, 0:s, :]

    vec = pl.BlockSpec((1, cols), lambda bi, n: (0, n))
    seq = pl.BlockSpec((None, s, cols), lambda bi, n: (bi, 0, n))
    mat = pl.BlockSpec((nj, rb, rb), lambda bi, n: (n, 0, 0))
    r_ins, r_in_specs, r_outs, r_out_specs, r_sems = _ride_args(ride)
    outs = pl.pallas_call(
        _riding(body, 8, 3, 3, ride, 2), name=name, grid=(b, c // cols),
        in_specs=[seq, pl.BlockSpec((kk, cols), lambda bi, n: (0, n)), vec, mat, vec, mat, vec, vec] + r_in_specs,
        out_specs=[seq] * 3 + r_out_specs,
        out_shape=[jax.ShapeDtypeStruct((b, s, c), F32)] * 3 + r_outs,
        scratch_shapes=[pltpu.VMEM((nj, 8 * seg, rb), F32)] * 3 + r_sems,
        compiler_params=_params(("arbitrary", "arbitrary")),
    )(xr, cw, cb, wa, ba, wx, bx, lam, *r_ins)
    return outs[:3], _ride_results(ride, outs[3:])


def rglru_bwd(xr, h, dh, a_fwd, xc_fwd, cw, cb, wa, ba, wx, bx, lam, name, ride=()):
    b, s, c = xr.shape
    nb, rb = wa.shape[0], wa.shape[1]
    kk = cw.shape[0]
    cols = _lru_cols(c, rb)
    nj = cols // rb
    seg = _seg_len(s)

    def body(x_ref, h_ref, dh_ref, a_ref, xc_ref, cw_ref, cb_ref, wa_ref, ba_ref, wx_ref, bx_ref, lam_ref,
             dx_ref, dcw_ref, dcb_ref, dwa_ref, dba_ref, dwx_ref, dbx_ref, dlam_ref, b_pad, g_pad, l_pad):
        @pl.when(pl.program_id(1) == 0)
        def _():
            for ref in (dcw_ref, dcb_ref, dwa_ref, dba_ref, dwx_ref, dbx_ref, dlam_ref):
                ref[...] = jnp.zeros(ref.shape, F32)

        row = lax.broadcasted_iota(jnp.int32, (s, rb), 0)

        for j in range(nj):
            b_pad[j, 0:s, :] = _shift_up(a_ref[:, j * rb:(j + 1) * rb], 1, 0.0, row)
            g_pad[j, 0:s, :] = dh_ref[:, j * rb:(j + 1) * rb]
        b_pad[:, s:, :] = jnp.zeros((nj, 8 * seg - s, rb), F32)
        g_pad[:, s:, :] = jnp.zeros((nj, 8 * seg - s, rb), F32)
        _scan_rows(b_pad, g_pad, l_pad, True)

        for j in range(nj):
            cs = slice(j * rb, (j + 1) * rb)
            x = x_ref[:, cs]
            cwv = cw_ref[:, cs]
            wav, wxv, lamv = wa_ref[j], wx_ref[j], lam_ref[:, cs]
            xc = xc_ref[:, cs]
            r, i, sp, a, mult, inv_mult = _lru_gates(xc, wav, ba_ref[:, cs], wxv, bx_ref[:, cs], lamv)
            lmb = l_pad[j, 0:s, :]
            h_prev = _shift_dn(h_ref[:, cs], 1, 0.0, row)
            da = lmb * h_prev
            ixc = i * xc
            dla = da * a - (lmb * ixc) * (a * a) * inv_mult
            di = lmb * mult * xc
            dxc = lmb * mult * i
            dr = dla * ((-LRU_C) * sp)
            dsp = jnp.sum(dla * ((-LRU_C) * r), axis=0, keepdims=True)
            dga = dr * r * (1.0 - r)
            dgx = di * i * (1.0 - i)
            dga_b, dgx_b = dga.astype(BF16), dgx.astype(BF16)
            xb = xc.astype(BF16)
            dwa_ref[j] += _dot_tn(xb, dga_b)
            dwx_ref[j] += _dot_tn(xb, dgx_b)
            dba_ref[:, cs] += jnp.sum(dga, axis=0, keepdims=True)
            dbx_ref[:, cs] += jnp.sum(dgx, axis=0, keepdims=True)
            dlam_ref[:, cs] += dsp * (-_sig(-lamv))
            dxc = dxc + _dot_nt(dga_b, wav) + _dot_nt(dgx_b, wxv)
            dcb_ref[:, cs] += jnp.sum(dxc, axis=0, keepdims=True)
            dx, dcw = _conv_bwd(dxc, x, cwv, row)
            dcw_ref[:, cs] += dcw
            dx_ref[:, cs] = dx.astype(dx_ref.dtype)

    vec = pl.BlockSpec((1, cols), lambda n, bi: (0, n))
    seq = pl.BlockSpec((None, s, cols), lambda n, bi: (bi, 0, n))
    mat = pl.BlockSpec((nj, rb, rb), lambda n, bi: (n, 0, 0))
    cws = pl.BlockSpec((kk, cols), lambda n, bi: (0, n))
    sd = jax.ShapeDtypeStruct
    r_ins, r_in_specs, r_outs, r_out_specs, r_sems = _ride_args(ride)
    outs = pl.pallas_call(
        _riding(body, 12, 8, 3, ride, 2), name=name, grid=(c // cols, b),
        in_specs=[seq, seq, seq, seq, seq, cws, vec, mat, vec, mat, vec, vec] + r_in_specs,
        out_specs=[seq, cws, vec, mat, vec, mat, vec, vec] + r_out_specs,
        out_shape=[sd((b, s, c), BF16), sd((kk, c), F32), sd((1, c), F32), sd((nb, rb, rb), F32),
                   sd((1, c), F32), sd((nb, rb, rb), F32), sd((1, c), F32), sd((1, c), F32)] + r_outs,
        scratch_shapes=[pltpu.VMEM((nj, 8 * seg, rb), F32)] * 3 + r_sems,
        compiler_params=_params(("arbitrary", "arbitrary")),
    )(xr, h, dh, a_fwd, xc_fwd, cw, cb, wa, ba, wx, bx, lam, *r_ins)
    return outs[:8], _ride_results(ride, outs[8:])


_GELU_C = math.sqrt(2.0 / math.pi)


def _gelu_parts(x):
    th = jnp.tanh(_GELU_C * (x + 0.044715 * x * x * x))
    gel = 0.5 * x * (1.0 + th)
    dgel = 0.5 * (1.0 + th) + 0.5 * x * (1.0 - th * th) * _GELU_C * (1.0 + 3 * 0.044715 * x * x)
    return gel, dgel


def ffn_in_act(x, g, wg, wu, cw, cb, seq_len, name, tm=256):
    t, d = x.shape
    f = wg.shape[1]
    kk = cw.shape[0]
    tm = _tile(seq_len, tm)
    tiles_per_seq = seq_len // tm
    keep = 8
    assert kk - 1 <= keep

    def body(x_ref, g_ref, wg_ref, wu_ref, cw_ref, cb_ref, hn_ref, gp_ref, up_ref, act_ref, tail):
        @pl.when(pl.program_id(0) % tiles_per_seq == 0)
        def _():
            tail[...] = jnp.zeros(tail.shape, F32)

        xv = x_ref[...]
        inv = lax.rsqrt(jnp.mean(xv * xv, axis=-1, keepdims=True) + EPS)
        hn = (xv * inv * g_ref[...]).astype(BF16)
        hn_ref[...] = hn
        gp = _dot(hn, wg_ref[...])
        up = _dot(hn, wu_ref[...])
        gp_ref[...] = gp
        up_ref[...] = up
        cwv = cw_ref[...]
        row = lax.broadcasted_iota(jnp.int32, (tm, 1), 0)
        gate = _conv_fwd(gp, cwv, cb_ref[...], row)
        row8 = lax.broadcasted_iota(jnp.int32, (keep, 1), 0)
        prev = tail[...]
        fix = jnp.zeros((keep, f), F32)
        for j in range(1, kk):
            fix = fix + cwv[kk - 1 - j:kk - j, :] * jnp.where(row8 < j, pltpu.roll(prev, j, 0), 0.0)
        gate = jnp.concatenate([gate[:keep] + fix, gate[keep:]], axis=0)
        tail[...] = gp[tm - keep:, :]
        gel, _ = _gelu_parts(gate)
        act_ref[...] = (gel * up).astype(BF16)

    sd = jax.ShapeDtypeStruct
    return pl.pallas_call(
        body, name=name, grid=(t // tm,),
        in_specs=[_rows(tm, d), _whole(g.shape), _whole(wg.shape), _whole(wu.shape), _whole(cw.shape), _whole(cb.shape)],
        out_specs=[_rows(tm, d), _rows(tm, f), _rows(tm, f), _rows(tm, f)],
        out_shape=[sd((t, d), BF16), sd((t, f), F32), sd((t, f), F32), sd((t, f), BF16)],
        scratch_shapes=[pltpu.VMEM((keep, f), F32)],
        compiler_params=_params(("arbitrary",)),
    )(x, g, wg, wu, cw, cb)


def ffn_in_bwd(dact, gate_pre, up, cw, cb, wg, wu, seq_len, name, ride=(), tm=256):
    t, f = gate_pre.shape
    d = wg.shape[0]
    kk = cw.shape[0]
    tm = _tile(seq_len, tm)
    nt = t // tm
    tiles_per_seq = seq_len // tm
    keep = 8
    assert kk - 1 <= keep

    def body(da_ref, g_ref, halo_ref, u_ref, cw_ref, cb_ref, wg_ref, wu_ref,
             dg_ref, du_ref, dhn_ref, dcw_ref, dcb_ref, nxt):
        tile = (nt - 1 - pl.program_id(0)) % tiles_per_seq

        @pl.when(pl.program_id(0) == 0)
        def _():
            dcw_ref[...] = jnp.zeros(dcw_ref.shape, F32)
            dcb_ref[...] = jnp.zeros(dcb_ref.shape, F32)

        @pl.when(tile == tiles_per_seq - 1)
        def _():
            nxt[...] = jnp.zeros(nxt.shape, F32)

        row = lax.broadcasted_iota(jnp.int32, (tm, 1), 0)
        row8 = lax.broadcasted_iota(jnp.int32, (keep, 1), 0)
        gp = g_ref[...]
        cwv = cw_ref[...]
        prev = jnp.where(tile > 0, halo_ref[...], 0.0)
        gate = _conv_fwd(gp, cwv, cb_ref[...], row)
        fix = jnp.zeros((keep, f), F32)
        for j in range(1, kk):
            fix = fix + cwv[kk - 1 - j:kk - j, :] * jnp.where(row8 < j, pltpu.roll(prev, j, 0), 0.0)
        gate = jnp.concatenate([gate[:keep] + fix, gate[keep:]], axis=0)
        gel, dgel = _gelu_parts(gate)
        da = da_ref[...]
        dup = (da * gel).astype(BF16)
        du_ref[...] = dup
        dgate = da * u_ref[...] * dgel
        dcb_ref[...] += jnp.sum(dgate, axis=0, keepdims=True)
        after = nxt[...]
        dgp = cwv[kk - 1:kk, :] * dgate
        tail_fix = jnp.zeros((keep, f), F32)
        dws = [None] * kk
        dws[kk - 1] = jnp.sum(dgate * gp, axis=0, keepdims=True)
        for j in range(1, kk):
            wj = cwv[kk - 1 - j:kk - j, :]
            dgp = dgp + wj * _shift_up(dgate, j, 0.0, row)
            tail_fix = tail_fix + wj * jnp.where(row8 >= keep - j, pltpu.roll(after, keep - j, 0), 0.0)
            dws[kk - 1 - j] = (jnp.sum(dgate * _shift_dn(gp, j, 0.0, row), axis=0, keepdims=True)
                               + jnp.sum(dgate[:keep] * jnp.where(row8 < j, pltpu.roll(prev, j, 0), 0.0),
                                         axis=0, keepdims=True))
        dgp = jnp.concatenate([dgp[:tm - keep], dgp[tm - keep:] + tail_fix], axis=0).astype(BF16)
        nxt[...] = dgate[:keep]
        dcw_ref[...] += jnp.concatenate(dws, axis=0)
        dg_ref[...] = dgp
        dhn_ref[...] = _dot_nt(dgp, wg_ref[...]) + _dot_nt(dup, wu_ref[...])

    def rev(i):
        return nt - 1 - i

    rows_f = pl.BlockSpec((tm, f), lambda i: (rev(i), 0))
    halo = pl.BlockSpec((None, keep, f), lambda i: (jnp.maximum(rev(i) * (tm // keep) - 1, 0), 0, 0))
    once = pl.Buffered(1)
    sd = jax.ShapeDtypeStruct
    r_ins, r_in_specs, r_outs, r_out_specs, r_sems = _ride_args(ride)
    outs = pl.pallas_call(
        _riding(body, 8, 5, 1, ride, 1), name=name, grid=(nt,),
        in_specs=[rows_f, rows_f, halo, rows_f, _whole(cw.shape), _whole(cb.shape),
                  pl.BlockSpec(wg.shape, lambda i: (0, 0), pipeline_mode=once),
                  pl.BlockSpec(wu.shape, lambda i: (0, 0), pipeline_mode=once)] + r_in_specs,
        out_specs=[rows_f, rows_f, pl.BlockSpec((tm, d), lambda i: (rev(i), 0)), _whole((kk, f)), _whole((1, f))]
        + r_out_specs,
        out_shape=[sd((t, f), BF16), sd((t, f), BF16), sd((t, d), F32), sd((kk, f), F32), sd((1, f), F32)] + r_outs,
        scratch_shapes=[pltpu.VMEM((keep, f), F32)] + r_sems,
        compiler_params=_params(("arbitrary",)),
    )(dact, gate_pre, gate_pre.reshape(t // keep, keep, f), up, cw, cb, wg, wu, *r_ins)
    return outs[:5], _ride_results(ride, outs[5:])


def _t5_bucket(dist):
    max_exact = REL_BUCKETS // 2
    d = np.maximum(dist, 1).astype(np.float32)
    large = max_exact + np.log(d / max_exact) / math.log(REL_MAX_DIST / max_exact) * (REL_BUCKETS - max_exact)
    large = np.minimum(large.astype(np.int32), REL_BUCKETS - 1)
    return np.where(dist < max_exact, dist, large).astype(np.int32)


def _band(window, dilation):
    qi = np.arange(ATTN_BLOCK)[:, None]
    kj = np.arange(2 * ATTN_BLOCK)[None, :]
    delta = ATTN_BLOCK + qi - kj
    mask = (delta >= 0) & (delta <= window // dilation)
    bucket = _t5_bucket(np.maximum(delta, 0) * dilation)
    return mask, bucket


def _attn_blocks(s, r):
    m = s // r
    assert m % ATTN_BLOCK == 0, "sequence length must be a multiple of dilation * block"
    return m // ATTN_BLOCK


def _perm_load(ref, r):
    if r == 1:
        return ref[...]
    m = ref.shape[0] // r
    return jnp.concatenate([ref[pl.ds(c, m, stride=r), :] for c in range(r)], axis=0)


def _perm_store(ref, g, val, r, add=False):
    if r == 1:
        ref[g] = ref[g] + val if add else val
        return
    m = val.shape[0] // r
    for c in range(r):
        rows = pl.ds(c, m, stride=r)
        part = val[c * m:(c + 1) * m]
        ref[g, rows, :] = ref[g, rows, :] + part if add else part


def _blocks(x):
    return x.reshape(x.shape[0] // ATTN_BLOCK, ATTN_BLOCK, x.shape[1])


def _prev_blocks(x):
    return jnp.concatenate([x[:1], x[:-1]], axis=0)


def _next_blocks(x):
    return jnp.concatenate([x[1:], jnp.zeros_like(x[:1])], axis=0)


def _first_block_neg(s, r):
    nblk = s // ATTN_BLOCK
    idx = lax.broadcasted_iota(jnp.int32, (nblk, 1, 1), 0)
    return jnp.where(idx % _attn_blocks(s, r) == 0, NEG, 0.0)


def _bdot_nt(a, b):
    return lax.dot_general(a, b, (((2,), (2,)), ((0,), (0,))), preferred_element_type=F32)


def _bdot(a, b):
    return lax.dot_general(a, b, (((2,), (1,)), ((0,), (0,))), preferred_element_type=F32)


def _bdot_tn(a, b):
    return lax.dot_general(a, b, (((1,), (1,)), ((0,), (0,))), preferred_element_type=F32)


def attn_fwd(qkv, biasm, n_heads, name, ride=()):
    b, s, _ = qkv.shape
    h = n_heads
    scale = HEAD_DIM ** -0.5
    blk = ATTN_BLOCK

    def body(q1_ref, q2_ref, q3_ref, k_ref, v_ref, bias_ref, o_ref, lse_ref, acc, m_s, l_s):
        for g, q_ref in enumerate((q1_ref, q2_ref, q3_ref)):
            r = DILATED[g][1]
            first = _first_block_neg(s, r)
            q = _blocks(_perm_load(q_ref, r).astype(BF16))
            k = _blocks(_perm_load(k_ref, r).astype(BF16))
            v = _blocks(_perm_load(v_ref, r).astype(BF16))
            s_cur = _bdot_nt(q, k) * scale + bias_ref[g, :, blk:]
            s_prev = _bdot_nt(q, _prev_blocks(k)) * scale + bias_ref[g, :, :blk] + first
            m = jnp.max(jnp.maximum(s_cur, s_prev), axis=-1, keepdims=True)
            p_cur = jnp.exp(s_cur - m)
            p_prev = jnp.exp(s_prev - m)
            l = jnp.sum(p_cur + p_prev, axis=-1, keepdims=True)
            o = _bdot(p_cur.astype(BF16), v) + _bdot(p_prev.astype(BF16), _prev_blocks(v))
            _perm_store(acc, g, o.reshape(s, HEAD_DIM), r)
            _perm_store(m_s, g, m.reshape(s, 1), r)
            _perm_store(l_s, g, l.reshape(s, 1), r)
        m_all = jnp.maximum(jnp.maximum(m_s[0], m_s[1]), m_s[2])
        w = [jnp.exp(m_s[g] - m_all) for g in range(N_GROUPS)]
        l = w[0] * l_s[0] + w[1] * l_s[1] + w[2] * l_s[2]
        o_ref[...] = (w[0] * acc[0] + w[1] * acc[1] + w[2] * acc[2]) / l
        lse_ref[...] = m_all + jnp.log(l)

    def col(j):
        return pl.BlockSpec((None, s, HEAD_DIM), lambda bi, hi, j=j: (bi, 0, j * h + hi))

    r_ins, r_in_specs, r_outs, r_out_specs, r_sems = _ride_args(ride)
    outs = pl.pallas_call(
        _riding(body, 6, 2, 3, ride, 2), name=name, grid=(b, h),
        in_specs=[col(0), col(1), col(2), col(3), col(4),
                  pl.BlockSpec((N_GROUPS, None, blk, 2 * blk), lambda bi, hi: (0, hi, 0, 0))] + r_in_specs,
        out_specs=[pl.BlockSpec((None, s, HEAD_DIM), lambda bi, hi: (bi, 0, hi)),
                   pl.BlockSpec((None, None, s, 1), lambda bi, hi: (bi, hi, 0, 0))] + r_out_specs,
        out_shape=[jax.ShapeDtypeStruct((b, s, h * HEAD_DIM), F32), jax.ShapeDtypeStruct((b, h, s, 1), F32)] + r_outs,
        scratch_shapes=[pltpu.VMEM((N_GROUPS, s, HEAD_DIM), F32), pltpu.VMEM((N_GROUPS, s, 1), F32),
                        pltpu.VMEM((N_GROUPS, s, 1), F32)] + r_sems,
        compiler_params=_params(("arbitrary", "arbitrary")),
    )(qkv, qkv, qkv, qkv, qkv, biasm, *r_ins)
    return outs[0], outs[1], _ride_results(ride, outs[2:])


def attn_bwd(qkv, biasm, o, lse, do, n_heads, name, ride=()):
    b, s, _ = qkv.shape
    h = n_heads
    scale = HEAD_DIM ** -0.5
    blk = ATTN_BLOCK

    def body(q1_ref, q2_ref, q3_ref, k_ref, v_ref, bias_ref, o_ref, lse_ref, do_ref,
             dq1_ref, dq2_ref, dq3_ref, dk_ref, dv_ref, ds_ref, dq_acc, kv_acc, delta):
        delta[...] = jnp.sum(do_ref[...] * o_ref[...], axis=-1, keepdims=True)
        kv_acc[...] = jnp.zeros(kv_acc.shape, F32)
        for g, q_ref in enumerate((q1_ref, q2_ref, q3_ref)):
            r = DILATED[g][1]
            first = _first_block_neg(s, r)
            q = _blocks(_perm_load(q_ref, r).astype(BF16))
            k = _blocks(_perm_load(k_ref, r).astype(BF16))
            v = _blocks(_perm_load(v_ref, r).astype(BF16))
            dob = _blocks(_perm_load(do_ref, r).astype(BF16))
            lse_b = _blocks(_perm_load(lse_ref, r))
            dl_b = _blocks(_perm_load(delta, r))
            k_prev, v_prev = _prev_blocks(k), _prev_blocks(v)
            p_cur = jnp.exp(_bdot_nt(q, k) * scale + bias_ref[g, :, blk:] - lse_b)
            p_prev = jnp.exp(_bdot_nt(q, k_prev) * scale + bias_ref[g, :, :blk] + first - lse_b)
            ds_cur = p_cur * (_bdot_nt(dob, v) - dl_b)
            ds_prev = p_prev * (_bdot_nt(dob, v_prev) - dl_b)
            ds_ref[g, :, blk:] = jnp.sum(ds_cur, axis=0)
            ds_ref[g, :, :blk] = jnp.sum(ds_prev, axis=0)
            ds_cur_b, ds_prev_b = ds_cur.astype(BF16), ds_prev.astype(BF16)
            dq = (_bdot(ds_cur_b, k) + _bdot(ds_prev_b, k_prev)) * scale
            _perm_store(dq_acc, g, dq.reshape(s, HEAD_DIM), r)
            dk = (_bdot_tn(ds_cur_b, q) + _next_blocks(_bdot_tn(ds_prev_b, q))) * scale
            dv = _bdot_tn(p_cur.astype(BF16), dob) + _next_blocks(_bdot_tn(p_prev.astype(BF16), dob))
            _perm_store(kv_acc, 0, dk.reshape(s, HEAD_DIM), r, add=True)
            _perm_store(kv_acc, 1, dv.reshape(s, HEAD_DIM), r, add=True)
        for g, out_ref in enumerate((dq1_ref, dq2_ref, dq3_ref)):
            out_ref[...] = dq_acc[g].astype(out_ref.dtype)
        dk_ref[...] = kv_acc[0].astype(dk_ref.dtype)
        dv_ref[...] = kv_acc[1].astype(dv_ref.dtype)

    def col(j):
        return pl.BlockSpec((None, s, HEAD_DIM), lambda bi, hi, j=j: (bi, 0, j * h + hi))

    head = pl.BlockSpec((None, s, HEAD_DIM), lambda bi, hi: (bi, 0, hi))
    sd = jax.ShapeDtypeStruct
    r_ins, r_in_specs, r_outs, r_out_specs, r_sems = _ride_args(ride)
    outs = pl.pallas_call(
        _riding(body, 9, 6, 3, ride, 2), name=name, grid=(b, h),
        in_specs=[col(0), col(1), col(2), col(3), col(4),
                  pl.BlockSpec((N_GROUPS, None, blk, 2 * blk), lambda bi, hi: (0, hi, 0, 0)),
                  head, pl.BlockSpec((None, None, s, 1), lambda bi, hi: (bi, hi, 0, 0)), head] + r_in_specs,
        out_specs=[head] * 5 + [pl.BlockSpec((None, None, N_GROUPS, blk, 2 * blk), lambda bi, hi: (bi, hi, 0, 0, 0))]
        + r_out_specs,
        out_shape=[sd((b, s, h * HEAD_DIM), BF16)] * 5 + [sd((b, h, N_GROUPS, blk, 2 * blk), F32)] + r_outs,
        scratch_shapes=[pltpu.VMEM((N_GROUPS, s, HEAD_DIM), F32), pltpu.VMEM((2, s, HEAD_DIM), F32),
                        pltpu.VMEM((s, 1), F32)] + r_sems,
        compiler_params=_params(("arbitrary", "arbitrary")),
    )(qkv, qkv, qkv, qkv, qkv, biasm, o, lse, do, *r_ins)
    return outs[:6], _ride_results(ride, outs[6:])


def bias_table(rel_rows, bucket_f, n_heads, name):
    g, blk, blk2 = bucket_f.shape
    h = n_heads

    def body(rb_ref, bk_ref, o_ref):
        bk = bk_ref[...]
        rb = rb_ref[...]
        acc = jnp.full((blk, blk2), NEG, F32)
        for bucket in range(REL_BUCKETS):
            acc = jnp.where(bk == float(bucket), rb[:, bucket:bucket + 1], acc)
        o_ref[...] = acc

    return pl.pallas_call(
        body, name=name, grid=(g, h),
        in_specs=[pl.BlockSpec((None, 1, 128), lambda gi, hi: (gi * h + hi, 0, 0)),
                  pl.BlockSpec((None, blk, blk2), lambda gi, hi: (gi, 0, 0))],
        out_specs=pl.BlockSpec((None, None, blk, blk2), lambda gi, hi: (gi, hi, 0, 0)),
        out_shape=jax.ShapeDtypeStruct((g, h, blk, blk2), F32),
        compiler_params=_params(("parallel", "parallel")),
    )(rel_rows, bucket_f)


def bias_grad(ds_sum, bucket_f, name):
    b, h, g, blk, blk2 = ds_sum.shape

    def body(ds_ref, bk_ref, o_ref):
        tot = jnp.sum(ds_ref[...], axis=0)
        bk = bk_ref[...]
        lane = lax.broadcasted_iota(jnp.int32, (1, 128), 1)
        vec = jnp.zeros((1, 128), F32)
        for bucket in range(REL_BUCKETS):
            val = jnp.sum(jnp.where(bk == float(bucket), tot, 0.0), keepdims=True)
            vec = vec + jnp.where(lane == bucket, val, 0.0)
        o_ref[...] = vec

    return pl.pallas_call(
        body, name=name, grid=(g, h),
        in_specs=[pl.BlockSpec((b, None, None, blk, blk2), lambda gi, hi: (0, hi, gi, 0, 0)),
                  pl.BlockSpec((None, blk, blk2), lambda gi, hi: (gi, 0, 0))],
        out_specs=pl.BlockSpec((None, 1, 128), lambda gi, hi: (gi * h + hi, 0, 0)),
        out_shape=jax.ShapeDtypeStruct((g * h, 1, 128), F32),
        compiler_params=_params(("parallel", "parallel")),
    )(ds_sum, bucket_f)


def _chip_peers():
    x, y, c = lax.axis_index("x"), lax.axis_index("y"), lax.axis_index("c")
    me = 2 * x + y
    peers = [(1 - x, y, c), (x, 1 - y, c), (1 - x, 1 - y, c)]
    peer_chip = [2 * (1 - x) + y, 2 * x + (1 - y), 2 * (1 - x) + (1 - y)]
    return me, peers, peer_chip


def _any_specs(n):
    return [pl.BlockSpec(memory_space=pl.ANY)] * n


_MID_NUM, _MID_DEN = 3, 4


class _Exchange:
    def start(self, ins, outs, sems):
        local, sends, _ = self._copies(ins, outs, sems)
        for cp in local + sends:
            cp.start()

    def mid(self, ins, outs, sems):
        pass

    def wait(self, ins, outs, sems):
        local, sends, recvs = self._copies(ins, outs, sems)
        for cp in recvs():
            cp.wait_recv()
        for cp in sends:
            cp.wait_send()
        for cp in local:
            cp.wait()


class _Gather(_Exchange):
    HALF_ROWS = 16

    def __init__(self, arrays):
        n = len(arrays)
        self.ins = list(arrays)
        self.split = [a.shape[0] % (2 * self.HALF_ROWS) == 0 for a in arrays]
        self.out_shape = [jax.ShapeDtypeStruct((N_CHIPS,) + a.shape, a.dtype) for a in arrays]
        dma = pltpu.SemaphoreType.DMA
        self.sems = [dma((3 * n,)), dma((3 * n,)), dma((n,)), dma((3 * n,)), dma((3 * n,))]

    def _half(self, i, ref, sibling=False):
        if not self.split[i]:
            return ref
        half = self.ins[i].shape[0] // 2
        c = lax.axis_index("c")
        c = 1 - c if sibling else c
        return ref.at[pl.ds(pl.multiple_of(c * half, self.HALF_ROWS), half)]

    def _plan(self, ins, outs, sems):
        send1, recv1, local_sems, send2, recv2 = sems
        me, peers, peer_chip = _chip_peers()
        x, y, c = lax.axis_index("x"), lax.axis_index("y"), lax.axis_index("c")
        n = len(ins)
        pairs = [(i, k) for i in range(n) for k in range(3)]

        def fetch(i, k, slot):
            return pltpu.make_async_remote_copy(src_ref=self._half(i, ins[i]), dst_ref=self._half(i, outs[i].at[slot]),
                                                send_sem=send1.at[3 * i + k], recv_sem=recv1.at[3 * i + k],
                                                device_id=peers[k], device_id_type=MESH)

        def share(i, k, sibling):
            part = self._half(i, outs[i].at[peer_chip[k]], sibling)
            return pltpu.make_async_remote_copy(src_ref=part, dst_ref=part, send_sem=send2.at[3 * i + k],
                                                recv_sem=recv2.at[3 * i + k], device_id=(x, y, 1 - c),
                                                device_id_type=MESH)

        split_pairs = [(i, k) for i, k in pairs if self.split[i]]
        return dict(
            local=lambda: [pltpu.make_async_copy(ins[i], outs[i].at[me], local_sems.at[i]) for i in range(n)],
            fetch_out=lambda: [fetch(i, k, me) for i, k in pairs],
            fetch_in=lambda: [fetch(i, k, peer_chip[k]) for i, k in pairs],
            share_out=lambda: [share(i, k, False) for i, k in split_pairs],
            share_in=lambda: [share(i, k, True) for i, k in split_pairs])

    def start(self, ins, outs, sems):
        plan = self._plan(ins, outs, sems)
        for cp in plan["local"]() + plan["fetch_out"]():
            cp.start()

    def mid(self, ins, outs, sems):
        plan = self._plan(ins, outs, sems)
        for cp in plan["fetch_in"]():
            cp.wait_recv()
        for cp in plan["share_out"]():
            cp.start()

    def wait(self, ins, outs, sems):
        plan = self._plan(ins, outs, sems)
        for cp in plan["share_in"]():
            cp.wait_recv()
        for cp in plan["fetch_out"]() + plan["share_out"]():
            cp.wait_send()
        for cp in plan["local"]():
            cp.wait()


class _Scatter(_Exchange):
    def __init__(self, slabs, whole=()):
        self.n_slabs = len(slabs)
        self.ins = list(slabs) + list(whole)
        n = len(self.ins)
        self.out_shape = [jax.ShapeDtypeStruct(a.shape, a.dtype) for a in slabs] \
            + [jax.ShapeDtypeStruct((N_CHIPS,) + a.shape, a.dtype) for a in whole]
        self.sems = [pltpu.SemaphoreType.DMA((3 * n,)), pltpu.SemaphoreType.DMA((3 * n,)), pltpu.SemaphoreType.DMA((n,))]

    def _copies(self, ins, outs, sems):
        send_sems, recv_sems, local_sems = sems
        me, peers, peer_chip = _chip_peers()
        n = len(ins)

        def src(i, chip):
            return ins[i].at[chip] if i < self.n_slabs else ins[i]

        def remote(i, k, src_chip, slot):
            return pltpu.make_async_remote_copy(src_ref=src(i, src_chip), dst_ref=outs[i].at[slot],
                                                send_sem=send_sems.at[3 * i + k], recv_sem=recv_sems.at[3 * i + k],
                                                device_id=peers[k], device_id_type=MESH)

        local = [pltpu.make_async_copy(src(i, me), outs[i].at[me], local_sems.at[i]) for i in range(n)]
        sends = [remote(i, k, peer_chip[k], me) for i in range(n) for k in range(3)]
        return local, sends, lambda: [remote(i, k, me, peer_chip[k]) for i in range(n) for k in range(3)]


class _Swap(_Exchange):
    def __init__(self, arrays):
        n = len(arrays)
        self.ins = list(arrays)
        self.out_shape = [jax.ShapeDtypeStruct(a.shape, a.dtype) for a in arrays]
        self.sems = [pltpu.SemaphoreType.DMA((n,)), pltpu.SemaphoreType.DMA((n,))]

    def _copies(self, ins, outs, sems):
        send_sems, recv_sems = sems
        x, y, c = lax.axis_index("x"), lax.axis_index("y"), lax.axis_index("c")
        cps = [pltpu.make_async_remote_copy(src_ref=ins[i], dst_ref=outs[i], send_sem=send_sems.at[i],
                                            recv_sem=recv_sems.at[i], device_id=(x, y, 1 - c), device_id_type=MESH)
               for i in range(len(ins))]
        return [], cps, lambda: cps


def _riding(body, n_in, n_out, n_scratch, ride, rank):
    if not ride:
        return body
    r_in = sum(len(e.ins) for e in ride)
    r_out = sum(len(e.out_shape) for e in ride)

    def split(refs, sizes):
        out, a = [], 0
        for sz in sizes:
            out.append(refs[a:a + sz])
            a += sz
        return out

    def wrapped(*refs):
        a = 0
        parts = []
        for sz in (n_in, r_in, n_out, r_out, n_scratch):
            parts.append(refs[a:a + sz])
            a += sz
        own_in, ex_in, own_out, ex_out, own_scratch = parts
        ex_sems = refs[a:]
        ins = split(ex_in, [len(e.ins) for e in ride])
        outs = split(ex_out, [len(e.out_shape) for e in ride])
        sems = split(ex_sems, [len(e.sems) for e in ride])
        if rank:
            step, total = 0, 1
            for d in range(rank):
                step = step * pl.num_programs(d) + pl.program_id(d)
                total = total * pl.num_programs(d)

            @pl.when(step == 0)
            def _():
                for e, i, o, s in zip(ride, ins, outs, sems):
                    e.start(i, o, s)

            body(*own_in, *own_out, *own_scratch)

            @pl.when(step == (total * _MID_NUM) // _MID_DEN)
            def _():
                for e, i, o, s in zip(ride, ins, outs, sems):
                    e.mid(i, o, s)

            @pl.when(step == total - 1)
            def _():
                for e, i, o, s in zip(ride, ins, outs, sems):
                    e.wait(i, o, s)
        else:
            for phase in ("start", "mid", "wait"):
                for e, i, o, s in zip(ride, ins, outs, sems):
                    getattr(e, phase)(i, o, s)

    return wrapped


def _ride_args(ride):
    ins = [a for e in ride for a in e.ins]
    outs = [s for e in ride for s in e.out_shape]
    sems = [s for e in ride for s in e.sems]
    return ins, _any_specs(len(ins)), outs, _any_specs(len(outs)), sems


def _ride_results(ride, flat):
    out, a = [], 0
    for e in ride:
        out.append(list(flat[a:a + len(e.out_shape)]))
        a += len(e.out_shape)
    return out


def exchange(ride, name):
    ins, in_specs, outs, out_specs, sems = _ride_args(ride)
    res = pl.pallas_call(
        _riding(lambda: None, 0, 0, 0, ride, 0), name=name,
        in_specs=in_specs, out_specs=out_specs, out_shape=outs, scratch_shapes=sems,
    )(*ins)
    return _ride_results(ride, res)


def _sum_slots(ref):
    acc = ref[0].astype(F32)
    for j in range(1, ref.shape[0]):
        acc = acc + ref[j].astype(F32)
    return acc


def sum_pairs(mine, other, name, tr=176):
    n, r, w = mine.shape
    tr = r if r <= tr else _tile(r, tr)

    def body(a_ref, b_ref, o_ref):
        o_ref[...] = _sum_slots(a_ref) + _sum_slots(b_ref)

    spec = pl.BlockSpec((n, tr, w), lambda i: (0, i, 0))
    return pl.pallas_call(
        body, name=name, grid=(r // tr,),
        in_specs=[spec, spec], out_specs=_rows(tr, w),
        out_shape=jax.ShapeDtypeStruct((r, w), F32),
        compiler_params=_params(("parallel",)),
    )(mine, other)


def adamw(w, m, v, gs, name, tr=256):
    r, c = w.shape
    tr = r if r % 8 else _tile(r, tr)
    c1 = 1.0 - ADAM_B1 ** ADAM_STEP
    c2 = 1.0 - ADAM_B2 ** ADAM_STEP
    ng = len(gs)

    def body(w_ref, m_ref, v_ref, *refs):
        g_refs, (g_ref, d_ref, nm_ref, nv_ref) = refs[:ng], refs[ng:]
        g = g_refs[0][...] if ng == 1 else _sum_slots(g_refs[0]) + _sum_slots(g_refs[1])
        nm = ADAM_B1 * m_ref[...] + (1.0 - ADAM_B1) * g
        nv = ADAM_B2 * v_ref[...] + (1.0 - ADAM_B2) * (g * g)
        g_ref[...] = g
        nm_ref[...] = nm
        nv_ref[...] = nv
        d_ref[...] = (-ADAM_LR) * ((nm / c1) / (jnp.sqrt(nv / c2) + ADAM_EPS) + ADAM_WD * w_ref[...])

    spec = _rows(tr, c)
    gspec = spec if ng == 1 else pl.BlockSpec((N_CHIPS, tr, c), lambda i: (0, i, 0))
    return pl.pallas_call(
        body, name=name, grid=(r // tr,),
        in_specs=[spec] * 3 + [gspec] * ng, out_specs=[spec] * 4,
        out_shape=[jax.ShapeDtypeStruct((r, c), F32)] * 4,
        compiler_params=_params(("parallel",)),
    )(w, m, v, *gs)


_PARAMS = (
    ("rel_bias", None), ("norm_mix_pre", None), ("norm_mix_post", None), ("w_in", 1), ("conv_rnn_w", 1),
    ("conv_rnn_b", None), ("w_rg_a", None), ("b_rg_a", None), ("w_rg_x", None), ("b_rg_x", None),
    ("lru_lambda", None), ("w_branch_rnn", 0), ("w_branch_att", 1), ("w_out", 0), ("norm_ffn_pre", None),
    ("norm_ffn_post", None), ("w_ffn_gate", 1), ("w_ffn_up", 1), ("conv_ffn_w", 1), ("conv_ffn_b", None),
    ("w_ffn_down", 0),
)
_SMALL = 65536


def _as2d(a):
    a = a[0] if a.shape[0] == 1 and a.ndim >= 3 else a
    return a.reshape(-1, a.shape[-1]) if a.ndim == 3 else a


def _pack(pieces, dtype):
    flat = jnp.concatenate([p.astype(dtype).reshape(-1) for p in pieces])
    unit = PACK_W * PACK_ROWS
    pad = (-flat.shape[0]) % unit
    flat = jnp.pad(flat, (0, pad))
    return flat.reshape(-1, PACK_W)


def _unpack(buf, shapes):
    flat = buf.reshape(-1)
    out, off = [], 0
    for shp in shapes:
        n = int(np.prod(shp))
        out.append(flat[off:off + n].reshape(shp))
        off += n
    return out


def _join(slots, ax):
    if ax == 0:
        return slots.reshape(-1, slots.shape[-1])
    return jnp.transpose(slots, (1, 0, 2)).reshape(slots.shape[1], -1)


def _cut(full, ax):
    if ax == 0:
        return full.reshape(N_CHIPS, -1, full.shape[-1])
    return jnp.transpose(full.reshape(full.shape[0], N_CHIPS, -1), (1, 0, 2))


def kernel(x, rel_bias, norm_mix_pre, norm_mix_post, w_in, conv_rnn_w, conv_rnn_b, w_rg_a, b_rg_a, w_rg_x, b_rg_x, lru_lambda, w_branch_rnn, w_branch_att, w_out, norm_ffn_pre, norm_ffn_post, w_ffn_gate, w_ffn_up, conv_ffn_w, conv_ffn_b, w_ffn_down, loss_target, m_rel_bias, m_norm_mix_pre, m_norm_mix_post, m_w_in, m_conv_rnn_w, m_conv_rnn_b, m_w_rg_a, m_b_rg_a, m_w_rg_x, m_b_rg_x, m_lru_lambda, m_w_branch_rnn, m_w_branch_att, m_w_out, m_norm_ffn_pre, m_norm_ffn_post, m_w_ffn_gate, m_w_ffn_up, m_conv_ffn_w, m_conv_ffn_b, m_w_ffn_down, v_rel_bias, v_norm_mix_pre, v_norm_mix_post, v_w_in, v_conv_rnn_w, v_conv_rnn_b, v_w_rg_a, v_b_rg_a, v_w_rg_x, v_b_rg_x, v_lru_lambda, v_w_branch_rnn, v_w_branch_att, v_w_out, v_norm_ffn_pre, v_norm_ffn_post, v_w_ffn_gate, v_w_ffn_up, v_conv_ffn_w, v_conv_ffn_b, v_w_ffn_down):
    args = dict(locals())
    names = [n for n, _ in _PARAMS]
    axis = dict(_PARAMS)
    w_loc = {n: args[n] for n in names}
    m_loc = {n: args["m_" + n] for n in names}
    v_loc = {n: args["v_" + n] for n in names}
    sharded = [n for n in names if axis[n] is not None]
    replicated = [n for n in names if axis[n] is None]

    big = [n for n in sharded if w_loc[n].size >= _SMALL]
    small_sharded = [n for n in sharded if n not in big]
    small = replicated + small_sharded

    first = ["w_in"] + small_sharded
    srcs = [_as2d(w_loc[n]).astype(BF16) if n in big else _as2d(w_loc[n]) for n in first]
    (gathered,) = exchange([_Gather(srcs)], "gather_first")
    p = {n: _join(a, axis[n]) for n, a in zip(first, gathered)}
    for n in replicated:
        p[n] = _as2d(w_loc[n])
    shards = {n: _as2d(w_loc[n]).astype(BF16) for n in big if n not in first}

    last = "norm_mix_pre"
    early = [n for n in small if n != last]
    received, sibling, g_small, loss_part = _local_step(x, loss_target, p, shards, early)

    ((received["last"],),) = exchange([_Scatter([], [_pack([g_small[last]], BF16)])], "scatter_last")
    late = [n for n in received if n not in sibling]
    (swapped,) = exchange([_Swap([received[n] for n in late])], "swap_last")
    sibling.update(zip(late, swapped))
    early_sum = sum_pairs(received["small"], sibling["small"], "sum_small")
    last_sum = sum_pairs(received["last"], sibling["last"], "sum_last")
    g_tot = dict(zip(early, _unpack(early_sum, [g_small[n].shape for n in early])))
    (g_tot[last],) = _unpack(last_sum, [g_small[last].shape])
    chip = 2 * lax.axis_index("x") + lax.axis_index("y")
    for n in small_sharded:
        size = g_tot[n].shape[axis[n]] // N_CHIPS
        g_tot[n] = lax.dynamic_slice_in_dim(g_tot[n], chip * size, size, axis=axis[n])

    out_g, out_d, out_m, out_v = {}, {}, {}, {}
    for i, n in enumerate(names):
        shp = w_loc[n].shape
        gs = (received[n], sibling[n]) if n in big else (g_tot[n],)
        g, d, nm, nv = adamw(_as2d(w_loc[n]), _as2d(m_loc[n]), _as2d(v_loc[n]), gs, "adamw_" + n)
        out_g[n], out_d[n], out_m[n], out_v[n] = (t.reshape(shp) for t in (g, d, nm, nv))

    d_model = x.shape[-1]
    loss = lax.psum(0.5 * jnp.sum(loss_part) / d_model, ("x", "y", "c"))
    grad_x = g_small["x"]
    return (loss, grad_x, *[out_g[n] for n in names], *[out_d[n] for n in names],
            *[out_m[n] for n in names], *[out_v[n] for n in names])


def _local_step(x, target, p, shards, small_early):
    axis = dict(_PARAMS)
    b, s, d = x.shape
    t = b * s
    rnn = p["b_rg_a"].shape[1]
    ffn = p["conv_ffn_b"].shape[1]
    nbk = rnn // p["w_rg_a"].shape[1]
    hkv = (p["w_in"].shape[1] - rnn - 2 * d) // (N_GROUPS + 2)
    h = hkv // HEAD_DIM
    nq = N_GROUPS * hkv

    x2 = x.reshape(t, d)
    tgt = target.reshape(t, d)
    w_in = p["w_in"]
    in_splits = (rnn, nq + 2 * hkv, 2 * d)
    wa = p["w_rg_a"].reshape(nbk, -1, p["w_rg_a"].shape[1]).astype(BF16)
    wx = p["w_rg_x"].reshape(nbk, -1, p["w_rg_x"].shape[1]).astype(BF16)
    cw_r, cb_r = p["conv_rnn_w"], p["conv_rnn_b"]
    cw_f, cb_f = p["conv_ffn_w"], p["conv_ffn_b"]

    masks, buckets = zip(*[_band(w_, r_) for w_, r_ in DILATED])
    bucket_f = jnp.asarray(np.where(np.stack(masks), np.stack(buckets), -1).astype(np.float32))
    rel_rows = jnp.pad(p["rel_bias"].T, ((0, 0), (0, 128 - REL_BUCKETS)))[:, None, :]
    biasm = bias_table(rel_rows, bucket_f, h, "bias_table")

    early = ["w_branch_rnn", "w_branch_att", "w_out"]
    hn1, (xr, qkv, gts), (got,) = norm_mm(x2, p["norm_mix_pre"], [w_in], [in_splits], "in_proj",
                                          ride=[_Gather([shards[n] for n in early])])
    p.update({n: _join(a, axis[n]) for n, a in zip(early, got)})
    xr3 = xr.reshape(b, s, rnn)
    (y_rnn, a_rnn, xc_rnn), (got,) = rglru_fwd(xr3, cw_r, cb_r, wa, p["b_rg_a"], wx, p["b_rg_x"], p["lru_lambda"], "rglru_fwd",
                              ride=[_Gather([shards[n] for n in ("w_ffn_gate", "w_ffn_up")])])
    p.update({n: _join(a, axis[n]) for n, a in zip(("w_ffn_gate", "w_ffn_up"), got)})
    qkv3 = qkv.reshape(b, s, -1)
    o_att, lse, ((got,),) = attn_fwd(qkv3, biasm, h, "attn_fwd", ride=[_Gather([shards["w_ffn_down"]])])
    p["w_ffn_down"] = _join(got, axis["w_ffn_down"])
    merged, br, ba, mix, h1 = merge_out(y_rnn.reshape(t, rnn), o_att.reshape(t, hkv), gts, p["w_branch_rnn"],
                                        p["w_branch_att"], p["w_out"], p["norm_mix_post"], x2, "merge_out")
    hn2, gate_pre, up, act = ffn_in_act(h1, p["norm_ffn_pre"], p["w_ffn_gate"], p["w_ffn_up"], cw_f, cb_f, s, "ffn_in")

    g, gb = {}, {}
    recv, sib = {}, {}

    def rows4(a):
        return a.reshape(N_CHIPS, -1, a.shape[-1])

    dy, dff, dact, g["norm_ffn_post"], loss_part = ffn_down_loss(act, p["w_ffn_down"], p["norm_ffn_post"], h1, tgt,
                                                                  "ffn_down")
    gb["w_ffn_down"] = rows4(mm_tn(act, [dff], "ffn_down_dw"))
    (dgp, dup, dhn2, g["conv_ffn_w"], g["conv_ffn_b"]), ((recv["w_ffn_down"],),) = ffn_in_bwd(
        dact, gate_pre, up, cw_f, cb_f, p["w_ffn_gate"], p["w_ffn_up"], s, "ffn_in_bwd",
        ride=[_Scatter([gb["w_ffn_down"]])])
    gb["w_ffn_gate"] = mm_tn(hn2, [dgp], "ffn_gate_dw", col_shards=N_CHIPS)
    gb["w_ffn_up"] = mm_tn(hn2, [dup], "ffn_up_dw", col_shards=N_CHIPS)
    (dh1, dgts, dy_rnn, do_att, g["norm_ffn_pre"], g["norm_mix_post"], dw_out, dw_br,
     gb["w_branch_att"]) = mid_bwd(dhn2, h1, p["norm_ffn_pre"], dy, mix, p["norm_mix_post"], p["w_out"], gts, br, ba,
                                   p["w_branch_rnn"], p["w_branch_att"], merged, y_rnn.reshape(t, rnn),
                                   o_att.reshape(t, hkv), "mid_bwd")
    gb["w_out"], gb["w_branch_rnn"] = rows4(dw_out), rows4(dw_br)
    ffn_in = ["w_ffn_gate", "w_ffn_up"]
    (dxr, g["conv_rnn_w"], g["conv_rnn_b"], dwa, g["b_rg_a"], dwx, g["b_rg_x"], g["lru_lambda"]), (got,) = rglru_bwd(
        xr3, y_rnn, dy_rnn.reshape(b, s, rnn), a_rnn, xc_rnn, cw_r, cb_r, wa, p["b_rg_a"], wx, p["b_rg_x"], p["lru_lambda"], "rglru_bwd",
        ride=[_Scatter([gb[n] for n in ffn_in])])
    recv.update(zip(ffn_in, got))
    g["w_rg_a"] = dwa.reshape(p["w_rg_a"].shape)
    g["w_rg_x"] = dwx.reshape(p["w_rg_x"].shape)
    mid = ["w_out", "w_branch_rnn", "w_branch_att"]
    early_recv = ["w_ffn_down"] + ffn_in
    (dq1, dq2, dq3, dk, dv, ds_sum), (got, swapped) = attn_bwd(
        qkv3, biasm, o_att, lse, do_att.reshape(b, s, hkv), h, "attn_bwd",
        ride=[_Scatter([gb[n] for n in mid]), _Swap([recv[n] for n in early_recv])])
    recv.update(zip(mid, got))
    sib.update(zip(early_recv, swapped))
    rows = bias_grad(ds_sum, bucket_f, "bias_grad")
    g["rel_bias"] = rows[:, 0, :REL_BUCKETS].T
    dproj = [dxr.reshape(t, rnn)] + [a.reshape(t, hkv) for a in (dq1, dq2, dq3, dk, dv)] + [dgts]
    dw_a = mm_tn(hn1, dproj[:4], "in_proj_dw_a")[0]
    dw_b = mm_tn(hn1, dproj[4:], "in_proj_dw_b")[0]
    gb["w_in"] = _cut(jnp.concatenate([dw_a, dw_b], axis=1), 1)
    pack = _pack([g[n] for n in small_early], BF16)
    dx, g["norm_mix_pre"], ((recv["w_in"], recv["small"]), got) = mm_nt(
        [(dproj, w_in)], "in_proj_dx", norm=(x2, p["norm_mix_pre"], dh1),
        ride=[_Scatter([gb["w_in"]], [pack]), _Swap([recv[n] for n in mid])])
    sib.update(zip(mid, got))
    g["x"] = dx.reshape(b, s, d)
    return recv, sib, g, loss_part
```

```python
import functools
import math

import numpy as np
import jax
import jax.numpy as jnp
from jax import lax
from jax.experimental import pallas as pl
from jax.experimental.pallas import tpu as pltpu

F32 = jnp.float32
BF16 = jnp.bfloat16

EPS = 1e-6
HEAD_DIM = 128
ATTN_BLOCK = 128
DILATED = ((128, 1), (512, 4), (2048, 16))
N_GROUPS = len(DILATED)
REL_BUCKETS = 32
REL_MAX_DIST = 2048
LRU_C = 8.0
NEG = -1e30

ADAM_LR = 0.001
ADAM_B1 = 0.9
ADAM_B2 = 0.999
ADAM_EPS = 1e-08
ADAM_WD = 0.01
ADAM_STEP = 10

N_CHIPS = 4
PACK_W = 1024
PACK_ROWS = 16
VMEM_LIMIT = 56 * 1024 * 1024
MESH = pl.DeviceIdType.MESH


def _params(sem=None):
    return pltpu.CompilerParams(dimension_semantics=sem, vmem_limit_bytes=VMEM_LIMIT)


def _dot(a, b):
    return jnp.dot(a, b, preferred_element_type=F32)


def _dot_nt(a, b):
    return lax.dot_general(a, b, (((1,), (1,)), ((), ())), preferred_element_type=F32)


def _dot_tn(a, b):
    return lax.dot_general(a, b, (((0,), (0,)), ((), ())), preferred_element_type=F32)


def _sig(x):
    return 0.5 * jnp.tanh(0.5 * x) + 0.5


def _rows(tm, w):
    return pl.BlockSpec((tm, w), lambda i: (i, 0))


def _whole(shape):
    nd = len(shape)
    return pl.BlockSpec(tuple(shape), lambda *_: (0,) * nd)


def _resident(shape):
    nd = len(shape)
    return pl.BlockSpec(tuple(shape), lambda *_: (0,) * nd, pipeline_mode=pl.Buffered(1))


def _tile(t, want):
    while t % want:
        want //= 2
    return want


def norm_mm(x, g, ws, splits, name, ride=(), tm=512):
    t, d = x.shape
    tm = _tile(t, tm)
    nw = len(ws)
    widths = [n for sp in splits for n in sp]

    def body(x_ref, g_ref, *refs):
        w_refs, hn_ref, o_refs = refs[:nw], refs[nw], refs[nw + 1:]
        xv = x_ref[...]
        inv = lax.rsqrt(jnp.mean(xv * xv, axis=-1, keepdims=True) + EPS)
        hn = (xv * inv * g_ref[...]).astype(BF16)
        hn_ref[...] = hn
        o = 0
        for w_ref, sp in zip(w_refs, splits):
            off = 0
            for n in sp:
                o_refs[o][...] = _dot(hn, w_ref[:, off:off + n])
                off += n
                o += 1

    r_ins, r_in_specs, r_outs, r_out_specs, r_sems = _ride_args(ride)
    n_out = 1 + len(widths)
    outs = pl.pallas_call(
        _riding(body, 2 + nw, n_out, 0, ride, 1), name=name, grid=(t // tm,),
        in_specs=[_rows(tm, d), _whole(g.shape)] + [_resident(w.shape) for w in ws] + r_in_specs,
        out_specs=[_rows(tm, d)] + [_rows(tm, n) for n in widths] + r_out_specs,
        out_shape=[jax.ShapeDtypeStruct((t, d), BF16)] + [jax.ShapeDtypeStruct((t, n), F32) for n in widths] + r_outs,
        scratch_shapes=r_sems,
        compiler_params=_params(("arbitrary",)),
    )(x, g, *ws, *r_ins)
    return outs[0], outs[1:n_out], _ride_results(ride, outs[n_out:])


def mm_nt(groups, name, ride=(), norm=None, tm=512):
    dys_all = [dy for dys, _ in groups for dy in dys]
    ws = [w for _, w in groups]
    t = dys_all[0].shape[0]
    k = ws[0].shape[0]
    tm = _tile(t, tm)
    n = len(dys_all)
    extra = list(norm) if norm else []

    def body(*refs):
        dy_refs, w_refs = refs[:n], refs[n:n + len(ws)]
        rest = refs[n + len(ws):]
        acc = None
        i = 0
        for (dys, _), w_ref in zip(groups, w_refs):
            off = 0
            for dy in dys:
                width = dy.shape[1]
                part = _dot_nt(dy_refs[i][...].astype(BF16), w_ref[:, off:off + width])
                acc = part if acc is None else acc + part
                off += width
                i += 1
        if norm:
            u_ref, g_ref, add_ref, o_ref, dg_ref = rest

            @pl.when(pl.program_id(0) == 0)
            def _():
                dg_ref[...] = jnp.zeros(dg_ref.shape, F32)

            du, dg_rows = _rms_bwd(acc, u_ref[...], g_ref[...])
            o_ref[...] = du + add_ref[...]
            dg_ref[...] += jnp.sum(dg_rows, axis=0, keepdims=True)
        else:
            rest[0][...] = acc

    n_out = 2 if norm else 1
    r_ins, r_in_specs, r_outs, r_out_specs, r_sems = _ride_args(ride)
    outs = pl.pallas_call(
        _riding(body, n + len(ws) + len(extra), n_out, 0, ride, 1), name=name, grid=(t // tm,),
        in_specs=[_rows(tm, dy.shape[1]) for dy in dys_all] + [_resident(w.shape) for w in ws]
        + ([_rows(tm, k), _whole((1, k)), _rows(tm, k)] if norm else []) + r_in_specs,
        out_specs=[_rows(tm, k)] + ([_whole((1, k))] if norm else []) + r_out_specs,
        out_shape=[jax.ShapeDtypeStruct((t, k), F32)] + ([jax.ShapeDtypeStruct((1, k), F32)] if norm else []) + r_outs,
        scratch_shapes=r_sems,
        compiler_params=_params(("arbitrary",)),
    )(*dys_all, *ws, *extra, *r_ins)
    return tuple(outs[:n_out]) + (_ride_results(ride, outs[n_out:]),)


def mm_tn(a, dys, name, col_shards=1, ride=(), tm=1024):
    t, k = a.shape
    tm = _tile(t, tm)
    n = len(dys)
    ntot = sum(dy.shape[1] for dy in dys)
    wsh = ntot // col_shards

    def body(a_ref, *refs):
        dy_refs, o_ref, acc = refs[:n], refs[n], refs[n + 1]

        @pl.when(pl.program_id(0) == 0)
        def _():
            acc[...] = jnp.zeros(acc.shape, F32)

        av = a_ref[...].astype(BF16)
        off = 0
        for dy_ref in dy_refs:
            width = dy_ref.shape[1]
            acc[:, off:off + width] += _dot_tn(av, dy_ref[...].astype(BF16))
            off += width

        @pl.when(pl.program_id(0) == pl.num_programs(0) - 1)
        def _():
            for j in range(col_shards):
                o_ref[j] = acc[:, j * wsh:(j + 1) * wsh].astype(o_ref.dtype)

    r_ins, r_in_specs, r_outs, r_out_specs, r_sems = _ride_args(ride)
    outs = pl.pallas_call(
        _riding(body, 1 + n, 1, 1, ride, 1), name=name, grid=(t // tm,),
        in_specs=[_rows(tm, k)] + [_rows(tm, dy.shape[1]) for dy in dys] + r_in_specs,
        out_specs=[_whole((col_shards, k, wsh))] + r_out_specs,
        out_shape=[jax.ShapeDtypeStruct((col_shards, k, wsh), BF16)] + r_outs,
        scratch_shapes=[pltpu.VMEM((k, ntot), F32)] + r_sems,
        compiler_params=_params(("arbitrary",)),
    )(a, *dys, *r_ins)
    return (outs[0], _ride_results(ride, outs[1:])) if ride else outs[0]


def _rms_bwd(dz, u, g):
    d = u.shape[-1]
    inv = lax.rsqrt(jnp.mean(u * u, axis=-1, keepdims=True) + EPS)
    dzg = dz * g
    proj = jnp.sum(dzg * u, axis=-1, keepdims=True) * (1.0 / d)
    du = inv * (dzg - u * (inv * inv) * proj)
    dg_rows = dz * u * inv
    return du, dg_rows


def ffn_down_loss(act, wd, g, h1, target, name, tm=512):
    t, f = act.shape
    d = wd.shape[1]
    tm = _tile(t, tm)

    def body(a_ref, w_ref, g_ref, h_ref, t_ref, dy_ref, dff_ref, dact_ref, dg_ref, loss_ref):
        @pl.when(pl.program_id(0) == 0)
        def _():
            dg_ref[...] = jnp.zeros(dg_ref.shape, F32)
            loss_ref[...] = jnp.zeros(loss_ref.shape, F32)

        wv = w_ref[...]
        gv = g_ref[...]
        ff = _dot(a_ref[...], wv)
        inv = lax.rsqrt(jnp.mean(ff * ff, axis=-1, keepdims=True) + EPS)
        err = h_ref[...] + ff * inv * gv - t_ref[...]
        loss_ref[...] += jnp.sum(err * err, axis=0, keepdims=True)
        dy = err * (1.0 / d)
        dy_ref[...] = dy
        du, dg_rows = _rms_bwd(dy, ff, gv)
        dff = du.astype(BF16)
        dff_ref[...] = dff
        dg_ref[...] += jnp.sum(dg_rows, axis=0, keepdims=True)
        dact_ref[...] = _dot_nt(dff, wv)

    return pl.pallas_call(
        body, name=name, grid=(t // tm,),
        in_specs=[_rows(tm, f), _resident(wd.shape), _whole(g.shape), _rows(tm, d), _rows(tm, d)],
        out_specs=[_rows(tm, d), _rows(tm, d), _rows(tm, f), _whole((1, d)), _whole((1, d))],
        out_shape=[jax.ShapeDtypeStruct((t, d), F32), jax.ShapeDtypeStruct((t, d), BF16),
                   jax.ShapeDtypeStruct((t, f), F32), jax.ShapeDtypeStruct((1, d), F32),
                   jax.ShapeDtypeStruct((1, d), F32)],
        compiler_params=_params(("arbitrary",)),
    )(act, wd, g, h1, target)


def merge_out(y_rnn, o_att, gts, w_br, w_ba, w_out, g, x, name, tm=256):
    t = y_rnn.shape[0]
    d = w_br.shape[1]
    tm = _tile(t, tm)

    def body(y_ref, o_ref, g_ref, wbr_ref, wba_ref, wo_ref, gn_ref, x_ref, m_ref, br_ref, ba_ref, mix_ref, h_ref):
        br = _dot(y_ref[...].astype(BF16), wbr_ref[...])
        ba = _dot(o_ref[...].astype(BF16), wba_ref[...])
        gv = g_ref[...]
        merged = (_sig(gv[:, :d]) * br + _sig(gv[:, d:]) * ba).astype(BF16)
        m_ref[...] = merged
        br_ref[...] = br
        ba_ref[...] = ba
        mix = _dot(merged, wo_ref[...])
        mix_ref[...] = mix
        inv = lax.rsqrt(jnp.mean(mix * mix, axis=-1, keepdims=True) + EPS)
        h_ref[...] = x_ref[...] + mix * inv * gn_ref[...]

    sd = jax.ShapeDtypeStruct
    return pl.pallas_call(
        body, name=name, grid=(t // tm,),
        in_specs=[_rows(tm, y_rnn.shape[1]), _rows(tm, o_att.shape[1]), _rows(tm, 2 * d),
                  _whole(w_br.shape), _whole(w_ba.shape), _whole(w_out.shape), _whole(g.shape), _rows(tm, d)],
        out_specs=[_rows(tm, d)] * 5,
        out_shape=[sd((t, d), BF16), sd((t, d), F32), sd((t, d), F32), sd((t, d), F32), sd((t, d), F32)],
        compiler_params=_params(("parallel",)),
    )(y_rnn, o_att, gts, w_br, w_ba, w_out, g, x)


def mid_bwd(dhn2, h1, g_ffn, dy, mix, g_mix, w_out, gts, br, ba, w_br, w_ba, merged, y_rnn, o_att, name, tm=256):
    t, d = h1.shape
    tm = _tile(t, tm)
    rnn, hkv = w_br.shape[0], w_ba.shape[0]
    wsh = d // N_CHIPS

    def body(dhn_ref, h_ref, gf_ref, dy_ref, mix_ref, gm_ref, wo_ref, g_ref, br_ref, ba_ref, wbr_ref, wba_ref,
             m_ref, y_ref, o_ref, dh_ref, dg_ref, dyr_ref, doa_ref, dgf_ref, dgm_ref, dwo_ref, dwbr_ref, dwba_ref,
             acc_o, acc_br, acc_ba):
        @pl.when(pl.program_id(0) == 0)
        def _():
            dgf_ref[...] = jnp.zeros(dgf_ref.shape, F32)
            dgm_ref[...] = jnp.zeros(dgm_ref.shape, F32)
            acc_o[...] = jnp.zeros(acc_o.shape, F32)
            acc_br[...] = jnp.zeros(acc_br.shape, F32)
            acc_ba[...] = jnp.zeros(acc_ba.shape, F32)

        du, rows_f = _rms_bwd(dhn_ref[...], h_ref[...], gf_ref[...])
        dh1 = du + dy_ref[...]
        dh_ref[...] = dh1
        dgf_ref[...] += jnp.sum(rows_f, axis=0, keepdims=True)
        dmx, rows_m = _rms_bwd(dh1, mix_ref[...], gm_ref[...])
        dmix = dmx.astype(BF16)
        acc_o[...] += _dot_tn(m_ref[...], dmix)
        dgm_ref[...] += jnp.sum(rows_m, axis=0, keepdims=True)
        dm = _dot_nt(dmix, wo_ref[...])
        gv = g_ref[...]
        sr = _sig(gv[:, :d])
        sa = _sig(gv[:, d:])
        dbr = (dm * sr).astype(BF16)
        dba = (dm * sa).astype(BF16)
        acc_br[...] += _dot_tn(y_ref[...].astype(BF16), dbr)
        acc_ba[...] += _dot_tn(o_ref[...].astype(BF16), dba)
        dg_ref[:, :d] = (dm * br_ref[...] * sr * (1.0 - sr)).astype(BF16)
        dg_ref[:, d:] = (dm * ba_ref[...] * sa * (1.0 - sa)).astype(BF16)
        dyr_ref[...] = _dot_nt(dbr, wbr_ref[...])
        doa_ref[...] = _dot_nt(dba, wba_ref[...])

        @pl.when(pl.program_id(0) == pl.num_programs(0) - 1)
        def _():
            dwo_ref[...] = acc_o[...].astype(BF16)
            dwbr_ref[...] = acc_br[...].astype(BF16)
            for j in range(N_CHIPS):
                dwba_ref[j] = acc_ba[:, j * wsh:(j + 1) * wsh].astype(BF16)

    sd = jax.ShapeDtypeStruct
    row, vec = _rows(tm, d), _whole((1, d))
    once = pl.Buffered(1)

    def resident(shape):
        return pl.BlockSpec(shape, lambda i: (0,) * len(shape), pipeline_mode=once)

    return pl.pallas_call(
        body, name=name, grid=(t // tm,),
        in_specs=[row, row, vec, row, row, vec, resident(w_out.shape), _rows(tm, 2 * d), row, row,
                  resident(w_br.shape), resident(w_ba.shape), row, _rows(tm, rnn), _rows(tm, hkv)],
        out_specs=[row, _rows(tm, 2 * d), _rows(tm, rnn), _rows(tm, hkv), vec, vec,
                   resident((d, d)), resident((rnn, d)), resident((N_CHIPS, hkv, wsh))],
        out_shape=[sd((t, d), F32), sd((t, 2 * d), BF16), sd((t, rnn), F32), sd((t, hkv), F32), sd((1, d), F32),
                   sd((1, d), F32), sd((d, d), BF16), sd((rnn, d), BF16), sd((N_CHIPS, hkv, wsh), BF16)],
        scratch_shapes=[pltpu.VMEM((d, d), F32), pltpu.VMEM((rnn, d), F32), pltpu.VMEM((hkv, d), F32)],
        compiler_params=_params(("arbitrary",)),
    )(dhn2, h1, g_ffn, dy, mix, g_mix, w_out, gts, br, ba, w_br, w_ba, merged, y_rnn, o_att)


def _shift_dn(x, d, fill, row):
    return jnp.where(row >= d, pltpu.roll(x, d, 0), fill)


def _shift_up(x, d, fill, row):
    s = x.shape[0]
    return jnp.where(row < s - d, pltpu.roll(x, s - d, 0), fill)


def _conv_fwd(x, w, b, row):
    kk = w.shape[0]
    y = b + w[kk - 1:kk, :] * x
    for j in range(1, kk):
        y = y + w[kk - 1 - j:kk - j, :] * _shift_dn(x, j, 0.0, row)
    return y


def _conv_bwd(dy, x, w, row):
    kk = w.shape[0]
    dx = w[kk - 1:kk, :] * dy
    dws = [None] * kk
    dws[kk - 1] = jnp.sum(dy * x, axis=0, keepdims=True)
    for j in range(1, kk):
        dx = dx + w[kk - 1 - j:kk - j, :] * _shift_up(dy, j, 0.0, row)
        dws[kk - 1 - j] = jnp.sum(dy * _shift_dn(x, j, 0.0, row), axis=0, keepdims=True)
    return dx, jnp.concatenate(dws, axis=0)


def _softplus(z):
    y = jnp.exp(-jnp.abs(z))
    u = 1.0 + y
    dd = u - 1.0
    log1p = jnp.where(dd == 0.0, y, jnp.log(u) * (y / jnp.where(dd == 0.0, 1.0, dd)))
    return jnp.maximum(z, 0.0) + log1p


def _lru_decay(xb, wa, ba, lam):
    r = _sig(_dot(xb, wa) + ba)
    sp = _softplus(-lam)
    la = (-LRU_C) * r * sp
    return r, sp, la, jnp.exp(la)


def _lru_gates(xc, wa, ba, wx, bx, lam):
    xb = xc.astype(BF16)
    r, sp, la, a = _lru_decay(xb, wa, ba, lam)
    i = _sig(_dot(xb, wx) + bx)
    one_m_a2 = jnp.tanh(-la) * (1.0 + a * a)
    inv_mult = lax.rsqrt(one_m_a2)
    return r, i, sp, a, one_m_a2 * inv_mult, inv_mult


def _seg_len(s):
    seg = -(-s // 8)
    return seg + (4 - seg % 8) % 8


def _scan_rows(a_pad, u_pad, out_pad, reverse):
    planes, rows8, lanes = a_pad.shape
    seg = rows8 // 8
    sub = lax.broadcasted_iota(jnp.int32, (planes, 8, lanes), 1)

    unroll = 4

    def rows(k, d):
        i = k * unroll + d
        return pl.ds((seg - 1 - i) if reverse else i, 8, stride=seg)

    def ends(k, carry):
        h, p = carry
        for d in range(unroll):
            a = a_pad[:, rows(k, d), :]
            h = a * h + u_pad[:, rows(k, d), :]
            p = a * p
        return h, p

    init = (jnp.zeros((planes, 8, lanes), F32), jnp.ones((planes, 8, lanes), F32))
    h_end, p_end = lax.fori_loop(0, seg // unroll, ends, init)
    start = jnp.zeros((planes, 8, lanes), F32)
    for _ in range(7):
        nxt = h_end + p_end * start
        if reverse:
            start = jnp.where(sub < 7, pltpu.roll(nxt, 7, 1), 0.0)
        else:
            start = jnp.where(sub >= 1, pltpu.roll(nxt, 1, 1), 0.0)

    def redo(k, h):
        for d in range(unroll):
            h = a_pad[:, rows(k, d), :] * h + u_pad[:, rows(k, d), :]
            out_pad[:, rows(k, d), :] = h
        return h

    lax.fori_loop(0, seg // unroll, redo, start)


def _lru_cols(c, rb):
    return 2 * rb if c % (2 * rb) == 0 else rb


def rglru_fwd(xr, cw, cb, wa, ba, wx, bx, lam, name, ride=()):
    b, s, c = xr.shape
    rb = wa.shape[1]
    kk = cw.shape[0]
    cols = _lru_cols(c, rb)
    nj = cols // rb
    seg = _seg_len(s)

    def body(x_ref, cw_ref, cb_ref, wa_ref, ba_ref, wx_ref, bx_ref, lam_ref, h_ref, a_ref, xc_ref, a_pad, u_pad, h_pad):
        row = lax.broadcasted_iota(jnp.int32, (s, rb), 0)
        for j in range(nj):
            cs = slice(j * rb, (j + 1) * rb)
            xc = _conv_fwd(x_ref[:, cs], cw_ref[:, cs], cb_ref[:, cs], row)
            _, i, _, a, mult, _ = _lru_gates(xc, wa_ref[j], ba_ref[:, cs], wx_ref[j], bx_ref[:, cs], lam_ref[:, cs])
            xc_ref[:, cs] = xc
            a_ref[:, cs] = a
            a_pad[j, 0:s, :] = a
            u_pad[j, 0:s, :] = mult * (i * xc)
        a_pad[:, s:, :] = jnp.ones((nj, 8 * seg - s, rb), F32)
        u_pad[:, s:, :] = jnp.zeros((nj, 8 * seg - s, rb), F32)
        _scan_rows(a_pad, u_pad, h_pad, False)
        for j in range(nj):
            h_ref[:, j * rb:(j + 1) * rb] = h_pad[j, 0:s, :]

    vec = pl.BlockSpec((1, cols), lambda bi, n: (0, n))
    seq = pl.BlockSpec((None, s, cols), lambda bi, n: (bi, 0, n))
    mat = pl.BlockSpec((nj, rb, rb), lambda bi, n: (n, 0, 0))
    r_ins, r_in_specs, r_outs, r_out_specs, r_sems = _ride_args(ride)
    outs = pl.pallas_call(
        _riding(body, 8, 3, 3, ride, 2), name=name, grid=(b, c // cols),
        in_specs=[seq, pl.BlockSpec((kk, cols), lambda bi, n: (0, n)), vec, mat, vec, mat, vec, vec] + r_in_specs,
        out_specs=[seq] * 3 + r_out_specs,
        out_shape=[jax.ShapeDtypeStruct((b, s, c), F32)] * 3 + r_outs,
        scratch_shapes=[pltpu.VMEM((nj, 8 * seg, rb), F32)] * 3 + r_sems,
        compiler_params=_params(("arbitrary", "arbitrary")),
    )(xr, cw, cb, wa, ba, wx, bx, lam, *r_ins)
    return outs[:3], _ride_results(ride, outs[3:])


def rglru_bwd(xr, h, dh, a_fwd, xc_fwd, cw, cb, wa, ba, wx, bx, lam, name, ride=()):
    b, s, c = xr.shape
    nb, rb = wa.shape[0], wa.shape[1]
    kk = cw.shape[0]
    cols = _lru_cols(c, rb)
    nj = cols // rb
    seg = _seg_len(s)

    def body(x_ref, h_ref, dh_ref, a_ref, xc_ref, cw_ref, cb_ref, wa_ref, ba_ref, wx_ref, bx_ref, lam_ref,
             dx_ref, dcw_ref, dcb_ref, dwa_ref, dba_ref, dwx_ref, dbx_ref, dlam_ref, b_pad, g_pad, l_pad):
        @pl.when(pl.program_id(1) == 0)
        def _():
            for ref in (dcw_ref, dcb_ref, dwa_ref, dba_ref, dwx_ref, dbx_ref, dlam_ref):
                ref[...] = jnp.zeros(ref.shape, F32)

        row = lax.broadcasted_iota(jnp.int32, (s, rb), 0)

        for j in range(nj):
            b_pad[j, 0:s, :] = _shift_up(a_ref[:, j * rb:(j + 1) * rb], 1, 0.0, row)
            g_pad[j, 0:s, :] = dh_ref[:, j * rb:(j + 1) * rb]
        b_pad[:, s:, :] = jnp.zeros((nj, 8 * seg - s, rb), F32)
        g_pad[:, s:, :] = jnp.zeros((nj, 8 * seg - s, rb), F32)
        _scan_rows(b_pad, g_pad, l_pad, True)

        for j in range(nj):
            cs = slice(j * rb, (j + 1) * rb)
            x = x_ref[:, cs]
            cwv = cw_ref[:, cs]
            wav, wxv, lamv = wa_ref[j], wx_ref[j], lam_ref[:, cs]
            xc = xc_ref[:, cs]
            r, i, sp, a, mult, inv_mult = _lru_gates(xc, wav, ba_ref[:, cs], wxv, bx_ref[:, cs], lamv)
            lmb = l_pad[j, 0:s, :]
            h_prev = _shift_dn(h_ref[:, cs], 1, 0.0, row)
            da = lmb * h_prev
            ixc = i * xc
            dla = da * a - (lmb * ixc) * (a * a) * inv_mult
            di = lmb * mult * xc
            dxc = lmb * mult * i
            dr = dla * ((-LRU_C) * sp)
            dsp = jnp.sum(dla * ((-LRU_C) * r), axis=0, keepdims=True)
            dga = dr * r * (1.0 - r)
            dgx = di * i * (1.0 - i)
            dga_b, dgx_b = dga.astype(BF16), dgx.astype(BF16)
            xb = xc.astype(BF16)
            dwa_ref[j] += _dot_tn(xb, dga_b)
            dwx_ref[j] += _dot_tn(xb, dgx_b)
            dba_ref[:, cs] += jnp.sum(dga, axis=0, keepdims=True)
            dbx_ref[:, cs] += jnp.sum(dgx, axis=0, keepdims=True)
            dlam_ref[:, cs] += dsp * (-_sig(-lamv))
            dxc = dxc + _dot_nt(dga_b, wav) + _dot_nt(dgx_b, wxv)
            dcb_ref[:, cs] += jnp.sum(dxc, axis=0, keepdims=True)
            dx, dcw = _conv_bwd(dxc, x, cwv, row)
            dcw_ref[:, cs] += dcw
            dx_ref[:, cs] = dx.astype(dx_ref.dtype)

    vec = pl.BlockSpec((1, cols), lambda n, bi: (0, n))
    seq = pl.BlockSpec((None, s, cols), lambda n, bi: (bi, 0, n))
    mat = pl.BlockSpec((nj, rb, rb), lambda n, bi: (n, 0, 0))
    cws = pl.BlockSpec((kk, cols), lambda n, bi: (0, n))
    sd = jax.ShapeDtypeStruct
    r_ins, r_in_specs, r_outs, r_out_specs, r_sems = _ride_args(ride)
    outs = pl.pallas_call(
        _riding(body, 12, 8, 3, ride, 2), name=name, grid=(c // cols, b),
        in_specs=[seq, seq, seq, seq, seq, cws, vec, mat, vec, mat, vec, vec] + r_in_specs,
        out_specs=[seq, cws, vec, mat, vec, mat, vec, vec] + r_out_specs,
        out_shape=[sd((b, s, c), BF16), sd((kk, c), F32), sd((1, c), F32), sd((nb, rb, rb), F32),
                   sd((1, c), F32), sd((nb, rb, rb), F32), sd((1, c), F32), sd((1, c), F32)] + r_outs,
        scratch_shapes=[pltpu.VMEM((nj, 8 * seg, rb), F32)] * 3 + r_sems,
        compiler_params=_params(("arbitrary", "arbitrary")),
    )(xr, h, dh, a_fwd, xc_fwd, cw, cb, wa, ba, wx, bx, lam, *r_ins)
    return outs[:8], _ride_results(ride, outs[8:])


_GELU_C = math.sqrt(2.0 / math.pi)


def _gelu_parts(x):
    th = jnp.tanh(_GELU_C * (x + 0.044715 * x * x * x))
    gel = 0.5 * x * (1.0 + th)
    dgel = 0.5 * (1.0 + th) + 0.5 * x * (1.0 - th * th) * _GELU_C * (1.0 + 3 * 0.044715 * x * x)
    return gel, dgel


def ffn_in_act(x, g, wg, wu, cw, cb, seq_len, name, tm=256):
    t, d = x.shape
    f = wg.shape[1]
    kk = cw.shape[0]
    tm = _tile(seq_len, tm)
    tiles_per_seq = seq_len // tm
    keep = 8
    assert kk - 1 <= keep

    def body(x_ref, g_ref, wg_ref, wu_ref, cw_ref, cb_ref, hn_ref, gp_ref, up_ref, act_ref, tail):
        @pl.when(pl.program_id(0) % tiles_per_seq == 0)
        def _():
            tail[...] = jnp.zeros(tail.shape, F32)

        xv = x_ref[...]
        inv = lax.rsqrt(jnp.mean(xv * xv, axis=-1, keepdims=True) + EPS)
        hn = (xv * inv * g_ref[...]).astype(BF16)
        hn_ref[...] = hn
        gp = _dot(hn, wg_ref[...])
        up = _dot(hn, wu_ref[...])
        gp_ref[...] = gp
        up_ref[...] = up
        cwv = cw_ref[...]
        row = lax.broadcasted_iota(jnp.int32, (tm, 1), 0)
        gate = _conv_fwd(gp, cwv, cb_ref[...], row)
        row8 = lax.broadcasted_iota(jnp.int32, (keep, 1), 0)
        prev = tail[...]
        fix = jnp.zeros((keep, f), F32)
        for j in range(1, kk):
            fix = fix + cwv[kk - 1 - j:kk - j, :] * jnp.where(row8 < j, pltpu.roll(prev, j, 0), 0.0)
        gate = jnp.concatenate([gate[:keep] + fix, gate[keep:]], axis=0)
        tail[...] = gp[tm - keep:, :]
        gel, _ = _gelu_parts(gate)
        act_ref[...] = (gel * up).astype(BF16)

    sd = jax.ShapeDtypeStruct
    return pl.pallas_call(
        body, name=name, grid=(t // tm,),
        in_specs=[_rows(tm, d), _whole(g.shape), _whole(wg.shape), _whole(wu.shape), _whole(cw.shape), _whole(cb.shape)],
        out_specs=[_rows(tm, d), _rows(tm, f), _rows(tm, f), _rows(tm, f)],
        out_shape=[sd((t, d), BF16), sd((t, f), F32), sd((t, f), F32), sd((t, f), BF16)],
        scratch_shapes=[pltpu.VMEM((keep, f), F32)],
        compiler_params=_params(("arbitrary",)),
    )(x, g, wg, wu, cw, cb)


def ffn_in_bwd(dact, gate_pre, up, cw, cb, wg, wu, seq_len, name, ride=(), tm=256):
    t, f = gate_pre.shape
    d = wg.shape[0]
    kk = cw.shape[0]
    tm = _tile(seq_len, tm)
    nt = t // tm
    tiles_per_seq = seq_len // tm
    keep = 8
    assert kk - 1 <= keep

    def body(da_ref, g_ref, halo_ref, u_ref, cw_ref, cb_ref, wg_ref, wu_ref,
             dg_ref, du_ref, dhn_ref, dcw_ref, dcb_ref, nxt):
        tile = (nt - 1 - pl.program_id(0)) % tiles_per_seq

        @pl.when(pl.program_id(0) == 0)
        def _():
            dcw_ref[...] = jnp.zeros(dcw_ref.shape, F32)
            dcb_ref[...] = jnp.zeros(dcb_ref.shape, F32)

        @pl.when(tile == tiles_per_seq - 1)
        def _():
            nxt[...] = jnp.zeros(nxt.shape, F32)

        row = lax.broadcasted_iota(jnp.int32, (tm, 1), 0)
        row8 = lax.broadcasted_iota(jnp.int32, (keep, 1), 0)
        gp = g_ref[...]
        cwv = cw_ref[...]
        prev = jnp.where(tile > 0, halo_ref[...], 0.0)
        gate = _conv_fwd(gp, cwv, cb_ref[...], row)
        fix = jnp.zeros((keep, f), F32)
        for j in range(1, kk):
            fix = fix + cwv[kk - 1 - j:kk - j, :] * jnp.where(row8 < j, pltpu.roll(prev, j, 0), 0.0)
        gate = jnp.concatenate([gate[:keep] + fix, gate[keep:]], axis=0)
        gel, dgel = _gelu_parts(gate)
        da = da_ref[...]
        dup = (da * gel).astype(BF16)
        du_ref[...] = dup
        dgate = da * u_ref[...] * dgel
        dcb_ref[...] += jnp.sum(dgate, axis=0, keepdims=True)
        after = nxt[...]
        dgp = cwv[kk - 1:kk, :] * dgate
        tail_fix = jnp.zeros((keep, f), F32)
        dws = [None] * kk
        dws[kk - 1] = jnp.sum(dgate * gp, axis=0, keepdims=True)
        for j in range(1, kk):
            wj = cwv[kk - 1 - j:kk - j, :]
            dgp = dgp + wj * _shift_up(dgate, j, 0.0, row)
            tail_fix = tail_fix + wj * jnp.where(row8 >= keep - j, pltpu.roll(after, keep - j, 0), 0.0)
            dws[kk - 1 - j] = (jnp.sum(dgate * _shift_dn(gp, j, 0.0, row), axis=0, keepdims=True)
                               + jnp.sum(dgate[:keep] * jnp.where(row8 < j, pltpu.roll(prev, j, 0), 0.0),
                                         axis=0, keepdims=True))
        dgp = jnp.concatenate([dgp[:tm - keep], dgp[tm - keep:] + tail_fix], axis=0).astype(BF16)
        nxt[...] = dgate[:keep]
        dcw_ref[...] += jnp.concatenate(dws, axis=0)
        dg_ref[...] = dgp
        dhn_ref[...] = _dot_nt(dgp, wg_ref[...]) + _dot_nt(dup, wu_ref[...])

    def rev(i):
        return nt - 1 - i

    rows_f = pl.BlockSpec((tm, f), lambda i: (rev(i), 0))
    halo = pl.BlockSpec((None, keep, f), lambda i: (jnp.maximum(rev(i) * (tm // keep) - 1, 0), 0, 0))
    once = pl.Buffered(1)
    sd = jax.ShapeDtypeStruct
    r_ins, r_in_specs, r_outs, r_out_specs, r_sems = _ride_args(ride)
    outs = pl.pallas_call(
        _riding(body, 8, 5, 1, ride, 1), name=name, grid=(nt,),
        in_specs=[rows_f, rows_f, halo, rows_f, _whole(cw.shape), _whole(cb.shape),
                  pl.BlockSpec(wg.shape, lambda i: (0, 0), pipeline_mode=once),
                  pl.BlockSpec(wu.shape, lambda i: (0, 0), pipeline_mode=once)] + r_in_specs,
        out_specs=[rows_f, rows_f, pl.BlockSpec((tm, d), lambda i: (rev(i), 0)), _whole((kk, f)), _whole((1, f))]
        + r_out_specs,
        out_shape=[sd((t, f), BF16), sd((t, f), BF16), sd((t, d), F32), sd((kk, f), F32), sd((1, f), F32)] + r_outs,
        scratch_shapes=[pltpu.VMEM((keep, f), F32)] + r_sems,
        compiler_params=_params(("arbitrary",)),
    )(dact, gate_pre, gate_pre.reshape(t // keep, keep, f), up, cw, cb, wg, wu, *r_ins)
    return outs[:5], _ride_results(ride, outs[5:])


def _t5_bucket(dist):
    max_exact = REL_BUCKETS // 2
    d = np.maximum(dist, 1).astype(np.float32)
    large = max_exact + np.log(d / max_exact) / math.log(REL_MAX_DIST / max_exact) * (REL_BUCKETS - max_exact)
    large = np.minimum(large.astype(np.int32), REL_BUCKETS - 1)
    return np.where(dist < max_exact, dist, large).astype(np.int32)


def _band(window, dilation):
    qi = np.arange(ATTN_BLOCK)[:, None]
    kj = np.arange(2 * ATTN_BLOCK)[None, :]
    delta = ATTN_BLOCK + qi - kj
    mask = (delta >= 0) & (delta <= window // dilation)
    bucket = _t5_bucket(np.maximum(delta, 0) * dilation)
    return mask, bucket


def _attn_blocks(s, r):
    m = s // r
    assert m % ATTN_BLOCK == 0, "sequence length must be a multiple of dilation * block"
    return m // ATTN_BLOCK


def _perm_load(ref, r):
    if r == 1:
        return ref[...]
    m = ref.shape[0] // r
    return jnp.concatenate([ref[pl.ds(c, m, stride=r), :] for c in range(r)], axis=0)


def _perm_store(ref, g, val, r, add=False):
    if r == 1:
        ref[g] = ref[g] + val if add else val
        return
    m = val.shape[0] // r
    for c in range(r):
        rows = pl.ds(c, m, stride=r)
        part = val[c * m:(c + 1) * m]
        ref[g, rows, :] = ref[g, rows, :] + part if add else part


def _blocks(x):
    return x.reshape(x.shape[0] // ATTN_BLOCK, ATTN_BLOCK, x.shape[1])


def _prev_blocks(x):
    return jnp.concatenate([x[:1], x[:-1]], axis=0)


def _next_blocks(x):
    return jnp.concatenate([x[1:], jnp.zeros_like(x[:1])], axis=0)


def _first_block_neg(s, r):
    nblk = s // ATTN_BLOCK
    idx = lax.broadcasted_iota(jnp.int32, (nblk, 1, 1), 0)
    return jnp.where(idx % _attn_blocks(s, r) == 0, NEG, 0.0)


def _bdot_nt(a, b):
    return lax.dot_general(a, b, (((2,), (2,)), ((0,), (0,))), preferred_element_type=F32)


def _bdot(a, b):
    return lax.dot_general(a, b, (((2,), (1,)), ((0,), (0,))), preferred_element_type=F32)


def _bdot_tn(a, b):
    return lax.dot_general(a, b, (((1,), (1,)), ((0,), (0,))), preferred_element_type=F32)


def attn_fwd(qkv, biasm, n_heads, name, ride=()):
    b, s, _ = qkv.shape
    h = n_heads
    scale = HEAD_DIM ** -0.5
    blk = ATTN_BLOCK

    def body(q1_ref, q2_ref, q3_ref, k_ref, v_ref, bias_ref, o_ref, lse_ref, acc, m_s, l_s):
        for g, q_ref in enumerate((q1_ref, q2_ref, q3_ref)):
            r = DILATED[g][1]
            first = _first_block_neg(s, r)
            q = _blocks(_perm_load(q_ref, r).astype(BF16))
            k = _blocks(_perm_load(k_ref, r).astype(BF16))
            v = _blocks(_perm_load(v_ref, r).astype(BF16))
            s_cur = _bdot_nt(q, k) * scale + bias_ref[g, :, blk:]
            s_prev = _bdot_nt(q, _prev_blocks(k)) * scale + bias_ref[g, :, :blk] + first
            m = jnp.max(jnp.maximum(s_cur, s_prev), axis=-1, keepdims=True)
            p_cur = jnp.exp(s_cur - m)
            p_prev = jnp.exp(s_prev - m)
            l = jnp.sum(p_cur + p_prev, axis=-1, keepdims=True)
            o = _bdot(p_cur.astype(BF16), v) + _bdot(p_prev.astype(BF16), _prev_blocks(v))
            _perm_store(acc, g, o.reshape(s, HEAD_DIM), r)
            _perm_store(m_s, g, m.reshape(s, 1), r)
            _perm_store(l_s, g, l.reshape(s, 1), r)
        m_all = jnp.maximum(jnp.maximum(m_s[0], m_s[1]), m_s[2])
        w = [jnp.exp(m_s[g] - m_all) for g in range(N_GROUPS)]
        l = w[0] * l_s[0] + w[1] * l_s[1] + w[2] * l_s[2]
        o_ref[...] = (w[0] * acc[0] + w[1] * acc[1] + w[2] * acc[2]) / l
        lse_ref[...] = m_all + jnp.log(l)

    def col(j):
        return pl.BlockSpec((None, s, HEAD_DIM), lambda bi, hi, j=j: (bi, 0, j * h + hi))

    r_ins, r_in_specs, r_outs, r_out_specs, r_sems = _ride_args(ride)
    outs = pl.pallas_call(
        _riding(body, 6, 2, 3, ride, 2), name=name, grid=(b, h),
        in_specs=[col(0), col(1), col(2), col(3), col(4),
                  pl.BlockSpec((N_GROUPS, None, blk, 2 * blk), lambda bi, hi: (0, hi, 0, 0))] + r_in_specs,
        out_specs=[pl.BlockSpec((None, s, HEAD_DIM), lambda bi, hi: (bi, 0, hi)),
                   pl.BlockSpec((None, None, s, 1), lambda bi, hi: (bi, hi, 0, 0))] + r_out_specs,
        out_shape=[jax.ShapeDtypeStruct((b, s, h * HEAD_DIM), F32), jax.ShapeDtypeStruct((b, h, s, 1), F32)] + r_outs,
        scratch_shapes=[pltpu.VMEM((N_GROUPS, s, HEAD_DIM), F32), pltpu.VMEM((N_GROUPS, s, 1), F32),
                        pltpu.VMEM((N_GROUPS, s, 1), F32)] + r_sems,
        compiler_params=_params(("arbitrary", "arbitrary")),
    )(qkv, qkv, qkv, qkv, qkv, biasm, *r_ins)
    return outs[0], outs[1], _ride_results(ride, outs[2:])


def attn_bwd(qkv, biasm, o, lse, do, n_heads, name, ride=()):
    b, s, _ = qkv.shape
    h = n_heads
    scale = HEAD_DIM ** -0.5
    blk = ATTN_BLOCK

    def body(q1_ref, q2_ref, q3_ref, k_ref, v_ref, bias_ref, o_ref, lse_ref, do_ref,
             dq1_ref, dq2_ref, dq3_ref, dk_ref, dv_ref, ds_ref, dq_acc, kv_acc, delta):
        delta[...] = jnp.sum(do_ref[...] * o_ref[...], axis=-1, keepdims=True)
        kv_acc[...] = jnp.zeros(kv_acc.shape, F32)
        for g, q_ref in enumerate((q1_ref, q2_ref, q3_ref)):
            r = DILATED[g][1]
            first = _first_block_neg(s, r)
            q = _blocks(_perm_load(q_ref, r).astype(BF16))
            k = _blocks(_perm_load(k_ref, r).astype(BF16))
            v = _blocks(_perm_load(v_ref, r).astype(BF16))
            dob = _blocks(_perm_load(do_ref, r).astype(BF16))
            lse_b = _blocks(_perm_load(lse_ref, r))
            dl_b = _blocks(_perm_load(delta, r))
            k_prev, v_prev = _prev_blocks(k), _prev_blocks(v)
            p_cur = jnp.exp(_bdot_nt(q, k) * scale + bias_ref[g, :, blk:] - lse_b)
            p_prev = jnp.exp(_bdot_nt(q, k_prev) * scale + bias_ref[g, :, :blk] + first - lse_b)
            ds_cur = p_cur * (_bdot_nt(dob, v) - dl_b)
            ds_prev = p_prev * (_bdot_nt(dob, v_prev) - dl_b)
            ds_ref[g, :, blk:] = jnp.sum(ds_cur, axis=0)
            ds_ref[g, :, :blk] = jnp.sum(ds_prev, axis=0)
            ds_cur_b, ds_prev_b = ds_cur.astype(BF16), ds_prev.astype(BF16)
            dq = (_bdot(ds_cur_b, k) + _bdot(ds_prev_b, k_prev)) * scale
            _perm_store(dq_acc, g, dq.reshape(s, HEAD_DIM), r)
            dk = (_bdot_tn(ds_cur_b, q) + _next_blocks(_bdot_tn(ds_prev_b, q))) * scale
            dv = _bdot_tn(p_cur.astype(BF16), dob) + _next_blocks(_bdot_tn(p_prev.astype(BF16), dob))
            _perm_store(kv_acc, 0, dk.reshape(s, HEAD_DIM), r, add=True)
            _perm_store(kv_acc, 1, dv.reshape(s, HEAD_DIM), r, add=True)
        for g, out_ref in enumerate((dq1_ref, dq2_ref, dq3_ref)):
            out_ref[...] = dq_acc[g].astype(out_ref.dtype)
        dk_ref[...] = kv_acc[0].astype(dk_ref.dtype)
        dv_ref[...] = kv_acc[1].astype(dv_ref.dtype)

    def col(j):
        return pl.BlockSpec((None, s, HEAD_DIM), lambda bi, hi, j=j: (bi, 0, j * h + hi))

    head = pl.BlockSpec((None, s, HEAD_DIM), lambda bi, hi: (bi, 0, hi))
    sd = jax.ShapeDtypeStruct
    r_ins, r_in_specs, r_outs, r_out_specs, r_sems = _ride_args(ride)
    outs = pl.pallas_call(
        _riding(body, 9, 6, 3, ride, 2), name=name, grid=(b, h),
        in_specs=[col(0), col(1), col(2), col(3), col(4),
                  pl.BlockSpec((N_GROUPS, None, blk, 2 * blk), lambda bi, hi: (0, hi, 0, 0)),
                  head, pl.BlockSpec((None, None, s, 1), lambda bi, hi: (bi, hi, 0, 0)), head] + r_in_specs,
        out_specs=[head] * 5 + [pl.BlockSpec((None, None, N_GROUPS, blk, 2 * blk), lambda bi, hi: (bi, hi, 0, 0, 0))]
        + r_out_specs,
        out_shape=[sd((b, s, h * HEAD_DIM), BF16)] * 5 + [sd((b, h, N_GROUPS, blk, 2 * blk), F32)] + r_outs,
        scratch_shapes=[pltpu.VMEM((N_GROUPS, s, HEAD_DIM), F32), pltpu.VMEM((2, s, HEAD_DIM), F32),
                        pltpu.VMEM((s, 1), F32)] + r_sems,
        compiler_params=_params(("arbitrary", "arbitrary")),
    )(qkv, qkv, qkv, qkv, qkv, biasm, o, lse, do, *r_ins)
    return outs[:6], _ride_results(ride, outs[6:])


def bias_table(rel_rows, bucket_f, n_heads, name):
    g, blk, blk2 = bucket_f.shape
    h = n_heads

    def body(rb_ref, bk_ref, o_ref):
        bk = bk_ref[...]
        rb = rb_ref[...]
        acc = jnp.full((blk, blk2), NEG, F32)
        for bucket in range(REL_BUCKETS):
            acc = jnp.where(bk == float(bucket), rb[:, bucket:bucket + 1], acc)
        o_ref[...] = acc

    return pl.pallas_call(
        body, name=name, grid=(g, h),
        in_specs=[pl.BlockSpec((None, 1, 128), lambda gi, hi: (gi * h + hi, 0, 0)),
                  pl.BlockSpec((None, blk, blk2), lambda gi, hi: (gi, 0, 0))],
        out_specs=pl.BlockSpec((None, None, blk, blk2), lambda gi, hi: (gi, hi, 0, 0)),
        out_shape=jax.ShapeDtypeStruct((g, h, blk, blk2), F32),
        compiler_params=_params(("parallel", "parallel")),
    )(rel_rows, bucket_f)


def bias_grad(ds_sum, bucket_f, name):
    b, h, g, blk, blk2 = ds_sum.shape

    def body(ds_ref, bk_ref, o_ref):
        tot = jnp.sum(ds_ref[...], axis=0)
        bk = bk_ref[...]
        lane = lax.broadcasted_iota(jnp.int32, (1, 128), 1)
        vec = jnp.zeros((1, 128), F32)
        for bucket in range(REL_BUCKETS):
            val = jnp.sum(jnp.where(bk == float(bucket), tot, 0.0), keepdims=True)
            vec = vec + jnp.where(lane == bucket, val, 0.0)
        o_ref[...] = vec

    return pl.pallas_call(
        body, name=name, grid=(g, h),
        in_specs=[pl.BlockSpec((b, None, None, blk, blk2), lambda gi, hi: (0, hi, gi, 0, 0)),
                  pl.BlockSpec((None, blk, blk2), lambda gi, hi: (gi, 0, 0))],
        out_specs=pl.BlockSpec((None, 1, 128), lambda gi, hi: (gi * h + hi, 0, 0)),
        out_shape=jax.ShapeDtypeStruct((g * h, 1, 128), F32),
        compiler_params=_params(("parallel", "parallel")),
    )(ds_sum, bucket_f)


def _chip_peers():
    x, y, c = lax.axis_index("x"), lax.axis_index("y"), lax.axis_index("c")
    me = 2 * x + y
    peers = [(1 - x, y, c), (x, 1 - y, c), (1 - x, 1 - y, c)]
    peer_chip = [2 * (1 - x) + y, 2 * x + (1 - y), 2 * (1 - x) + (1 - y)]
    return me, peers, peer_chip


def _any_specs(n):
    return [pl.BlockSpec(memory_space=pl.ANY)] * n


_MID_NUM, _MID_DEN = 3, 4


class _Exchange:
    def start(self, ins, outs, sems):
        local, sends, _ = self._copies(ins, outs, sems)
        for cp in local + sends:
            cp.start()

    def mid(self, ins, outs, sems):
        pass

    def wait(self, ins, outs, sems):
        local, sends, recvs = self._copies(ins, outs, sems)
        for cp in recvs():
            cp.wait_recv()
        for cp in sends:
            cp.wait_send()
        for cp in local:
            cp.wait()


class _Gather(_Exchange):
    HALF_ROWS = 16

    def __init__(self, arrays):
        n = len(arrays)
        self.ins = list(arrays)
        self.split = [a.shape[0] % (2 * self.HALF_ROWS) == 0 for a in arrays]
        self.out_shape = [jax.ShapeDtypeStruct((N_CHIPS,) + a.shape, a.dtype) for a in arrays]
        dma = pltpu.SemaphoreType.DMA
        self.sems = [dma((3 * n,)), dma((3 * n,)), dma((n,)), dma((3 * n,)), dma((3 * n,))]

    def _half(self, i, ref, sibling=False):
        if not self.split[i]:
            return ref
        half = self.ins[i].shape[0] // 2
        c = lax.axis_index("c")
        c = 1 - c if sibling else c
        return ref.at[pl.ds(pl.multiple_of(c * half, self.HALF_ROWS), half)]

    def _plan(self, ins, outs, sems):
        send1, recv1, local_sems, send2, recv2 = sems
        me, peers, peer_chip = _chip_peers()
        x, y, c = lax.axis_index("x"), lax.axis_index("y"), lax.axis_index("c")
        n = len(ins)
        pairs = [(i, k) for i in range(n) for k in range(3)]

        def fetch(i, k, slot):
            return pltpu.make_async_remote_copy(src_ref=self._half(i, ins[i]), dst_ref=self._half(i, outs[i].at[slot]),
                                                send_sem=send1.at[3 * i + k], recv_sem=recv1.at[3 * i + k],
                                                device_id=peers[k], device_id_type=MESH)

        def share(i, k, sibling):
            part = self._half(i, outs[i].at[peer_chip[k]], sibling)
            return pltpu.make_async_remote_copy(src_ref=part, dst_ref=part, send_sem=send2.at[3 * i + k],
                                                recv_sem=recv2.at[3 * i + k], device_id=(x, y, 1 - c),
                                                device_id_type=MESH)

        split_pairs = [(i, k) for i, k in pairs if self.split[i]]
        return dict(
            local=lambda: [pltpu.make_async_copy(ins[i], outs[i].at[me], local_sems.at[i]) for i in range(n)],
            fetch_out=lambda: [fetch(i, k, me) for i, k in pairs],
            fetch_in=lambda: [fetch(i, k, peer_chip[k]) for i, k in pairs],
            share_out=lambda: [share(i, k, False) for i, k in split_pairs],
            share_in=lambda: [share(i, k, True) for i, k in split_pairs])

    def start(self, ins, outs, sems):
        plan = self._plan(ins, outs, sems)
        for cp in plan["local"]() + plan["fetch_out"]():
            cp.start()

    def mid(self, ins, outs, sems):
        plan = self._plan(ins, outs, sems)
        for cp in plan["fetch_in"]():
            cp.wait_recv()
        for cp in plan["share_out"]():
            cp.start()

    def wait(self, ins, outs, sems):
        plan = self._plan(ins, outs, sems)
        for cp in plan["share_in"]():
            cp.wait_recv()
        for cp in plan["fetch_out"]() + plan["share_out"]():
            cp.wait_send()
        for cp in plan["local"]():
            cp.wait()


class _Scatter(_Exchange):
    def __init__(self, slabs, whole=()):
        self.n_slabs = len(slabs)
        self.ins = list(slabs) + list(whole)
        n = len(self.ins)
        self.out_shape = [jax.ShapeDtypeStruct(a.shape, a.dtype) for a in slabs] \
            + [jax.ShapeDtypeStruct((N_CHIPS,) + a.shape, a.dtype) for a in whole]
        self.sems = [pltpu.SemaphoreType.DMA((3 * n,)), pltpu.SemaphoreType.DMA((3 * n,)), pltpu.SemaphoreType.DMA((n,))]

    def _copies(self, ins, outs, sems):
        send_sems, recv_sems, local_sems = sems
        me, peers, peer_chip = _chip_peers()
        n = len(ins)

        def src(i, chip):
            return ins[i].at[chip] if i < self.n_slabs else ins[i]

        def remote(i, k, src_chip, slot):
            return pltpu.make_async_remote_copy(src_ref=src(i, src_chip), dst_ref=outs[i].at[slot],
                                                send_sem=send_sems.at[3 * i + k], recv_sem=recv_sems.at[3 * i + k],
                                                device_id=peers[k], device_id_type=MESH)

        local = [pltpu.make_async_copy(src(i, me), outs[i].at[me], local_sems.at[i]) for i in range(n)]
        sends = [remote(i, k, peer_chip[k], me) for i in range(n) for k in range(3)]
        return local, sends, lambda: [remote(i, k, me, peer_chip[k]) for i in range(n) for k in range(3)]


class _Swap(_Exchange):
    def __init__(self, arrays):
        n = len(arrays)
        self.ins = list(arrays)
        self.out_shape = [jax.ShapeDtypeStruct(a.shape, a.dtype) for a in arrays]
        self.sems = [pltpu.SemaphoreType.DMA((n,)), pltpu.SemaphoreType.DMA((n,))]

    def _copies(self, ins, outs, sems):
        send_sems, recv_sems = sems
        x, y, c = lax.axis_index("x"), lax.axis_index("y"), lax.axis_index("c")
        cps = [pltpu.make_async_remote_copy(src_ref=ins[i], dst_ref=outs[i], send_sem=send_sems.at[i],
                                            recv_sem=recv_sems.at[i], device_id=(x, y, 1 - c), device_id_type=MESH)
               for i in range(len(ins))]
        return [], cps, lambda: cps


def _riding(body, n_in, n_out, n_scratch, ride, rank):
    if not ride:
        return body
    r_in = sum(len(e.ins) for e in ride)
    r_out = sum(len(e.out_shape) for e in ride)

    def split(refs, sizes):
        out, a = [], 0
        for sz in sizes:
            out.append(refs[a:a + sz])
            a += sz
        return out

    def wrapped(*refs):
        a = 0
        parts = []
        for sz in (n_in, r_in, n_out, r_out, n_scratch):
            parts.append(refs[a:a + sz])
            a += sz
        own_in, ex_in, own_out, ex_out, own_scratch = parts
        ex_sems = refs[a:]
        ins = split(ex_in, [len(e.ins) for e in ride])
        outs = split(ex_out, [len(e.out_shape) for e in ride])
        sems = split(ex_sems, [len(e.sems) for e in ride])
        if rank:
            step, total = 0, 1
            for d in range(rank):
                step = step * pl.num_programs(d) + pl.program_id(d)
                total = total * pl.num_programs(d)

            @pl.when(step == 0)
            def _():
                for e, i, o, s in zip(ride, ins, outs, sems):
                    e.start(i, o, s)

            body(*own_in, *own_out, *own_scratch)

            @pl.when(step == (total * _MID_NUM) // _MID_DEN)
            def _():
                for e, i, o, s in zip(ride, ins, outs, sems):
                    e.mid(i, o, s)

            @pl.when(step == total - 1)
            def _():
                for e, i, o, s in zip(ride, ins, outs, sems):
                    e.wait(i, o, s)
        else:
            for phase in ("start", "mid", "wait"):
                for e, i, o, s in zip(ride, ins, outs, sems):
                    getattr(e, phase)(i, o, s)

    return wrapped


def _ride_args(ride):
    ins = [a for e in ride for a in e.ins]
    outs = [s for e in ride for s in e.out_shape]
    sems = [s for e in ride for s in e.sems]
    return ins, _any_specs(len(ins)), outs, _any_specs(len(outs)), sems


def _ride_results(ride, flat):
    out, a = [], 0
    for e in ride:
        out.append(list(flat[a:a + len(e.out_shape)]))
        a += len(e.out_shape)
    return out


def exchange(ride, name):
    ins, in_specs, outs, out_specs, sems = _ride_args(ride)
    res = pl.pallas_call(
        _riding(lambda: None, 0, 0, 0, ride, 0), name=name,
        in_specs=in_specs, out_specs=out_specs, out_shape=outs, scratch_shapes=sems,
    )(*ins)
    return _ride_results(ride, res)


def _sum_slots(ref):
    acc = ref[0].astype(F32)
    for j in range(1, ref.shape[0]):
        acc = acc + ref[j].astype(F32)
    return acc


def sum_pairs(mine, other, name, tr=176):
    n, r, w = mine.shape
    tr = r if r <= tr else _tile(r, tr)

    def body(a_ref, b_ref, o_ref):
        o_ref[...] = _sum_slots(a_ref) + _sum_slots(b_ref)

    spec = pl.BlockSpec((n, tr, w), lambda i: (0, i, 0))
    return pl.pallas_call(
        body, name=name, grid=(r // tr,),
        in_specs=[spec, spec], out_specs=_rows(tr, w),
        out_shape=jax.ShapeDtypeStruct((r, w), F32),
        compiler_params=_params(("parallel",)),
    )(mine, other)


def adamw(w, m, v, gs, name, tr=256):
    r, c = w.shape
    tr = r if r % 8 else _tile(r, tr)
    c1 = 1.0 - ADAM_B1 ** ADAM_STEP
    c2 = 1.0 - ADAM_B2 ** ADAM_STEP
    ng = len(gs)

    def body(w_ref, m_ref, v_ref, *refs):
        g_refs, (g_ref, d_ref, nm_ref, nv_ref) = refs[:ng], refs[ng:]
        g = g_refs[0][...] if ng == 1 else _sum_slots(g_refs[0]) + _sum_slots(g_refs[1])
        nm = ADAM_B1 * m_ref[...] + (1.0 - ADAM_B1) * g
        nv = ADAM_B2 * v_ref[...] + (1.0 - ADAM_B2) * (g * g)
        g_ref[...] = g
        nm_ref[...] = nm
        nv_ref[...] = nv
        d_ref[...] = (-ADAM_LR) * ((nm / c1) / (jnp.sqrt(nv / c2) + ADAM_EPS) + ADAM_WD * w_ref[...])

    spec = _rows(tr, c)
    gspec = spec if ng == 1 else pl.BlockSpec((N_CHIPS, tr, c), lambda i: (0, i, 0))
    return pl.pallas_call(
        body, name=name, grid=(r // tr,),
        in_specs=[spec] * 3 + [gspec] * ng, out_specs=[spec] * 4,
        out_shape=[jax.ShapeDtypeStruct((r, c), F32)] * 4,
        compiler_params=_params(("parallel",)),
    )(w, m, v, *gs)


_PARAMS = (
    ("rel_bias", None), ("norm_mix_pre", None), ("norm_mix_post", None), ("w_in", 1), ("conv_rnn_w", 1),
    ("conv_rnn_b", None), ("w_rg_a", None), ("b_rg_a", None), ("w_rg_x", None), ("b_rg_x", None),
    ("lru_lambda", None), ("w_branch_rnn", 0), ("w_branch_att", 1), ("w_out", 0), ("norm_ffn_pre", None),
    ("norm_ffn_post", None), ("w_ffn_gate", 1), ("w_ffn_up", 1), ("conv_ffn_w", 1), ("conv_ffn_b", None),
    ("w_ffn_down", 0),
)
_SMALL = 65536


def _as2d(a):
    a = a[0] if a.shape[0] == 1 and a.ndim >= 3 else a
    return a.reshape(-1, a.shape[-1]) if a.ndim == 3 else a


def _pack(pieces, dtype):
    flat = jnp.concatenate([p.astype(dtype).reshape(-1) for p in pieces])
    unit = PACK_W * PACK_ROWS
    pad = (-flat.shape[0]) % unit
    flat = jnp.pad(flat, (0, pad))
    return flat.reshape(-1, PACK_W)


def _unpack(buf, shapes):
    flat = buf.reshape(-1)
    out, off = [], 0
    for shp in shapes:
        n = int(np.prod(shp))
        out.append(flat[off:off + n].reshape(shp))
        off += n
    return out


def _join(slots, ax):
    if ax == 0:
        return slots.reshape(-1, slots.shape[-1])
    return jnp.transpose(slots, (1, 0, 2)).reshape(slots.shape[1], -1)


def _cut(full, ax):
    if ax == 0:
        return full.reshape(N_CHIPS, -1, full.shape[-1])
    return jnp.transpose(full.reshape(full.shape[0], N_CHIPS, -1), (1, 0, 2))


def kernel(x, rel_bias, norm_mix_pre, norm_mix_post, w_in, conv_rnn_w, conv_rnn_b, w_rg_a, b_rg_a, w_rg_x, b_rg_x, lru_lambda, w_branch_rnn, w_branch_att, w_out, norm_ffn_pre, norm_ffn_post, w_ffn_gate, w_ffn_up, conv_ffn_w, conv_ffn_b, w_ffn_down, loss_target, m_rel_bias, m_norm_mix_pre, m_norm_mix_post, m_w_in, m_conv_rnn_w, m_conv_rnn_b, m_w_rg_a, m_b_rg_a, m_w_rg_x, m_b_rg_x, m_lru_lambda, m_w_branch_rnn, m_w_branch_att, m_w_out, m_norm_ffn_pre, m_norm_ffn_post, m_w_ffn_gate, m_w_ffn_up, m_conv_ffn_w, m_conv_ffn_b, m_w_ffn_down, v_rel_bias, v_norm_mix_pre, v_norm_mix_post, v_w_in, v_conv_rnn_w, v_conv_rnn_b, v_w_rg_a, v_b_rg_a, v_w_rg_x, v_b_rg_x, v_lru_lambda, v_w_branch_rnn, v_w_branch_att, v_w_out, v_norm_ffn_pre, v_norm_ffn_post, v_w_ffn_gate, v_w_ffn_up, v_conv_ffn_w, v_conv_ffn_b, v_w_ffn_down):
    args = dict(locals())
    names = [n for n, _ in _PARAMS]
    axis = dict(_PARAMS)
    w_loc = {n: args[n] for n in names}
    m_loc = {n: args["m_" + n] for n in names}
    v_loc = {n: args["v_" + n] for n in names}
    sharded = [n for n in names if axis[n] is not None]
    replicated = [n for n in names if axis[n] is None]

    big = [n for n in sharded if w_loc[n].size >= _SMALL]
    small_sharded = [n for n in sharded if n not in big]
    small = replicated + small_sharded

    first = ["w_in"] + small_sharded
    srcs = [_as2d(w_loc[n]).astype(BF16) if n in big else _as2d(w_loc[n]) for n in first]
    (gathered,) = exchange([_Gather(srcs)], "gather_first")
    p = {n: _join(a, axis[n]) for n, a in zip(first, gathered)}
    for n in replicated:
        p[n] = _as2d(w_loc[n])
    shards = {n: _as2d(w_loc[n]).astype(BF16) for n in big if n not in first}

    last = "norm_mix_pre"
    early = [n for n in small if n != last]
    received, sibling, g_small, loss_part = _local_step(x, loss_target, p, shards, early)

    ((received["last"],),) = exchange([_Scatter([], [_pack([g_small[last]], BF16)])], "scatter_last")
    late = [n for n in received if n not in sibling]
    (swapped,) = exchange([_Swap([received[n] for n in late])], "swap_last")
    sibling.update(zip(late, swapped))
    early_sum = sum_pairs(received["small"], sibling["small"], "sum_small")
    last_sum = sum_pairs(received["last"], sibling["last"], "sum_last")
    g_tot = dict(zip(early, _unpack(early_sum, [g_small[n].shape for n in early])))
    (g_tot[last],) = _unpack(last_sum, [g_small[last].shape])
    chip = 2 * lax.axis_index("x") + lax.axis_index("y")
    for n in small_sharded:
        size = g_tot[n].shape[axis[n]] // N_CHIPS
        g_tot[n] = lax.dynamic_slice_in_dim(g_tot[n], chip * size, size, axis=axis[n])

    out_g, out_d, out_m, out_v = {}, {}, {}, {}
    for i, n in enumerate(names):
        shp = w_loc[n].shape
        gs = (received[n], sibling[n]) if n in big else (g_tot[n],)
        g, d, nm, nv = adamw(_as2d(w_loc[n]), _as2d(m_loc[n]), _as2d(v_loc[n]), gs, "adamw_" + n)
        out_g[n], out_d[n], out_m[n], out_v[n] = (t.reshape(shp) for t in (g, d, nm, nv))

    d_model = x.shape[-1]
    loss = lax.psum(0.5 * jnp.sum(loss_part) / d_model, ("x", "y", "c"))
    grad_x = g_small["x"]
    return (loss, grad_x, *[out_g[n] for n in names], *[out_d[n] for n in names],
            *[out_m[n] for n in names], *[out_v[n] for n in names])


def _local_step(x, target, p, shards, small_early):
    axis = dict(_PARAMS)
    b, s, d = x.shape
    t = b * s
    rnn = p["b_rg_a"].shape[1]
    ffn = p["conv_ffn_b"].shape[1]
    nbk = rnn // p["w_rg_a"].shape[1]
    hkv = (p["w_in"].shape[1] - rnn - 2 * d) // (N_GROUPS + 2)
    h = hkv // HEAD_DIM
    nq = N_GROUPS * hkv

    x2 = x.reshape(t, d)
    tgt = target.reshape(t, d)
    w_in = p["w_in"]
    in_splits = (rnn, nq + 2 * hkv, 2 * d)
    wa = p["w_rg_a"].reshape(nbk, -1, p["w_rg_a"].shape[1]).astype(BF16)
    wx = p["w_rg_x"].reshape(nbk, -1, p["w_rg_x"].shape[1]).astype(BF16)
    cw_r, cb_r = p["conv_rnn_w"], p["conv_rnn_b"]
    cw_f, cb_f = p["conv_ffn_w"], p["conv_ffn_b"]

    masks, buckets = zip(*[_band(w_, r_) for w_, r_ in DILATED])
    bucket_f = jnp.asarray(np.where(np.stack(masks), np.stack(buckets), -1).astype(np.float32))
    rel_rows = jnp.pad(p["rel_bias"].T, ((0, 0), (0, 128 - REL_BUCKETS)))[:, None, :]
    biasm = bias_table(rel_rows, bucket_f, h, "bias_table")

    early = ["w_branch_rnn", "w_branch_att", "w_out"]
    hn1, (xr, qkv, gts), (got,) = norm_mm(x2, p["norm_mix_pre"], [w_in], [in_splits], "in_proj",
                                          ride=[_Gather([shards[n] for n in early])])
    p.update({n: _join(a, axis[n]) for n, a in zip(early, got)})
    xr3 = xr.reshape(b, s, rnn)
    (y_rnn, a_rnn, xc_rnn), (got,) = rglru_fwd(xr3, cw_r, cb_r, wa, p["b_rg_a"], wx, p["b_rg_x"], p["lru_lambda"], "rglru_fwd",
                              ride=[_Gather([shards[n] for n in ("w_ffn_gate", "w_ffn_up")])])
    p.update({n: _join(a, axis[n]) for n, a in zip(("w_ffn_gate", "w_ffn_up"), got)})
    qkv3 = qkv.reshape(b, s, -1)
    o_att, lse, ((got,),) = attn_fwd(qkv3, biasm, h, "attn_fwd", ride=[_Gather([shards["w_ffn_down"]])])
    p["w_ffn_down"] = _join(got, axis["w_ffn_down"])
    merged, br, ba, mix, h1 = merge_out(y_rnn.reshape(t, rnn), o_att.reshape(t, hkv), gts, p["w_branch_rnn"],
                                        p["w_branch_att"], p["w_out"], p["norm_mix_post"], x2, "merge_out")
    hn2, gate_pre, up, act = ffn_in_act(h1, p["norm_ffn_pre"], p["w_ffn_gate"], p["w_ffn_up"], cw_f, cb_f, s, "ffn_in")

    g, gb = {}, {}
    recv, sib = {}, {}

    def rows4(a):
        return a.reshape(N_CHIPS, -1, a.shape[-1])

    dy, dff, dact, g["norm_ffn_post"], loss_part = ffn_down_loss(act, p["w_ffn_down"], p["norm_ffn_post"], h1, tgt,
                                                                  "ffn_down")
    gb["w_ffn_down"] = rows4(mm_tn(act, [dff], "ffn_down_dw"))
    (dgp, dup, dhn2, g["conv_ffn_w"], g["conv_ffn_b"]), ((recv["w_ffn_down"],),) = ffn_in_bwd(
        dact, gate_pre, up, cw_f, cb_f, p["w_ffn_gate"], p["w_ffn_up"], s, "ffn_in_bwd",
        ride=[_Scatter([gb["w_ffn_down"]])])
    gb["w_ffn_gate"] = mm_tn(hn2, [dgp], "ffn_gate_dw", col_shards=N_CHIPS)
    gb["w_ffn_up"] = mm_tn(hn2, [dup], "ffn_up_dw", col_shards=N_CHIPS)
    (dh1, dgts, dy_rnn, do_att, g["norm_ffn_pre"], g["norm_mix_post"], dw_out, dw_br,
     gb["w_branch_att"]) = mid_bwd(dhn2, h1, p["norm_ffn_pre"], dy, mix, p["norm_mix_post"], p["w_out"], gts, br, ba,
                                   p["w_branch_rnn"], p["w_branch_att"], merged, y_rnn.reshape(t, rnn),
                                   o_att.reshape(t, hkv), "mid_bwd")
    gb["w_out"], gb["w_branch_rnn"] = rows4(dw_out), rows4(dw_br)
    ffn_in = ["w_ffn_gate", "w_ffn_up"]
    (dxr, g["conv_rnn_w"], g["conv_rnn_b"], dwa, g["b_rg_a"], dwx, g["b_rg_x"], g["lru_lambda"]), (got,) = rglru_bwd(
        xr3, y_rnn, dy_rnn.reshape(b, s, rnn), a_rnn, xc_rnn, cw_r, cb_r, wa, p["b_rg_a"], wx, p["b_rg_x"], p["lru_lambda"], "rglru_bwd",
        ride=[_Scatter([gb[n] for n in ffn_in])])
    recv.update(zip(ffn_in, got))
    g["w_rg_a"] = dwa.reshape(p["w_rg_a"].shape)
    g["w_rg_x"] = dwx.reshape(p["w_rg_x"].shape)
    mid = ["w_out", "w_branch_rnn", "w_branch_att"]
    early_recv = ["w_ffn_down"] + ffn_in
    (dq1, dq2, dq3, dk, dv, ds_sum), (got, swapped) = attn_bwd(
        qkv3, biasm, o_att, lse, do_att.reshape(b, s, hkv), h, "attn_bwd",
        ride=[_Scatter([gb[n] for n in mid]), _Swap([recv[n] for n in early_recv])])
    recv.update(zip(mid, got))
    sib.update(zip(early_recv, swapped))
    rows = bias_grad(ds_sum, bucket_f, "bias_grad")
    g["rel_bias"] = rows[:, 0, :REL_BUCKETS].T
    dproj = [dxr.reshape(t, rnn)] + [a.reshape(t, hkv) for a in (dq1, dq2, dq3, dk, dv)] + [dgts]
    dw_a, (got,) = mm_tn(hn1, dproj[:4], "in_proj_dw_a", ride=[_Swap([recv[n] for n in mid])])
    sib.update(zip(mid, got))
    pack = _pack([g[n] for n in small_early], BF16)
    dw_b, ((recv["small"],),) = mm_tn(hn1, dproj[4:], "in_proj_dw_b", ride=[_Scatter([], [pack])])
    gb["w_in"] = _cut(jnp.concatenate([dw_a[0], dw_b[0]], axis=1), 1)
    dx, g["norm_mix_pre"], ((recv["w_in"],),) = mm_nt(
        [(dproj, w_in)], "in_proj_dx", norm=(x2, p["norm_mix_pre"], dh1), ride=[_Scatter([gb["w_in"]])])
    g["x"] = dx.reshape(b, s, d)
    return recv, sib, g, loss_part
```

```python
import math

import numpy as np
import jax
import jax.numpy as jnp
from jax import lax
from jax.experimental import pallas as pl
from jax.experimental.pallas import tpu as pltpu

F32 = jnp.float32
BF16 = jnp.bfloat16

EPS = 1e-6
HEAD_DIM = 128
ATTN_BLOCK = 128
DILATED = ((128, 1), (512, 4), (2048, 16))
N_GROUPS = len(DILATED)
REL_BUCKETS = 32
REL_MAX_DIST = 2048
LRU_C = 8.0
NEG = -1e30

ADAM_LR = 0.001
ADAM_B1 = 0.9
ADAM_B2 = 0.999
ADAM_EPS = 1e-08
ADAM_WD = 0.01
ADAM_STEP = 10

N_CHIPS = 4
PACK_W = 1024
PACK_ROWS = 16
VMEM_LIMIT = 56 * 1024 * 1024
MESH = pl.DeviceIdType.MESH


def _params(sem=None):
    return pltpu.CompilerParams(dimension_semantics=sem, vmem_limit_bytes=VMEM_LIMIT)


def _dot(a, b):
    return jnp.dot(a, b, preferred_element_type=F32)


def _dot_nt(a, b):
    return lax.dot_general(a, b, (((1,), (1,)), ((), ())), preferred_element_type=F32)


def _dot_tn(a, b):
    return lax.dot_general(a, b, (((0,), (0,)), ((), ())), preferred_element_type=F32)


def _sig(x):
    return 0.5 * jnp.tanh(0.5 * x) + 0.5


def _rows(tm, w):
    return pl.BlockSpec((tm, w), lambda i: (i, 0))


def _whole(shape):
    nd = len(shape)
    return pl.BlockSpec(tuple(shape), lambda *_: (0,) * nd)


def _resident(shape):
    nd = len(shape)
    return pl.BlockSpec(tuple(shape), lambda *_: (0,) * nd, pipeline_mode=pl.Buffered(1))


def _tile(t, want):
    while t % want:
        want //= 2
    return want


def norm_mm(x, g, ws, splits, name, ride=(), tm=512):
    t, d = x.shape
    tm = _tile(t, tm)
    nw = len(ws)
    widths = [n for sp in splits for n in sp]

    def body(x_ref, g_ref, *refs):
        w_refs, hn_ref, o_refs = refs[:nw], refs[nw], refs[nw + 1:]
        xv = x_ref[...]
        inv = lax.rsqrt(jnp.mean(xv * xv, axis=-1, keepdims=True) + EPS)
        hn = (xv * inv * g_ref[...]).astype(BF16)
        hn_ref[...] = hn
        o = 0
        for w_ref, sp in zip(w_refs, splits):
            off = 0
            for n in sp:
                o_refs[o][...] = _dot(hn, w_ref[:, off:off + n])
                off += n
                o += 1

    r_ins, r_in_specs, r_outs, r_out_specs, r_sems = _ride_args(ride)
    n_out = 1 + len(widths)
    outs = pl.pallas_call(
        _riding(body, 2 + nw, n_out, 0, ride, 1), name=name, grid=(t // tm,),
        in_specs=[_rows(tm, d), _whole(g.shape)] + [_resident(w.shape) for w in ws] + r_in_specs,
        out_specs=[_rows(tm, d)] + [_rows(tm, n) for n in widths] + r_out_specs,
        out_shape=[jax.ShapeDtypeStruct((t, d), BF16)] + [jax.ShapeDtypeStruct((t, n), F32) for n in widths] + r_outs,
        scratch_shapes=r_sems,
        compiler_params=_params(("arbitrary",)),
    )(x, g, *ws, *r_ins)
    return outs[0], outs[1:n_out], _ride_results(ride, outs[n_out:])


def mm_nt(groups, name, ride=(), norm=None, tm=512):
    dys_all = [dy for dys, _ in groups for dy in dys]
    ws = [w for _, w in groups]
    t = dys_all[0].shape[0]
    k = ws[0].shape[0]
    tm = _tile(t, tm)
    n = len(dys_all)
    extra = list(norm) if norm else []

    def body(*refs):
        dy_refs, w_refs = refs[:n], refs[n:n + len(ws)]
        rest = refs[n + len(ws):]
        acc = None
        i = 0
        for (dys, _), w_ref in zip(groups, w_refs):
            off = 0
            for dy in dys:
                width = dy.shape[1]
                part = _dot_nt(dy_refs[i][...].astype(BF16), w_ref[:, off:off + width])
                acc = part if acc is None else acc + part
                off += width
                i += 1
        if norm:
            u_ref, g_ref, add_ref, o_ref, dg_ref = rest

            @pl.when(pl.program_id(0) == 0)
            def _():
                dg_ref[...] = jnp.zeros(dg_ref.shape, F32)

            du, dg_rows = _rms_bwd(acc, u_ref[...], g_ref[...])
            o_ref[...] = du + add_ref[...]
            dg_ref[...] += jnp.sum(dg_rows, axis=0, keepdims=True)
        else:
            rest[0][...] = acc

    n_out = 2 if norm else 1
    r_ins, r_in_specs, r_outs, r_out_specs, r_sems = _ride_args(ride)
    outs = pl.pallas_call(
        _riding(body, n + len(ws) + len(extra), n_out, 0, ride, 1), name=name, grid=(t // tm,),
        in_specs=[_rows(tm, dy.shape[1]) for dy in dys_all] + [_resident(w.shape) for w in ws]
        + ([_rows(tm, k), _whole((1, k)), _rows(tm, k)] if norm else []) + r_in_specs,
        out_specs=[_rows(tm, k)] + ([_whole((1, k))] if norm else []) + r_out_specs,
        out_shape=[jax.ShapeDtypeStruct((t, k), F32)] + ([jax.ShapeDtypeStruct((1, k), F32)] if norm else []) + r_outs,
        scratch_shapes=r_sems,
        compiler_params=_params(("arbitrary",)),
    )(*dys_all, *ws, *extra, *r_ins)
    return tuple(outs[:n_out]) + (_ride_results(ride, outs[n_out:]),)


def mm_tn(a, dys, name, col_shards=1, ride=(), tm=1024):
    t, k = a.shape
    tm = _tile(t, tm)
    n = len(dys)
    ntot = sum(dy.shape[1] for dy in dys)
    wsh = ntot // col_shards

    def body(a_ref, *refs):
        dy_refs, o_ref, acc = refs[:n], refs[n], refs[n + 1]

        @pl.when(pl.program_id(0) == 0)
        def _():
            acc[...] = jnp.zeros(acc.shape, F32)

        av = a_ref[...].astype(BF16)
        off = 0
        for dy_ref in dy_refs:
            width = dy_ref.shape[1]
            acc[:, off:off + width] += _dot_tn(av, dy_ref[...].astype(BF16))
            off += width

        @pl.when(pl.program_id(0) == pl.num_programs(0) - 1)
        def _():
            for j in range(col_shards):
                o_ref[j] = acc[:, j * wsh:(j + 1) * wsh].astype(o_ref.dtype)

    r_ins, r_in_specs, r_outs, r_out_specs, r_sems = _ride_args(ride)
    outs = pl.pallas_call(
        _riding(body, 1 + n, 1, 1, ride, 1), name=name, grid=(t // tm,),
        in_specs=[_rows(tm, k)] + [_rows(tm, dy.shape[1]) for dy in dys] + r_in_specs,
        out_specs=[_whole((col_shards, k, wsh))] + r_out_specs,
        out_shape=[jax.ShapeDtypeStruct((col_shards, k, wsh), BF16)] + r_outs,
        scratch_shapes=[pltpu.VMEM((k, ntot), F32)] + r_sems,
        compiler_params=_params(("arbitrary",)),
    )(a, *dys, *r_ins)
    return (outs[0], _ride_results(ride, outs[1:])) if ride else outs[0]


def _rms_bwd(dz, u, g):
    d = u.shape[-1]
    inv = lax.rsqrt(jnp.mean(u * u, axis=-1, keepdims=True) + EPS)
    dzg = dz * g
    proj = jnp.sum(dzg * u, axis=-1, keepdims=True) * (1.0 / d)
    du = inv * (dzg - u * (inv * inv) * proj)
    dg_rows = dz * u * inv
    return du, dg_rows


def ffn_down_loss(act, wd, g, h1, target, name, tm=512):
    t, f = act.shape
    d = wd.shape[1]
    tm = _tile(t, tm)

    def body(a_ref, w_ref, g_ref, h_ref, t_ref, dy_ref, dff_ref, dact_ref, dg_ref, loss_ref):
        @pl.when(pl.program_id(0) == 0)
        def _():
            dg_ref[...] = jnp.zeros(dg_ref.shape, F32)
            loss_ref[...] = jnp.zeros(loss_ref.shape, F32)

        wv = w_ref[...]
        gv = g_ref[...]
        ff = _dot(a_ref[...], wv)
        inv = lax.rsqrt(jnp.mean(ff * ff, axis=-1, keepdims=True) + EPS)
        err = h_ref[...] + ff * inv * gv - t_ref[...]
        loss_ref[...] += jnp.sum(err * err, axis=0, keepdims=True)
        dy = err * (1.0 / d)
        dy_ref[...] = dy
        du, dg_rows = _rms_bwd(dy, ff, gv)
        dff = du.astype(BF16)
        dff_ref[...] = dff
        dg_ref[...] += jnp.sum(dg_rows, axis=0, keepdims=True)
        dact_ref[...] = _dot_nt(dff, wv)

    return pl.pallas_call(
        body, name=name, grid=(t // tm,),
        in_specs=[_rows(tm, f), _resident(wd.shape), _whole(g.shape), _rows(tm, d), _rows(tm, d)],
        out_specs=[_rows(tm, d), _rows(tm, d), _rows(tm, f), _whole((1, d)), _whole((1, d))],
        out_shape=[jax.ShapeDtypeStruct((t, d), F32), jax.ShapeDtypeStruct((t, d), BF16),
                   jax.ShapeDtypeStruct((t, f), F32), jax.ShapeDtypeStruct((1, d), F32),
                   jax.ShapeDtypeStruct((1, d), F32)],
        compiler_params=_params(("arbitrary",)),
    )(act, wd, g, h1, target)


def merge_out(y_rnn, o_att, gts, w_br, w_ba, w_out, g, x, name, tm=256):
    t = y_rnn.shape[0]
    d = w_br.shape[1]
    tm = _tile(t, tm)

    def body(y_ref, o_ref, g_ref, wbr_ref, wba_ref, wo_ref, gn_ref, x_ref, m_ref, br_ref, ba_ref, mix_ref, h_ref):
        br = _dot(y_ref[...].astype(BF16), wbr_ref[...])
        ba = _dot(o_ref[...].astype(BF16), wba_ref[...])
        gv = g_ref[...]
        merged = (_sig(gv[:, :d]) * br + _sig(gv[:, d:]) * ba).astype(BF16)
        m_ref[...] = merged
        br_ref[...] = br
        ba_ref[...] = ba
        mix = _dot(merged, wo_ref[...])
        mix_ref[...] = mix
        inv = lax.rsqrt(jnp.mean(mix * mix, axis=-1, keepdims=True) + EPS)
        h_ref[...] = x_ref[...] + mix * inv * gn_ref[...]

    sd = jax.ShapeDtypeStruct
    return pl.pallas_call(
        body, name=name, grid=(t // tm,),
        in_specs=[_rows(tm, y_rnn.shape[1]), _rows(tm, o_att.shape[1]), _rows(tm, 2 * d),
                  _whole(w_br.shape), _whole(w_ba.shape), _whole(w_out.shape), _whole(g.shape), _rows(tm, d)],
        out_specs=[_rows(tm, d)] * 5,
        out_shape=[sd((t, d), BF16), sd((t, d), F32), sd((t, d), F32), sd((t, d), F32), sd((t, d), F32)],
        compiler_params=_params(("parallel",)),
    )(y_rnn, o_att, gts, w_br, w_ba, w_out, g, x)


def mid_bwd(dhn2, h1, g_ffn, dy, mix, g_mix, w_out, gts, br, ba, w_br, w_ba, merged, y_rnn, o_att, name, tm=256):
    t, d = h1.shape
    tm = _tile(t, tm)
    rnn, hkv = w_br.shape[0], w_ba.shape[0]
    wsh = d // N_CHIPS

    def body(dhn_ref, h_ref, gf_ref, dy_ref, mix_ref, gm_ref, wo_ref, g_ref, br_ref, ba_ref, wbr_ref, wba_ref,
             m_ref, y_ref, o_ref, dh_ref, dg_ref, dyr_ref, doa_ref, dgf_ref, dgm_ref, dwo_ref, dwbr_ref, dwba_ref,
             acc_o, acc_br, acc_ba):
        @pl.when(pl.program_id(0) == 0)
        def _():
            dgf_ref[...] = jnp.zeros(dgf_ref.shape, F32)
            dgm_ref[...] = jnp.zeros(dgm_ref.shape, F32)
            acc_o[...] = jnp.zeros(acc_o.shape, F32)
            acc_br[...] = jnp.zeros(acc_br.shape, F32)
            acc_ba[...] = jnp.zeros(acc_ba.shape, F32)

        du, rows_f = _rms_bwd(dhn_ref[...], h_ref[...], gf_ref[...])
        dh1 = du + dy_ref[...]
        dh_ref[...] = dh1
        dgf_ref[...] += jnp.sum(rows_f, axis=0, keepdims=True)
        dmx, rows_m = _rms_bwd(dh1, mix_ref[...], gm_ref[...])
        dmix = dmx.astype(BF16)
        acc_o[...] += _dot_tn(m_ref[...], dmix)
        dgm_ref[...] += jnp.sum(rows_m, axis=0, keepdims=True)
        dm = _dot_nt(dmix, wo_ref[...])
        gv = g_ref[...]
        sr = _sig(gv[:, :d])
        sa = _sig(gv[:, d:])
        dbr = (dm * sr).astype(BF16)
        dba = (dm * sa).astype(BF16)
        acc_br[...] += _dot_tn(y_ref[...].astype(BF16), dbr)
        acc_ba[...] += _dot_tn(o_ref[...].astype(BF16), dba)
        dg_ref[:, :d] = (dm * br_ref[...] * sr * (1.0 - sr)).astype(BF16)
        dg_ref[:, d:] = (dm * ba_ref[...] * sa * (1.0 - sa)).astype(BF16)
        dyr_ref[...] = _dot_nt(dbr, wbr_ref[...])
        doa_ref[...] = _dot_nt(dba, wba_ref[...])

        @pl.when(pl.program_id(0) == pl.num_programs(0) - 1)
        def _():
            dwo_ref[...] = acc_o[...].astype(BF16)
            dwbr_ref[...] = acc_br[...].astype(BF16)
            for j in range(N_CHIPS):
                dwba_ref[j] = acc_ba[:, j * wsh:(j + 1) * wsh].astype(BF16)

    sd = jax.ShapeDtypeStruct
    row, vec = _rows(tm, d), _whole((1, d))
    once = pl.Buffered(1)

    def resident(shape):
        return pl.BlockSpec(shape, lambda i: (0,) * len(shape), pipeline_mode=once)

    return pl.pallas_call(
        body, name=name, grid=(t // tm,),
        in_specs=[row, row, vec, row, row, vec, resident(w_out.shape), _rows(tm, 2 * d), row, row,
                  resident(w_br.shape), resident(w_ba.shape), row, _rows(tm, rnn), _rows(tm, hkv)],
        out_specs=[row, _rows(tm, 2 * d), _rows(tm, rnn), _rows(tm, hkv), vec, vec,
                   resident((d, d)), resident((rnn, d)), resident((N_CHIPS, hkv, wsh))],
        out_shape=[sd((t, d), F32), sd((t, 2 * d), BF16), sd((t, rnn), F32), sd((t, hkv), F32), sd((1, d), F32),
                   sd((1, d), F32), sd((d, d), BF16), sd((rnn, d), BF16), sd((N_CHIPS, hkv, wsh), BF16)],
        scratch_shapes=[pltpu.VMEM((d, d), F32), pltpu.VMEM((rnn, d), F32), pltpu.VMEM((hkv, d), F32)],
        compiler_params=_params(("arbitrary",)),
    )(dhn2, h1, g_ffn, dy, mix, g_mix, w_out, gts, br, ba, w_br, w_ba, merged, y_rnn, o_att)


def _shift_dn(x, d, fill, row):
    return jnp.where(row >= d, pltpu.roll(x, d, 0), fill)


def _shift_up(x, d, fill, row):
    s = x.shape[0]
    return jnp.where(row < s - d, pltpu.roll(x, s - d, 0), fill)


def _conv_fwd(x, w, b, row):
    kk = w.shape[0]
    y = b + w[kk - 1:kk, :] * x
    for j in range(1, kk):
        y = y + w[kk - 1 - j:kk - j, :] * _shift_dn(x, j, 0.0, row)
    return y


def _conv_bwd(dy, x, w, row):
    kk = w.shape[0]
    dx = w[kk - 1:kk, :] * dy
    dws = [None] * kk
    dws[kk - 1] = jnp.sum(dy * x, axis=0, keepdims=True)
    for j in range(1, kk):
        ahead = _shift_up(dy, j, 0.0, row)
        dx = dx + w[kk - 1 - j:kk - j, :] * ahead
        dws[kk - 1 - j] = jnp.sum(ahead * x, axis=0, keepdims=True)
    return dx, jnp.concatenate(dws, axis=0)


def _softplus(z):
    y = jnp.exp(-jnp.abs(z))
    u = 1.0 + y
    dd = u - 1.0
    log1p = jnp.where(dd == 0.0, y, jnp.log(u) * (y / jnp.where(dd == 0.0, 1.0, dd)))
    return jnp.maximum(z, 0.0) + log1p


def _lru_decay(xb, wa, ba, lam):
    r = _sig(_dot(xb, wa) + ba)
    sp = _softplus(-lam)
    la = (-LRU_C) * r * sp
    return r, sp, la, jnp.exp(la)


def _lru_gates(xc, wa, ba, wx, bx, lam):
    xb = xc.astype(BF16)
    r, sp, la, a = _lru_decay(xb, wa, ba, lam)
    i = _sig(_dot(xb, wx) + bx)
    one_m_a2 = jnp.tanh(-la) * (1.0 + a * a)
    inv_mult = lax.rsqrt(one_m_a2)
    return r, i, sp, a, one_m_a2 * inv_mult, inv_mult


def _seg_len(s):
    seg = -(-s // 8)
    return seg + (4 - seg % 8) % 8


def _scan_rows(a_pad, u_pad, out_pad, reverse):
    planes, rows8, lanes = a_pad.shape
    seg = rows8 // 8
    sub = lax.broadcasted_iota(jnp.int32, (planes, 8, lanes), 1)

    unroll = 4

    def rows(k, d):
        i = k * unroll + d
        return pl.ds((seg - 1 - i) if reverse else i, 8, stride=seg)

    def ends(k, carry):
        h, p = carry
        for d in range(unroll):
            a = a_pad[:, rows(k, d), :]
            h = a * h + u_pad[:, rows(k, d), :]
            p = a * p
        return h, p

    init = (jnp.zeros((planes, 8, lanes), F32), jnp.ones((planes, 8, lanes), F32))
    h_end, p_end = lax.fori_loop(0, seg // unroll, ends, init)
    start = jnp.zeros((planes, 8, lanes), F32)
    for _ in range(7):
        nxt = h_end + p_end * start
        if reverse:
            start = jnp.where(sub < 7, pltpu.roll(nxt, 7, 1), 0.0)
        else:
            start = jnp.where(sub >= 1, pltpu.roll(nxt, 1, 1), 0.0)

    def redo(k, h):
        for d in range(unroll):
            h = a_pad[:, rows(k, d), :] * h + u_pad[:, rows(k, d), :]
            out_pad[:, rows(k, d), :] = h
        return h

    lax.fori_loop(0, seg // unroll, redo, start)


def _lru_cols(c, rb):
    return 2 * rb if c % (2 * rb) == 0 else rb


def rglru_fwd(xr, cw, cb, wa, ba, wx, bx, lam, name, ride=()):
    b, s, c = xr.shape
    rb = wa.shape[1]
    kk = cw.shape[0]
    cols = _lru_cols(c, rb)
    nj = cols // rb
    seg = _seg_len(s)

    def body(x_ref, cw_ref, cb_ref, wa_ref, ba_ref, wx_ref, bx_ref, lam_ref, h_ref, a_ref, xc_ref, a_pad, u_pad, h_pad):
        row = lax.broadcasted_iota(jnp.int32, (s, rb), 0)
        for j in range(nj):
            cs = slice(j * rb, (j + 1) * rb)
            xc = _conv_fwd(x_ref[:, cs], cw_ref[:, cs], cb_ref[:, cs], row)
            _, i, _, a, mult, _ = _lru_gates(xc, wa_ref[j], ba_ref[:, cs], wx_ref[j], bx_ref[:, cs], lam_ref[:, cs])
            xc_ref[:, cs] = xc
            a_ref[:, cs] = a
            a_pad[j, 0:s, :] = a
            u_pad[j, 0:s, :] = mult * (i * xc)
        a_pad[:, s:, :] = jnp.ones((nj, 8 * seg - s, rb), F32)
        u_pad[:, s:, :] = jnp.zeros((nj, 8 * seg - s, rb), F32)
        _scan_rows(a_pad, u_pad, h_pad, False)
        for j in range(nj):
            h_ref[:, j * rb:(j + 1) * rb] = h_pad[j, 0:s, :]

    vec = pl.BlockSpec((1, cols), lambda bi, n: (0, n))
    seq = pl.BlockSpec((None, s, cols), lambda bi, n: (bi, 0, n))
    mat = pl.BlockSpec((nj, rb, rb), lambda bi, n: (n, 0, 0))
    r_ins, r_in_specs, r_outs, r_out_specs, r_sems = _ride_args(ride)
    outs = pl.pallas_call(
        _riding(body, 8, 3, 3, ride, 2), name=name, grid=(b, c // cols),
        in_specs=[seq, pl.BlockSpec((kk, cols), lambda bi, n: (0, n)), vec, mat, vec, mat, vec, vec] + r_in_specs,
        out_specs=[seq] * 3 + r_out_specs,
        out_shape=[jax.ShapeDtypeStruct((b, s, c), F32)] * 3 + r_outs,
        scratch_shapes=[pltpu.VMEM((nj, 8 * seg, rb), F32)] * 3 + r_sems,
        compiler_params=_params(("arbitrary", "arbitrary")),
    )(xr, cw, cb, wa, ba, wx, bx, lam, *r_ins)
    return outs[:3], _ride_results(ride, outs[3:])


def rglru_bwd(xr, h, dh, a_fwd, xc_fwd, cw, wa, ba, wx, bx, lam, name, ride=()):
    b, s, c = xr.shape
    nb, rb = wa.shape[0], wa.shape[1]
    kk = cw.shape[0]
    cols = _lru_cols(c, rb)
    nj = cols // rb
    seg = _seg_len(s)

    def body(x_ref, h_ref, dh_ref, a_ref, xc_ref, cw_ref, wa_ref, ba_ref, wx_ref, bx_ref, lam_ref,
             dx_ref, dcw_ref, dcb_ref, dwa_ref, dba_ref, dwx_ref, dbx_ref, dlam_ref, b_pad, g_pad, l_pad):
        @pl.when(pl.program_id(1) == 0)
        def _():
            for ref in (dcw_ref, dcb_ref, dwa_ref, dba_ref, dwx_ref, dbx_ref, dlam_ref):
                ref[...] = jnp.zeros(ref.shape, F32)

        row = lax.broadcasted_iota(jnp.int32, (s, rb), 0)

        for j in range(nj):
            b_pad[j, 0:s, :] = _shift_up(a_ref[:, j * rb:(j + 1) * rb], 1, 0.0, row)
            g_pad[j, 0:s, :] = dh_ref[:, j * rb:(j + 1) * rb]
        b_pad[:, s:, :] = jnp.zeros((nj, 8 * seg - s, rb), F32)
        g_pad[:, s:, :] = jnp.zeros((nj, 8 * seg - s, rb), F32)
        _scan_rows(b_pad, g_pad, l_pad, True)

        for j in range(nj):
            cs = slice(j * rb, (j + 1) * rb)
            x = x_ref[:, cs]
            cwv = cw_ref[:, cs]
            wav, wxv, lamv = wa_ref[j], wx_ref[j], lam_ref[:, cs]
            xc = xc_ref[:, cs]
            r, i, sp, a, mult, inv_mult = _lru_gates(xc, wav, ba_ref[:, cs], wxv, bx_ref[:, cs], lamv)
            lmb = l_pad[j, 0:s, :]
            h_prev = _shift_dn(h_ref[:, cs], 1, 0.0, row)
            da = lmb * h_prev
            ixc = i * xc
            dla = da * a - (lmb * ixc) * (a * a) * inv_mult
            di = lmb * mult * xc
            dxc = lmb * mult * i
            dr = dla * ((-LRU_C) * sp)
            dsp = jnp.sum(dla * ((-LRU_C) * r), axis=0, keepdims=True)
            dga = dr * r * (1.0 - r)
            dgx = di * i * (1.0 - i)
            dga_b, dgx_b = dga.astype(BF16), dgx.astype(BF16)
            xb = xc.astype(BF16)
            dwa_ref[j] += _dot_tn(xb, dga_b)
            dwx_ref[j] += _dot_tn(xb, dgx_b)
            dba_ref[:, cs] += jnp.sum(dga, axis=0, keepdims=True)
            dbx_ref[:, cs] += jnp.sum(dgx, axis=0, keepdims=True)
            dlam_ref[:, cs] += dsp * (-_sig(-lamv))
            dxc = dxc + _dot_nt(dga_b, wav) + _dot_nt(dgx_b, wxv)
            dcb_ref[:, cs] += jnp.sum(dxc, axis=0, keepdims=True)
            dx, dcw = _conv_bwd(dxc, x, cwv, row)
            dcw_ref[:, cs] += dcw
            dx_ref[:, cs] = dx.astype(dx_ref.dtype)

    vec = pl.BlockSpec((1, cols), lambda n, bi: (0, n))
    seq = pl.BlockSpec((None, s, cols), lambda n, bi: (bi, 0, n))
    mat = pl.BlockSpec((nj, rb, rb), lambda n, bi: (n, 0, 0))
    cws = pl.BlockSpec((kk, cols), lambda n, bi: (0, n))
    sd = jax.ShapeDtypeStruct
    r_ins, r_in_specs, r_outs, r_out_specs, r_sems = _ride_args(ride)
    outs = pl.pallas_call(
        _riding(body, 11, 8, 3, ride, 2), name=name, grid=(c // cols, b),
        in_specs=[seq, seq, seq, seq, seq, cws, mat, vec, mat, vec, vec] + r_in_specs,
        out_specs=[seq, cws, vec, mat, vec, mat, vec, vec] + r_out_specs,
        out_shape=[sd((b, s, c), BF16), sd((kk, c), F32), sd((1, c), F32), sd((nb, rb, rb), F32),
                   sd((1, c), F32), sd((nb, rb, rb), F32), sd((1, c), F32), sd((1, c), F32)] + r_outs,
        scratch_shapes=[pltpu.VMEM((nj, 8 * seg, rb), F32)] * 3 + r_sems,
        compiler_params=_params(("arbitrary", "arbitrary")),
    )(xr, h, dh, a_fwd, xc_fwd, cw, wa, ba, wx, bx, lam, *r_ins)
    return outs[:8], _ride_results(ride, outs[8:])


_GELU_C = math.sqrt(2.0 / math.pi)


def _gelu_parts(x):
    th = jnp.tanh(_GELU_C * (x + 0.044715 * x * x * x))
    gel = 0.5 * x * (1.0 + th)
    dgel = 0.5 * (1.0 + th) + 0.5 * x * (1.0 - th * th) * _GELU_C * (1.0 + 3 * 0.044715 * x * x)
    return gel, dgel


def ffn_in_act(x, g, wg, wu, cw, cb, seq_len, name, tm=256):
    t, d = x.shape
    f = wg.shape[1]
    kk = cw.shape[0]
    tm = _tile(seq_len, tm)
    tiles_per_seq = seq_len // tm
    keep = 8
    assert kk - 1 <= keep

    def body(x_ref, g_ref, wg_ref, wu_ref, cw_ref, cb_ref, hn_ref, gp_ref, up_ref, act_ref, tail):
        @pl.when(pl.program_id(0) % tiles_per_seq == 0)
        def _():
            tail[...] = jnp.zeros(tail.shape, F32)

        xv = x_ref[...]
        inv = lax.rsqrt(jnp.mean(xv * xv, axis=-1, keepdims=True) + EPS)
        hn = (xv * inv * g_ref[...]).astype(BF16)
        hn_ref[...] = hn
        gp = _dot(hn, wg_ref[...])
        up = _dot(hn, wu_ref[...])
        gp_ref[...] = gp
        up_ref[...] = up
        cwv = cw_ref[...]
        row = lax.broadcasted_iota(jnp.int32, (tm, 1), 0)
        gate = _conv_fwd(gp, cwv, cb_ref[...], row)
        row8 = lax.broadcasted_iota(jnp.int32, (keep, 1), 0)
        prev = tail[...]
        fix = jnp.zeros((keep, f), F32)
        for j in range(1, kk):
            fix = fix + cwv[kk - 1 - j:kk - j, :] * jnp.where(row8 < j, pltpu.roll(prev, j, 0), 0.0)
        gate = jnp.concatenate([gate[:keep] + fix, gate[keep:]], axis=0)
        tail[...] = gp[tm - keep:, :]
        gel, _ = _gelu_parts(gate)
        act_ref[...] = (gel * up).astype(BF16)

    sd = jax.ShapeDtypeStruct
    return pl.pallas_call(
        body, name=name, grid=(t // tm,),
        in_specs=[_rows(tm, d), _whole(g.shape), _whole(wg.shape), _whole(wu.shape), _whole(cw.shape), _whole(cb.shape)],
        out_specs=[_rows(tm, d), _rows(tm, f), _rows(tm, f), _rows(tm, f)],
        out_shape=[sd((t, d), BF16), sd((t, f), F32), sd((t, f), F32), sd((t, f), BF16)],
        scratch_shapes=[pltpu.VMEM((keep, f), F32)],
        compiler_params=_params(("arbitrary",)),
    )(x, g, wg, wu, cw, cb)


def ffn_in_bwd(dact, gate_pre, up, cw, cb, wg, wu, seq_len, name, ride=(), tm=256):
    t, f = gate_pre.shape
    d = wg.shape[0]
    kk = cw.shape[0]
    tm = _tile(seq_len, tm)
    nt = t // tm
    tiles_per_seq = seq_len // tm
    keep = 8
    assert kk - 1 <= keep

    def body(da_ref, g_ref, halo_ref, u_ref, cw_ref, cb_ref, wg_ref, wu_ref,
             dg_ref, du_ref, dhn_ref, dcw_ref, dcb_ref, nxt):
        tile = (nt - 1 - pl.program_id(0)) % tiles_per_seq

        @pl.when(pl.program_id(0) == 0)
        def _():
            dcw_ref[...] = jnp.zeros(dcw_ref.shape, F32)
            dcb_ref[...] = jnp.zeros(dcb_ref.shape, F32)

        @pl.when(tile == tiles_per_seq - 1)
        def _():
            nxt[...] = jnp.zeros(nxt.shape, F32)

        row = lax.broadcasted_iota(jnp.int32, (tm, 1), 0)
        row8 = lax.broadcasted_iota(jnp.int32, (keep, 1), 0)
        gp = g_ref[...]
        cwv = cw_ref[...]
        prev = jnp.where(tile > 0, halo_ref[...], 0.0)
        gate = _conv_fwd(gp, cwv, cb_ref[...], row)
        fix = jnp.zeros((keep, f), F32)
        for j in range(1, kk):
            fix = fix + cwv[kk - 1 - j:kk - j, :] * jnp.where(row8 < j, pltpu.roll(prev, j, 0), 0.0)
        gate = jnp.concatenate([gate[:keep] + fix, gate[keep:]], axis=0)
        gel, dgel = _gelu_parts(gate)
        da = da_ref[...]
        dup = (da * gel).astype(BF16)
        du_ref[...] = dup
        dgate = da * u_ref[...] * dgel
        dcb_ref[...] += jnp.sum(dgate, axis=0, keepdims=True)
        after = nxt[...]
        dgp = cwv[kk - 1:kk, :] * dgate
        tail_fix = jnp.zeros((keep, f), F32)
        dws = [None] * kk
        dws[kk - 1] = jnp.sum(dgate * gp, axis=0, keepdims=True)
        for j in range(1, kk):
            wj = cwv[kk - 1 - j:kk - j, :]
            ahead = _shift_up(dgate, j, 0.0, row)
            ahead_next = jnp.where(row8 >= keep - j, pltpu.roll(after, keep - j, 0), 0.0)
            dgp = dgp + wj * ahead
            tail_fix = tail_fix + wj * ahead_next
            dws[kk - 1 - j] = (jnp.sum(ahead * gp, axis=0, keepdims=True)
                               + jnp.sum(ahead_next * gp[tm - keep:], axis=0, keepdims=True))
        dgp = jnp.concatenate([dgp[:tm - keep], dgp[tm - keep:] + tail_fix], axis=0).astype(BF16)
        nxt[...] = dgate[:keep]
        dcw_ref[...] += jnp.concatenate(dws, axis=0)
        dg_ref[...] = dgp
        dhn_ref[...] = _dot_nt(dgp, wg_ref[...]) + _dot_nt(dup, wu_ref[...])

    def rev(i):
        return nt - 1 - i

    rows_f = pl.BlockSpec((tm, f), lambda i: (rev(i), 0))
    halo = pl.BlockSpec((None, keep, f), lambda i: (jnp.maximum(rev(i) * (tm // keep) - 1, 0), 0, 0))
    once = pl.Buffered(1)
    sd = jax.ShapeDtypeStruct
    r_ins, r_in_specs, r_outs, r_out_specs, r_sems = _ride_args(ride)
    outs = pl.pallas_call(
        _riding(body, 8, 5, 1, ride, 1), name=name, grid=(nt,),
        in_specs=[rows_f, rows_f, halo, rows_f, _whole(cw.shape), _whole(cb.shape),
                  pl.BlockSpec(wg.shape, lambda i: (0, 0), pipeline_mode=once),
                  pl.BlockSpec(wu.shape, lambda i: (0, 0), pipeline_mode=once)] + r_in_specs,
        out_specs=[rows_f, rows_f, pl.BlockSpec((tm, d), lambda i: (rev(i), 0)), _whole((kk, f)), _whole((1, f))]
        + r_out_specs,
        out_shape=[sd((t, f), BF16), sd((t, f), BF16), sd((t, d), F32), sd((kk, f), F32), sd((1, f), F32)] + r_outs,
        scratch_shapes=[pltpu.VMEM((keep, f), F32)] + r_sems,
        compiler_params=_params(("arbitrary",)),
    )(dact, gate_pre, gate_pre.reshape(t // keep, keep, f), up, cw, cb, wg, wu, *r_ins)
    return outs[:5], _ride_results(ride, outs[5:])


def _t5_bucket(dist):
    max_exact = REL_BUCKETS // 2
    d = np.maximum(dist, 1).astype(np.float32)
    large = max_exact + np.log(d / max_exact) / math.log(REL_MAX_DIST / max_exact) * (REL_BUCKETS - max_exact)
    large = np.minimum(large.astype(np.int32), REL_BUCKETS - 1)
    return np.where(dist < max_exact, dist, large).astype(np.int32)


def _band(window, dilation):
    qi = np.arange(ATTN_BLOCK)[:, None]
    kj = np.arange(2 * ATTN_BLOCK)[None, :]
    delta = ATTN_BLOCK + qi - kj
    mask = (delta >= 0) & (delta <= window // dilation)
    bucket = _t5_bucket(np.maximum(delta, 0) * dilation)
    return mask, bucket


def _attn_blocks(s, r):
    m = s // r
    assert m % ATTN_BLOCK == 0, "sequence length must be a multiple of dilation * block"
    return m // ATTN_BLOCK


def _perm_load(ref, r):
    if r == 1:
        return ref[...]
    m = ref.shape[0] // r
    return jnp.concatenate([ref[pl.ds(c, m, stride=r), :] for c in range(r)], axis=0)


def _perm_store(ref, g, val, r, add=False):
    if r == 1:
        ref[g] = ref[g] + val if add else val
        return
    m = val.shape[0] // r
    for c in range(r):
        rows = pl.ds(c, m, stride=r)
        part = val[c * m:(c + 1) * m]
        ref[g, rows, :] = ref[g, rows, :] + part if add else part


def _blocks(x):
    return x.reshape(x.shape[0] // ATTN_BLOCK, ATTN_BLOCK, x.shape[1])


def _prev_blocks(x):
    return jnp.concatenate([x[:1], x[:-1]], axis=0)


def _next_blocks(x):
    return jnp.concatenate([x[1:], jnp.zeros_like(x[:1])], axis=0)


def _first_block_neg(s, r):
    nblk = s // ATTN_BLOCK
    idx = lax.broadcasted_iota(jnp.int32, (nblk, 1, 1), 0)
    return jnp.where(idx % _attn_blocks(s, r) == 0, NEG, 0.0)


def _bdot_nt(a, b):
    return lax.dot_general(a, b, (((2,), (2,)), ((0,), (0,))), preferred_element_type=F32)


def _bdot(a, b):
    return lax.dot_general(a, b, (((2,), (1,)), ((0,), (0,))), preferred_element_type=F32)


def _bdot_tn(a, b):
    return lax.dot_general(a, b, (((1,), (1,)), ((0,), (0,))), preferred_element_type=F32)


def attn_fwd(qkv, biasm, n_heads, name, ride=()):
    b, s, _ = qkv.shape
    h = n_heads
    scale = HEAD_DIM ** -0.5
    blk = ATTN_BLOCK

    def body(q1_ref, q2_ref, q3_ref, k_ref, v_ref, bias_ref, o_ref, lse_ref, acc, m_s, l_s):
        for g, q_ref in enumerate((q1_ref, q2_ref, q3_ref)):
            r = DILATED[g][1]
            first = _first_block_neg(s, r)
            q = _blocks(_perm_load(q_ref, r).astype(BF16))
            k = _blocks(_perm_load(k_ref, r).astype(BF16))
            v = _blocks(_perm_load(v_ref, r).astype(BF16))
            s_cur = _bdot_nt(q, k) * scale + bias_ref[g, :, blk:]
            s_prev = _bdot_nt(q, _prev_blocks(k)) * scale + bias_ref[g, :, :blk] + first
            m = jnp.max(jnp.maximum(s_cur, s_prev), axis=-1, keepdims=True)
            p_cur = jnp.exp(s_cur - m)
            p_prev = jnp.exp(s_prev - m)
            l = jnp.sum(p_cur + p_prev, axis=-1, keepdims=True)
            o = _bdot(p_cur.astype(BF16), v) + _bdot(p_prev.astype(BF16), _prev_blocks(v))
            _perm_store(acc, g, o.reshape(s, HEAD_DIM), r)
            _perm_store(m_s, g, m.reshape(s, 1), r)
            _perm_store(l_s, g, l.reshape(s, 1), r)
        m_all = jnp.maximum(jnp.maximum(m_s[0], m_s[1]), m_s[2])
        w = [jnp.exp(m_s[g] - m_all) for g in range(N_GROUPS)]
        l = w[0] * l_s[0] + w[1] * l_s[1] + w[2] * l_s[2]
        o_ref[...] = (w[0] * acc[0] + w[1] * acc[1] + w[2] * acc[2]) / l
        lse_ref[...] = m_all + jnp.log(l)

    def col(j):
        return pl.BlockSpec((None, s, HEAD_DIM), lambda bi, hi, j=j: (bi, 0, j * h + hi))

    r_ins, r_in_specs, r_outs, r_out_specs, r_sems = _ride_args(ride)
    outs = pl.pallas_call(
        _riding(body, 6, 2, 3, ride, 2), name=name, grid=(b, h),
        in_specs=[col(0), col(1), col(2), col(3), col(4),
                  pl.BlockSpec((N_GROUPS, None, blk, 2 * blk), lambda bi, hi: (0, hi, 0, 0))] + r_in_specs,
        out_specs=[pl.BlockSpec((None, s, HEAD_DIM), lambda bi, hi: (bi, 0, hi)),
                   pl.BlockSpec((None, None, s, 1), lambda bi, hi: (bi, hi, 0, 0))] + r_out_specs,
        out_shape=[jax.ShapeDtypeStruct((b, s, h * HEAD_DIM), F32), jax.ShapeDtypeStruct((b, h, s, 1), F32)] + r_outs,
        scratch_shapes=[pltpu.VMEM((N_GROUPS, s, HEAD_DIM), F32), pltpu.VMEM((N_GROUPS, s, 1), F32),
                        pltpu.VMEM((N_GROUPS, s, 1), F32)] + r_sems,
        compiler_params=_params(("arbitrary", "arbitrary")),
    )(qkv, qkv, qkv, qkv, qkv, biasm, *r_ins)
    return outs[0], outs[1], _ride_results(ride, outs[2:])


def attn_bwd(qkv, biasm, o, lse, do, n_heads, name, ride=()):
    b, s, _ = qkv.shape
    h = n_heads
    scale = HEAD_DIM ** -0.5
    blk = ATTN_BLOCK

    def body(q1_ref, q2_ref, q3_ref, k_ref, v_ref, bias_ref, o_ref, lse_ref, do_ref,
             dq1_ref, dq2_ref, dq3_ref, dk_ref, dv_ref, ds_ref, dq_acc, kv_acc, delta):
        delta[...] = jnp.sum(do_ref[...] * o_ref[...], axis=-1, keepdims=True)
        kv_acc[...] = jnp.zeros(kv_acc.shape, F32)
        for g, q_ref in enumerate((q1_ref, q2_ref, q3_ref)):
            r = DILATED[g][1]
            first = _first_block_neg(s, r)
            q = _blocks(_perm_load(q_ref, r).astype(BF16))
            k = _blocks(_perm_load(k_ref, r).astype(BF16))
            v = _blocks(_perm_load(v_ref, r).astype(BF16))
            dob = _blocks(_perm_load(do_ref, r).astype(BF16))
            lse_b = _blocks(_perm_load(lse_ref, r))
            dl_b = _blocks(_perm_load(delta, r))
            k_prev, v_prev = _prev_blocks(k), _prev_blocks(v)
            p_cur = jnp.exp(_bdot_nt(q, k) * scale + bias_ref[g, :, blk:] - lse_b)
            p_prev = jnp.exp(_bdot_nt(q, k_prev) * scale + bias_ref[g, :, :blk] + first - lse_b)
            ds_cur = p_cur * (_bdot_nt(dob, v) - dl_b)
            ds_prev = p_prev * (_bdot_nt(dob, v_prev) - dl_b)
            ds_ref[g, :, blk:] = jnp.sum(ds_cur, axis=0)
            ds_ref[g, :, :blk] = jnp.sum(ds_prev, axis=0)
            ds_cur_b, ds_prev_b = ds_cur.astype(BF16), ds_prev.astype(BF16)
            dq = (_bdot(ds_cur_b, k) + _bdot(ds_prev_b, k_prev)) * scale
            _perm_store(dq_acc, g, dq.reshape(s, HEAD_DIM), r)
            dk = (_bdot_tn(ds_cur_b, q) + _next_blocks(_bdot_tn(ds_prev_b, q))) * scale
            dv = _bdot_tn(p_cur.astype(BF16), dob) + _next_blocks(_bdot_tn(p_prev.astype(BF16), dob))
            _perm_store(kv_acc, 0, dk.reshape(s, HEAD_DIM), r, add=True)
            _perm_store(kv_acc, 1, dv.reshape(s, HEAD_DIM), r, add=True)
        for g, out_ref in enumerate((dq1_ref, dq2_ref, dq3_ref)):
            out_ref[...] = dq_acc[g].astype(out_ref.dtype)
        dk_ref[...] = kv_acc[0].astype(dk_ref.dtype)
        dv_ref[...] = kv_acc[1].astype(dv_ref.dtype)

    def col(j):
        return pl.BlockSpec((None, s, HEAD_DIM), lambda bi, hi, j=j: (bi, 0, j * h + hi))

    head = pl.BlockSpec((None, s, HEAD_DIM), lambda bi, hi: (bi, 0, hi))
    sd = jax.ShapeDtypeStruct
    r_ins, r_in_specs, r_outs, r_out_specs, r_sems = _ride_args(ride)
    outs = pl.pallas_call(
        _riding(body, 9, 6, 3, ride, 2), name=name, grid=(b, h),
        in_specs=[col(0), col(1), col(2), col(3), col(4),
                  pl.BlockSpec((N_GROUPS, None, blk, 2 * blk), lambda bi, hi: (0, hi, 0, 0)),
                  head, pl.BlockSpec((None, None, s, 1), lambda bi, hi: (bi, hi, 0, 0)), head] + r_in_specs,
        out_specs=[head] * 5 + [pl.BlockSpec((None, None, N_GROUPS, blk, 2 * blk), lambda bi, hi: (bi, hi, 0, 0, 0))]
        + r_out_specs,
        out_shape=[sd((b, s, h * HEAD_DIM), BF16)] * 5 + [sd((b, h, N_GROUPS, blk, 2 * blk), F32)] + r_outs,
        scratch_shapes=[pltpu.VMEM((N_GROUPS, s, HEAD_DIM), F32), pltpu.VMEM((2, s, HEAD_DIM), F32),
                        pltpu.VMEM((s, 1), F32)] + r_sems,
        compiler_params=_params(("arbitrary", "arbitrary")),
    )(qkv, qkv, qkv, qkv, qkv, biasm, o, lse, do, *r_ins)
    return outs[:6], _ride_results(ride, outs[6:])


def bias_table(rel_rows, bucket_f, n_heads, name):
    g, blk, blk2 = bucket_f.shape
    h = n_heads

    def body(rb_ref, bk_ref, o_ref):
        bk = bk_ref[...]
        rb = rb_ref[...]
        acc = jnp.full((blk, blk2), NEG, F32)
        for bucket in range(REL_BUCKETS):
            acc = jnp.where(bk == float(bucket), rb[:, bucket:bucket + 1], acc)
        o_ref[...] = acc

    return pl.pallas_call(
        body, name=name, grid=(g, h),
        in_specs=[pl.BlockSpec((None, 1, 128), lambda gi, hi: (gi * h + hi, 0, 0)),
                  pl.BlockSpec((None, blk, blk2), lambda gi, hi: (gi, 0, 0))],
        out_specs=pl.BlockSpec((None, None, blk, blk2), lambda gi, hi: (gi, hi, 0, 0)),
        out_shape=jax.ShapeDtypeStruct((g, h, blk, blk2), F32),
        compiler_params=_params(("parallel", "parallel")),
    )(rel_rows, bucket_f)


def bias_grad(ds_sum, bucket_f, name):
    b, h, g, blk, blk2 = ds_sum.shape

    def body(ds_ref, bk_ref, o_ref):
        tot = jnp.sum(ds_ref[...], axis=0)
        bk = bk_ref[...]
        lane = lax.broadcasted_iota(jnp.int32, (1, 128), 1)
        vec = jnp.zeros((1, 128), F32)
        for bucket in range(REL_BUCKETS):
            val = jnp.sum(jnp.where(bk == float(bucket), tot, 0.0), keepdims=True)
            vec = vec + jnp.where(lane == bucket, val, 0.0)
        o_ref[...] = vec

    return pl.pallas_call(
        body, name=name, grid=(g, h),
        in_specs=[pl.BlockSpec((b, None, None, blk, blk2), lambda gi, hi: (0, hi, gi, 0, 0)),
                  pl.BlockSpec((None, blk, blk2), lambda gi, hi: (gi, 0, 0))],
        out_specs=pl.BlockSpec((None, 1, 128), lambda gi, hi: (gi * h + hi, 0, 0)),
        out_shape=jax.ShapeDtypeStruct((g * h, 1, 128), F32),
        compiler_params=_params(("parallel", "parallel")),
    )(ds_sum, bucket_f)


def _chip_peers():
    x, y, c = lax.axis_index("x"), lax.axis_index("y"), lax.axis_index("c")
    me = 2 * x + y
    peers = [(1 - x, y, c), (x, 1 - y, c), (1 - x, 1 - y, c)]
    peer_chip = [2 * (1 - x) + y, 2 * x + (1 - y), 2 * (1 - x) + (1 - y)]
    return me, peers, peer_chip


def _any_specs(n):
    return [pl.BlockSpec(memory_space=pl.ANY)] * n


_MID_NUM, _MID_DEN = 3, 4


class _Exchange:
    def start(self, ins, outs, sems):
        local, sends, _ = self._copies(ins, outs, sems)
        for cp in local + sends:
            cp.start()

    def mid(self, ins, outs, sems):
        pass

    def wait(self, ins, outs, sems):
        local, sends, recvs = self._copies(ins, outs, sems)
        for cp in recvs():
            cp.wait_recv()
        for cp in sends:
            cp.wait_send()
        for cp in local:
            cp.wait()


class _Gather(_Exchange):
    HALF_ROWS = 16

    def __init__(self, arrays):
        n = len(arrays)
        self.ins = list(arrays)
        self.split = [a.shape[0] % (2 * self.HALF_ROWS) == 0 for a in arrays]
        self.out_shape = [jax.ShapeDtypeStruct((N_CHIPS,) + a.shape, a.dtype) for a in arrays]
        dma = pltpu.SemaphoreType.DMA
        self.sems = [dma((3 * n,)), dma((3 * n,)), dma((n,)), dma((3 * n,)), dma((3 * n,))]

    def _half(self, i, ref, sibling=False):
        if not self.split[i]:
            return ref
        half = self.ins[i].shape[0] // 2
        c = lax.axis_index("c")
        c = 1 - c if sibling else c
        return ref.at[pl.ds(pl.multiple_of(c * half, self.HALF_ROWS), half)]

    def _plan(self, ins, outs, sems):
        send1, recv1, local_sems, send2, recv2 = sems
        me, peers, peer_chip = _chip_peers()
        x, y, c = lax.axis_index("x"), lax.axis_index("y"), lax.axis_index("c")
        n = len(ins)
        pairs = [(i, k) for i in range(n) for k in range(3)]

        def fetch(i, k, slot):
            return pltpu.make_async_remote_copy(src_ref=self._half(i, ins[i]), dst_ref=self._half(i, outs[i].at[slot]),
                                                send_sem=send1.at[3 * i + k], recv_sem=recv1.at[3 * i + k],
                                                device_id=peers[k], device_id_type=MESH)

        def share(i, k, sibling):
            part = self._half(i, outs[i].at[peer_chip[k]], sibling)
            return pltpu.make_async_remote_copy(src_ref=part, dst_ref=part, send_sem=send2.at[3 * i + k],
                                                recv_sem=recv2.at[3 * i + k], device_id=(x, y, 1 - c),
                                                device_id_type=MESH)

        split_pairs = [(i, k) for i, k in pairs if self.split[i]]
        return dict(
            local=lambda: [pltpu.make_async_copy(ins[i], outs[i].at[me], local_sems.at[i]) for i in range(n)],
            fetch_out=lambda: [fetch(i, k, me) for i, k in pairs],
            fetch_in=lambda: [fetch(i, k, peer_chip[k]) for i, k in pairs],
            share_out=lambda: [share(i, k, False) for i, k in split_pairs],
            share_in=lambda: [share(i, k, True) for i, k in split_pairs])

    def start(self, ins, outs, sems):
        plan = self._plan(ins, outs, sems)
        for cp in plan["local"]() + plan["fetch_out"]():
            cp.start()

    def mid(self, ins, outs, sems):
        plan = self._plan(ins, outs, sems)
        for cp in plan["fetch_in"]():
            cp.wait_recv()
        for cp in plan["share_out"]():
            cp.start()

    def wait(self, ins, outs, sems):
        plan = self._plan(ins, outs, sems)
        for cp in plan["share_in"]():
            cp.wait_recv()
        for cp in plan["fetch_out"]() + plan["share_out"]():
            cp.wait_send()
        for cp in plan["local"]():
            cp.wait()


class _Scatter(_Exchange):
    def __init__(self, slabs, whole=()):
        self.n_slabs = len(slabs)
        self.ins = list(slabs) + list(whole)
        n = len(self.ins)
        self.out_shape = [jax.ShapeDtypeStruct(a.shape, a.dtype) for a in slabs] \
            + [jax.ShapeDtypeStruct((N_CHIPS,) + a.shape, a.dtype) for a in whole]
        self.sems = [pltpu.SemaphoreType.DMA((3 * n,)), pltpu.SemaphoreType.DMA((3 * n,)), pltpu.SemaphoreType.DMA((n,))]

    def _copies(self, ins, outs, sems):
        send_sems, recv_sems, local_sems = sems
        me, peers, peer_chip = _chip_peers()
        n = len(ins)

        def src(i, chip):
            return ins[i].at[chip] if i < self.n_slabs else ins[i]

        def remote(i, k, src_chip, slot):
            return pltpu.make_async_remote_copy(src_ref=src(i, src_chip), dst_ref=outs[i].at[slot],
                                                send_sem=send_sems.at[3 * i + k], recv_sem=recv_sems.at[3 * i + k],
                                                device_id=peers[k], device_id_type=MESH)

        local = [pltpu.make_async_copy(src(i, me), outs[i].at[me], local_sems.at[i]) for i in range(n)]
        sends = [remote(i, k, peer_chip[k], me) for i in range(n) for k in range(3)]
        return local, sends, lambda: [remote(i, k, me, peer_chip[k]) for i in range(n) for k in range(3)]


class _Swap(_Exchange):
    def __init__(self, arrays):
        n = len(arrays)
        self.ins = list(arrays)
        self.out_shape = [jax.ShapeDtypeStruct(a.shape, a.dtype) for a in arrays]
        self.sems = [pltpu.SemaphoreType.DMA((n,)), pltpu.SemaphoreType.DMA((n,))]

    def _copies(self, ins, outs, sems):
        send_sems, recv_sems = sems
        x, y, c = lax.axis_index("x"), lax.axis_index("y"), lax.axis_index("c")
        cps = [pltpu.make_async_remote_copy(src_ref=ins[i], dst_ref=outs[i], send_sem=send_sems.at[i],
                                            recv_sem=recv_sems.at[i], device_id=(x, y, 1 - c), device_id_type=MESH)
               for i in range(len(ins))]
        return [], cps, lambda: cps


def _riding(body, n_in, n_out, n_scratch, ride, rank):
    if not ride:
        return body
    r_in = sum(len(e.ins) for e in ride)
    r_out = sum(len(e.out_shape) for e in ride)

    def split(refs, sizes):
        out, a = [], 0
        for sz in sizes:
            out.append(refs[a:a + sz])
            a += sz
        return out

    def wrapped(*refs):
        a = 0
        parts = []
        for sz in (n_in, r_in, n_out, r_out, n_scratch):
            parts.append(refs[a:a + sz])
            a += sz
        own_in, ex_in, own_out, ex_out, own_scratch = parts
        ex_sems = refs[a:]
        ins = split(ex_in, [len(e.ins) for e in ride])
        outs = split(ex_out, [len(e.out_shape) for e in ride])
        sems = split(ex_sems, [len(e.sems) for e in ride])
        if rank:
            step, total = 0, 1
            for d in range(rank):
                step = step * pl.num_programs(d) + pl.program_id(d)
                total = total * pl.num_programs(d)

            @pl.when(step == 0)
            def _():
                for e, i, o, s in zip(ride, ins, outs, sems):
                    e.start(i, o, s)

            body(*own_in, *own_out, *own_scratch)

            @pl.when(step == (total * _MID_NUM) // _MID_DEN)
            def _():
                for e, i, o, s in zip(ride, ins, outs, sems):
                    e.mid(i, o, s)

            @pl.when(step == total - 1)
            def _():
                for e, i, o, s in zip(ride, ins, outs, sems):
                    e.wait(i, o, s)
        else:
            for phase in ("start", "mid", "wait"):
                for e, i, o, s in zip(ride, ins, outs, sems):
                    getattr(e, phase)(i, o, s)

    return wrapped


def _ride_args(ride):
    ins = [a for e in ride for a in e.ins]
    outs = [s for e in ride for s in e.out_shape]
    sems = [s for e in ride for s in e.sems]
    return ins, _any_specs(len(ins)), outs, _any_specs(len(outs)), sems


def _ride_results(ride, flat):
    out, a = [], 0
    for e in ride:
        out.append(list(flat[a:a + len(e.out_shape)]))
        a += len(e.out_shape)
    return out


def exchange(ride, name):
    ins, in_specs, outs, out_specs, sems = _ride_args(ride)
    res = pl.pallas_call(
        _riding(lambda: None, 0, 0, 0, ride, 0), name=name,
        in_specs=in_specs, out_specs=out_specs, out_shape=outs, scratch_shapes=sems,
    )(*ins)
    return _ride_results(ride, res)


def _sum_slots(ref):
    acc = ref[0].astype(F32)
    for j in range(1, ref.shape[0]):
        acc = acc + ref[j].astype(F32)
    return acc


def sum_pairs(mine, other, name, tr=176):
    n, r, w = mine.shape
    tr = r if r <= tr else _tile(r, tr)

    def body(a_ref, b_ref, o_ref):
        o_ref[...] = _sum_slots(a_ref) + _sum_slots(b_ref)

    spec = pl.BlockSpec((n, tr, w), lambda i: (0, i, 0))
    return pl.pallas_call(
        body, name=name, grid=(r // tr,),
        in_specs=[spec, spec], out_specs=_rows(tr, w),
        out_shape=jax.ShapeDtypeStruct((r, w), F32),
        compiler_params=_params(("parallel",)),
    )(mine, other)


def adamw(w, m, v, gs, name, tr=256):
    r, c = w.shape
    tr = r if r % 8 else _tile(r, tr)
    c1 = 1.0 - ADAM_B1 ** ADAM_STEP
    c2 = 1.0 - ADAM_B2 ** ADAM_STEP
    ng = len(gs)

    def body(w_ref, m_ref, v_ref, *refs):
        g_refs, (g_ref, d_ref, nm_ref, nv_ref) = refs[:ng], refs[ng:]
        g = g_refs[0][...] if ng == 1 else _sum_slots(g_refs[0]) + _sum_slots(g_refs[1])
        nm = ADAM_B1 * m_ref[...] + (1.0 - ADAM_B1) * g
        nv = ADAM_B2 * v_ref[...] + (1.0 - ADAM_B2) * (g * g)
        g_ref[...] = g
        nm_ref[...] = nm
        nv_ref[...] = nv
        d_ref[...] = (-ADAM_LR) * ((nm / c1) / (jnp.sqrt(nv / c2) + ADAM_EPS) + ADAM_WD * w_ref[...])

    spec = _rows(tr, c)
    gspec = spec if ng == 1 else pl.BlockSpec((N_CHIPS, tr, c), lambda i: (0, i, 0))
    return pl.pallas_call(
        body, name=name, grid=(r // tr,),
        in_specs=[spec] * 3 + [gspec] * ng, out_specs=[spec] * 4,
        out_shape=[jax.ShapeDtypeStruct((r, c), F32)] * 4,
        compiler_params=_params(("parallel",)),
    )(w, m, v, *gs)


_PARAMS = (
    ("rel_bias", None), ("norm_mix_pre", None), ("norm_mix_post", None), ("w_in", 1), ("conv_rnn_w", 1),
    ("conv_rnn_b", None), ("w_rg_a", None), ("b_rg_a", None), ("w_rg_x", None), ("b_rg_x", None),
    ("lru_lambda", None), ("w_branch_rnn", 0), ("w_branch_att", 1), ("w_out", 0), ("norm_ffn_pre", None),
    ("norm_ffn_post", None), ("w_ffn_gate", 1), ("w_ffn_up", 1), ("conv_ffn_w", 1), ("conv_ffn_b", None),
    ("w_ffn_down", 0),
)
_SMALL = 65536


def _as2d(a):
    a = a[0] if a.shape[0] == 1 and a.ndim >= 3 else a
    return a.reshape(-1, a.shape[-1]) if a.ndim == 3 else a


def _pack(pieces, dtype):
    flat = jnp.concatenate([p.astype(dtype).reshape(-1) for p in pieces])
    unit = PACK_W * PACK_ROWS
    pad = (-flat.shape[0]) % unit
    flat = jnp.pad(flat, (0, pad))
    return flat.reshape(-1, PACK_W)


def _unpack(buf, shapes):
    flat = buf.reshape(-1)
    out, off = [], 0
    for shp in shapes:
        n = int(np.prod(shp))
        out.append(flat[off:off + n].reshape(shp))
        off += n
    return out


def _join(slots, ax):
    if ax == 0:
        return slots.reshape(-1, slots.shape[-1])
    return jnp.transpose(slots, (1, 0, 2)).reshape(slots.shape[1], -1)


def _cut(full, ax):
    if ax == 0:
        return full.reshape(N_CHIPS, -1, full.shape[-1])
    return jnp.transpose(full.reshape(full.shape[0], N_CHIPS, -1), (1, 0, 2))


def kernel(x, rel_bias, norm_mix_pre, norm_mix_post, w_in, conv_rnn_w, conv_rnn_b, w_rg_a, b_rg_a, w_rg_x, b_rg_x, lru_lambda, w_branch_rnn, w_branch_att, w_out, norm_ffn_pre, norm_ffn_post, w_ffn_gate, w_ffn_up, conv_ffn_w, conv_ffn_b, w_ffn_down, loss_target, m_rel_bias, m_norm_mix_pre, m_norm_mix_post, m_w_in, m_conv_rnn_w, m_conv_rnn_b, m_w_rg_a, m_b_rg_a, m_w_rg_x, m_b_rg_x, m_lru_lambda, m_w_branch_rnn, m_w_branch_att, m_w_out, m_norm_ffn_pre, m_norm_ffn_post, m_w_ffn_gate, m_w_ffn_up, m_conv_ffn_w, m_conv_ffn_b, m_w_ffn_down, v_rel_bias, v_norm_mix_pre, v_norm_mix_post, v_w_in, v_conv_rnn_w, v_conv_rnn_b, v_w_rg_a, v_b_rg_a, v_w_rg_x, v_b_rg_x, v_lru_lambda, v_w_branch_rnn, v_w_branch_att, v_w_out, v_norm_ffn_pre, v_norm_ffn_post, v_w_ffn_gate, v_w_ffn_up, v_conv_ffn_w, v_conv_ffn_b, v_w_ffn_down):
    args = dict(locals())
    names = [n for n, _ in _PARAMS]
    axis = dict(_PARAMS)
    w_loc = {n: args[n] for n in names}
    m_loc = {n: args["m_" + n] for n in names}
    v_loc = {n: args["v_" + n] for n in names}
    sharded = [n for n in names if axis[n] is not None]
    replicated = [n for n in names if axis[n] is None]

    big = [n for n in sharded if w_loc[n].size >= _SMALL]
    small_sharded = [n for n in sharded if n not in big]
    small = replicated + small_sharded

    first = ["w_in"] + small_sharded
    srcs = [_as2d(w_loc[n]).astype(BF16) if n in big else _as2d(w_loc[n]) for n in first]
    (gathered,) = exchange([_Gather(srcs)], "gather_first")
    p = {n: _join(a, axis[n]) for n, a in zip(first, gathered)}
    for n in replicated:
        p[n] = _as2d(w_loc[n])
    shards = {n: _as2d(w_loc[n]).astype(BF16) for n in big if n not in first}

    last = "norm_mix_pre"
    early = [n for n in small if n != last]
    received, sibling, g_small, loss_part = _local_step(x, loss_target, p, shards, early)

    ((received["last"],),) = exchange([_Scatter([], [_pack([g_small[last]], BF16)])], "scatter_last")
    late = [n for n in received if n not in sibling]
    (swapped,) = exchange([_Swap([received[n] for n in late])], "swap_last")
    sibling.update(zip(late, swapped))
    early_sum = sum_pairs(received["small"], sibling["small"], "sum_small")
    last_sum = sum_pairs(received["last"], sibling["last"], "sum_last")
    g_tot = dict(zip(early, _unpack(early_sum, [g_small[n].shape for n in early])))
    (g_tot[last],) = _unpack(last_sum, [g_small[last].shape])
    chip = 2 * lax.axis_index("x") + lax.axis_index("y")
    for n in small_sharded:
        size = g_tot[n].shape[axis[n]] // N_CHIPS
        g_tot[n] = lax.dynamic_slice_in_dim(g_tot[n], chip * size, size, axis=axis[n])

    out_g, out_d, out_m, out_v = {}, {}, {}, {}
    for i, n in enumerate(names):
        shp = w_loc[n].shape
        gs = (received[n], sibling[n]) if n in big else (g_tot[n],)
        g, d, nm, nv = adamw(_as2d(w_loc[n]), _as2d(m_loc[n]), _as2d(v_loc[n]), gs, "adamw_" + n)
        out_g[n], out_d[n], out_m[n], out_v[n] = (t.reshape(shp) for t in (g, d, nm, nv))

    d_model = x.shape[-1]
    loss = lax.psum(0.5 * jnp.sum(loss_part) / d_model, ("x", "y", "c"))
    grad_x = g_small["x"]
    return (loss, grad_x, *[out_g[n] for n in names], *[out_d[n] for n in names],
            *[out_m[n] for n in names], *[out_v[n] for n in names])


def _local_step(x, target, p, shards, small_early):
    axis = dict(_PARAMS)
    b, s, d = x.shape
    t = b * s
    rnn = p["b_rg_a"].shape[1]
    ffn = p["conv_ffn_b"].shape[1]
    nbk = rnn // p["w_rg_a"].shape[1]
    hkv = (p["w_in"].shape[1] - rnn - 2 * d) // (N_GROUPS + 2)
    h = hkv // HEAD_DIM
    nq = N_GROUPS * hkv

    x2 = x.reshape(t, d)
    tgt = target.reshape(t, d)
    w_in = p["w_in"]
    in_splits = (rnn, nq + 2 * hkv, 2 * d)
    wa = p["w_rg_a"].reshape(nbk, -1, p["w_rg_a"].shape[1]).astype(BF16)
    wx = p["w_rg_x"].reshape(nbk, -1, p["w_rg_x"].shape[1]).astype(BF16)
    cw_r, cb_r = p["conv_rnn_w"], p["conv_rnn_b"]
    cw_f, cb_f = p["conv_ffn_w"], p["conv_ffn_b"]

    masks, buckets = zip(*[_band(w_, r_) for w_, r_ in DILATED])
    bucket_f = jnp.asarray(np.where(np.stack(masks), np.stack(buckets), -1).astype(np.float32))
    rel_rows = jnp.pad(p["rel_bias"].T, ((0, 0), (0, 128 - REL_BUCKETS)))[:, None, :]
    biasm = bias_table(rel_rows, bucket_f, h, "bias_table")

    early = ["w_branch_rnn", "w_branch_att", "w_out"]
    hn1, (xr, qkv, gts), (got,) = norm_mm(x2, p["norm_mix_pre"], [w_in], [in_splits], "in_proj",
                                          ride=[_Gather([shards[n] for n in early])])
    p.update({n: _join(a, axis[n]) for n, a in zip(early, got)})
    xr3 = xr.reshape(b, s, rnn)
    (y_rnn, a_rnn, xc_rnn), (got,) = rglru_fwd(xr3, cw_r, cb_r, wa, p["b_rg_a"], wx, p["b_rg_x"], p["lru_lambda"], "rglru_fwd",
                              ride=[_Gather([shards[n] for n in ("w_ffn_gate", "w_ffn_up")])])
    p.update({n: _join(a, axis[n]) for n, a in zip(("w_ffn_gate", "w_ffn_up"), got)})
    qkv3 = qkv.reshape(b, s, -1)
    o_att, lse, ((got,),) = attn_fwd(qkv3, biasm, h, "attn_fwd", ride=[_Gather([shards["w_ffn_down"]])])
    p["w_ffn_down"] = _join(got, axis["w_ffn_down"])
    merged, br, ba, mix, h1 = merge_out(y_rnn.reshape(t, rnn), o_att.reshape(t, hkv), gts, p["w_branch_rnn"],
                                        p["w_branch_att"], p["w_out"], p["norm_mix_post"], x2, "merge_out")
    hn2, gate_pre, up, act = ffn_in_act(h1, p["norm_ffn_pre"], p["w_ffn_gate"], p["w_ffn_up"], cw_f, cb_f, s, "ffn_in")

    g, gb = {}, {}
    recv, sib = {}, {}

    def rows4(a):
        return a.reshape(N_CHIPS, -1, a.shape[-1])

    dy, dff, dact, g["norm_ffn_post"], loss_part = ffn_down_loss(act, p["w_ffn_down"], p["norm_ffn_post"], h1, tgt,
                                                                  "ffn_down")
    gb["w_ffn_down"] = rows4(mm_tn(act, [dff], "ffn_down_dw"))
    (dgp, dup, dhn2, g["conv_ffn_w"], g["conv_ffn_b"]), ((recv["w_ffn_down"],),) = ffn_in_bwd(
        dact, gate_pre, up, cw_f, cb_f, p["w_ffn_gate"], p["w_ffn_up"], s, "ffn_in_bwd",
        ride=[_Scatter([gb["w_ffn_down"]])])
    gb["w_ffn_gate"] = mm_tn(hn2, [dgp], "ffn_gate_dw", col_shards=N_CHIPS)
    gb["w_ffn_up"] = mm_tn(hn2, [dup], "ffn_up_dw", col_shards=N_CHIPS)
    (dh1, dgts, dy_rnn, do_att, g["norm_ffn_pre"], g["norm_mix_post"], dw_out, dw_br,
     gb["w_branch_att"]) = mid_bwd(dhn2, h1, p["norm_ffn_pre"], dy, mix, p["norm_mix_post"], p["w_out"], gts, br, ba,
                                   p["w_branch_rnn"], p["w_branch_att"], merged, y_rnn.reshape(t, rnn),
                                   o_att.reshape(t, hkv), "mid_bwd")
    gb["w_out"], gb["w_branch_rnn"] = rows4(dw_out), rows4(dw_br)
    ffn_in = ["w_ffn_gate", "w_ffn_up"]
    (dxr, g["conv_rnn_w"], g["conv_rnn_b"], dwa, g["b_rg_a"], dwx, g["b_rg_x"], g["lru_lambda"]), (got,) = rglru_bwd(
        xr3, y_rnn, dy_rnn.reshape(b, s, rnn), a_rnn, xc_rnn, cw_r, wa, p["b_rg_a"], wx, p["b_rg_x"], p["lru_lambda"], "rglru_bwd",
        ride=[_Scatter([gb[n] for n in ffn_in])])
    recv.update(zip(ffn_in, got))
    g["w_rg_a"] = dwa.reshape(p["w_rg_a"].shape)
    g["w_rg_x"] = dwx.reshape(p["w_rg_x"].shape)
    mid = ["w_out", "w_branch_rnn", "w_branch_att"]
    early_recv = ["w_ffn_down"] + ffn_in
    (dq1, dq2, dq3, dk, dv, ds_sum), (got, swapped) = attn_bwd(
        qkv3, biasm, o_att, lse, do_att.reshape(b, s, hkv), h, "attn_bwd",
        ride=[_Scatter([gb[n] for n in mid]), _Swap([recv[n] for n in early_recv])])
    recv.update(zip(mid, got))
    sib.update(zip(early_recv, swapped))
    rows = bias_grad(ds_sum, bucket_f, "bias_grad")
    g["rel_bias"] = rows[:, 0, :REL_BUCKETS].T
    dproj = [dxr.reshape(t, rnn)] + [a.reshape(t, hkv) for a in (dq1, dq2, dq3, dk, dv)] + [dgts]
    dw_a, (got,) = mm_tn(hn1, dproj[:4], "in_proj_dw_a", ride=[_Swap([recv[n] for n in mid])])
    sib.update(zip(mid, got))
    pack = _pack([g[n] for n in small_early], BF16)
    dw_b, ((recv["small"],),) = mm_tn(hn1, dproj[4:], "in_proj_dw_b", ride=[_Scatter([], [pack])])
    gb["w_in"] = _cut(jnp.concatenate([dw_a[0], dw_b[0]], axis=1), 1)
    dx, g["norm_mix_pre"], ((recv["w_in"],),) = mm_nt(
        [(dproj, w_in)], "in_proj_dx", norm=(x2, p["norm_mix_pre"], dh1), ride=[_Scatter([gb["w_in"]])])
    g["x"] = dx.reshape(b, s, d)
    return recv, sib, g, loss_part
```

```python
import math

import numpy as np
import jax
import jax.numpy as jnp
from jax import lax
from jax.experimental import pallas as pl
from jax.experimental.pallas import tpu as pltpu

F32 = jnp.float32
BF16 = jnp.bfloat16

EPS = 1e-6
HEAD_DIM = 128
ATTN_BLOCK = 128
DILATED = ((128, 1), (512, 4), (2048, 16))
N_GROUPS = len(DILATED)
REL_BUCKETS = 32
REL_MAX_DIST = 2048
LRU_C = 8.0
NEG = -1e30

ADAM_LR = 0.001
ADAM_B1 = 0.9
ADAM_B2 = 0.999
ADAM_EPS = 1e-08
ADAM_WD = 0.01
ADAM_STEP = 10

N_CHIPS = 4
PACK_W = 1024
PACK_ROWS = 16
VMEM_LIMIT = 56 * 1024 * 1024
MESH = pl.DeviceIdType.MESH


def _params(sem=None):
    return pltpu.CompilerParams(dimension_semantics=sem, vmem_limit_bytes=VMEM_LIMIT)


def _dot(a, b):
    return jnp.dot(a, b, preferred_element_type=F32)


def _dot_nt(a, b):
    return lax.dot_general(a, b, (((1,), (1,)), ((), ())), preferred_element_type=F32)


def _dot_tn(a, b):
    return lax.dot_general(a, b, (((0,), (0,)), ((), ())), preferred_element_type=F32)


def _sig(x):
    return 0.5 * jnp.tanh(0.5 * x) + 0.5


def _rows(tm, w):
    return pl.BlockSpec((tm, w), lambda i: (i, 0))


def _whole(shape):
    nd = len(shape)
    return pl.BlockSpec(tuple(shape), lambda *_: (0,) * nd)


def _resident(shape):
    nd = len(shape)
    return pl.BlockSpec(tuple(shape), lambda *_: (0,) * nd, pipeline_mode=pl.Buffered(1))


def _tile(t, want):
    while t % want:
        want //= 2
    return want


def norm_mm(x, g, ws, splits, name, ride=(), tm=512):
    t, d = x.shape
    tm = _tile(t, tm)
    nw = len(ws)
    widths = [n for sp in splits for n in sp]

    def body(x_ref, g_ref, *refs):
        w_refs, hn_ref, o_refs = refs[:nw], refs[nw], refs[nw + 1:]
        xv = x_ref[...]
        inv = lax.rsqrt(jnp.mean(xv * xv, axis=-1, keepdims=True) + EPS)
        hn = (xv * inv * g_ref[...]).astype(BF16)
        hn_ref[...] = hn
        o = 0
        for w_ref, sp in zip(w_refs, splits):
            off = 0
            for n in sp:
                o_refs[o][...] = _dot(hn, w_ref[:, off:off + n])
                off += n
                o += 1

    r_ins, r_in_specs, r_outs, r_out_specs, r_sems = _ride_args(ride)
    n_out = 1 + len(widths)
    outs = pl.pallas_call(
        _riding(body, 2 + nw, n_out, 0, ride, 1), name=name, grid=(t // tm,),
        in_specs=[_rows(tm, d), _whole(g.shape)] + [_resident(w.shape) for w in ws] + r_in_specs,
        out_specs=[_rows(tm, d)] + [_rows(tm, n) for n in widths] + r_out_specs,
        out_shape=[jax.ShapeDtypeStruct((t, d), BF16)] + [jax.ShapeDtypeStruct((t, n), F32) for n in widths] + r_outs,
        scratch_shapes=r_sems,
        compiler_params=_params(("arbitrary",)),
    )(x, g, *ws, *r_ins)
    return outs[0], outs[1:n_out], _ride_results(ride, outs[n_out:])


def mm_nt(groups, name, ride=(), norm=None, tm=512):
    dys_all = [dy for dys, _ in groups for dy in dys]
    ws = [w for _, w in groups]
    t = dys_all[0].shape[0]
    k = ws[0].shape[0]
    tm = _tile(t, tm)
    n = len(dys_all)
    extra = list(norm) if norm else []

    def body(*refs):
        dy_refs, w_refs = refs[:n], refs[n:n + len(ws)]
        rest = refs[n + len(ws):]
        acc = None
        i = 0
        for (dys, _), w_ref in zip(groups, w_refs):
            off = 0
            for dy in dys:
                width = dy.shape[1]
                part = _dot_nt(dy_refs[i][...].astype(BF16), w_ref[:, off:off + width])
                acc = part if acc is None else acc + part
                off += width
                i += 1
        if norm:
            u_ref, g_ref, add_ref, o_ref, dg_ref = rest

            @pl.when(pl.program_id(0) == 0)
            def _():
                dg_ref[...] = jnp.zeros(dg_ref.shape, F32)

            du, dg_rows = _rms_bwd(acc, u_ref[...], g_ref[...])
            o_ref[...] = du + add_ref[...]
            dg_ref[...] += jnp.sum(dg_rows, axis=0, keepdims=True)
        else:
            rest[0][...] = acc

    n_out = 2 if norm else 1
    r_ins, r_in_specs, r_outs, r_out_specs, r_sems = _ride_args(ride)
    outs = pl.pallas_call(
        _riding(body, n + len(ws) + len(extra), n_out, 0, ride, 1), name=name, grid=(t // tm,),
        in_specs=[_rows(tm, dy.shape[1]) for dy in dys_all] + [_resident(w.shape) for w in ws]
        + ([_rows(tm, k), _whole((1, k)), _rows(tm, k)] if norm else []) + r_in_specs,
        out_specs=[_rows(tm, k)] + ([_whole((1, k))] if norm else []) + r_out_specs,
        out_shape=[jax.ShapeDtypeStruct((t, k), F32)] + ([jax.ShapeDtypeStruct((1, k), F32)] if norm else []) + r_outs,
        scratch_shapes=r_sems,
        compiler_params=_params(("arbitrary",)),
    )(*dys_all, *ws, *extra, *r_ins)
    return tuple(outs[:n_out]) + (_ride_results(ride, outs[n_out:]),)


def mm_tn(a, dys, name, col_shards=1, ride=(), tm=1024):
    t, k = a.shape
    tm = _tile(t, tm)
    n = len(dys)
    ntot = sum(dy.shape[1] for dy in dys)
    wsh = ntot // col_shards

    def body(a_ref, *refs):
        dy_refs, o_ref, acc = refs[:n], refs[n], refs[n + 1]

        @pl.when(pl.program_id(0) == 0)
        def _():
            acc[...] = jnp.zeros(acc.shape, F32)

        av = a_ref[...].astype(BF16)
        off = 0
        for dy_ref in dy_refs:
            width = dy_ref.shape[1]
            acc[:, off:off + width] += _dot_tn(av, dy_ref[...].astype(BF16))
            off += width

        @pl.when(pl.program_id(0) == pl.num_programs(0) - 1)
        def _():
            for j in range(col_shards):
                o_ref[j] = acc[:, j * wsh:(j + 1) * wsh].astype(o_ref.dtype)

    r_ins, r_in_specs, r_outs, r_out_specs, r_sems = _ride_args(ride)
    outs = pl.pallas_call(
        _riding(body, 1 + n, 1, 1, ride, 1), name=name, grid=(t // tm,),
        in_specs=[_rows(tm, k)] + [_rows(tm, dy.shape[1]) for dy in dys] + r_in_specs,
        out_specs=[_whole((col_shards, k, wsh))] + r_out_specs,
        out_shape=[jax.ShapeDtypeStruct((col_shards, k, wsh), BF16)] + r_outs,
        scratch_shapes=[pltpu.VMEM((k, ntot), F32)] + r_sems,
        compiler_params=_params(("arbitrary",)),
    )(a, *dys, *r_ins)
    return (outs[0], _ride_results(ride, outs[1:])) if ride else outs[0]


def _rms_bwd(dz, u, g):
    d = u.shape[-1]
    inv = lax.rsqrt(jnp.mean(u * u, axis=-1, keepdims=True) + EPS)
    dzg = dz * g
    proj = jnp.sum(dzg * u, axis=-1, keepdims=True) * (1.0 / d)
    du = inv * (dzg - u * (inv * inv) * proj)
    dg_rows = dz * u * inv
    return du, dg_rows


def ffn_down_loss(act, wd, g, h1, target, name, tm=512):
    t, f = act.shape
    d = wd.shape[1]
    tm = _tile(t, tm)

    def body(a_ref, w_ref, g_ref, h_ref, t_ref, dy_ref, dff_ref, dact_ref, dg_ref, loss_ref):
        @pl.when(pl.program_id(0) == 0)
        def _():
            dg_ref[...] = jnp.zeros(dg_ref.shape, F32)
            loss_ref[...] = jnp.zeros(loss_ref.shape, F32)

        wv = w_ref[...]
        gv = g_ref[...]
        ff = _dot(a_ref[...], wv)
        inv = lax.rsqrt(jnp.mean(ff * ff, axis=-1, keepdims=True) + EPS)
        err = h_ref[...] + ff * inv * gv - t_ref[...]
        loss_ref[...] += jnp.sum(err * err, axis=0, keepdims=True)
        dy = err * (1.0 / d)
        dy_ref[...] = dy
        du, dg_rows = _rms_bwd(dy, ff, gv)
        dff = du.astype(BF16)
        dff_ref[...] = dff
        dg_ref[...] += jnp.sum(dg_rows, axis=0, keepdims=True)
        dact_ref[...] = _dot_nt(dff, wv)

    return pl.pallas_call(
        body, name=name, grid=(t // tm,),
        in_specs=[_rows(tm, f), _resident(wd.shape), _whole(g.shape), _rows(tm, d), _rows(tm, d)],
        out_specs=[_rows(tm, d), _rows(tm, d), _rows(tm, f), _whole((1, d)), _whole((1, d))],
        out_shape=[jax.ShapeDtypeStruct((t, d), F32), jax.ShapeDtypeStruct((t, d), BF16),
                   jax.ShapeDtypeStruct((t, f), F32), jax.ShapeDtypeStruct((1, d), F32),
                   jax.ShapeDtypeStruct((1, d), F32)],
        compiler_params=_params(("arbitrary",)),
    )(act, wd, g, h1, target)


def merge_out(y_rnn, o_att, gts, w_br, w_ba, w_out, g, x, name, tm=256):
    t = y_rnn.shape[0]
    d = w_br.shape[1]
    tm = _tile(t, tm)

    def body(y_ref, o_ref, g_ref, wbr_ref, wba_ref, wo_ref, gn_ref, x_ref, m_ref, br_ref, ba_ref, mix_ref, h_ref):
        br = _dot(y_ref[...].astype(BF16), wbr_ref[...])
        ba = _dot(o_ref[...].astype(BF16), wba_ref[...])
        gv = g_ref[...]
        merged = (_sig(gv[:, :d]) * br + _sig(gv[:, d:]) * ba).astype(BF16)
        m_ref[...] = merged
        br_ref[...] = br
        ba_ref[...] = ba
        mix = _dot(merged, wo_ref[...])
        mix_ref[...] = mix
        inv = lax.rsqrt(jnp.mean(mix * mix, axis=-1, keepdims=True) + EPS)
        h_ref[...] = x_ref[...] + mix * inv * gn_ref[...]

    sd = jax.ShapeDtypeStruct
    return pl.pallas_call(
        body, name=name, grid=(t // tm,),
        in_specs=[_rows(tm, y_rnn.shape[1]), _rows(tm, o_att.shape[1]), _rows(tm, 2 * d),
                  _whole(w_br.shape), _whole(w_ba.shape), _whole(w_out.shape), _whole(g.shape), _rows(tm, d)],
        out_specs=[_rows(tm, d)] * 5,
        out_shape=[sd((t, d), BF16), sd((t, d), F32), sd((t, d), F32), sd((t, d), F32), sd((t, d), F32)],
        compiler_params=_params(("parallel",)),
    )(y_rnn, o_att, gts, w_br, w_ba, w_out, g, x)


def mid_bwd(dhn2, h1, g_ffn, dy, mix, g_mix, w_out, gts, br, ba, w_br, w_ba, merged, y_rnn, o_att, name, tm=256):
    t, d = h1.shape
    tm = _tile(t, tm)
    rnn, hkv = w_br.shape[0], w_ba.shape[0]
    wsh = d // N_CHIPS

    def body(dhn_ref, h_ref, gf_ref, dy_ref, mix_ref, gm_ref, wo_ref, g_ref, br_ref, ba_ref, wbr_ref, wba_ref,
             m_ref, y_ref, o_ref, dh_ref, dg_ref, dyr_ref, doa_ref, dgf_ref, dgm_ref, dwo_ref, dwbr_ref, dwba_ref,
             acc_o, acc_br, acc_ba):
        @pl.when(pl.program_id(0) == 0)
        def _():
            dgf_ref[...] = jnp.zeros(dgf_ref.shape, F32)
            dgm_ref[...] = jnp.zeros(dgm_ref.shape, F32)
            acc_o[...] = jnp.zeros(acc_o.shape, F32)
            acc_br[...] = jnp.zeros(acc_br.shape, F32)
            acc_ba[...] = jnp.zeros(acc_ba.shape, F32)

        du, rows_f = _rms_bwd(dhn_ref[...], h_ref[...], gf_ref[...])
        dh1 = du + dy_ref[...]
        dh_ref[...] = dh1
        dgf_ref[...] += jnp.sum(rows_f, axis=0, keepdims=True)
        dmx, rows_m = _rms_bwd(dh1, mix_ref[...], gm_ref[...])
        dmix = dmx.astype(BF16)
        acc_o[...] += _dot_tn(m_ref[...], dmix)
        dgm_ref[...] += jnp.sum(rows_m, axis=0, keepdims=True)
        dm = _dot_nt(dmix, wo_ref[...])
        gv = g_ref[...]
        sr = _sig(gv[:, :d])
        sa = _sig(gv[:, d:])
        dbr = (dm * sr).astype(BF16)
        dba = (dm * sa).astype(BF16)
        acc_br[...] += _dot_tn(y_ref[...].astype(BF16), dbr)
        acc_ba[...] += _dot_tn(o_ref[...].astype(BF16), dba)
        dg_ref[:, :d] = (dm * br_ref[...] * sr * (1.0 - sr)).astype(BF16)
        dg_ref[:, d:] = (dm * ba_ref[...] * sa * (1.0 - sa)).astype(BF16)
        dyr_ref[...] = _dot_nt(dbr, wbr_ref[...])
        doa_ref[...] = _dot_nt(dba, wba_ref[...])

        @pl.when(pl.program_id(0) == pl.num_programs(0) - 1)
        def _():
            dwo_ref[...] = acc_o[...].astype(BF16)
            dwbr_ref[...] = acc_br[...].astype(BF16)
            for j in range(N_CHIPS):
                dwba_ref[j] = acc_ba[:, j * wsh:(j + 1) * wsh].astype(BF16)

    sd = jax.ShapeDtypeStruct
    row, vec = _rows(tm, d), _whole((1, d))
    once = pl.Buffered(1)

    def resident(shape):
        return pl.BlockSpec(shape, lambda i: (0,) * len(shape), pipeline_mode=once)

    return pl.pallas_call(
        body, name=name, grid=(t // tm,),
        in_specs=[row, row, vec, row, row, vec, resident(w_out.shape), _rows(tm, 2 * d), row, row,
                  resident(w_br.shape), resident(w_ba.shape), row, _rows(tm, rnn), _rows(tm, hkv)],
        out_specs=[row, _rows(tm, 2 * d), _rows(tm, rnn), _rows(tm, hkv), vec, vec,
                   resident((d, d)), resident((rnn, d)), resident((N_CHIPS, hkv, wsh))],
        out_shape=[sd((t, d), F32), sd((t, 2 * d), BF16), sd((t, rnn), F32), sd((t, hkv), F32), sd((1, d), F32),
                   sd((1, d), F32), sd((d, d), BF16), sd((rnn, d), BF16), sd((N_CHIPS, hkv, wsh), BF16)],
        scratch_shapes=[pltpu.VMEM((d, d), F32), pltpu.VMEM((rnn, d), F32), pltpu.VMEM((hkv, d), F32)],
        compiler_params=_params(("arbitrary",)),
    )(dhn2, h1, g_ffn, dy, mix, g_mix, w_out, gts, br, ba, w_br, w_ba, merged, y_rnn, o_att)


def _shift_dn(x, d, fill, row):
    return jnp.where(row >= d, pltpu.roll(x, d, 0), fill)


def _shift_up(x, d, fill, row):
    s = x.shape[0]
    return jnp.where(row < s - d, pltpu.roll(x, s - d, 0), fill)


def _conv_fwd(x, w, b, row):
    kk = w.shape[0]
    y = b + w[kk - 1:kk, :] * x
    for j in range(1, kk):
        y = y + w[kk - 1 - j:kk - j, :] * _shift_dn(x, j, 0.0, row)
    return y


def _conv_bwd(dy, x, w, row):
    kk = w.shape[0]
    dx = w[kk - 1:kk, :] * dy
    dws = [None] * kk
    dws[kk - 1] = jnp.sum(dy * x, axis=0, keepdims=True)
    for j in range(1, kk):
        ahead = _shift_up(dy, j, 0.0, row)
        dx = dx + w[kk - 1 - j:kk - j, :] * ahead
        dws[kk - 1 - j] = jnp.sum(ahead * x, axis=0, keepdims=True)
    return dx, jnp.concatenate(dws, axis=0)


def _softplus(z):
    y = jnp.exp(-jnp.abs(z))
    u = 1.0 + y
    dd = u - 1.0
    log1p = jnp.where(dd == 0.0, y, jnp.log(u) * (y / jnp.where(dd == 0.0, 1.0, dd)))
    return jnp.maximum(z, 0.0) + log1p


def _lru_decay(xb, wa, ba, lam):
    r = _sig(_dot(xb, wa) + ba)
    sp = _softplus(-lam)
    la = (-LRU_C) * r * sp
    return r, sp, la, jnp.exp(la)


def _lru_gates(xc, wa, ba, wx, bx, lam):
    xb = xc.astype(BF16)
    r, sp, la, a = _lru_decay(xb, wa, ba, lam)
    i = _sig(_dot(xb, wx) + bx)
    one_m_a2 = jnp.tanh(-la) * (1.0 + a * a)
    inv_mult = lax.rsqrt(one_m_a2)
    return r, i, sp, a, one_m_a2 * inv_mult, inv_mult


def _seg_len(s):
    seg = -(-s // 8)
    return seg + (4 - seg % 8) % 8


def _scan_rows(a_pad, u_pad, out_pad, reverse):
    planes, rows8, lanes = a_pad.shape
    seg = rows8 // 8
    sub = lax.broadcasted_iota(jnp.int32, (planes, 8, lanes), 1)

    unroll = 4

    def rows(k, d):
        i = k * unroll + d
        return pl.ds((seg - 1 - i) if reverse else i, 8, stride=seg)

    def ends(k, carry):
        h, p = carry
        for d in range(unroll):
            a = a_pad[:, rows(k, d), :]
            h = a * h + u_pad[:, rows(k, d), :]
            p = a * p
        return h, p

    init = (jnp.zeros((planes, 8, lanes), F32), jnp.ones((planes, 8, lanes), F32))
    h_end, p_end = lax.fori_loop(0, seg // unroll, ends, init)
    start = jnp.zeros((planes, 8, lanes), F32)
    for _ in range(7):
        nxt = h_end + p_end * start
        if reverse:
            start = jnp.where(sub < 7, pltpu.roll(nxt, 7, 1), 0.0)
        else:
            start = jnp.where(sub >= 1, pltpu.roll(nxt, 1, 1), 0.0)

    def redo(k, h):
        for d in range(unroll):
            h = a_pad[:, rows(k, d), :] * h + u_pad[:, rows(k, d), :]
            out_pad[:, rows(k, d), :] = h
        return h

    lax.fori_loop(0, seg // unroll, redo, start)


def _lru_cols(c, rb):
    return 2 * rb if c % (2 * rb) == 0 else rb


def rglru_fwd(xr, cw, cb, wa, ba, wx, bx, lam, name, ride=()):
    b, s, c = xr.shape
    rb = wa.shape[1]
    kk = cw.shape[0]
    cols = _lru_cols(c, rb)
    nj = cols // rb
    seg = _seg_len(s)

    def body(x_ref, cw_ref, cb_ref, wa_ref, ba_ref, wx_ref, bx_ref, lam_ref, h_ref, a_ref, xc_ref, a_pad, u_pad, h_pad):
        row = lax.broadcasted_iota(jnp.int32, (s, rb), 0)
        for j in range(nj):
            cs = slice(j * rb, (j + 1) * rb)
            xc = _conv_fwd(x_ref[:, cs], cw_ref[:, cs], cb_ref[:, cs], row)
            _, i, _, a, mult, _ = _lru_gates(xc, wa_ref[j], ba_ref[:, cs], wx_ref[j], bx_ref[:, cs], lam_ref[:, cs])
            xc_ref[:, cs] = xc
            a_ref[:, cs] = a
            a_pad[j, 0:s, :] = a
            u_pad[j, 0:s, :] = mult * (i * xc)
        a_pad[:, s:, :] = jnp.ones((nj, 8 * seg - s, rb), F32)
        u_pad[:, s:, :] = jnp.zeros((nj, 8 * seg - s, rb), F32)
        _scan_rows(a_pad, u_pad, h_pad, False)
        for j in range(nj):
            h_ref[:, j * rb:(j + 1) * rb] = h_pad[j, 0:s, :]

    vec = pl.BlockSpec((1, cols), lambda bi, n: (0, n))
    seq = pl.BlockSpec((None, s, cols), lambda bi, n: (bi, 0, n))
    mat = pl.BlockSpec((nj, rb, rb), lambda bi, n: (n, 0, 0))
    r_ins, r_in_specs, r_outs, r_out_specs, r_sems = _ride_args(ride)
    outs = pl.pallas_call(
        _riding(body, 8, 3, 3, ride, 2), name=name, grid=(b, c // cols),
        in_specs=[seq, pl.BlockSpec((kk, cols), lambda bi, n: (0, n)), vec, mat, vec, mat, vec, vec] + r_in_specs,
        out_specs=[seq] * 3 + r_out_specs,
        out_shape=[jax.ShapeDtypeStruct((b, s, c), F32)] * 3 + r_outs,
        scratch_shapes=[pltpu.VMEM((nj, 8 * seg, rb), F32)] * 3 + r_sems,
        compiler_params=_params(("arbitrary", "arbitrary")),
    )(xr, cw, cb, wa, ba, wx, bx, lam, *r_ins)
    return outs[:3], _ride_results(ride, outs[3:])


def rglru_bwd(xr, h, dh, a_fwd, xc_fwd, cw, wa, ba, wx, bx, lam, name, ride=()):
    b, s, c = xr.shape
    nb, rb = wa.shape[0], wa.shape[1]
    kk = cw.shape[0]
    cols = _lru_cols(c, rb)
    nj = cols // rb
    seg = _seg_len(s)

    def body(x_ref, h_ref, dh_ref, a_ref, xc_ref, cw_ref, wa_ref, ba_ref, wx_ref, bx_ref, lam_ref,
             dx_ref, dcw_ref, dcb_ref, dwa_ref, dba_ref, dwx_ref, dbx_ref, dlam_ref, b_pad, g_pad, l_pad):
        @pl.when(pl.program_id(1) == 0)
        def _():
            for ref in (dcw_ref, dcb_ref, dwa_ref, dba_ref, dwx_ref, dbx_ref, dlam_ref):
                ref[...] = jnp.zeros(ref.shape, F32)

        row = lax.broadcasted_iota(jnp.int32, (s, rb), 0)

        for j in range(nj):
            b_pad[j, 0:s, :] = _shift_up(a_ref[:, j * rb:(j + 1) * rb], 1, 0.0, row)
            g_pad[j, 0:s, :] = dh_ref[:, j * rb:(j + 1) * rb]
        b_pad[:, s:, :] = jnp.zeros((nj, 8 * seg - s, rb), F32)
        g_pad[:, s:, :] = jnp.zeros((nj, 8 * seg - s, rb), F32)
        _scan_rows(b_pad, g_pad, l_pad, True)

        for j in range(nj):
            cs = slice(j * rb, (j + 1) * rb)
            x = x_ref[:, cs]
            cwv = cw_ref[:, cs]
            wav, wxv, lamv = wa_ref[j], wx_ref[j], lam_ref[:, cs]
            xc = xc_ref[:, cs]
            r, i, sp, a, mult, inv_mult = _lru_gates(xc, wav, ba_ref[:, cs], wxv, bx_ref[:, cs], lamv)
            lmb = l_pad[j, 0:s, :]
            h_prev = _shift_dn(h_ref[:, cs], 1, 0.0, row)
            da = lmb * h_prev
            ixc = i * xc
            dla = da * a - (lmb * ixc) * (a * a) * inv_mult
            di = lmb * mult * xc
            dxc = lmb * mult * i
            dr = dla * ((-LRU_C) * sp)
            dsp = jnp.sum(dla * ((-LRU_C) * r), axis=0, keepdims=True)
            dga = dr * r * (1.0 - r)
            dgx = di * i * (1.0 - i)
            dga_b, dgx_b = dga.astype(BF16), dgx.astype(BF16)
            xb = xc.astype(BF16)
            dwa_ref[j] += _dot_tn(xb, dga_b)
            dwx_ref[j] += _dot_tn(xb, dgx_b)
            dba_ref[:, cs] += jnp.sum(dga, axis=0, keepdims=True)
            dbx_ref[:, cs] += jnp.sum(dgx, axis=0, keepdims=True)
            dlam_ref[:, cs] += dsp * (-_sig(-lamv))
            dxc = dxc + _dot_nt(dga_b, wav) + _dot_nt(dgx_b, wxv)
            dcb_ref[:, cs] += jnp.sum(dxc, axis=0, keepdims=True)
            dx, dcw = _conv_bwd(dxc, x, cwv, row)
            dcw_ref[:, cs] += dcw
            dx_ref[:, cs] = dx.astype(dx_ref.dtype)

    vec = pl.BlockSpec((1, cols), lambda n, bi: (0, n))
    seq = pl.BlockSpec((None, s, cols), lambda n, bi: (bi, 0, n))
    mat = pl.BlockSpec((nj, rb, rb), lambda n, bi: (n, 0, 0))
    cws = pl.BlockSpec((kk, cols), lambda n, bi: (0, n))
    sd = jax.ShapeDtypeStruct
    r_ins, r_in_specs, r_outs, r_out_specs, r_sems = _ride_args(ride)
    outs = pl.pallas_call(
        _riding(body, 11, 8, 3, ride, 2), name=name, grid=(c // cols, b),
        in_specs=[seq, seq, seq, seq, seq, cws, mat, vec, mat, vec, vec] + r_in_specs,
        out_specs=[seq, cws, vec, mat, vec, mat, vec, vec] + r_out_specs,
        out_shape=[sd((b, s, c), BF16), sd((kk, c), F32), sd((1, c), F32), sd((nb, rb, rb), F32),
                   sd((1, c), F32), sd((nb, rb, rb), F32), sd((1, c), F32), sd((1, c), F32)] + r_outs,
        scratch_shapes=[pltpu.VMEM((nj, 8 * seg, rb), F32)] * 3 + r_sems,
        compiler_params=_params(("arbitrary", "arbitrary")),
    )(xr, h, dh, a_fwd, xc_fwd, cw, wa, ba, wx, bx, lam, *r_ins)
    return outs[:8], _ride_results(ride, outs[8:])


_GELU_C = math.sqrt(2.0 / math.pi)


def _gelu_parts(x):
    th = jnp.tanh(_GELU_C * (x + 0.044715 * x * x * x))
    gel = 0.5 * x * (1.0 + th)
    dgel = 0.5 * (1.0 + th) + 0.5 * x * (1.0 - th * th) * _GELU_C * (1.0 + 3 * 0.044715 * x * x)
    return gel, dgel


def ffn_in_act(x, g, wg, wu, cw, cb, seq_len, name, tm=256):
    t, d = x.shape
    f = wg.shape[1]
    kk = cw.shape[0]
    tm = _tile(seq_len, tm)
    tiles_per_seq = seq_len // tm
    keep = 8
    assert kk - 1 <= keep

    def body(x_ref, g_ref, wg_ref, wu_ref, cw_ref, cb_ref, hn_ref, gp_ref, up_ref, act_ref, tail):
        @pl.when(pl.program_id(0) % tiles_per_seq == 0)
        def _():
            tail[...] = jnp.zeros(tail.shape, F32)

        xv = x_ref[...]
        inv = lax.rsqrt(jnp.mean(xv * xv, axis=-1, keepdims=True) + EPS)
        hn = (xv * inv * g_ref[...]).astype(BF16)
        hn_ref[...] = hn
        gp = _dot(hn, wg_ref[...])
        up = _dot(hn, wu_ref[...])
        gp_ref[...] = gp
        up_ref[...] = up
        cwv = cw_ref[...]
        row = lax.broadcasted_iota(jnp.int32, (tm, 1), 0)
        gate = _conv_fwd(gp, cwv, cb_ref[...], row)
        row8 = lax.broadcasted_iota(jnp.int32, (keep, 1), 0)
        prev = tail[...]
        fix = jnp.zeros((keep, f), F32)
        for j in range(1, kk):
            fix = fix + cwv[kk - 1 - j:kk - j, :] * jnp.where(row8 < j, pltpu.roll(prev, j, 0), 0.0)
        gate = jnp.concatenate([gate[:keep] + fix, gate[keep:]], axis=0)
        tail[...] = gp[tm - keep:, :]
        gel, _ = _gelu_parts(gate)
        act_ref[...] = (gel * up).astype(BF16)

    sd = jax.ShapeDtypeStruct
    return pl.pallas_call(
        body, name=name, grid=(t // tm,),
        in_specs=[_rows(tm, d), _whole(g.shape), _whole(wg.shape), _whole(wu.shape), _whole(cw.shape), _whole(cb.shape)],
        out_specs=[_rows(tm, d), _rows(tm, f), _rows(tm, f), _rows(tm, f)],
        out_shape=[sd((t, d), BF16), sd((t, f), F32), sd((t, f), F32), sd((t, f), BF16)],
        scratch_shapes=[pltpu.VMEM((keep, f), F32)],
        compiler_params=_params(("arbitrary",)),
    )(x, g, wg, wu, cw, cb)


def ffn_in_bwd(dact, gate_pre, up, cw, cb, wg, wu, seq_len, name, ride=(), tm=256):
    t, f = gate_pre.shape
    d = wg.shape[0]
    kk = cw.shape[0]
    tm = _tile(seq_len, tm)
    nt = t // tm
    tiles_per_seq = seq_len // tm
    keep = 8
    assert kk - 1 <= keep

    def body(da_ref, g_ref, halo_ref, u_ref, cw_ref, cb_ref, wg_ref, wu_ref,
             dg_ref, du_ref, dhn_ref, dcw_ref, dcb_ref, nxt):
        tile = (nt - 1 - pl.program_id(0)) % tiles_per_seq

        @pl.when(pl.program_id(0) == 0)
        def _():
            dcw_ref[...] = jnp.zeros(dcw_ref.shape, F32)
            dcb_ref[...] = jnp.zeros(dcb_ref.shape, F32)

        @pl.when(tile == tiles_per_seq - 1)
        def _():
            nxt[...] = jnp.zeros(nxt.shape, F32)

        row = lax.broadcasted_iota(jnp.int32, (tm, 1), 0)
        row8 = lax.broadcasted_iota(jnp.int32, (keep, 1), 0)
        gp = g_ref[...]
        cwv = cw_ref[...]
        prev = jnp.where(tile > 0, halo_ref[...], 0.0)
        gate = _conv_fwd(gp, cwv, cb_ref[...], row)
        fix = jnp.zeros((keep, f), F32)
        for j in range(1, kk):
            fix = fix + cwv[kk - 1 - j:kk - j, :] * jnp.where(row8 < j, pltpu.roll(prev, j, 0), 0.0)
        gate = jnp.concatenate([gate[:keep] + fix, gate[keep:]], axis=0)
        gel, dgel = _gelu_parts(gate)
        da = da_ref[...]
        dup = (da * gel).astype(BF16)
        du_ref[...] = dup
        dgate = da * u_ref[...] * dgel
        dcb_ref[...] += jnp.sum(dgate, axis=0, keepdims=True)
        after = nxt[...]
        dgp = cwv[kk - 1:kk, :] * dgate
        tail_fix = jnp.zeros((keep, f), F32)
        dws = [None] * kk
        dws[kk - 1] = jnp.sum(dgate * gp, axis=0, keepdims=True)
        for j in range(1, kk):
            wj = cwv[kk - 1 - j:kk - j, :]
            dgp = dgp + wj * _shift_up(dgate, j, 0.0, row)
            tail_fix = tail_fix + wj * jnp.where(row8 >= keep - j, pltpu.roll(after, keep - j, 0), 0.0)
            dws[kk - 1 - j] = (jnp.sum(dgate * _shift_dn(gp, j, 0.0, row), axis=0, keepdims=True)
                               + jnp.sum(dgate[:keep] * jnp.where(row8 < j, pltpu.roll(prev, j, 0), 0.0),
                                         axis=0, keepdims=True))
        dgp = jnp.concatenate([dgp[:tm - keep], dgp[tm - keep:] + tail_fix], axis=0).astype(BF16)
        nxt[...] = dgate[:keep]
        dcw_ref[...] += jnp.concatenate(dws, axis=0)
        dg_ref[...] = dgp
        dhn_ref[...] = _dot_nt(dgp, wg_ref[...]) + _dot_nt(dup, wu_ref[...])

    def rev(i):
        return nt - 1 - i

    rows_f = pl.BlockSpec((tm, f), lambda i: (rev(i), 0))
    halo = pl.BlockSpec((None, keep, f), lambda i: (jnp.maximum(rev(i) * (tm // keep) - 1, 0), 0, 0))
    once = pl.Buffered(1)
    sd = jax.ShapeDtypeStruct
    r_ins, r_in_specs, r_outs, r_out_specs, r_sems = _ride_args(ride)
    outs = pl.pallas_call(
        _riding(body, 8, 5, 1, ride, 1), name=name, grid=(nt,),
        in_specs=[rows_f, rows_f, halo, rows_f, _whole(cw.shape), _whole(cb.shape),
                  pl.BlockSpec(wg.shape, lambda i: (0, 0), pipeline_mode=once),
                  pl.BlockSpec(wu.shape, lambda i: (0, 0), pipeline_mode=once)] + r_in_specs,
        out_specs=[rows_f, rows_f, pl.BlockSpec((tm, d), lambda i: (rev(i), 0)), _whole((kk, f)), _whole((1, f))]
        + r_out_specs,
        out_shape=[sd((t, f), BF16), sd((t, f), BF16), sd((t, d), F32), sd((kk, f), F32), sd((1, f), F32)] + r_outs,
        scratch_shapes=[pltpu.VMEM((keep, f), F32)] + r_sems,
        compiler_params=_params(("arbitrary",)),
    )(dact, gate_pre, gate_pre.reshape(t // keep, keep, f), up, cw, cb, wg, wu, *r_ins)
    return outs[:5], _ride_results(ride, outs[5:])


def _t5_bucket(dist):
    max_exact = REL_BUCKETS // 2
    d = np.maximum(dist, 1).astype(np.float32)
    large = max_exact + np.log(d / max_exact) / math.log(REL_MAX_DIST / max_exact) * (REL_BUCKETS - max_exact)
    large = np.minimum(large.astype(np.int32), REL_BUCKETS - 1)
    return np.where(dist < max_exact, dist, large).astype(np.int32)


def _band(window, dilation):
    qi = np.arange(ATTN_BLOCK)[:, None]
    kj = np.arange(2 * ATTN_BLOCK)[None, :]
    delta = ATTN_BLOCK + qi - kj
    mask = (delta >= 0) & (delta <= window // dilation)
    bucket = _t5_bucket(np.maximum(delta, 0) * dilation)
    return mask, bucket


def _attn_blocks(s, r):
    m = s // r
    assert m % ATTN_BLOCK == 0, "sequence length must be a multiple of dilation * block"
    return m // ATTN_BLOCK


def _perm_load(ref, r):
    if r == 1:
        return ref[...]
    m = ref.shape[0] // r
    return jnp.concatenate([ref[pl.ds(c, m, stride=r), :] for c in range(r)], axis=0)


def _perm_store(ref, g, val, r, add=False):
    if r == 1:
        ref[g] = ref[g] + val if add else val
        return
    m = val.shape[0] // r
    for c in range(r):
        rows = pl.ds(c, m, stride=r)
        part = val[c * m:(c + 1) * m]
        ref[g, rows, :] = ref[g, rows, :] + part if add else part


def _blocks(x):
    return x.reshape(x.shape[0] // ATTN_BLOCK, ATTN_BLOCK, x.shape[1])


def _prev_blocks(x):
    return jnp.concatenate([x[:1], x[:-1]], axis=0)


def _next_blocks(x):
    return jnp.concatenate([x[1:], jnp.zeros_like(x[:1])], axis=0)


def _first_block_neg(s, r):
    nblk = s // ATTN_BLOCK
    idx = lax.broadcasted_iota(jnp.int32, (nblk, 1, 1), 0)
    return jnp.where(idx % _attn_blocks(s, r) == 0, NEG, 0.0)


def _bdot_nt(a, b):
    return lax.dot_general(a, b, (((2,), (2,)), ((0,), (0,))), preferred_element_type=F32)


def _bdot(a, b):
    return lax.dot_general(a, b, (((2,), (1,)), ((0,), (0,))), preferred_element_type=F32)


def _bdot_tn(a, b):
    return lax.dot_general(a, b, (((1,), (1,)), ((0,), (0,))), preferred_element_type=F32)


def attn_fwd(qkv, biasm, n_heads, name, ride=()):
    b, s, _ = qkv.shape
    h = n_heads
    scale = HEAD_DIM ** -0.5
    blk = ATTN_BLOCK

    def body(q1_ref, q2_ref, q3_ref, k_ref, v_ref, bias_ref, o_ref, lse_ref, acc, m_s, l_s):
        for g, q_ref in enumerate((q1_ref, q2_ref, q3_ref)):
            r = DILATED[g][1]
            first = _first_block_neg(s, r)
            q = _blocks(_perm_load(q_ref, r).astype(BF16))
            k = _blocks(_perm_load(k_ref, r).astype(BF16))
            v = _blocks(_perm_load(v_ref, r).astype(BF16))
            s_cur = _bdot_nt(q, k) * scale + bias_ref[g, :, blk:]
            s_prev = _bdot_nt(q, _prev_blocks(k)) * scale + bias_ref[g, :, :blk] + first
            m = jnp.max(jnp.maximum(s_cur, s_prev), axis=-1, keepdims=True)
            p_cur = jnp.exp(s_cur - m)
            p_prev = jnp.exp(s_prev - m)
            l = jnp.sum(p_cur + p_prev, axis=-1, keepdims=True)
            o = _bdot(p_cur.astype(BF16), v) + _bdot(p_prev.astype(BF16), _prev_blocks(v))
            _perm_store(acc, g, o.reshape(s, HEAD_DIM), r)
            _perm_store(m_s, g, m.reshape(s, 1), r)
            _perm_store(l_s, g, l.reshape(s, 1), r)
        m_all = jnp.maximum(jnp.maximum(m_s[0], m_s[1]), m_s[2])
        w = [jnp.exp(m_s[g] - m_all) for g in range(N_GROUPS)]
        l = w[0] * l_s[0] + w[1] * l_s[1] + w[2] * l_s[2]
        o_ref[...] = (w[0] * acc[0] + w[1] * acc[1] + w[2] * acc[2]) / l
        lse_ref[...] = m_all + jnp.log(l)

    def col(j):
        return pl.BlockSpec((None, s, HEAD_DIM), lambda bi, hi, j=j: (bi, 0, j * h + hi))

    r_ins, r_in_specs, r_outs, r_out_specs, r_sems = _ride_args(ride)
    outs = pl.pallas_call(
        _riding(body, 6, 2, 3, ride, 2), name=name, grid=(b, h),
        in_specs=[col(0), col(1), col(2), col(3), col(4),
                  pl.BlockSpec((N_GROUPS, None, blk, 2 * blk), lambda bi, hi: (0, hi, 0, 0))] + r_in_specs,
        out_specs=[pl.BlockSpec((None, s, HEAD_DIM), lambda bi, hi: (bi, 0, hi)),
                   pl.BlockSpec((None, None, s, 1), lambda bi, hi: (bi, hi, 0, 0))] + r_out_specs,
        out_shape=[jax.ShapeDtypeStruct((b, s, h * HEAD_DIM), F32), jax.ShapeDtypeStruct((b, h, s, 1), F32)] + r_outs,
        scratch_shapes=[pltpu.VMEM((N_GROUPS, s, HEAD_DIM), F32), pltpu.VMEM((N_GROUPS, s, 1), F32),
                        pltpu.VMEM((N_GROUPS, s, 1), F32)] + r_sems,
        compiler_params=_params(("arbitrary", "arbitrary")),
    )(qkv, qkv, qkv, qkv, qkv, biasm, *r_ins)
    return outs[0], outs[1], _ride_results(ride, outs[2:])


def attn_bwd(qkv, biasm, o, lse, do, n_heads, name, ride=()):
    b, s, _ = qkv.shape
    h = n_heads
    scale = HEAD_DIM ** -0.5
    blk = ATTN_BLOCK

    def body(q1_ref, q2_ref, q3_ref, k_ref, v_ref, bias_ref, o_ref, lse_ref, do_ref,
             dq1_ref, dq2_ref, dq3_ref, dk_ref, dv_ref, ds_ref, dq_acc, kv_acc, delta):
        delta[...] = jnp.sum(do_ref[...] * o_ref[...], axis=-1, keepdims=True)
        kv_acc[...] = jnp.zeros(kv_acc.shape, F32)
        for g, q_ref in enumerate((q1_ref, q2_ref, q3_ref)):
            r = DILATED[g][1]
            first = _first_block_neg(s, r)
            q = _blocks(_perm_load(q_ref, r).astype(BF16))
            k = _blocks(_perm_load(k_ref, r).astype(BF16))
            v = _blocks(_perm_load(v_ref, r).astype(BF16))
            dob = _blocks(_perm_load(do_ref, r).astype(BF16))
            lse_b = _blocks(_perm_load(lse_ref, r))
            dl_b = _blocks(_perm_load(delta, r))
            k_prev, v_prev = _prev_blocks(k), _prev_blocks(v)
            p_cur = jnp.exp(_bdot_nt(q, k) * scale + bias_ref[g, :, blk:] - lse_b)
            p_prev = jnp.exp(_bdot_nt(q, k_prev) * scale + bias_ref[g, :, :blk] + first - lse_b)
            ds_cur = p_cur * (_bdot_nt(dob, v) - dl_b)
            ds_prev = p_prev * (_bdot_nt(dob, v_prev) - dl_b)
            ds_ref[g, :, blk:] = jnp.sum(ds_cur, axis=0)
            ds_ref[g, :, :blk] = jnp.sum(ds_prev, axis=0)
            ds_cur_b, ds_prev_b = ds_cur.astype(BF16), ds_prev.astype(BF16)
            dq = (_bdot(ds_cur_b, k) + _bdot(ds_prev_b, k_prev)) * scale
            _perm_store(dq_acc, g, dq.reshape(s, HEAD_DIM), r)
            dk = (_bdot_tn(ds_cur_b, q) + _next_blocks(_bdot_tn(ds_prev_b, q))) * scale
            dv = _bdot_tn(p_cur.astype(BF16), dob) + _next_blocks(_bdot_tn(p_prev.astype(BF16), dob))
            _perm_store(kv_acc, 0, dk.reshape(s, HEAD_DIM), r, add=True)
            _perm_store(kv_acc, 1, dv.reshape(s, HEAD_DIM), r, add=True)
        for g, out_ref in enumerate((dq1_ref, dq2_ref, dq3_ref)):
            out_ref[...] = dq_acc[g].astype(out_ref.dtype)
        dk_ref[...] = kv_acc[0].astype(dk_ref.dtype)
        dv_ref[...] = kv_acc[1].astype(dv_ref.dtype)

    def col(j):
        return pl.BlockSpec((None, s, HEAD_DIM), lambda bi, hi, j=j: (bi, 0, j * h + hi))

    head = pl.BlockSpec((None, s, HEAD_DIM), lambda bi, hi: (bi, 0, hi))
    sd = jax.ShapeDtypeStruct
    r_ins, r_in_specs, r_outs, r_out_specs, r_sems = _ride_args(ride)
    outs = pl.pallas_call(
        _riding(body, 9, 6, 3, ride, 2), name=name, grid=(b, h),
        in_specs=[col(0), col(1), col(2), col(3), col(4),
                  pl.BlockSpec((N_GROUPS, None, blk, 2 * blk), lambda bi, hi: (0, hi, 0, 0)),
                  head, pl.BlockSpec((None, None, s, 1), lambda bi, hi: (bi, hi, 0, 0)), head] + r_in_specs,
        out_specs=[head] * 5 + [pl.BlockSpec((None, None, N_GROUPS, blk, 2 * blk), lambda bi, hi: (bi, hi, 0, 0, 0))]
        + r_out_specs,
        out_shape=[sd((b, s, h * HEAD_DIM), BF16)] * 5 + [sd((b, h, N_GROUPS, blk, 2 * blk), F32)] + r_outs,
        scratch_shapes=[pltpu.VMEM((N_GROUPS, s, HEAD_DIM), F32), pltpu.VMEM((2, s, HEAD_DIM), F32),
                        pltpu.VMEM((s, 1), F32)] + r_sems,
        compiler_params=_params(("arbitrary", "arbitrary")),
    )(qkv, qkv, qkv, qkv, qkv, biasm, o, lse, do, *r_ins)
    return outs[:6], _ride_results(ride, outs[6:])


def bias_table(rel_rows, bucket_f, n_heads, name):
    g, blk, blk2 = bucket_f.shape
    h = n_heads

    def body(rb_ref, bk_ref, o_ref):
        bk = bk_ref[...]
        rb = rb_ref[...]
        acc = jnp.full((blk, blk2), NEG, F32)
        for bucket in range(REL_BUCKETS):
            acc = jnp.where(bk == float(bucket), rb[:, bucket:bucket + 1], acc)
        o_ref[...] = acc

    return pl.pallas_call(
        body, name=name, grid=(g, h),
        in_specs=[pl.BlockSpec((None, 1, 128), lambda gi, hi: (gi * h + hi, 0, 0)),
                  pl.BlockSpec((None, blk, blk2), lambda gi, hi: (gi, 0, 0))],
        out_specs=pl.BlockSpec((None, None, blk, blk2), lambda gi, hi: (gi, hi, 0, 0)),
        out_shape=jax.ShapeDtypeStruct((g, h, blk, blk2), F32),
        compiler_params=_params(("parallel", "parallel")),
    )(rel_rows, bucket_f)


def bias_grad(ds_sum, bucket_f, name):
    b, h, g, blk, blk2 = ds_sum.shape

    def body(ds_ref, bk_ref, o_ref):
        tot = jnp.sum(ds_ref[...], axis=0)
        bk = bk_ref[...]
        lane = lax.broadcasted_iota(jnp.int32, (1, 128), 1)
        vec = jnp.zeros((1, 128), F32)
        for bucket in range(REL_BUCKETS):
            val = jnp.sum(jnp.where(bk == float(bucket), tot, 0.0), keepdims=True)
            vec = vec + jnp.where(lane == bucket, val, 0.0)
        o_ref[...] = vec

    return pl.pallas_call(
        body, name=name, grid=(g, h),
        in_specs=[pl.BlockSpec((b, None, None, blk, blk2), lambda gi, hi: (0, hi, gi, 0, 0)),
                  pl.BlockSpec((None, blk, blk2), lambda gi, hi: (gi, 0, 0))],
        out_specs=pl.BlockSpec((None, 1, 128), lambda gi, hi: (gi * h + hi, 0, 0)),
        out_shape=jax.ShapeDtypeStruct((g * h, 1, 128), F32),
        compiler_params=_params(("parallel", "parallel")),
    )(ds_sum, bucket_f)


def _chip_peers():
    x, y, c = lax.axis_index("x"), lax.axis_index("y"), lax.axis_index("c")
    me = 2 * x + y
    peers = [(1 - x, y, c), (x, 1 - y, c), (1 - x, 1 - y, c)]
    peer_chip = [2 * (1 - x) + y, 2 * x + (1 - y), 2 * (1 - x) + (1 - y)]
    return me, peers, peer_chip


def _any_specs(n):
    return [pl.BlockSpec(memory_space=pl.ANY)] * n


_MID_NUM, _MID_DEN = 3, 4


class _Exchange:
    def start(self, ins, outs, sems):
        local, sends, _ = self._copies(ins, outs, sems)
        for cp in local + sends:
            cp.start()

    def mid(self, ins, outs, sems):
        pass

    def wait(self, ins, outs, sems):
        local, sends, recvs = self._copies(ins, outs, sems)
        for cp in recvs():
            cp.wait_recv()
        for cp in sends:
            cp.wait_send()
        for cp in local:
            cp.wait()


class _Gather(_Exchange):
    HALF_ROWS = 16

    def __init__(self, arrays):
        n = len(arrays)
        self.ins = list(arrays)
        self.split = [a.shape[0] % (2 * self.HALF_ROWS) == 0 for a in arrays]
        self.out_shape = [jax.ShapeDtypeStruct((N_CHIPS,) + a.shape, a.dtype) for a in arrays]
        dma = pltpu.SemaphoreType.DMA
        self.sems = [dma((3 * n,)), dma((3 * n,)), dma((n,)), dma((3 * n,)), dma((3 * n,))]

    def _half(self, i, ref, sibling=False):
        if not self.split[i]:
            return ref
        half = self.ins[i].shape[0] // 2
        c = lax.axis_index("c")
        c = 1 - c if sibling else c
        return ref.at[pl.ds(pl.multiple_of(c * half, self.HALF_ROWS), half)]

    def _plan(self, ins, outs, sems):
        send1, recv1, local_sems, send2, recv2 = sems
        me, peers, peer_chip = _chip_peers()
        x, y, c = lax.axis_index("x"), lax.axis_index("y"), lax.axis_index("c")
        n = len(ins)
        pairs = [(i, k) for i in range(n) for k in range(3)]

        def fetch(i, k, slot):
            return pltpu.make_async_remote_copy(src_ref=self._half(i, ins[i]), dst_ref=self._half(i, outs[i].at[slot]),
                                                send_sem=send1.at[3 * i + k], recv_sem=recv1.at[3 * i + k],
                                                device_id=peers[k], device_id_type=MESH)

        def share(i, k, sibling):
            part = self._half(i, outs[i].at[peer_chip[k]], sibling)
            return pltpu.make_async_remote_copy(src_ref=part, dst_ref=part, send_sem=send2.at[3 * i + k],
                                                recv_sem=recv2.at[3 * i + k], device_id=(x, y, 1 - c),
                                                device_id_type=MESH)

        split_pairs = [(i, k) for i, k in pairs if self.split[i]]
        return dict(
            local=lambda: [pltpu.make_async_copy(ins[i], outs[i].at[me], local_sems.at[i]) for i in range(n)],
            fetch_out=lambda: [fetch(i, k, me) for i, k in pairs],
            fetch_in=lambda: [fetch(i, k, peer_chip[k]) for i, k in pairs],
            share_out=lambda: [share(i, k, False) for i, k in split_pairs],
            share_in=lambda: [share(i, k, True) for i, k in split_pairs])

    def start(self, ins, outs, sems):
        plan = self._plan(ins, outs, sems)
        for cp in plan["local"]() + plan["fetch_out"]():
            cp.start()

    def mid(self, ins, outs, sems):
        plan = self._plan(ins, outs, sems)
        for cp in plan["fetch_in"]():
            cp.wait_recv()
        for cp in plan["share_out"]():
            cp.start()

    def wait(self, ins, outs, sems):
        plan = self._plan(ins, outs, sems)
        for cp in plan["share_in"]():
            cp.wait_recv()
        for cp in plan["fetch_out"]() + plan["share_out"]():
            cp.wait_send()
        for cp in plan["local"]():
            cp.wait()


class _Scatter(_Exchange):
    def __init__(self, slabs, whole=()):
        self.n_slabs = len(slabs)
        self.ins = list(slabs) + list(whole)
        n = len(self.ins)
        self.out_shape = [jax.ShapeDtypeStruct(a.shape, a.dtype) for a in slabs] \
            + [jax.ShapeDtypeStruct((N_CHIPS,) + a.shape, a.dtype) for a in whole]
        self.sems = [pltpu.SemaphoreType.DMA((3 * n,)), pltpu.SemaphoreType.DMA((3 * n,)), pltpu.SemaphoreType.DMA((n,))]

    def _copies(self, ins, outs, sems):
        send_sems, recv_sems, local_sems = sems
        me, peers, peer_chip = _chip_peers()
        n = len(ins)

        def src(i, chip):
            return ins[i].at[chip] if i < self.n_slabs else ins[i]

        def remote(i, k, src_chip, slot):
            return pltpu.make_async_remote_copy(src_ref=src(i, src_chip), dst_ref=outs[i].at[slot],
                                                send_sem=send_sems.at[3 * i + k], recv_sem=recv_sems.at[3 * i + k],
                                                device_id=peers[k], device_id_type=MESH)

        local = [pltpu.make_async_copy(src(i, me), outs[i].at[me], local_sems.at[i]) for i in range(n)]
        sends = [remote(i, k, peer_chip[k], me) for i in range(n) for k in range(3)]
        return local, sends, lambda: [remote(i, k, me, peer_chip[k]) for i in range(n) for k in range(3)]


class _Swap(_Exchange):
    def __init__(self, arrays):
        n = len(arrays)
        self.ins = list(arrays)
        self.out_shape = [jax.ShapeDtypeStruct(a.shape, a.dtype) for a in arrays]
        self.sems = [pltpu.SemaphoreType.DMA((n,)), pltpu.SemaphoreType.DMA((n,))]

    def _copies(self, ins, outs, sems):
        send_sems, recv_sems = sems
        x, y, c = lax.axis_index("x"), lax.axis_index("y"), lax.axis_index("c")
        cps = [pltpu.make_async_remote_copy(src_ref=ins[i], dst_ref=outs[i], send_sem=send_sems.at[i],
                                            recv_sem=recv_sems.at[i], device_id=(x, y, 1 - c), device_id_type=MESH)
               for i in range(len(ins))]
        return [], cps, lambda: cps


def _riding(body, n_in, n_out, n_scratch, ride, rank):
    if not ride:
        return body
    r_in = sum(len(e.ins) for e in ride)
    r_out = sum(len(e.out_shape) for e in ride)

    def split(refs, sizes):
        out, a = [], 0
        for sz in sizes:
            out.append(refs[a:a + sz])
            a += sz
        return out

    def wrapped(*refs):
        a = 0
        parts = []
        for sz in (n_in, r_in, n_out, r_out, n_scratch):
            parts.append(refs[a:a + sz])
            a += sz
        own_in, ex_in, own_out, ex_out, own_scratch = parts
        ex_sems = refs[a:]
        ins = split(ex_in, [len(e.ins) for e in ride])
        outs = split(ex_out, [len(e.out_shape) for e in ride])
        sems = split(ex_sems, [len(e.sems) for e in ride])
        if rank:
            step, total = 0, 1
            for d in range(rank):
                step = step * pl.num_programs(d) + pl.program_id(d)
                total = total * pl.num_programs(d)

            @pl.when(step == 0)
            def _():
                for e, i, o, s in zip(ride, ins, outs, sems):
                    e.start(i, o, s)

            body(*own_in, *own_out, *own_scratch)

            @pl.when(step == (total * _MID_NUM) // _MID_DEN)
            def _():
                for e, i, o, s in zip(ride, ins, outs, sems):
                    e.mid(i, o, s)

            @pl.when(step == total - 1)
            def _():
                for e, i, o, s in zip(ride, ins, outs, sems):
                    e.wait(i, o, s)
        else:
            for phase in ("start", "mid", "wait"):
                for e, i, o, s in zip(ride, ins, outs, sems):
                    getattr(e, phase)(i, o, s)

    return wrapped


def _ride_args(ride):
    ins = [a for e in ride for a in e.ins]
    outs = [s for e in ride for s in e.out_shape]
    sems = [s for e in ride for s in e.sems]
    return ins, _any_specs(len(ins)), outs, _any_specs(len(outs)), sems


def _ride_results(ride, flat):
    out, a = [], 0
    for e in ride:
        out.append(list(flat[a:a + len(e.out_shape)]))
        a += len(e.out_shape)
    return out


def exchange(ride, name):
    ins, in_specs, outs, out_specs, sems = _ride_args(ride)
    res = pl.pallas_call(
        _riding(lambda: None, 0, 0, 0, ride, 0), name=name,
        in_specs=in_specs, out_specs=out_specs, out_shape=outs, scratch_shapes=sems,
    )(*ins)
    return _ride_results(ride, res)


def _sum_slots(ref):
    acc = ref[0].astype(F32)
    for j in range(1, ref.shape[0]):
        acc = acc + ref[j].astype(F32)
    return acc


def sum_pairs(mine, other, name, tr=176):
    n, r, w = mine.shape
    tr = r if r <= tr else _tile(r, tr)

    def body(a_ref, b_ref, o_ref):
        o_ref[...] = _sum_slots(a_ref) + _sum_slots(b_ref)

    spec = pl.BlockSpec((n, tr, w), lambda i: (0, i, 0))
    return pl.pallas_call(
        body, name=name, grid=(r // tr,),
        in_specs=[spec, spec], out_specs=_rows(tr, w),
        out_shape=jax.ShapeDtypeStruct((r, w), F32),
        compiler_params=_params(("parallel",)),
    )(mine, other)


def adamw(w, m, v, gs, name, tr=256):
    r, c = w.shape
    tr = r if r % 8 else _tile(r, tr)
    c1 = 1.0 - ADAM_B1 ** ADAM_STEP
    c2 = 1.0 - ADAM_B2 ** ADAM_STEP
    ng = len(gs)

    def body(w_ref, m_ref, v_ref, *refs):
        g_refs, (g_ref, d_ref, nm_ref, nv_ref) = refs[:ng], refs[ng:]
        g = g_refs[0][...] if ng == 1 else _sum_slots(g_refs[0]) + _sum_slots(g_refs[1])
        nm = ADAM_B1 * m_ref[...] + (1.0 - ADAM_B1) * g
        nv = ADAM_B2 * v_ref[...] + (1.0 - ADAM_B2) * (g * g)
        g_ref[...] = g
        nm_ref[...] = nm
        nv_ref[...] = nv
        d_ref[...] = (-ADAM_LR) * ((nm / c1) / (jnp.sqrt(nv / c2) + ADAM_EPS) + ADAM_WD * w_ref[...])

    spec = _rows(tr, c)
    gspec = spec if ng == 1 else pl.BlockSpec((N_CHIPS, tr, c), lambda i: (0, i, 0))
    return pl.pallas_call(
        body, name=name, grid=(r // tr,),
        in_specs=[spec] * 3 + [gspec] * ng, out_specs=[spec] * 4,
        out_shape=[jax.ShapeDtypeStruct((r, c), F32)] * 4,
        compiler_params=_params(("parallel",)),
    )(w, m, v, *gs)


_PARAMS = (
    ("rel_bias", None), ("norm_mix_pre", None), ("norm_mix_post", None), ("w_in", 1), ("conv_rnn_w", 1),
    ("conv_rnn_b", None), ("w_rg_a", None), ("b_rg_a", None), ("w_rg_x", None), ("b_rg_x", None),
    ("lru_lambda", None), ("w_branch_rnn", 0), ("w_branch_att", 1), ("w_out", 0), ("norm_ffn_pre", None),
    ("norm_ffn_post", None), ("w_ffn_gate", 1), ("w_ffn_up", 1), ("conv_ffn_w", 1), ("conv_ffn_b", None),
    ("w_ffn_down", 0),
)
_SMALL = 65536


def _as2d(a):
    a = a[0] if a.shape[0] == 1 and a.ndim >= 3 else a
    return a.reshape(-1, a.shape[-1]) if a.ndim == 3 else a


def _pack(pieces, dtype):
    flat = jnp.concatenate([p.astype(dtype).reshape(-1) for p in pieces])
    unit = PACK_W * PACK_ROWS
    pad = (-flat.shape[0]) % unit
    flat = jnp.pad(flat, (0, pad))
    return flat.reshape(-1, PACK_W)


def _unpack(buf, shapes):
    flat = buf.reshape(-1)
    out, off = [], 0
    for shp in shapes:
        n = int(np.prod(shp))
        out.append(flat[off:off + n].reshape(shp))
        off += n
    return out


def _join(slots, ax):
    if ax == 0:
        return slots.reshape(-1, slots.shape[-1])
    return jnp.transpose(slots, (1, 0, 2)).reshape(slots.shape[1], -1)


def _cut(full, ax):
    if ax == 0:
        return full.reshape(N_CHIPS, -1, full.shape[-1])
    return jnp.transpose(full.reshape(full.shape[0], N_CHIPS, -1), (1, 0, 2))


def kernel(x, rel_bias, norm_mix_pre, norm_mix_post, w_in, conv_rnn_w, conv_rnn_b, w_rg_a, b_rg_a, w_rg_x, b_rg_x, lru_lambda, w_branch_rnn, w_branch_att, w_out, norm_ffn_pre, norm_ffn_post, w_ffn_gate, w_ffn_up, conv_ffn_w, conv_ffn_b, w_ffn_down, loss_target, m_rel_bias, m_norm_mix_pre, m_norm_mix_post, m_w_in, m_conv_rnn_w, m_conv_rnn_b, m_w_rg_a, m_b_rg_a, m_w_rg_x, m_b_rg_x, m_lru_lambda, m_w_branch_rnn, m_w_branch_att, m_w_out, m_norm_ffn_pre, m_norm_ffn_post, m_w_ffn_gate, m_w_ffn_up, m_conv_ffn_w, m_conv_ffn_b, m_w_ffn_down, v_rel_bias, v_norm_mix_pre, v_norm_mix_post, v_w_in, v_conv_rnn_w, v_conv_rnn_b, v_w_rg_a, v_b_rg_a, v_w_rg_x, v_b_rg_x, v_lru_lambda, v_w_branch_rnn, v_w_branch_att, v_w_out, v_norm_ffn_pre, v_norm_ffn_post, v_w_ffn_gate, v_w_ffn_up, v_conv_ffn_w, v_conv_ffn_b, v_w_ffn_down):
    args = dict(locals())
    names = [n for n, _ in _PARAMS]
    axis = dict(_PARAMS)
    w_loc = {n: args[n] for n in names}
    m_loc = {n: args["m_" + n] for n in names}
    v_loc = {n: args["v_" + n] for n in names}
    sharded = [n for n in names if axis[n] is not None]
    replicated = [n for n in names if axis[n] is None]

    big = [n for n in sharded if w_loc[n].size >= _SMALL]
    small_sharded = [n for n in sharded if n not in big]
    small = replicated + small_sharded

    first = ["w_in"] + small_sharded
    srcs = [_as2d(w_loc[n]).astype(BF16) if n in big else _as2d(w_loc[n]) for n in first]
    (gathered,) = exchange([_Gather(srcs)], "gather_first")
    p = {n: _join(a, axis[n]) for n, a in zip(first, gathered)}
    for n in replicated:
        p[n] = _as2d(w_loc[n])
    shards = {n: _as2d(w_loc[n]).astype(BF16) for n in big if n not in first}

    last = "norm_mix_pre"
    early = [n for n in small if n != last]
    received, sibling, g_small, loss_part = _local_step(x, loss_target, p, shards, early)

    ((received["last"],),) = exchange([_Scatter([], [_pack([g_small[last]], BF16)])], "scatter_last")
    late = [n for n in received if n not in sibling]
    (swapped,) = exchange([_Swap([received[n] for n in late])], "swap_last")
    sibling.update(zip(late, swapped))
    early_sum = sum_pairs(received["small"], sibling["small"], "sum_small")
    last_sum = sum_pairs(received["last"], sibling["last"], "sum_last")
    g_tot = dict(zip(early, _unpack(early_sum, [g_small[n].shape for n in early])))
    (g_tot[last],) = _unpack(last_sum, [g_small[last].shape])
    chip = 2 * lax.axis_index("x") + lax.axis_index("y")
    for n in small_sharded:
        size = g_tot[n].shape[axis[n]] // N_CHIPS
        g_tot[n] = lax.dynamic_slice_in_dim(g_tot[n], chip * size, size, axis=axis[n])

    out_g, out_d, out_m, out_v = {}, {}, {}, {}
    for i, n in enumerate(names):
        shp = w_loc[n].shape
        gs = (received[n], sibling[n]) if n in big else (g_tot[n],)
        g, d, nm, nv = adamw(_as2d(w_loc[n]), _as2d(m_loc[n]), _as2d(v_loc[n]), gs, "adamw_" + n)
        out_g[n], out_d[n], out_m[n], out_v[n] = (t.reshape(shp) for t in (g, d, nm, nv))

    d_model = x.shape[-1]
    loss = lax.psum(0.5 * jnp.sum(loss_part) / d_model, ("x", "y", "c"))
    grad_x = g_small["x"]
    return (loss, grad_x, *[out_g[n] for n in names], *[out_d[n] for n in names],
            *[out_m[n] for n in names], *[out_v[n] for n in names])


def _local_step(x, target, p, shards, small_early):
    axis = dict(_PARAMS)
    b, s, d = x.shape
    t = b * s
    rnn = p["b_rg_a"].shape[1]
    ffn = p["conv_ffn_b"].shape[1]
    nbk = rnn // p["w_rg_a"].shape[1]
    hkv = (p["w_in"].shape[1] - rnn - 2 * d) // (N_GROUPS + 2)
    h = hkv // HEAD_DIM
    nq = N_GROUPS * hkv

    x2 = x.reshape(t, d)
    tgt = target.reshape(t, d)
    w_in = p["w_in"]
    in_splits = (rnn, nq + 2 * hkv, 2 * d)
    wa = p["w_rg_a"].reshape(nbk, -1, p["w_rg_a"].shape[1]).astype(BF16)
    wx = p["w_rg_x"].reshape(nbk, -1, p["w_rg_x"].shape[1]).astype(BF16)
    cw_r, cb_r = p["conv_rnn_w"], p["conv_rnn_b"]
    cw_f, cb_f = p["conv_ffn_w"], p["conv_ffn_b"]

    masks, buckets = zip(*[_band(w_, r_) for w_, r_ in DILATED])
    bucket_f = jnp.asarray(np.where(np.stack(masks), np.stack(buckets), -1).astype(np.float32))
    rel_rows = jnp.pad(p["rel_bias"].T, ((0, 0), (0, 128 - REL_BUCKETS)))[:, None, :]
    biasm = bias_table(rel_rows, bucket_f, h, "bias_table")

    early = ["w_branch_rnn", "w_branch_att", "w_out"]
    hn1, (xr, qkv, gts), (got,) = norm_mm(x2, p["norm_mix_pre"], [w_in], [in_splits], "in_proj",
                                          ride=[_Gather([shards[n] for n in early])])
    p.update({n: _join(a, axis[n]) for n, a in zip(early, got)})
    xr3 = xr.reshape(b, s, rnn)
    (y_rnn, a_rnn, xc_rnn), (got,) = rglru_fwd(xr3, cw_r, cb_r, wa, p["b_rg_a"], wx, p["b_rg_x"], p["lru_lambda"], "rglru_fwd",
                              ride=[_Gather([shards[n] for n in ("w_ffn_gate", "w_ffn_up")])])
    p.update({n: _join(a, axis[n]) for n, a in zip(("w_ffn_gate", "w_ffn_up"), got)})
    qkv3 = qkv.reshape(b, s, -1)
    o_att, lse, ((got,),) = attn_fwd(qkv3, biasm, h, "attn_fwd", ride=[_Gather([shards["w_ffn_down"]])])
    p["w_ffn_down"] = _join(got, axis["w_ffn_down"])
    merged, br, ba, mix, h1 = merge_out(y_rnn.reshape(t, rnn), o_att.reshape(t, hkv), gts, p["w_branch_rnn"],
                                        p["w_branch_att"], p["w_out"], p["norm_mix_post"], x2, "merge_out")
    hn2, gate_pre, up, act = ffn_in_act(h1, p["norm_ffn_pre"], p["w_ffn_gate"], p["w_ffn_up"], cw_f, cb_f, s, "ffn_in")

    g, gb = {}, {}
    recv, sib = {}, {}

    def rows4(a):
        return a.reshape(N_CHIPS, -1, a.shape[-1])

    dy, dff, dact, g["norm_ffn_post"], loss_part = ffn_down_loss(act, p["w_ffn_down"], p["norm_ffn_post"], h1, tgt,
                                                                  "ffn_down")
    gb["w_ffn_down"] = rows4(mm_tn(act, [dff], "ffn_down_dw"))
    (dgp, dup, dhn2, g["conv_ffn_w"], g["conv_ffn_b"]), ((recv["w_ffn_down"],),) = ffn_in_bwd(
        dact, gate_pre, up, cw_f, cb_f, p["w_ffn_gate"], p["w_ffn_up"], s, "ffn_in_bwd",
        ride=[_Scatter([gb["w_ffn_down"]])])
    gb["w_ffn_gate"] = mm_tn(hn2, [dgp], "ffn_gate_dw", col_shards=N_CHIPS)
    gb["w_ffn_up"] = mm_tn(hn2, [dup], "ffn_up_dw", col_shards=N_CHIPS)
    (dh1, dgts, dy_rnn, do_att, g["norm_ffn_pre"], g["norm_mix_post"], dw_out, dw_br,
     gb["w_branch_att"]) = mid_bwd(dhn2, h1, p["norm_ffn_pre"], dy, mix, p["norm_mix_post"], p["w_out"], gts, br, ba,
                                   p["w_branch_rnn"], p["w_branch_att"], merged, y_rnn.reshape(t, rnn),
                                   o_att.reshape(t, hkv), "mid_bwd")
    gb["w_out"], gb["w_branch_rnn"] = rows4(dw_out), rows4(dw_br)
    ffn_in = ["w_ffn_gate", "w_ffn_up"]
    (dxr, g["conv_rnn_w"], g["conv_rnn_b"], dwa, g["b_rg_a"], dwx, g["b_rg_x"], g["lru_lambda"]), (got,) = rglru_bwd(
        xr3, y_rnn, dy_rnn.reshape(b, s, rnn), a_rnn, xc_rnn, cw_r, wa, p["b_rg_a"], wx, p["b_rg_x"], p["lru_lambda"], "rglru_bwd",
        ride=[_Scatter([gb[n] for n in ffn_in])])
    recv.update(zip(ffn_in, got))
    g["w_rg_a"] = dwa.reshape(p["w_rg_a"].shape)
    g["w_rg_x"] = dwx.reshape(p["w_rg_x"].shape)
    mid = ["w_out", "w_branch_rnn", "w_branch_att"]
    early_recv = ["w_ffn_down"] + ffn_in
    (dq1, dq2, dq3, dk, dv, ds_sum), (got, swapped) = attn_bwd(
        qkv3, biasm, o_att, lse, do_att.reshape(b, s, hkv), h, "attn_bwd",
        ride=[_Scatter([gb[n] for n in mid]), _Swap([recv[n] for n in early_recv])])
    recv.update(zip(mid, got))
    sib.update(zip(early_recv, swapped))
    rows = bias_grad(ds_sum, bucket_f, "bias_grad")
    g["rel_bias"] = rows[:, 0, :REL_BUCKETS].T
    dproj = [dxr.reshape(t, rnn)] + [a.reshape(t, hkv) for a in (dq1, dq2, dq3, dk, dv)] + [dgts]
    dw_a, (got,) = mm_tn(hn1, dproj[:4], "in_proj_dw_a", ride=[_Swap([recv[n] for n in mid])])
    sib.update(zip(mid, got))
    pack = _pack([g[n] for n in small_early], BF16)
    dw_b, ((recv["small"],),) = mm_tn(hn1, dproj[4:], "in_proj_dw_b", ride=[_Scatter([], [pack])])
    gb["w_in"] = _cut(jnp.concatenate([dw_a[0], dw_b[0]], axis=1), 1)
    dx, g["norm_mix_pre"], ((recv["w_in"],),) = mm_nt(
        [(dproj, w_in)], "in_proj_dx", norm=(x2, p["norm_mix_pre"], dh1), ride=[_Scatter([gb["w_in"]])])
    g["x"] = dx.reshape(b, s, d)
    return recv, sib, g, loss_part
```

```python
import math

import numpy as np
import jax
import jax.numpy as jnp
from jax import lax
from jax.experimental import pallas as pl
from jax.experimental.pallas import tpu as pltpu

F32 = jnp.float32
BF16 = jnp.bfloat16

EPS = 1e-6
HEAD_DIM = 128
ATTN_BLOCK = 128
DILATED = ((128, 1), (512, 4), (2048, 16))
N_GROUPS = len(DILATED)
REL_BUCKETS = 32
REL_MAX_DIST = 2048
LRU_C = 8.0
NEG = -1e30

ADAM_LR = 0.001
ADAM_B1 = 0.9
ADAM_B2 = 0.999
ADAM_EPS = 1e-08
ADAM_WD = 0.01
ADAM_STEP = 10

N_CHIPS = 4
PACK_W = 1024
PACK_ROWS = 16
VMEM_LIMIT = 56 * 1024 * 1024
MESH = pl.DeviceIdType.MESH


def _params(sem=None):
    return pltpu.CompilerParams(dimension_semantics=sem, vmem_limit_bytes=VMEM_LIMIT)


def _dot(a, b):
    return jnp.dot(a, b, preferred_element_type=F32)


def _dot_nt(a, b):
    return lax.dot_general(a, b, (((1,), (1,)), ((), ())), preferred_element_type=F32)


def _dot_tn(a, b):
    return lax.dot_general(a, b, (((0,), (0,)), ((), ())), preferred_element_type=F32)


def _sig(x):
    return 0.5 * jnp.tanh(0.5 * x) + 0.5


def _rows(tm, w):
    return pl.BlockSpec((tm, w), lambda i: (i, 0))


def _whole(shape):
    nd = len(shape)
    return pl.BlockSpec(tuple(shape), lambda *_: (0,) * nd)


def _resident(shape):
    nd = len(shape)
    return pl.BlockSpec(tuple(shape), lambda *_: (0,) * nd, pipeline_mode=pl.Buffered(1))


def _tile(t, want):
    while t % want:
        want //= 2
    return want


def norm_mm(x, g, ws, splits, name, ride=(), tm=512):
    t, d = x.shape
    tm = _tile(t, tm)
    nw = len(ws)
    widths = [n for sp in splits for n in sp]

    def body(x_ref, g_ref, *refs):
        w_refs, hn_ref, o_refs = refs[:nw], refs[nw], refs[nw + 1:]
        xv = x_ref[...]
        inv = lax.rsqrt(jnp.mean(xv * xv, axis=-1, keepdims=True) + EPS)
        hn = (xv * inv * g_ref[...]).astype(BF16)
        hn_ref[...] = hn
        o = 0
        for w_ref, sp in zip(w_refs, splits):
            off = 0
            for n in sp:
                o_refs[o][...] = _dot(hn, w_ref[:, off:off + n])
                off += n
                o += 1

    r_ins, r_in_specs, r_outs, r_out_specs, r_sems = _ride_args(ride)
    n_out = 1 + len(widths)
    outs = pl.pallas_call(
        _riding(body, 2 + nw, n_out, 0, ride, 1), name=name, grid=(t // tm,),
        in_specs=[_rows(tm, d), _whole(g.shape)] + [_resident(w.shape) for w in ws] + r_in_specs,
        out_specs=[_rows(tm, d)] + [_rows(tm, n) for n in widths] + r_out_specs,
        out_shape=[jax.ShapeDtypeStruct((t, d), BF16)] + [jax.ShapeDtypeStruct((t, n), F32) for n in widths] + r_outs,
        scratch_shapes=r_sems,
        compiler_params=_params(("arbitrary",)),
    )(x, g, *ws, *r_ins)
    return outs[0], outs[1:n_out], _ride_results(ride, outs[n_out:])


def mm_nt(groups, name, ride=(), norm=None, tm=512):
    dys_all = [dy for dys, _ in groups for dy in dys]
    ws = [w for _, w in groups]
    t = dys_all[0].shape[0]
    k = ws[0].shape[0]
    tm = _tile(t, tm)
    n = len(dys_all)
    extra = list(norm) if norm else []

    def body(*refs):
        dy_refs, w_refs = refs[:n], refs[n:n + len(ws)]
        rest = refs[n + len(ws):]
        acc = None
        i = 0
        for (dys, _), w_ref in zip(groups, w_refs):
            off = 0
            for dy in dys:
                width = dy.shape[1]
                part = _dot_nt(dy_refs[i][...].astype(BF16), w_ref[:, off:off + width])
                acc = part if acc is None else acc + part
                off += width
                i += 1
        if norm:
            u_ref, g_ref, add_ref, o_ref, dg_ref = rest

            @pl.when(pl.program_id(0) == 0)
            def _():
                dg_ref[...] = jnp.zeros(dg_ref.shape, F32)

            du, dg_rows = _rms_bwd(acc, u_ref[...], g_ref[...])
            o_ref[...] = du + add_ref[...]
            dg_ref[...] += jnp.sum(dg_rows, axis=0, keepdims=True)
        else:
            rest[0][...] = acc

    n_out = 2 if norm else 1
    r_ins, r_in_specs, r_outs, r_out_specs, r_sems = _ride_args(ride)
    outs = pl.pallas_call(
        _riding(body, n + len(ws) + len(extra), n_out, 0, ride, 1), name=name, grid=(t // tm,),
        in_specs=[_rows(tm, dy.shape[1]) for dy in dys_all] + [_resident(w.shape) for w in ws]
        + ([_rows(tm, k), _whole((1, k)), _rows(tm, k)] if norm else []) + r_in_specs,
        out_specs=[_rows(tm, k)] + ([_whole((1, k))] if norm else []) + r_out_specs,
        out_shape=[jax.ShapeDtypeStruct((t, k), F32)] + ([jax.ShapeDtypeStruct((1, k), F32)] if norm else []) + r_outs,
        scratch_shapes=r_sems,
        compiler_params=_params(("arbitrary",)),
    )(*dys_all, *ws, *extra, *r_ins)
    return tuple(outs[:n_out]) + (_ride_results(ride, outs[n_out:]),)


def mm_tn(a, dys, name, col_shards=1, ride=(), tm=1024):
    t, k = a.shape
    tm = _tile(t, tm)
    n = len(dys)
    ntot = sum(dy.shape[1] for dy in dys)
    wsh = ntot // col_shards

    def body(a_ref, *refs):
        dy_refs, o_ref, acc = refs[:n], refs[n], refs[n + 1]

        @pl.when(pl.program_id(0) == 0)
        def _():
            acc[...] = jnp.zeros(acc.shape, F32)

        av = a_ref[...].astype(BF16)
        off = 0
        for dy_ref in dy_refs:
            width = dy_ref.shape[1]
            acc[:, off:off + width] += _dot_tn(av, dy_ref[...].astype(BF16))
            off += width

        @pl.when(pl.program_id(0) == pl.num_programs(0) - 1)
        def _():
            for j in range(col_shards):
                o_ref[j] = acc[:, j * wsh:(j + 1) * wsh].astype(o_ref.dtype)

    r_ins, r_in_specs, r_outs, r_out_specs, r_sems = _ride_args(ride)
    outs = pl.pallas_call(
        _riding(body, 1 + n, 1, 1, ride, 1), name=name, grid=(t // tm,),
        in_specs=[_rows(tm, k)] + [_rows(tm, dy.shape[1]) for dy in dys] + r_in_specs,
        out_specs=[_whole((col_shards, k, wsh))] + r_out_specs,
        out_shape=[jax.ShapeDtypeStruct((col_shards, k, wsh), BF16)] + r_outs,
        scratch_shapes=[pltpu.VMEM((k, ntot), F32)] + r_sems,
        compiler_params=_params(("arbitrary",)),
    )(a, *dys, *r_ins)
    return (outs[0], _ride_results(ride, outs[1:])) if ride else outs[0]


def _rms_bwd(dz, u, g):
    d = u.shape[-1]
    inv = lax.rsqrt(jnp.mean(u * u, axis=-1, keepdims=True) + EPS)
    dzg = dz * g
    proj = jnp.sum(dzg * u, axis=-1, keepdims=True) * (1.0 / d)
    du = inv * (dzg - u * (inv * inv) * proj)
    dg_rows = dz * u * inv
    return du, dg_rows


def ffn_down_loss(act, wd, g, h1, target, name, tm=512):
    t, f = act.shape
    d = wd.shape[1]
    tm = _tile(t, tm)

    def body(a_ref, w_ref, g_ref, h_ref, t_ref, dy_ref, dff_ref, dact_ref, dg_ref, loss_ref):
        @pl.when(pl.program_id(0) == 0)
        def _():
            dg_ref[...] = jnp.zeros(dg_ref.shape, F32)
            loss_ref[...] = jnp.zeros(loss_ref.shape, F32)

        wv = w_ref[...]
        gv = g_ref[...]
        ff = _dot(a_ref[...], wv)
        inv = lax.rsqrt(jnp.mean(ff * ff, axis=-1, keepdims=True) + EPS)
        err = h_ref[...] + ff * inv * gv - t_ref[...]
        loss_ref[...] += jnp.sum(err * err, axis=0, keepdims=True)
        dy = err * (1.0 / d)
        dy_ref[...] = dy
        du, dg_rows = _rms_bwd(dy, ff, gv)
        dff = du.astype(BF16)
        dff_ref[...] = dff
        dg_ref[...] += jnp.sum(dg_rows, axis=0, keepdims=True)
        dact_ref[...] = _dot_nt(dff, wv)

    return pl.pallas_call(
        body, name=name, grid=(t // tm,),
        in_specs=[_rows(tm, f), _resident(wd.shape), _whole(g.shape), _rows(tm, d), _rows(tm, d)],
        out_specs=[_rows(tm, d), _rows(tm, d), _rows(tm, f), _whole((1, d)), _whole((1, d))],
        out_shape=[jax.ShapeDtypeStruct((t, d), F32), jax.ShapeDtypeStruct((t, d), BF16),
                   jax.ShapeDtypeStruct((t, f), F32), jax.ShapeDtypeStruct((1, d), F32),
                   jax.ShapeDtypeStruct((1, d), F32)],
        compiler_params=_params(("arbitrary",)),
    )(act, wd, g, h1, target)


def merge_out(y_rnn, o_att, gts, w_br, w_ba, w_out, g, x, name, tm=512):
    t = y_rnn.shape[0]
    d = w_br.shape[1]
    tm = _tile(t, tm)

    def body(y_ref, o_ref, g_ref, wbr_ref, wba_ref, wo_ref, gn_ref, x_ref, m_ref, br_ref, ba_ref, mix_ref, h_ref):
        br = _dot(y_ref[...].astype(BF16), wbr_ref[...])
        ba = _dot(o_ref[...].astype(BF16), wba_ref[...])
        gv = g_ref[...]
        merged = (_sig(gv[:, :d]) * br + _sig(gv[:, d:]) * ba).astype(BF16)
        m_ref[...] = merged
        br_ref[...] = br
        ba_ref[...] = ba
        mix = _dot(merged, wo_ref[...])
        mix_ref[...] = mix
        inv = lax.rsqrt(jnp.mean(mix * mix, axis=-1, keepdims=True) + EPS)
        h_ref[...] = x_ref[...] + mix * inv * gn_ref[...]

    sd = jax.ShapeDtypeStruct
    return pl.pallas_call(
        body, name=name, grid=(t // tm,),
        in_specs=[_rows(tm, y_rnn.shape[1]), _rows(tm, o_att.shape[1]), _rows(tm, 2 * d),
                  _resident(w_br.shape), _resident(w_ba.shape), _resident(w_out.shape), _whole(g.shape), _rows(tm, d)],
        out_specs=[_rows(tm, d)] * 5,
        out_shape=[sd((t, d), BF16), sd((t, d), F32), sd((t, d), F32), sd((t, d), F32), sd((t, d), F32)],
        compiler_params=_params(("parallel",)),
    )(y_rnn, o_att, gts, w_br, w_ba, w_out, g, x)


def mid_bwd(dhn2, h1, g_ffn, dy, mix, g_mix, w_out, gts, br, ba, w_br, w_ba, merged, y_rnn, o_att, name, tm=256):
    t, d = h1.shape
    tm = _tile(t, tm)
    rnn, hkv = w_br.shape[0], w_ba.shape[0]
    wsh = d // N_CHIPS

    def body(dhn_ref, h_ref, gf_ref, dy_ref, mix_ref, gm_ref, wo_ref, g_ref, br_ref, ba_ref, wbr_ref, wba_ref,
             m_ref, y_ref, o_ref, dh_ref, dg_ref, dyr_ref, doa_ref, dgf_ref, dgm_ref, dwo_ref, dwbr_ref, dwba_ref,
             acc_o, acc_br, acc_ba):
        @pl.when(pl.program_id(0) == 0)
        def _():
            dgf_ref[...] = jnp.zeros(dgf_ref.shape, F32)
            dgm_ref[...] = jnp.zeros(dgm_ref.shape, F32)
            acc_o[...] = jnp.zeros(acc_o.shape, F32)
            acc_br[...] = jnp.zeros(acc_br.shape, F32)
            acc_ba[...] = jnp.zeros(acc_ba.shape, F32)

        du, rows_f = _rms_bwd(dhn_ref[...], h_ref[...], gf_ref[...])
        dh1 = du + dy_ref[...]
        dh_ref[...] = dh1
        dgf_ref[...] += jnp.sum(rows_f, axis=0, keepdims=True)
        dmx, rows_m = _rms_bwd(dh1, mix_ref[...], gm_ref[...])
        dmix = dmx.astype(BF16)
        acc_o[...] += _dot_tn(m_ref[...], dmix)
        dgm_ref[...] += jnp.sum(rows_m, axis=0, keepdims=True)
        dm = _dot_nt(dmix, wo_ref[...])
        gv = g_ref[...]
        sr = _sig(gv[:, :d])
        sa = _sig(gv[:, d:])
        dbr = (dm * sr).astype(BF16)
        dba = (dm * sa).astype(BF16)
        acc_br[...] += _dot_tn(y_ref[...].astype(BF16), dbr)
        acc_ba[...] += _dot_tn(o_ref[...].astype(BF16), dba)
        dg_ref[:, :d] = (dm * br_ref[...] * sr * (1.0 - sr)).astype(BF16)
        dg_ref[:, d:] = (dm * ba_ref[...] * sa * (1.0 - sa)).astype(BF16)
        dyr_ref[...] = _dot_nt(dbr, wbr_ref[...])
        doa_ref[...] = _dot_nt(dba, wba_ref[...])

        @pl.when(pl.program_id(0) == pl.num_programs(0) - 1)
        def _():
            dwo_ref[...] = acc_o[...].astype(BF16)
            dwbr_ref[...] = acc_br[...].astype(BF16)
            for j in range(N_CHIPS):
                dwba_ref[j] = acc_ba[:, j * wsh:(j + 1) * wsh].astype(BF16)

    sd = jax.ShapeDtypeStruct
    row, vec = _rows(tm, d), _whole((1, d))
    once = pl.Buffered(1)

    def resident(shape):
        return pl.BlockSpec(shape, lambda i: (0,) * len(shape), pipeline_mode=once)

    return pl.pallas_call(
        body, name=name, grid=(t // tm,),
        in_specs=[row, row, vec, row, row, vec, resident(w_out.shape), _rows(tm, 2 * d), row, row,
                  resident(w_br.shape), resident(w_ba.shape), row, _rows(tm, rnn), _rows(tm, hkv)],
        out_specs=[row, _rows(tm, 2 * d), _rows(tm, rnn), _rows(tm, hkv), vec, vec,
                   resident((d, d)), resident((rnn, d)), resident((N_CHIPS, hkv, wsh))],
        out_shape=[sd((t, d), F32), sd((t, 2 * d), BF16), sd((t, rnn), F32), sd((t, hkv), F32), sd((1, d), F32),
                   sd((1, d), F32), sd((d, d), BF16), sd((rnn, d), BF16), sd((N_CHIPS, hkv, wsh), BF16)],
        scratch_shapes=[pltpu.VMEM((d, d), F32), pltpu.VMEM((rnn, d), F32), pltpu.VMEM((hkv, d), F32)],
        compiler_params=_params(("arbitrary",)),
    )(dhn2, h1, g_ffn, dy, mix, g_mix, w_out, gts, br, ba, w_br, w_ba, merged, y_rnn, o_att)


def _shift_dn(x, d, fill, row):
    return jnp.where(row >= d, pltpu.roll(x, d, 0), fill)


def _shift_up(x, d, fill, row):
    s = x.shape[0]
    return jnp.where(row < s - d, pltpu.roll(x, s - d, 0), fill)


def _conv_fwd(x, w, b, row):
    kk = w.shape[0]
    y = b + w[kk - 1:kk, :] * x
    for j in range(1, kk):
        y = y + w[kk - 1 - j:kk - j, :] * _shift_dn(x, j, 0.0, row)
    return y


def _conv_bwd(dy, x, w, row):
    kk = w.shape[0]
    dx = w[kk - 1:kk, :] * dy
    dws = [None] * kk
    dws[kk - 1] = jnp.sum(dy * x, axis=0, keepdims=True)
    for j in range(1, kk):
        ahead = _shift_up(dy, j, 0.0, row)
        dx = dx + w[kk - 1 - j:kk - j, :] * ahead
        dws[kk - 1 - j] = jnp.sum(ahead * x, axis=0, keepdims=True)
    return dx, jnp.concatenate(dws, axis=0)


def _softplus(z):
    y = jnp.exp(-jnp.abs(z))
    u = 1.0 + y
    dd = u - 1.0
    log1p = jnp.where(dd == 0.0, y, jnp.log(u) * (y / jnp.where(dd == 0.0, 1.0, dd)))
    return jnp.maximum(z, 0.0) + log1p


def _lru_decay(xb, wa, ba, lam):
    r = _sig(_dot(xb, wa) + ba)
    sp = _softplus(-lam)
    la = (-LRU_C) * r * sp
    return r, sp, la, jnp.exp(la)


def _lru_gates(xc, wa, ba, wx, bx, lam):
    xb = xc.astype(BF16)
    r, sp, la, a = _lru_decay(xb, wa, ba, lam)
    i = _sig(_dot(xb, wx) + bx)
    one_m_a2 = jnp.tanh(-la) * (1.0 + a * a)
    inv_mult = lax.rsqrt(one_m_a2)
    return r, i, sp, a, one_m_a2 * inv_mult, inv_mult


def _seg_len(s):
    seg = -(-s // 8)
    return seg + (4 - seg % 8) % 8


def _scan_rows(a_pad, u_pad, out_pad, reverse):
    planes, rows8, lanes = a_pad.shape
    seg = rows8 // 8
    sub = lax.broadcasted_iota(jnp.int32, (planes, 8, lanes), 1)

    unroll = 4

    def rows(k, d):
        i = k * unroll + d
        return pl.ds((seg - 1 - i) if reverse else i, 8, stride=seg)

    def ends(k, carry):
        h, p = carry
        for d in range(unroll):
            a = a_pad[:, rows(k, d), :]
            h = a * h + u_pad[:, rows(k, d), :]
            p = a * p
        return h, p

    init = (jnp.zeros((planes, 8, lanes), F32), jnp.ones((planes, 8, lanes), F32))
    h_end, p_end = lax.fori_loop(0, seg // unroll, ends, init)
    start = jnp.zeros((planes, 8, lanes), F32)
    for _ in range(7):
        nxt = h_end + p_end * start
        if reverse:
            start = jnp.where(sub < 7, pltpu.roll(nxt, 7, 1), 0.0)
        else:
            start = jnp.where(sub >= 1, pltpu.roll(nxt, 1, 1), 0.0)

    def redo(k, h):
        for d in range(unroll):
            h = a_pad[:, rows(k, d), :] * h + u_pad[:, rows(k, d), :]
            out_pad[:, rows(k, d), :] = h
        return h

    lax.fori_loop(0, seg // unroll, redo, start)


def _lru_cols(c, rb):
    return 2 * rb if c % (2 * rb) == 0 else rb


def rglru_fwd(xr, cw, cb, wa, ba, wx, bx, lam, name, ride=()):
    b, s, c = xr.shape
    rb = wa.shape[1]
    kk = cw.shape[0]
    cols = _lru_cols(c, rb)
    nj = cols // rb
    seg = _seg_len(s)

    def body(x_ref, cw_ref, cb_ref, wa_ref, ba_ref, wx_ref, bx_ref, lam_ref, h_ref, a_ref, xc_ref, a_pad, u_pad, h_pad):
        row = lax.broadcasted_iota(jnp.int32, (s, rb), 0)
        for j in range(nj):
            cs = slice(j * rb, (j + 1) * rb)
            xc = _conv_fwd(x_ref[:, cs], cw_ref[:, cs], cb_ref[:, cs], row)
            _, i, _, a, mult, _ = _lru_gates(xc, wa_ref[j], ba_ref[:, cs], wx_ref[j], bx_ref[:, cs], lam_ref[:, cs])
            xc_ref[:, cs] = xc
            a_ref[:, cs] = a
            a_pad[j, 0:s, :] = a
            u_pad[j, 0:s, :] = mult * (i * xc)
        a_pad[:, s:, :] = jnp.ones((nj, 8 * seg - s, rb), F32)
        u_pad[:, s:, :] = jnp.zeros((nj, 8 * seg - s, rb), F32)
        _scan_rows(a_pad, u_pad, h_pad, False)
        for j in range(nj):
            h_ref[:, j * rb:(j + 1) * rb] = h_pad[j, 0:s, :]

    vec = pl.BlockSpec((1, cols), lambda bi, n: (0, n))
    seq = pl.BlockSpec((None, s, cols), lambda bi, n: (bi, 0, n))
    mat = pl.BlockSpec((nj, rb, rb), lambda bi, n: (n, 0, 0))
    r_ins, r_in_specs, r_outs, r_out_specs, r_sems = _ride_args(ride)
    outs = pl.pallas_call(
        _riding(body, 8, 3, 3, ride, 2), name=name, grid=(b, c // cols),
        in_specs=[seq, pl.BlockSpec((kk, cols), lambda bi, n: (0, n)), vec, mat, vec, mat, vec, vec] + r_in_specs,
        out_specs=[seq] * 3 + r_out_specs,
        out_shape=[jax.ShapeDtypeStruct((b, s, c), F32)] * 3 + r_outs,
        scratch_shapes=[pltpu.VMEM((nj, 8 * seg, rb), F32)] * 3 + r_sems,
        compiler_params=_params(("arbitrary", "arbitrary")),
    )(xr, cw, cb, wa, ba, wx, bx, lam, *r_ins)
    return outs[:3], _ride_results(ride, outs[3:])


def rglru_bwd(xr, h, dh, a_fwd, xc_fwd, cw, wa, ba, wx, bx, lam, name, ride=()):
    b, s, c = xr.shape
    nb, rb = wa.shape[0], wa.shape[1]
    kk = cw.shape[0]
    cols = _lru_cols(c, rb)
    nj = cols // rb
    seg = _seg_len(s)

    def body(x_ref, h_ref, dh_ref, a_ref, xc_ref, cw_ref, wa_ref, ba_ref, wx_ref, bx_ref, lam_ref,
             dx_ref, dcw_ref, dcb_ref, dwa_ref, dba_ref, dwx_ref, dbx_ref, dlam_ref, b_pad, g_pad, l_pad):
        @pl.when(pl.program_id(1) == 0)
        def _():
            for ref in (dcw_ref, dcb_ref, dwa_ref, dba_ref, dwx_ref, dbx_ref, dlam_ref):
                ref[...] = jnp.zeros(ref.shape, F32)

        row = lax.broadcasted_iota(jnp.int32, (s, rb), 0)

        for j in range(nj):
            b_pad[j, 0:s, :] = _shift_up(a_ref[:, j * rb:(j + 1) * rb], 1, 0.0, row)
            g_pad[j, 0:s, :] = dh_ref[:, j * rb:(j + 1) * rb]
        b_pad[:, s:, :] = jnp.zeros((nj, 8 * seg - s, rb), F32)
        g_pad[:, s:, :] = jnp.zeros((nj, 8 * seg - s, rb), F32)
        _scan_rows(b_pad, g_pad, l_pad, True)

        for j in range(nj):
            cs = slice(j * rb, (j + 1) * rb)
            x = x_ref[:, cs]
            cwv = cw_ref[:, cs]
            wav, wxv, lamv = wa_ref[j], wx_ref[j], lam_ref[:, cs]
            xc = xc_ref[:, cs]
            r, i, sp, a, mult, inv_mult = _lru_gates(xc, wav, ba_ref[:, cs], wxv, bx_ref[:, cs], lamv)
            lmb = l_pad[j, 0:s, :]
            h_prev = _shift_dn(h_ref[:, cs], 1, 0.0, row)
            da = lmb * h_prev
            ixc = i * xc
            dla = da * a - (lmb * ixc) * (a * a) * inv_mult
            di = lmb * mult * xc
            dxc = lmb * mult * i
            dr = dla * ((-LRU_C) * sp)
            dsp = jnp.sum(dla * ((-LRU_C) * r), axis=0, keepdims=True)
            dga = dr * r * (1.0 - r)
            dgx = di * i * (1.0 - i)
            dga_b, dgx_b = dga.astype(BF16), dgx.astype(BF16)
            xb = xc.astype(BF16)
            dwa_ref[j] += _dot_tn(xb, dga_b)
            dwx_ref[j] += _dot_tn(xb, dgx_b)
            dba_ref[:, cs] += jnp.sum(dga, axis=0, keepdims=True)
            dbx_ref[:, cs] += jnp.sum(dgx, axis=0, keepdims=True)
            dlam_ref[:, cs] += dsp * (-_sig(-lamv))
            dxc = dxc + _dot_nt(dga_b, wav) + _dot_nt(dgx_b, wxv)
            dcb_ref[:, cs] += jnp.sum(dxc, axis=0, keepdims=True)
            dx, dcw = _conv_bwd(dxc, x, cwv, row)
            dcw_ref[:, cs] += dcw
            dx_ref[:, cs] = dx.astype(dx_ref.dtype)

    vec = pl.BlockSpec((1, cols), lambda n, bi: (0, n))
    seq = pl.BlockSpec((None, s, cols), lambda n, bi: (bi, 0, n))
    mat = pl.BlockSpec((nj, rb, rb), lambda n, bi: (n, 0, 0))
    cws = pl.BlockSpec((kk, cols), lambda n, bi: (0, n))
    sd = jax.ShapeDtypeStruct
    r_ins, r_in_specs, r_outs, r_out_specs, r_sems = _ride_args(ride)
    outs = pl.pallas_call(
        _riding(body, 11, 8, 3, ride, 2), name=name, grid=(c // cols, b),
        in_specs=[seq, seq, seq, seq, seq, cws, mat, vec, mat, vec, vec] + r_in_specs,
        out_specs=[seq, cws, vec, mat, vec, mat, vec, vec] + r_out_specs,
        out_shape=[sd((b, s, c), BF16), sd((kk, c), F32), sd((1, c), F32), sd((nb, rb, rb), F32),
                   sd((1, c), F32), sd((nb, rb, rb), F32), sd((1, c), F32), sd((1, c), F32)] + r_outs,
        scratch_shapes=[pltpu.VMEM((nj, 8 * seg, rb), F32)] * 3 + r_sems,
        compiler_params=_params(("arbitrary", "arbitrary")),
    )(xr, h, dh, a_fwd, xc_fwd, cw, wa, ba, wx, bx, lam, *r_ins)
    return outs[:8], _ride_results(ride, outs[8:])


_GELU_C = math.sqrt(2.0 / math.pi)


def _gelu_parts(x):
    th = jnp.tanh(_GELU_C * (x + 0.044715 * x * x * x))
    gel = 0.5 * x * (1.0 + th)
    dgel = 0.5 * (1.0 + th) + 0.5 * x * (1.0 - th * th) * _GELU_C * (1.0 + 3 * 0.044715 * x * x)
    return gel, dgel


def ffn_in_act(x, g, wg, wu, cw, cb, seq_len, name, tm=256):
    t, d = x.shape
    f = wg.shape[1]
    kk = cw.shape[0]
    tm = _tile(seq_len, tm)
    tiles_per_seq = seq_len // tm
    keep = 8
    assert kk - 1 <= keep

    def body(x_ref, g_ref, wg_ref, wu_ref, cw_ref, cb_ref, hn_ref, gp_ref, up_ref, act_ref, tail):
        @pl.when(pl.program_id(0) % tiles_per_seq == 0)
        def _():
            tail[...] = jnp.zeros(tail.shape, F32)

        xv = x_ref[...]
        inv = lax.rsqrt(jnp.mean(xv * xv, axis=-1, keepdims=True) + EPS)
        hn = (xv * inv * g_ref[...]).astype(BF16)
        hn_ref[...] = hn
        gp = _dot(hn, wg_ref[...])
        up = _dot(hn, wu_ref[...])
        gp_ref[...] = gp
        up_ref[...] = up
        cwv = cw_ref[...]
        row = lax.broadcasted_iota(jnp.int32, (tm, 1), 0)
        gate = _conv_fwd(gp, cwv, cb_ref[...], row)
        row8 = lax.broadcasted_iota(jnp.int32, (keep, 1), 0)
        prev = tail[...]
        fix = jnp.zeros((keep, f), F32)
        for j in range(1, kk):
            fix = fix + cwv[kk - 1 - j:kk - j, :] * jnp.where(row8 < j, pltpu.roll(prev, j, 0), 0.0)
        gate = jnp.concatenate([gate[:keep] + fix, gate[keep:]], axis=0)
        tail[...] = gp[tm - keep:, :]
        gel, _ = _gelu_parts(gate)
        act_ref[...] = (gel * up).astype(BF16)

    sd = jax.ShapeDtypeStruct
    return pl.pallas_call(
        body, name=name, grid=(t // tm,),
        in_specs=[_rows(tm, d), _whole(g.shape), _whole(wg.shape), _whole(wu.shape), _whole(cw.shape), _whole(cb.shape)],
        out_specs=[_rows(tm, d), _rows(tm, f), _rows(tm, f), _rows(tm, f)],
        out_shape=[sd((t, d), BF16), sd((t, f), F32), sd((t, f), F32), sd((t, f), BF16)],
        scratch_shapes=[pltpu.VMEM((keep, f), F32)],
        compiler_params=_params(("arbitrary",)),
    )(x, g, wg, wu, cw, cb)


def ffn_in_bwd(dact, gate_pre, up, cw, cb, wg, wu, seq_len, name, ride=(), tm=256):
    t, f = gate_pre.shape
    d = wg.shape[0]
    kk = cw.shape[0]
    tm = _tile(seq_len, tm)
    nt = t // tm
    tiles_per_seq = seq_len // tm
    keep = 8
    assert kk - 1 <= keep

    def body(da_ref, g_ref, halo_ref, u_ref, cw_ref, cb_ref, wg_ref, wu_ref,
             dg_ref, du_ref, dhn_ref, dcw_ref, dcb_ref, nxt):
        tile = (nt - 1 - pl.program_id(0)) % tiles_per_seq

        @pl.when(pl.program_id(0) == 0)
        def _():
            dcw_ref[...] = jnp.zeros(dcw_ref.shape, F32)
            dcb_ref[...] = jnp.zeros(dcb_ref.shape, F32)

        @pl.when(tile == tiles_per_seq - 1)
        def _():
            nxt[...] = jnp.zeros(nxt.shape, F32)

        row = lax.broadcasted_iota(jnp.int32, (tm, 1), 0)
        row8 = lax.broadcasted_iota(jnp.int32, (keep, 1), 0)
        gp = g_ref[...]
        cwv = cw_ref[...]
        prev = jnp.where(tile > 0, halo_ref[...], 0.0)
        gate = _conv_fwd(gp, cwv, cb_ref[...], row)
        fix = jnp.zeros((keep, f), F32)
        for j in range(1, kk):
            fix = fix + cwv[kk - 1 - j:kk - j, :] * jnp.where(row8 < j, pltpu.roll(prev, j, 0), 0.0)
        gate = jnp.concatenate([gate[:keep] + fix, gate[keep:]], axis=0)
        gel, dgel = _gelu_parts(gate)
        da = da_ref[...]
        dup = (da * gel).astype(BF16)
        du_ref[...] = dup
        dgate = da * u_ref[...] * dgel
        dcb_ref[...] += jnp.sum(dgate, axis=0, keepdims=True)
        after = nxt[...]
        dgp = cwv[kk - 1:kk, :] * dgate
        tail_fix = jnp.zeros((keep, f), F32)
        dws = [None] * kk
        dws[kk - 1] = jnp.sum(dgate * gp, axis=0, keepdims=True)
        for j in range(1, kk):
            wj = cwv[kk - 1 - j:kk - j, :]
            dgp = dgp + wj * _shift_up(dgate, j, 0.0, row)
            tail_fix = tail_fix + wj * jnp.where(row8 >= keep - j, pltpu.roll(after, keep - j, 0), 0.0)
            dws[kk - 1 - j] = (jnp.sum(dgate * _shift_dn(gp, j, 0.0, row), axis=0, keepdims=True)
                               + jnp.sum(dgate[:keep] * jnp.where(row8 < j, pltpu.roll(prev, j, 0), 0.0),
                                         axis=0, keepdims=True))
        dgp = jnp.concatenate([dgp[:tm - keep], dgp[tm - keep:] + tail_fix], axis=0).astype(BF16)
        nxt[...] = dgate[:keep]
        dcw_ref[...] += jnp.concatenate(dws, axis=0)
        dg_ref[...] = dgp
        dhn_ref[...] = _dot_nt(dgp, wg_ref[...]) + _dot_nt(dup, wu_ref[...])

    def rev(i):
        return nt - 1 - i

    rows_f = pl.BlockSpec((tm, f), lambda i: (rev(i), 0))
    halo = pl.BlockSpec((None, keep, f), lambda i: (jnp.maximum(rev(i) * (tm // keep) - 1, 0), 0, 0))
    once = pl.Buffered(1)
    sd = jax.ShapeDtypeStruct
    r_ins, r_in_specs, r_outs, r_out_specs, r_sems = _ride_args(ride)
    outs = pl.pallas_call(
        _riding(body, 8, 5, 1, ride, 1), name=name, grid=(nt,),
        in_specs=[rows_f, rows_f, halo, rows_f, _whole(cw.shape), _whole(cb.shape),
                  pl.BlockSpec(wg.shape, lambda i: (0, 0), pipeline_mode=once),
                  pl.BlockSpec(wu.shape, lambda i: (0, 0), pipeline_mode=once)] + r_in_specs,
        out_specs=[rows_f, rows_f, pl.BlockSpec((tm, d), lambda i: (rev(i), 0)), _whole((kk, f)), _whole((1, f))]
        + r_out_specs,
        out_shape=[sd((t, f), BF16), sd((t, f), BF16), sd((t, d), F32), sd((kk, f), F32), sd((1, f), F32)] + r_outs,
        scratch_shapes=[pltpu.VMEM((keep, f), F32)] + r_sems,
        compiler_params=_params(("arbitrary",)),
    )(dact, gate_pre, gate_pre.reshape(t // keep, keep, f), up, cw, cb, wg, wu, *r_ins)
    return outs[:5], _ride_results(ride, outs[5:])


def _t5_bucket(dist):
    max_exact = REL_BUCKETS // 2
    d = np.maximum(dist, 1).astype(np.float32)
    large = max_exact + np.log(d / max_exact) / math.log(REL_MAX_DIST / max_exact) * (REL_BUCKETS - max_exact)
    large = np.minimum(large.astype(np.int32), REL_BUCKETS - 1)
    return np.where(dist < max_exact, dist, large).astype(np.int32)


def _band(window, dilation):
    qi = np.arange(ATTN_BLOCK)[:, None]
    kj = np.arange(2 * ATTN_BLOCK)[None, :]
    delta = ATTN_BLOCK + qi - kj
    mask = (delta >= 0) & (delta <= window // dilation)
    bucket = _t5_bucket(np.maximum(delta, 0) * dilation)
    return mask, bucket


def _attn_blocks(s, r):
    m = s // r
    assert m % ATTN_BLOCK == 0, "sequence length must be a multiple of dilation * block"
    return m // ATTN_BLOCK


def _perm_load(ref, r):
    if r == 1:
        return ref[...]
    m = ref.shape[0] // r
    return jnp.concatenate([ref[pl.ds(c, m, stride=r), :] for c in range(r)], axis=0)


def _perm_store(ref, g, val, r, add=False):
    if r == 1:
        ref[g] = ref[g] + val if add else val
        return
    m = val.shape[0] // r
    for c in range(r):
        rows = pl.ds(c, m, stride=r)
        part = val[c * m:(c + 1) * m]
        ref[g, rows, :] = ref[g, rows, :] + part if add else part


def _blocks(x):
    return x.reshape(x.shape[0] // ATTN_BLOCK, ATTN_BLOCK, x.shape[1])


def _prev_blocks(x):
    return jnp.concatenate([x[:1], x[:-1]], axis=0)


def _next_blocks(x):
    return jnp.concatenate([x[1:], jnp.zeros_like(x[:1])], axis=0)


def _first_block_neg(s, r):
    nblk = s // ATTN_BLOCK
    idx = lax.broadcasted_iota(jnp.int32, (nblk, 1, 1), 0)
    return jnp.where(idx % _attn_blocks(s, r) == 0, NEG, 0.0)


def _bdot_nt(a, b):
    return lax.dot_general(a, b, (((2,), (2,)), ((0,), (0,))), preferred_element_type=F32)


def _bdot(a, b):
    return lax.dot_general(a, b, (((2,), (1,)), ((0,), (0,))), preferred_element_type=F32)


def _bdot_tn(a, b):
    return lax.dot_general(a, b, (((1,), (1,)), ((0,), (0,))), preferred_element_type=F32)


def attn_fwd(qkv, biasm, n_heads, name, ride=()):
    b, s, _ = qkv.shape
    h = n_heads
    scale = HEAD_DIM ** -0.5
    blk = ATTN_BLOCK

    def body(q1_ref, q2_ref, q3_ref, k_ref, v_ref, bias_ref, o_ref, lse_ref, acc, m_s, l_s):
        for g, q_ref in enumerate((q1_ref, q2_ref, q3_ref)):
            r = DILATED[g][1]
            first = _first_block_neg(s, r)
            q = _blocks(_perm_load(q_ref, r).astype(BF16))
            k = _blocks(_perm_load(k_ref, r).astype(BF16))
            v = _blocks(_perm_load(v_ref, r).astype(BF16))
            s_cur = _bdot_nt(q, k) * scale + bias_ref[g, :, blk:]
            s_prev = _bdot_nt(q, _prev_blocks(k)) * scale + bias_ref[g, :, :blk] + first
            m = jnp.max(jnp.maximum(s_cur, s_prev), axis=-1, keepdims=True)
            p_cur = jnp.exp(s_cur - m)
            p_prev = jnp.exp(s_prev - m)
            l = jnp.sum(p_cur + p_prev, axis=-1, keepdims=True)
            o = _bdot(p_cur.astype(BF16), v) + _bdot(p_prev.astype(BF16), _prev_blocks(v))
            _perm_store(acc, g, o.reshape(s, HEAD_DIM), r)
            _perm_store(m_s, g, m.reshape(s, 1), r)
            _perm_store(l_s, g, l.reshape(s, 1), r)
        m_all = jnp.maximum(jnp.maximum(m_s[0], m_s[1]), m_s[2])
        w = [jnp.exp(m_s[g] - m_all) for g in range(N_GROUPS)]
        l = w[0] * l_s[0] + w[1] * l_s[1] + w[2] * l_s[2]
        o_ref[...] = (w[0] * acc[0] + w[1] * acc[1] + w[2] * acc[2]) / l
        lse_ref[...] = m_all + jnp.log(l)

    def col(j):
        return pl.BlockSpec((None, s, HEAD_DIM), lambda bi, hi, j=j: (bi, 0, j * h + hi))

    r_ins, r_in_specs, r_outs, r_out_specs, r_sems = _ride_args(ride)
    outs = pl.pallas_call(
        _riding(body, 6, 2, 3, ride, 2), name=name, grid=(b, h),
        in_specs=[col(0), col(1), col(2), col(3), col(4),
                  pl.BlockSpec((N_GROUPS, None, blk, 2 * blk), lambda bi, hi: (0, hi, 0, 0))] + r_in_specs,
        out_specs=[pl.BlockSpec((None, s, HEAD_DIM), lambda bi, hi: (bi, 0, hi)),
                   pl.BlockSpec((None, None, s, 1), lambda bi, hi: (bi, hi, 0, 0))] + r_out_specs,
        out_shape=[jax.ShapeDtypeStruct((b, s, h * HEAD_DIM), F32), jax.ShapeDtypeStruct((b, h, s, 1), F32)] + r_outs,
        scratch_shapes=[pltpu.VMEM((N_GROUPS, s, HEAD_DIM), F32), pltpu.VMEM((N_GROUPS, s, 1), F32),
                        pltpu.VMEM((N_GROUPS, s, 1), F32)] + r_sems,
        compiler_params=_params(("arbitrary", "arbitrary")),
    )(qkv, qkv, qkv, qkv, qkv, biasm, *r_ins)
    return outs[0], outs[1], _ride_results(ride, outs[2:])


def attn_bwd(qkv, biasm, o, lse, do, n_heads, name, ride=()):
    b, s, _ = qkv.shape
    h = n_heads
    scale = HEAD_DIM ** -0.5
    blk = ATTN_BLOCK

    def body(q1_ref, q2_ref, q3_ref, k_ref, v_ref, bias_ref, o_ref, lse_ref, do_ref,
             dq1_ref, dq2_ref, dq3_ref, dk_ref, dv_ref, ds_ref, dq_acc, kv_acc, delta):
        delta[...] = jnp.sum(do_ref[...] * o_ref[...], axis=-1, keepdims=True)
        kv_acc[...] = jnp.zeros(kv_acc.shape, F32)
        for g, q_ref in enumerate((q1_ref, q2_ref, q3_ref)):
            r = DILATED[g][1]
            first = _first_block_neg(s, r)
            q = _blocks(_perm_load(q_ref, r).astype(BF16))
            k = _blocks(_perm_load(k_ref, r).astype(BF16))
            v = _blocks(_perm_load(v_ref, r).astype(BF16))
            dob = _blocks(_perm_load(do_ref, r).astype(BF16))
            lse_b = _blocks(_perm_load(lse_ref, r))
            dl_b = _blocks(_perm_load(delta, r))
            k_prev, v_prev = _prev_blocks(k), _prev_blocks(v)
            p_cur = jnp.exp(_bdot_nt(q, k) * scale + bias_ref[g, :, blk:] - lse_b)
            p_prev = jnp.exp(_bdot_nt(q, k_prev) * scale + bias_ref[g, :, :blk] + first - lse_b)
            ds_cur = p_cur * (_bdot_nt(dob, v) - dl_b)
            ds_prev = p_prev * (_bdot_nt(dob, v_prev) - dl_b)
            ds_ref[g, :, blk:] = jnp.sum(ds_cur, axis=0)
            ds_ref[g, :, :blk] = jnp.sum(ds_prev, axis=0)
            ds_cur_b, ds_prev_b = ds_cur.astype(BF16), ds_prev.astype(BF16)
            dq = (_bdot(ds_cur_b, k) + _bdot(ds_prev_b, k_prev)) * scale
            _perm_store(dq_acc, g, dq.reshape(s, HEAD_DIM), r)
            dk = (_bdot_tn(ds_cur_b, q) + _next_blocks(_bdot_tn(ds_prev_b, q))) * scale
            dv = _bdot_tn(p_cur.astype(BF16), dob) + _next_blocks(_bdot_tn(p_prev.astype(BF16), dob))
            _perm_store(kv_acc, 0, dk.reshape(s, HEAD_DIM), r, add=True)
            _perm_store(kv_acc, 1, dv.reshape(s, HEAD_DIM), r, add=True)
        for g, out_ref in enumerate((dq1_ref, dq2_ref, dq3_ref)):
            out_ref[...] = dq_acc[g].astype(out_ref.dtype)
        dk_ref[...] = kv_acc[0].astype(dk_ref.dtype)
        dv_ref[...] = kv_acc[1].astype(dv_ref.dtype)

    def col(j):
        return pl.BlockSpec((None, s, HEAD_DIM), lambda bi, hi, j=j: (bi, 0, j * h + hi))

    head = pl.BlockSpec((None, s, HEAD_DIM), lambda bi, hi: (bi, 0, hi))
    sd = jax.ShapeDtypeStruct
    r_ins, r_in_specs, r_outs, r_out_specs, r_sems = _ride_args(ride)
    outs = pl.pallas_call(
        _riding(body, 9, 6, 3, ride, 2), name=name, grid=(b, h),
        in_specs=[col(0), col(1), col(2), col(3), col(4),
                  pl.BlockSpec((N_GROUPS, None, blk, 2 * blk), lambda bi, hi: (0, hi, 0, 0)),
                  head, pl.BlockSpec((None, None, s, 1), lambda bi, hi: (bi, hi, 0, 0)), head] + r_in_specs,
        out_specs=[head] * 5 + [pl.BlockSpec((None, None, N_GROUPS, blk, 2 * blk), lambda bi, hi: (bi, hi, 0, 0, 0))]
        + r_out_specs,
        out_shape=[sd((b, s, h * HEAD_DIM), BF16)] * 5 + [sd((b, h, N_GROUPS, blk, 2 * blk), F32)] + r_outs,
        scratch_shapes=[pltpu.VMEM((N_GROUPS, s, HEAD_DIM), F32), pltpu.VMEM((2, s, HEAD_DIM), F32),
                        pltpu.VMEM((s, 1), F32)] + r_sems,
        compiler_params=_params(("arbitrary", "arbitrary")),
    )(qkv, qkv, qkv, qkv, qkv, biasm, o, lse, do, *r_ins)
    return outs[:6], _ride_results(ride, outs[6:])


def bias_table(rel_rows, bucket_f, n_heads, name):
    g, blk, blk2 = bucket_f.shape
    h = n_heads

    def body(rb_ref, bk_ref, o_ref):
        bk = bk_ref[...]
        rb = rb_ref[...]
        acc = jnp.full((blk, blk2), NEG, F32)
        for bucket in range(REL_BUCKETS):
            acc = jnp.where(bk == float(bucket), rb[:, bucket:bucket + 1], acc)
        o_ref[...] = acc

    return pl.pallas_call(
        body, name=name, grid=(g, h),
        in_specs=[pl.BlockSpec((None, 1, 128), lambda gi, hi: (gi * h + hi, 0, 0)),
                  pl.BlockSpec((None, blk, blk2), lambda gi, hi: (gi, 0, 0))],
        out_specs=pl.BlockSpec((None, None, blk, blk2), lambda gi, hi: (gi, hi, 0, 0)),
        out_shape=jax.ShapeDtypeStruct((g, h, blk, blk2), F32),
        compiler_params=_params(("parallel", "parallel")),
    )(rel_rows, bucket_f)


def bias_grad(ds_sum, bucket_f, name):
    b, h, g, blk, blk2 = ds_sum.shape

    def body(ds_ref, bk_ref, o_ref):
        tot = jnp.sum(ds_ref[...], axis=0)
        bk = bk_ref[...]
        lane = lax.broadcasted_iota(jnp.int32, (1, 128), 1)
        vec = jnp.zeros((1, 128), F32)
        for bucket in range(REL_BUCKETS):
            val = jnp.sum(jnp.where(bk == float(bucket), tot, 0.0), keepdims=True)
            vec = vec + jnp.where(lane == bucket, val, 0.0)
        o_ref[...] = vec

    return pl.pallas_call(
        body, name=name, grid=(g, h),
        in_specs=[pl.BlockSpec((b, None, None, blk, blk2), lambda gi, hi: (0, hi, gi, 0, 0)),
                  pl.BlockSpec((None, blk, blk2), lambda gi, hi: (gi, 0, 0))],
        out_specs=pl.BlockSpec((None, 1, 128), lambda gi, hi: (gi * h + hi, 0, 0)),
        out_shape=jax.ShapeDtypeStruct((g * h, 1, 128), F32),
        compiler_params=_params(("parallel", "parallel")),
    )(ds_sum, bucket_f)


def _chip_peers():
    x, y, c = lax.axis_index("x"), lax.axis_index("y"), lax.axis_index("c")
    me = 2 * x + y
    peers = [(1 - x, y, c), (x, 1 - y, c), (1 - x, 1 - y, c)]
    peer_chip = [2 * (1 - x) + y, 2 * x + (1 - y), 2 * (1 - x) + (1 - y)]
    return me, peers, peer_chip


def _any_specs(n):
    return [pl.BlockSpec(memory_space=pl.ANY)] * n


_MID_NUM, _MID_DEN = 3, 4


class _Exchange:
    def start(self, ins, outs, sems):
        local, sends, _ = self._copies(ins, outs, sems)
        for cp in local + sends:
            cp.start()

    def mid(self, ins, outs, sems):
        pass

    def wait(self, ins, outs, sems):
        local, sends, recvs = self._copies(ins, outs, sems)
        for cp in recvs():
            cp.wait_recv()
        for cp in sends:
            cp.wait_send()
        for cp in local:
            cp.wait()


class _Gather(_Exchange):
    HALF_ROWS = 16

    def __init__(self, arrays):
        n = len(arrays)
        self.ins = list(arrays)
        self.split = [a.shape[0] % (2 * self.HALF_ROWS) == 0 for a in arrays]
        self.out_shape = [jax.ShapeDtypeStruct((N_CHIPS,) + a.shape, a.dtype) for a in arrays]
        dma = pltpu.SemaphoreType.DMA
        self.sems = [dma((3 * n,)), dma((3 * n,)), dma((n,)), dma((3 * n,)), dma((3 * n,))]

    def _half(self, i, ref, sibling=False):
        if not self.split[i]:
            return ref
        half = self.ins[i].shape[0] // 2
        c = lax.axis_index("c")
        c = 1 - c if sibling else c
        return ref.at[pl.ds(pl.multiple_of(c * half, self.HALF_ROWS), half)]

    def _plan(self, ins, outs, sems):
        send1, recv1, local_sems, send2, recv2 = sems
        me, peers, peer_chip = _chip_peers()
        x, y, c = lax.axis_index("x"), lax.axis_index("y"), lax.axis_index("c")
        n = len(ins)
        pairs = [(i, k) for k in range(3) for i in range(n)]

        def fetch(i, k, slot):
            return pltpu.make_async_remote_copy(src_ref=self._half(i, ins[i]), dst_ref=self._half(i, outs[i].at[slot]),
                                                send_sem=send1.at[3 * i + k], recv_sem=recv1.at[3 * i + k],
                                                device_id=peers[k], device_id_type=MESH)

        def share(i, k, sibling):
            part = self._half(i, outs[i].at[peer_chip[k]], sibling)
            return pltpu.make_async_remote_copy(src_ref=part, dst_ref=part, send_sem=send2.at[3 * i + k],
                                                recv_sem=recv2.at[3 * i + k], device_id=(x, y, 1 - c),
                                                device_id_type=MESH)

        split_pairs = [(i, k) for i, k in pairs if self.split[i]]
        return dict(
            local=lambda: [pltpu.make_async_copy(ins[i], outs[i].at[me], local_sems.at[i]) for i in range(n)],
            fetch_out=lambda: [fetch(i, k, me) for i, k in pairs],
            fetch_in=lambda: [(fetch(i, k, peer_chip[k]), share(i, k, False) if self.split[i] else None)
                              for i, k in pairs],
            share_out=lambda: [share(i, k, False) for i, k in split_pairs],
            share_in=lambda: [share(i, k, True) for i, k in split_pairs])

    def start(self, ins, outs, sems):
        plan = self._plan(ins, outs, sems)
        for cp in plan["local"]() + plan["fetch_out"]():
            cp.start()

    def mid(self, ins, outs, sems):
        plan = self._plan(ins, outs, sems)
        for arrived, forward in plan["fetch_in"]():
            arrived.wait_recv()
            if forward is not None:
                forward.start()

    def wait(self, ins, outs, sems):
        plan = self._plan(ins, outs, sems)
        for cp in plan["share_in"]():
            cp.wait_recv()
        for cp in plan["fetch_out"]() + plan["share_out"]():
            cp.wait_send()
        for cp in plan["local"]():
            cp.wait()


class _Scatter(_Exchange):
    def __init__(self, slabs, whole=()):
        self.n_slabs = len(slabs)
        self.ins = list(slabs) + list(whole)
        n = len(self.ins)
        self.out_shape = [jax.ShapeDtypeStruct(a.shape, a.dtype) for a in slabs] \
            + [jax.ShapeDtypeStruct((N_CHIPS,) + a.shape, a.dtype) for a in whole]
        self.sems = [pltpu.SemaphoreType.DMA((3 * n,)), pltpu.SemaphoreType.DMA((3 * n,)), pltpu.SemaphoreType.DMA((n,))]

    def _copies(self, ins, outs, sems):
        send_sems, recv_sems, local_sems = sems
        me, peers, peer_chip = _chip_peers()
        n = len(ins)

        def src(i, chip):
            return ins[i].at[chip] if i < self.n_slabs else ins[i]

        def remote(i, k, src_chip, slot):
            return pltpu.make_async_remote_copy(src_ref=src(i, src_chip), dst_ref=outs[i].at[slot],
                                                send_sem=send_sems.at[3 * i + k], recv_sem=recv_sems.at[3 * i + k],
                                                device_id=peers[k], device_id_type=MESH)

        local = [pltpu.make_async_copy(src(i, me), outs[i].at[me], local_sems.at[i]) for i in range(n)]
        sends = [remote(i, k, peer_chip[k], me) for i in range(n) for k in range(3)]
        return local, sends, lambda: [remote(i, k, me, peer_chip[k]) for i in range(n) for k in range(3)]


class _Swap(_Exchange):
    def __init__(self, arrays):
        n = len(arrays)
        self.ins = list(arrays)
        self.out_shape = [jax.ShapeDtypeStruct(a.shape, a.dtype) for a in arrays]
        self.sems = [pltpu.SemaphoreType.DMA((n,)), pltpu.SemaphoreType.DMA((n,))]

    def _copies(self, ins, outs, sems):
        send_sems, recv_sems = sems
        x, y, c = lax.axis_index("x"), lax.axis_index("y"), lax.axis_index("c")
        cps = [pltpu.make_async_remote_copy(src_ref=ins[i], dst_ref=outs[i], send_sem=send_sems.at[i],
                                            recv_sem=recv_sems.at[i], device_id=(x, y, 1 - c), device_id_type=MESH)
               for i in range(len(ins))]
        return [], cps, lambda: cps


def _riding(body, n_in, n_out, n_scratch, ride, rank):
    if not ride:
        return body
    r_in = sum(len(e.ins) for e in ride)
    r_out = sum(len(e.out_shape) for e in ride)

    def split(refs, sizes):
        out, a = [], 0
        for sz in sizes:
            out.append(refs[a:a + sz])
            a += sz
        return out

    def wrapped(*refs):
        a = 0
        parts = []
        for sz in (n_in, r_in, n_out, r_out, n_scratch):
            parts.append(refs[a:a + sz])
            a += sz
        own_in, ex_in, own_out, ex_out, own_scratch = parts
        ex_sems = refs[a:]
        ins = split(ex_in, [len(e.ins) for e in ride])
        outs = split(ex_out, [len(e.out_shape) for e in ride])
        sems = split(ex_sems, [len(e.sems) for e in ride])
        if rank:
            step, total = 0, 1
            for d in range(rank):
                step = step * pl.num_programs(d) + pl.program_id(d)
                total = total * pl.num_programs(d)

            @pl.when(step == 0)
            def _():
                for e, i, o, s in zip(ride, ins, outs, sems):
                    e.start(i, o, s)

            body(*own_in, *own_out, *own_scratch)

            @pl.when(step == (total * _MID_NUM) // _MID_DEN)
            def _():
                for e, i, o, s in zip(ride, ins, outs, sems):
                    e.mid(i, o, s)

            @pl.when(step == total - 1)
            def _():
                for e, i, o, s in zip(ride, ins, outs, sems):
                    e.wait(i, o, s)
        else:
            for phase in ("start", "mid", "wait"):
                for e, i, o, s in zip(ride, ins, outs, sems):
                    getattr(e, phase)(i, o, s)

    return wrapped


def _ride_args(ride):
    ins = [a for e in ride for a in e.ins]
    outs = [s for e in ride for s in e.out_shape]
    sems = [s for e in ride for s in e.sems]
    return ins, _any_specs(len(ins)), outs, _any_specs(len(outs)), sems


def _ride_results(ride, flat):
    out, a = [], 0
    for e in ride:
        out.append(list(flat[a:a + len(e.out_shape)]))
        a += len(e.out_shape)
    return out


def exchange(ride, name):
    ins, in_specs, outs, out_specs, sems = _ride_args(ride)
    res = pl.pallas_call(
        _riding(lambda: None, 0, 0, 0, ride, 0), name=name,
        in_specs=in_specs, out_specs=out_specs, out_shape=outs, scratch_shapes=sems,
    )(*ins)
    return _ride_results(ride, res)


def _sum_slots(ref):
    acc = ref[0].astype(F32)
    for j in range(1, ref.shape[0]):
        acc = acc + ref[j].astype(F32)
    return acc


def sum_pairs(mine, other, name, tr=176):
    n, r, w = mine.shape
    tr = r if r <= tr else _tile(r, tr)

    def body(a_ref, b_ref, o_ref):
        o_ref[...] = _sum_slots(a_ref) + _sum_slots(b_ref)

    spec = pl.BlockSpec((n, tr, w), lambda i: (0, i, 0))
    return pl.pallas_call(
        body, name=name, grid=(r // tr,),
        in_specs=[spec, spec], out_specs=_rows(tr, w),
        out_shape=jax.ShapeDtypeStruct((r, w), F32),
        compiler_params=_params(("parallel",)),
    )(mine, other)


def adamw(w, m, v, gs, name, tr=256):
    r, c = w.shape
    tr = r if r % 8 else _tile(r, tr)
    c1 = 1.0 - ADAM_B1 ** ADAM_STEP
    c2 = 1.0 - ADAM_B2 ** ADAM_STEP
    ng = len(gs)

    def body(w_ref, m_ref, v_ref, *refs):
        g_refs, (g_ref, d_ref, nm_ref, nv_ref) = refs[:ng], refs[ng:]
        g = g_refs[0][...] if ng == 1 else _sum_slots(g_refs[0]) + _sum_slots(g_refs[1])
        nm = ADAM_B1 * m_ref[...] + (1.0 - ADAM_B1) * g
        nv = ADAM_B2 * v_ref[...] + (1.0 - ADAM_B2) * (g * g)
        g_ref[...] = g
        nm_ref[...] = nm
        nv_ref[...] = nv
        d_ref[...] = (-ADAM_LR) * ((nm / c1) / (jnp.sqrt(nv / c2) + ADAM_EPS) + ADAM_WD * w_ref[...])

    spec = _rows(tr, c)
    gspec = spec if ng == 1 else pl.BlockSpec((N_CHIPS, tr, c), lambda i: (0, i, 0))
    return pl.pallas_call(
        body, name=name, grid=(r // tr,),
        in_specs=[spec] * 3 + [gspec] * ng, out_specs=[spec] * 4,
        out_shape=[jax.ShapeDtypeStruct((r, c), F32)] * 4,
        compiler_params=_params(("parallel",)),
    )(w, m, v, *gs)


_PARAMS = (
    ("rel_bias", None), ("norm_mix_pre", None), ("norm_mix_post", None), ("w_in", 1), ("conv_rnn_w", 1),
    ("conv_rnn_b", None), ("w_rg_a", None), ("b_rg_a", None), ("w_rg_x", None), ("b_rg_x", None),
    ("lru_lambda", None), ("w_branch_rnn", 0), ("w_branch_att", 1), ("w_out", 0), ("norm_ffn_pre", None),
    ("norm_ffn_post", None), ("w_ffn_gate", 1), ("w_ffn_up", 1), ("conv_ffn_w", 1), ("conv_ffn_b", None),
    ("w_ffn_down", 0),
)
_SMALL = 65536


def _as2d(a):
    a = a[0] if a.shape[0] == 1 and a.ndim >= 3 else a
    return a.reshape(-1, a.shape[-1]) if a.ndim == 3 else a


def _pack(pieces, dtype):
    flat = jnp.concatenate([p.astype(dtype).reshape(-1) for p in pieces])
    unit = PACK_W * PACK_ROWS
    pad = (-flat.shape[0]) % unit
    flat = jnp.pad(flat, (0, pad))
    return flat.reshape(-1, PACK_W)


def _unpack(buf, shapes):
    flat = buf.reshape(-1)
    out, off = [], 0
    for shp in shapes:
        n = int(np.prod(shp))
        out.append(flat[off:off + n].reshape(shp))
        off += n
    return out


def _join(slots, ax):
    if ax == 0:
        return slots.reshape(-1, slots.shape[-1])
    return jnp.transpose(slots, (1, 0, 2)).reshape(slots.shape[1], -1)


def _cut(full, ax):
    if ax == 0:
        return full.reshape(N_CHIPS, -1, full.shape[-1])
    return jnp.transpose(full.reshape(full.shape[0], N_CHIPS, -1), (1, 0, 2))


def kernel(x, rel_bias, norm_mix_pre, norm_mix_post, w_in, conv_rnn_w, conv_rnn_b, w_rg_a, b_rg_a, w_rg_x, b_rg_x, lru_lambda, w_branch_rnn, w_branch_att, w_out, norm_ffn_pre, norm_ffn_post, w_ffn_gate, w_ffn_up, conv_ffn_w, conv_ffn_b, w_ffn_down, loss_target, m_rel_bias, m_norm_mix_pre, m_norm_mix_post, m_w_in, m_conv_rnn_w, m_conv_rnn_b, m_w_rg_a, m_b_rg_a, m_w_rg_x, m_b_rg_x, m_lru_lambda, m_w_branch_rnn, m_w_branch_att, m_w_out, m_norm_ffn_pre, m_norm_ffn_post, m_w_ffn_gate, m_w_ffn_up, m_conv_ffn_w, m_conv_ffn_b, m_w_ffn_down, v_rel_bias, v_norm_mix_pre, v_norm_mix_post, v_w_in, v_conv_rnn_w, v_conv_rnn_b, v_w_rg_a, v_b_rg_a, v_w_rg_x, v_b_rg_x, v_lru_lambda, v_w_branch_rnn, v_w_branch_att, v_w_out, v_norm_ffn_pre, v_norm_ffn_post, v_w_ffn_gate, v_w_ffn_up, v_conv_ffn_w, v_conv_ffn_b, v_w_ffn_down):
    args = dict(locals())
    names = [n for n, _ in _PARAMS]
    axis = dict(_PARAMS)
    w_loc = {n: args[n] for n in names}
    m_loc = {n: args["m_" + n] for n in names}
    v_loc = {n: args["v_" + n] for n in names}
    sharded = [n for n in names if axis[n] is not None]
    replicated = [n for n in names if axis[n] is None]

    big = [n for n in sharded if w_loc[n].size >= _SMALL]
    small_sharded = [n for n in sharded if n not in big]
    small = replicated + small_sharded

    first = ["w_in"] + small_sharded
    srcs = [_as2d(w_loc[n]).astype(BF16) if n in big else _as2d(w_loc[n]) for n in first]
    (gathered,) = exchange([_Gather(srcs)], "gather_first")
    p = {n: _join(a, axis[n]) for n, a in zip(first, gathered)}
    for n in replicated:
        p[n] = _as2d(w_loc[n])
    shards = {n: _as2d(w_loc[n]).astype(BF16) for n in big if n not in first}

    last = "norm_mix_pre"
    early = [n for n in small if n != last]
    received, sibling, g_small, loss_part = _local_step(x, loss_target, p, shards, early)

    ((received["last"],),) = exchange([_Scatter([], [_pack([g_small[last]], BF16)])], "scatter_last")
    late = [n for n in received if n not in sibling]
    (swapped,) = exchange([_Swap([received[n] for n in late])], "swap_last")
    sibling.update(zip(late, swapped))
    early_sum = sum_pairs(received["small"], sibling["small"], "sum_small")
    last_sum = sum_pairs(received["last"], sibling["last"], "sum_last")
    g_tot = dict(zip(early, _unpack(early_sum, [g_small[n].shape for n in early])))
    (g_tot[last],) = _unpack(last_sum, [g_small[last].shape])
    chip = 2 * lax.axis_index("x") + lax.axis_index("y")
    for n in small_sharded:
        size = g_tot[n].shape[axis[n]] // N_CHIPS
        g_tot[n] = lax.dynamic_slice_in_dim(g_tot[n], chip * size, size, axis=axis[n])

    out_g, out_d, out_m, out_v = {}, {}, {}, {}
    for i, n in enumerate(names):
        shp = w_loc[n].shape
        gs = (received[n], sibling[n]) if n in big else (g_tot[n],)
        g, d, nm, nv = adamw(_as2d(w_loc[n]), _as2d(m_loc[n]), _as2d(v_loc[n]), gs, "adamw_" + n)
        out_g[n], out_d[n], out_m[n], out_v[n] = (t.reshape(shp) for t in (g, d, nm, nv))

    d_model = x.shape[-1]
    loss = lax.psum(0.5 * jnp.sum(loss_part) / d_model, ("x", "y", "c"))
    grad_x = g_small["x"]
    return (loss, grad_x, *[out_g[n] for n in names], *[out_d[n] for n in names],
            *[out_m[n] for n in names], *[out_v[n] for n in names])


def _local_step(x, target, p, shards, small_early):
    axis = dict(_PARAMS)
    b, s, d = x.shape
    t = b * s
    rnn = p["b_rg_a"].shape[1]
    ffn = p["conv_ffn_b"].shape[1]
    nbk = rnn // p["w_rg_a"].shape[1]
    hkv = (p["w_in"].shape[1] - rnn - 2 * d) // (N_GROUPS + 2)
    h = hkv // HEAD_DIM
    nq = N_GROUPS * hkv

    x2 = x.reshape(t, d)
    tgt = target.reshape(t, d)
    w_in = p["w_in"]
    in_splits = (rnn, nq + 2 * hkv, 2 * d)
    wa = p["w_rg_a"].reshape(nbk, -1, p["w_rg_a"].shape[1]).astype(BF16)
    wx = p["w_rg_x"].reshape(nbk, -1, p["w_rg_x"].shape[1]).astype(BF16)
    cw_r, cb_r = p["conv_rnn_w"], p["conv_rnn_b"]
    cw_f, cb_f = p["conv_ffn_w"], p["conv_ffn_b"]

    masks, buckets = zip(*[_band(w_, r_) for w_, r_ in DILATED])
    bucket_f = jnp.asarray(np.where(np.stack(masks), np.stack(buckets), -1).astype(np.float32))
    rel_rows = jnp.pad(p["rel_bias"].T, ((0, 0), (0, 128 - REL_BUCKETS)))[:, None, :]
    biasm = bias_table(rel_rows, bucket_f, h, "bias_table")

    early = ["w_branch_rnn", "w_branch_att", "w_out"]
    hn1, (xr, qkv, gts), (got,) = norm_mm(x2, p["norm_mix_pre"], [w_in], [in_splits], "in_proj",
                                          ride=[_Gather([shards[n] for n in early])])
    p.update({n: _join(a, axis[n]) for n, a in zip(early, got)})
    xr3 = xr.reshape(b, s, rnn)
    (y_rnn, a_rnn, xc_rnn), (got,) = rglru_fwd(xr3, cw_r, cb_r, wa, p["b_rg_a"], wx, p["b_rg_x"], p["lru_lambda"], "rglru_fwd",
                              ride=[_Gather([shards[n] for n in ("w_ffn_gate", "w_ffn_up")])])
    p.update({n: _join(a, axis[n]) for n, a in zip(("w_ffn_gate", "w_ffn_up"), got)})
    qkv3 = qkv.reshape(b, s, -1)
    o_att, lse, ((got,),) = attn_fwd(qkv3, biasm, h, "attn_fwd", ride=[_Gather([shards["w_ffn_down"]])])
    p["w_ffn_down"] = _join(got, axis["w_ffn_down"])
    merged, br, ba, mix, h1 = merge_out(y_rnn.reshape(t, rnn), o_att.reshape(t, hkv), gts, p["w_branch_rnn"],
                                        p["w_branch_att"], p["w_out"], p["norm_mix_post"], x2, "merge_out")
    hn2, gate_pre, up, act = ffn_in_act(h1, p["norm_ffn_pre"], p["w_ffn_gate"], p["w_ffn_up"], cw_f, cb_f, s, "ffn_in")

    g, gb = {}, {}
    recv, sib = {}, {}

    def rows4(a):
        return a.reshape(N_CHIPS, -1, a.shape[-1])

    dy, dff, dact, g["norm_ffn_post"], loss_part = ffn_down_loss(act, p["w_ffn_down"], p["norm_ffn_post"], h1, tgt,
                                                                  "ffn_down")
    gb["w_ffn_down"] = rows4(mm_tn(act, [dff], "ffn_down_dw"))
    (dgp, dup, dhn2, g["conv_ffn_w"], g["conv_ffn_b"]), ((recv["w_ffn_down"],),) = ffn_in_bwd(
        dact, gate_pre, up, cw_f, cb_f, p["w_ffn_gate"], p["w_ffn_up"], s, "ffn_in_bwd",
        ride=[_Scatter([gb["w_ffn_down"]])])
    gb["w_ffn_gate"] = mm_tn(hn2, [dgp], "ffn_gate_dw", col_shards=N_CHIPS)
    gb["w_ffn_up"] = mm_tn(hn2, [dup], "ffn_up_dw", col_shards=N_CHIPS)
    (dh1, dgts, dy_rnn, do_att, g["norm_ffn_pre"], g["norm_mix_post"], dw_out, dw_br,
     gb["w_branch_att"]) = mid_bwd(dhn2, h1, p["norm_ffn_pre"], dy, mix, p["norm_mix_post"], p["w_out"], gts, br, ba,
                                   p["w_branch_rnn"], p["w_branch_att"], merged, y_rnn.reshape(t, rnn),
                                   o_att.reshape(t, hkv), "mid_bwd")
    gb["w_out"], gb["w_branch_rnn"] = rows4(dw_out), rows4(dw_br)
    ffn_in = ["w_ffn_gate", "w_ffn_up"]
    (dxr, g["conv_rnn_w"], g["conv_rnn_b"], dwa, g["b_rg_a"], dwx, g["b_rg_x"], g["lru_lambda"]), (got,) = rglru_bwd(
        xr3, y_rnn, dy_rnn.reshape(b, s, rnn), a_rnn, xc_rnn, cw_r, wa, p["b_rg_a"], wx, p["b_rg_x"], p["lru_lambda"], "rglru_bwd",
        ride=[_Scatter([gb[n] for n in ffn_in])])
    recv.update(zip(ffn_in, got))
    g["w_rg_a"] = dwa.reshape(p["w_rg_a"].shape)
    g["w_rg_x"] = dwx.reshape(p["w_rg_x"].shape)
    mid = ["w_out", "w_branch_rnn", "w_branch_att"]
    early_recv = ["w_ffn_down"] + ffn_in
    (dq1, dq2, dq3, dk, dv, ds_sum), (got, swapped) = attn_bwd(
        qkv3, biasm, o_att, lse, do_att.reshape(b, s, hkv), h, "attn_bwd",
        ride=[_Scatter([gb[n] for n in mid]), _Swap([recv[n] for n in early_recv])])
    recv.update(zip(mid, got))
    sib.update(zip(early_recv, swapped))
    rows = bias_grad(ds_sum, bucket_f, "bias_grad")
    g["rel_bias"] = rows[:, 0, :REL_BUCKETS].T
    dproj = [dxr.reshape(t, rnn)] + [a.reshape(t, hkv) for a in (dq1, dq2, dq3, dk, dv)] + [dgts]
    dw_a, (got,) = mm_tn(hn1, dproj[:4], "in_proj_dw_a", ride=[_Swap([recv[n] for n in mid])])
    sib.update(zip(mid, got))
    pack = _pack([g[n] for n in small_early], BF16)
    dw_b, ((recv["small"],),) = mm_tn(hn1, dproj[4:], "in_proj_dw_b", ride=[_Scatter([], [pack])])
    gb["w_in"] = _cut(jnp.concatenate([dw_a[0], dw_b[0]], axis=1), 1)
    dx, g["norm_mix_pre"], ((recv["w_in"],),) = mm_nt(
        [(dproj, w_in)], "in_proj_dx", norm=(x2, p["norm_mix_pre"], dh1), ride=[_Scatter([gb["w_in"]])])
    g["x"] = dx.reshape(b, s, d)
    return recv, sib, g, loss_part
```

```python
import math

import numpy as np
import jax
import jax.numpy as jnp
from jax import lax
from jax.experimental import pallas as pl
from jax.experimental.pallas import tpu as pltpu

F32 = jnp.float32
BF16 = jnp.bfloat16

EPS = 1e-6
HEAD_DIM = 128
ATTN_BLOCK = 128
DILATED = ((128, 1), (512, 4), (2048, 16))
N_GROUPS = len(DILATED)
REL_BUCKETS = 32
REL_MAX_DIST = 2048
LRU_C = 8.0
NEG = -1e30

ADAM_LR = 0.001
ADAM_B1 = 0.9
ADAM_B2 = 0.999
ADAM_EPS = 1e-08
ADAM_WD = 0.01
ADAM_STEP = 10

N_CHIPS = 4
PACK_W = 1024
PACK_ROWS = 16
VMEM_LIMIT = 56 * 1024 * 1024
MESH = pl.DeviceIdType.MESH


def _params(sem=None):
    return pltpu.CompilerParams(dimension_semantics=sem, vmem_limit_bytes=VMEM_LIMIT)


def _dot(a, b):
    return jnp.dot(a, b, preferred_element_type=F32)


def _dot_nt(a, b):
    return lax.dot_general(a, b, (((1,), (1,)), ((), ())), preferred_element_type=F32)


def _dot_tn(a, b):
    return lax.dot_general(a, b, (((0,), (0,)), ((), ())), preferred_element_type=F32)


def _sig(x):
    return 0.5 * jnp.tanh(0.5 * x) + 0.5


def _rows(tm, w):
    return pl.BlockSpec((tm, w), lambda i: (i, 0))


def _whole(shape):
    nd = len(shape)
    return pl.BlockSpec(tuple(shape), lambda *_: (0,) * nd)


def _resident(shape):
    nd = len(shape)
    return pl.BlockSpec(tuple(shape), lambda *_: (0,) * nd, pipeline_mode=pl.Buffered(1))


def _tile(t, want):
    while t % want:
        want //= 2
    return want


def norm_mm(x, g, ws, splits, name, ride=(), tm=512):
    t, d = x.shape
    tm = _tile(t, tm)
    nw = len(ws)
    widths = [n for sp in splits for n in sp]

    def body(x_ref, g_ref, *refs):
        w_refs, hn_ref, o_refs = refs[:nw], refs[nw], refs[nw + 1:]
        xv = x_ref[...]
        inv = lax.rsqrt(jnp.mean(xv * xv, axis=-1, keepdims=True) + EPS)
        hn = (xv * inv * g_ref[...]).astype(BF16)
        hn_ref[...] = hn
        o = 0
        for w_ref, sp in zip(w_refs, splits):
            off = 0
            for n in sp:
                o_refs[o][...] = _dot(hn, w_ref[:, off:off + n])
                off += n
                o += 1

    r_ins, r_in_specs, r_outs, r_out_specs, r_sems = _ride_args(ride)
    n_out = 1 + len(widths)
    outs = pl.pallas_call(
        _riding(body, 2 + nw, n_out, 0, ride, 1), name=name, grid=(t // tm,),
        in_specs=[_rows(tm, d), _whole(g.shape)] + [_resident(w.shape) for w in ws] + r_in_specs,
        out_specs=[_rows(tm, d)] + [_rows(tm, n) for n in widths] + r_out_specs,
        out_shape=[jax.ShapeDtypeStruct((t, d), BF16)] + [jax.ShapeDtypeStruct((t, n), F32) for n in widths] + r_outs,
        scratch_shapes=r_sems,
        compiler_params=_params(("arbitrary",)),
    )(x, g, *ws, *r_ins)
    return outs[0], outs[1:n_out], _ride_results(ride, outs[n_out:])


def mm_nt(groups, name, ride=(), norm=None, tm=512):
    dys_all = [dy for dys, _ in groups for dy in dys]
    ws = [w for _, w in groups]
    t = dys_all[0].shape[0]
    k = ws[0].shape[0]
    tm = _tile(t, tm)
    n = len(dys_all)
    extra = list(norm) if norm else []

    def body(*refs):
        dy_refs, w_refs = refs[:n], refs[n:n + len(ws)]
        rest = refs[n + len(ws):]
        acc = None
        i = 0
        for (dys, _), w_ref in zip(groups, w_refs):
            off = 0
            for dy in dys:
                width = dy.shape[1]
                part = _dot_nt(dy_refs[i][...].astype(BF16), w_ref[:, off:off + width])
                acc = part if acc is None else acc + part
                off += width
                i += 1
        if norm:
            u_ref, g_ref, add_ref, o_ref, dg_ref = rest

            @pl.when(pl.program_id(0) == 0)
            def _():
                dg_ref[...] = jnp.zeros(dg_ref.shape, F32)

            du, dg_rows = _rms_bwd(acc, u_ref[...], g_ref[...])
            o_ref[...] = du + add_ref[...]
            dg_ref[...] += jnp.sum(dg_rows, axis=0, keepdims=True)
        else:
            rest[0][...] = acc

    n_out = 2 if norm else 1
    r_ins, r_in_specs, r_outs, r_out_specs, r_sems = _ride_args(ride)
    outs = pl.pallas_call(
        _riding(body, n + len(ws) + len(extra), n_out, 0, ride, 1), name=name, grid=(t // tm,),
        in_specs=[_rows(tm, dy.shape[1]) for dy in dys_all] + [_resident(w.shape) for w in ws]
        + ([_rows(tm, k), _whole((1, k)), _rows(tm, k)] if norm else []) + r_in_specs,
        out_specs=[_rows(tm, k)] + ([_whole((1, k))] if norm else []) + r_out_specs,
        out_shape=[jax.ShapeDtypeStruct((t, k), F32)] + ([jax.ShapeDtypeStruct((1, k), F32)] if norm else []) + r_outs,
        scratch_shapes=r_sems,
        compiler_params=_params(("arbitrary",)),
    )(*dys_all, *ws, *extra, *r_ins)
    return tuple(outs[:n_out]) + (_ride_results(ride, outs[n_out:]),)


def mm_tn(a, dys, name, col_shards=1, ride=(), tm=1024):
    t, k = a.shape
    tm = _tile(t, tm)
    n = len(dys)
    ntot = sum(dy.shape[1] for dy in dys)
    wsh = ntot // col_shards

    def body(a_ref, *refs):
        dy_refs, o_ref, acc = refs[:n], refs[n], refs[n + 1]

        @pl.when(pl.program_id(0) == 0)
        def _():
            acc[...] = jnp.zeros(acc.shape, F32)

        av = a_ref[...].astype(BF16)
        off = 0
        for dy_ref in dy_refs:
            width = dy_ref.shape[1]
            acc[:, off:off + width] += _dot_tn(av, dy_ref[...].astype(BF16))
            off += width

        @pl.when(pl.program_id(0) == pl.num_programs(0) - 1)
        def _():
            for j in range(col_shards):
                o_ref[j] = acc[:, j * wsh:(j + 1) * wsh].astype(o_ref.dtype)

    r_ins, r_in_specs, r_outs, r_out_specs, r_sems = _ride_args(ride)
    outs = pl.pallas_call(
        _riding(body, 1 + n, 1, 1, ride, 1), name=name, grid=(t // tm,),
        in_specs=[_rows(tm, k)] + [_rows(tm, dy.shape[1]) for dy in dys] + r_in_specs,
        out_specs=[_whole((col_shards, k, wsh))] + r_out_specs,
        out_shape=[jax.ShapeDtypeStruct((col_shards, k, wsh), BF16)] + r_outs,
        scratch_shapes=[pltpu.VMEM((k, ntot), F32)] + r_sems,
        compiler_params=_params(("arbitrary",)),
    )(a, *dys, *r_ins)
    return (outs[0], _ride_results(ride, outs[1:])) if ride else outs[0]


def _rms_bwd(dz, u, g):
    d = u.shape[-1]
    inv = lax.rsqrt(jnp.mean(u * u, axis=-1, keepdims=True) + EPS)
    dzg = dz * g
    proj = jnp.sum(dzg * u, axis=-1, keepdims=True) * (1.0 / d)
    du = inv * (dzg - u * (inv * inv) * proj)
    dg_rows = dz * u * inv
    return du, dg_rows


def ffn_down_loss(act, wd, g, h1, target, name, tm=512):
    t, f = act.shape
    d = wd.shape[1]
    tm = _tile(t, tm)

    def body(a_ref, w_ref, g_ref, h_ref, t_ref, dy_ref, dff_ref, dact_ref, dg_ref, loss_ref):
        @pl.when(pl.program_id(0) == 0)
        def _():
            dg_ref[...] = jnp.zeros(dg_ref.shape, F32)
            loss_ref[...] = jnp.zeros(loss_ref.shape, F32)

        wv = w_ref[...]
        gv = g_ref[...]
        ff = _dot(a_ref[...], wv)
        inv = lax.rsqrt(jnp.mean(ff * ff, axis=-1, keepdims=True) + EPS)
        err = h_ref[...] + ff * inv * gv - t_ref[...]
        loss_ref[...] += jnp.sum(err * err, axis=0, keepdims=True)
        dy = err * (1.0 / d)
        dy_ref[...] = dy
        du, dg_rows = _rms_bwd(dy, ff, gv)
        dff = du.astype(BF16)
        dff_ref[...] = dff
        dg_ref[...] += jnp.sum(dg_rows, axis=0, keepdims=True)
        dact_ref[...] = _dot_nt(dff, wv)

    return pl.pallas_call(
        body, name=name, grid=(t // tm,),
        in_specs=[_rows(tm, f), _resident(wd.shape), _whole(g.shape), _rows(tm, d), _rows(tm, d)],
        out_specs=[_rows(tm, d), _rows(tm, d), _rows(tm, f), _whole((1, d)), _whole((1, d))],
        out_shape=[jax.ShapeDtypeStruct((t, d), F32), jax.ShapeDtypeStruct((t, d), BF16),
                   jax.ShapeDtypeStruct((t, f), F32), jax.ShapeDtypeStruct((1, d), F32),
                   jax.ShapeDtypeStruct((1, d), F32)],
        compiler_params=_params(("arbitrary",)),
    )(act, wd, g, h1, target)


def merge_out(y_rnn, o_att, gts, w_br, w_ba, w_out, g, x, name, tm=512):
    t = y_rnn.shape[0]
    d = w_br.shape[1]
    tm = _tile(t, tm)

    def body(y_ref, o_ref, g_ref, wbr_ref, wba_ref, wo_ref, gn_ref, x_ref, m_ref, br_ref, ba_ref, mix_ref, h_ref):
        br = _dot(y_ref[...].astype(BF16), wbr_ref[...])
        ba = _dot(o_ref[...].astype(BF16), wba_ref[...])
        gv = g_ref[...]
        merged = (_sig(gv[:, :d]) * br + _sig(gv[:, d:]) * ba).astype(BF16)
        m_ref[...] = merged
        br_ref[...] = br
        ba_ref[...] = ba
        mix = _dot(merged, wo_ref[...])
        mix_ref[...] = mix
        inv = lax.rsqrt(jnp.mean(mix * mix, axis=-1, keepdims=True) + EPS)
        h_ref[...] = x_ref[...] + mix * inv * gn_ref[...]

    sd = jax.ShapeDtypeStruct
    return pl.pallas_call(
        body, name=name, grid=(t // tm,),
        in_specs=[_rows(tm, y_rnn.shape[1]), _rows(tm, o_att.shape[1]), _rows(tm, 2 * d),
                  _resident(w_br.shape), _resident(w_ba.shape), _resident(w_out.shape), _whole(g.shape), _rows(tm, d)],
        out_specs=[_rows(tm, d)] * 5,
        out_shape=[sd((t, d), BF16), sd((t, d), F32), sd((t, d), F32), sd((t, d), F32), sd((t, d), F32)],
        compiler_params=_params(("parallel",)),
    )(y_rnn, o_att, gts, w_br, w_ba, w_out, g, x)


def mid_bwd(dhn2, h1, g_ffn, dy, mix, g_mix, w_out, gts, br, ba, w_br, w_ba, merged, y_rnn, o_att, name, tm=256):
    t, d = h1.shape
    tm = _tile(t, tm)
    rnn, hkv = w_br.shape[0], w_ba.shape[0]
    wsh = d // N_CHIPS

    def body(dhn_ref, h_ref, gf_ref, dy_ref, mix_ref, gm_ref, wo_ref, g_ref, br_ref, ba_ref, wbr_ref, wba_ref,
             m_ref, y_ref, o_ref, dh_ref, dg_ref, dyr_ref, doa_ref, dgf_ref, dgm_ref, dwo_ref, dwbr_ref, dwba_ref,
             acc_o, acc_br, acc_ba):
        @pl.when(pl.program_id(0) == 0)
        def _():
            dgf_ref[...] = jnp.zeros(dgf_ref.shape, F32)
            dgm_ref[...] = jnp.zeros(dgm_ref.shape, F32)
            acc_o[...] = jnp.zeros(acc_o.shape, F32)
            acc_br[...] = jnp.zeros(acc_br.shape, F32)
            acc_ba[...] = jnp.zeros(acc_ba.shape, F32)

        du, rows_f = _rms_bwd(dhn_ref[...], h_ref[...], gf_ref[...])
        dh1 = du + dy_ref[...]
        dh_ref[...] = dh1
        dgf_ref[...] += jnp.sum(rows_f, axis=0, keepdims=True)
        dmx, rows_m = _rms_bwd(dh1, mix_ref[...], gm_ref[...])
        dmix = dmx.astype(BF16)
        acc_o[...] += _dot_tn(m_ref[...], dmix)
        dgm_ref[...] += jnp.sum(rows_m, axis=0, keepdims=True)
        dm = _dot_nt(dmix, wo_ref[...])
        gv = g_ref[...]
        sr = _sig(gv[:, :d])
        sa = _sig(gv[:, d:])
        dbr = (dm * sr).astype(BF16)
        dba = (dm * sa).astype(BF16)
        acc_br[...] += _dot_tn(y_ref[...].astype(BF16), dbr)
        acc_ba[...] += _dot_tn(o_ref[...].astype(BF16), dba)
        dg_ref[:, :d] = (dm * br_ref[...] * sr * (1.0 - sr)).astype(BF16)
        dg_ref[:, d:] = (dm * ba_ref[...] * sa * (1.0 - sa)).astype(BF16)
        dyr_ref[...] = _dot_nt(dbr, wbr_ref[...])
        doa_ref[...] = _dot_nt(dba, wba_ref[...])

        @pl.when(pl.program_id(0) == pl.num_programs(0) - 1)
        def _():
            dwo_ref[...] = acc_o[...].astype(BF16)
            dwbr_ref[...] = acc_br[...].astype(BF16)
            for j in range(N_CHIPS):
                dwba_ref[j] = acc_ba[:, j * wsh:(j + 1) * wsh].astype(BF16)

    sd = jax.ShapeDtypeStruct
    row, vec = _rows(tm, d), _whole((1, d))
    once = pl.Buffered(1)

    def resident(shape):
        return pl.BlockSpec(shape, lambda i: (0,) * len(shape), pipeline_mode=once)

    return pl.pallas_call(
        body, name=name, grid=(t // tm,),
        in_specs=[row, row, vec, row, row, vec, resident(w_out.shape), _rows(tm, 2 * d), row, row,
                  resident(w_br.shape), resident(w_ba.shape), row, _rows(tm, rnn), _rows(tm, hkv)],
        out_specs=[row, _rows(tm, 2 * d), _rows(tm, rnn), _rows(tm, hkv), vec, vec,
                   resident((d, d)), resident((rnn, d)), resident((N_CHIPS, hkv, wsh))],
        out_shape=[sd((t, d), F32), sd((t, 2 * d), BF16), sd((t, rnn), F32), sd((t, hkv), F32), sd((1, d), F32),
                   sd((1, d), F32), sd((d, d), BF16), sd((rnn, d), BF16), sd((N_CHIPS, hkv, wsh), BF16)],
        scratch_shapes=[pltpu.VMEM((d, d), F32), pltpu.VMEM((rnn, d), F32), pltpu.VMEM((hkv, d), F32)],
        compiler_params=_params(("arbitrary",)),
    )(dhn2, h1, g_ffn, dy, mix, g_mix, w_out, gts, br, ba, w_br, w_ba, merged, y_rnn, o_att)


def _shift_dn(x, d, fill, row):
    return jnp.where(row >= d, pltpu.roll(x, d, 0), fill)


def _shift_up(x, d, fill, row):
    s = x.shape[0]
    return jnp.where(row < s - d, pltpu.roll(x, s - d, 0), fill)


def _conv_fwd(x, w, b, row):
    kk = w.shape[0]
    y = b + w[kk - 1:kk, :] * x
    for j in range(1, kk):
        y = y + w[kk - 1 - j:kk - j, :] * _shift_dn(x, j, 0.0, row)
    return y


def _conv_bwd(dy, x, w, row):
    kk = w.shape[0]
    dx = w[kk - 1:kk, :] * dy
    dws = [None] * kk
    dws[kk - 1] = jnp.sum(dy * x, axis=0, keepdims=True)
    for j in range(1, kk):
        ahead = _shift_up(dy, j, 0.0, row)
        dx = dx + w[kk - 1 - j:kk - j, :] * ahead
        dws[kk - 1 - j] = jnp.sum(ahead * x, axis=0, keepdims=True)
    return dx, jnp.concatenate(dws, axis=0)


def _softplus(z):
    y = jnp.exp(-jnp.abs(z))
    u = 1.0 + y
    dd = u - 1.0
    log1p = jnp.where(dd == 0.0, y, jnp.log(u) * (y / jnp.where(dd == 0.0, 1.0, dd)))
    return jnp.maximum(z, 0.0) + log1p


def _lru_decay(xb, wa, ba, lam):
    r = _sig(_dot(xb, wa) + ba)
    sp = _softplus(-lam)
    la = (-LRU_C) * r * sp
    return r, sp, la, jnp.exp(la)


def _lru_gates(xc, wa, ba, wx, bx, lam):
    xb = xc.astype(BF16)
    r, sp, la, a = _lru_decay(xb, wa, ba, lam)
    i = _sig(_dot(xb, wx) + bx)
    one_m_a2 = jnp.tanh(-la) * (1.0 + a * a)
    inv_mult = lax.rsqrt(one_m_a2)
    return r, i, sp, a, one_m_a2 * inv_mult, inv_mult


def _seg_len(s):
    seg = -(-s // 8)
    return seg + (4 - seg % 8) % 8


def _scan_rows(a_pad, u_pad, out_pad, reverse):
    planes, rows8, lanes = a_pad.shape
    seg = rows8 // 8
    sub = lax.broadcasted_iota(jnp.int32, (planes, 8, lanes), 1)

    unroll = 4

    def rows(k, d):
        i = k * unroll + d
        return pl.ds((seg - 1 - i) if reverse else i, 8, stride=seg)

    def ends(k, carry):
        h, p = carry
        for d in range(unroll):
            a = a_pad[:, rows(k, d), :]
            h = a * h + u_pad[:, rows(k, d), :]
            p = a * p
        return h, p

    init = (jnp.zeros((planes, 8, lanes), F32), jnp.ones((planes, 8, lanes), F32))
    h_end, p_end = lax.fori_loop(0, seg // unroll, ends, init)
    start = jnp.zeros((planes, 8, lanes), F32)
    for _ in range(7):
        nxt = h_end + p_end * start
        if reverse:
            start = jnp.where(sub < 7, pltpu.roll(nxt, 7, 1), 0.0)
        else:
            start = jnp.where(sub >= 1, pltpu.roll(nxt, 1, 1), 0.0)

    def redo(k, h):
        for d in range(unroll):
            h = a_pad[:, rows(k, d), :] * h + u_pad[:, rows(k, d), :]
            out_pad[:, rows(k, d), :] = h
        return h

    lax.fori_loop(0, seg // unroll, redo, start)


def _lru_cols(c, rb):
    return 2 * rb if c % (2 * rb) == 0 else rb


def rglru_fwd(xr, cw, cb, wa, ba, wx, bx, lam, name, ride=()):
    b, s, c = xr.shape
    rb = wa.shape[1]
    kk = cw.shape[0]
    cols = _lru_cols(c, rb)
    nj = cols // rb
    seg = _seg_len(s)

    def body(x_ref, cw_ref, cb_ref, wa_ref, ba_ref, wx_ref, bx_ref, lam_ref, h_ref, a_ref, xc_ref, a_pad, u_pad, h_pad):
        row = lax.broadcasted_iota(jnp.int32, (s, rb), 0)
        for j in range(nj):
            cs = slice(j * rb, (j + 1) * rb)
            xc = _conv_fwd(x_ref[:, cs], cw_ref[:, cs], cb_ref[:, cs], row)
            _, i, _, a, mult, _ = _lru_gates(xc, wa_ref[j], ba_ref[:, cs], wx_ref[j], bx_ref[:, cs], lam_ref[:, cs])
            xc_ref[:, cs] = xc
            a_ref[:, cs] = a
            a_pad[j, 0:s, :] = a
            u_pad[j, 0:s, :] = mult * (i * xc)
        a_pad[:, s:, :] = jnp.ones((nj, 8 * seg - s, rb), F32)
        u_pad[:, s:, :] = jnp.zeros((nj, 8 * seg - s, rb), F32)
        _scan_rows(a_pad, u_pad, h_pad, False)
        for j in range(nj):
            h_ref[:, j * rb:(j + 1) * rb] = h_pad[j, 0:s, :]

    vec = pl.BlockSpec((1, cols), lambda bi, n: (0, n))
    seq = pl.BlockSpec((None, s, cols), lambda bi, n: (bi, 0, n))
    mat = pl.BlockSpec((nj, rb, rb), lambda bi, n: (n, 0, 0))
    r_ins, r_in_specs, r_outs, r_out_specs, r_sems = _ride_args(ride)
    outs = pl.pallas_call(
        _riding(body, 8, 3, 3, ride, 2), name=name, grid=(b, c // cols),
        in_specs=[seq, pl.BlockSpec((kk, cols), lambda bi, n: (0, n)), vec, mat, vec, mat, vec, vec] + r_in_specs,
        out_specs=[seq] * 3 + r_out_specs,
        out_shape=[jax.ShapeDtypeStruct((b, s, c), F32)] * 3 + r_outs,
        scratch_shapes=[pltpu.VMEM((nj, 8 * seg, rb), F32)] * 3 + r_sems,
        compiler_params=_params(("arbitrary", "arbitrary")),
    )(xr, cw, cb, wa, ba, wx, bx, lam, *r_ins)
    return outs[:3], _ride_results(ride, outs[3:])


def rglru_bwd(xr, h, dh, a_fwd, xc_fwd, cw, wa, ba, wx, bx, lam, name, ride=()):
    b, s, c = xr.shape
    nb, rb = wa.shape[0], wa.shape[1]
    kk = cw.shape[0]
    cols = _lru_cols(c, rb)
    nj = cols // rb
    seg = _seg_len(s)

    def body(x_ref, h_ref, dh_ref, a_ref, xc_ref, cw_ref, wa_ref, ba_ref, wx_ref, bx_ref, lam_ref,
             dx_ref, dcw_ref, dcb_ref, dwa_ref, dba_ref, dwx_ref, dbx_ref, dlam_ref, b_pad, g_pad, l_pad):
        @pl.when(pl.program_id(1) == 0)
        def _():
            for ref in (dcw_ref, dcb_ref, dwa_ref, dba_ref, dwx_ref, dbx_ref, dlam_ref):
                ref[...] = jnp.zeros(ref.shape, F32)

        row = lax.broadcasted_iota(jnp.int32, (s, rb), 0)

        for j in range(nj):
            b_pad[j, 0:s, :] = _shift_up(a_ref[:, j * rb:(j + 1) * rb], 1, 0.0, row)
            g_pad[j, 0:s, :] = dh_ref[:, j * rb:(j + 1) * rb]
        b_pad[:, s:, :] = jnp.zeros((nj, 8 * seg - s, rb), F32)
        g_pad[:, s:, :] = jnp.zeros((nj, 8 * seg - s, rb), F32)
        _scan_rows(b_pad, g_pad, l_pad, True)

        for j in range(nj):
            cs = slice(j * rb, (j + 1) * rb)
            x = x_ref[:, cs]
            cwv = cw_ref[:, cs]
            wav, wxv, lamv = wa_ref[j], wx_ref[j], lam_ref[:, cs]
            xc = xc_ref[:, cs]
            r, i, sp, a, mult, inv_mult = _lru_gates(xc, wav, ba_ref[:, cs], wxv, bx_ref[:, cs], lamv)
            lmb = l_pad[j, 0:s, :]
            h_prev = _shift_dn(h_ref[:, cs], 1, 0.0, row)
            da = lmb * h_prev
            ixc = i * xc
            dla = da * a - (lmb * ixc) * (a * a) * inv_mult
            di = lmb * mult * xc
            dxc = lmb * mult * i
            dr = dla * ((-LRU_C) * sp)
            dsp = jnp.sum(dla * ((-LRU_C) * r), axis=0, keepdims=True)
            dga = dr * r * (1.0 - r)
            dgx = di * i * (1.0 - i)
            dga_b, dgx_b = dga.astype(BF16), dgx.astype(BF16)
            xb = xc.astype(BF16)
            dwa_ref[j] += _dot_tn(xb, dga_b)
            dwx_ref[j] += _dot_tn(xb, dgx_b)
            dba_ref[:, cs] += jnp.sum(dga, axis=0, keepdims=True)
            dbx_ref[:, cs] += jnp.sum(dgx, axis=0, keepdims=True)
            dlam_ref[:, cs] += dsp * (-_sig(-lamv))
            dxc = dxc + _dot_nt(dga_b, wav) + _dot_nt(dgx_b, wxv)
            dcb_ref[:, cs] += jnp.sum(dxc, axis=0, keepdims=True)
            dx, dcw = _conv_bwd(dxc, x, cwv, row)
            dcw_ref[:, cs] += dcw
            dx_ref[:, cs] = dx.astype(dx_ref.dtype)

    vec = pl.BlockSpec((1, cols), lambda n, bi: (0, n))
    seq = pl.BlockSpec((None, s, cols), lambda n, bi: (bi, 0, n))
    mat = pl.BlockSpec((nj, rb, rb), lambda n, bi: (n, 0, 0))
    cws = pl.BlockSpec((kk, cols), lambda n, bi: (0, n))
    sd = jax.ShapeDtypeStruct
    r_ins, r_in_specs, r_outs, r_out_specs, r_sems = _ride_args(ride)
    outs = pl.pallas_call(
        _riding(body, 11, 8, 3, ride, 2), name=name, grid=(c // cols, b),
        in_specs=[seq, seq, seq, seq, seq, cws, mat, vec, mat, vec, vec] + r_in_specs,
        out_specs=[seq, cws, vec, mat, vec, mat, vec, vec] + r_out_specs,
        out_shape=[sd((b, s, c), BF16), sd((kk, c), F32), sd((1, c), F32), sd((nb, rb, rb), F32),
                   sd((1, c), F32), sd((nb, rb, rb), F32), sd((1, c), F32), sd((1, c), F32)] + r_outs,
        scratch_shapes=[pltpu.VMEM((nj, 8 * seg, rb), F32)] * 3 + r_sems,
        compiler_params=_params(("arbitrary", "arbitrary")),
    )(xr, h, dh, a_fwd, xc_fwd, cw, wa, ba, wx, bx, lam, *r_ins)
    return outs[:8], _ride_results(ride, outs[8:])


_GELU_C = math.sqrt(2.0 / math.pi)


def _gelu_parts(x):
    th = jnp.tanh(_GELU_C * (x + 0.044715 * x * x * x))
    gel = 0.5 * x * (1.0 + th)
    dgel = 0.5 * (1.0 + th) + 0.5 * x * (1.0 - th * th) * _GELU_C * (1.0 + 3 * 0.044715 * x * x)
    return gel, dgel


def ffn_in_act(x, g, wg, wu, cw, cb, seq_len, name, tm=256):
    t, d = x.shape
    f = wg.shape[1]
    kk = cw.shape[0]
    tm = _tile(seq_len, tm)
    tiles_per_seq = seq_len // tm
    keep = 8
    assert kk - 1 <= keep

    def body(x_ref, g_ref, wg_ref, wu_ref, cw_ref, cb_ref, hn_ref, gp_ref, up_ref, act_ref, tail):
        @pl.when(pl.program_id(0) % tiles_per_seq == 0)
        def _():
            tail[...] = jnp.zeros(tail.shape, F32)

        xv = x_ref[...]
        inv = lax.rsqrt(jnp.mean(xv * xv, axis=-1, keepdims=True) + EPS)
        hn = (xv * inv * g_ref[...]).astype(BF16)
        hn_ref[...] = hn
        gp = _dot(hn, wg_ref[...])
        up = _dot(hn, wu_ref[...])
        gp_ref[...] = gp
        up_ref[...] = up
        cwv = cw_ref[...]
        row = lax.broadcasted_iota(jnp.int32, (tm, 1), 0)
        gate = _conv_fwd(gp, cwv, cb_ref[...], row)
        row8 = lax.broadcasted_iota(jnp.int32, (keep, 1), 0)
        prev = tail[...]
        fix = jnp.zeros((keep, f), F32)
        for j in range(1, kk):
            fix = fix + cwv[kk - 1 - j:kk - j, :] * jnp.where(row8 < j, pltpu.roll(prev, j, 0), 0.0)
        gate = jnp.concatenate([gate[:keep] + fix, gate[keep:]], axis=0)
        tail[...] = gp[tm - keep:, :]
        gel, _ = _gelu_parts(gate)
        act_ref[...] = (gel * up).astype(BF16)

    sd = jax.ShapeDtypeStruct
    return pl.pallas_call(
        body, name=name, grid=(t // tm,),
        in_specs=[_rows(tm, d), _whole(g.shape), _whole(wg.shape), _whole(wu.shape), _whole(cw.shape), _whole(cb.shape)],
        out_specs=[_rows(tm, d), _rows(tm, f), _rows(tm, f), _rows(tm, f)],
        out_shape=[sd((t, d), BF16), sd((t, f), F32), sd((t, f), F32), sd((t, f), BF16)],
        scratch_shapes=[pltpu.VMEM((keep, f), F32)],
        compiler_params=_params(("arbitrary",)),
    )(x, g, wg, wu, cw, cb)


def ffn_in_bwd(dact, gate_pre, up, cw, cb, wg, wu, seq_len, name, ride=(), tm=256):
    t, f = gate_pre.shape
    d = wg.shape[0]
    kk = cw.shape[0]
    tm = _tile(seq_len, tm)
    nt = t // tm
    tiles_per_seq = seq_len // tm
    keep = 8
    assert kk - 1 <= keep

    def body(da_ref, g_ref, halo_ref, u_ref, cw_ref, cb_ref, wg_ref, wu_ref,
             dg_ref, du_ref, dhn_ref, dcw_ref, dcb_ref, nxt):
        tile = (nt - 1 - pl.program_id(0)) % tiles_per_seq

        @pl.when(pl.program_id(0) == 0)
        def _():
            dcw_ref[...] = jnp.zeros(dcw_ref.shape, F32)
            dcb_ref[...] = jnp.zeros(dcb_ref.shape, F32)

        @pl.when(tile == tiles_per_seq - 1)
        def _():
            nxt[...] = jnp.zeros(nxt.shape, F32)

        row = lax.broadcasted_iota(jnp.int32, (tm, 1), 0)
        row8 = lax.broadcasted_iota(jnp.int32, (keep, 1), 0)
        gp = g_ref[...]
        cwv = cw_ref[...]
        prev = jnp.where(tile > 0, halo_ref[...], 0.0)
        gate = _conv_fwd(gp, cwv, cb_ref[...], row)
        fix = jnp.zeros((keep, f), F32)
        for j in range(1, kk):
            fix = fix + cwv[kk - 1 - j:kk - j, :] * jnp.where(row8 < j, pltpu.roll(prev, j, 0), 0.0)
        gate = jnp.concatenate([gate[:keep] + fix, gate[keep:]], axis=0)
        gel, dgel = _gelu_parts(gate)
        da = da_ref[...]
        dup = (da * gel).astype(BF16)
        du_ref[...] = dup
        dgate = da * u_ref[...] * dgel
        dcb_ref[...] += jnp.sum(dgate, axis=0, keepdims=True)
        after = nxt[...]
        dgp = cwv[kk - 1:kk, :] * dgate
        tail_fix = jnp.zeros((keep, f), F32)
        dws = [None] * kk
        dws[kk - 1] = jnp.sum(dgate * gp, axis=0, keepdims=True)
        for j in range(1, kk):
            wj = cwv[kk - 1 - j:kk - j, :]
            dgp = dgp + wj * _shift_up(dgate, j, 0.0, row)
            tail_fix = tail_fix + wj * jnp.where(row8 >= keep - j, pltpu.roll(after, keep - j, 0), 0.0)
            dws[kk - 1 - j] = (jnp.sum(dgate * _shift_dn(gp, j, 0.0, row), axis=0, keepdims=True)
                               + jnp.sum(dgate[:keep] * jnp.where(row8 < j, pltpu.roll(prev, j, 0), 0.0),
                                         axis=0, keepdims=True))
        dgp = jnp.concatenate([dgp[:tm - keep], dgp[tm - keep:] + tail_fix], axis=0).astype(BF16)
        nxt[...] = dgate[:keep]
        dcw_ref[...] += jnp.concatenate(dws, axis=0)
        dg_ref[...] = dgp
        dhn_ref[...] = _dot_nt(dgp, wg_ref[...]) + _dot_nt(dup, wu_ref[...])

    def rev(i):
        return nt - 1 - i

    rows_f = pl.BlockSpec((tm, f), lambda i: (rev(i), 0))
    halo = pl.BlockSpec((None, keep, f), lambda i: (jnp.maximum(rev(i) * (tm // keep) - 1, 0), 0, 0))
    once = pl.Buffered(1)
    sd = jax.ShapeDtypeStruct
    r_ins, r_in_specs, r_outs, r_out_specs, r_sems = _ride_args(ride)
    outs = pl.pallas_call(
        _riding(body, 8, 5, 1, ride, 1), name=name, grid=(nt,),
        in_specs=[rows_f, rows_f, halo, rows_f, _whole(cw.shape), _whole(cb.shape),
                  pl.BlockSpec(wg.shape, lambda i: (0, 0), pipeline_mode=once),
                  pl.BlockSpec(wu.shape, lambda i: (0, 0), pipeline_mode=once)] + r_in_specs,
        out_specs=[rows_f, rows_f, pl.BlockSpec((tm, d), lambda i: (rev(i), 0)), _whole((kk, f)), _whole((1, f))]
        + r_out_specs,
        out_shape=[sd((t, f), BF16), sd((t, f), BF16), sd((t, d), F32), sd((kk, f), F32), sd((1, f), F32)] + r_outs,
        scratch_shapes=[pltpu.VMEM((keep, f), F32)] + r_sems,
        compiler_params=_params(("arbitrary",)),
    )(dact, gate_pre, gate_pre.reshape(t // keep, keep, f), up, cw, cb, wg, wu, *r_ins)
    return outs[:5], _ride_results(ride, outs[5:])


def _t5_bucket(dist):
    max_exact = REL_BUCKETS // 2
    d = np.maximum(dist, 1).astype(np.float32)
    large = max_exact + np.log(d / max_exact) / math.log(REL_MAX_DIST / max_exact) * (REL_BUCKETS - max_exact)
    large = np.minimum(large.astype(np.int32), REL_BUCKETS - 1)
    return np.where(dist < max_exact, dist, large).astype(np.int32)


def _band(window, dilation):
    qi = np.arange(ATTN_BLOCK)[:, None]
    kj = np.arange(2 * ATTN_BLOCK)[None, :]
    delta = ATTN_BLOCK + qi - kj
    mask = (delta >= 0) & (delta <= window // dilation)
    bucket = _t5_bucket(np.maximum(delta, 0) * dilation)
    return mask, bucket


def _attn_blocks(s, r):
    m = s // r
    assert m % ATTN_BLOCK == 0, "sequence length must be a multiple of dilation * block"
    return m // ATTN_BLOCK


def _perm_load(ref, r):
    if r == 1:
        return ref[...]
    m = ref.shape[0] // r
    return jnp.concatenate([ref[pl.ds(c, m, stride=r), :] for c in range(r)], axis=0)


def _perm_store(ref, g, val, r, add=False):
    if r == 1:
        ref[g] = ref[g] + val if add else val
        return
    m = val.shape[0] // r
    for c in range(r):
        rows = pl.ds(c, m, stride=r)
        part = val[c * m:(c + 1) * m]
        ref[g, rows, :] = ref[g, rows, :] + part if add else part


def _blocks(x):
    return x.reshape(x.shape[0] // ATTN_BLOCK, ATTN_BLOCK, x.shape[1])


def _prev_blocks(x):
    return jnp.concatenate([x[:1], x[:-1]], axis=0)


def _next_blocks(x):
    return jnp.concatenate([x[1:], jnp.zeros_like(x[:1])], axis=0)


def _first_block_neg(s, r):
    nblk = s // ATTN_BLOCK
    idx = lax.broadcasted_iota(jnp.int32, (nblk, 1, 1), 0)
    return jnp.where(idx % _attn_blocks(s, r) == 0, NEG, 0.0)


def _bdot_nt(a, b):
    return lax.dot_general(a, b, (((2,), (2,)), ((0,), (0,))), preferred_element_type=F32)


def _bdot(a, b):
    return lax.dot_general(a, b, (((2,), (1,)), ((0,), (0,))), preferred_element_type=F32)


def _bdot_tn(a, b):
    return lax.dot_general(a, b, (((1,), (1,)), ((0,), (0,))), preferred_element_type=F32)


def attn_fwd(qkv, biasm, n_heads, name, ride=()):
    b, s, _ = qkv.shape
    h = n_heads
    scale = HEAD_DIM ** -0.5
    blk = ATTN_BLOCK

    def body(q1_ref, q2_ref, q3_ref, k_ref, v_ref, bias_ref, o_ref, lse_ref, acc, m_s, l_s):
        for g, q_ref in enumerate((q1_ref, q2_ref, q3_ref)):
            r = DILATED[g][1]
            first = _first_block_neg(s, r)
            q = _blocks(_perm_load(q_ref, r).astype(BF16))
            k = _blocks(_perm_load(k_ref, r).astype(BF16))
            v = _blocks(_perm_load(v_ref, r).astype(BF16))
            s_cur = _bdot_nt(q, k) * scale + bias_ref[g, :, blk:]
            s_prev = _bdot_nt(q, _prev_blocks(k)) * scale + bias_ref[g, :, :blk] + first
            m = jnp.max(jnp.maximum(s_cur, s_prev), axis=-1, keepdims=True)
            p_cur = jnp.exp(s_cur - m)
            p_prev = jnp.exp(s_prev - m)
            l = jnp.sum(p_cur + p_prev, axis=-1, keepdims=True)
            o = _bdot(p_cur.astype(BF16), v) + _bdot(p_prev.astype(BF16), _prev_blocks(v))
            _perm_store(acc, g, o.reshape(s, HEAD_DIM), r)
            _perm_store(m_s, g, m.reshape(s, 1), r)
            _perm_store(l_s, g, l.reshape(s, 1), r)
        m_all = jnp.maximum(jnp.maximum(m_s[0], m_s[1]), m_s[2])
        w = [jnp.exp(m_s[g] - m_all) for g in range(N_GROUPS)]
        l = w[0] * l_s[0] + w[1] * l_s[1] + w[2] * l_s[2]
        o_ref[...] = (w[0] * acc[0] + w[1] * acc[1] + w[2] * acc[2]) / l
        lse_ref[...] = m_all + jnp.log(l)

    def col(j):
        return pl.BlockSpec((None, s, HEAD_DIM), lambda bi, hi, j=j: (bi, 0, j * h + hi))

    r_ins, r_in_specs, r_outs, r_out_specs, r_sems = _ride_args(ride)
    outs = pl.pallas_call(
        _riding(body, 6, 2, 3, ride, 2), name=name, grid=(b, h),
        in_specs=[col(0), col(1), col(2), col(3), col(4),
                  pl.BlockSpec((N_GROUPS, None, blk, 2 * blk), lambda bi, hi: (0, hi, 0, 0))] + r_in_specs,
        out_specs=[pl.BlockSpec((None, s, HEAD_DIM), lambda bi, hi: (bi, 0, hi)),
                   pl.BlockSpec((None, None, s, 1), lambda bi, hi: (bi, hi, 0, 0))] + r_out_specs,
        out_shape=[jax.ShapeDtypeStruct((b, s, h * HEAD_DIM), F32), jax.ShapeDtypeStruct((b, h, s, 1), F32)] + r_outs,
        scratch_shapes=[pltpu.VMEM((N_GROUPS, s, HEAD_DIM), F32), pltpu.VMEM((N_GROUPS, s, 1), F32),
                        pltpu.VMEM((N_GROUPS, s, 1), F32)] + r_sems,
        compiler_params=_params(("arbitrary", "arbitrary")),
    )(qkv, qkv, qkv, qkv, qkv, biasm, *r_ins)
    return outs[0], outs[1], _ride_results(ride, outs[2:])


def attn_bwd(qkv, biasm, o, lse, do, n_heads, name, ride=()):
    b, s, _ = qkv.shape
    h = n_heads
    scale = HEAD_DIM ** -0.5
    blk = ATTN_BLOCK

    def body(q1_ref, q2_ref, q3_ref, k_ref, v_ref, bias_ref, o_ref, lse_ref, do_ref,
             dq1_ref, dq2_ref, dq3_ref, dk_ref, dv_ref, ds_ref, dq_acc, kv_acc, delta):
        delta[...] = jnp.sum(do_ref[...] * o_ref[...], axis=-1, keepdims=True)
        kv_acc[...] = jnp.zeros(kv_acc.shape, F32)
        for g, q_ref in enumerate((q1_ref, q2_ref, q3_ref)):
            r = DILATED[g][1]
            first = _first_block_neg(s, r)
            q = _blocks(_perm_load(q_ref, r).astype(BF16))
            k = _blocks(_perm_load(k_ref, r).astype(BF16))
            v = _blocks(_perm_load(v_ref, r).astype(BF16))
            dob = _blocks(_perm_load(do_ref, r).astype(BF16))
            lse_b = _blocks(_perm_load(lse_ref, r))
            dl_b = _blocks(_perm_load(delta, r))
            k_prev, v_prev = _prev_blocks(k), _prev_blocks(v)
            p_cur = jnp.exp(_bdot_nt(q, k) * scale + bias_ref[g, :, blk:] - lse_b)
            p_prev = jnp.exp(_bdot_nt(q, k_prev) * scale + bias_ref[g, :, :blk] + first - lse_b)
            ds_cur = p_cur * (_bdot_nt(dob, v) - dl_b)
            ds_prev = p_prev * (_bdot_nt(dob, v_prev) - dl_b)
            ds_ref[g, :, blk:] = jnp.sum(ds_cur, axis=0)
            ds_ref[g, :, :blk] = jnp.sum(ds_prev, axis=0)
            ds_cur_b, ds_prev_b = ds_cur.astype(BF16), ds_prev.astype(BF16)
            dq = (_bdot(ds_cur_b, k) + _bdot(ds_prev_b, k_prev)) * scale
            _perm_store(dq_acc, g, dq.reshape(s, HEAD_DIM), r)
            dk = (_bdot_tn(ds_cur_b, q) + _next_blocks(_bdot_tn(ds_prev_b, q))) * scale
            dv = _bdot_tn(p_cur.astype(BF16), dob) + _next_blocks(_bdot_tn(p_prev.astype(BF16), dob))
            _perm_store(kv_acc, 0, dk.reshape(s, HEAD_DIM), r, add=True)
            _perm_store(kv_acc, 1, dv.reshape(s, HEAD_DIM), r, add=True)
        for g, out_ref in enumerate((dq1_ref, dq2_ref, dq3_ref)):
            out_ref[...] = dq_acc[g].astype(out_ref.dtype)
        dk_ref[...] = kv_acc[0].astype(dk_ref.dtype)
        dv_ref[...] = kv_acc[1].astype(dv_ref.dtype)

    def col(j):
        return pl.BlockSpec((None, s, HEAD_DIM), lambda bi, hi, j=j: (bi, 0, j * h + hi))

    head = pl.BlockSpec((None, s, HEAD_DIM), lambda bi, hi: (bi, 0, hi))
    sd = jax.ShapeDtypeStruct
    r_ins, r_in_specs, r_outs, r_out_specs, r_sems = _ride_args(ride)
    outs = pl.pallas_call(
        _riding(body, 9, 6, 3, ride, 2), name=name, grid=(b, h),
        in_specs=[col(0), col(1), col(2), col(3), col(4),
                  pl.BlockSpec((N_GROUPS, None, blk, 2 * blk), lambda bi, hi: (0, hi, 0, 0)),
                  head, pl.BlockSpec((None, None, s, 1), lambda bi, hi: (bi, hi, 0, 0)), head] + r_in_specs,
        out_specs=[head] * 5 + [pl.BlockSpec((None, None, N_GROUPS, blk, 2 * blk), lambda bi, hi: (bi, hi, 0, 0, 0))]
        + r_out_specs,
        out_shape=[sd((b, s, h * HEAD_DIM), BF16)] * 5 + [sd((b, h, N_GROUPS, blk, 2 * blk), F32)] + r_outs,
        scratch_shapes=[pltpu.VMEM((N_GROUPS, s, HEAD_DIM), F32), pltpu.VMEM((2, s, HEAD_DIM), F32),
                        pltpu.VMEM((s, 1), F32)] + r_sems,
        compiler_params=_params(("arbitrary", "arbitrary")),
    )(qkv, qkv, qkv, qkv, qkv, biasm, o, lse, do, *r_ins)
    return outs[:6], _ride_results(ride, outs[6:])


def bias_table(rel_rows, bucket_f, n_heads, name):
    g, blk, blk2 = bucket_f.shape
    h = n_heads

    def body(rb_ref, bk_ref, o_ref):
        bk = bk_ref[...]
        rb = rb_ref[...]
        acc = jnp.full((blk, blk2), NEG, F32)
        for bucket in range(REL_BUCKETS):
            acc = jnp.where(bk == float(bucket), rb[:, bucket:bucket + 1], acc)
        o_ref[...] = acc

    return pl.pallas_call(
        body, name=name, grid=(g, h),
        in_specs=[pl.BlockSpec((None, 1, 128), lambda gi, hi: (gi * h + hi, 0, 0)),
                  pl.BlockSpec((None, blk, blk2), lambda gi, hi: (gi, 0, 0))],
        out_specs=pl.BlockSpec((None, None, blk, blk2), lambda gi, hi: (gi, hi, 0, 0)),
        out_shape=jax.ShapeDtypeStruct((g, h, blk, blk2), F32),
        compiler_params=_params(("parallel", "parallel")),
    )(rel_rows, bucket_f)


def bias_grad(ds_sum, bucket_f, name):
    b, h, g, blk, blk2 = ds_sum.shape

    def body(ds_ref, bk_ref, o_ref):
        tot = jnp.sum(ds_ref[...], axis=0)
        bk = bk_ref[...]
        lane = lax.broadcasted_iota(jnp.int32, (1, 128), 1)
        vec = jnp.zeros((1, 128), F32)
        for bucket in range(REL_BUCKETS):
            val = jnp.sum(jnp.where(bk == float(bucket), tot, 0.0), keepdims=True)
            vec = vec + jnp.where(lane == bucket, val, 0.0)
        o_ref[...] = vec

    return pl.pallas_call(
        body, name=name, grid=(g, h),
        in_specs=[pl.BlockSpec((b, None, None, blk, blk2), lambda gi, hi: (0, hi, gi, 0, 0)),
                  pl.BlockSpec((None, blk, blk2), lambda gi, hi: (gi, 0, 0))],
        out_specs=pl.BlockSpec((None, 1, 128), lambda gi, hi: (gi * h + hi, 0, 0)),
        out_shape=jax.ShapeDtypeStruct((g * h, 1, 128), F32),
        compiler_params=_params(("parallel", "parallel")),
    )(ds_sum, bucket_f)


def _chip_peers():
    x, y, c = lax.axis_index("x"), lax.axis_index("y"), lax.axis_index("c")
    me = 2 * x + y
    peers = [(1 - x, y, c), (x, 1 - y, c), (1 - x, 1 - y, c)]
    peer_chip = [2 * (1 - x) + y, 2 * x + (1 - y), 2 * (1 - x) + (1 - y)]
    return me, peers, peer_chip


def _any_specs(n):
    return [pl.BlockSpec(memory_space=pl.ANY)] * n


_MID_NUM, _MID_DEN = 3, 4


class _Exchange:
    def start(self, ins, outs, sems):
        local, sends, _ = self._copies(ins, outs, sems)
        for cp in local + sends:
            cp.start()

    def mid(self, ins, outs, sems):
        pass

    def wait(self, ins, outs, sems):
        local, sends, recvs = self._copies(ins, outs, sems)
        for cp in recvs():
            cp.wait_recv()
        for cp in sends:
            cp.wait_send()
        for cp in local:
            cp.wait()


class _Gather(_Exchange):
    HALF_ROWS = 16

    def __init__(self, arrays):
        n = len(arrays)
        self.ins = list(arrays)
        self.split = [a.shape[0] % (2 * self.HALF_ROWS) == 0 for a in arrays]
        self.out_shape = [jax.ShapeDtypeStruct((N_CHIPS,) + a.shape, a.dtype) for a in arrays]
        dma = pltpu.SemaphoreType.DMA
        self.sems = [dma((3 * n,)), dma((3 * n,)), dma((n,)), dma((3 * n,)), dma((3 * n,))]

    def _half(self, i, ref, sibling=False):
        if not self.split[i]:
            return ref
        half = self.ins[i].shape[0] // 2
        c = lax.axis_index("c")
        c = 1 - c if sibling else c
        return ref.at[pl.ds(pl.multiple_of(c * half, self.HALF_ROWS), half)]

    def _plan(self, ins, outs, sems):
        send1, recv1, local_sems, send2, recv2 = sems
        me, peers, peer_chip = _chip_peers()
        x, y, c = lax.axis_index("x"), lax.axis_index("y"), lax.axis_index("c")
        n = len(ins)
        pairs = [(i, k) for k in range(3) for i in range(n)]

        def fetch(i, k, slot):
            return pltpu.make_async_remote_copy(src_ref=self._half(i, ins[i]), dst_ref=self._half(i, outs[i].at[slot]),
                                                send_sem=send1.at[3 * i + k], recv_sem=recv1.at[3 * i + k],
                                                device_id=peers[k], device_id_type=MESH)

        def share(i, k, sibling):
            part = self._half(i, outs[i].at[peer_chip[k]], sibling)
            return pltpu.make_async_remote_copy(src_ref=part, dst_ref=part, send_sem=send2.at[3 * i + k],
                                                recv_sem=recv2.at[3 * i + k], device_id=(x, y, 1 - c),
                                                device_id_type=MESH)

        split_pairs = [(i, k) for i, k in pairs if self.split[i]]
        return dict(
            local=lambda: [pltpu.make_async_copy(ins[i], outs[i].at[me], local_sems.at[i]) for i in range(n)],
            fetch_out=lambda: [fetch(i, k, me) for i, k in pairs],
            fetch_in=lambda: [(fetch(i, k, peer_chip[k]), share(i, k, False) if self.split[i] else None)
                              for i, k in pairs],
            share_out=lambda: [share(i, k, False) for i, k in split_pairs],
            share_in=lambda: [share(i, k, True) for i, k in split_pairs])

    def start(self, ins, outs, sems):
        plan = self._plan(ins, outs, sems)
        for cp in plan["local"]() + plan["fetch_out"]():
            cp.start()

    def mid(self, ins, outs, sems):
        plan = self._plan(ins, outs, sems)
        for arrived, forward in plan["fetch_in"]():
            arrived.wait_recv()
            if forward is not None:
                forward.start()

    def wait(self, ins, outs, sems):
        plan = self._plan(ins, outs, sems)
        for cp in plan["share_in"]():
            cp.wait_recv()
        for cp in plan["fetch_out"]() + plan["share_out"]():
            cp.wait_send()
        for cp in plan["local"]():
            cp.wait()


class _Scatter(_Exchange):
    def __init__(self, slabs, whole=()):
        self.n_slabs = len(slabs)
        self.ins = list(slabs) + list(whole)
        n = len(self.ins)
        self.out_shape = [jax.ShapeDtypeStruct(a.shape, a.dtype) for a in slabs] \
            + [jax.ShapeDtypeStruct((N_CHIPS,) + a.shape, a.dtype) for a in whole]
        self.sems = [pltpu.SemaphoreType.DMA((3 * n,)), pltpu.SemaphoreType.DMA((3 * n,)), pltpu.SemaphoreType.DMA((n,))]

    def _copies(self, ins, outs, sems):
        send_sems, recv_sems, local_sems = sems
        me, peers, peer_chip = _chip_peers()
        n = len(ins)

        def src(i, chip):
            return ins[i].at[chip] if i < self.n_slabs else ins[i]

        def remote(i, k, src_chip, slot):
            return pltpu.make_async_remote_copy(src_ref=src(i, src_chip), dst_ref=outs[i].at[slot],
                                                send_sem=send_sems.at[3 * i + k], recv_sem=recv_sems.at[3 * i + k],
                                                device_id=peers[k], device_id_type=MESH)

        local = [pltpu.make_async_copy(src(i, me), outs[i].at[me], local_sems.at[i]) for i in range(n)]
        sends = [remote(i, k, peer_chip[k], me) for i in range(n) for k in range(3)]
        return local, sends, lambda: [remote(i, k, me, peer_chip[k]) for i in range(n) for k in range(3)]


class _Swap(_Exchange):
    def __init__(self, arrays):
        n = len(arrays)
        self.ins = list(arrays)
        self.out_shape = [jax.ShapeDtypeStruct(a.shape, a.dtype) for a in arrays]
        self.sems = [pltpu.SemaphoreType.DMA((n,)), pltpu.SemaphoreType.DMA((n,))]

    def _copies(self, ins, outs, sems):
        send_sems, recv_sems = sems
        x, y, c = lax.axis_index("x"), lax.axis_index("y"), lax.axis_index("c")
        cps = [pltpu.make_async_remote_copy(src_ref=ins[i], dst_ref=outs[i], send_sem=send_sems.at[i],
                                            recv_sem=recv_sems.at[i], device_id=(x, y, 1 - c), device_id_type=MESH)
               for i in range(len(ins))]
        return [], cps, lambda: cps


def _riding(body, n_in, n_out, n_scratch, ride, rank):
    if not ride:
        return body
    r_in = sum(len(e.ins) for e in ride)
    r_out = sum(len(e.out_shape) for e in ride)

    def split(refs, sizes):
        out, a = [], 0
        for sz in sizes:
            out.append(refs[a:a + sz])
            a += sz
        return out

    def wrapped(*refs):
        a = 0
        parts = []
        for sz in (n_in, r_in, n_out, r_out, n_scratch):
            parts.append(refs[a:a + sz])
            a += sz
        own_in, ex_in, own_out, ex_out, own_scratch = parts
        ex_sems = refs[a:]
        ins = split(ex_in, [len(e.ins) for e in ride])
        outs = split(ex_out, [len(e.out_shape) for e in ride])
        sems = split(ex_sems, [len(e.sems) for e in ride])
        if rank:
            step, total = 0, 1
            for d in range(rank):
                step = step * pl.num_programs(d) + pl.program_id(d)
                total = total * pl.num_programs(d)

            @pl.when(step == 0)
            def _():
                for e, i, o, s in zip(ride, ins, outs, sems):
                    e.start(i, o, s)

            body(*own_in, *own_out, *own_scratch)

            @pl.when(step == (total * _MID_NUM) // _MID_DEN)
            def _():
                for e, i, o, s in zip(ride, ins, outs, sems):
                    e.mid(i, o, s)

            @pl.when(step == total - 1)
            def _():
                for e, i, o, s in zip(ride, ins, outs, sems):
                    e.wait(i, o, s)
        else:
            for phase in ("start", "mid", "wait"):
                for e, i, o, s in zip(ride, ins, outs, sems):
                    getattr(e, phase)(i, o, s)

    return wrapped


def _ride_args(ride):
    ins = [a for e in ride for a in e.ins]
    outs = [s for e in ride for s in e.out_shape]
    sems = [s for e in ride for s in e.sems]
    return ins, _any_specs(len(ins)), outs, _any_specs(len(outs)), sems


def _ride_results(ride, flat):
    out, a = [], 0
    for e in ride:
        out.append(list(flat[a:a + len(e.out_shape)]))
        a += len(e.out_shape)
    return out


def exchange(ride, name):
    ins, in_specs, outs, out_specs, sems = _ride_args(ride)
    res = pl.pallas_call(
        _riding(lambda: None, 0, 0, 0, ride, 0), name=name,
        in_specs=in_specs, out_specs=out_specs, out_shape=outs, scratch_shapes=sems,
    )(*ins)
    return _ride_results(ride, res)


def _sum_slots(ref):
    acc = ref[0].astype(F32)
    for j in range(1, ref.shape[0]):
        acc = acc + ref[j].astype(F32)
    return acc


def sum_pairs(mine, other, name, tr=176):
    n, r, w = mine.shape
    tr = r if r <= tr else _tile(r, tr)

    def body(a_ref, b_ref, o_ref):
        o_ref[...] = _sum_slots(a_ref) + _sum_slots(b_ref)

    spec = pl.BlockSpec((n, tr, w), lambda i: (0, i, 0))
    return pl.pallas_call(
        body, name=name, grid=(r // tr,),
        in_specs=[spec, spec], out_specs=_rows(tr, w),
        out_shape=jax.ShapeDtypeStruct((r, w), F32),
        compiler_params=_params(("parallel",)),
    )(mine, other)


def _adamw_update(w, m, v, g):
    c1 = 1.0 - ADAM_B1 ** ADAM_STEP
    c2 = 1.0 - ADAM_B2 ** ADAM_STEP
    nm = ADAM_B1 * m + (1.0 - ADAM_B1) * g
    nv = ADAM_B2 * v + (1.0 - ADAM_B2) * (g * g)
    return nm, nv, (-ADAM_LR) * ((nm / c1) / (jnp.sqrt(nv / c2) + ADAM_EPS) + ADAM_WD * w)


def adamw(w, m, v, mine, other, name, tr=256):
    r, c = w.shape
    tr = r if r % 8 else _tile(r, tr)

    def body(w_ref, m_ref, v_ref, a_ref, b_ref, g_ref, d_ref, nm_ref, nv_ref):
        g = _sum_slots(a_ref) + _sum_slots(b_ref)
        nm, nv, delta = _adamw_update(w_ref[...], m_ref[...], v_ref[...], g)
        g_ref[...] = g
        nm_ref[...] = nm
        nv_ref[...] = nv
        d_ref[...] = delta

    spec = _rows(tr, c)
    gspec = pl.BlockSpec((N_CHIPS, tr, c), lambda i: (0, i, 0))
    return pl.pallas_call(
        body, name=name, grid=(r // tr,),
        in_specs=[spec] * 3 + [gspec] * 2, out_specs=[spec] * 4,
        out_shape=[jax.ShapeDtypeStruct((r, c), F32)] * 4,
        compiler_params=_params(("parallel",)),
    )(w, m, v, mine, other)


def adamw_small(ws, ms, vs, gs, name):
    n = len(ws)

    def body(*refs):
        ins, outs = refs[:4 * n], refs[4 * n:]
        for i in range(n):
            w_ref, m_ref, v_ref, g_ref = ins[4 * i:4 * i + 4]
            d_ref, nm_ref, nv_ref = outs[3 * i:3 * i + 3]
            nm, nv, delta = _adamw_update(w_ref[...], m_ref[...], v_ref[...], g_ref[...])
            d_ref[...] = delta
            nm_ref[...] = nm
            nv_ref[...] = nv

    flat = [a for quad in zip(ws, ms, vs, gs) for a in quad]
    vmem = pl.BlockSpec(memory_space=pltpu.VMEM)
    outs = pl.pallas_call(
        body, name=name,
        in_specs=[vmem] * (4 * n), out_specs=[vmem] * (3 * n),
        out_shape=[jax.ShapeDtypeStruct(w.shape, F32) for w in ws for _ in range(3)],
        compiler_params=_params(),
    )(*flat)
    return [tuple(outs[3 * i:3 * i + 3]) for i in range(n)]


_PARAMS = (
    ("rel_bias", None), ("norm_mix_pre", None), ("norm_mix_post", None), ("w_in", 1), ("conv_rnn_w", 1),
    ("conv_rnn_b", None), ("w_rg_a", None), ("b_rg_a", None), ("w_rg_x", None), ("b_rg_x", None),
    ("lru_lambda", None), ("w_branch_rnn", 0), ("w_branch_att", 1), ("w_out", 0), ("norm_ffn_pre", None),
    ("norm_ffn_post", None), ("w_ffn_gate", 1), ("w_ffn_up", 1), ("conv_ffn_w", 1), ("conv_ffn_b", None),
    ("w_ffn_down", 0),
)
_SMALL = 65536


def _as2d(a):
    a = a[0] if a.shape[0] == 1 and a.ndim >= 3 else a
    return a.reshape(-1, a.shape[-1]) if a.ndim == 3 else a


def _pack(pieces, dtype):
    flat = jnp.concatenate([p.astype(dtype).reshape(-1) for p in pieces])
    unit = PACK_W * PACK_ROWS
    pad = (-flat.shape[0]) % unit
    flat = jnp.pad(flat, (0, pad))
    return flat.reshape(-1, PACK_W)


def _unpack(buf, shapes):
    flat = buf.reshape(-1)
    out, off = [], 0
    for shp in shapes:
        n = int(np.prod(shp))
        out.append(flat[off:off + n].reshape(shp))
        off += n
    return out


def _join(slots, ax):
    if ax == 0:
        return slots.reshape(-1, slots.shape[-1])
    return jnp.transpose(slots, (1, 0, 2)).reshape(slots.shape[1], -1)


def _cut(full, ax):
    if ax == 0:
        return full.reshape(N_CHIPS, -1, full.shape[-1])
    return jnp.transpose(full.reshape(full.shape[0], N_CHIPS, -1), (1, 0, 2))


def kernel(x, rel_bias, norm_mix_pre, norm_mix_post, w_in, conv_rnn_w, conv_rnn_b, w_rg_a, b_rg_a, w_rg_x, b_rg_x, lru_lambda, w_branch_rnn, w_branch_att, w_out, norm_ffn_pre, norm_ffn_post, w_ffn_gate, w_ffn_up, conv_ffn_w, conv_ffn_b, w_ffn_down, loss_target, m_rel_bias, m_norm_mix_pre, m_norm_mix_post, m_w_in, m_conv_rnn_w, m_conv_rnn_b, m_w_rg_a, m_b_rg_a, m_w_rg_x, m_b_rg_x, m_lru_lambda, m_w_branch_rnn, m_w_branch_att, m_w_out, m_norm_ffn_pre, m_norm_ffn_post, m_w_ffn_gate, m_w_ffn_up, m_conv_ffn_w, m_conv_ffn_b, m_w_ffn_down, v_rel_bias, v_norm_mix_pre, v_norm_mix_post, v_w_in, v_conv_rnn_w, v_conv_rnn_b, v_w_rg_a, v_b_rg_a, v_w_rg_x, v_b_rg_x, v_lru_lambda, v_w_branch_rnn, v_w_branch_att, v_w_out, v_norm_ffn_pre, v_norm_ffn_post, v_w_ffn_gate, v_w_ffn_up, v_conv_ffn_w, v_conv_ffn_b, v_w_ffn_down):
    args = dict(locals())
    names = [n for n, _ in _PARAMS]
    axis = dict(_PARAMS)
    w_loc = {n: args[n] for n in names}
    m_loc = {n: args["m_" + n] for n in names}
    v_loc = {n: args["v_" + n] for n in names}
    sharded = [n for n in names if axis[n] is not None]
    replicated = [n for n in names if axis[n] is None]

    big = [n for n in sharded if w_loc[n].size >= _SMALL]
    small_sharded = [n for n in sharded if n not in big]
    small = replicated + small_sharded

    first = ["w_in"] + small_sharded
    srcs = [_as2d(w_loc[n]).astype(BF16) if n in big else _as2d(w_loc[n]) for n in first]
    (gathered,) = exchange([_Gather(srcs)], "gather_first")
    p = {n: _join(a, axis[n]) for n, a in zip(first, gathered)}
    for n in replicated:
        p[n] = _as2d(w_loc[n])
    shards = {n: _as2d(w_loc[n]).astype(BF16) for n in big if n not in first}

    last = "norm_mix_pre"
    early = [n for n in small if n != last]
    received, sibling, g_small, loss_part = _local_step(x, loss_target, p, shards, early)

    ((received["last"],),) = exchange([_Scatter([], [_pack([g_small[last]], BF16)])], "scatter_last")
    late = [n for n in received if n not in sibling]
    (swapped,) = exchange([_Swap([received[n] for n in late])], "swap_last")
    sibling.update(zip(late, swapped))
    early_sum = sum_pairs(received["small"], sibling["small"], "sum_small")
    last_sum = sum_pairs(received["last"], sibling["last"], "sum_last")
    g_tot = dict(zip(early, _unpack(early_sum, [g_small[n].shape for n in early])))
    (g_tot[last],) = _unpack(last_sum, [g_small[last].shape])
    chip = 2 * lax.axis_index("x") + lax.axis_index("y")
    for n in small_sharded:
        size = g_tot[n].shape[axis[n]] // N_CHIPS
        g_tot[n] = lax.dynamic_slice_in_dim(g_tot[n], chip * size, size, axis=axis[n])

    out_g, out_d, out_m, out_v = {}, {}, {}, {}
    for n in big:
        res = adamw(_as2d(w_loc[n]), _as2d(m_loc[n]), _as2d(v_loc[n]), received[n], sibling[n], "adamw_" + n)
        out_g[n], out_d[n], out_m[n], out_v[n] = (t.reshape(w_loc[n].shape) for t in res)
    res = adamw_small([_as2d(w_loc[n]) for n in small], [_as2d(m_loc[n]) for n in small],
                      [_as2d(v_loc[n]) for n in small], [g_tot[n] for n in small], "adamw_small")
    for n, (d, nm, nv) in zip(small, res):
        out_g[n], out_d[n], out_m[n], out_v[n] = (t.reshape(w_loc[n].shape) for t in (g_tot[n], d, nm, nv))

    d_model = x.shape[-1]
    loss = lax.psum(0.5 * jnp.sum(loss_part) / d_model, ("x", "y", "c"))
    grad_x = g_small["x"]
    return (loss, grad_x, *[out_g[n] for n in names], *[out_d[n] for n in names],
            *[out_m[n] for n in names], *[out_v[n] for n in names])


def _local_step(x, target, p, shards, small_early):
    axis = dict(_PARAMS)
    b, s, d = x.shape
    t = b * s
    rnn = p["b_rg_a"].shape[1]
    ffn = p["conv_ffn_b"].shape[1]
    nbk = rnn // p["w_rg_a"].shape[1]
    hkv = (p["w_in"].shape[1] - rnn - 2 * d) // (N_GROUPS + 2)
    h = hkv // HEAD_DIM
    nq = N_GROUPS * hkv

    x2 = x.reshape(t, d)
    tgt = target.reshape(t, d)
    w_in = p["w_in"]
    in_splits = (rnn, nq + 2 * hkv, 2 * d)
    wa = p["w_rg_a"].reshape(nbk, -1, p["w_rg_a"].shape[1]).astype(BF16)
    wx = p["w_rg_x"].reshape(nbk, -1, p["w_rg_x"].shape[1]).astype(BF16)
    cw_r, cb_r = p["conv_rnn_w"], p["conv_rnn_b"]
    cw_f, cb_f = p["conv_ffn_w"], p["conv_ffn_b"]

    masks, buckets = zip(*[_band(w_, r_) for w_, r_ in DILATED])
    bucket_f = jnp.asarray(np.where(np.stack(masks), np.stack(buckets), -1).astype(np.float32))
    rel_rows = jnp.pad(p["rel_bias"].T, ((0, 0), (0, 128 - REL_BUCKETS)))[:, None, :]
    biasm = bias_table(rel_rows, bucket_f, h, "bias_table")

    early = ["w_branch_rnn", "w_branch_att", "w_out"]
    hn1, (xr, qkv, gts), (got,) = norm_mm(x2, p["norm_mix_pre"], [w_in], [in_splits], "in_proj",
                                          ride=[_Gather([shards[n] for n in early])])
    p.update({n: _join(a, axis[n]) for n, a in zip(early, got)})
    xr3 = xr.reshape(b, s, rnn)
    (y_rnn, a_rnn, xc_rnn), (got,) = rglru_fwd(xr3, cw_r, cb_r, wa, p["b_rg_a"], wx, p["b_rg_x"], p["lru_lambda"], "rglru_fwd",
                              ride=[_Gather([shards[n] for n in ("w_ffn_gate", "w_ffn_up")])])
    p.update({n: _join(a, axis[n]) for n, a in zip(("w_ffn_gate", "w_ffn_up"), got)})
    qkv3 = qkv.reshape(b, s, -1)
    o_att, lse, ((got,),) = attn_fwd(qkv3, biasm, h, "attn_fwd", ride=[_Gather([shards["w_ffn_down"]])])
    p["w_ffn_down"] = _join(got, axis["w_ffn_down"])
    merged, br, ba, mix, h1 = merge_out(y_rnn.reshape(t, rnn), o_att.reshape(t, hkv), gts, p["w_branch_rnn"],
                                        p["w_branch_att"], p["w_out"], p["norm_mix_post"], x2, "merge_out")
    hn2, gate_pre, up, act = ffn_in_act(h1, p["norm_ffn_pre"], p["w_ffn_gate"], p["w_ffn_up"], cw_f, cb_f, s, "ffn_in")

    g, gb = {}, {}
    recv, sib = {}, {}

    def rows4(a):
        return a.reshape(N_CHIPS, -1, a.shape[-1])

    dy, dff, dact, g["norm_ffn_post"], loss_part = ffn_down_loss(act, p["w_ffn_down"], p["norm_ffn_post"], h1, tgt,
                                                                  "ffn_down")
    gb["w_ffn_down"] = rows4(mm_tn(act, [dff], "ffn_down_dw"))
    (dgp, dup, dhn2, g["conv_ffn_w"], g["conv_ffn_b"]), ((recv["w_ffn_down"],),) = ffn_in_bwd(
        dact, gate_pre, up, cw_f, cb_f, p["w_ffn_gate"], p["w_ffn_up"], s, "ffn_in_bwd",
        ride=[_Scatter([gb["w_ffn_down"]])])
    gb["w_ffn_gate"] = mm_tn(hn2, [dgp], "ffn_gate_dw", col_shards=N_CHIPS)
    gb["w_ffn_up"] = mm_tn(hn2, [dup], "ffn_up_dw", col_shards=N_CHIPS)
    (dh1, dgts, dy_rnn, do_att, g["norm_ffn_pre"], g["norm_mix_post"], dw_out, dw_br,
     gb["w_branch_att"]) = mid_bwd(dhn2, h1, p["norm_ffn_pre"], dy, mix, p["norm_mix_post"], p["w_out"], gts, br, ba,
                                   p["w_branch_rnn"], p["w_branch_att"], merged, y_rnn.reshape(t, rnn),
                                   o_att.reshape(t, hkv), "mid_bwd")
    gb["w_out"], gb["w_branch_rnn"] = rows4(dw_out), rows4(dw_br)
    ffn_in = ["w_ffn_gate", "w_ffn_up"]
    (dxr, g["conv_rnn_w"], g["conv_rnn_b"], dwa, g["b_rg_a"], dwx, g["b_rg_x"], g["lru_lambda"]), (got,) = rglru_bwd(
        xr3, y_rnn, dy_rnn.reshape(b, s, rnn), a_rnn, xc_rnn, cw_r, wa, p["b_rg_a"], wx, p["b_rg_x"], p["lru_lambda"], "rglru_bwd",
        ride=[_Scatter([gb[n] for n in ffn_in])])
    recv.update(zip(ffn_in, got))
    g["w_rg_a"] = dwa.reshape(p["w_rg_a"].shape)
    g["w_rg_x"] = dwx.reshape(p["w_rg_x"].shape)
    mid = ["w_out", "w_branch_rnn", "w_branch_att"]
    early_recv = ["w_ffn_down"] + ffn_in
    (dq1, dq2, dq3, dk, dv, ds_sum), (got, swapped) = attn_bwd(
        qkv3, biasm, o_att, lse, do_att.reshape(b, s, hkv), h, "attn_bwd",
        ride=[_Scatter([gb[n] for n in mid]), _Swap([recv[n] for n in early_recv])])
    recv.update(zip(mid, got))
    sib.update(zip(early_recv, swapped))
    rows = bias_grad(ds_sum, bucket_f, "bias_grad")
    g["rel_bias"] = rows[:, 0, :REL_BUCKETS].T
    dproj = [dxr.reshape(t, rnn)] + [a.reshape(t, hkv) for a in (dq1, dq2, dq3, dk, dv)] + [dgts]
    dw_a, (got,) = mm_tn(hn1, dproj[:4], "in_proj_dw_a", ride=[_Swap([recv[n] for n in mid])])
    sib.update(zip(mid, got))
    pack = _pack([g[n] for n in small_early], BF16)
    dw_b, ((recv["small"],),) = mm_tn(hn1, dproj[4:], "in_proj_dw_b", ride=[_Scatter([], [pack])])
    gb["w_in"] = _cut(jnp.concatenate([dw_a[0], dw_b[0]], axis=1), 1)
    dx, g["norm_mix_pre"], ((recv["w_in"],),) = mm_nt(
        [(dproj, w_in)], "in_proj_dx", norm=(x2, p["norm_mix_pre"], dh1), ride=[_Scatter([gb["w_in"]])])
    g["x"] = dx.reshape(b, s, d)
    return recv, sib, g, loss_part
```

```python
import math

import numpy as np
import jax
import jax.numpy as jnp
from jax import lax
from jax.experimental import pallas as pl
from jax.experimental.pallas import tpu as pltpu

F32 = jnp.float32
BF16 = jnp.bfloat16

EPS = 1e-6
HEAD_DIM = 128
ATTN_BLOCK = 128
DILATED = ((128, 1), (512, 4), (2048, 16))
N_GROUPS = len(DILATED)
REL_BUCKETS = 32
REL_MAX_DIST = 2048
LRU_C = 8.0
NEG = -1e30

ADAM_LR = 0.001
ADAM_B1 = 0.9
ADAM_B2 = 0.999
ADAM_EPS = 1e-08
ADAM_WD = 0.01
ADAM_STEP = 10

N_CHIPS = 4
PACK_W = 1024
PACK_ROWS = 16
VMEM_LIMIT = 56 * 1024 * 1024
MESH = pl.DeviceIdType.MESH


def _params(sem=None):
    return pltpu.CompilerParams(dimension_semantics=sem, vmem_limit_bytes=VMEM_LIMIT)


def _dot(a, b):
    return jnp.dot(a, b, preferred_element_type=F32)


def _dot_nt(a, b):
    return lax.dot_general(a, b, (((1,), (1,)), ((), ())), preferred_element_type=F32)


def _dot_tn(a, b):
    return lax.dot_general(a, b, (((0,), (0,)), ((), ())), preferred_element_type=F32)


def _sig(x):
    return 0.5 * jnp.tanh(0.5 * x) + 0.5


def _rows(tm, w):
    return pl.BlockSpec((tm, w), lambda i: (i, 0))


def _whole(shape):
    nd = len(shape)
    return pl.BlockSpec(tuple(shape), lambda *_: (0,) * nd)


def _resident(shape):
    nd = len(shape)
    return pl.BlockSpec(tuple(shape), lambda *_: (0,) * nd, pipeline_mode=pl.Buffered(1))


def _tile(t, want):
    while t % want:
        want //= 2
    return want


def norm_mm(x, g, ws, splits, name, ride=(), tm=512):
    t, d = x.shape
    tm = _tile(t, tm)
    nw = len(ws)
    widths = [n for sp in splits for n in sp]

    def body(x_ref, g_ref, *refs):
        w_refs, hn_ref, o_refs = refs[:nw], refs[nw], refs[nw + 1:]
        xv = x_ref[...]
        inv = lax.rsqrt(jnp.mean(xv * xv, axis=-1, keepdims=True) + EPS)
        hn = (xv * inv * g_ref[...]).astype(BF16)
        hn_ref[...] = hn
        o = 0
        for w_ref, sp in zip(w_refs, splits):
            off = 0
            for n in sp:
                o_refs[o][...] = _dot(hn, w_ref[:, off:off + n])
                off += n
                o += 1

    r_ins, r_in_specs, r_outs, r_out_specs, r_sems = _ride_args(ride)
    n_out = 1 + len(widths)
    outs = pl.pallas_call(
        _riding(body, 2 + nw, n_out, 0, ride, 1), name=name, grid=(t // tm,),
        in_specs=[_rows(tm, d), _whole(g.shape)] + [_resident(w.shape) for w in ws] + r_in_specs,
        out_specs=[_rows(tm, d)] + [_rows(tm, n) for n in widths] + r_out_specs,
        out_shape=[jax.ShapeDtypeStruct((t, d), BF16)] + [jax.ShapeDtypeStruct((t, n), F32) for n in widths] + r_outs,
        scratch_shapes=r_sems,
        compiler_params=_params(("arbitrary",)),
    )(x, g, *ws, *r_ins)
    return outs[0], outs[1:n_out], _ride_results(ride, outs[n_out:])


def mm_nt(groups, name, ride=(), norm=None, tm=512):
    dys_all = [dy for dys, _ in groups for dy in dys]
    ws = [w for _, w in groups]
    t = dys_all[0].shape[0]
    k = ws[0].shape[0]
    tm = _tile(t, tm)
    n = len(dys_all)
    extra = list(norm) if norm else []

    def body(*refs):
        dy_refs, w_refs = refs[:n], refs[n:n + len(ws)]
        rest = refs[n + len(ws):]
        acc = None
        i = 0
        for (dys, _), w_ref in zip(groups, w_refs):
            off = 0
            for dy in dys:
                width = dy.shape[1]
                part = _dot_nt(dy_refs[i][...].astype(BF16), w_ref[:, off:off + width])
                acc = part if acc is None else acc + part
                off += width
                i += 1
        if norm:
            u_ref, g_ref, add_ref, o_ref, dg_ref = rest

            @pl.when(pl.program_id(0) == 0)
            def _():
                dg_ref[...] = jnp.zeros(dg_ref.shape, F32)

            du, dg_rows = _rms_bwd(acc, u_ref[...], g_ref[...])
            o_ref[...] = du + add_ref[...]
            dg_ref[...] += jnp.sum(dg_rows, axis=0, keepdims=True)
        else:
            rest[0][...] = acc

    n_out = 2 if norm else 1
    r_ins, r_in_specs, r_outs, r_out_specs, r_sems = _ride_args(ride)
    outs = pl.pallas_call(
        _riding(body, n + len(ws) + len(extra), n_out, 0, ride, 1), name=name, grid=(t // tm,),
        in_specs=[_rows(tm, dy.shape[1]) for dy in dys_all] + [_resident(w.shape) for w in ws]
        + ([_rows(tm, k), _whole((1, k)), _rows(tm, k)] if norm else []) + r_in_specs,
        out_specs=[_rows(tm, k)] + ([_whole((1, k))] if norm else []) + r_out_specs,
        out_shape=[jax.ShapeDtypeStruct((t, k), F32)] + ([jax.ShapeDtypeStruct((1, k), F32)] if norm else []) + r_outs,
        scratch_shapes=r_sems,
        compiler_params=_params(("arbitrary",)),
    )(*dys_all, *ws, *extra, *r_ins)
    return tuple(outs[:n_out]) + (_ride_results(ride, outs[n_out:]),)


def mm_tn(a, dys, name, col_shards=1, ride=(), tm=1024):
    t, k = a.shape
    tm = _tile(t, tm)
    n = len(dys)
    ntot = sum(dy.shape[1] for dy in dys)
    wsh = ntot // col_shards

    def body(a_ref, *refs):
        dy_refs, o_ref, acc = refs[:n], refs[n], refs[n + 1]

        @pl.when(pl.program_id(0) == 0)
        def _():
            acc[...] = jnp.zeros(acc.shape, F32)

        av = a_ref[...].astype(BF16)
        off = 0
        for dy_ref in dy_refs:
            width = dy_ref.shape[1]
            acc[:, off:off + width] += _dot_tn(av, dy_ref[...].astype(BF16))
            off += width

        @pl.when(pl.program_id(0) == pl.num_programs(0) - 1)
        def _():
            for j in range(col_shards):
                o_ref[j] = acc[:, j * wsh:(j + 1) * wsh].astype(o_ref.dtype)

    r_ins, r_in_specs, r_outs, r_out_specs, r_sems = _ride_args(ride)
    outs = pl.pallas_call(
        _riding(body, 1 + n, 1, 1, ride, 1), name=name, grid=(t // tm,),
        in_specs=[_rows(tm, k)] + [_rows(tm, dy.shape[1]) for dy in dys] + r_in_specs,
        out_specs=[_whole((col_shards, k, wsh))] + r_out_specs,
        out_shape=[jax.ShapeDtypeStruct((col_shards, k, wsh), BF16)] + r_outs,
        scratch_shapes=[pltpu.VMEM((k, ntot), F32)] + r_sems,
        compiler_params=_params(("arbitrary",)),
    )(a, *dys, *r_ins)
    return (outs[0], _ride_results(ride, outs[1:])) if ride else outs[0]


def _rms_bwd(dz, u, g):
    d = u.shape[-1]
    inv = lax.rsqrt(jnp.mean(u * u, axis=-1, keepdims=True) + EPS)
    dzg = dz * g
    proj = jnp.sum(dzg * u, axis=-1, keepdims=True) * (1.0 / d)
    du = inv * (dzg - u * (inv * inv) * proj)
    dg_rows = dz * u * inv
    return du, dg_rows


def ffn_down_loss(act, wd, g, h1, target, name, tm=512):
    t, f = act.shape
    d = wd.shape[1]
    tm = _tile(t, tm)

    def body(a_ref, w_ref, g_ref, h_ref, t_ref, dy_ref, dff_ref, dact_ref, dg_ref, loss_ref):
        @pl.when(pl.program_id(0) == 0)
        def _():
            dg_ref[...] = jnp.zeros(dg_ref.shape, F32)
            loss_ref[...] = jnp.zeros(loss_ref.shape, F32)

        wv = w_ref[...]
        gv = g_ref[...]
        ff = _dot(a_ref[...], wv)
        inv = lax.rsqrt(jnp.mean(ff * ff, axis=-1, keepdims=True) + EPS)
        err = h_ref[...] + ff * inv * gv - t_ref[...]
        loss_ref[...] += jnp.sum(err * err, axis=0, keepdims=True)
        dy = err * (1.0 / d)
        dy_ref[...] = dy
        du, dg_rows = _rms_bwd(dy, ff, gv)
        dff = du.astype(BF16)
        dff_ref[...] = dff
        dg_ref[...] += jnp.sum(dg_rows, axis=0, keepdims=True)
        dact_ref[...] = _dot_nt(dff, wv)

    return pl.pallas_call(
        body, name=name, grid=(t // tm,),
        in_specs=[_rows(tm, f), _resident(wd.shape), _whole(g.shape), _rows(tm, d), _rows(tm, d)],
        out_specs=[_rows(tm, d), _rows(tm, d), _rows(tm, f), _whole((1, d)), _whole((1, d))],
        out_shape=[jax.ShapeDtypeStruct((t, d), F32), jax.ShapeDtypeStruct((t, d), BF16),
                   jax.ShapeDtypeStruct((t, f), F32), jax.ShapeDtypeStruct((1, d), F32),
                   jax.ShapeDtypeStruct((1, d), F32)],
        compiler_params=_params(("arbitrary",)),
    )(act, wd, g, h1, target)


def merge_out(y_rnn, o_att, gts, w_br, w_ba, w_out, g, x, name, tm=512):
    t = y_rnn.shape[0]
    d = w_br.shape[1]
    tm = _tile(t, tm)

    def body(y_ref, o_ref, g_ref, wbr_ref, wba_ref, wo_ref, gn_ref, x_ref, m_ref, br_ref, ba_ref, mix_ref, h_ref):
        br = _dot(y_ref[...].astype(BF16), wbr_ref[...])
        ba = _dot(o_ref[...].astype(BF16), wba_ref[...])
        gv = g_ref[...]
        merged = (_sig(gv[:, :d]) * br + _sig(gv[:, d:]) * ba).astype(BF16)
        m_ref[...] = merged
        br_ref[...] = br
        ba_ref[...] = ba
        mix = _dot(merged, wo_ref[...])
        mix_ref[...] = mix
        inv = lax.rsqrt(jnp.mean(mix * mix, axis=-1, keepdims=True) + EPS)
        h_ref[...] = x_ref[...] + mix * inv * gn_ref[...]

    sd = jax.ShapeDtypeStruct
    return pl.pallas_call(
        body, name=name, grid=(t // tm,),
        in_specs=[_rows(tm, y_rnn.shape[1]), _rows(tm, o_att.shape[1]), _rows(tm, 2 * d),
                  _resident(w_br.shape), _resident(w_ba.shape), _resident(w_out.shape), _whole(g.shape), _rows(tm, d)],
        out_specs=[_rows(tm, d)] * 5,
        out_shape=[sd((t, d), BF16), sd((t, d), F32), sd((t, d), F32), sd((t, d), F32), sd((t, d), F32)],
        compiler_params=_params(("parallel",)),
    )(y_rnn, o_att, gts, w_br, w_ba, w_out, g, x)


def mid_bwd(dhn2, h1, g_ffn, dy, mix, g_mix, w_out, gts, br, ba, w_br, w_ba, merged, y_rnn, o_att, name, tm=256):
    t, d = h1.shape
    tm = _tile(t, tm)
    rnn, hkv = w_br.shape[0], w_ba.shape[0]
    wsh = d // N_CHIPS

    def body(dhn_ref, h_ref, gf_ref, dy_ref, mix_ref, gm_ref, wo_ref, g_ref, br_ref, ba_ref, wbr_ref, wba_ref,
             m_ref, y_ref, o_ref, dh_ref, dg_ref, dyr_ref, doa_ref, dgf_ref, dgm_ref, dwo_ref, dwbr_ref, dwba_ref,
             acc_o, acc_br, acc_ba):
        @pl.when(pl.program_id(0) == 0)
        def _():
            dgf_ref[...] = jnp.zeros(dgf_ref.shape, F32)
            dgm_ref[...] = jnp.zeros(dgm_ref.shape, F32)
            acc_o[...] = jnp.zeros(acc_o.shape, F32)
            acc_br[...] = jnp.zeros(acc_br.shape, F32)
            acc_ba[...] = jnp.zeros(acc_ba.shape, F32)

        du, rows_f = _rms_bwd(dhn_ref[...], h_ref[...], gf_ref[...])
        dh1 = du + dy_ref[...]
        dh_ref[...] = dh1
        dgf_ref[...] += jnp.sum(rows_f, axis=0, keepdims=True)
        dmx, rows_m = _rms_bwd(dh1, mix_ref[...], gm_ref[...])
        dmix = dmx.astype(BF16)
        acc_o[...] += _dot_tn(m_ref[...], dmix)
        dgm_ref[...] += jnp.sum(rows_m, axis=0, keepdims=True)
        dm = _dot_nt(dmix, wo_ref[...])
        gv = g_ref[...]
        sr = _sig(gv[:, :d])
        sa = _sig(gv[:, d:])
        dbr = (dm * sr).astype(BF16)
        dba = (dm * sa).astype(BF16)
        acc_br[...] += _dot_tn(y_ref[...].astype(BF16), dbr)
        acc_ba[...] += _dot_tn(o_ref[...].astype(BF16), dba)
        dg_ref[:, :d] = (dm * br_ref[...] * sr * (1.0 - sr)).astype(BF16)
        dg_ref[:, d:] = (dm * ba_ref[...] * sa * (1.0 - sa)).astype(BF16)
        dyr_ref[...] = _dot_nt(dbr, wbr_ref[...])
        doa_ref[...] = _dot_nt(dba, wba_ref[...])

        @pl.when(pl.program_id(0) == pl.num_programs(0) - 1)
        def _():
            dwo_ref[...] = acc_o[...].astype(BF16)
            dwbr_ref[...] = acc_br[...].astype(BF16)
            for j in range(N_CHIPS):
                dwba_ref[j] = acc_ba[:, j * wsh:(j + 1) * wsh].astype(BF16)

    sd = jax.ShapeDtypeStruct
    row, vec = _rows(tm, d), _whole((1, d))
    once = pl.Buffered(1)

    def resident(shape):
        return pl.BlockSpec(shape, lambda i: (0,) * len(shape), pipeline_mode=once)

    return pl.pallas_call(
        body, name=name, grid=(t // tm,),
        in_specs=[row, row, vec, row, row, vec, resident(w_out.shape), _rows(tm, 2 * d), row, row,
                  resident(w_br.shape), resident(w_ba.shape), row, _rows(tm, rnn), _rows(tm, hkv)],
        out_specs=[row, _rows(tm, 2 * d), _rows(tm, rnn), _rows(tm, hkv), vec, vec,
                   resident((d, d)), resident((rnn, d)), resident((N_CHIPS, hkv, wsh))],
        out_shape=[sd((t, d), F32), sd((t, 2 * d), BF16), sd((t, rnn), F32), sd((t, hkv), F32), sd((1, d), F32),
                   sd((1, d), F32), sd((d, d), BF16), sd((rnn, d), BF16), sd((N_CHIPS, hkv, wsh), BF16)],
        scratch_shapes=[pltpu.VMEM((d, d), F32), pltpu.VMEM((rnn, d), F32), pltpu.VMEM((hkv, d), F32)],
        compiler_params=_params(("arbitrary",)),
    )(dhn2, h1, g_ffn, dy, mix, g_mix, w_out, gts, br, ba, w_br, w_ba, merged, y_rnn, o_att)


def _shift_dn(x, d, fill, row):
    return jnp.where(row >= d, pltpu.roll(x, d, 0), fill)


def _shift_up(x, d, fill, row):
    s = x.shape[0]
    return jnp.where(row < s - d, pltpu.roll(x, s - d, 0), fill)


def _conv_fwd(x, w, b, row):
    kk = w.shape[0]
    y = b + w[kk - 1:kk, :] * x
    for j in range(1, kk):
        y = y + w[kk - 1 - j:kk - j, :] * _shift_dn(x, j, 0.0, row)
    return y


def _conv_bwd(dy, x, w, row):
    kk = w.shape[0]
    dx = w[kk - 1:kk, :] * dy
    dws = [None] * kk
    dws[kk - 1] = jnp.sum(dy * x, axis=0, keepdims=True)
    for j in range(1, kk):
        ahead = _shift_up(dy, j, 0.0, row)
        dx = dx + w[kk - 1 - j:kk - j, :] * ahead
        dws[kk - 1 - j] = jnp.sum(ahead * x, axis=0, keepdims=True)
    return dx, jnp.concatenate(dws, axis=0)


def _softplus(z):
    y = jnp.exp(-jnp.abs(z))
    u = 1.0 + y
    dd = u - 1.0
    log1p = jnp.where(dd == 0.0, y, jnp.log(u) * (y / jnp.where(dd == 0.0, 1.0, dd)))
    return jnp.maximum(z, 0.0) + log1p


def _lru_decay(xb, wa, ba, lam):
    r = _sig(_dot(xb, wa) + ba)
    sp = _softplus(-lam)
    la = (-LRU_C) * r * sp
    return r, sp, la, jnp.exp(la)


def _lru_gates(xc, wa, ba, wx, bx, lam):
    xb = xc.astype(BF16)
    r, sp, la, a = _lru_decay(xb, wa, ba, lam)
    i = _sig(_dot(xb, wx) + bx)
    one_m_a2 = jnp.tanh(-la) * (1.0 + a * a)
    inv_mult = lax.rsqrt(one_m_a2)
    return r, i, sp, a, one_m_a2 * inv_mult, inv_mult


def _seg_len(s):
    seg = -(-s // 8)
    return seg + (4 - seg % 8) % 8


def _scan_rows(a_pad, u_pad, out_pad, reverse):
    planes, rows8, lanes = a_pad.shape
    seg = rows8 // 8
    sub = lax.broadcasted_iota(jnp.int32, (planes, 8, lanes), 1)

    unroll = 4

    def rows(k, d):
        i = k * unroll + d
        return pl.ds((seg - 1 - i) if reverse else i, 8, stride=seg)

    def ends(k, carry):
        h, p = carry
        for d in range(unroll):
            a = a_pad[:, rows(k, d), :]
            h = a * h + u_pad[:, rows(k, d), :]
            p = a * p
        return h, p

    init = (jnp.zeros((planes, 8, lanes), F32), jnp.ones((planes, 8, lanes), F32))
    h_end, p_end = lax.fori_loop(0, seg // unroll, ends, init)
    start = jnp.zeros((planes, 8, lanes), F32)
    for _ in range(7):
        nxt = h_end + p_end * start
        if reverse:
            start = jnp.where(sub < 7, pltpu.roll(nxt, 7, 1), 0.0)
        else:
            start = jnp.where(sub >= 1, pltpu.roll(nxt, 1, 1), 0.0)

    def redo(k, h):
        for d in range(unroll):
            h = a_pad[:, rows(k, d), :] * h + u_pad[:, rows(k, d), :]
            out_pad[:, rows(k, d), :] = h
        return h

    lax.fori_loop(0, seg // unroll, redo, start)


def _lru_cols(c, rb):
    return 2 * rb if c % (2 * rb) == 0 else rb


def rglru_fwd(xr, cw, cb, wa, ba, wx, bx, lam, name, ride=()):
    b, s, c = xr.shape
    rb = wa.shape[1]
    kk = cw.shape[0]
    cols = _lru_cols(c, rb)
    nj = cols // rb
    seg = _seg_len(s)

    def body(x_ref, cw_ref, cb_ref, wa_ref, ba_ref, wx_ref, bx_ref, lam_ref, h_ref, a_ref, xc_ref, a_pad, u_pad, h_pad):
        row = lax.broadcasted_iota(jnp.int32, (s, rb), 0)
        for j in range(nj):
            cs = slice(j * rb, (j + 1) * rb)
            xc = _conv_fwd(x_ref[:, cs], cw_ref[:, cs], cb_ref[:, cs], row)
            _, i, _, a, mult, _ = _lru_gates(xc, wa_ref[j], ba_ref[:, cs], wx_ref[j], bx_ref[:, cs], lam_ref[:, cs])
            xc_ref[:, cs] = xc
            a_ref[:, cs] = a
            a_pad[j, 0:s, :] = a
            u_pad[j, 0:s, :] = mult * (i * xc)
        a_pad[:, s:, :] = jnp.ones((nj, 8 * seg - s, rb), F32)
        u_pad[:, s:, :] = jnp.zeros((nj, 8 * seg - s, rb), F32)
        _scan_rows(a_pad, u_pad, h_pad, False)
        for j in range(nj):
            h_ref[:, j * rb:(j + 1) * rb] = h_pad[j, 0:s, :]

    vec = pl.BlockSpec((1, cols), lambda bi, n: (0, n))
    seq = pl.BlockSpec((None, s, cols), lambda bi, n: (bi, 0, n))
    mat = pl.BlockSpec((nj, rb, rb), lambda bi, n: (n, 0, 0))
    r_ins, r_in_specs, r_outs, r_out_specs, r_sems = _ride_args(ride)
    outs = pl.pallas_call(
        _riding(body, 8, 3, 3, ride, 2), name=name, grid=(b, c // cols),
        in_specs=[seq, pl.BlockSpec((kk, cols), lambda bi, n: (0, n)), vec, mat, vec, mat, vec, vec] + r_in_specs,
        out_specs=[seq] * 3 + r_out_specs,
        out_shape=[jax.ShapeDtypeStruct((b, s, c), F32)] * 3 + r_outs,
        scratch_shapes=[pltpu.VMEM((nj, 8 * seg, rb), F32)] * 3 + r_sems,
        compiler_params=_params(("arbitrary", "arbitrary")),
    )(xr, cw, cb, wa, ba, wx, bx, lam, *r_ins)
    return outs[:3], _ride_results(ride, outs[3:])


def rglru_bwd(xr, h, dh, a_fwd, xc_fwd, cw, wa, ba, wx, bx, lam, name, ride=()):
    b, s, c = xr.shape
    nb, rb = wa.shape[0], wa.shape[1]
    kk = cw.shape[0]
    cols = _lru_cols(c, rb)
    nj = cols // rb
    seg = _seg_len(s)

    def body(x_ref, h_ref, dh_ref, a_ref, xc_ref, cw_ref, wa_ref, ba_ref, wx_ref, bx_ref, lam_ref,
             dx_ref, dcw_ref, dcb_ref, dwa_ref, dba_ref, dwx_ref, dbx_ref, dlam_ref, b_pad, g_pad, l_pad):
        @pl.when(pl.program_id(1) == 0)
        def _():
            for ref in (dcw_ref, dcb_ref, dwa_ref, dba_ref, dwx_ref, dbx_ref, dlam_ref):
                ref[...] = jnp.zeros(ref.shape, F32)

        row = lax.broadcasted_iota(jnp.int32, (s, rb), 0)

        for j in range(nj):
            b_pad[j, 0:s, :] = _shift_up(a_ref[:, j * rb:(j + 1) * rb], 1, 0.0, row)
            g_pad[j, 0:s, :] = dh_ref[:, j * rb:(j + 1) * rb]
        b_pad[:, s:, :] = jnp.zeros((nj, 8 * seg - s, rb), F32)
        g_pad[:, s:, :] = jnp.zeros((nj, 8 * seg - s, rb), F32)
        _scan_rows(b_pad, g_pad, l_pad, True)

        for j in range(nj):
            cs = slice(j * rb, (j + 1) * rb)
            x = x_ref[:, cs]
            cwv = cw_ref[:, cs]
            wav, wxv, lamv = wa_ref[j], wx_ref[j], lam_ref[:, cs]
            xc = xc_ref[:, cs]
            r, i, sp, a, mult, inv_mult = _lru_gates(xc, wav, ba_ref[:, cs], wxv, bx_ref[:, cs], lamv)
            lmb = l_pad[j, 0:s, :]
            h_prev = _shift_dn(h_ref[:, cs], 1, 0.0, row)
            da = lmb * h_prev
            ixc = i * xc
            dla = da * a - (lmb * ixc) * (a * a) * inv_mult
            di = lmb * mult * xc
            dxc = lmb * mult * i
            dr = dla * ((-LRU_C) * sp)
            dsp = jnp.sum(dla * ((-LRU_C) * r), axis=0, keepdims=True)
            dga = dr * r * (1.0 - r)
            dgx = di * i * (1.0 - i)
            dga_b, dgx_b = dga.astype(BF16), dgx.astype(BF16)
            xb = xc.astype(BF16)
            dwa_ref[j] += _dot_tn(xb, dga_b)
            dwx_ref[j] += _dot_tn(xb, dgx_b)
            dba_ref[:, cs] += jnp.sum(dga, axis=0, keepdims=True)
            dbx_ref[:, cs] += jnp.sum(dgx, axis=0, keepdims=True)
            dlam_ref[:, cs] += dsp * (-_sig(-lamv))
            dxc = dxc + _dot_nt(dga_b, wav) + _dot_nt(dgx_b, wxv)
            dcb_ref[:, cs] += jnp.sum(dxc, axis=0, keepdims=True)
            dx, dcw = _conv_bwd(dxc, x, cwv, row)
            dcw_ref[:, cs] += dcw
            dx_ref[:, cs] = dx.astype(dx_ref.dtype)

    vec = pl.BlockSpec((1, cols), lambda n, bi: (0, n))
    seq = pl.BlockSpec((None, s, cols), lambda n, bi: (bi, 0, n))
    mat = pl.BlockSpec((nj, rb, rb), lambda n, bi: (n, 0, 0))
    cws = pl.BlockSpec((kk, cols), lambda n, bi: (0, n))
    sd = jax.ShapeDtypeStruct
    r_ins, r_in_specs, r_outs, r_out_specs, r_sems = _ride_args(ride)
    outs = pl.pallas_call(
        _riding(body, 11, 8, 3, ride, 2), name=name, grid=(c // cols, b),
        in_specs=[seq, seq, seq, seq, seq, cws, mat, vec, mat, vec, vec] + r_in_specs,
        out_specs=[seq, cws, vec, mat, vec, mat, vec, vec] + r_out_specs,
        out_shape=[sd((b, s, c), BF16), sd((kk, c), F32), sd((1, c), F32), sd((nb, rb, rb), F32),
                   sd((1, c), F32), sd((nb, rb, rb), F32), sd((1, c), F32), sd((1, c), F32)] + r_outs,
        scratch_shapes=[pltpu.VMEM((nj, 8 * seg, rb), F32)] * 3 + r_sems,
        compiler_params=_params(("arbitrary", "arbitrary")),
    )(xr, h, dh, a_fwd, xc_fwd, cw, wa, ba, wx, bx, lam, *r_ins)
    return outs[:8], _ride_results(ride, outs[8:])


_GELU_C = math.sqrt(2.0 / math.pi)


def _gelu_parts(x):
    th = jnp.tanh(_GELU_C * (x + 0.044715 * x * x * x))
    gel = 0.5 * x * (1.0 + th)
    dgel = 0.5 * (1.0 + th) + 0.5 * x * (1.0 - th * th) * _GELU_C * (1.0 + 3 * 0.044715 * x * x)
    return gel, dgel


def ffn_in_act(x, g, wg, wu, cw, cb, seq_len, name, tm=256):
    t, d = x.shape
    f = N_CHIPS * wg.shape[2]
    kk = cw.shape[0]
    tm = _tile(seq_len, tm)
    tiles_per_seq = seq_len // tm
    keep = 8
    assert kk - 1 <= keep

    def body(x_ref, g_ref, wg_ref, wu_ref, cw_ref, cb_ref, hn_ref, gp_ref, up_ref, act_ref, tail):
        @pl.when(pl.program_id(0) % tiles_per_seq == 0)
        def _():
            tail[...] = jnp.zeros(tail.shape, F32)

        xv = x_ref[...]
        inv = lax.rsqrt(jnp.mean(xv * xv, axis=-1, keepdims=True) + EPS)
        hn = (xv * inv * g_ref[...]).astype(BF16)
        hn_ref[...] = hn
        gp = jnp.concatenate([_dot(hn, wg_ref[j]) for j in range(N_CHIPS)], axis=1)
        up = jnp.concatenate([_dot(hn, wu_ref[j]) for j in range(N_CHIPS)], axis=1)
        gp_ref[...] = gp
        up_ref[...] = up
        cwv = cw_ref[...]
        row = lax.broadcasted_iota(jnp.int32, (tm, 1), 0)
        gate = _conv_fwd(gp, cwv, cb_ref[...], row)
        row8 = lax.broadcasted_iota(jnp.int32, (keep, 1), 0)
        prev = tail[...]
        fix = jnp.zeros((keep, f), F32)
        for j in range(1, kk):
            fix = fix + cwv[kk - 1 - j:kk - j, :] * jnp.where(row8 < j, pltpu.roll(prev, j, 0), 0.0)
        gate = jnp.concatenate([gate[:keep] + fix, gate[keep:]], axis=0)
        tail[...] = gp[tm - keep:, :]
        gel, _ = _gelu_parts(gate)
        act_ref[...] = (gel * up).astype(BF16)

    sd = jax.ShapeDtypeStruct
    return pl.pallas_call(
        body, name=name, grid=(t // tm,),
        in_specs=[_rows(tm, d), _whole(g.shape), _whole(wg.shape), _whole(wu.shape), _whole(cw.shape), _whole(cb.shape)],
        out_specs=[_rows(tm, d), _rows(tm, f), _rows(tm, f), _rows(tm, f)],
        out_shape=[sd((t, d), BF16), sd((t, f), F32), sd((t, f), F32), sd((t, f), BF16)],
        scratch_shapes=[pltpu.VMEM((keep, f), F32)],
        compiler_params=_params(("arbitrary",)),
    )(x, g, wg, wu, cw, cb)


def ffn_in_bwd(dact, gate_pre, up, cw, cb, wg, wu, seq_len, name, ride=(), tm=256):
    t, f = gate_pre.shape
    d = wg.shape[1]
    fs = f // N_CHIPS
    kk = cw.shape[0]
    tm = _tile(seq_len, tm)
    nt = t // tm
    tiles_per_seq = seq_len // tm
    keep = 8
    assert kk - 1 <= keep

    def body(da_ref, g_ref, halo_ref, u_ref, cw_ref, cb_ref, wg_ref, wu_ref,
             dg_ref, du_ref, dhn_ref, dcw_ref, dcb_ref, nxt):
        tile = (nt - 1 - pl.program_id(0)) % tiles_per_seq

        @pl.when(pl.program_id(0) == 0)
        def _():
            dcw_ref[...] = jnp.zeros(dcw_ref.shape, F32)
            dcb_ref[...] = jnp.zeros(dcb_ref.shape, F32)

        @pl.when(tile == tiles_per_seq - 1)
        def _():
            nxt[...] = jnp.zeros(nxt.shape, F32)

        row = lax.broadcasted_iota(jnp.int32, (tm, 1), 0)
        row8 = lax.broadcasted_iota(jnp.int32, (keep, 1), 0)
        gp = g_ref[...]
        cwv = cw_ref[...]
        prev = jnp.where(tile > 0, halo_ref[...], 0.0)
        gate = _conv_fwd(gp, cwv, cb_ref[...], row)
        fix = jnp.zeros((keep, f), F32)
        for j in range(1, kk):
            fix = fix + cwv[kk - 1 - j:kk - j, :] * jnp.where(row8 < j, pltpu.roll(prev, j, 0), 0.0)
        gate = jnp.concatenate([gate[:keep] + fix, gate[keep:]], axis=0)
        gel, dgel = _gelu_parts(gate)
        da = da_ref[...]
        dup = (da * gel).astype(BF16)
        du_ref[...] = dup
        dgate = da * u_ref[...] * dgel
        dcb_ref[...] += jnp.sum(dgate, axis=0, keepdims=True)
        after = nxt[...]
        dgp = cwv[kk - 1:kk, :] * dgate
        tail_fix = jnp.zeros((keep, f), F32)
        dws = [None] * kk
        dws[kk - 1] = jnp.sum(dgate * gp, axis=0, keepdims=True)
        for j in range(1, kk):
            wj = cwv[kk - 1 - j:kk - j, :]
            dgp = dgp + wj * _shift_up(dgate, j, 0.0, row)
            tail_fix = tail_fix + wj * jnp.where(row8 >= keep - j, pltpu.roll(after, keep - j, 0), 0.0)
            dws[kk - 1 - j] = (jnp.sum(dgate * _shift_dn(gp, j, 0.0, row), axis=0, keepdims=True)
                               + jnp.sum(dgate[:keep] * jnp.where(row8 < j, pltpu.roll(prev, j, 0), 0.0),
                                         axis=0, keepdims=True))
        dgp = jnp.concatenate([dgp[:tm - keep], dgp[tm - keep:] + tail_fix], axis=0).astype(BF16)
        nxt[...] = dgate[:keep]
        dcw_ref[...] += jnp.concatenate(dws, axis=0)
        dg_ref[...] = dgp
        dhn = None
        for j in range(N_CHIPS):
            cs = slice(j * fs, (j + 1) * fs)
            part = _dot_nt(dgp[:, cs], wg_ref[j]) + _dot_nt(dup[:, cs], wu_ref[j])
            dhn = part if dhn is None else dhn + part
        dhn_ref[...] = dhn

    def rev(i):
        return nt - 1 - i

    rows_f = pl.BlockSpec((tm, f), lambda i: (rev(i), 0))
    halo = pl.BlockSpec((None, keep, f), lambda i: (jnp.maximum(rev(i) * (tm // keep) - 1, 0), 0, 0))
    once = pl.Buffered(1)
    sd = jax.ShapeDtypeStruct
    r_ins, r_in_specs, r_outs, r_out_specs, r_sems = _ride_args(ride)
    outs = pl.pallas_call(
        _riding(body, 8, 5, 1, ride, 1), name=name, grid=(nt,),
        in_specs=[rows_f, rows_f, halo, rows_f, _whole(cw.shape), _whole(cb.shape),
                  pl.BlockSpec(wg.shape, lambda i: (0, 0, 0), pipeline_mode=once),
                  pl.BlockSpec(wu.shape, lambda i: (0, 0, 0), pipeline_mode=once)] + r_in_specs,
        out_specs=[rows_f, rows_f, pl.BlockSpec((tm, d), lambda i: (rev(i), 0)), _whole((kk, f)), _whole((1, f))]
        + r_out_specs,
        out_shape=[sd((t, f), BF16), sd((t, f), BF16), sd((t, d), F32), sd((kk, f), F32), sd((1, f), F32)] + r_outs,
        scratch_shapes=[pltpu.VMEM((keep, f), F32)] + r_sems,
        compiler_params=_params(("arbitrary",)),
    )(dact, gate_pre, gate_pre.reshape(t // keep, keep, f), up, cw, cb, wg, wu, *r_ins)
    return outs[:5], _ride_results(ride, outs[5:])


def _t5_bucket(dist):
    max_exact = REL_BUCKETS // 2
    d = np.maximum(dist, 1).astype(np.float32)
    large = max_exact + np.log(d / max_exact) / math.log(REL_MAX_DIST / max_exact) * (REL_BUCKETS - max_exact)
    large = np.minimum(large.astype(np.int32), REL_BUCKETS - 1)
    return np.where(dist < max_exact, dist, large).astype(np.int32)


def _band(window, dilation):
    qi = np.arange(ATTN_BLOCK)[:, None]
    kj = np.arange(2 * ATTN_BLOCK)[None, :]
    delta = ATTN_BLOCK + qi - kj
    mask = (delta >= 0) & (delta <= window // dilation)
    bucket = _t5_bucket(np.maximum(delta, 0) * dilation)
    return mask, bucket


def _attn_blocks(s, r):
    m = s // r
    assert m % ATTN_BLOCK == 0, "sequence length must be a multiple of dilation * block"
    return m // ATTN_BLOCK


def _perm_load(ref, r):
    if r == 1:
        return ref[...]
    m = ref.shape[0] // r
    return jnp.concatenate([ref[pl.ds(c, m, stride=r), :] for c in range(r)], axis=0)


def _perm_store(ref, g, val, r, add=False):
    if r == 1:
        ref[g] = ref[g] + val if add else val
        return
    m = val.shape[0] // r
    for c in range(r):
        rows = pl.ds(c, m, stride=r)
        part = val[c * m:(c + 1) * m]
        ref[g, rows, :] = ref[g, rows, :] + part if add else part


def _blocks(x):
    return x.reshape(x.shape[0] // ATTN_BLOCK, ATTN_BLOCK, x.shape[1])


def _prev_blocks(x):
    return jnp.concatenate([x[:1], x[:-1]], axis=0)


def _next_blocks(x):
    return jnp.concatenate([x[1:], jnp.zeros_like(x[:1])], axis=0)


def _first_block_neg(s, r):
    nblk = s // ATTN_BLOCK
    idx = lax.broadcasted_iota(jnp.int32, (nblk, 1, 1), 0)
    return jnp.where(idx % _attn_blocks(s, r) == 0, NEG, 0.0)


def _bdot_nt(a, b):
    return lax.dot_general(a, b, (((2,), (2,)), ((0,), (0,))), preferred_element_type=F32)


def _bdot(a, b):
    return lax.dot_general(a, b, (((2,), (1,)), ((0,), (0,))), preferred_element_type=F32)


def _bdot_tn(a, b):
    return lax.dot_general(a, b, (((1,), (1,)), ((0,), (0,))), preferred_element_type=F32)


def attn_fwd(qkv, biasm, n_heads, name, ride=()):
    b, s, _ = qkv.shape
    h = n_heads
    scale = HEAD_DIM ** -0.5
    blk = ATTN_BLOCK

    def body(q1_ref, q2_ref, q3_ref, k_ref, v_ref, bias_ref, o_ref, lse_ref, acc, m_s, l_s):
        for g, q_ref in enumerate((q1_ref, q2_ref, q3_ref)):
            r = DILATED[g][1]
            first = _first_block_neg(s, r)
            q = _blocks(_perm_load(q_ref, r).astype(BF16))
            k = _blocks(_perm_load(k_ref, r).astype(BF16))
            v = _blocks(_perm_load(v_ref, r).astype(BF16))
            s_cur = _bdot_nt(q, k) * scale + bias_ref[g, :, blk:]
            s_prev = _bdot_nt(q, _prev_blocks(k)) * scale + bias_ref[g, :, :blk] + first
            m = jnp.max(jnp.maximum(s_cur, s_prev), axis=-1, keepdims=True)
            p_cur = jnp.exp(s_cur - m)
            p_prev = jnp.exp(s_prev - m)
            l = jnp.sum(p_cur + p_prev, axis=-1, keepdims=True)
            o = _bdot(p_cur.astype(BF16), v) + _bdot(p_prev.astype(BF16), _prev_blocks(v))
            _perm_store(acc, g, o.reshape(s, HEAD_DIM), r)
            _perm_store(m_s, g, m.reshape(s, 1), r)
            _perm_store(l_s, g, l.reshape(s, 1), r)
        m_all = jnp.maximum(jnp.maximum(m_s[0], m_s[1]), m_s[2])
        w = [jnp.exp(m_s[g] - m_all) for g in range(N_GROUPS)]
        l = w[0] * l_s[0] + w[1] * l_s[1] + w[2] * l_s[2]
        o_ref[...] = (w[0] * acc[0] + w[1] * acc[1] + w[2] * acc[2]) / l
        lse_ref[...] = m_all + jnp.log(l)

    def col(j):
        return pl.BlockSpec((None, s, HEAD_DIM), lambda bi, hi, j=j: (bi, 0, j * h + hi))

    r_ins, r_in_specs, r_outs, r_out_specs, r_sems = _ride_args(ride)
    outs = pl.pallas_call(
        _riding(body, 6, 2, 3, ride, 2), name=name, grid=(b, h),
        in_specs=[col(0), col(1), col(2), col(3), col(4),
                  pl.BlockSpec((N_GROUPS, None, blk, 2 * blk), lambda bi, hi: (0, hi, 0, 0))] + r_in_specs,
        out_specs=[pl.BlockSpec((None, s, HEAD_DIM), lambda bi, hi: (bi, 0, hi)),
                   pl.BlockSpec((None, None, s, 1), lambda bi, hi: (bi, hi, 0, 0))] + r_out_specs,
        out_shape=[jax.ShapeDtypeStruct((b, s, h * HEAD_DIM), F32), jax.ShapeDtypeStruct((b, h, s, 1), F32)] + r_outs,
        scratch_shapes=[pltpu.VMEM((N_GROUPS, s, HEAD_DIM), F32), pltpu.VMEM((N_GROUPS, s, 1), F32),
                        pltpu.VMEM((N_GROUPS, s, 1), F32)] + r_sems,
        compiler_params=_params(("arbitrary", "arbitrary")),
    )(qkv, qkv, qkv, qkv, qkv, biasm, *r_ins)
    return outs[0], outs[1], _ride_results(ride, outs[2:])


def attn_bwd(qkv, biasm, o, lse, do, n_heads, name, ride=()):
    b, s, _ = qkv.shape
    h = n_heads
    scale = HEAD_DIM ** -0.5
    blk = ATTN_BLOCK

    def body(q1_ref, q2_ref, q3_ref, k_ref, v_ref, bias_ref, o_ref, lse_ref, do_ref,
             dq1_ref, dq2_ref, dq3_ref, dk_ref, dv_ref, ds_ref, dq_acc, kv_acc, delta):
        delta[...] = jnp.sum(do_ref[...] * o_ref[...], axis=-1, keepdims=True)
        kv_acc[...] = jnp.zeros(kv_acc.shape, F32)
        for g, q_ref in enumerate((q1_ref, q2_ref, q3_ref)):
            r = DILATED[g][1]
            first = _first_block_neg(s, r)
            q = _blocks(_perm_load(q_ref, r).astype(BF16))
            k = _blocks(_perm_load(k_ref, r).astype(BF16))
            v = _blocks(_perm_load(v_ref, r).astype(BF16))
            dob = _blocks(_perm_load(do_ref, r).astype(BF16))
            lse_b = _blocks(_perm_load(lse_ref, r))
            dl_b = _blocks(_perm_load(delta, r))
            k_prev, v_prev = _prev_blocks(k), _prev_blocks(v)
            p_cur = jnp.exp(_bdot_nt(q, k) * scale + bias_ref[g, :, blk:] - lse_b)
            p_prev = jnp.exp(_bdot_nt(q, k_prev) * scale + bias_ref[g, :, :blk] + first - lse_b)
            ds_cur = p_cur * (_bdot_nt(dob, v) - dl_b)
            ds_prev = p_prev * (_bdot_nt(dob, v_prev) - dl_b)
            ds_ref[g, :, blk:] = jnp.sum(ds_cur, axis=0)
            ds_ref[g, :, :blk] = jnp.sum(ds_prev, axis=0)
            ds_cur_b, ds_prev_b = ds_cur.astype(BF16), ds_prev.astype(BF16)
            dq = (_bdot(ds_cur_b, k) + _bdot(ds_prev_b, k_prev)) * scale
            _perm_store(dq_acc, g, dq.reshape(s, HEAD_DIM), r)
            dk = (_bdot_tn(ds_cur_b, q) + _next_blocks(_bdot_tn(ds_prev_b, q))) * scale
            dv = _bdot_tn(p_cur.astype(BF16), dob) + _next_blocks(_bdot_tn(p_prev.astype(BF16), dob))
            _perm_store(kv_acc, 0, dk.reshape(s, HEAD_DIM), r, add=True)
            _perm_store(kv_acc, 1, dv.reshape(s, HEAD_DIM), r, add=True)
        for g, out_ref in enumerate((dq1_ref, dq2_ref, dq3_ref)):
            out_ref[...] = dq_acc[g].astype(out_ref.dtype)
        dk_ref[...] = kv_acc[0].astype(dk_ref.dtype)
        dv_ref[...] = kv_acc[1].astype(dv_ref.dtype)

    def col(j):
        return pl.BlockSpec((None, s, HEAD_DIM), lambda bi, hi, j=j: (bi, 0, j * h + hi))

    head = pl.BlockSpec((None, s, HEAD_DIM), lambda bi, hi: (bi, 0, hi))
    sd = jax.ShapeDtypeStruct
    r_ins, r_in_specs, r_outs, r_out_specs, r_sems = _ride_args(ride)
    outs = pl.pallas_call(
        _riding(body, 9, 6, 3, ride, 2), name=name, grid=(b, h),
        in_specs=[col(0), col(1), col(2), col(3), col(4),
                  pl.BlockSpec((N_GROUPS, None, blk, 2 * blk), lambda bi, hi: (0, hi, 0, 0)),
                  head, pl.BlockSpec((None, None, s, 1), lambda bi, hi: (bi, hi, 0, 0)), head] + r_in_specs,
        out_specs=[head] * 5 + [pl.BlockSpec((None, None, N_GROUPS, blk, 2 * blk), lambda bi, hi: (bi, hi, 0, 0, 0))]
        + r_out_specs,
        out_shape=[sd((b, s, h * HEAD_DIM), BF16)] * 5 + [sd((b, h, N_GROUPS, blk, 2 * blk), F32)] + r_outs,
        scratch_shapes=[pltpu.VMEM((N_GROUPS, s, HEAD_DIM), F32), pltpu.VMEM((2, s, HEAD_DIM), F32),
                        pltpu.VMEM((s, 1), F32)] + r_sems,
        compiler_params=_params(("arbitrary", "arbitrary")),
    )(qkv, qkv, qkv, qkv, qkv, biasm, o, lse, do, *r_ins)
    return outs[:6], _ride_results(ride, outs[6:])


def bias_table(rel_rows, bucket_f, n_heads, name):
    g, blk, blk2 = bucket_f.shape
    h = n_heads

    def body(rb_ref, bk_ref, o_ref):
        for gi in range(g):
            bk = bk_ref[gi]
            for hi in range(h):
                rb = rb_ref[gi * h + hi]
                acc = jnp.full((blk, blk2), NEG, F32)
                for bucket in range(REL_BUCKETS):
                    acc = jnp.where(bk == float(bucket), rb[:, bucket:bucket + 1], acc)
                o_ref[gi, hi] = acc

    vmem = pl.BlockSpec(memory_space=pltpu.VMEM)
    return pl.pallas_call(
        body, name=name, in_specs=[vmem, vmem], out_specs=vmem,
        out_shape=jax.ShapeDtypeStruct((g, h, blk, blk2), F32),
        compiler_params=_params(),
    )(rel_rows, bucket_f)


def bias_grad(ds_sum, bucket_f, name):
    b, h, g, blk, blk2 = ds_sum.shape

    def body(ds_ref, bk_ref, o_ref):
        tot = jnp.sum(ds_ref[...], axis=0)
        bk = bk_ref[...]
        lane = lax.broadcasted_iota(jnp.int32, (1, 128), 1)
        vec = jnp.zeros((1, 128), F32)
        for bucket in range(REL_BUCKETS):
            val = jnp.sum(jnp.where(bk == float(bucket), tot, 0.0), keepdims=True)
            vec = vec + jnp.where(lane == bucket, val, 0.0)
        o_ref[...] = vec

    return pl.pallas_call(
        body, name=name, grid=(g, h),
        in_specs=[pl.BlockSpec((b, None, None, blk, blk2), lambda gi, hi: (0, hi, gi, 0, 0)),
                  pl.BlockSpec((None, blk, blk2), lambda gi, hi: (gi, 0, 0))],
        out_specs=pl.BlockSpec((None, 1, 128), lambda gi, hi: (gi * h + hi, 0, 0)),
        out_shape=jax.ShapeDtypeStruct((g * h, 1, 128), F32),
        compiler_params=_params(("parallel", "parallel")),
    )(ds_sum, bucket_f)


def _chip_peers():
    x, y, c = lax.axis_index("x"), lax.axis_index("y"), lax.axis_index("c")
    me = 2 * x + y
    peers = [(1 - x, y, c), (x, 1 - y, c), (1 - x, 1 - y, c)]
    peer_chip = [2 * (1 - x) + y, 2 * x + (1 - y), 2 * (1 - x) + (1 - y)]
    return me, peers, peer_chip


def _any_specs(n):
    return [pl.BlockSpec(memory_space=pl.ANY)] * n


_MID_NUM, _MID_DEN = 3, 4


class _Exchange:
    def start(self, ins, outs, sems):
        local, sends, _ = self._copies(ins, outs, sems)
        for cp in local + sends:
            cp.start()

    def mid(self, ins, outs, sems):
        pass

    def wait(self, ins, outs, sems):
        local, sends, recvs = self._copies(ins, outs, sems)
        for cp in recvs():
            cp.wait_recv()
        for cp in sends:
            cp.wait_send()
        for cp in local:
            cp.wait()


class _Gather(_Exchange):
    HALF_ROWS = 16

    def __init__(self, arrays):
        n = len(arrays)
        self.ins = list(arrays)
        self.split = [a.shape[0] % (2 * self.HALF_ROWS) == 0 for a in arrays]
        self.out_shape = [jax.ShapeDtypeStruct((N_CHIPS,) + a.shape, a.dtype) for a in arrays]
        dma = pltpu.SemaphoreType.DMA
        self.sems = [dma((3 * n,)), dma((3 * n,)), dma((n,)), dma((3 * n,)), dma((3 * n,))]

    def _half(self, i, ref, sibling=False):
        if not self.split[i]:
            return ref
        half = self.ins[i].shape[0] // 2
        c = lax.axis_index("c")
        c = 1 - c if sibling else c
        return ref.at[pl.ds(pl.multiple_of(c * half, self.HALF_ROWS), half)]

    def _plan(self, ins, outs, sems):
        send1, recv1, local_sems, send2, recv2 = sems
        me, peers, peer_chip = _chip_peers()
        x, y, c = lax.axis_index("x"), lax.axis_index("y"), lax.axis_index("c")
        n = len(ins)
        pairs = [(i, k) for k in range(3) for i in range(n)]

        def fetch(i, k, slot):
            return pltpu.make_async_remote_copy(src_ref=self._half(i, ins[i]), dst_ref=self._half(i, outs[i].at[slot]),
                                                send_sem=send1.at[3 * i + k], recv_sem=recv1.at[3 * i + k],
                                                device_id=peers[k], device_id_type=MESH)

        def share(i, k, sibling):
            part = self._half(i, outs[i].at[peer_chip[k]], sibling)
            return pltpu.make_async_remote_copy(src_ref=part, dst_ref=part, send_sem=send2.at[3 * i + k],
                                                recv_sem=recv2.at[3 * i + k], device_id=(x, y, 1 - c),
                                                device_id_type=MESH)

        split_pairs = [(i, k) for i, k in pairs if self.split[i]]
        return dict(
            local=lambda: [pltpu.make_async_copy(ins[i], outs[i].at[me], local_sems.at[i]) for i in range(n)],
            fetch_out=lambda: [fetch(i, k, me) for i, k in pairs],
            fetch_in=lambda: [(fetch(i, k, peer_chip[k]), share(i, k, False) if self.split[i] else None)
                              for i, k in pairs],
            share_out=lambda: [share(i, k, False) for i, k in split_pairs],
            share_in=lambda: [share(i, k, True) for i, k in split_pairs])

    def start(self, ins, outs, sems):
        plan = self._plan(ins, outs, sems)
        for cp in plan["local"]() + plan["fetch_out"]():
            cp.start()

    def mid(self, ins, outs, sems):
        plan = self._plan(ins, outs, sems)
        for arrived, forward in plan["fetch_in"]():
            arrived.wait_recv()
            if forward is not None:
                forward.start()

    def wait(self, ins, outs, sems):
        plan = self._plan(ins, outs, sems)
        for cp in plan["share_in"]():
            cp.wait_recv()
        for cp in plan["fetch_out"]() + plan["share_out"]():
            cp.wait_send()
        for cp in plan["local"]():
            cp.wait()


class _Scatter(_Exchange):
    def __init__(self, slabs, whole=()):
        self.n_slabs = len(slabs)
        self.ins = list(slabs) + list(whole)
        n = len(self.ins)
        self.out_shape = [jax.ShapeDtypeStruct(a.shape, a.dtype) for a in slabs] \
            + [jax.ShapeDtypeStruct((N_CHIPS,) + a.shape, a.dtype) for a in whole]
        self.sems = [pltpu.SemaphoreType.DMA((3 * n,)), pltpu.SemaphoreType.DMA((3 * n,)), pltpu.SemaphoreType.DMA((n,))]

    def _copies(self, ins, outs, sems):
        send_sems, recv_sems, local_sems = sems
        me, peers, peer_chip = _chip_peers()
        n = len(ins)

        def src(i, chip):
            return ins[i].at[chip] if i < self.n_slabs else ins[i]

        def remote(i, k, src_chip, slot):
            return pltpu.make_async_remote_copy(src_ref=src(i, src_chip), dst_ref=outs[i].at[slot],
                                                send_sem=send_sems.at[3 * i + k], recv_sem=recv_sems.at[3 * i + k],
                                                device_id=peers[k], device_id_type=MESH)

        local = [pltpu.make_async_copy(src(i, me), outs[i].at[me], local_sems.at[i]) for i in range(n)]
        sends = [remote(i, k, peer_chip[k], me) for i in range(n) for k in range(3)]
        return local, sends, lambda: [remote(i, k, me, peer_chip[k]) for i in range(n) for k in range(3)]


class _Swap(_Exchange):
    def __init__(self, arrays):
        n = len(arrays)
        self.ins = list(arrays)
        self.out_shape = [jax.ShapeDtypeStruct(a.shape, a.dtype) for a in arrays]
        self.sems = [pltpu.SemaphoreType.DMA((n,)), pltpu.SemaphoreType.DMA((n,))]

    def _copies(self, ins, outs, sems):
        send_sems, recv_sems = sems
        x, y, c = lax.axis_index("x"), lax.axis_index("y"), lax.axis_index("c")
        cps = [pltpu.make_async_remote_copy(src_ref=ins[i], dst_ref=outs[i], send_sem=send_sems.at[i],
                                            recv_sem=recv_sems.at[i], device_id=(x, y, 1 - c), device_id_type=MESH)
               for i in range(len(ins))]
        return [], cps, lambda: cps


def _riding(body, n_in, n_out, n_scratch, ride, rank):
    if not ride:
        return body
    r_in = sum(len(e.ins) for e in ride)
    r_out = sum(len(e.out_shape) for e in ride)

    def split(refs, sizes):
        out, a = [], 0
        for sz in sizes:
            out.append(refs[a:a + sz])
            a += sz
        return out

    def wrapped(*refs):
        a = 0
        parts = []
        for sz in (n_in, r_in, n_out, r_out, n_scratch):
            parts.append(refs[a:a + sz])
            a += sz
        own_in, ex_in, own_out, ex_out, own_scratch = parts
        ex_sems = refs[a:]
        ins = split(ex_in, [len(e.ins) for e in ride])
        outs = split(ex_out, [len(e.out_shape) for e in ride])
        sems = split(ex_sems, [len(e.sems) for e in ride])
        if rank:
            step, total = 0, 1
            for d in range(rank):
                step = step * pl.num_programs(d) + pl.program_id(d)
                total = total * pl.num_programs(d)

            @pl.when(step == 0)
            def _():
                for e, i, o, s in zip(ride, ins, outs, sems):
                    e.start(i, o, s)

            body(*own_in, *own_out, *own_scratch)

            @pl.when(step == (total * _MID_NUM) // _MID_DEN)
            def _():
                for e, i, o, s in zip(ride, ins, outs, sems):
                    e.mid(i, o, s)

            @pl.when(step == total - 1)
            def _():
                for e, i, o, s in zip(ride, ins, outs, sems):
                    e.wait(i, o, s)
        else:
            for phase in ("start", "mid", "wait"):
                for e, i, o, s in zip(ride, ins, outs, sems):
                    getattr(e, phase)(i, o, s)

    return wrapped


def _ride_args(ride):
    ins = [a for e in ride for a in e.ins]
    outs = [s for e in ride for s in e.out_shape]
    sems = [s for e in ride for s in e.sems]
    return ins, _any_specs(len(ins)), outs, _any_specs(len(outs)), sems


def _ride_results(ride, flat):
    out, a = [], 0
    for e in ride:
        out.append(list(flat[a:a + len(e.out_shape)]))
        a += len(e.out_shape)
    return out


def exchange(ride, name):
    ins, in_specs, outs, out_specs, sems = _ride_args(ride)
    res = pl.pallas_call(
        _riding(lambda: None, 0, 0, 0, ride, 0), name=name,
        in_specs=in_specs, out_specs=out_specs, out_shape=outs, scratch_shapes=sems,
    )(*ins)
    return _ride_results(ride, res)


def _sum_slots(ref):
    acc = ref[0].astype(F32)
    for j in range(1, ref.shape[0]):
        acc = acc + ref[j].astype(F32)
    return acc


def sum_pairs(mine, other, name, tr=176):
    n, r, w = mine.shape
    tr = r if r <= tr else _tile(r, tr)

    def body(a_ref, b_ref, o_ref):
        o_ref[...] = _sum_slots(a_ref) + _sum_slots(b_ref)

    spec = pl.BlockSpec((n, tr, w), lambda i: (0, i, 0))
    return pl.pallas_call(
        body, name=name, grid=(r // tr,),
        in_specs=[spec, spec], out_specs=_rows(tr, w),
        out_shape=jax.ShapeDtypeStruct((r, w), F32),
        compiler_params=_params(("parallel",)),
    )(mine, other)


def _adamw_update(w, m, v, g):
    c1 = 1.0 - ADAM_B1 ** ADAM_STEP
    c2 = 1.0 - ADAM_B2 ** ADAM_STEP
    nm = ADAM_B1 * m + (1.0 - ADAM_B1) * g
    nv = ADAM_B2 * v + (1.0 - ADAM_B2) * (g * g)
    return nm, nv, (-ADAM_LR) * ((nm / c1) / (jnp.sqrt(nv / c2) + ADAM_EPS) + ADAM_WD * w)


def adamw(w, m, v, mine, other, name, tr=256):
    r, c = w.shape
    tr = r if r % 8 else _tile(r, tr)

    def body(w_ref, m_ref, v_ref, a_ref, b_ref, g_ref, d_ref, nm_ref, nv_ref):
        g = _sum_slots(a_ref) + _sum_slots(b_ref)
        nm, nv, delta = _adamw_update(w_ref[...], m_ref[...], v_ref[...], g)
        g_ref[...] = g
        nm_ref[...] = nm
        nv_ref[...] = nv
        d_ref[...] = delta

    spec = _rows(tr, c)
    gspec = pl.BlockSpec((N_CHIPS, tr, c), lambda i: (0, i, 0))
    return pl.pallas_call(
        body, name=name, grid=(r // tr,),
        in_specs=[spec] * 3 + [gspec] * 2, out_specs=[spec] * 4,
        out_shape=[jax.ShapeDtypeStruct((r, c), F32)] * 4,
        compiler_params=_params(("parallel",)),
    )(w, m, v, mine, other)


def adamw_small(ws, ms, vs, gs, name):
    n = len(ws)

    def body(*refs):
        ins, outs = refs[:4 * n], refs[4 * n:]
        for i in range(n):
            w_ref, m_ref, v_ref, g_ref = ins[4 * i:4 * i + 4]
            d_ref, nm_ref, nv_ref = outs[3 * i:3 * i + 3]
            nm, nv, delta = _adamw_update(w_ref[...], m_ref[...], v_ref[...], g_ref[...])
            d_ref[...] = delta
            nm_ref[...] = nm
            nv_ref[...] = nv

    flat = [a for quad in zip(ws, ms, vs, gs) for a in quad]
    vmem = pl.BlockSpec(memory_space=pltpu.VMEM)
    outs = pl.pallas_call(
        body, name=name,
        in_specs=[vmem] * (4 * n), out_specs=[vmem] * (3 * n),
        out_shape=[jax.ShapeDtypeStruct(w.shape, F32) for w in ws for _ in range(3)],
        compiler_params=_params(),
    )(*flat)
    return [tuple(outs[3 * i:3 * i + 3]) for i in range(n)]


_PARAMS = (
    ("rel_bias", None), ("norm_mix_pre", None), ("norm_mix_post", None), ("w_in", 1), ("conv_rnn_w", 1),
    ("conv_rnn_b", None), ("w_rg_a", None), ("b_rg_a", None), ("w_rg_x", None), ("b_rg_x", None),
    ("lru_lambda", None), ("w_branch_rnn", 0), ("w_branch_att", 1), ("w_out", 0), ("norm_ffn_pre", None),
    ("norm_ffn_post", None), ("w_ffn_gate", 1), ("w_ffn_up", 1), ("conv_ffn_w", 1), ("conv_ffn_b", None),
    ("w_ffn_down", 0),
)
_SMALL = 65536


def _as2d(a):
    a = a[0] if a.shape[0] == 1 and a.ndim >= 3 else a
    return a.reshape(-1, a.shape[-1]) if a.ndim == 3 else a


def _pack(pieces, dtype):
    flat = jnp.concatenate([p.astype(dtype).reshape(-1) for p in pieces])
    unit = PACK_W * PACK_ROWS
    pad = (-flat.shape[0]) % unit
    flat = jnp.pad(flat, (0, pad))
    return flat.reshape(-1, PACK_W)


def _unpack(buf, shapes):
    flat = buf.reshape(-1)
    out, off = [], 0
    for shp in shapes:
        n = int(np.prod(shp))
        out.append(flat[off:off + n].reshape(shp))
        off += n
    return out


def _join(slots, ax):
    if ax == 0:
        return slots.reshape(-1, slots.shape[-1])
    return jnp.transpose(slots, (1, 0, 2)).reshape(slots.shape[1], -1)


def _cut(full, ax):
    if ax == 0:
        return full.reshape(N_CHIPS, -1, full.shape[-1])
    return jnp.transpose(full.reshape(full.shape[0], N_CHIPS, -1), (1, 0, 2))


def kernel(x, rel_bias, norm_mix_pre, norm_mix_post, w_in, conv_rnn_w, conv_rnn_b, w_rg_a, b_rg_a, w_rg_x, b_rg_x, lru_lambda, w_branch_rnn, w_branch_att, w_out, norm_ffn_pre, norm_ffn_post, w_ffn_gate, w_ffn_up, conv_ffn_w, conv_ffn_b, w_ffn_down, loss_target, m_rel_bias, m_norm_mix_pre, m_norm_mix_post, m_w_in, m_conv_rnn_w, m_conv_rnn_b, m_w_rg_a, m_b_rg_a, m_w_rg_x, m_b_rg_x, m_lru_lambda, m_w_branch_rnn, m_w_branch_att, m_w_out, m_norm_ffn_pre, m_norm_ffn_post, m_w_ffn_gate, m_w_ffn_up, m_conv_ffn_w, m_conv_ffn_b, m_w_ffn_down, v_rel_bias, v_norm_mix_pre, v_norm_mix_post, v_w_in, v_conv_rnn_w, v_conv_rnn_b, v_w_rg_a, v_b_rg_a, v_w_rg_x, v_b_rg_x, v_lru_lambda, v_w_branch_rnn, v_w_branch_att, v_w_out, v_norm_ffn_pre, v_norm_ffn_post, v_w_ffn_gate, v_w_ffn_up, v_conv_ffn_w, v_conv_ffn_b, v_w_ffn_down):
    args = dict(locals())
    names = [n for n, _ in _PARAMS]
    axis = dict(_PARAMS)
    w_loc = {n: args[n] for n in names}
    m_loc = {n: args["m_" + n] for n in names}
    v_loc = {n: args["v_" + n] for n in names}
    sharded = [n for n in names if axis[n] is not None]
    replicated = [n for n in names if axis[n] is None]

    big = [n for n in sharded if w_loc[n].size >= _SMALL]
    small_sharded = [n for n in sharded if n not in big]
    small = replicated + small_sharded

    first = ["w_in"] + small_sharded
    srcs = [_as2d(w_loc[n]).astype(BF16) if n in big else _as2d(w_loc[n]) for n in first]
    (gathered,) = exchange([_Gather(srcs)], "gather_first")
    p = {n: _join(a, axis[n]) for n, a in zip(first, gathered)}
    for n in replicated:
        p[n] = _as2d(w_loc[n])
    shards = {n: _as2d(w_loc[n]).astype(BF16) for n in big if n not in first}

    last = "norm_mix_pre"
    early = [n for n in small if n != last]
    received, sibling, g_small, loss_part = _local_step(x, loss_target, p, shards, early)

    ((received["last"],),) = exchange([_Scatter([], [_pack([g_small[last]], BF16)])], "scatter_last")
    late = [n for n in received if n not in sibling]
    (swapped,) = exchange([_Swap([received[n] for n in late])], "swap_last")
    sibling.update(zip(late, swapped))
    early_sum = sum_pairs(received["small"], sibling["small"], "sum_small")
    last_sum = sum_pairs(received["last"], sibling["last"], "sum_last")
    g_tot = dict(zip(early, _unpack(early_sum, [g_small[n].shape for n in early])))
    (g_tot[last],) = _unpack(last_sum, [g_small[last].shape])
    chip = 2 * lax.axis_index("x") + lax.axis_index("y")
    for n in small_sharded:
        size = g_tot[n].shape[axis[n]] // N_CHIPS
        g_tot[n] = lax.dynamic_slice_in_dim(g_tot[n], chip * size, size, axis=axis[n])

    out_g, out_d, out_m, out_v = {}, {}, {}, {}
    for n in big:
        res = adamw(_as2d(w_loc[n]), _as2d(m_loc[n]), _as2d(v_loc[n]), received[n], sibling[n], "adamw_" + n)
        out_g[n], out_d[n], out_m[n], out_v[n] = (t.reshape(w_loc[n].shape) for t in res)
    res = adamw_small([_as2d(w_loc[n]) for n in small], [_as2d(m_loc[n]) for n in small],
                      [_as2d(v_loc[n]) for n in small], [g_tot[n] for n in small], "adamw_small")
    for n, (d, nm, nv) in zip(small, res):
        out_g[n], out_d[n], out_m[n], out_v[n] = (t.reshape(w_loc[n].shape) for t in (g_tot[n], d, nm, nv))

    d_model = x.shape[-1]
    loss = lax.psum(0.5 * jnp.sum(loss_part) / d_model, ("x", "y", "c"))
    grad_x = g_small["x"]
    return (loss, grad_x, *[out_g[n] for n in names], *[out_d[n] for n in names],
            *[out_m[n] for n in names], *[out_v[n] for n in names])


def _local_step(x, target, p, shards, small_early):
    axis = dict(_PARAMS)
    b, s, d = x.shape
    t = b * s
    rnn = p["b_rg_a"].shape[1]
    ffn = p["conv_ffn_b"].shape[1]
    nbk = rnn // p["w_rg_a"].shape[1]
    hkv = (p["w_in"].shape[1] - rnn - 2 * d) // (N_GROUPS + 2)
    h = hkv // HEAD_DIM
    nq = N_GROUPS * hkv

    x2 = x.reshape(t, d)
    tgt = target.reshape(t, d)
    w_in = p["w_in"]
    in_splits = (rnn, nq + 2 * hkv, 2 * d)
    wa = p["w_rg_a"].reshape(nbk, -1, p["w_rg_a"].shape[1]).astype(BF16)
    wx = p["w_rg_x"].reshape(nbk, -1, p["w_rg_x"].shape[1]).astype(BF16)
    cw_r, cb_r = p["conv_rnn_w"], p["conv_rnn_b"]
    cw_f, cb_f = p["conv_ffn_w"], p["conv_ffn_b"]

    masks, buckets = zip(*[_band(w_, r_) for w_, r_ in DILATED])
    bucket_f = jnp.asarray(np.where(np.stack(masks), np.stack(buckets), -1).astype(np.float32))
    rel_rows = jnp.pad(p["rel_bias"].T, ((0, 0), (0, 128 - REL_BUCKETS)))[:, None, :]
    biasm = bias_table(rel_rows, bucket_f, h, "bias_table")

    early = ["w_branch_rnn", "w_branch_att", "w_out"]
    hn1, (xr, qkv, gts), (got,) = norm_mm(x2, p["norm_mix_pre"], [w_in], [in_splits], "in_proj",
                                          ride=[_Gather([shards[n] for n in early])])
    p.update({n: _join(a, axis[n]) for n, a in zip(early, got)})
    xr3 = xr.reshape(b, s, rnn)
    (y_rnn, a_rnn, xc_rnn), (got,) = rglru_fwd(xr3, cw_r, cb_r, wa, p["b_rg_a"], wx, p["b_rg_x"], p["lru_lambda"], "rglru_fwd",
                              ride=[_Gather([shards[n] for n in ("w_ffn_gate", "w_ffn_up")])])
    p.update(zip(("w_ffn_gate", "w_ffn_up"), got))
    qkv3 = qkv.reshape(b, s, -1)
    o_att, lse, ((got,),) = attn_fwd(qkv3, biasm, h, "attn_fwd", ride=[_Gather([shards["w_ffn_down"]])])
    p["w_ffn_down"] = _join(got, axis["w_ffn_down"])
    merged, br, ba, mix, h1 = merge_out(y_rnn.reshape(t, rnn), o_att.reshape(t, hkv), gts, p["w_branch_rnn"],
                                        p["w_branch_att"], p["w_out"], p["norm_mix_post"], x2, "merge_out")
    hn2, gate_pre, up, act = ffn_in_act(h1, p["norm_ffn_pre"], p["w_ffn_gate"], p["w_ffn_up"], cw_f, cb_f, s, "ffn_in")

    g, gb = {}, {}
    recv, sib = {}, {}

    def rows4(a):
        return a.reshape(N_CHIPS, -1, a.shape[-1])

    dy, dff, dact, g["norm_ffn_post"], loss_part = ffn_down_loss(act, p["w_ffn_down"], p["norm_ffn_post"], h1, tgt,
                                                                  "ffn_down")
    gb["w_ffn_down"] = rows4(mm_tn(act, [dff], "ffn_down_dw"))
    (dgp, dup, dhn2, g["conv_ffn_w"], g["conv_ffn_b"]), ((recv["w_ffn_down"],),) = ffn_in_bwd(
        dact, gate_pre, up, cw_f, cb_f, p["w_ffn_gate"], p["w_ffn_up"], s, "ffn_in_bwd",
        ride=[_Scatter([gb["w_ffn_down"]])])
    gb["w_ffn_gate"] = mm_tn(hn2, [dgp], "ffn_gate_dw", col_shards=N_CHIPS)
    gb["w_ffn_up"] = mm_tn(hn2, [dup], "ffn_up_dw", col_shards=N_CHIPS)
    (dh1, dgts, dy_rnn, do_att, g["norm_ffn_pre"], g["norm_mix_post"], dw_out, dw_br,
     gb["w_branch_att"]) = mid_bwd(dhn2, h1, p["norm_ffn_pre"], dy, mix, p["norm_mix_post"], p["w_out"], gts, br, ba,
                                   p["w_branch_rnn"], p["w_branch_att"], merged, y_rnn.reshape(t, rnn),
                                   o_att.reshape(t, hkv), "mid_bwd")
    gb["w_out"], gb["w_branch_rnn"] = rows4(dw_out), rows4(dw_br)
    ffn_in = ["w_ffn_gate", "w_ffn_up"]
    (dxr, g["conv_rnn_w"], g["conv_rnn_b"], dwa, g["b_rg_a"], dwx, g["b_rg_x"], g["lru_lambda"]), (got,) = rglru_bwd(
        xr3, y_rnn, dy_rnn.reshape(b, s, rnn), a_rnn, xc_rnn, cw_r, wa, p["b_rg_a"], wx, p["b_rg_x"], p["lru_lambda"], "rglru_bwd",
        ride=[_Scatter([gb[n] for n in ffn_in])])
    recv.update(zip(ffn_in, got))
    g["w_rg_a"] = dwa.reshape(p["w_rg_a"].shape)
    g["w_rg_x"] = dwx.reshape(p["w_rg_x"].shape)
    mid = ["w_out", "w_branch_rnn", "w_branch_att"]
    early_recv = ["w_ffn_down"] + ffn_in
    (dq1, dq2, dq3, dk, dv, ds_sum), (got, swapped) = attn_bwd(
        qkv3, biasm, o_att, lse, do_att.reshape(b, s, hkv), h, "attn_bwd",
        ride=[_Scatter([gb[n] for n in mid]), _Swap([recv[n] for n in early_recv])])
    recv.update(zip(mid, got))
    sib.update(zip(early_recv, swapped))
    rows = bias_grad(ds_sum, bucket_f, "bias_grad")
    g["rel_bias"] = rows[:, 0, :REL_BUCKETS].T
    dproj = [dxr.reshape(t, rnn)] + [a.reshape(t, hkv) for a in (dq1, dq2, dq3, dk, dv)] + [dgts]
    dw_a, (got,) = mm_tn(hn1, dproj[:4], "in_proj_dw_a", ride=[_Swap([recv[n] for n in mid])])
    sib.update(zip(mid, got))
    pack = _pack([g[n] for n in small_early], BF16)
    dw_b, ((recv["small"],),) = mm_tn(hn1, dproj[4:], "in_proj_dw_b", ride=[_Scatter([], [pack])])
    gb["w_in"] = _cut(jnp.concatenate([dw_a[0], dw_b[0]], axis=1), 1)
    dx, g["norm_mix_pre"], ((recv["w_in"],),) = mm_nt(
        [(dproj, w_in)], "in_proj_dx", norm=(x2, p["norm_mix_pre"], dh1), ride=[_Scatter([gb["w_in"]])])
    g["x"] = dx.reshape(b, s, d)
    return recv, sib, g, loss_part
```

```python
import math

import numpy as np
import jax
import jax.numpy as jnp
from jax import lax
from jax.experimental import pallas as pl
from jax.experimental.pallas import tpu as pltpu

F32 = jnp.float32
BF16 = jnp.bfloat16

EPS = 1e-6
HEAD_DIM = 128
ATTN_BLOCK = 128
DILATED = ((128, 1), (512, 4), (2048, 16))
N_GROUPS = len(DILATED)
REL_BUCKETS = 32
REL_MAX_DIST = 2048
LRU_C = 8.0
NEG = -1e30

ADAM_LR = 0.001
ADAM_B1 = 0.9
ADAM_B2 = 0.999
ADAM_EPS = 1e-08
ADAM_WD = 0.01
ADAM_STEP = 10

N_CHIPS = 4
PACK_W = 1024
PACK_ROWS = 16
VMEM_LIMIT = 56 * 1024 * 1024
MESH = pl.DeviceIdType.MESH


def _params(sem=None):
    return pltpu.CompilerParams(dimension_semantics=sem, vmem_limit_bytes=VMEM_LIMIT)


def _dot(a, b):
    return jnp.dot(a, b, preferred_element_type=F32)


def _dot_nt(a, b):
    return lax.dot_general(a, b, (((1,), (1,)), ((), ())), preferred_element_type=F32)


def _dot_tn(a, b):
    return lax.dot_general(a, b, (((0,), (0,)), ((), ())), preferred_element_type=F32)


def _sig(x):
    return 0.5 * jnp.tanh(0.5 * x) + 0.5


def _rows(tm, w):
    return pl.BlockSpec((tm, w), lambda i: (i, 0))


def _whole(shape):
    nd = len(shape)
    return pl.BlockSpec(tuple(shape), lambda *_: (0,) * nd)


def _resident(shape):
    nd = len(shape)
    return pl.BlockSpec(tuple(shape), lambda *_: (0,) * nd, pipeline_mode=pl.Buffered(1))


def _tile(t, want):
    while t % want:
        want //= 2
    return want


def norm_mm(x, g, ws, splits, name, ride=(), tm=512):
    t, d = x.shape
    tm = _tile(t, tm)
    nw = len(ws)
    widths = [n for sp in splits for n in sp]

    def body(x_ref, g_ref, *refs):
        w_refs, hn_ref, o_refs = refs[:nw], refs[nw], refs[nw + 1:]
        xv = x_ref[...]
        inv = lax.rsqrt(jnp.mean(xv * xv, axis=-1, keepdims=True) + EPS)
        hn = (xv * inv * g_ref[...]).astype(BF16)
        hn_ref[...] = hn
        o = 0
        for w_ref, sp in zip(w_refs, splits):
            off = 0
            for n in sp:
                o_refs[o][...] = _dot(hn, w_ref[:, off:off + n])
                off += n
                o += 1

    r_ins, r_in_specs, r_outs, r_out_specs, r_sems = _ride_args(ride)
    n_out = 1 + len(widths)
    outs = pl.pallas_call(
        _riding(body, 2 + nw, n_out, 0, ride, 1), name=name, grid=(t // tm,),
        in_specs=[_rows(tm, d), _whole(g.shape)] + [_resident(w.shape) for w in ws] + r_in_specs,
        out_specs=[_rows(tm, d)] + [_rows(tm, n) for n in widths] + r_out_specs,
        out_shape=[jax.ShapeDtypeStruct((t, d), BF16)] + [jax.ShapeDtypeStruct((t, n), F32) for n in widths] + r_outs,
        scratch_shapes=r_sems,
        compiler_params=_params(("arbitrary",)),
    )(x, g, *ws, *r_ins)
    return outs[0], outs[1:n_out], _ride_results(ride, outs[n_out:])


def mm_nt(groups, name, ride=(), norm=None, tm=512):
    dys_all = [dy for dys, _ in groups for dy in dys]
    ws = [w for _, w in groups]
    t = dys_all[0].shape[0]
    k = ws[0].shape[0]
    tm = _tile(t, tm)
    n = len(dys_all)
    extra = list(norm) if norm else []

    def body(*refs):
        dy_refs, w_refs = refs[:n], refs[n:n + len(ws)]
        rest = refs[n + len(ws):]
        acc = None
        i = 0
        for (dys, _), w_ref in zip(groups, w_refs):
            off = 0
            for dy in dys:
                width = dy.shape[1]
                part = _dot_nt(dy_refs[i][...].astype(BF16), w_ref[:, off:off + width])
                acc = part if acc is None else acc + part
                off += width
                i += 1
        if norm:
            u_ref, g_ref, add_ref, o_ref, dg_ref = rest

            @pl.when(pl.program_id(0) == 0)
            def _():
                dg_ref[...] = jnp.zeros(dg_ref.shape, F32)

            du, dg_rows = _rms_bwd(acc, u_ref[...], g_ref[...])
            o_ref[...] = du + add_ref[...]
            dg_ref[...] += jnp.sum(dg_rows, axis=0, keepdims=True)
        else:
            rest[0][...] = acc

    n_out = 2 if norm else 1
    r_ins, r_in_specs, r_outs, r_out_specs, r_sems = _ride_args(ride)
    outs = pl.pallas_call(
        _riding(body, n + len(ws) + len(extra), n_out, 0, ride, 1), name=name, grid=(t // tm,),
        in_specs=[_rows(tm, dy.shape[1]) for dy in dys_all] + [_resident(w.shape) for w in ws]
        + ([_rows(tm, k), _whole((1, k)), _rows(tm, k)] if norm else []) + r_in_specs,
        out_specs=[_rows(tm, k)] + ([_whole((1, k))] if norm else []) + r_out_specs,
        out_shape=[jax.ShapeDtypeStruct((t, k), F32)] + ([jax.ShapeDtypeStruct((1, k), F32)] if norm else []) + r_outs,
        scratch_shapes=r_sems,
        compiler_params=_params(("arbitrary",)),
    )(*dys_all, *ws, *extra, *r_ins)
    return tuple(outs[:n_out]) + (_ride_results(ride, outs[n_out:]),)


def mm_tn(a, dys, name, col_shards=1, ride=(), tm=1024):
    t, k = a.shape
    tm = _tile(t, tm)
    n = len(dys)
    ntot = sum(dy.shape[1] for dy in dys)
    wsh = ntot // col_shards

    def body(a_ref, *refs):
        dy_refs, o_ref, acc = refs[:n], refs[n], refs[n + 1]

        @pl.when(pl.program_id(0) == 0)
        def _():
            acc[...] = jnp.zeros(acc.shape, F32)

        av = a_ref[...].astype(BF16)
        off = 0
        for dy_ref in dy_refs:
            width = dy_ref.shape[1]
            acc[:, off:off + width] += _dot_tn(av, dy_ref[...].astype(BF16))
            off += width

        @pl.when(pl.program_id(0) == pl.num_programs(0) - 1)
        def _():
            for j in range(col_shards):
                o_ref[j] = acc[:, j * wsh:(j + 1) * wsh].astype(o_ref.dtype)

    r_ins, r_in_specs, r_outs, r_out_specs, r_sems = _ride_args(ride)
    outs = pl.pallas_call(
        _riding(body, 1 + n, 1, 1, ride, 1), name=name, grid=(t // tm,),
        in_specs=[_rows(tm, k)] + [_rows(tm, dy.shape[1]) for dy in dys] + r_in_specs,
        out_specs=[_whole((col_shards, k, wsh))] + r_out_specs,
        out_shape=[jax.ShapeDtypeStruct((col_shards, k, wsh), BF16)] + r_outs,
        scratch_shapes=[pltpu.VMEM((k, ntot), F32)] + r_sems,
        compiler_params=_params(("arbitrary",)),
    )(a, *dys, *r_ins)
    return (outs[0], _ride_results(ride, outs[1:])) if ride else outs[0]


def _rms_bwd(dz, u, g):
    d = u.shape[-1]
    inv = lax.rsqrt(jnp.mean(u * u, axis=-1, keepdims=True) + EPS)
    dzg = dz * g
    proj = jnp.sum(dzg * u, axis=-1, keepdims=True) * (1.0 / d)
    du = inv * (dzg - u * (inv * inv) * proj)
    dg_rows = dz * u * inv
    return du, dg_rows


def ffn_down_loss(act, wd, g, h1, target, name, tm=512):
    t, f = act.shape
    d = wd.shape[1]
    tm = _tile(t, tm)

    def body(a_ref, w_ref, g_ref, h_ref, t_ref, dy_ref, dff_ref, dact_ref, dg_ref, loss_ref):
        @pl.when(pl.program_id(0) == 0)
        def _():
            dg_ref[...] = jnp.zeros(dg_ref.shape, F32)
            loss_ref[...] = jnp.zeros(loss_ref.shape, F32)

        wv = w_ref[...]
        gv = g_ref[...]
        ff = _dot(a_ref[...], wv)
        inv = lax.rsqrt(jnp.mean(ff * ff, axis=-1, keepdims=True) + EPS)
        err = h_ref[...] + ff * inv * gv - t_ref[...]
        loss_ref[...] += jnp.sum(err * err, axis=0, keepdims=True)
        dy = err * (1.0 / d)
        dy_ref[...] = dy
        du, dg_rows = _rms_bwd(dy, ff, gv)
        dff = du.astype(BF16)
        dff_ref[...] = dff
        dg_ref[...] += jnp.sum(dg_rows, axis=0, keepdims=True)
        dact_ref[...] = _dot_nt(dff, wv)

    return pl.pallas_call(
        body, name=name, grid=(t // tm,),
        in_specs=[_rows(tm, f), _resident(wd.shape), _whole(g.shape), _rows(tm, d), _rows(tm, d)],
        out_specs=[_rows(tm, d), _rows(tm, d), _rows(tm, f), _whole((1, d)), _whole((1, d))],
        out_shape=[jax.ShapeDtypeStruct((t, d), F32), jax.ShapeDtypeStruct((t, d), BF16),
                   jax.ShapeDtypeStruct((t, f), F32), jax.ShapeDtypeStruct((1, d), F32),
                   jax.ShapeDtypeStruct((1, d), F32)],
        compiler_params=_params(("arbitrary",)),
    )(act, wd, g, h1, target)


def merge_out(y_rnn, o_att, gts, w_br, w_ba, w_out, g, x, name, tm=512):
    t = y_rnn.shape[0]
    d = w_br.shape[1]
    tm = _tile(t, tm)

    def body(y_ref, o_ref, g_ref, wbr_ref, wba_ref, wo_ref, gn_ref, x_ref, m_ref, br_ref, ba_ref, mix_ref, h_ref):
        br = _dot(y_ref[...].astype(BF16), wbr_ref[...])
        ba = _dot(o_ref[...].astype(BF16), wba_ref[...])
        gv = g_ref[...]
        merged = (_sig(gv[:, :d]) * br + _sig(gv[:, d:]) * ba).astype(BF16)
        m_ref[...] = merged
        br_ref[...] = br
        ba_ref[...] = ba
        mix = _dot(merged, wo_ref[...])
        mix_ref[...] = mix
        inv = lax.rsqrt(jnp.mean(mix * mix, axis=-1, keepdims=True) + EPS)
        h_ref[...] = x_ref[...] + mix * inv * gn_ref[...]

    sd = jax.ShapeDtypeStruct
    return pl.pallas_call(
        body, name=name, grid=(t // tm,),
        in_specs=[_rows(tm, y_rnn.shape[1]), _rows(tm, o_att.shape[1]), _rows(tm, 2 * d),
                  _resident(w_br.shape), _resident(w_ba.shape), _resident(w_out.shape), _whole(g.shape), _rows(tm, d)],
        out_specs=[_rows(tm, d)] * 5,
        out_shape=[sd((t, d), BF16), sd((t, d), F32), sd((t, d), F32), sd((t, d), F32), sd((t, d), F32)],
        compiler_params=_params(("parallel",)),
    )(y_rnn, o_att, gts, w_br, w_ba, w_out, g, x)


def mid_bwd(dhn2, h1, g_ffn, dy, mix, g_mix, w_out, gts, br, ba, w_br, w_ba, merged, y_rnn, o_att, name, tm=256):
    t, d = h1.shape
    tm = _tile(t, tm)
    rnn, hkv = w_br.shape[0], w_ba.shape[0]
    wsh = d // N_CHIPS

    def body(dhn_ref, h_ref, gf_ref, dy_ref, mix_ref, gm_ref, wo_ref, g_ref, br_ref, ba_ref, wbr_ref, wba_ref,
             m_ref, y_ref, o_ref, dh_ref, dg_ref, dyr_ref, doa_ref, dgf_ref, dgm_ref, dwo_ref, dwbr_ref, dwba_ref,
             acc_o, acc_br, acc_ba):
        @pl.when(pl.program_id(0) == 0)
        def _():
            dgf_ref[...] = jnp.zeros(dgf_ref.shape, F32)
            dgm_ref[...] = jnp.zeros(dgm_ref.shape, F32)
            acc_o[...] = jnp.zeros(acc_o.shape, F32)
            acc_br[...] = jnp.zeros(acc_br.shape, F32)
            acc_ba[...] = jnp.zeros(acc_ba.shape, F32)

        du, rows_f = _rms_bwd(dhn_ref[...], h_ref[...], gf_ref[...])
        dh1 = du + dy_ref[...]
        dh_ref[...] = dh1
        dgf_ref[...] += jnp.sum(rows_f, axis=0, keepdims=True)
        dmx, rows_m = _rms_bwd(dh1, mix_ref[...], gm_ref[...])
        dmix = dmx.astype(BF16)
        acc_o[...] += _dot_tn(m_ref[...], dmix)
        dgm_ref[...] += jnp.sum(rows_m, axis=0, keepdims=True)
        dm = _dot_nt(dmix, wo_ref[...])
        gv = g_ref[...]
        sr = _sig(gv[:, :d])
        sa = _sig(gv[:, d:])
        dbr = (dm * sr).astype(BF16)
        dba = (dm * sa).astype(BF16)
        acc_br[...] += _dot_tn(y_ref[...].astype(BF16), dbr)
        acc_ba[...] += _dot_tn(o_ref[...].astype(BF16), dba)
        dg_ref[:, :d] = (dm * br_ref[...] * sr * (1.0 - sr)).astype(BF16)
        dg_ref[:, d:] = (dm * ba_ref[...] * sa * (1.0 - sa)).astype(BF16)
        dyr_ref[...] = _dot_nt(dbr, wbr_ref[...])
        doa_ref[...] = _dot_nt(dba, wba_ref[...])

        @pl.when(pl.program_id(0) == pl.num_programs(0) - 1)
        def _():
            dwo_ref[...] = acc_o[...].astype(BF16)
            dwbr_ref[...] = acc_br[...].astype(BF16)
            for j in range(N_CHIPS):
                dwba_ref[j] = acc_ba[:, j * wsh:(j + 1) * wsh].astype(BF16)

    sd = jax.ShapeDtypeStruct
    row, vec = _rows(tm, d), _whole((1, d))
    once = pl.Buffered(1)

    def resident(shape):
        return pl.BlockSpec(shape, lambda i: (0,) * len(shape), pipeline_mode=once)

    return pl.pallas_call(
        body, name=name, grid=(t // tm,),
        in_specs=[row, row, vec, row, row, vec, resident(w_out.shape), _rows(tm, 2 * d), row, row,
                  resident(w_br.shape), resident(w_ba.shape), row, _rows(tm, rnn), _rows(tm, hkv)],
        out_specs=[row, _rows(tm, 2 * d), _rows(tm, rnn), _rows(tm, hkv), vec, vec,
                   resident((d, d)), resident((rnn, d)), resident((N_CHIPS, hkv, wsh))],
        out_shape=[sd((t, d), F32), sd((t, 2 * d), BF16), sd((t, rnn), F32), sd((t, hkv), F32), sd((1, d), F32),
                   sd((1, d), F32), sd((d, d), BF16), sd((rnn, d), BF16), sd((N_CHIPS, hkv, wsh), BF16)],
        scratch_shapes=[pltpu.VMEM((d, d), F32), pltpu.VMEM((rnn, d), F32), pltpu.VMEM((hkv, d), F32)],
        compiler_params=_params(("arbitrary",)),
    )(dhn2, h1, g_ffn, dy, mix, g_mix, w_out, gts, br, ba, w_br, w_ba, merged, y_rnn, o_att)


def _shift_dn(x, d, fill, row):
    return jnp.where(row >= d, pltpu.roll(x, d, 0), fill)


def _shift_up(x, d, fill, row):
    s = x.shape[0]
    return jnp.where(row < s - d, pltpu.roll(x, s - d, 0), fill)


def _conv_fwd(x, w, b, row):
    kk = w.shape[0]
    y = b + w[kk - 1:kk, :] * x
    for j in range(1, kk):
        y = y + w[kk - 1 - j:kk - j, :] * _shift_dn(x, j, 0.0, row)
    return y


def _conv_bwd(dy, x, w, row):
    kk = w.shape[0]
    dx = w[kk - 1:kk, :] * dy
    dws = [None] * kk
    dws[kk - 1] = jnp.sum(dy * x, axis=0, keepdims=True)
    for j in range(1, kk):
        ahead = _shift_up(dy, j, 0.0, row)
        dx = dx + w[kk - 1 - j:kk - j, :] * ahead
        dws[kk - 1 - j] = jnp.sum(ahead * x, axis=0, keepdims=True)
    return dx, jnp.concatenate(dws, axis=0)


def _softplus(z):
    y = jnp.exp(-jnp.abs(z))
    u = 1.0 + y
    dd = u - 1.0
    log1p = jnp.where(dd == 0.0, y, jnp.log(u) * (y / jnp.where(dd == 0.0, 1.0, dd)))
    return jnp.maximum(z, 0.0) + log1p


def _lru_decay(xb, wa, ba, lam):
    r = _sig(_dot(xb, wa) + ba)
    sp = _softplus(-lam)
    la = (-LRU_C) * r * sp
    return r, sp, la, jnp.exp(la)


def _lru_gates(xc, wa, ba, wx, bx, lam):
    xb = xc.astype(BF16)
    r, sp, la, a = _lru_decay(xb, wa, ba, lam)
    i = _sig(_dot(xb, wx) + bx)
    one_m_a2 = jnp.tanh(-la) * (1.0 + a * a)
    inv_mult = lax.rsqrt(one_m_a2)
    return r, i, sp, a, one_m_a2 * inv_mult, inv_mult


def _seg_len(s):
    seg = -(-s // 8)
    return seg + (4 - seg % 8) % 8


def _scan_rows(a_pad, u_pad, out_pad, reverse):
    planes, rows8, lanes = a_pad.shape
    seg = rows8 // 8
    sub = lax.broadcasted_iota(jnp.int32, (planes, 8, lanes), 1)

    unroll = 4

    def rows(k, d):
        i = k * unroll + d
        return pl.ds((seg - 1 - i) if reverse else i, 8, stride=seg)

    def ends(k, carry):
        h, p = carry
        for d in range(unroll):
            a = a_pad[:, rows(k, d), :]
            h = a * h + u_pad[:, rows(k, d), :]
            p = a * p
        return h, p

    init = (jnp.zeros((planes, 8, lanes), F32), jnp.ones((planes, 8, lanes), F32))
    h_end, p_end = lax.fori_loop(0, seg // unroll, ends, init)
    start = jnp.zeros((planes, 8, lanes), F32)
    for _ in range(7):
        nxt = h_end + p_end * start
        if reverse:
            start = jnp.where(sub < 7, pltpu.roll(nxt, 7, 1), 0.0)
        else:
            start = jnp.where(sub >= 1, pltpu.roll(nxt, 1, 1), 0.0)

    def redo(k, h):
        for d in range(unroll):
            h = a_pad[:, rows(k, d), :] * h + u_pad[:, rows(k, d), :]
            out_pad[:, rows(k, d), :] = h
        return h

    lax.fori_loop(0, seg // unroll, redo, start)


def _lru_cols(c, rb):
    return 2 * rb if c % (2 * rb) == 0 else rb


def rglru_fwd(xr, cw, cb, wa, ba, wx, bx, lam, name, ride=()):
    b, s, c = xr.shape
    rb = wa.shape[1]
    kk = cw.shape[0]
    cols = _lru_cols(c, rb)
    nj = cols // rb
    seg = _seg_len(s)

    def body(x_ref, cw_ref, cb_ref, wa_ref, ba_ref, wx_ref, bx_ref, lam_ref, h_ref, a_ref, xc_ref, a_pad, u_pad, h_pad):
        row = lax.broadcasted_iota(jnp.int32, (s, rb), 0)
        for j in range(nj):
            cs = slice(j * rb, (j + 1) * rb)
            xc = _conv_fwd(x_ref[:, cs], cw_ref[:, cs], cb_ref[:, cs], row)
            _, i, _, a, mult, _ = _lru_gates(xc, wa_ref[j], ba_ref[:, cs], wx_ref[j], bx_ref[:, cs], lam_ref[:, cs])
            xc_ref[:, cs] = xc
            a_ref[:, cs] = a
            a_pad[j, 0:s, :] = a
            u_pad[j, 0:s, :] = mult * (i * xc)
        a_pad[:, s:, :] = jnp.ones((nj, 8 * seg - s, rb), F32)
        u_pad[:, s:, :] = jnp.zeros((nj, 8 * seg - s, rb), F32)
        _scan_rows(a_pad, u_pad, h_pad, False)
        for j in range(nj):
            h_ref[:, j * rb:(j + 1) * rb] = h_pad[j, 0:s, :]

    vec = pl.BlockSpec((1, cols), lambda bi, n: (0, n))
    seq = pl.BlockSpec((None, s, cols), lambda bi, n: (bi, 0, n))
    mat = pl.BlockSpec((nj, rb, rb), lambda bi, n: (n, 0, 0))
    r_ins, r_in_specs, r_outs, r_out_specs, r_sems = _ride_args(ride)
    outs = pl.pallas_call(
        _riding(body, 8, 3, 3, ride, 2), name=name, grid=(b, c // cols),
        in_specs=[seq, pl.BlockSpec((kk, cols), lambda bi, n: (0, n)), vec, mat, vec, mat, vec, vec] + r_in_specs,
        out_specs=[seq] * 3 + r_out_specs,
        out_shape=[jax.ShapeDtypeStruct((b, s, c), F32)] * 3 + r_outs,
        scratch_shapes=[pltpu.VMEM((nj, 8 * seg, rb), F32)] * 3 + r_sems,
        compiler_params=_params(("arbitrary", "arbitrary")),
    )(xr, cw, cb, wa, ba, wx, bx, lam, *r_ins)
    return outs[:3], _ride_results(ride, outs[3:])


def rglru_bwd(xr, h, dh, a_fwd, xc_fwd, cw, wa, ba, wx, bx, lam, name, ride=()):
    b, s, c = xr.shape
    nb, rb = wa.shape[0], wa.shape[1]
    kk = cw.shape[0]
    cols = _lru_cols(c, rb)
    nj = cols // rb
    seg = _seg_len(s)

    def body(x_ref, h_ref, dh_ref, a_ref, xc_ref, cw_ref, wa_ref, ba_ref, wx_ref, bx_ref, lam_ref,
             dx_ref, dcw_ref, dcb_ref, dwa_ref, dba_ref, dwx_ref, dbx_ref, dlam_ref, b_pad, g_pad, l_pad):
        @pl.when(pl.program_id(1) == 0)
        def _():
            for ref in (dcw_ref, dcb_ref, dwa_ref, dba_ref, dwx_ref, dbx_ref, dlam_ref):
                ref[...] = jnp.zeros(ref.shape, F32)

        row = lax.broadcasted_iota(jnp.int32, (s, rb), 0)

        for j in range(nj):
            b_pad[j, 0:s, :] = _shift_up(a_ref[:, j * rb:(j + 1) * rb], 1, 0.0, row)
            g_pad[j, 0:s, :] = dh_ref[:, j * rb:(j + 1) * rb]
        b_pad[:, s:, :] = jnp.zeros((nj, 8 * seg - s, rb), F32)
        g_pad[:, s:, :] = jnp.zeros((nj, 8 * seg - s, rb), F32)
        _scan_rows(b_pad, g_pad, l_pad, True)

        for j in range(nj):
            cs = slice(j * rb, (j + 1) * rb)
            x = x_ref[:, cs]
            cwv = cw_ref[:, cs]
            wav, wxv, lamv = wa_ref[j], wx_ref[j], lam_ref[:, cs]
            xc = xc_ref[:, cs]
            r, i, sp, a, mult, inv_mult = _lru_gates(xc, wav, ba_ref[:, cs], wxv, bx_ref[:, cs], lamv)
            lmb = l_pad[j, 0:s, :]
            h_prev = _shift_dn(h_ref[:, cs], 1, 0.0, row)
            da = lmb * h_prev
            ixc = i * xc
            dla = da * a - (lmb * ixc) * (a * a) * inv_mult
            di = lmb * mult * xc
            dxc = lmb * mult * i
            dr = dla * ((-LRU_C) * sp)
            dsp = jnp.sum(dla * ((-LRU_C) * r), axis=0, keepdims=True)
            dga = dr * r * (1.0 - r)
            dgx = di * i * (1.0 - i)
            dga_b, dgx_b = dga.astype(BF16), dgx.astype(BF16)
            xb = xc.astype(BF16)
            dwa_ref[j] += _dot_tn(xb, dga_b)
            dwx_ref[j] += _dot_tn(xb, dgx_b)
            dba_ref[:, cs] += jnp.sum(dga, axis=0, keepdims=True)
            dbx_ref[:, cs] += jnp.sum(dgx, axis=0, keepdims=True)
            dlam_ref[:, cs] += dsp * (-_sig(-lamv))
            dxc = dxc + _dot_nt(dga_b, wav) + _dot_nt(dgx_b, wxv)
            dcb_ref[:, cs] += jnp.sum(dxc, axis=0, keepdims=True)
            dx, dcw = _conv_bwd(dxc, x, cwv, row)
            dcw_ref[:, cs] += dcw
            dx_ref[:, cs] = dx.astype(dx_ref.dtype)

    vec = pl.BlockSpec((1, cols), lambda n, bi: (0, n))
    seq = pl.BlockSpec((None, s, cols), lambda n, bi: (bi, 0, n))
    mat = pl.BlockSpec((nj, rb, rb), lambda n, bi: (n, 0, 0))
    cws = pl.BlockSpec((kk, cols), lambda n, bi: (0, n))
    sd = jax.ShapeDtypeStruct
    r_ins, r_in_specs, r_outs, r_out_specs, r_sems = _ride_args(ride)
    outs = pl.pallas_call(
        _riding(body, 11, 8, 3, ride, 2), name=name, grid=(c // cols, b),
        in_specs=[seq, seq, seq, seq, seq, cws, mat, vec, mat, vec, vec] + r_in_specs,
        out_specs=[seq, cws, vec, mat, vec, mat, vec, vec] + r_out_specs,
        out_shape=[sd((b, s, c), BF16), sd((kk, c), F32), sd((1, c), F32), sd((nb, rb, rb), F32),
                   sd((1, c), F32), sd((nb, rb, rb), F32), sd((1, c), F32), sd((1, c), F32)] + r_outs,
        scratch_shapes=[pltpu.VMEM((nj, 8 * seg, rb), F32)] * 3 + r_sems,
        compiler_params=_params(("arbitrary", "arbitrary")),
    )(xr, h, dh, a_fwd, xc_fwd, cw, wa, ba, wx, bx, lam, *r_ins)
    return outs[:8], _ride_results(ride, outs[8:])


_GELU_C = math.sqrt(2.0 / math.pi)


def _gelu_parts(x):
    th = jnp.tanh(_GELU_C * (x + 0.044715 * x * x * x))
    gel = 0.5 * x * (1.0 + th)
    dgel = 0.5 * (1.0 + th) + 0.5 * x * (1.0 - th * th) * _GELU_C * (1.0 + 3 * 0.044715 * x * x)
    return gel, dgel


def ffn_in_act(x, g, wg, wu, cw, cb, seq_len, name, tm=256):
    t, d = x.shape
    f = N_CHIPS * wg.shape[2]
    kk = cw.shape[0]
    tm = _tile(seq_len, tm)
    tiles_per_seq = seq_len // tm
    keep = 8
    assert kk - 1 <= keep

    def body(x_ref, g_ref, wg_ref, wu_ref, cw_ref, cb_ref, hn_ref, gp_ref, up_ref, act_ref, tail):
        @pl.when(pl.program_id(0) % tiles_per_seq == 0)
        def _():
            tail[...] = jnp.zeros(tail.shape, F32)

        xv = x_ref[...]
        inv = lax.rsqrt(jnp.mean(xv * xv, axis=-1, keepdims=True) + EPS)
        hn = (xv * inv * g_ref[...]).astype(BF16)
        hn_ref[...] = hn
        gp = jnp.concatenate([_dot(hn, wg_ref[j]) for j in range(N_CHIPS)], axis=1)
        up = jnp.concatenate([_dot(hn, wu_ref[j]) for j in range(N_CHIPS)], axis=1)
        gp_ref[...] = gp
        up_ref[...] = up
        cwv = cw_ref[...]
        row = lax.broadcasted_iota(jnp.int32, (tm, 1), 0)
        gate = _conv_fwd(gp, cwv, cb_ref[...], row)
        row8 = lax.broadcasted_iota(jnp.int32, (keep, 1), 0)
        prev = tail[...]
        fix = jnp.zeros((keep, f), F32)
        for j in range(1, kk):
            fix = fix + cwv[kk - 1 - j:kk - j, :] * jnp.where(row8 < j, pltpu.roll(prev, j, 0), 0.0)
        gate = jnp.concatenate([gate[:keep] + fix, gate[keep:]], axis=0)
        tail[...] = gp[tm - keep:, :]
        gel, _ = _gelu_parts(gate)
        act_ref[...] = (gel * up).astype(BF16)

    sd = jax.ShapeDtypeStruct
    return pl.pallas_call(
        body, name=name, grid=(t // tm,),
        in_specs=[_rows(tm, d), _whole(g.shape), _whole(wg.shape), _whole(wu.shape), _whole(cw.shape), _whole(cb.shape)],
        out_specs=[_rows(tm, d), _rows(tm, f), _rows(tm, f), _rows(tm, f)],
        out_shape=[sd((t, d), BF16), sd((t, f), F32), sd((t, f), F32), sd((t, f), BF16)],
        scratch_shapes=[pltpu.VMEM((keep, f), F32)],
        compiler_params=_params(("arbitrary",)),
    )(x, g, wg, wu, cw, cb)


def ffn_in_bwd(dact, gate_pre, up, cw, cb, wg, wu, seq_len, name, ride=(), tm=256):
    t, f = gate_pre.shape
    d = wg.shape[1]
    fs = f // N_CHIPS
    kk = cw.shape[0]
    tm = _tile(seq_len, tm)
    nt = t // tm
    tiles_per_seq = seq_len // tm
    keep = 8
    assert kk - 1 <= keep

    def body(da_ref, g_ref, halo_ref, u_ref, cw_ref, cb_ref, wg_ref, wu_ref,
             dg_ref, du_ref, dhn_ref, dcw_ref, dcb_ref, nxt):
        tile = (nt - 1 - pl.program_id(0)) % tiles_per_seq

        @pl.when(pl.program_id(0) == 0)
        def _():
            dcw_ref[...] = jnp.zeros(dcw_ref.shape, F32)
            dcb_ref[...] = jnp.zeros(dcb_ref.shape, F32)

        @pl.when(tile == tiles_per_seq - 1)
        def _():
            nxt[...] = jnp.zeros(nxt.shape, F32)

        row = lax.broadcasted_iota(jnp.int32, (tm, 1), 0)
        row8 = lax.broadcasted_iota(jnp.int32, (keep, 1), 0)
        gp = g_ref[...]
        cwv = cw_ref[...]
        prev = jnp.where(tile > 0, halo_ref[...], 0.0)
        gate = _conv_fwd(gp, cwv, cb_ref[...], row)
        fix = jnp.zeros((keep, f), F32)
        for j in range(1, kk):
            fix = fix + cwv[kk - 1 - j:kk - j, :] * jnp.where(row8 < j, pltpu.roll(prev, j, 0), 0.0)
        gate = jnp.concatenate([gate[:keep] + fix, gate[keep:]], axis=0)
        gel, dgel = _gelu_parts(gate)
        da = da_ref[...]
        dup = (da * gel).astype(BF16)
        du_ref[...] = dup
        dgate = da * u_ref[...] * dgel
        dcb_ref[...] += jnp.sum(dgate, axis=0, keepdims=True)
        after = nxt[...]
        dgp = cwv[kk - 1:kk, :] * dgate
        tail_fix = jnp.zeros((keep, f), F32)
        dws = [None] * kk
        dws[kk - 1] = jnp.sum(dgate * gp, axis=0, keepdims=True)
        for j in range(1, kk):
            wj = cwv[kk - 1 - j:kk - j, :]
            dgp = dgp + wj * _shift_up(dgate, j, 0.0, row)
            tail_fix = tail_fix + wj * jnp.where(row8 >= keep - j, pltpu.roll(after, keep - j, 0), 0.0)
            dws[kk - 1 - j] = (jnp.sum(dgate * _shift_dn(gp, j, 0.0, row), axis=0, keepdims=True)
                               + jnp.sum(dgate[:keep] * jnp.where(row8 < j, pltpu.roll(prev, j, 0), 0.0),
                                         axis=0, keepdims=True))
        dgp = jnp.concatenate([dgp[:tm - keep], dgp[tm - keep:] + tail_fix], axis=0).astype(BF16)
        nxt[...] = dgate[:keep]
        dcw_ref[...] += jnp.concatenate(dws, axis=0)
        dg_ref[...] = dgp
        dhn = None
        for j in range(N_CHIPS):
            cs = slice(j * fs, (j + 1) * fs)
            part = _dot_nt(dgp[:, cs], wg_ref[j]) + _dot_nt(dup[:, cs], wu_ref[j])
            dhn = part if dhn is None else dhn + part
        dhn_ref[...] = dhn

    def rev(i):
        return nt - 1 - i

    rows_f = pl.BlockSpec((tm, f), lambda i: (rev(i), 0))
    halo = pl.BlockSpec((None, keep, f), lambda i: (jnp.maximum(rev(i) * (tm // keep) - 1, 0), 0, 0))
    once = pl.Buffered(1)
    sd = jax.ShapeDtypeStruct
    r_ins, r_in_specs, r_outs, r_out_specs, r_sems = _ride_args(ride)
    outs = pl.pallas_call(
        _riding(body, 8, 5, 1, ride, 1), name=name, grid=(nt,),
        in_specs=[rows_f, rows_f, halo, rows_f, _whole(cw.shape), _whole(cb.shape),
                  pl.BlockSpec(wg.shape, lambda i: (0, 0, 0), pipeline_mode=once),
                  pl.BlockSpec(wu.shape, lambda i: (0, 0, 0), pipeline_mode=once)] + r_in_specs,
        out_specs=[rows_f, rows_f, pl.BlockSpec((tm, d), lambda i: (rev(i), 0)), _whole((kk, f)), _whole((1, f))]
        + r_out_specs,
        out_shape=[sd((t, f), BF16), sd((t, f), BF16), sd((t, d), F32), sd((kk, f), F32), sd((1, f), F32)] + r_outs,
        scratch_shapes=[pltpu.VMEM((keep, f), F32)] + r_sems,
        compiler_params=_params(("arbitrary",)),
    )(dact, gate_pre, gate_pre.reshape(t // keep, keep, f), up, cw, cb, wg, wu, *r_ins)
    return outs[:5], _ride_results(ride, outs[5:])


def _t5_bucket(dist):
    max_exact = REL_BUCKETS // 2
    d = np.maximum(dist, 1).astype(np.float32)
    large = max_exact + np.log(d / max_exact) / math.log(REL_MAX_DIST / max_exact) * (REL_BUCKETS - max_exact)
    large = np.minimum(large.astype(np.int32), REL_BUCKETS - 1)
    return np.where(dist < max_exact, dist, large).astype(np.int32)


def _band(window, dilation):
    qi = np.arange(ATTN_BLOCK)[:, None]
    kj = np.arange(2 * ATTN_BLOCK)[None, :]
    delta = ATTN_BLOCK + qi - kj
    mask = (delta >= 0) & (delta <= window // dilation)
    bucket = _t5_bucket(np.maximum(delta, 0) * dilation)
    return mask, bucket


def _attn_blocks(s, r):
    m = s // r
    assert m % ATTN_BLOCK == 0, "sequence length must be a multiple of dilation * block"
    return m // ATTN_BLOCK


def _perm_load(ref, r):
    if r == 1:
        return ref[...]
    m = ref.shape[0] // r
    return jnp.concatenate([ref[pl.ds(c, m, stride=r), :] for c in range(r)], axis=0)


def _perm_store(ref, g, val, r, add=False):
    if r == 1:
        ref[g] = ref[g] + val if add else val
        return
    m = val.shape[0] // r
    for c in range(r):
        rows = pl.ds(c, m, stride=r)
        part = val[c * m:(c + 1) * m]
        ref[g, rows, :] = ref[g, rows, :] + part if add else part


def _blocks(x):
    return x.reshape(x.shape[0] // ATTN_BLOCK, ATTN_BLOCK, x.shape[1])


def _prev_blocks(x):
    return jnp.concatenate([x[:1], x[:-1]], axis=0)


def _next_blocks(x):
    return jnp.concatenate([x[1:], jnp.zeros_like(x[:1])], axis=0)


def _first_block_neg(s, r):
    nblk = s // ATTN_BLOCK
    idx = lax.broadcasted_iota(jnp.int32, (nblk, 1, 1), 0)
    return jnp.where(idx % _attn_blocks(s, r) == 0, NEG, 0.0)


def _bdot_nt(a, b):
    return lax.dot_general(a, b, (((2,), (2,)), ((0,), (0,))), preferred_element_type=F32)


def _bdot(a, b):
    return lax.dot_general(a, b, (((2,), (1,)), ((0,), (0,))), preferred_element_type=F32)


def _bdot_tn(a, b):
    return lax.dot_general(a, b, (((1,), (1,)), ((0,), (0,))), preferred_element_type=F32)


def attn_fwd(qkv, biasm, n_heads, name, ride=()):
    b, s, _ = qkv.shape
    h = n_heads
    scale = HEAD_DIM ** -0.5
    blk = ATTN_BLOCK

    def body(q1_ref, q2_ref, q3_ref, k_ref, v_ref, bias_ref, o_ref, lse_ref, acc, m_s, l_s):
        for g, q_ref in enumerate((q1_ref, q2_ref, q3_ref)):
            r = DILATED[g][1]
            first = _first_block_neg(s, r)
            q = _blocks(_perm_load(q_ref, r).astype(BF16))
            k = _blocks(_perm_load(k_ref, r).astype(BF16))
            v = _blocks(_perm_load(v_ref, r).astype(BF16))
            s_cur = _bdot_nt(q, k) * scale + bias_ref[g, :, blk:]
            s_prev = _bdot_nt(q, _prev_blocks(k)) * scale + bias_ref[g, :, :blk] + first
            m = jnp.max(jnp.maximum(s_cur, s_prev), axis=-1, keepdims=True)
            p_cur = jnp.exp(s_cur - m)
            p_prev = jnp.exp(s_prev - m)
            l = jnp.sum(p_cur + p_prev, axis=-1, keepdims=True)
            o = _bdot(p_cur.astype(BF16), v) + _bdot(p_prev.astype(BF16), _prev_blocks(v))
            _perm_store(acc, g, o.reshape(s, HEAD_DIM), r)
            _perm_store(m_s, g, m.reshape(s, 1), r)
            _perm_store(l_s, g, l.reshape(s, 1), r)
        m_all = jnp.maximum(jnp.maximum(m_s[0], m_s[1]), m_s[2])
        w = [jnp.exp(m_s[g] - m_all) for g in range(N_GROUPS)]
        l = w[0] * l_s[0] + w[1] * l_s[1] + w[2] * l_s[2]
        o_ref[...] = (w[0] * acc[0] + w[1] * acc[1] + w[2] * acc[2]) / l
        lse_ref[...] = m_all + jnp.log(l)

    def col(j):
        return pl.BlockSpec((None, s, HEAD_DIM), lambda bi, hi, j=j: (bi, 0, j * h + hi))

    r_ins, r_in_specs, r_outs, r_out_specs, r_sems = _ride_args(ride)
    outs = pl.pallas_call(
        _riding(body, 6, 2, 3, ride, 2), name=name, grid=(b, h),
        in_specs=[col(0), col(1), col(2), col(3), col(4),
                  pl.BlockSpec((N_GROUPS, None, blk, 2 * blk), lambda bi, hi: (0, hi, 0, 0))] + r_in_specs,
        out_specs=[pl.BlockSpec((None, s, HEAD_DIM), lambda bi, hi: (bi, 0, hi)),
                   pl.BlockSpec((None, None, s, 1), lambda bi, hi: (bi, hi, 0, 0))] + r_out_specs,
        out_shape=[jax.ShapeDtypeStruct((b, s, h * HEAD_DIM), F32), jax.ShapeDtypeStruct((b, h, s, 1), F32)] + r_outs,
        scratch_shapes=[pltpu.VMEM((N_GROUPS, s, HEAD_DIM), F32), pltpu.VMEM((N_GROUPS, s, 1), F32),
                        pltpu.VMEM((N_GROUPS, s, 1), F32)] + r_sems,
        compiler_params=_params(("arbitrary", "arbitrary")),
    )(qkv, qkv, qkv, qkv, qkv, biasm, *r_ins)
    return outs[0], outs[1], _ride_results(ride, outs[2:])


def attn_bwd(qkv, biasm, o, lse, do, n_heads, name, ride=()):
    b, s, _ = qkv.shape
    h = n_heads
    scale = HEAD_DIM ** -0.5
    blk = ATTN_BLOCK

    def body(q1_ref, q2_ref, q3_ref, k_ref, v_ref, bias_ref, o_ref, lse_ref, do_ref,
             dq1_ref, dq2_ref, dq3_ref, dk_ref, dv_ref, ds_ref, dq_acc, kv_acc, delta):
        delta[...] = jnp.sum(do_ref[...] * o_ref[...], axis=-1, keepdims=True)
        kv_acc[...] = jnp.zeros(kv_acc.shape, F32)
        for g, q_ref in enumerate((q1_ref, q2_ref, q3_ref)):
            r = DILATED[g][1]
            first = _first_block_neg(s, r)
            q = _blocks(_perm_load(q_ref, r).astype(BF16))
            k = _blocks(_perm_load(k_ref, r).astype(BF16))
            v = _blocks(_perm_load(v_ref, r).astype(BF16))
            dob = _blocks(_perm_load(do_ref, r).astype(BF16))
            lse_b = _blocks(_perm_load(lse_ref, r))
            dl_b = _blocks(_perm_load(delta, r))
            k_prev, v_prev = _prev_blocks(k), _prev_blocks(v)
            p_cur = jnp.exp(_bdot_nt(q, k) * scale + bias_ref[g, :, blk:] - lse_b)
            p_prev = jnp.exp(_bdot_nt(q, k_prev) * scale + bias_ref[g, :, :blk] + first - lse_b)
            ds_cur = p_cur * (_bdot_nt(dob, v) - dl_b)
            ds_prev = p_prev * (_bdot_nt(dob, v_prev) - dl_b)
            ds_ref[g, :, blk:] = jnp.sum(ds_cur, axis=0)
            ds_ref[g, :, :blk] = jnp.sum(ds_prev, axis=0)
            ds_cur_b, ds_prev_b = ds_cur.astype(BF16), ds_prev.astype(BF16)
            dq = (_bdot(ds_cur_b, k) + _bdot(ds_prev_b, k_prev)) * scale
            _perm_store(dq_acc, g, dq.reshape(s, HEAD_DIM), r)
            dk = (_bdot_tn(ds_cur_b, q) + _next_blocks(_bdot_tn(ds_prev_b, q))) * scale
            dv = _bdot_tn(p_cur.astype(BF16), dob) + _next_blocks(_bdot_tn(p_prev.astype(BF16), dob))
            _perm_store(kv_acc, 0, dk.reshape(s, HEAD_DIM), r, add=True)
            _perm_store(kv_acc, 1, dv.reshape(s, HEAD_DIM), r, add=True)
        for g, out_ref in enumerate((dq1_ref, dq2_ref, dq3_ref)):
            out_ref[...] = dq_acc[g].astype(out_ref.dtype)
        dk_ref[...] = kv_acc[0].astype(dk_ref.dtype)
        dv_ref[...] = kv_acc[1].astype(dv_ref.dtype)

    def col(j):
        return pl.BlockSpec((None, s, HEAD_DIM), lambda bi, hi, j=j: (bi, 0, j * h + hi))

    head = pl.BlockSpec((None, s, HEAD_DIM), lambda bi, hi: (bi, 0, hi))
    sd = jax.ShapeDtypeStruct
    r_ins, r_in_specs, r_outs, r_out_specs, r_sems = _ride_args(ride)
    outs = pl.pallas_call(
        _riding(body, 9, 6, 3, ride, 2), name=name, grid=(b, h),
        in_specs=[col(0), col(1), col(2), col(3), col(4),
                  pl.BlockSpec((N_GROUPS, None, blk, 2 * blk), lambda bi, hi: (0, hi, 0, 0)),
                  head, pl.BlockSpec((None, None, s, 1), lambda bi, hi: (bi, hi, 0, 0)), head] + r_in_specs,
        out_specs=[head] * 5 + [pl.BlockSpec((None, None, N_GROUPS, blk, 2 * blk), lambda bi, hi: (bi, hi, 0, 0, 0))]
        + r_out_specs,
        out_shape=[sd((b, s, h * HEAD_DIM), BF16)] * 5 + [sd((b, h, N_GROUPS, blk, 2 * blk), F32)] + r_outs,
        scratch_shapes=[pltpu.VMEM((N_GROUPS, s, HEAD_DIM), F32), pltpu.VMEM((2, s, HEAD_DIM), F32),
                        pltpu.VMEM((s, 1), F32)] + r_sems,
        compiler_params=_params(("arbitrary", "arbitrary")),
    )(qkv, qkv, qkv, qkv, qkv, biasm, o, lse, do, *r_ins)
    return outs[:6], _ride_results(ride, outs[6:])


def bias_table(rel_rows, bucket_f, n_heads, name):
    g, blk, blk2 = bucket_f.shape
    h = n_heads

    def body(rb_ref, bk_ref, o_ref):
        for gi in range(g):
            bk = bk_ref[gi]
            for hi in range(h):
                rb = rb_ref[gi * h + hi]
                acc = jnp.full((blk, blk2), NEG, F32)
                for bucket in range(REL_BUCKETS):
                    acc = jnp.where(bk == float(bucket), rb[:, bucket:bucket + 1], acc)
                o_ref[gi, hi] = acc

    vmem = pl.BlockSpec(memory_space=pltpu.VMEM)
    return pl.pallas_call(
        body, name=name, in_specs=[vmem, vmem], out_specs=vmem,
        out_shape=jax.ShapeDtypeStruct((g, h, blk, blk2), F32),
        compiler_params=_params(),
    )(rel_rows, bucket_f)


def bias_grad(ds_sum, bucket_f, name):
    b, h, g, blk, blk2 = ds_sum.shape

    def body(ds_ref, bk_ref, o_ref):
        lane = lax.broadcasted_iota(jnp.int32, (1, 128), 1)
        for gi in range(g):
            bk = bk_ref[gi]
            for hi in range(h):
                tot = ds_ref[0, hi, gi]
                for bi in range(1, b):
                    tot = tot + ds_ref[bi, hi, gi]
                vec = jnp.zeros((1, 128), F32)
                for bucket in range(REL_BUCKETS):
                    val = jnp.sum(jnp.where(bk == float(bucket), tot, 0.0), keepdims=True)
                    vec = vec + jnp.where(lane == bucket, val, 0.0)
                o_ref[gi * h + hi] = vec

    vmem = pl.BlockSpec(memory_space=pltpu.VMEM)
    return pl.pallas_call(
        body, name=name, in_specs=[vmem, vmem], out_specs=vmem,
        out_shape=jax.ShapeDtypeStruct((g * h, 1, 128), F32),
        compiler_params=_params(),
    )(ds_sum, bucket_f)


def _chip_peers():
    x, y, c = lax.axis_index("x"), lax.axis_index("y"), lax.axis_index("c")
    me = 2 * x + y
    peers = [(1 - x, y, c), (x, 1 - y, c), (1 - x, 1 - y, c)]
    peer_chip = [2 * (1 - x) + y, 2 * x + (1 - y), 2 * (1 - x) + (1 - y)]
    return me, peers, peer_chip


def _any_specs(n):
    return [pl.BlockSpec(memory_space=pl.ANY)] * n


_MID_NUM, _MID_DEN = 3, 4


class _Exchange:
    def start(self, ins, outs, sems):
        local, sends, _ = self._copies(ins, outs, sems)
        for cp in local + sends:
            cp.start()

    def mid(self, ins, outs, sems):
        pass

    def wait(self, ins, outs, sems):
        local, sends, recvs = self._copies(ins, outs, sems)
        for cp in recvs():
            cp.wait_recv()
        for cp in sends:
            cp.wait_send()
        for cp in local:
            cp.wait()


class _Gather(_Exchange):
    HALF_ROWS = 16

    def __init__(self, arrays):
        n = len(arrays)
        self.ins = list(arrays)
        self.split = [a.shape[0] % (2 * self.HALF_ROWS) == 0 for a in arrays]
        self.out_shape = [jax.ShapeDtypeStruct((N_CHIPS,) + a.shape, a.dtype) for a in arrays]
        dma = pltpu.SemaphoreType.DMA
        self.sems = [dma((3 * n,)), dma((3 * n,)), dma((n,)), dma((3 * n,)), dma((3 * n,))]

    def _half(self, i, ref, sibling=False):
        if not self.split[i]:
            return ref
        half = self.ins[i].shape[0] // 2
        c = lax.axis_index("c")
        c = 1 - c if sibling else c
        return ref.at[pl.ds(pl.multiple_of(c * half, self.HALF_ROWS), half)]

    def _plan(self, ins, outs, sems):
        send1, recv1, local_sems, send2, recv2 = sems
        me, peers, peer_chip = _chip_peers()
        x, y, c = lax.axis_index("x"), lax.axis_index("y"), lax.axis_index("c")
        n = len(ins)
        pairs = [(i, k) for k in range(3) for i in range(n)]

        def fetch(i, k, slot):
            return pltpu.make_async_remote_copy(src_ref=self._half(i, ins[i]), dst_ref=self._half(i, outs[i].at[slot]),
                                                send_sem=send1.at[3 * i + k], recv_sem=recv1.at[3 * i + k],
                                                device_id=peers[k], device_id_type=MESH)

        def share(i, k, sibling):
            part = self._half(i, outs[i].at[peer_chip[k]], sibling)
            return pltpu.make_async_remote_copy(src_ref=part, dst_ref=part, send_sem=send2.at[3 * i + k],
                                                recv_sem=recv2.at[3 * i + k], device_id=(x, y, 1 - c),
                                                device_id_type=MESH)

        split_pairs = [(i, k) for i, k in pairs if self.split[i]]
        return dict(
            local=lambda: [pltpu.make_async_copy(ins[i], outs[i].at[me], local_sems.at[i]) for i in range(n)],
            fetch_out=lambda: [fetch(i, k, me) for i, k in pairs],
            fetch_in=lambda: [(fetch(i, k, peer_chip[k]), share(i, k, False) if self.split[i] else None)
                              for i, k in pairs],
            share_out=lambda: [share(i, k, False) for i, k in split_pairs],
            share_in=lambda: [share(i, k, True) for i, k in split_pairs])

    def start(self, ins, outs, sems):
        plan = self._plan(ins, outs, sems)
        for cp in plan["local"]() + plan["fetch_out"]():
            cp.start()

    def mid(self, ins, outs, sems):
        plan = self._plan(ins, outs, sems)
        for arrived, forward in plan["fetch_in"]():
            arrived.wait_recv()
            if forward is not None:
                forward.start()

    def wait(self, ins, outs, sems):
        plan = self._plan(ins, outs, sems)
        for cp in plan["share_in"]():
            cp.wait_recv()
        for cp in plan["fetch_out"]() + plan["share_out"]():
            cp.wait_send()
        for cp in plan["local"]():
            cp.wait()


class _Scatter(_Exchange):
    def __init__(self, slabs, whole=()):
        self.n_slabs = len(slabs)
        self.ins = list(slabs) + list(whole)
        n = len(self.ins)
        self.out_shape = [jax.ShapeDtypeStruct(a.shape, a.dtype) for a in slabs] \
            + [jax.ShapeDtypeStruct((N_CHIPS,) + a.shape, a.dtype) for a in whole]
        self.sems = [pltpu.SemaphoreType.DMA((3 * n,)), pltpu.SemaphoreType.DMA((3 * n,)), pltpu.SemaphoreType.DMA((n,))]

    def _copies(self, ins, outs, sems):
        send_sems, recv_sems, local_sems = sems
        me, peers, peer_chip = _chip_peers()
        n = len(ins)

        def src(i, chip):
            return ins[i].at[chip] if i < self.n_slabs else ins[i]

        def remote(i, k, src_chip, slot):
            return pltpu.make_async_remote_copy(src_ref=src(i, src_chip), dst_ref=outs[i].at[slot],
                                                send_sem=send_sems.at[3 * i + k], recv_sem=recv_sems.at[3 * i + k],
                                                device_id=peers[k], device_id_type=MESH)

        local = [pltpu.make_async_copy(src(i, me), outs[i].at[me], local_sems.at[i]) for i in range(n)]
        sends = [remote(i, k, peer_chip[k], me) for i in range(n) for k in range(3)]
        return local, sends, lambda: [remote(i, k, me, peer_chip[k]) for i in range(n) for k in range(3)]


class _Swap(_Exchange):
    def __init__(self, arrays):
        n = len(arrays)
        self.ins = list(arrays)
        self.out_shape = [jax.ShapeDtypeStruct(a.shape, a.dtype) for a in arrays]
        self.sems = [pltpu.SemaphoreType.DMA((n,)), pltpu.SemaphoreType.DMA((n,))]

    def _copies(self, ins, outs, sems):
        send_sems, recv_sems = sems
        x, y, c = lax.axis_index("x"), lax.axis_index("y"), lax.axis_index("c")
        cps = [pltpu.make_async_remote_copy(src_ref=ins[i], dst_ref=outs[i], send_sem=send_sems.at[i],
                                            recv_sem=recv_sems.at[i], device_id=(x, y, 1 - c), device_id_type=MESH)
               for i in range(len(ins))]
        return [], cps, lambda: cps


def _riding(body, n_in, n_out, n_scratch, ride, rank):
    if not ride:
        return body
    r_in = sum(len(e.ins) for e in ride)
    r_out = sum(len(e.out_shape) for e in ride)

    def split(refs, sizes):
        out, a = [], 0
        for sz in sizes:
            out.append(refs[a:a + sz])
            a += sz
        return out

    def wrapped(*refs):
        a = 0
        parts = []
        for sz in (n_in, r_in, n_out, r_out, n_scratch):
            parts.append(refs[a:a + sz])
            a += sz
        own_in, ex_in, own_out, ex_out, own_scratch = parts
        ex_sems = refs[a:]
        ins = split(ex_in, [len(e.ins) for e in ride])
        outs = split(ex_out, [len(e.out_shape) for e in ride])
        sems = split(ex_sems, [len(e.sems) for e in ride])
        if rank:
            step, total = 0, 1
            for d in range(rank):
                step = step * pl.num_programs(d) + pl.program_id(d)
                total = total * pl.num_programs(d)

            @pl.when(step == 0)
            def _():
                for e, i, o, s in zip(ride, ins, outs, sems):
                    e.start(i, o, s)

            body(*own_in, *own_out, *own_scratch)

            @pl.when(step == (total * _MID_NUM) // _MID_DEN)
            def _():
                for e, i, o, s in zip(ride, ins, outs, sems):
                    e.mid(i, o, s)

            @pl.when(step == total - 1)
            def _():
                for e, i, o, s in zip(ride, ins, outs, sems):
                    e.wait(i, o, s)
        else:
            for phase in ("start", "mid", "wait"):
                for e, i, o, s in zip(ride, ins, outs, sems):
                    getattr(e, phase)(i, o, s)

    return wrapped


def _ride_args(ride):
    ins = [a for e in ride for a in e.ins]
    outs = [s for e in ride for s in e.out_shape]
    sems = [s for e in ride for s in e.sems]
    return ins, _any_specs(len(ins)), outs, _any_specs(len(outs)), sems


def _ride_results(ride, flat):
    out, a = [], 0
    for e in ride:
        out.append(list(flat[a:a + len(e.out_shape)]))
        a += len(e.out_shape)
    return out


def exchange(ride, name):
    ins, in_specs, outs, out_specs, sems = _ride_args(ride)
    res = pl.pallas_call(
        _riding(lambda: None, 0, 0, 0, ride, 0), name=name,
        in_specs=in_specs, out_specs=out_specs, out_shape=outs, scratch_shapes=sems,
    )(*ins)
    return _ride_results(ride, res)


def _sum_slots(ref):
    acc = ref[0].astype(F32)
    for j in range(1, ref.shape[0]):
        acc = acc + ref[j].astype(F32)
    return acc


def sum_pairs(mine, other, name, tr=176):
    n, r, w = mine.shape
    tr = r if r <= tr else _tile(r, tr)

    def body(a_ref, b_ref, o_ref):
        o_ref[...] = _sum_slots(a_ref) + _sum_slots(b_ref)

    spec = pl.BlockSpec((n, tr, w), lambda i: (0, i, 0))
    return pl.pallas_call(
        body, name=name, grid=(r // tr,),
        in_specs=[spec, spec], out_specs=_rows(tr, w),
        out_shape=jax.ShapeDtypeStruct((r, w), F32),
        compiler_params=_params(("parallel",)),
    )(mine, other)


def _adamw_update(w, m, v, g):
    c1 = 1.0 - ADAM_B1 ** ADAM_STEP
    c2 = 1.0 - ADAM_B2 ** ADAM_STEP
    nm = ADAM_B1 * m + (1.0 - ADAM_B1) * g
    nv = ADAM_B2 * v + (1.0 - ADAM_B2) * (g * g)
    return nm, nv, (-ADAM_LR) * ((nm / c1) / (jnp.sqrt(nv / c2) + ADAM_EPS) + ADAM_WD * w)


def adamw(w, m, v, mine, other, name, tr=256):
    r, c = w.shape
    tr = r if r % 8 else _tile(r, tr)

    def body(w_ref, m_ref, v_ref, a_ref, b_ref, g_ref, d_ref, nm_ref, nv_ref):
        g = _sum_slots(a_ref) + _sum_slots(b_ref)
        nm, nv, delta = _adamw_update(w_ref[...], m_ref[...], v_ref[...], g)
        g_ref[...] = g
        nm_ref[...] = nm
        nv_ref[...] = nv
        d_ref[...] = delta

    spec = _rows(tr, c)
    gspec = pl.BlockSpec((N_CHIPS, tr, c), lambda i: (0, i, 0))
    return pl.pallas_call(
        body, name=name, grid=(r // tr,),
        in_specs=[spec] * 3 + [gspec] * 2, out_specs=[spec] * 4,
        out_shape=[jax.ShapeDtypeStruct((r, c), F32)] * 4,
        compiler_params=_params(("parallel",)),
    )(w, m, v, mine, other)


def adamw_small(ws, ms, vs, gs, name):
    n = len(ws)

    def body(*refs):
        ins, outs = refs[:4 * n], refs[4 * n:]
        for i in range(n):
            w_ref, m_ref, v_ref, g_ref = ins[4 * i:4 * i + 4]
            d_ref, nm_ref, nv_ref = outs[3 * i:3 * i + 3]
            nm, nv, delta = _adamw_update(w_ref[...], m_ref[...], v_ref[...], g_ref[...])
            d_ref[...] = delta
            nm_ref[...] = nm
            nv_ref[...] = nv

    flat = [a for quad in zip(ws, ms, vs, gs) for a in quad]
    vmem = pl.BlockSpec(memory_space=pltpu.VMEM)
    outs = pl.pallas_call(
        body, name=name,
        in_specs=[vmem] * (4 * n), out_specs=[vmem] * (3 * n),
        out_shape=[jax.ShapeDtypeStruct(w.shape, F32) for w in ws for _ in range(3)],
        compiler_params=_params(),
    )(*flat)
    return [tuple(outs[3 * i:3 * i + 3]) for i in range(n)]


_PARAMS = (
    ("rel_bias", None), ("norm_mix_pre", None), ("norm_mix_post", None), ("w_in", 1), ("conv_rnn_w", 1),
    ("conv_rnn_b", None), ("w_rg_a", None), ("b_rg_a", None), ("w_rg_x", None), ("b_rg_x", None),
    ("lru_lambda", None), ("w_branch_rnn", 0), ("w_branch_att", 1), ("w_out", 0), ("norm_ffn_pre", None),
    ("norm_ffn_post", None), ("w_ffn_gate", 1), ("w_ffn_up", 1), ("conv_ffn_w", 1), ("conv_ffn_b", None),
    ("w_ffn_down", 0),
)
_SMALL = 65536


def _as2d(a):
    a = a[0] if a.shape[0] == 1 and a.ndim >= 3 else a
    return a.reshape(-1, a.shape[-1]) if a.ndim == 3 else a


def _pack(pieces, dtype):
    flat = jnp.concatenate([p.astype(dtype).reshape(-1) for p in pieces])
    unit = PACK_W * PACK_ROWS
    pad = (-flat.shape[0]) % unit
    flat = jnp.pad(flat, (0, pad))
    return flat.reshape(-1, PACK_W)


def _unpack(buf, shapes):
    flat = buf.reshape(-1)
    out, off = [], 0
    for shp in shapes:
        n = int(np.prod(shp))
        out.append(flat[off:off + n].reshape(shp))
        off += n
    return out


def _join(slots, ax):
    if ax == 0:
        return slots.reshape(-1, slots.shape[-1])
    return jnp.transpose(slots, (1, 0, 2)).reshape(slots.shape[1], -1)


def _cut(full, ax):
    if ax == 0:
        return full.reshape(N_CHIPS, -1, full.shape[-1])
    return jnp.transpose(full.reshape(full.shape[0], N_CHIPS, -1), (1, 0, 2))


def kernel(x, rel_bias, norm_mix_pre, norm_mix_post, w_in, conv_rnn_w, conv_rnn_b, w_rg_a, b_rg_a, w_rg_x, b_rg_x, lru_lambda, w_branch_rnn, w_branch_att, w_out, norm_ffn_pre, norm_ffn_post, w_ffn_gate, w_ffn_up, conv_ffn_w, conv_ffn_b, w_ffn_down, loss_target, m_rel_bias, m_norm_mix_pre, m_norm_mix_post, m_w_in, m_conv_rnn_w, m_conv_rnn_b, m_w_rg_a, m_b_rg_a, m_w_rg_x, m_b_rg_x, m_lru_lambda, m_w_branch_rnn, m_w_branch_att, m_w_out, m_norm_ffn_pre, m_norm_ffn_post, m_w_ffn_gate, m_w_ffn_up, m_conv_ffn_w, m_conv_ffn_b, m_w_ffn_down, v_rel_bias, v_norm_mix_pre, v_norm_mix_post, v_w_in, v_conv_rnn_w, v_conv_rnn_b, v_w_rg_a, v_b_rg_a, v_w_rg_x, v_b_rg_x, v_lru_lambda, v_w_branch_rnn, v_w_branch_att, v_w_out, v_norm_ffn_pre, v_norm_ffn_post, v_w_ffn_gate, v_w_ffn_up, v_conv_ffn_w, v_conv_ffn_b, v_w_ffn_down):
    args = dict(locals())
    names = [n for n, _ in _PARAMS]
    axis = dict(_PARAMS)
    w_loc = {n: args[n] for n in names}
    m_loc = {n: args["m_" + n] for n in names}
    v_loc = {n: args["v_" + n] for n in names}
    sharded = [n for n in names if axis[n] is not None]
    replicated = [n for n in names if axis[n] is None]

    big = [n for n in sharded if w_loc[n].size >= _SMALL]
    small_sharded = [n for n in sharded if n not in big]
    small = replicated + small_sharded

    first = ["w_in"] + small_sharded
    srcs = [_as2d(w_loc[n]).astype(BF16) if n in big else _as2d(w_loc[n]) for n in first]
    (gathered,) = exchange([_Gather(srcs)], "gather_first")
    p = {n: _join(a, axis[n]) for n, a in zip(first, gathered)}
    for n in replicated:
        p[n] = _as2d(w_loc[n])
    shards = {n: _as2d(w_loc[n]).astype(BF16) for n in big if n not in first}

    last = "norm_mix_pre"
    early = [n for n in small if n != last]
    received, sibling, g_small, loss_part = _local_step(x, loss_target, p, shards, early)

    ((received["last"],),) = exchange([_Scatter([], [_pack([g_small[last]], BF16)])], "scatter_last")
    late = [n for n in received if n not in sibling]
    (swapped,) = exchange([_Swap([received[n] for n in late])], "swap_last")
    sibling.update(zip(late, swapped))
    early_sum = sum_pairs(received["small"], sibling["small"], "sum_small")
    last_sum = sum_pairs(received["last"], sibling["last"], "sum_last")
    g_tot = dict(zip(early, _unpack(early_sum, [g_small[n].shape for n in early])))
    (g_tot[last],) = _unpack(last_sum, [g_small[last].shape])
    chip = 2 * lax.axis_index("x") + lax.axis_index("y")
    for n in small_sharded:
        size = g_tot[n].shape[axis[n]] // N_CHIPS
        g_tot[n] = lax.dynamic_slice_in_dim(g_tot[n], chip * size, size, axis=axis[n])

    out_g, out_d, out_m, out_v = {}, {}, {}, {}
    for n in big:
        res = adamw(_as2d(w_loc[n]), _as2d(m_loc[n]), _as2d(v_loc[n]), received[n], sibling[n], "adamw_" + n)
        out_g[n], out_d[n], out_m[n], out_v[n] = (t.reshape(w_loc[n].shape) for t in res)
    res = adamw_small([_as2d(w_loc[n]) for n in small], [_as2d(m_loc[n]) for n in small],
                      [_as2d(v_loc[n]) for n in small], [g_tot[n] for n in small], "adamw_small")
    for n, (d, nm, nv) in zip(small, res):
        out_g[n], out_d[n], out_m[n], out_v[n] = (t.reshape(w_loc[n].shape) for t in (g_tot[n], d, nm, nv))

    d_model = x.shape[-1]
    loss = lax.psum(0.5 * jnp.sum(loss_part) / d_model, ("x", "y", "c"))
    grad_x = g_small["x"]
    return (loss, grad_x, *[out_g[n] for n in names], *[out_d[n] for n in names],
            *[out_m[n] for n in names], *[out_v[n] for n in names])


def _local_step(x, target, p, shards, small_early):
    axis = dict(_PARAMS)
    b, s, d = x.shape
    t = b * s
    rnn = p["b_rg_a"].shape[1]
    ffn = p["conv_ffn_b"].shape[1]
    nbk = rnn // p["w_rg_a"].shape[1]
    hkv = (p["w_in"].shape[1] - rnn - 2 * d) // (N_GROUPS + 2)
    h = hkv // HEAD_DIM
    nq = N_GROUPS * hkv

    x2 = x.reshape(t, d)
    tgt = target.reshape(t, d)
    w_in = p["w_in"]
    in_splits = (rnn, nq + 2 * hkv, 2 * d)
    wa = p["w_rg_a"].reshape(nbk, -1, p["w_rg_a"].shape[1]).astype(BF16)
    wx = p["w_rg_x"].reshape(nbk, -1, p["w_rg_x"].shape[1]).astype(BF16)
    cw_r, cb_r = p["conv_rnn_w"], p["conv_rnn_b"]
    cw_f, cb_f = p["conv_ffn_w"], p["conv_ffn_b"]

    masks, buckets = zip(*[_band(w_, r_) for w_, r_ in DILATED])
    bucket_f = jnp.asarray(np.where(np.stack(masks), np.stack(buckets), -1).astype(np.float32))
    rel_rows = jnp.pad(p["rel_bias"].T, ((0, 0), (0, 128 - REL_BUCKETS)))[:, None, :]
    biasm = bias_table(rel_rows, bucket_f, h, "bias_table")

    early = ["w_branch_rnn", "w_branch_att", "w_out"]
    hn1, (xr, qkv, gts), (got,) = norm_mm(x2, p["norm_mix_pre"], [w_in], [in_splits], "in_proj",
                                          ride=[_Gather([shards[n] for n in early])])
    p.update({n: _join(a, axis[n]) for n, a in zip(early, got)})
    xr3 = xr.reshape(b, s, rnn)
    (y_rnn, a_rnn, xc_rnn), (got,) = rglru_fwd(xr3, cw_r, cb_r, wa, p["b_rg_a"], wx, p["b_rg_x"], p["lru_lambda"], "rglru_fwd",
                              ride=[_Gather([shards[n] for n in ("w_ffn_gate", "w_ffn_up")])])
    p.update(zip(("w_ffn_gate", "w_ffn_up"), got))
    qkv3 = qkv.reshape(b, s, -1)
    o_att, lse, ((got,),) = attn_fwd(qkv3, biasm, h, "attn_fwd", ride=[_Gather([shards["w_ffn_down"]])])
    p["w_ffn_down"] = _join(got, axis["w_ffn_down"])
    merged, br, ba, mix, h1 = merge_out(y_rnn.reshape(t, rnn), o_att.reshape(t, hkv), gts, p["w_branch_rnn"],
                                        p["w_branch_att"], p["w_out"], p["norm_mix_post"], x2, "merge_out")
    hn2, gate_pre, up, act = ffn_in_act(h1, p["norm_ffn_pre"], p["w_ffn_gate"], p["w_ffn_up"], cw_f, cb_f, s, "ffn_in")

    g, gb = {}, {}
    recv, sib = {}, {}

    def rows4(a):
        return a.reshape(N_CHIPS, -1, a.shape[-1])

    dy, dff, dact, g["norm_ffn_post"], loss_part = ffn_down_loss(act, p["w_ffn_down"], p["norm_ffn_post"], h1, tgt,
                                                                  "ffn_down")
    gb["w_ffn_down"] = rows4(mm_tn(act, [dff], "ffn_down_dw"))
    (dgp, dup, dhn2, g["conv_ffn_w"], g["conv_ffn_b"]), ((recv["w_ffn_down"],),) = ffn_in_bwd(
        dact, gate_pre, up, cw_f, cb_f, p["w_ffn_gate"], p["w_ffn_up"], s, "ffn_in_bwd",
        ride=[_Scatter([gb["w_ffn_down"]])])
    gb["w_ffn_gate"] = mm_tn(hn2, [dgp], "ffn_gate_dw", col_shards=N_CHIPS)
    gb["w_ffn_up"] = mm_tn(hn2, [dup], "ffn_up_dw", col_shards=N_CHIPS)
    (dh1, dgts, dy_rnn, do_att, g["norm_ffn_pre"], g["norm_mix_post"], dw_out, dw_br,
     gb["w_branch_att"]) = mid_bwd(dhn2, h1, p["norm_ffn_pre"], dy, mix, p["norm_mix_post"], p["w_out"], gts, br, ba,
                                   p["w_branch_rnn"], p["w_branch_att"], merged, y_rnn.reshape(t, rnn),
                                   o_att.reshape(t, hkv), "mid_bwd")
    gb["w_out"], gb["w_branch_rnn"] = rows4(dw_out), rows4(dw_br)
    ffn_in = ["w_ffn_gate", "w_ffn_up"]
    (dxr, g["conv_rnn_w"], g["conv_rnn_b"], dwa, g["b_rg_a"], dwx, g["b_rg_x"], g["lru_lambda"]), (got,) = rglru_bwd(
        xr3, y_rnn, dy_rnn.reshape(b, s, rnn), a_rnn, xc_rnn, cw_r, wa, p["b_rg_a"], wx, p["b_rg_x"], p["lru_lambda"], "rglru_bwd",
        ride=[_Scatter([gb[n] for n in ffn_in])])
    recv.update(zip(ffn_in, got))
    g["w_rg_a"] = dwa.reshape(p["w_rg_a"].shape)
    g["w_rg_x"] = dwx.reshape(p["w_rg_x"].shape)
    mid = ["w_out", "w_branch_rnn", "w_branch_att"]
    early_recv = ["w_ffn_down"] + ffn_in
    (dq1, dq2, dq3, dk, dv, ds_sum), (got, swapped) = attn_bwd(
        qkv3, biasm, o_att, lse, do_att.reshape(b, s, hkv), h, "attn_bwd",
        ride=[_Scatter([gb[n] for n in mid]), _Swap([recv[n] for n in early_recv])])
    recv.update(zip(mid, got))
    sib.update(zip(early_recv, swapped))
    rows = bias_grad(ds_sum, bucket_f, "bias_grad")
    g["rel_bias"] = rows[:, 0, :REL_BUCKETS].T
    dproj = [dxr.reshape(t, rnn)] + [a.reshape(t, hkv) for a in (dq1, dq2, dq3, dk, dv)] + [dgts]
    dw_a, (got,) = mm_tn(hn1, dproj[:4], "in_proj_dw_a", ride=[_Swap([recv[n] for n in mid])])
    sib.update(zip(mid, got))
    pack = _pack([g[n] for n in small_early], BF16)
    dw_b, ((recv["small"],),) = mm_tn(hn1, dproj[4:], "in_proj_dw_b", ride=[_Scatter([], [pack])])
    gb["w_in"] = _cut(jnp.concatenate([dw_a[0], dw_b[0]], axis=1), 1)
    dx, g["norm_mix_pre"], ((recv["w_in"],),) = mm_nt(
        [(dproj, w_in)], "in_proj_dx", norm=(x2, p["norm_mix_pre"], dh1), ride=[_Scatter([gb["w_in"]])])
    g["x"] = dx.reshape(b, s, d)
    return recv, sib, g, loss_part
```

```python
import math

import numpy as np
import jax
import jax.numpy as jnp
from jax import lax
from jax.experimental import pallas as pl
from jax.experimental.pallas import tpu as pltpu

F32 = jnp.float32
BF16 = jnp.bfloat16

EPS = 1e-6
HEAD_DIM = 128
ATTN_BLOCK = 128
DILATED = ((128, 1), (512, 4), (2048, 16))
N_GROUPS = len(DILATED)
REL_BUCKETS = 32
REL_MAX_DIST = 2048
LRU_C = 8.0
NEG = -1e30

ADAM_LR = 0.001
ADAM_B1 = 0.9
ADAM_B2 = 0.999
ADAM_EPS = 1e-08
ADAM_WD = 0.01
ADAM_STEP = 10

N_CHIPS = 4
PACK_W = 1024
PACK_ROWS = 16
VMEM_LIMIT = 56 * 1024 * 1024
MESH = pl.DeviceIdType.MESH


def _params(sem=None):
    return pltpu.CompilerParams(dimension_semantics=sem, vmem_limit_bytes=VMEM_LIMIT)


def _dot(a, b):
    return jnp.dot(a, b, preferred_element_type=F32)


def _dot_nt(a, b):
    return lax.dot_general(a, b, (((1,), (1,)), ((), ())), preferred_element_type=F32)


def _dot_tn(a, b):
    return lax.dot_general(a, b, (((0,), (0,)), ((), ())), preferred_element_type=F32)


def _sig(x):
    return 0.5 * jnp.tanh(0.5 * x) + 0.5


def _rows(tm, w):
    return pl.BlockSpec((tm, w), lambda i: (i, 0))


def _whole(shape):
    nd = len(shape)
    return pl.BlockSpec(tuple(shape), lambda *_: (0,) * nd)


def _resident(shape):
    nd = len(shape)
    return pl.BlockSpec(tuple(shape), lambda *_: (0,) * nd, pipeline_mode=pl.Buffered(1))


def _tile(t, want):
    while t % want:
        want //= 2
    return want


def norm_mm(x, g, ws, splits, name, ride=(), tm=512):
    t, d = x.shape
    tm = _tile(t, tm)
    nw = len(ws)
    widths = [n for sp in splits for n in sp]

    def body(x_ref, g_ref, *refs):
        w_refs, hn_ref, o_refs = refs[:nw], refs[nw], refs[nw + 1:]
        xv = x_ref[...]
        inv = lax.rsqrt(jnp.mean(xv * xv, axis=-1, keepdims=True) + EPS)
        hn = (xv * inv * g_ref[...]).astype(BF16)
        hn_ref[...] = hn
        o = 0
        for w_ref, sp in zip(w_refs, splits):
            off = 0
            for n in sp:
                o_refs[o][...] = _dot(hn, w_ref[:, off:off + n])
                off += n
                o += 1

    r_ins, r_in_specs, r_outs, r_out_specs, r_sems = _ride_args(ride)
    n_out = 1 + len(widths)
    outs = pl.pallas_call(
        _riding(body, 2 + nw, n_out, 0, ride, 1), name=name, grid=(t // tm,),
        in_specs=[_rows(tm, d), _whole(g.shape)] + [_resident(w.shape) for w in ws] + r_in_specs,
        out_specs=[_rows(tm, d)] + [_rows(tm, n) for n in widths] + r_out_specs,
        out_shape=[jax.ShapeDtypeStruct((t, d), BF16)] + [jax.ShapeDtypeStruct((t, n), F32) for n in widths] + r_outs,
        scratch_shapes=r_sems,
        compiler_params=_params(("arbitrary",)),
    )(x, g, *ws, *r_ins)
    return outs[0], outs[1:n_out], _ride_results(ride, outs[n_out:])


def mm_nt(groups, name, ride=(), norm=None, tm=512):
    dys_all = [dy for dys, _ in groups for dy in dys]
    ws = [w for _, w in groups]
    t = dys_all[0].shape[0]
    k = ws[0].shape[0]
    tm = _tile(t, tm)
    n = len(dys_all)
    extra = list(norm) if norm else []

    def body(*refs):
        dy_refs, w_refs = refs[:n], refs[n:n + len(ws)]
        rest = refs[n + len(ws):]
        acc = None
        i = 0
        for (dys, _), w_ref in zip(groups, w_refs):
            off = 0
            for dy in dys:
                width = dy.shape[1]
                part = _dot_nt(dy_refs[i][...].astype(BF16), w_ref[:, off:off + width])
                acc = part if acc is None else acc + part
                off += width
                i += 1
        if norm:
            u_ref, g_ref, add_ref, o_ref, dg_ref = rest

            @pl.when(pl.program_id(0) == 0)
            def _():
                dg_ref[...] = jnp.zeros(dg_ref.shape, F32)

            du, dg_rows = _rms_bwd(acc, u_ref[...], g_ref[...])
            o_ref[...] = du + add_ref[...]
            dg_ref[...] += jnp.sum(dg_rows, axis=0, keepdims=True)
        else:
            rest[0][...] = acc

    n_out = 2 if norm else 1
    r_ins, r_in_specs, r_outs, r_out_specs, r_sems = _ride_args(ride)
    outs = pl.pallas_call(
        _riding(body, n + len(ws) + len(extra), n_out, 0, ride, 1), name=name, grid=(t // tm,),
        in_specs=[_rows(tm, dy.shape[1]) for dy in dys_all] + [_resident(w.shape) for w in ws]
        + ([_rows(tm, k), _whole((1, k)), _rows(tm, k)] if norm else []) + r_in_specs,
        out_specs=[_rows(tm, k)] + ([_whole((1, k))] if norm else []) + r_out_specs,
        out_shape=[jax.ShapeDtypeStruct((t, k), F32)] + ([jax.ShapeDtypeStruct((1, k), F32)] if norm else []) + r_outs,
        scratch_shapes=r_sems,
        compiler_params=_params(("arbitrary",)),
    )(*dys_all, *ws, *extra, *r_ins)
    return tuple(outs[:n_out]) + (_ride_results(ride, outs[n_out:]),)


def mm_tn(a, dys, name, col_shards=1, ride=(), tm=1024):
    t, k = a.shape
    tm = _tile(t, tm)
    n = len(dys)
    pieces = [dy if isinstance(dy, tuple) else (dy, dy.shape[1], 0) for dy in dys]
    ntot = sum(width for _, width, _ in pieces)
    wsh = ntot // col_shards

    def body(a_ref, *refs):
        dy_refs, o_ref, acc = refs[:n], refs[n], refs[n + 1]

        @pl.when(pl.program_id(0) == 0)
        def _():
            acc[...] = jnp.zeros(acc.shape, F32)

        av = a_ref[...].astype(BF16)
        off = 0
        for dy_ref in dy_refs:
            width = dy_ref.shape[1]
            acc[:, off:off + width] += _dot_tn(av, dy_ref[...].astype(BF16))
            off += width

        @pl.when(pl.program_id(0) == pl.num_programs(0) - 1)
        def _():
            for j in range(col_shards):
                o_ref[j] = acc[:, j * wsh:(j + 1) * wsh].astype(o_ref.dtype)

    r_ins, r_in_specs, r_outs, r_out_specs, r_sems = _ride_args(ride)
    outs = pl.pallas_call(
        _riding(body, 1 + n, 1, 1, ride, 1), name=name, grid=(t // tm,),
        in_specs=[_rows(tm, k)] + [pl.BlockSpec((tm, width), lambda i, j=j: (i, j)) for _, width, j in pieces]
        + r_in_specs,
        out_specs=[_whole((col_shards, k, wsh))] + r_out_specs,
        out_shape=[jax.ShapeDtypeStruct((col_shards, k, wsh), BF16)] + r_outs,
        scratch_shapes=[pltpu.VMEM((k, ntot), F32)] + r_sems,
        compiler_params=_params(("arbitrary",)),
    )(a, *[arr for arr, _, _ in pieces], *r_ins)
    return (outs[0], _ride_results(ride, outs[1:])) if ride else outs[0]


def _rms_bwd(dz, u, g):
    d = u.shape[-1]
    inv = lax.rsqrt(jnp.mean(u * u, axis=-1, keepdims=True) + EPS)
    dzg = dz * g
    proj = jnp.sum(dzg * u, axis=-1, keepdims=True) * (1.0 / d)
    du = inv * (dzg - u * (inv * inv) * proj)
    dg_rows = dz * u * inv
    return du, dg_rows


def ffn_down_loss(act, wd, g, h1, target, name, tm=512):
    t, f = act.shape
    d = wd.shape[1]
    tm = _tile(t, tm)

    def body(a_ref, w_ref, g_ref, h_ref, t_ref, dy_ref, dff_ref, dact_ref, dg_ref, loss_ref):
        @pl.when(pl.program_id(0) == 0)
        def _():
            dg_ref[...] = jnp.zeros(dg_ref.shape, F32)
            loss_ref[...] = jnp.zeros(loss_ref.shape, F32)

        wv = w_ref[...]
        gv = g_ref[...]
        ff = _dot(a_ref[...], wv)
        inv = lax.rsqrt(jnp.mean(ff * ff, axis=-1, keepdims=True) + EPS)
        err = h_ref[...] + ff * inv * gv - t_ref[...]
        loss_ref[...] += jnp.sum(err * err, axis=0, keepdims=True)
        dy = err * (1.0 / d)
        dy_ref[...] = dy
        du, dg_rows = _rms_bwd(dy, ff, gv)
        dff = du.astype(BF16)
        dff_ref[...] = dff
        dg_ref[...] += jnp.sum(dg_rows, axis=0, keepdims=True)
        dact_ref[...] = _dot_nt(dff, wv)

    return pl.pallas_call(
        body, name=name, grid=(t // tm,),
        in_specs=[_rows(tm, f), _resident(wd.shape), _whole(g.shape), _rows(tm, d), _rows(tm, d)],
        out_specs=[_rows(tm, d), _rows(tm, d), _rows(tm, f), _whole((1, d)), _whole((1, d))],
        out_shape=[jax.ShapeDtypeStruct((t, d), F32), jax.ShapeDtypeStruct((t, d), BF16),
                   jax.ShapeDtypeStruct((t, f), F32), jax.ShapeDtypeStruct((1, d), F32),
                   jax.ShapeDtypeStruct((1, d), F32)],
        compiler_params=_params(("arbitrary",)),
    )(act, wd, g, h1, target)


def merge_out(y_rnn, o_att, gts, w_br, w_ba, w_out, g, x, name, tm=512):
    t = y_rnn.shape[0]
    d = w_br.shape[1]
    tm = _tile(t, tm)

    def body(y_ref, o_ref, g_ref, wbr_ref, wba_ref, wo_ref, gn_ref, x_ref, m_ref, br_ref, ba_ref, mix_ref, h_ref):
        br = _dot(y_ref[...].astype(BF16), wbr_ref[...])
        ba = _dot(o_ref[...].astype(BF16), wba_ref[...])
        gv = g_ref[...]
        merged = (_sig(gv[:, :d]) * br + _sig(gv[:, d:]) * ba).astype(BF16)
        m_ref[...] = merged
        br_ref[...] = br
        ba_ref[...] = ba
        mix = _dot(merged, wo_ref[...])
        mix_ref[...] = mix
        inv = lax.rsqrt(jnp.mean(mix * mix, axis=-1, keepdims=True) + EPS)
        h_ref[...] = x_ref[...] + mix * inv * gn_ref[...]

    sd = jax.ShapeDtypeStruct
    return pl.pallas_call(
        body, name=name, grid=(t // tm,),
        in_specs=[_rows(tm, y_rnn.shape[1]), _rows(tm, o_att.shape[1]), _rows(tm, 2 * d),
                  _resident(w_br.shape), _resident(w_ba.shape), _resident(w_out.shape), _whole(g.shape), _rows(tm, d)],
        out_specs=[_rows(tm, d)] * 5,
        out_shape=[sd((t, d), BF16), sd((t, d), F32), sd((t, d), F32), sd((t, d), F32), sd((t, d), F32)],
        compiler_params=_params(("parallel",)),
    )(y_rnn, o_att, gts, w_br, w_ba, w_out, g, x)


def mid_bwd(dhn2, h1, g_ffn, dy, mix, g_mix, w_out, gts, br, ba, w_br, w_ba, merged, y_rnn, o_att, name, tm=256):
    t, d = h1.shape
    tm = _tile(t, tm)
    rnn, hkv = w_br.shape[0], w_ba.shape[0]
    wsh = d // N_CHIPS

    def body(dhn_ref, h_ref, gf_ref, dy_ref, mix_ref, gm_ref, wo_ref, g_ref, br_ref, ba_ref, wbr_ref, wba_ref,
             m_ref, y_ref, o_ref, dh_ref, dg_ref, dyr_ref, doa_ref, dgf_ref, dgm_ref, dwo_ref, dwbr_ref, dwba_ref,
             acc_o, acc_br, acc_ba):
        @pl.when(pl.program_id(0) == 0)
        def _():
            dgf_ref[...] = jnp.zeros(dgf_ref.shape, F32)
            dgm_ref[...] = jnp.zeros(dgm_ref.shape, F32)
            acc_o[...] = jnp.zeros(acc_o.shape, F32)
            acc_br[...] = jnp.zeros(acc_br.shape, F32)
            acc_ba[...] = jnp.zeros(acc_ba.shape, F32)

        du, rows_f = _rms_bwd(dhn_ref[...], h_ref[...], gf_ref[...])
        dh1 = du + dy_ref[...]
        dh_ref[...] = dh1
        dgf_ref[...] += jnp.sum(rows_f, axis=0, keepdims=True)
        dmx, rows_m = _rms_bwd(dh1, mix_ref[...], gm_ref[...])
        dmix = dmx.astype(BF16)
        acc_o[...] += _dot_tn(m_ref[...], dmix)
        dgm_ref[...] += jnp.sum(rows_m, axis=0, keepdims=True)
        dm = _dot_nt(dmix, wo_ref[...])
        gv = g_ref[...]
        sr = _sig(gv[:, :d])
        sa = _sig(gv[:, d:])
        dbr = (dm * sr).astype(BF16)
        dba = (dm * sa).astype(BF16)
        acc_br[...] += _dot_tn(y_ref[...].astype(BF16), dbr)
        acc_ba[...] += _dot_tn(o_ref[...].astype(BF16), dba)
        dg_ref[:, :d] = (dm * br_ref[...] * sr * (1.0 - sr)).astype(BF16)
        dg_ref[:, d:] = (dm * ba_ref[...] * sa * (1.0 - sa)).astype(BF16)
        dyr_ref[...] = _dot_nt(dbr, wbr_ref[...])
        doa_ref[...] = _dot_nt(dba, wba_ref[...])

        @pl.when(pl.program_id(0) == pl.num_programs(0) - 1)
        def _():
            dwo_ref[...] = acc_o[...].astype(BF16)
            dwbr_ref[...] = acc_br[...].astype(BF16)
            for j in range(N_CHIPS):
                dwba_ref[j] = acc_ba[:, j * wsh:(j + 1) * wsh].astype(BF16)

    sd = jax.ShapeDtypeStruct
    row, vec = _rows(tm, d), _whole((1, d))
    once = pl.Buffered(1)

    def resident(shape):
        return pl.BlockSpec(shape, lambda i: (0,) * len(shape), pipeline_mode=once)

    return pl.pallas_call(
        body, name=name, grid=(t // tm,),
        in_specs=[row, row, vec, row, row, vec, resident(w_out.shape), _rows(tm, 2 * d), row, row,
                  resident(w_br.shape), resident(w_ba.shape), row, _rows(tm, rnn), _rows(tm, hkv)],
        out_specs=[row, _rows(tm, 2 * d), _rows(tm, rnn), _rows(tm, hkv), vec, vec,
                   resident((d, d)), resident((rnn, d)), resident((N_CHIPS, hkv, wsh))],
        out_shape=[sd((t, d), F32), sd((t, 2 * d), BF16), sd((t, rnn), F32), sd((t, hkv), F32), sd((1, d), F32),
                   sd((1, d), F32), sd((d, d), BF16), sd((rnn, d), BF16), sd((N_CHIPS, hkv, wsh), BF16)],
        scratch_shapes=[pltpu.VMEM((d, d), F32), pltpu.VMEM((rnn, d), F32), pltpu.VMEM((hkv, d), F32)],
        compiler_params=_params(("arbitrary",)),
    )(dhn2, h1, g_ffn, dy, mix, g_mix, w_out, gts, br, ba, w_br, w_ba, merged, y_rnn, o_att)


def _shift_dn(x, d, fill, row):
    return jnp.where(row >= d, pltpu.roll(x, d, 0), fill)


def _shift_up(x, d, fill, row):
    s = x.shape[0]
    return jnp.where(row < s - d, pltpu.roll(x, s - d, 0), fill)


def _conv_fwd(x, w, b, row):
    kk = w.shape[0]
    y = b + w[kk - 1:kk, :] * x
    for j in range(1, kk):
        y = y + w[kk - 1 - j:kk - j, :] * _shift_dn(x, j, 0.0, row)
    return y


def _conv_bwd(dy, x, w, row):
    kk = w.shape[0]
    dx = w[kk - 1:kk, :] * dy
    dws = [None] * kk
    dws[kk - 1] = jnp.sum(dy * x, axis=0, keepdims=True)
    for j in range(1, kk):
        ahead = _shift_up(dy, j, 0.0, row)
        dx = dx + w[kk - 1 - j:kk - j, :] * ahead
        dws[kk - 1 - j] = jnp.sum(ahead * x, axis=0, keepdims=True)
    return dx, jnp.concatenate(dws, axis=0)


def _softplus(z):
    y = jnp.exp(-jnp.abs(z))
    u = 1.0 + y
    dd = u - 1.0
    log1p = jnp.where(dd == 0.0, y, jnp.log(u) * (y / jnp.where(dd == 0.0, 1.0, dd)))
    return jnp.maximum(z, 0.0) + log1p


def _lru_decay(xb, wa, ba, lam):
    r = _sig(_dot(xb, wa) + ba)
    sp = _softplus(-lam)
    la = (-LRU_C) * r * sp
    return r, sp, la, jnp.exp(la)


def _lru_gates(xc, wa, ba, wx, bx, lam):
    xb = xc.astype(BF16)
    r, sp, la, a = _lru_decay(xb, wa, ba, lam)
    i = _sig(_dot(xb, wx) + bx)
    one_m_a2 = jnp.tanh(-la) * (1.0 + a * a)
    inv_mult = lax.rsqrt(one_m_a2)
    return r, i, sp, a, one_m_a2 * inv_mult, inv_mult


def _seg_len(s):
    seg = -(-s // 8)
    return seg + (4 - seg % 8) % 8


def _scan_rows(a_pad, u_pad, out_pad, reverse):
    planes, rows8, lanes = a_pad.shape
    seg = rows8 // 8
    sub = lax.broadcasted_iota(jnp.int32, (planes, 8, lanes), 1)

    unroll = 4

    def rows(k, d):
        i = k * unroll + d
        return pl.ds((seg - 1 - i) if reverse else i, 8, stride=seg)

    def ends(k, carry):
        h, p = carry
        for d in range(unroll):
            a = a_pad[:, rows(k, d), :]
            h = a * h + u_pad[:, rows(k, d), :]
            p = a * p
        return h, p

    init = (jnp.zeros((planes, 8, lanes), F32), jnp.ones((planes, 8, lanes), F32))
    h_end, p_end = lax.fori_loop(0, seg // unroll, ends, init)
    start = jnp.zeros((planes, 8, lanes), F32)
    for _ in range(7):
        nxt = h_end + p_end * start
        if reverse:
            start = jnp.where(sub < 7, pltpu.roll(nxt, 7, 1), 0.0)
        else:
            start = jnp.where(sub >= 1, pltpu.roll(nxt, 1, 1), 0.0)

    def redo(k, h):
        for d in range(unroll):
            h = a_pad[:, rows(k, d), :] * h + u_pad[:, rows(k, d), :]
            out_pad[:, rows(k, d), :] = h
        return h

    lax.fori_loop(0, seg // unroll, redo, start)


def _lru_cols(c, rb):
    return 2 * rb if c % (2 * rb) == 0 else rb


def rglru_fwd(xr, cw, cb, wa, ba, wx, bx, lam, name, ride=()):
    b, s, c = xr.shape
    rb = wa.shape[1]
    kk = cw.shape[0]
    cols = _lru_cols(c, rb)
    nj = cols // rb
    seg = _seg_len(s)

    def body(x_ref, cw_ref, cb_ref, wa_ref, ba_ref, wx_ref, bx_ref, lam_ref, h_ref, a_ref, xc_ref, a_pad, u_pad, h_pad):
        row = lax.broadcasted_iota(jnp.int32, (s, rb), 0)
        for j in range(nj):
            cs = slice(j * rb, (j + 1) * rb)
            xc = _conv_fwd(x_ref[:, cs], cw_ref[:, cs], cb_ref[:, cs], row)
            _, i, _, a, mult, _ = _lru_gates(xc, wa_ref[j], ba_ref[:, cs], wx_ref[j], bx_ref[:, cs], lam_ref[:, cs])
            xc_ref[:, cs] = xc
            a_ref[:, cs] = a
            a_pad[j, 0:s, :] = a
            u_pad[j, 0:s, :] = mult * (i * xc)
        a_pad[:, s:, :] = jnp.ones((nj, 8 * seg - s, rb), F32)
        u_pad[:, s:, :] = jnp.zeros((nj, 8 * seg - s, rb), F32)
        _scan_rows(a_pad, u_pad, h_pad, False)
        for j in range(nj):
            h_ref[:, j * rb:(j + 1) * rb] = h_pad[j, 0:s, :]

    vec = pl.BlockSpec((1, cols), lambda bi, n: (0, n))
    seq = pl.BlockSpec((None, s, cols), lambda bi, n: (bi, 0, n))
    mat = pl.BlockSpec((nj, rb, rb), lambda bi, n: (n, 0, 0))
    r_ins, r_in_specs, r_outs, r_out_specs, r_sems = _ride_args(ride)
    outs = pl.pallas_call(
        _riding(body, 8, 3, 3, ride, 2), name=name, grid=(b, c // cols),
        in_specs=[seq, pl.BlockSpec((kk, cols), lambda bi, n: (0, n)), vec, mat, vec, mat, vec, vec] + r_in_specs,
        out_specs=[seq] * 3 + r_out_specs,
        out_shape=[jax.ShapeDtypeStruct((b, s, c), F32)] * 3 + r_outs,
        scratch_shapes=[pltpu.VMEM((nj, 8 * seg, rb), F32)] * 3 + r_sems,
        compiler_params=_params(("arbitrary", "arbitrary")),
    )(xr, cw, cb, wa, ba, wx, bx, lam, *r_ins)
    return outs[:3], _ride_results(ride, outs[3:])


def rglru_bwd(xr, h, dh, a_fwd, xc_fwd, cw, wa, ba, wx, bx, lam, name, ride=()):
    b, s, c = xr.shape
    nb, rb = wa.shape[0], wa.shape[1]
    kk = cw.shape[0]
    cols = _lru_cols(c, rb)
    nj = cols // rb
    seg = _seg_len(s)

    def body(x_ref, h_ref, dh_ref, a_ref, xc_ref, cw_ref, wa_ref, ba_ref, wx_ref, bx_ref, lam_ref,
             dx_ref, dcw_ref, dcb_ref, dwa_ref, dba_ref, dwx_ref, dbx_ref, dlam_ref, b_pad, g_pad, l_pad):
        @pl.when(pl.program_id(1) == 0)
        def _():
            for ref in (dcw_ref, dcb_ref, dwa_ref, dba_ref, dwx_ref, dbx_ref, dlam_ref):
                ref[...] = jnp.zeros(ref.shape, F32)

        row = lax.broadcasted_iota(jnp.int32, (s, rb), 0)

        for j in range(nj):
            b_pad[j, 0:s, :] = _shift_up(a_ref[:, j * rb:(j + 1) * rb], 1, 0.0, row)
            g_pad[j, 0:s, :] = dh_ref[:, j * rb:(j + 1) * rb]
        b_pad[:, s:, :] = jnp.zeros((nj, 8 * seg - s, rb), F32)
        g_pad[:, s:, :] = jnp.zeros((nj, 8 * seg - s, rb), F32)
        _scan_rows(b_pad, g_pad, l_pad, True)

        for j in range(nj):
            cs = slice(j * rb, (j + 1) * rb)
            x = x_ref[:, cs]
            cwv = cw_ref[:, cs]
            wav, wxv, lamv = wa_ref[j], wx_ref[j], lam_ref[:, cs]
            xc = xc_ref[:, cs]
            r, i, sp, a, mult, inv_mult = _lru_gates(xc, wav, ba_ref[:, cs], wxv, bx_ref[:, cs], lamv)
            lmb = l_pad[j, 0:s, :]
            h_prev = _shift_dn(h_ref[:, cs], 1, 0.0, row)
            da = lmb * h_prev
            ixc = i * xc
            dla = da * a - (lmb * ixc) * (a * a) * inv_mult
            di = lmb * mult * xc
            dxc = lmb * mult * i
            dr = dla * ((-LRU_C) * sp)
            dsp = jnp.sum(dla * ((-LRU_C) * r), axis=0, keepdims=True)
            dga = dr * r * (1.0 - r)
            dgx = di * i * (1.0 - i)
            dga_b, dgx_b = dga.astype(BF16), dgx.astype(BF16)
            xb = xc.astype(BF16)
            dwa_ref[j] += _dot_tn(xb, dga_b)
            dwx_ref[j] += _dot_tn(xb, dgx_b)
            dba_ref[:, cs] += jnp.sum(dga, axis=0, keepdims=True)
            dbx_ref[:, cs] += jnp.sum(dgx, axis=0, keepdims=True)
            dlam_ref[:, cs] += dsp * (-_sig(-lamv))
            dxc = dxc + _dot_nt(dga_b, wav) + _dot_nt(dgx_b, wxv)
            dcb_ref[:, cs] += jnp.sum(dxc, axis=0, keepdims=True)
            dx, dcw = _conv_bwd(dxc, x, cwv, row)
            dcw_ref[:, cs] += dcw
            dx_ref[:, cs] = dx.astype(dx_ref.dtype)

    vec = pl.BlockSpec((1, cols), lambda n, bi: (0, n))
    seq = pl.BlockSpec((None, s, cols), lambda n, bi: (bi, 0, n))
    mat = pl.BlockSpec((nj, rb, rb), lambda n, bi: (n, 0, 0))
    cws = pl.BlockSpec((kk, cols), lambda n, bi: (0, n))
    sd = jax.ShapeDtypeStruct
    r_ins, r_in_specs, r_outs, r_out_specs, r_sems = _ride_args(ride)
    outs = pl.pallas_call(
        _riding(body, 11, 8, 3, ride, 2), name=name, grid=(c // cols, b),
        in_specs=[seq, seq, seq, seq, seq, cws, mat, vec, mat, vec, vec] + r_in_specs,
        out_specs=[seq, cws, vec, mat, vec, mat, vec, vec] + r_out_specs,
        out_shape=[sd((b, s, c), BF16), sd((kk, c), F32), sd((1, c), F32), sd((nb, rb, rb), F32),
                   sd((1, c), F32), sd((nb, rb, rb), F32), sd((1, c), F32), sd((1, c), F32)] + r_outs,
        scratch_shapes=[pltpu.VMEM((nj, 8 * seg, rb), F32)] * 3 + r_sems,
        compiler_params=_params(("arbitrary", "arbitrary")),
    )(xr, h, dh, a_fwd, xc_fwd, cw, wa, ba, wx, bx, lam, *r_ins)
    return outs[:8], _ride_results(ride, outs[8:])


_GELU_C = math.sqrt(2.0 / math.pi)


def _gelu_parts(x):
    th = jnp.tanh(_GELU_C * (x + 0.044715 * x * x * x))
    gel = 0.5 * x * (1.0 + th)
    dgel = 0.5 * (1.0 + th) + 0.5 * x * (1.0 - th * th) * _GELU_C * (1.0 + 3 * 0.044715 * x * x)
    return gel, dgel


def ffn_in_act(x, g, wg, wu, cw, cb, seq_len, name, tm=256):
    t, d = x.shape
    f = N_CHIPS * wg.shape[2]
    kk = cw.shape[0]
    tm = _tile(seq_len, tm)
    tiles_per_seq = seq_len // tm
    keep = 8
    assert kk - 1 <= keep

    def body(x_ref, g_ref, wg_ref, wu_ref, cw_ref, cb_ref, hn_ref, gp_ref, up_ref, act_ref, tail):
        @pl.when(pl.program_id(0) % tiles_per_seq == 0)
        def _():
            tail[...] = jnp.zeros(tail.shape, F32)

        xv = x_ref[...]
        inv = lax.rsqrt(jnp.mean(xv * xv, axis=-1, keepdims=True) + EPS)
        hn = (xv * inv * g_ref[...]).astype(BF16)
        hn_ref[...] = hn
        gp = jnp.concatenate([_dot(hn, wg_ref[j]) for j in range(N_CHIPS)], axis=1)
        up = jnp.concatenate([_dot(hn, wu_ref[j]) for j in range(N_CHIPS)], axis=1)
        gp_ref[...] = gp
        up_ref[...] = up
        cwv = cw_ref[...]
        row = lax.broadcasted_iota(jnp.int32, (tm, 1), 0)
        gate = _conv_fwd(gp, cwv, cb_ref[...], row)
        row8 = lax.broadcasted_iota(jnp.int32, (keep, 1), 0)
        prev = tail[...]
        fix = jnp.zeros((keep, f), F32)
        for j in range(1, kk):
            fix = fix + cwv[kk - 1 - j:kk - j, :] * jnp.where(row8 < j, pltpu.roll(prev, j, 0), 0.0)
        gate = jnp.concatenate([gate[:keep] + fix, gate[keep:]], axis=0)
        tail[...] = gp[tm - keep:, :]
        gel, _ = _gelu_parts(gate)
        act_ref[...] = (gel * up).astype(BF16)

    sd = jax.ShapeDtypeStruct
    return pl.pallas_call(
        body, name=name, grid=(t // tm,),
        in_specs=[_rows(tm, d), _whole(g.shape), _whole(wg.shape), _whole(wu.shape), _whole(cw.shape), _whole(cb.shape)],
        out_specs=[_rows(tm, d), _rows(tm, f), _rows(tm, f), _rows(tm, f)],
        out_shape=[sd((t, d), BF16), sd((t, f), F32), sd((t, f), F32), sd((t, f), BF16)],
        scratch_shapes=[pltpu.VMEM((keep, f), F32)],
        compiler_params=_params(("arbitrary",)),
    )(x, g, wg, wu, cw, cb)


def ffn_in_bwd(dact, gate_pre, up, cw, cb, wg, wu, seq_len, name, ride=(), tm=256):
    t, f = gate_pre.shape
    d = wg.shape[1]
    fs = f // N_CHIPS
    kk = cw.shape[0]
    tm = _tile(seq_len, tm)
    nt = t // tm
    tiles_per_seq = seq_len // tm
    keep = 8
    assert kk - 1 <= keep

    def body(da_ref, g_ref, halo_ref, u_ref, cw_ref, cb_ref, wg_ref, wu_ref,
             dg_ref, du_ref, dhn_ref, dcw_ref, dcb_ref, nxt):
        tile = (nt - 1 - pl.program_id(0)) % tiles_per_seq

        @pl.when(pl.program_id(0) == 0)
        def _():
            dcw_ref[...] = jnp.zeros(dcw_ref.shape, F32)
            dcb_ref[...] = jnp.zeros(dcb_ref.shape, F32)

        @pl.when(tile == tiles_per_seq - 1)
        def _():
            nxt[...] = jnp.zeros(nxt.shape, F32)

        row = lax.broadcasted_iota(jnp.int32, (tm, 1), 0)
        row8 = lax.broadcasted_iota(jnp.int32, (keep, 1), 0)
        gp = g_ref[...]
        cwv = cw_ref[...]
        prev = jnp.where(tile > 0, halo_ref[...], 0.0)
        gate = _conv_fwd(gp, cwv, cb_ref[...], row)
        fix = jnp.zeros((keep, f), F32)
        for j in range(1, kk):
            fix = fix + cwv[kk - 1 - j:kk - j, :] * jnp.where(row8 < j, pltpu.roll(prev, j, 0), 0.0)
        gate = jnp.concatenate([gate[:keep] + fix, gate[keep:]], axis=0)
        gel, dgel = _gelu_parts(gate)
        da = da_ref[...]
        dup = (da * gel).astype(BF16)
        du_ref[...] = dup
        dgate = da * u_ref[...] * dgel
        dcb_ref[...] += jnp.sum(dgate, axis=0, keepdims=True)
        after = nxt[...]
        dgp = cwv[kk - 1:kk, :] * dgate
        tail_fix = jnp.zeros((keep, f), F32)
        dws = [None] * kk
        dws[kk - 1] = jnp.sum(dgate * gp, axis=0, keepdims=True)
        for j in range(1, kk):
            wj = cwv[kk - 1 - j:kk - j, :]
            dgp = dgp + wj * _shift_up(dgate, j, 0.0, row)
            tail_fix = tail_fix + wj * jnp.where(row8 >= keep - j, pltpu.roll(after, keep - j, 0), 0.0)
            dws[kk - 1 - j] = (jnp.sum(dgate * _shift_dn(gp, j, 0.0, row), axis=0, keepdims=True)
                               + jnp.sum(dgate[:keep] * jnp.where(row8 < j, pltpu.roll(prev, j, 0), 0.0),
                                         axis=0, keepdims=True))
        dgp = jnp.concatenate([dgp[:tm - keep], dgp[tm - keep:] + tail_fix], axis=0).astype(BF16)
        nxt[...] = dgate[:keep]
        dcw_ref[...] += jnp.concatenate(dws, axis=0)
        dg_ref[...] = dgp
        dhn = None
        for j in range(N_CHIPS):
            cs = slice(j * fs, (j + 1) * fs)
            part = _dot_nt(dgp[:, cs], wg_ref[j]) + _dot_nt(dup[:, cs], wu_ref[j])
            dhn = part if dhn is None else dhn + part
        dhn_ref[...] = dhn

    def rev(i):
        return nt - 1 - i

    rows_f = pl.BlockSpec((tm, f), lambda i: (rev(i), 0))
    halo = pl.BlockSpec((None, keep, f), lambda i: (jnp.maximum(rev(i) * (tm // keep) - 1, 0), 0, 0))
    once = pl.Buffered(1)
    sd = jax.ShapeDtypeStruct
    r_ins, r_in_specs, r_outs, r_out_specs, r_sems = _ride_args(ride)
    outs = pl.pallas_call(
        _riding(body, 8, 5, 1, ride, 1), name=name, grid=(nt,),
        in_specs=[rows_f, rows_f, halo, rows_f, _whole(cw.shape), _whole(cb.shape),
                  pl.BlockSpec(wg.shape, lambda i: (0, 0, 0), pipeline_mode=once),
                  pl.BlockSpec(wu.shape, lambda i: (0, 0, 0), pipeline_mode=once)] + r_in_specs,
        out_specs=[rows_f, rows_f, pl.BlockSpec((tm, d), lambda i: (rev(i), 0)), _whole((kk, f)), _whole((1, f))]
        + r_out_specs,
        out_shape=[sd((t, f), BF16), sd((t, f), BF16), sd((t, d), F32), sd((kk, f), F32), sd((1, f), F32)] + r_outs,
        scratch_shapes=[pltpu.VMEM((keep, f), F32)] + r_sems,
        compiler_params=_params(("arbitrary",)),
    )(dact, gate_pre, gate_pre.reshape(t // keep, keep, f), up, cw, cb, wg, wu, *r_ins)
    return outs[:5], _ride_results(ride, outs[5:])


def _t5_bucket(dist):
    max_exact = REL_BUCKETS // 2
    d = np.maximum(dist, 1).astype(np.float32)
    large = max_exact + np.log(d / max_exact) / math.log(REL_MAX_DIST / max_exact) * (REL_BUCKETS - max_exact)
    large = np.minimum(large.astype(np.int32), REL_BUCKETS - 1)
    return np.where(dist < max_exact, dist, large).astype(np.int32)


def _band(window, dilation):
    qi = np.arange(ATTN_BLOCK)[:, None]
    kj = np.arange(2 * ATTN_BLOCK)[None, :]
    delta = ATTN_BLOCK + qi - kj
    mask = (delta >= 0) & (delta <= window // dilation)
    bucket = _t5_bucket(np.maximum(delta, 0) * dilation)
    return mask, bucket


def _attn_blocks(s, r):
    m = s // r
    assert m % ATTN_BLOCK == 0, "sequence length must be a multiple of dilation * block"
    return m // ATTN_BLOCK


def _perm_load(ref, r):
    if r == 1:
        return ref[...]
    m = ref.shape[0] // r
    return jnp.concatenate([ref[pl.ds(c, m, stride=r), :] for c in range(r)], axis=0)


def _perm_store(ref, g, val, r, add=False):
    if r == 1:
        ref[g] = ref[g] + val if add else val
        return
    m = val.shape[0] // r
    for c in range(r):
        rows = pl.ds(c, m, stride=r)
        part = val[c * m:(c + 1) * m]
        ref[g, rows, :] = ref[g, rows, :] + part if add else part


def _blocks(x):
    return x.reshape(x.shape[0] // ATTN_BLOCK, ATTN_BLOCK, x.shape[1])


def _prev_blocks(x):
    return jnp.concatenate([x[:1], x[:-1]], axis=0)


def _next_blocks(x):
    return jnp.concatenate([x[1:], jnp.zeros_like(x[:1])], axis=0)


def _first_block_neg(s, r):
    nblk = s // ATTN_BLOCK
    idx = lax.broadcasted_iota(jnp.int32, (nblk, 1, 1), 0)
    return jnp.where(idx % _attn_blocks(s, r) == 0, NEG, 0.0)


def _bdot_nt(a, b):
    return lax.dot_general(a, b, (((2,), (2,)), ((0,), (0,))), preferred_element_type=F32)


def _bdot(a, b):
    return lax.dot_general(a, b, (((2,), (1,)), ((0,), (0,))), preferred_element_type=F32)


def _bdot_tn(a, b):
    return lax.dot_general(a, b, (((1,), (1,)), ((0,), (0,))), preferred_element_type=F32)


def attn_fwd(qkv, biasm, n_heads, name, ride=()):
    b, s, _ = qkv.shape
    h = n_heads
    scale = HEAD_DIM ** -0.5
    blk = ATTN_BLOCK

    def body(q1_ref, q2_ref, q3_ref, k_ref, v_ref, bias_ref, o_ref, lse_ref, acc, m_s, l_s):
        for g, q_ref in enumerate((q1_ref, q2_ref, q3_ref)):
            r = DILATED[g][1]
            first = _first_block_neg(s, r)
            q = _blocks(_perm_load(q_ref, r).astype(BF16))
            k = _blocks(_perm_load(k_ref, r).astype(BF16))
            v = _blocks(_perm_load(v_ref, r).astype(BF16))
            s_cur = _bdot_nt(q, k) * scale + bias_ref[g, :, blk:]
            s_prev = _bdot_nt(q, _prev_blocks(k)) * scale + bias_ref[g, :, :blk] + first
            m = jnp.max(jnp.maximum(s_cur, s_prev), axis=-1, keepdims=True)
            p_cur = jnp.exp(s_cur - m)
            p_prev = jnp.exp(s_prev - m)
            l = jnp.sum(p_cur + p_prev, axis=-1, keepdims=True)
            o = _bdot(p_cur.astype(BF16), v) + _bdot(p_prev.astype(BF16), _prev_blocks(v))
            _perm_store(acc, g, o.reshape(s, HEAD_DIM), r)
            _perm_store(m_s, g, m.reshape(s, 1), r)
            _perm_store(l_s, g, l.reshape(s, 1), r)
        m_all = jnp.maximum(jnp.maximum(m_s[0], m_s[1]), m_s[2])
        w = [jnp.exp(m_s[g] - m_all) for g in range(N_GROUPS)]
        l = w[0] * l_s[0] + w[1] * l_s[1] + w[2] * l_s[2]
        o_ref[...] = (w[0] * acc[0] + w[1] * acc[1] + w[2] * acc[2]) / l
        lse_ref[...] = m_all + jnp.log(l)

    def col(j):
        return pl.BlockSpec((None, s, HEAD_DIM), lambda bi, hi, j=j: (bi, 0, j * h + hi))

    r_ins, r_in_specs, r_outs, r_out_specs, r_sems = _ride_args(ride)
    outs = pl.pallas_call(
        _riding(body, 6, 2, 3, ride, 2), name=name, grid=(b, h),
        in_specs=[col(0), col(1), col(2), col(3), col(4),
                  pl.BlockSpec((N_GROUPS, None, blk, 2 * blk), lambda bi, hi: (0, hi, 0, 0))] + r_in_specs,
        out_specs=[pl.BlockSpec((None, s, HEAD_DIM), lambda bi, hi: (bi, 0, hi)),
                   pl.BlockSpec((None, None, s, 1), lambda bi, hi: (bi, hi, 0, 0))] + r_out_specs,
        out_shape=[jax.ShapeDtypeStruct((b, s, h * HEAD_DIM), F32), jax.ShapeDtypeStruct((b, h, s, 1), F32)] + r_outs,
        scratch_shapes=[pltpu.VMEM((N_GROUPS, s, HEAD_DIM), F32), pltpu.VMEM((N_GROUPS, s, 1), F32),
                        pltpu.VMEM((N_GROUPS, s, 1), F32)] + r_sems,
        compiler_params=_params(("arbitrary", "arbitrary")),
    )(qkv, qkv, qkv, qkv, qkv, biasm, *r_ins)
    return outs[0], outs[1], _ride_results(ride, outs[2:])


def attn_bwd(qkv, biasm, o, lse, do, n_heads, name, ride=()):
    b, s, _ = qkv.shape
    h = n_heads
    scale = HEAD_DIM ** -0.5
    blk = ATTN_BLOCK

    def body(q1_ref, q2_ref, q3_ref, k_ref, v_ref, bias_ref, o_ref, lse_ref, do_ref,
             dq1_ref, dq2_ref, dq3_ref, dk_ref, dv_ref, ds_ref, dq_acc, kv_acc, delta):
        delta[...] = jnp.sum(do_ref[...] * o_ref[...], axis=-1, keepdims=True)
        kv_acc[...] = jnp.zeros(kv_acc.shape, F32)
        for g, q_ref in enumerate((q1_ref, q2_ref, q3_ref)):
            r = DILATED[g][1]
            first = _first_block_neg(s, r)
            q = _blocks(_perm_load(q_ref, r).astype(BF16))
            k = _blocks(_perm_load(k_ref, r).astype(BF16))
            v = _blocks(_perm_load(v_ref, r).astype(BF16))
            dob = _blocks(_perm_load(do_ref, r).astype(BF16))
            lse_b = _blocks(_perm_load(lse_ref, r))
            dl_b = _blocks(_perm_load(delta, r))
            k_prev, v_prev = _prev_blocks(k), _prev_blocks(v)
            p_cur = jnp.exp(_bdot_nt(q, k) * scale + bias_ref[g, :, blk:] - lse_b)
            p_prev = jnp.exp(_bdot_nt(q, k_prev) * scale + bias_ref[g, :, :blk] + first - lse_b)
            ds_cur = p_cur * (_bdot_nt(dob, v) - dl_b)
            ds_prev = p_prev * (_bdot_nt(dob, v_prev) - dl_b)
            ds_ref[g, :, blk:] = jnp.sum(ds_cur, axis=0)
            ds_ref[g, :, :blk] = jnp.sum(ds_prev, axis=0)
            ds_cur_b, ds_prev_b = ds_cur.astype(BF16), ds_prev.astype(BF16)
            dq = (_bdot(ds_cur_b, k) + _bdot(ds_prev_b, k_prev)) * scale
            _perm_store(dq_acc, g, dq.reshape(s, HEAD_DIM), r)
            dk = (_bdot_tn(ds_cur_b, q) + _next_blocks(_bdot_tn(ds_prev_b, q))) * scale
            dv = _bdot_tn(p_cur.astype(BF16), dob) + _next_blocks(_bdot_tn(p_prev.astype(BF16), dob))
            _perm_store(kv_acc, 0, dk.reshape(s, HEAD_DIM), r, add=True)
            _perm_store(kv_acc, 1, dv.reshape(s, HEAD_DIM), r, add=True)
        for g, out_ref in enumerate((dq1_ref, dq2_ref, dq3_ref)):
            out_ref[...] = dq_acc[g].astype(out_ref.dtype)
        dk_ref[...] = kv_acc[0].astype(dk_ref.dtype)
        dv_ref[...] = kv_acc[1].astype(dv_ref.dtype)

    def col(j):
        return pl.BlockSpec((None, s, HEAD_DIM), lambda bi, hi, j=j: (bi, 0, j * h + hi))

    head = pl.BlockSpec((None, s, HEAD_DIM), lambda bi, hi: (bi, 0, hi))
    sd = jax.ShapeDtypeStruct
    r_ins, r_in_specs, r_outs, r_out_specs, r_sems = _ride_args(ride)
    outs = pl.pallas_call(
        _riding(body, 9, 6, 3, ride, 2), name=name, grid=(b, h),
        in_specs=[col(0), col(1), col(2), col(3), col(4),
                  pl.BlockSpec((N_GROUPS, None, blk, 2 * blk), lambda bi, hi: (0, hi, 0, 0)),
                  head, pl.BlockSpec((None, None, s, 1), lambda bi, hi: (bi, hi, 0, 0)), head] + r_in_specs,
        out_specs=[head] * 5 + [pl.BlockSpec((None, None, N_GROUPS, blk, 2 * blk), lambda bi, hi: (bi, hi, 0, 0, 0))]
        + r_out_specs,
        out_shape=[sd((b, s, h * HEAD_DIM), BF16)] * 5 + [sd((b, h, N_GROUPS, blk, 2 * blk), F32)] + r_outs,
        scratch_shapes=[pltpu.VMEM((N_GROUPS, s, HEAD_DIM), F32), pltpu.VMEM((2, s, HEAD_DIM), F32),
                        pltpu.VMEM((s, 1), F32)] + r_sems,
        compiler_params=_params(("arbitrary", "arbitrary")),
    )(qkv, qkv, qkv, qkv, qkv, biasm, o, lse, do, *r_ins)
    return outs[:6], _ride_results(ride, outs[6:])


def bias_table(rel_rows, bucket_f, n_heads, name):
    g, blk, blk2 = bucket_f.shape
    h = n_heads

    def body(rb_ref, bk_ref, o_ref):
        for gi in range(g):
            bk = bk_ref[gi]
            for hi in range(h):
                rb = rb_ref[gi * h + hi]
                acc = jnp.full((blk, blk2), NEG, F32)
                for bucket in range(REL_BUCKETS):
                    acc = jnp.where(bk == float(bucket), rb[:, bucket:bucket + 1], acc)
                o_ref[gi, hi] = acc

    vmem = pl.BlockSpec(memory_space=pltpu.VMEM)
    return pl.pallas_call(
        body, name=name, in_specs=[vmem, vmem], out_specs=vmem,
        out_shape=jax.ShapeDtypeStruct((g, h, blk, blk2), F32),
        compiler_params=_params(),
    )(rel_rows, bucket_f)


def bias_grad(ds_sum, bucket_f, name):
    b, h, g, blk, blk2 = ds_sum.shape

    def body(ds_ref, bk_ref, o_ref):
        lane = lax.broadcasted_iota(jnp.int32, (1, 128), 1)
        for gi in range(g):
            bk = bk_ref[gi]
            for hi in range(h):
                tot = ds_ref[0, hi, gi]
                for bi in range(1, b):
                    tot = tot + ds_ref[bi, hi, gi]
                vec = jnp.zeros((1, 128), F32)
                for bucket in range(REL_BUCKETS):
                    val = jnp.sum(jnp.where(bk == float(bucket), tot, 0.0), keepdims=True)
                    vec = vec + jnp.where(lane == bucket, val, 0.0)
                o_ref[gi * h + hi] = vec

    vmem = pl.BlockSpec(memory_space=pltpu.VMEM)
    return pl.pallas_call(
        body, name=name, in_specs=[vmem, vmem], out_specs=vmem,
        out_shape=jax.ShapeDtypeStruct((g * h, 1, 128), F32),
        compiler_params=_params(),
    )(ds_sum, bucket_f)


def _chip_peers():
    x, y, c = lax.axis_index("x"), lax.axis_index("y"), lax.axis_index("c")
    me = 2 * x + y
    peers = [(1 - x, y, c), (x, 1 - y, c), (1 - x, 1 - y, c)]
    peer_chip = [2 * (1 - x) + y, 2 * x + (1 - y), 2 * (1 - x) + (1 - y)]
    return me, peers, peer_chip


def _any_specs(n):
    return [pl.BlockSpec(memory_space=pl.ANY)] * n


_MID_NUM, _MID_DEN = 3, 4


class _Exchange:
    def start(self, ins, outs, sems):
        local, sends, _ = self._copies(ins, outs, sems)
        for cp in local + sends:
            cp.start()

    def mid(self, ins, outs, sems):
        pass

    def wait(self, ins, outs, sems):
        local, sends, recvs = self._copies(ins, outs, sems)
        for cp in recvs():
            cp.wait_recv()
        for cp in sends:
            cp.wait_send()
        for cp in local:
            cp.wait()


class _Gather(_Exchange):
    HALF_ROWS = 16

    def __init__(self, arrays):
        n = len(arrays)
        self.ins = list(arrays)
        self.split = [a.shape[0] % (2 * self.HALF_ROWS) == 0 for a in arrays]
        self.out_shape = [jax.ShapeDtypeStruct((N_CHIPS,) + a.shape, a.dtype) for a in arrays]
        dma = pltpu.SemaphoreType.DMA
        self.sems = [dma((3 * n,)), dma((3 * n,)), dma((n,)), dma((3 * n,)), dma((3 * n,))]

    def _half(self, i, ref, sibling=False):
        if not self.split[i]:
            return ref
        half = self.ins[i].shape[0] // 2
        c = lax.axis_index("c")
        c = 1 - c if sibling else c
        return ref.at[pl.ds(pl.multiple_of(c * half, self.HALF_ROWS), half)]

    def _plan(self, ins, outs, sems):
        send1, recv1, local_sems, send2, recv2 = sems
        me, peers, peer_chip = _chip_peers()
        x, y, c = lax.axis_index("x"), lax.axis_index("y"), lax.axis_index("c")
        n = len(ins)
        pairs = [(i, k) for k in range(3) for i in range(n)]

        def fetch(i, k, slot):
            return pltpu.make_async_remote_copy(src_ref=self._half(i, ins[i]), dst_ref=self._half(i, outs[i].at[slot]),
                                                send_sem=send1.at[3 * i + k], recv_sem=recv1.at[3 * i + k],
                                                device_id=peers[k], device_id_type=MESH)

        def share(i, k, sibling):
            part = self._half(i, outs[i].at[peer_chip[k]], sibling)
            return pltpu.make_async_remote_copy(src_ref=part, dst_ref=part, send_sem=send2.at[3 * i + k],
                                                recv_sem=recv2.at[3 * i + k], device_id=(x, y, 1 - c),
                                                device_id_type=MESH)

        split_pairs = [(i, k) for i, k in pairs if self.split[i]]
        return dict(
            local=lambda: [pltpu.make_async_copy(ins[i], outs[i].at[me], local_sems.at[i]) for i in range(n)],
            fetch_out=lambda: [fetch(i, k, me) for i, k in pairs],
            fetch_in=lambda: [(fetch(i, k, peer_chip[k]), share(i, k, False) if self.split[i] else None)
                              for i, k in pairs],
            share_out=lambda: [share(i, k, False) for i, k in split_pairs],
            share_in=lambda: [share(i, k, True) for i, k in split_pairs])

    def start(self, ins, outs, sems):
        plan = self._plan(ins, outs, sems)
        for cp in plan["local"]() + plan["fetch_out"]():
            cp.start()

    def mid(self, ins, outs, sems):
        plan = self._plan(ins, outs, sems)
        for arrived, forward in plan["fetch_in"]():
            arrived.wait_recv()
            if forward is not None:
                forward.start()

    def wait(self, ins, outs, sems):
        plan = self._plan(ins, outs, sems)
        for cp in plan["share_in"]():
            cp.wait_recv()
        for cp in plan["fetch_out"]() + plan["share_out"]():
            cp.wait_send()
        for cp in plan["local"]():
            cp.wait()


class _Scatter(_Exchange):
    def __init__(self, slabs, whole=()):
        self.n_slabs = len(slabs)
        self.ins = list(slabs) + list(whole)
        n = len(self.ins)
        self.out_shape = [jax.ShapeDtypeStruct(a.shape, a.dtype) for a in slabs] \
            + [jax.ShapeDtypeStruct((N_CHIPS,) + a.shape, a.dtype) for a in whole]
        self.sems = [pltpu.SemaphoreType.DMA((3 * n,)), pltpu.SemaphoreType.DMA((3 * n,)), pltpu.SemaphoreType.DMA((n,))]

    def _copies(self, ins, outs, sems):
        send_sems, recv_sems, local_sems = sems
        me, peers, peer_chip = _chip_peers()
        n = len(ins)

        def src(i, chip):
            return ins[i].at[chip] if i < self.n_slabs else ins[i]

        def remote(i, k, src_chip, slot):
            return pltpu.make_async_remote_copy(src_ref=src(i, src_chip), dst_ref=outs[i].at[slot],
                                                send_sem=send_sems.at[3 * i + k], recv_sem=recv_sems.at[3 * i + k],
                                                device_id=peers[k], device_id_type=MESH)

        local = [pltpu.make_async_copy(src(i, me), outs[i].at[me], local_sems.at[i]) for i in range(n)]
        sends = [remote(i, k, peer_chip[k], me) for i in range(n) for k in range(3)]
        return local, sends, lambda: [remote(i, k, me, peer_chip[k]) for i in range(n) for k in range(3)]


class _Swap(_Exchange):
    def __init__(self, arrays):
        n = len(arrays)
        self.ins = list(arrays)
        self.out_shape = [jax.ShapeDtypeStruct(a.shape, a.dtype) for a in arrays]
        self.sems = [pltpu.SemaphoreType.DMA((n,)), pltpu.SemaphoreType.DMA((n,))]

    def _copies(self, ins, outs, sems):
        send_sems, recv_sems = sems
        x, y, c = lax.axis_index("x"), lax.axis_index("y"), lax.axis_index("c")
        cps = [pltpu.make_async_remote_copy(src_ref=ins[i], dst_ref=outs[i], send_sem=send_sems.at[i],
                                            recv_sem=recv_sems.at[i], device_id=(x, y, 1 - c), device_id_type=MESH)
               for i in range(len(ins))]
        return [], cps, lambda: cps


def _riding(body, n_in, n_out, n_scratch, ride, rank):
    if not ride:
        return body
    r_in = sum(len(e.ins) for e in ride)
    r_out = sum(len(e.out_shape) for e in ride)

    def split(refs, sizes):
        out, a = [], 0
        for sz in sizes:
            out.append(refs[a:a + sz])
            a += sz
        return out

    def wrapped(*refs):
        a = 0
        parts = []
        for sz in (n_in, r_in, n_out, r_out, n_scratch):
            parts.append(refs[a:a + sz])
            a += sz
        own_in, ex_in, own_out, ex_out, own_scratch = parts
        ex_sems = refs[a:]
        ins = split(ex_in, [len(e.ins) for e in ride])
        outs = split(ex_out, [len(e.out_shape) for e in ride])
        sems = split(ex_sems, [len(e.sems) for e in ride])
        if rank:
            step, total = 0, 1
            for d in range(rank):
                step = step * pl.num_programs(d) + pl.program_id(d)
                total = total * pl.num_programs(d)

            @pl.when(step == 0)
            def _():
                for e, i, o, s in zip(ride, ins, outs, sems):
                    e.start(i, o, s)

            body(*own_in, *own_out, *own_scratch)

            @pl.when(step == (total * _MID_NUM) // _MID_DEN)
            def _():
                for e, i, o, s in zip(ride, ins, outs, sems):
                    e.mid(i, o, s)

            @pl.when(step == total - 1)
            def _():
                for e, i, o, s in zip(ride, ins, outs, sems):
                    e.wait(i, o, s)
        else:
            for phase in ("start", "mid", "wait"):
                for e, i, o, s in zip(ride, ins, outs, sems):
                    getattr(e, phase)(i, o, s)

    return wrapped


def _ride_args(ride):
    ins = [a for e in ride for a in e.ins]
    outs = [s for e in ride for s in e.out_shape]
    sems = [s for e in ride for s in e.sems]
    return ins, _any_specs(len(ins)), outs, _any_specs(len(outs)), sems


def _ride_results(ride, flat):
    out, a = [], 0
    for e in ride:
        out.append(list(flat[a:a + len(e.out_shape)]))
        a += len(e.out_shape)
    return out


def exchange(ride, name):
    ins, in_specs, outs, out_specs, sems = _ride_args(ride)
    res = pl.pallas_call(
        _riding(lambda: None, 0, 0, 0, ride, 0), name=name,
        in_specs=in_specs, out_specs=out_specs, out_shape=outs, scratch_shapes=sems,
    )(*ins)
    return _ride_results(ride, res)


def _sum_slots(ref):
    acc = ref[0].astype(F32)
    for j in range(1, ref.shape[0]):
        acc = acc + ref[j].astype(F32)
    return acc


def sum_pairs(mine, other, name, tr=176):
    n, r, w = mine.shape
    tr = r if r <= tr else _tile(r, tr)

    def body(a_ref, b_ref, o_ref):
        o_ref[...] = _sum_slots(a_ref) + _sum_slots(b_ref)

    spec = pl.BlockSpec((n, tr, w), lambda i: (0, i, 0))
    return pl.pallas_call(
        body, name=name, grid=(r // tr,),
        in_specs=[spec, spec], out_specs=_rows(tr, w),
        out_shape=jax.ShapeDtypeStruct((r, w), F32),
        compiler_params=_params(("parallel",)),
    )(mine, other)


def _adamw_update(w, m, v, g):
    c1 = 1.0 - ADAM_B1 ** ADAM_STEP
    c2 = 1.0 - ADAM_B2 ** ADAM_STEP
    nm = ADAM_B1 * m + (1.0 - ADAM_B1) * g
    nv = ADAM_B2 * v + (1.0 - ADAM_B2) * (g * g)
    return nm, nv, (-ADAM_LR) * ((nm / c1) / (jnp.sqrt(nv / c2) + ADAM_EPS) + ADAM_WD * w)


def adamw(w, m, v, mine, other, name, tr=256):
    r, c = w.shape
    tr = r if r % 8 else _tile(r, tr)

    def body(w_ref, m_ref, v_ref, a_ref, b_ref, g_ref, d_ref, nm_ref, nv_ref):
        g = _sum_slots(a_ref) + _sum_slots(b_ref)
        nm, nv, delta = _adamw_update(w_ref[...], m_ref[...], v_ref[...], g)
        g_ref[...] = g
        nm_ref[...] = nm
        nv_ref[...] = nv
        d_ref[...] = delta

    spec = _rows(tr, c)
    gspec = pl.BlockSpec((N_CHIPS, tr, c), lambda i: (0, i, 0))
    return pl.pallas_call(
        body, name=name, grid=(r // tr,),
        in_specs=[spec] * 3 + [gspec] * 2, out_specs=[spec] * 4,
        out_shape=[jax.ShapeDtypeStruct((r, c), F32)] * 4,
        compiler_params=_params(("parallel",)),
    )(w, m, v, mine, other)


def adamw_small(ws, ms, vs, gs, name):
    n = len(ws)

    def body(*refs):
        ins, outs = refs[:4 * n], refs[4 * n:]
        for i in range(n):
            w_ref, m_ref, v_ref, g_ref = ins[4 * i:4 * i + 4]
            d_ref, nm_ref, nv_ref = outs[3 * i:3 * i + 3]
            nm, nv, delta = _adamw_update(w_ref[...], m_ref[...], v_ref[...], g_ref[...])
            d_ref[...] = delta
            nm_ref[...] = nm
            nv_ref[...] = nv

    flat = [a for quad in zip(ws, ms, vs, gs) for a in quad]
    vmem = pl.BlockSpec(memory_space=pltpu.VMEM)
    outs = pl.pallas_call(
        body, name=name,
        in_specs=[vmem] * (4 * n), out_specs=[vmem] * (3 * n),
        out_shape=[jax.ShapeDtypeStruct(w.shape, F32) for w in ws for _ in range(3)],
        compiler_params=_params(),
    )(*flat)
    return [tuple(outs[3 * i:3 * i + 3]) for i in range(n)]


_PARAMS = (
    ("rel_bias", None), ("norm_mix_pre", None), ("norm_mix_post", None), ("w_in", 1), ("conv_rnn_w", 1),
    ("conv_rnn_b", None), ("w_rg_a", None), ("b_rg_a", None), ("w_rg_x", None), ("b_rg_x", None),
    ("lru_lambda", None), ("w_branch_rnn", 0), ("w_branch_att", 1), ("w_out", 0), ("norm_ffn_pre", None),
    ("norm_ffn_post", None), ("w_ffn_gate", 1), ("w_ffn_up", 1), ("conv_ffn_w", 1), ("conv_ffn_b", None),
    ("w_ffn_down", 0),
)
_SMALL = 65536


def _as2d(a):
    a = a[0] if a.shape[0] == 1 and a.ndim >= 3 else a
    return a.reshape(-1, a.shape[-1]) if a.ndim == 3 else a


def _pack(pieces, dtype):
    flat = jnp.concatenate([p.astype(dtype).reshape(-1) for p in pieces])
    unit = PACK_W * PACK_ROWS
    pad = (-flat.shape[0]) % unit
    flat = jnp.pad(flat, (0, pad))
    return flat.reshape(-1, PACK_W)


def _unpack(buf, shapes):
    flat = buf.reshape(-1)
    out, off = [], 0
    for shp in shapes:
        n = int(np.prod(shp))
        out.append(flat[off:off + n].reshape(shp))
        off += n
    return out


def _join(slots, ax):
    if ax == 0:
        return slots.reshape(-1, slots.shape[-1])
    return jnp.transpose(slots, (1, 0, 2)).reshape(slots.shape[1], -1)


def _cut(full, ax):
    if ax == 0:
        return full.reshape(N_CHIPS, -1, full.shape[-1])
    return jnp.transpose(full.reshape(full.shape[0], N_CHIPS, -1), (1, 0, 2))


def kernel(x, rel_bias, norm_mix_pre, norm_mix_post, w_in, conv_rnn_w, conv_rnn_b, w_rg_a, b_rg_a, w_rg_x, b_rg_x, lru_lambda, w_branch_rnn, w_branch_att, w_out, norm_ffn_pre, norm_ffn_post, w_ffn_gate, w_ffn_up, conv_ffn_w, conv_ffn_b, w_ffn_down, loss_target, m_rel_bias, m_norm_mix_pre, m_norm_mix_post, m_w_in, m_conv_rnn_w, m_conv_rnn_b, m_w_rg_a, m_b_rg_a, m_w_rg_x, m_b_rg_x, m_lru_lambda, m_w_branch_rnn, m_w_branch_att, m_w_out, m_norm_ffn_pre, m_norm_ffn_post, m_w_ffn_gate, m_w_ffn_up, m_conv_ffn_w, m_conv_ffn_b, m_w_ffn_down, v_rel_bias, v_norm_mix_pre, v_norm_mix_post, v_w_in, v_conv_rnn_w, v_conv_rnn_b, v_w_rg_a, v_b_rg_a, v_w_rg_x, v_b_rg_x, v_lru_lambda, v_w_branch_rnn, v_w_branch_att, v_w_out, v_norm_ffn_pre, v_norm_ffn_post, v_w_ffn_gate, v_w_ffn_up, v_conv_ffn_w, v_conv_ffn_b, v_w_ffn_down):
    args = dict(locals())
    names = [n for n, _ in _PARAMS]
    axis = dict(_PARAMS)
    w_loc = {n: args[n] for n in names}
    m_loc = {n: args["m_" + n] for n in names}
    v_loc = {n: args["v_" + n] for n in names}
    sharded = [n for n in names if axis[n] is not None]
    replicated = [n for n in names if axis[n] is None]

    big = [n for n in sharded if w_loc[n].size >= _SMALL]
    small_sharded = [n for n in sharded if n not in big]
    small = replicated + small_sharded

    first = ["w_in"] + small_sharded
    srcs = [_as2d(w_loc[n]).astype(BF16) if n in big else _as2d(w_loc[n]) for n in first]
    (gathered,) = exchange([_Gather(srcs)], "gather_first")
    p = {n: _join(a, axis[n]) for n, a in zip(first, gathered)}
    for n in replicated:
        p[n] = _as2d(w_loc[n])
    shards = {n: _as2d(w_loc[n]).astype(BF16) for n in big if n not in first}

    last = "norm_mix_pre"
    early = [n for n in small if n != last]
    received, sibling, g_small, loss_part = _local_step(x, loss_target, p, shards, early)

    ((received["last"],),) = exchange([_Scatter([], [_pack([g_small[last]], BF16)])], "scatter_last")
    late = [n for n in received if n not in sibling]
    (swapped,) = exchange([_Swap([received[n] for n in late])], "swap_last")
    sibling.update(zip(late, swapped))
    early_sum = sum_pairs(received["small"], sibling["small"], "sum_small")
    last_sum = sum_pairs(received["last"], sibling["last"], "sum_last")
    g_tot = dict(zip(early, _unpack(early_sum, [g_small[n].shape for n in early])))
    (g_tot[last],) = _unpack(last_sum, [g_small[last].shape])
    chip = 2 * lax.axis_index("x") + lax.axis_index("y")
    for n in small_sharded:
        size = g_tot[n].shape[axis[n]] // N_CHIPS
        g_tot[n] = lax.dynamic_slice_in_dim(g_tot[n], chip * size, size, axis=axis[n])

    out_g, out_d, out_m, out_v = {}, {}, {}, {}
    for n in big:
        res = adamw(_as2d(w_loc[n]), _as2d(m_loc[n]), _as2d(v_loc[n]), received[n], sibling[n], "adamw_" + n)
        out_g[n], out_d[n], out_m[n], out_v[n] = (t.reshape(w_loc[n].shape) for t in res)
    res = adamw_small([_as2d(w_loc[n]) for n in small], [_as2d(m_loc[n]) for n in small],
                      [_as2d(v_loc[n]) for n in small], [g_tot[n] for n in small], "adamw_small")
    for n, (d, nm, nv) in zip(small, res):
        out_g[n], out_d[n], out_m[n], out_v[n] = (t.reshape(w_loc[n].shape) for t in (g_tot[n], d, nm, nv))

    d_model = x.shape[-1]
    loss = lax.psum(0.5 * jnp.sum(loss_part) / d_model, ("x", "y", "c"))
    grad_x = g_small["x"]
    return (loss, grad_x, *[out_g[n] for n in names], *[out_d[n] for n in names],
            *[out_m[n] for n in names], *[out_v[n] for n in names])


def _local_step(x, target, p, shards, small_early):
    axis = dict(_PARAMS)
    b, s, d = x.shape
    t = b * s
    rnn = p["b_rg_a"].shape[1]
    ffn = p["conv_ffn_b"].shape[1]
    nbk = rnn // p["w_rg_a"].shape[1]
    hkv = (p["w_in"].shape[1] - rnn - 2 * d) // (N_GROUPS + 2)
    h = hkv // HEAD_DIM
    nq = N_GROUPS * hkv

    x2 = x.reshape(t, d)
    tgt = target.reshape(t, d)
    w_in = p["w_in"]
    in_splits = (rnn, nq + 2 * hkv, 2 * d)
    wa = p["w_rg_a"].reshape(nbk, -1, p["w_rg_a"].shape[1]).astype(BF16)
    wx = p["w_rg_x"].reshape(nbk, -1, p["w_rg_x"].shape[1]).astype(BF16)
    cw_r, cb_r = p["conv_rnn_w"], p["conv_rnn_b"]
    cw_f, cb_f = p["conv_ffn_w"], p["conv_ffn_b"]

    masks, buckets = zip(*[_band(w_, r_) for w_, r_ in DILATED])
    bucket_f = jnp.asarray(np.where(np.stack(masks), np.stack(buckets), -1).astype(np.float32))
    rel_rows = jnp.pad(p["rel_bias"].T, ((0, 0), (0, 128 - REL_BUCKETS)))[:, None, :]
    biasm = bias_table(rel_rows, bucket_f, h, "bias_table")

    early = ["w_branch_rnn", "w_branch_att", "w_out"]
    hn1, (xr, qkv, gts), (got,) = norm_mm(x2, p["norm_mix_pre"], [w_in], [in_splits], "in_proj",
                                          ride=[_Gather([shards[n] for n in early])])
    p.update({n: _join(a, axis[n]) for n, a in zip(early, got)})
    xr3 = xr.reshape(b, s, rnn)
    (y_rnn, a_rnn, xc_rnn), (got,) = rglru_fwd(xr3, cw_r, cb_r, wa, p["b_rg_a"], wx, p["b_rg_x"], p["lru_lambda"], "rglru_fwd",
                              ride=[_Gather([shards[n] for n in ("w_ffn_gate", "w_ffn_up")])])
    p.update(zip(("w_ffn_gate", "w_ffn_up"), got))
    qkv3 = qkv.reshape(b, s, -1)
    o_att, lse, ((got,),) = attn_fwd(qkv3, biasm, h, "attn_fwd", ride=[_Gather([shards["w_ffn_down"]])])
    p["w_ffn_down"] = _join(got, axis["w_ffn_down"])
    merged, br, ba, mix, h1 = merge_out(y_rnn.reshape(t, rnn), o_att.reshape(t, hkv), gts, p["w_branch_rnn"],
                                        p["w_branch_att"], p["w_out"], p["norm_mix_post"], x2, "merge_out")
    hn2, gate_pre, up, act = ffn_in_act(h1, p["norm_ffn_pre"], p["w_ffn_gate"], p["w_ffn_up"], cw_f, cb_f, s, "ffn_in")

    g, gb = {}, {}
    recv, sib = {}, {}

    def rows4(a):
        return a.reshape(N_CHIPS, -1, a.shape[-1])

    dy, dff, dact, g["norm_ffn_post"], loss_part = ffn_down_loss(act, p["w_ffn_down"], p["norm_ffn_post"], h1, tgt,
                                                                  "ffn_down")
    gb["w_ffn_down"] = rows4(mm_tn(act, [dff], "ffn_down_dw"))
    (dgp, dup, dhn2, g["conv_ffn_w"], g["conv_ffn_b"]), ((recv["w_ffn_down"],),) = ffn_in_bwd(
        dact, gate_pre, up, cw_f, cb_f, p["w_ffn_gate"], p["w_ffn_up"], s, "ffn_in_bwd",
        ride=[_Scatter([gb["w_ffn_down"]])])
    gb["w_ffn_gate"] = mm_tn(hn2, [dgp], "ffn_gate_dw", col_shards=N_CHIPS)
    gb["w_ffn_up"] = mm_tn(hn2, [dup], "ffn_up_dw", col_shards=N_CHIPS)
    (dh1, dgts, dy_rnn, do_att, g["norm_ffn_pre"], g["norm_mix_post"], dw_out, dw_br,
     gb["w_branch_att"]) = mid_bwd(dhn2, h1, p["norm_ffn_pre"], dy, mix, p["norm_mix_post"], p["w_out"], gts, br, ba,
                                   p["w_branch_rnn"], p["w_branch_att"], merged, y_rnn.reshape(t, rnn),
                                   o_att.reshape(t, hkv), "mid_bwd")
    gb["w_out"], gb["w_branch_rnn"] = rows4(dw_out), rows4(dw_br)
    ffn_in = ["w_ffn_gate", "w_ffn_up"]
    (dxr, g["conv_rnn_w"], g["conv_rnn_b"], dwa, g["b_rg_a"], dwx, g["b_rg_x"], g["lru_lambda"]), (got,) = rglru_bwd(
        xr3, y_rnn, dy_rnn.reshape(b, s, rnn), a_rnn, xc_rnn, cw_r, wa, p["b_rg_a"], wx, p["b_rg_x"], p["lru_lambda"], "rglru_bwd",
        ride=[_Scatter([gb[n] for n in ffn_in])])
    recv.update(zip(ffn_in, got))
    g["w_rg_a"] = dwa.reshape(p["w_rg_a"].shape)
    g["w_rg_x"] = dwx.reshape(p["w_rg_x"].shape)
    mid = ["w_out", "w_branch_rnn", "w_branch_att"]
    early_recv = ["w_ffn_down"] + ffn_in
    (dq1, dq2, dq3, dk, dv, ds_sum), (got, swapped) = attn_bwd(
        qkv3, biasm, o_att, lse, do_att.reshape(b, s, hkv), h, "attn_bwd",
        ride=[_Scatter([gb[n] for n in mid]), _Swap([recv[n] for n in early_recv])])
    recv.update(zip(mid, got))
    sib.update(zip(early_recv, swapped))
    rows = bias_grad(ds_sum, bucket_f, "bias_grad")
    g["rel_bias"] = rows[:, 0, :REL_BUCKETS].T
    dproj = [dxr.reshape(t, rnn)] + [a.reshape(t, hkv) for a in (dq1, dq2, dq3, dk, dv)] + [dgts]
    lanes = 128
    cut_at = w_in.shape[1] // 2 - (rnn + nq)
    direct = 0 < cut_at < hkv and cut_at % lanes == 0 and hkv % lanes == 0
    if direct:
        first_half = dproj[:4] + [(dproj[4], cut_at, 0)]
        second_half = [(dproj[4], lanes, j) for j in range(cut_at // lanes, hkv // lanes)] + dproj[5:]
    else:
        first_half, second_half = dproj[:4], dproj[4:]
    shards_per_call = N_CHIPS // 2 if direct else 1
    dw_a, (got,) = mm_tn(hn1, first_half, "in_proj_dw_a", col_shards=shards_per_call,
                         ride=[_Swap([recv[n] for n in mid])])
    sib.update(zip(mid, got))
    pack = _pack([g[n] for n in small_early], BF16)
    dw_b, ((recv["small"],),) = mm_tn(hn1, second_half, "in_proj_dw_b", col_shards=shards_per_call,
                                      ride=[_Scatter([], [pack])])
    if direct:
        gb["w_in"] = jnp.concatenate([dw_a, dw_b], axis=0)
    else:
        gb["w_in"] = _cut(jnp.concatenate([dw_a[0], dw_b[0]], axis=1), 1)
    dx, g["norm_mix_pre"], ((recv["w_in"],),) = mm_nt(
        [(dproj, w_in)], "in_proj_dx", norm=(x2, p["norm_mix_pre"], dh1), ride=[_Scatter([gb["w_in"]])])
    g["x"] = dx.reshape(b, s, d)
    return recv, sib, g, loss_part
```

```python
import math

import numpy as np
import jax
import jax.numpy as jnp
from jax import lax
from jax.experimental import pallas as pl
from jax.experimental.pallas import tpu as pltpu

F32 = jnp.float32
BF16 = jnp.bfloat16

EPS = 1e-6
HEAD_DIM = 128
ATTN_BLOCK = 128
DILATED = ((128, 1), (512, 4), (2048, 16))
N_GROUPS = len(DILATED)
REL_BUCKETS = 32
REL_MAX_DIST = 2048
LRU_C = 8.0
NEG = -1e30

ADAM_LR = 0.001
ADAM_B1 = 0.9
ADAM_B2 = 0.999
ADAM_EPS = 1e-08
ADAM_WD = 0.01
ADAM_STEP = 10

N_CHIPS = 4
PACK_W = 1024
PACK_ROWS = 16
VMEM_LIMIT = 56 * 1024 * 1024
MESH = pl.DeviceIdType.MESH


def _params(sem=None):
    return pltpu.CompilerParams(dimension_semantics=sem, vmem_limit_bytes=VMEM_LIMIT)


def _dot(a, b):
    return jnp.dot(a, b, preferred_element_type=F32)


def _dot_nt(a, b):
    return lax.dot_general(a, b, (((1,), (1,)), ((), ())), preferred_element_type=F32)


def _dot_tn(a, b):
    return lax.dot_general(a, b, (((0,), (0,)), ((), ())), preferred_element_type=F32)


def _sig(x):
    return 0.5 * jnp.tanh(0.5 * x) + 0.5


def _rows(tm, w):
    return pl.BlockSpec((tm, w), lambda i: (i, 0))


def _whole(shape):
    nd = len(shape)
    return pl.BlockSpec(tuple(shape), lambda *_: (0,) * nd)


def _resident(shape):
    nd = len(shape)
    return pl.BlockSpec(tuple(shape), lambda *_: (0,) * nd, pipeline_mode=pl.Buffered(1))


def _tile(t, want):
    while t % want:
        want //= 2
    return want


def norm_mm(x, g, ws, splits, name, ride=(), tm=512):
    t, d = x.shape
    tm = _tile(t, tm)
    nw = len(ws)
    widths = [n for sp in splits for n in sp]

    def body(x_ref, g_ref, *refs):
        w_refs, hn_ref, o_refs = refs[:nw], refs[nw], refs[nw + 1:]
        xv = x_ref[...]
        inv = lax.rsqrt(jnp.mean(xv * xv, axis=-1, keepdims=True) + EPS)
        hn = (xv * inv * g_ref[...]).astype(BF16)
        hn_ref[...] = hn
        o = 0
        for w_ref, sp in zip(w_refs, splits):
            off = 0
            for n in sp:
                o_refs[o][...] = _dot(hn, w_ref[:, off:off + n])
                off += n
                o += 1

    r_ins, r_in_specs, r_outs, r_out_specs, r_sems = _ride_args(ride)
    n_out = 1 + len(widths)
    outs = pl.pallas_call(
        _riding(body, 2 + nw, n_out, 0, ride, 1), name=name, grid=(t // tm,),
        in_specs=[_rows(tm, d), _whole(g.shape)] + [_resident(w.shape) for w in ws] + r_in_specs,
        out_specs=[_rows(tm, d)] + [_rows(tm, n) for n in widths] + r_out_specs,
        out_shape=[jax.ShapeDtypeStruct((t, d), BF16)] + [jax.ShapeDtypeStruct((t, n), F32) for n in widths] + r_outs,
        scratch_shapes=r_sems,
        compiler_params=_params(("arbitrary",)),
    )(x, g, *ws, *r_ins)
    return outs[0], outs[1:n_out], _ride_results(ride, outs[n_out:])


def mm_nt(groups, name, ride=(), norm=None, tm=512):
    dys_all = [dy for dys, _ in groups for dy in dys]
    ws = [w for _, w in groups]
    t = dys_all[0].shape[0]
    k = ws[0].shape[0]
    tm = _tile(t, tm)
    n = len(dys_all)
    extra = list(norm) if norm else []

    def body(*refs):
        dy_refs, w_refs = refs[:n], refs[n:n + len(ws)]
        rest = refs[n + len(ws):]
        acc = None
        i = 0
        for (dys, _), w_ref in zip(groups, w_refs):
            off = 0
            for dy in dys:
                width = dy.shape[1]
                part = _dot_nt(dy_refs[i][...].astype(BF16), w_ref[:, off:off + width])
                acc = part if acc is None else acc + part
                off += width
                i += 1
        if norm:
            u_ref, g_ref, add_ref, o_ref, dg_ref = rest

            @pl.when(pl.program_id(0) == 0)
            def _():
                dg_ref[...] = jnp.zeros(dg_ref.shape, F32)

            du, dg_rows = _rms_bwd(acc, u_ref[...], g_ref[...])
            o_ref[...] = du + add_ref[...]
            dg_ref[...] += jnp.sum(dg_rows, axis=0, keepdims=True)
        else:
            rest[0][...] = acc

    n_out = 2 if norm else 1
    r_ins, r_in_specs, r_outs, r_out_specs, r_sems = _ride_args(ride)
    outs = pl.pallas_call(
        _riding(body, n + len(ws) + len(extra), n_out, 0, ride, 1), name=name, grid=(t // tm,),
        in_specs=[_rows(tm, dy.shape[1]) for dy in dys_all] + [_resident(w.shape) for w in ws]
        + ([_rows(tm, k), _whole((1, k)), _rows(tm, k)] if norm else []) + r_in_specs,
        out_specs=[_rows(tm, k)] + ([_whole((1, k))] if norm else []) + r_out_specs,
        out_shape=[jax.ShapeDtypeStruct((t, k), F32)] + ([jax.ShapeDtypeStruct((1, k), F32)] if norm else []) + r_outs,
        scratch_shapes=r_sems,
        compiler_params=_params(("arbitrary",)),
    )(*dys_all, *ws, *extra, *r_ins)
    return tuple(outs[:n_out]) + (_ride_results(ride, outs[n_out:]),)


def mm_tn(a, dys, name, col_shards=1, ride=(), tm=1024, groups=1, group=0, into=None):
    t, k = a.shape
    tm = _tile(t, tm)
    n = len(dys)
    pieces = [dy if isinstance(dy, tuple) else (dy, dy.shape[1], 0) for dy in dys]
    ntot = sum(width for _, width, _ in pieces)
    wsh = ntot // col_shards
    n_own = 1 + n + (into is not None)

    def body(a_ref, *refs):
        dy_refs, o_ref, acc = refs[:n], refs[n_own - 1], refs[n_own]

        @pl.when(pl.program_id(0) == 0)
        def _():
            acc[...] = jnp.zeros(acc.shape, F32)

        av = a_ref[...].astype(BF16)
        off = 0
        for dy_ref in dy_refs:
            width = dy_ref.shape[1]
            acc[:, off:off + width] += _dot_tn(av, dy_ref[...].astype(BF16))
            off += width

        @pl.when(pl.program_id(0) == pl.num_programs(0) - 1)
        def _():
            for j in range(col_shards):
                o_ref[j] = acc[:, j * wsh:(j + 1) * wsh].astype(o_ref.dtype)

    r_ins, r_in_specs, r_outs, r_out_specs, r_sems = _ride_args(ride)
    outs = pl.pallas_call(
        _riding(body, n_own, 1, 1, ride, 1), name=name, grid=(t // tm,),
        in_specs=[_rows(tm, k)] + [pl.BlockSpec((tm, width), lambda i, j=j: (i, j)) for _, width, j in pieces]
        + ([pl.BlockSpec(memory_space=pl.ANY)] if into is not None else []) + r_in_specs,
        out_specs=[pl.BlockSpec((col_shards, k, wsh), lambda i: (group, 0, 0))] + r_out_specs,
        out_shape=[jax.ShapeDtypeStruct((groups * col_shards, k, wsh), BF16)] + r_outs,
        scratch_shapes=[pltpu.VMEM((k, ntot), F32)] + r_sems,
        input_output_aliases={n_own - 1: 0} if into is not None else {},
        compiler_params=_params(("arbitrary",)),
    )(a, *[arr for arr, _, _ in pieces], *([into] if into is not None else []), *r_ins)
    return (outs[0], _ride_results(ride, outs[1:])) if ride else outs[0]


def _rms_bwd(dz, u, g):
    d = u.shape[-1]
    inv = lax.rsqrt(jnp.mean(u * u, axis=-1, keepdims=True) + EPS)
    dzg = dz * g
    proj = jnp.sum(dzg * u, axis=-1, keepdims=True) * (1.0 / d)
    du = inv * (dzg - u * (inv * inv) * proj)
    dg_rows = dz * u * inv
    return du, dg_rows


def ffn_down_loss(act, wd, g, h1, target, name, tm=512):
    t, f = act.shape
    d = wd.shape[1]
    tm = _tile(t, tm)

    def body(a_ref, w_ref, g_ref, h_ref, t_ref, dy_ref, dff_ref, dact_ref, dg_ref, loss_ref):
        @pl.when(pl.program_id(0) == 0)
        def _():
            dg_ref[...] = jnp.zeros(dg_ref.shape, F32)
            loss_ref[...] = jnp.zeros(loss_ref.shape, F32)

        wv = w_ref[...]
        gv = g_ref[...]
        ff = _dot(a_ref[...], wv)
        inv = lax.rsqrt(jnp.mean(ff * ff, axis=-1, keepdims=True) + EPS)
        err = h_ref[...] + ff * inv * gv - t_ref[...]
        loss_ref[...] += jnp.sum(err * err, axis=0, keepdims=True)
        dy = err * (1.0 / d)
        dy_ref[...] = dy
        du, dg_rows = _rms_bwd(dy, ff, gv)
        dff = du.astype(BF16)
        dff_ref[...] = dff
        dg_ref[...] += jnp.sum(dg_rows, axis=0, keepdims=True)
        dact_ref[...] = _dot_nt(dff, wv)

    return pl.pallas_call(
        body, name=name, grid=(t // tm,),
        in_specs=[_rows(tm, f), _resident(wd.shape), _whole(g.shape), _rows(tm, d), _rows(tm, d)],
        out_specs=[_rows(tm, d), _rows(tm, d), _rows(tm, f), _whole((1, d)), _whole((1, d))],
        out_shape=[jax.ShapeDtypeStruct((t, d), F32), jax.ShapeDtypeStruct((t, d), BF16),
                   jax.ShapeDtypeStruct((t, f), F32), jax.ShapeDtypeStruct((1, d), F32),
                   jax.ShapeDtypeStruct((1, d), F32)],
        compiler_params=_params(("arbitrary",)),
    )(act, wd, g, h1, target)


def merge_out(y_rnn, o_att, gts, w_br, w_ba, w_out, g, x, name, tm=512):
    t = y_rnn.shape[0]
    d = w_br.shape[1]
    tm = _tile(t, tm)

    def body(y_ref, o_ref, g_ref, wbr_ref, wba_ref, wo_ref, gn_ref, x_ref, m_ref, br_ref, ba_ref, mix_ref, h_ref):
        br = _dot(y_ref[...].astype(BF16), wbr_ref[...])
        ba = _dot(o_ref[...].astype(BF16), wba_ref[...])
        gv = g_ref[...]
        merged = (_sig(gv[:, :d]) * br + _sig(gv[:, d:]) * ba).astype(BF16)
        m_ref[...] = merged
        br_ref[...] = br
        ba_ref[...] = ba
        mix = _dot(merged, wo_ref[...])
        mix_ref[...] = mix
        inv = lax.rsqrt(jnp.mean(mix * mix, axis=-1, keepdims=True) + EPS)
        h_ref[...] = x_ref[...] + mix * inv * gn_ref[...]

    sd = jax.ShapeDtypeStruct
    return pl.pallas_call(
        body, name=name, grid=(t // tm,),
        in_specs=[_rows(tm, y_rnn.shape[1]), _rows(tm, o_att.shape[1]), _rows(tm, 2 * d),
                  _resident(w_br.shape), _resident(w_ba.shape), _resident(w_out.shape), _whole(g.shape), _rows(tm, d)],
        out_specs=[_rows(tm, d)] * 5,
        out_shape=[sd((t, d), BF16), sd((t, d), F32), sd((t, d), F32), sd((t, d), F32), sd((t, d), F32)],
        compiler_params=_params(("parallel",)),
    )(y_rnn, o_att, gts, w_br, w_ba, w_out, g, x)


def mid_bwd(dhn2, h1, g_ffn, dy, mix, g_mix, w_out, gts, br, ba, w_br, w_ba, merged, y_rnn, o_att, name, tm=256):
    t, d = h1.shape
    tm = _tile(t, tm)
    rnn, hkv = w_br.shape[0], w_ba.shape[0]
    wsh = d // N_CHIPS

    def body(dhn_ref, h_ref, gf_ref, dy_ref, mix_ref, gm_ref, wo_ref, g_ref, br_ref, ba_ref, wbr_ref, wba_ref,
             m_ref, y_ref, o_ref, dh_ref, dg_ref, dyr_ref, doa_ref, dgf_ref, dgm_ref, dwo_ref, dwbr_ref, dwba_ref,
             acc_o, acc_br, acc_ba):
        @pl.when(pl.program_id(0) == 0)
        def _():
            dgf_ref[...] = jnp.zeros(dgf_ref.shape, F32)
            dgm_ref[...] = jnp.zeros(dgm_ref.shape, F32)
            acc_o[...] = jnp.zeros(acc_o.shape, F32)
            acc_br[...] = jnp.zeros(acc_br.shape, F32)
            acc_ba[...] = jnp.zeros(acc_ba.shape, F32)

        du, rows_f = _rms_bwd(dhn_ref[...], h_ref[...], gf_ref[...])
        dh1 = du + dy_ref[...]
        dh_ref[...] = dh1
        dgf_ref[...] += jnp.sum(rows_f, axis=0, keepdims=True)
        dmx, rows_m = _rms_bwd(dh1, mix_ref[...], gm_ref[...])
        dmix = dmx.astype(BF16)
        acc_o[...] += _dot_tn(m_ref[...], dmix)
        dgm_ref[...] += jnp.sum(rows_m, axis=0, keepdims=True)
        dm = _dot_nt(dmix, wo_ref[...])
        gv = g_ref[...]
        sr = _sig(gv[:, :d])
        sa = _sig(gv[:, d:])
        dbr = (dm * sr).astype(BF16)
        dba = (dm * sa).astype(BF16)
        acc_br[...] += _dot_tn(y_ref[...].astype(BF16), dbr)
        acc_ba[...] += _dot_tn(o_ref[...].astype(BF16), dba)
        dg_ref[:, :d] = (dm * br_ref[...] * sr * (1.0 - sr)).astype(BF16)
        dg_ref[:, d:] = (dm * ba_ref[...] * sa * (1.0 - sa)).astype(BF16)
        dyr_ref[...] = _dot_nt(dbr, wbr_ref[...])
        doa_ref[...] = _dot_nt(dba, wba_ref[...])

        @pl.when(pl.program_id(0) == pl.num_programs(0) - 1)
        def _():
            dwo_ref[...] = acc_o[...].astype(BF16)
            dwbr_ref[...] = acc_br[...].astype(BF16)
            for j in range(N_CHIPS):
                dwba_ref[j] = acc_ba[:, j * wsh:(j + 1) * wsh].astype(BF16)

    sd = jax.ShapeDtypeStruct
    row, vec = _rows(tm, d), _whole((1, d))
    once = pl.Buffered(1)

    def resident(shape):
        return pl.BlockSpec(shape, lambda i: (0,) * len(shape), pipeline_mode=once)

    return pl.pallas_call(
        body, name=name, grid=(t // tm,),
        in_specs=[row, row, vec, row, row, vec, resident(w_out.shape), _rows(tm, 2 * d), row, row,
                  resident(w_br.shape), resident(w_ba.shape), row, _rows(tm, rnn), _rows(tm, hkv)],
        out_specs=[row, _rows(tm, 2 * d), _rows(tm, rnn), _rows(tm, hkv), vec, vec,
                   resident((d, d)), resident((rnn, d)), resident((N_CHIPS, hkv, wsh))],
        out_shape=[sd((t, d), F32), sd((t, 2 * d), BF16), sd((t, rnn), F32), sd((t, hkv), F32), sd((1, d), F32),
                   sd((1, d), F32), sd((d, d), BF16), sd((rnn, d), BF16), sd((N_CHIPS, hkv, wsh), BF16)],
        scratch_shapes=[pltpu.VMEM((d, d), F32), pltpu.VMEM((rnn, d), F32), pltpu.VMEM((hkv, d), F32)],
        compiler_params=_params(("arbitrary",)),
    )(dhn2, h1, g_ffn, dy, mix, g_mix, w_out, gts, br, ba, w_br, w_ba, merged, y_rnn, o_att)


def _shift_dn(x, d, fill, row):
    return jnp.where(row >= d, pltpu.roll(x, d, 0), fill)


def _shift_up(x, d, fill, row):
    s = x.shape[0]
    return jnp.where(row < s - d, pltpu.roll(x, s - d, 0), fill)


def _conv_fwd(x, w, b, row):
    kk = w.shape[0]
    y = b + w[kk - 1:kk, :] * x
    for j in range(1, kk):
        y = y + w[kk - 1 - j:kk - j, :] * _shift_dn(x, j, 0.0, row)
    return y


def _conv_bwd(dy, x, w, row):
    kk = w.shape[0]
    dx = w[kk - 1:kk, :] * dy
    dws = [None] * kk
    dws[kk - 1] = jnp.sum(dy * x, axis=0, keepdims=True)
    for j in range(1, kk):
        ahead = _shift_up(dy, j, 0.0, row)
        dx = dx + w[kk - 1 - j:kk - j, :] * ahead
        dws[kk - 1 - j] = jnp.sum(ahead * x, axis=0, keepdims=True)
    return dx, jnp.concatenate(dws, axis=0)


def _softplus(z):
    y = jnp.exp(-jnp.abs(z))
    u = 1.0 + y
    dd = u - 1.0
    log1p = jnp.where(dd == 0.0, y, jnp.log(u) * (y / jnp.where(dd == 0.0, 1.0, dd)))
    return jnp.maximum(z, 0.0) + log1p


def _lru_decay(xb, wa, ba, lam):
    r = _sig(_dot(xb, wa) + ba)
    sp = _softplus(-lam)
    la = (-LRU_C) * r * sp
    return r, sp, la, jnp.exp(la)


def _lru_gates(xc, wa, ba, wx, bx, lam):
    xb = xc.astype(BF16)
    r, sp, la, a = _lru_decay(xb, wa, ba, lam)
    i = _sig(_dot(xb, wx) + bx)
    one_m_a2 = jnp.tanh(-la) * (1.0 + a * a)
    inv_mult = lax.rsqrt(one_m_a2)
    return r, i, sp, a, one_m_a2 * inv_mult, inv_mult


def _seg_len(s):
    seg = -(-s // 8)
    return seg + (4 - seg % 8) % 8


def _scan_rows(a_pad, u_pad, out_pad, reverse):
    planes, rows8, lanes = a_pad.shape
    seg = rows8 // 8
    sub = lax.broadcasted_iota(jnp.int32, (planes, 8, lanes), 1)

    unroll = 4

    def rows(k, d):
        i = k * unroll + d
        return pl.ds((seg - 1 - i) if reverse else i, 8, stride=seg)

    def ends(k, carry):
        h, p = carry
        for d in range(unroll):
            a = a_pad[:, rows(k, d), :]
            h = a * h + u_pad[:, rows(k, d), :]
            p = a * p
        return h, p

    init = (jnp.zeros((planes, 8, lanes), F32), jnp.ones((planes, 8, lanes), F32))
    h_end, p_end = lax.fori_loop(0, seg // unroll, ends, init)
    start = jnp.zeros((planes, 8, lanes), F32)
    for _ in range(7):
        nxt = h_end + p_end * start
        if reverse:
            start = jnp.where(sub < 7, pltpu.roll(nxt, 7, 1), 0.0)
        else:
            start = jnp.where(sub >= 1, pltpu.roll(nxt, 1, 1), 0.0)

    def redo(k, h):
        for d in range(unroll):
            h = a_pad[:, rows(k, d), :] * h + u_pad[:, rows(k, d), :]
            out_pad[:, rows(k, d), :] = h
        return h

    lax.fori_loop(0, seg // unroll, redo, start)


def _lru_cols(c, rb):
    return 2 * rb if c % (2 * rb) == 0 else rb


def rglru_fwd(xr, cw, cb, wa, ba, wx, bx, lam, name, ride=()):
    b, s, c = xr.shape
    rb = wa.shape[1]
    kk = cw.shape[0]
    cols = _lru_cols(c, rb)
    nj = cols // rb
    seg = _seg_len(s)

    def body(x_ref, cw_ref, cb_ref, wa_ref, ba_ref, wx_ref, bx_ref, lam_ref, h_ref, a_ref, xc_ref, a_pad, u_pad, h_pad):
        row = lax.broadcasted_iota(jnp.int32, (s, rb), 0)
        for j in range(nj):
            cs = slice(j * rb, (j + 1) * rb)
            xc = _conv_fwd(x_ref[:, cs], cw_ref[:, cs], cb_ref[:, cs], row)
            _, i, _, a, mult, _ = _lru_gates(xc, wa_ref[j], ba_ref[:, cs], wx_ref[j], bx_ref[:, cs], lam_ref[:, cs])
            xc_ref[:, cs] = xc
            a_ref[:, cs] = a
            a_pad[j, 0:s, :] = a
            u_pad[j, 0:s, :] = mult * (i * xc)
        a_pad[:, s:, :] = jnp.ones((nj, 8 * seg - s, rb), F32)
        u_pad[:, s:, :] = jnp.zeros((nj, 8 * seg - s, rb), F32)
        _scan_rows(a_pad, u_pad, h_pad, False)
        for j in range(nj):
            h_ref[:, j * rb:(j + 1) * rb] = h_pad[j, 0:s, :]

    vec = pl.BlockSpec((1, cols), lambda bi, n: (0, n))
    seq = pl.BlockSpec((None, s, cols), lambda bi, n: (bi, 0, n))
    mat = pl.BlockSpec((nj, rb, rb), lambda bi, n: (n, 0, 0))
    r_ins, r_in_specs, r_outs, r_out_specs, r_sems = _ride_args(ride)
    outs = pl.pallas_call(
        _riding(body, 8, 3, 3, ride, 2), name=name, grid=(b, c // cols),
        in_specs=[seq, pl.BlockSpec((kk, cols), lambda bi, n: (0, n)), vec, mat, vec, mat, vec, vec] + r_in_specs,
        out_specs=[seq] * 3 + r_out_specs,
        out_shape=[jax.ShapeDtypeStruct((b, s, c), F32)] * 3 + r_outs,
        scratch_shapes=[pltpu.VMEM((nj, 8 * seg, rb), F32)] * 3 + r_sems,
        compiler_params=_params(("arbitrary", "arbitrary")),
    )(xr, cw, cb, wa, ba, wx, bx, lam, *r_ins)
    return outs[:3], _ride_results(ride, outs[3:])


def rglru_bwd(xr, h, dh, a_fwd, xc_fwd, cw, wa, ba, wx, bx, lam, name, ride=()):
    b, s, c = xr.shape
    nb, rb = wa.shape[0], wa.shape[1]
    kk = cw.shape[0]
    cols = _lru_cols(c, rb)
    nj = cols // rb
    seg = _seg_len(s)

    def body(x_ref, h_ref, dh_ref, a_ref, xc_ref, cw_ref, wa_ref, ba_ref, wx_ref, bx_ref, lam_ref,
             dx_ref, dcw_ref, dcb_ref, dwa_ref, dba_ref, dwx_ref, dbx_ref, dlam_ref, b_pad, g_pad, l_pad):
        @pl.when(pl.program_id(1) == 0)
        def _():
            for ref in (dcw_ref, dcb_ref, dwa_ref, dba_ref, dwx_ref, dbx_ref, dlam_ref):
                ref[...] = jnp.zeros(ref.shape, F32)

        row = lax.broadcasted_iota(jnp.int32, (s, rb), 0)

        for j in range(nj):
            b_pad[j, 0:s, :] = _shift_up(a_ref[:, j * rb:(j + 1) * rb], 1, 0.0, row)
            g_pad[j, 0:s, :] = dh_ref[:, j * rb:(j + 1) * rb]
        b_pad[:, s:, :] = jnp.zeros((nj, 8 * seg - s, rb), F32)
        g_pad[:, s:, :] = jnp.zeros((nj, 8 * seg - s, rb), F32)
        _scan_rows(b_pad, g_pad, l_pad, True)

        for j in range(nj):
            cs = slice(j * rb, (j + 1) * rb)
            x = x_ref[:, cs]
            cwv = cw_ref[:, cs]
            wav, wxv, lamv = wa_ref[j], wx_ref[j], lam_ref[:, cs]
            xc = xc_ref[:, cs]
            r, i, sp, a, mult, inv_mult = _lru_gates(xc, wav, ba_ref[:, cs], wxv, bx_ref[:, cs], lamv)
            lmb = l_pad[j, 0:s, :]
            h_prev = _shift_dn(h_ref[:, cs], 1, 0.0, row)
            da = lmb * h_prev
            ixc = i * xc
            dla = da * a - (lmb * ixc) * (a * a) * inv_mult
            di = lmb * mult * xc
            dxc = lmb * mult * i
            dr = dla * ((-LRU_C) * sp)
            dsp = jnp.sum(dla * ((-LRU_C) * r), axis=0, keepdims=True)
            dga = dr * r * (1.0 - r)
            dgx = di * i * (1.0 - i)
            dga_b, dgx_b = dga.astype(BF16), dgx.astype(BF16)
            xb = xc.astype(BF16)
            dwa_ref[j] += _dot_tn(xb, dga_b)
            dwx_ref[j] += _dot_tn(xb, dgx_b)
            dba_ref[:, cs] += jnp.sum(dga, axis=0, keepdims=True)
            dbx_ref[:, cs] += jnp.sum(dgx, axis=0, keepdims=True)
            dlam_ref[:, cs] += dsp * (-_sig(-lamv))
            dxc = dxc + _dot_nt(dga_b, wav) + _dot_nt(dgx_b, wxv)
            dcb_ref[:, cs] += jnp.sum(dxc, axis=0, keepdims=True)
            dx, dcw = _conv_bwd(dxc, x, cwv, row)
            dcw_ref[:, cs] += dcw
            dx_ref[:, cs] = dx.astype(dx_ref.dtype)

    vec = pl.BlockSpec((1, cols), lambda n, bi: (0, n))
    seq = pl.BlockSpec((None, s, cols), lambda n, bi: (bi, 0, n))
    mat = pl.BlockSpec((nj, rb, rb), lambda n, bi: (n, 0, 0))
    cws = pl.BlockSpec((kk, cols), lambda n, bi: (0, n))
    sd = jax.ShapeDtypeStruct
    r_ins, r_in_specs, r_outs, r_out_specs, r_sems = _ride_args(ride)
    outs = pl.pallas_call(
        _riding(body, 11, 8, 3, ride, 2), name=name, grid=(c // cols, b),
        in_specs=[seq, seq, seq, seq, seq, cws, mat, vec, mat, vec, vec] + r_in_specs,
        out_specs=[seq, cws, vec, mat, vec, mat, vec, vec] + r_out_specs,
        out_shape=[sd((b, s, c), BF16), sd((kk, c), F32), sd((1, c), F32), sd((nb, rb, rb), F32),
                   sd((1, c), F32), sd((nb, rb, rb), F32), sd((1, c), F32), sd((1, c), F32)] + r_outs,
        scratch_shapes=[pltpu.VMEM((nj, 8 * seg, rb), F32)] * 3 + r_sems,
        compiler_params=_params(("arbitrary", "arbitrary")),
    )(xr, h, dh, a_fwd, xc_fwd, cw, wa, ba, wx, bx, lam, *r_ins)
    return outs[:8], _ride_results(ride, outs[8:])


_GELU_C = math.sqrt(2.0 / math.pi)


def _gelu_parts(x):
    th = jnp.tanh(_GELU_C * (x + 0.044715 * x * x * x))
    gel = 0.5 * x * (1.0 + th)
    dgel = 0.5 * (1.0 + th) + 0.5 * x * (1.0 - th * th) * _GELU_C * (1.0 + 3 * 0.044715 * x * x)
    return gel, dgel


def ffn_in_act(x, g, wg, wu, cw, cb, seq_len, name, tm=256):
    t, d = x.shape
    f = N_CHIPS * wg.shape[2]
    kk = cw.shape[0]
    tm = _tile(seq_len, tm)
    tiles_per_seq = seq_len // tm
    keep = 8
    assert kk - 1 <= keep

    def body(x_ref, g_ref, wg_ref, wu_ref, cw_ref, cb_ref, hn_ref, gp_ref, up_ref, act_ref, tail):
        @pl.when(pl.program_id(0) % tiles_per_seq == 0)
        def _():
            tail[...] = jnp.zeros(tail.shape, F32)

        xv = x_ref[...]
        inv = lax.rsqrt(jnp.mean(xv * xv, axis=-1, keepdims=True) + EPS)
        hn = (xv * inv * g_ref[...]).astype(BF16)
        hn_ref[...] = hn
        gp = jnp.concatenate([_dot(hn, wg_ref[j]) for j in range(N_CHIPS)], axis=1)
        up = jnp.concatenate([_dot(hn, wu_ref[j]) for j in range(N_CHIPS)], axis=1)
        gp_ref[...] = gp
        up_ref[...] = up
        cwv = cw_ref[...]
        row = lax.broadcasted_iota(jnp.int32, (tm, 1), 0)
        gate = _conv_fwd(gp, cwv, cb_ref[...], row)
        row8 = lax.broadcasted_iota(jnp.int32, (keep, 1), 0)
        prev = tail[...]
        fix = jnp.zeros((keep, f), F32)
        for j in range(1, kk):
            fix = fix + cwv[kk - 1 - j:kk - j, :] * jnp.where(row8 < j, pltpu.roll(prev, j, 0), 0.0)
        gate = jnp.concatenate([gate[:keep] + fix, gate[keep:]], axis=0)
        tail[...] = gp[tm - keep:, :]
        gel, _ = _gelu_parts(gate)
        act_ref[...] = (gel * up).astype(BF16)

    sd = jax.ShapeDtypeStruct
    return pl.pallas_call(
        body, name=name, grid=(t // tm,),
        in_specs=[_rows(tm, d), _whole(g.shape), _whole(wg.shape), _whole(wu.shape), _whole(cw.shape), _whole(cb.shape)],
        out_specs=[_rows(tm, d), _rows(tm, f), _rows(tm, f), _rows(tm, f)],
        out_shape=[sd((t, d), BF16), sd((t, f), F32), sd((t, f), F32), sd((t, f), BF16)],
        scratch_shapes=[pltpu.VMEM((keep, f), F32)],
        compiler_params=_params(("arbitrary",)),
    )(x, g, wg, wu, cw, cb)


def ffn_in_bwd(dact, gate_pre, up, cw, cb, wg, wu, seq_len, name, ride=(), tm=256):
    t, f = gate_pre.shape
    d = wg.shape[1]
    fs = f // N_CHIPS
    kk = cw.shape[0]
    tm = _tile(seq_len, tm)
    nt = t // tm
    tiles_per_seq = seq_len // tm
    keep = 8
    assert kk - 1 <= keep

    def body(da_ref, g_ref, halo_ref, u_ref, cw_ref, cb_ref, wg_ref, wu_ref,
             dg_ref, du_ref, dhn_ref, dcw_ref, dcb_ref, nxt):
        tile = (nt - 1 - pl.program_id(0)) % tiles_per_seq

        @pl.when(pl.program_id(0) == 0)
        def _():
            dcw_ref[...] = jnp.zeros(dcw_ref.shape, F32)
            dcb_ref[...] = jnp.zeros(dcb_ref.shape, F32)

        @pl.when(tile == tiles_per_seq - 1)
        def _():
            nxt[...] = jnp.zeros(nxt.shape, F32)

        row = lax.broadcasted_iota(jnp.int32, (tm, 1), 0)
        row8 = lax.broadcasted_iota(jnp.int32, (keep, 1), 0)
        gp = g_ref[...]
        cwv = cw_ref[...]
        prev = jnp.where(tile > 0, halo_ref[...], 0.0)
        gate = _conv_fwd(gp, cwv, cb_ref[...], row)
        fix = jnp.zeros((keep, f), F32)
        for j in range(1, kk):
            fix = fix + cwv[kk - 1 - j:kk - j, :] * jnp.where(row8 < j, pltpu.roll(prev, j, 0), 0.0)
        gate = jnp.concatenate([gate[:keep] + fix, gate[keep:]], axis=0)
        gel, dgel = _gelu_parts(gate)
        da = da_ref[...]
        dup = (da * gel).astype(BF16)
        du_ref[...] = dup
        dgate = da * u_ref[...] * dgel
        dcb_ref[...] += jnp.sum(dgate, axis=0, keepdims=True)
        after = nxt[...]
        dgp = cwv[kk - 1:kk, :] * dgate
        tail_fix = jnp.zeros((keep, f), F32)
        dws = [None] * kk
        dws[kk - 1] = jnp.sum(dgate * gp, axis=0, keepdims=True)
        for j in range(1, kk):
            wj = cwv[kk - 1 - j:kk - j, :]
            dgp = dgp + wj * _shift_up(dgate, j, 0.0, row)
            tail_fix = tail_fix + wj * jnp.where(row8 >= keep - j, pltpu.roll(after, keep - j, 0), 0.0)
            dws[kk - 1 - j] = (jnp.sum(dgate * _shift_dn(gp, j, 0.0, row), axis=0, keepdims=True)
                               + jnp.sum(dgate[:keep] * jnp.where(row8 < j, pltpu.roll(prev, j, 0), 0.0),
                                         axis=0, keepdims=True))
        dgp = jnp.concatenate([dgp[:tm - keep], dgp[tm - keep:] + tail_fix], axis=0).astype(BF16)
        nxt[...] = dgate[:keep]
        dcw_ref[...] += jnp.concatenate(dws, axis=0)
        dg_ref[...] = dgp
        dhn = None
        for j in range(N_CHIPS):
            cs = slice(j * fs, (j + 1) * fs)
            part = _dot_nt(dgp[:, cs], wg_ref[j]) + _dot_nt(dup[:, cs], wu_ref[j])
            dhn = part if dhn is None else dhn + part
        dhn_ref[...] = dhn

    def rev(i):
        return nt - 1 - i

    rows_f = pl.BlockSpec((tm, f), lambda i: (rev(i), 0))
    halo = pl.BlockSpec((None, keep, f), lambda i: (jnp.maximum(rev(i) * (tm // keep) - 1, 0), 0, 0))
    once = pl.Buffered(1)
    sd = jax.ShapeDtypeStruct
    r_ins, r_in_specs, r_outs, r_out_specs, r_sems = _ride_args(ride)
    outs = pl.pallas_call(
        _riding(body, 8, 5, 1, ride, 1), name=name, grid=(nt,),
        in_specs=[rows_f, rows_f, halo, rows_f, _whole(cw.shape), _whole(cb.shape),
                  pl.BlockSpec(wg.shape, lambda i: (0, 0, 0), pipeline_mode=once),
                  pl.BlockSpec(wu.shape, lambda i: (0, 0, 0), pipeline_mode=once)] + r_in_specs,
        out_specs=[rows_f, rows_f, pl.BlockSpec((tm, d), lambda i: (rev(i), 0)), _whole((kk, f)), _whole((1, f))]
        + r_out_specs,
        out_shape=[sd((t, f), BF16), sd((t, f), BF16), sd((t, d), F32), sd((kk, f), F32), sd((1, f), F32)] + r_outs,
        scratch_shapes=[pltpu.VMEM((keep, f), F32)] + r_sems,
        compiler_params=_params(("arbitrary",)),
    )(dact, gate_pre, gate_pre.reshape(t // keep, keep, f), up, cw, cb, wg, wu, *r_ins)
    return outs[:5], _ride_results(ride, outs[5:])


def _t5_bucket(dist):
    max_exact = REL_BUCKETS // 2
    d = np.maximum(dist, 1).astype(np.float32)
    large = max_exact + np.log(d / max_exact) / math.log(REL_MAX_DIST / max_exact) * (REL_BUCKETS - max_exact)
    large = np.minimum(large.astype(np.int32), REL_BUCKETS - 1)
    return np.where(dist < max_exact, dist, large).astype(np.int32)


def _band(window, dilation):
    qi = np.arange(ATTN_BLOCK)[:, None]
    kj = np.arange(2 * ATTN_BLOCK)[None, :]
    delta = ATTN_BLOCK + qi - kj
    mask = (delta >= 0) & (delta <= window // dilation)
    bucket = _t5_bucket(np.maximum(delta, 0) * dilation)
    return mask, bucket


def _attn_blocks(s, r):
    m = s // r
    assert m % ATTN_BLOCK == 0, "sequence length must be a multiple of dilation * block"
    return m // ATTN_BLOCK


def _perm_load(ref, r):
    if r == 1:
        return ref[...]
    m = ref.shape[0] // r
    return jnp.concatenate([ref[pl.ds(c, m, stride=r), :] for c in range(r)], axis=0)


def _perm_store(ref, g, val, r, add=False):
    if r == 1:
        ref[g] = ref[g] + val if add else val
        return
    m = val.shape[0] // r
    for c in range(r):
        rows = pl.ds(c, m, stride=r)
        part = val[c * m:(c + 1) * m]
        ref[g, rows, :] = ref[g, rows, :] + part if add else part


def _blocks(x):
    return x.reshape(x.shape[0] // ATTN_BLOCK, ATTN_BLOCK, x.shape[1])


def _prev_blocks(x):
    return jnp.concatenate([x[:1], x[:-1]], axis=0)


def _next_blocks(x):
    return jnp.concatenate([x[1:], jnp.zeros_like(x[:1])], axis=0)


def _first_block_neg(s, r):
    nblk = s // ATTN_BLOCK
    idx = lax.broadcasted_iota(jnp.int32, (nblk, 1, 1), 0)
    return jnp.where(idx % _attn_blocks(s, r) == 0, NEG, 0.0)


def _bdot_nt(a, b):
    return lax.dot_general(a, b, (((2,), (2,)), ((0,), (0,))), preferred_element_type=F32)


def _bdot(a, b):
    return lax.dot_general(a, b, (((2,), (1,)), ((0,), (0,))), preferred_element_type=F32)


def _bdot_tn(a, b):
    return lax.dot_general(a, b, (((1,), (1,)), ((0,), (0,))), preferred_element_type=F32)


def attn_fwd(qkv, biasm, n_heads, name, ride=()):
    b, s, _ = qkv.shape
    h = n_heads
    scale = HEAD_DIM ** -0.5
    blk = ATTN_BLOCK

    def body(q1_ref, q2_ref, q3_ref, k_ref, v_ref, bias_ref, o_ref, lse_ref, acc, m_s, l_s):
        for g, q_ref in enumerate((q1_ref, q2_ref, q3_ref)):
            r = DILATED[g][1]
            first = _first_block_neg(s, r)
            q = _blocks(_perm_load(q_ref, r).astype(BF16))
            k = _blocks(_perm_load(k_ref, r).astype(BF16))
            v = _blocks(_perm_load(v_ref, r).astype(BF16))
            s_cur = _bdot_nt(q, k) * scale + bias_ref[g, :, blk:]
            s_prev = _bdot_nt(q, _prev_blocks(k)) * scale + bias_ref[g, :, :blk] + first
            m = jnp.max(jnp.maximum(s_cur, s_prev), axis=-1, keepdims=True)
            p_cur = jnp.exp(s_cur - m)
            p_prev = jnp.exp(s_prev - m)
            l = jnp.sum(p_cur + p_prev, axis=-1, keepdims=True)
            o = _bdot(p_cur.astype(BF16), v) + _bdot(p_prev.astype(BF16), _prev_blocks(v))
            _perm_store(acc, g, o.reshape(s, HEAD_DIM), r)
            _perm_store(m_s, g, m.reshape(s, 1), r)
            _perm_store(l_s, g, l.reshape(s, 1), r)
        m_all = jnp.maximum(jnp.maximum(m_s[0], m_s[1]), m_s[2])
        w = [jnp.exp(m_s[g] - m_all) for g in range(N_GROUPS)]
        l = w[0] * l_s[0] + w[1] * l_s[1] + w[2] * l_s[2]
        o_ref[...] = (w[0] * acc[0] + w[1] * acc[1] + w[2] * acc[2]) / l
        lse_ref[...] = m_all + jnp.log(l)

    def col(j):
        return pl.BlockSpec((None, s, HEAD_DIM), lambda bi, hi, j=j: (bi, 0, j * h + hi))

    r_ins, r_in_specs, r_outs, r_out_specs, r_sems = _ride_args(ride)
    outs = pl.pallas_call(
        _riding(body, 6, 2, 3, ride, 2), name=name, grid=(b, h),
        in_specs=[col(0), col(1), col(2), col(3), col(4),
                  pl.BlockSpec((N_GROUPS, None, blk, 2 * blk), lambda bi, hi: (0, hi, 0, 0))] + r_in_specs,
        out_specs=[pl.BlockSpec((None, s, HEAD_DIM), lambda bi, hi: (bi, 0, hi)),
                   pl.BlockSpec((None, None, s, 1), lambda bi, hi: (bi, hi, 0, 0))] + r_out_specs,
        out_shape=[jax.ShapeDtypeStruct((b, s, h * HEAD_DIM), F32), jax.ShapeDtypeStruct((b, h, s, 1), F32)] + r_outs,
        scratch_shapes=[pltpu.VMEM((N_GROUPS, s, HEAD_DIM), F32), pltpu.VMEM((N_GROUPS, s, 1), F32),
                        pltpu.VMEM((N_GROUPS, s, 1), F32)] + r_sems,
        compiler_params=_params(("arbitrary", "arbitrary")),
    )(qkv, qkv, qkv, qkv, qkv, biasm, *r_ins)
    return outs[0], outs[1], _ride_results(ride, outs[2:])


def attn_bwd(qkv, biasm, o, lse, do, n_heads, name, ride=()):
    b, s, _ = qkv.shape
    h = n_heads
    scale = HEAD_DIM ** -0.5
    blk = ATTN_BLOCK

    def body(q1_ref, q2_ref, q3_ref, k_ref, v_ref, bias_ref, o_ref, lse_ref, do_ref,
             dq1_ref, dq2_ref, dq3_ref, dk_ref, dv_ref, ds_ref, dq_acc, kv_acc, delta):
        delta[...] = jnp.sum(do_ref[...] * o_ref[...], axis=-1, keepdims=True)
        kv_acc[...] = jnp.zeros(kv_acc.shape, F32)
        for g, q_ref in enumerate((q1_ref, q2_ref, q3_ref)):
            r = DILATED[g][1]
            first = _first_block_neg(s, r)
            q = _blocks(_perm_load(q_ref, r).astype(BF16))
            k = _blocks(_perm_load(k_ref, r).astype(BF16))
            v = _blocks(_perm_load(v_ref, r).astype(BF16))
            dob = _blocks(_perm_load(do_ref, r).astype(BF16))
            lse_b = _blocks(_perm_load(lse_ref, r))
            dl_b = _blocks(_perm_load(delta, r))
            k_prev, v_prev = _prev_blocks(k), _prev_blocks(v)
            p_cur = jnp.exp(_bdot_nt(q, k) * scale + bias_ref[g, :, blk:] - lse_b)
            p_prev = jnp.exp(_bdot_nt(q, k_prev) * scale + bias_ref[g, :, :blk] + first - lse_b)
            ds_cur = p_cur * (_bdot_nt(dob, v) - dl_b)
            ds_prev = p_prev * (_bdot_nt(dob, v_prev) - dl_b)
            ds_ref[g, :, blk:] = jnp.sum(ds_cur, axis=0)
            ds_ref[g, :, :blk] = jnp.sum(ds_prev, axis=0)
            ds_cur_b, ds_prev_b = ds_cur.astype(BF16), ds_prev.astype(BF16)
            dq = (_bdot(ds_cur_b, k) + _bdot(ds_prev_b, k_prev)) * scale
            _perm_store(dq_acc, g, dq.reshape(s, HEAD_DIM), r)
            dk = (_bdot_tn(ds_cur_b, q) + _next_blocks(_bdot_tn(ds_prev_b, q))) * scale
            dv = _bdot_tn(p_cur.astype(BF16), dob) + _next_blocks(_bdot_tn(p_prev.astype(BF16), dob))
            _perm_store(kv_acc, 0, dk.reshape(s, HEAD_DIM), r, add=True)
            _perm_store(kv_acc, 1, dv.reshape(s, HEAD_DIM), r, add=True)
        for g, out_ref in enumerate((dq1_ref, dq2_ref, dq3_ref)):
            out_ref[...] = dq_acc[g].astype(out_ref.dtype)
        dk_ref[...] = kv_acc[0].astype(dk_ref.dtype)
        dv_ref[...] = kv_acc[1].astype(dv_ref.dtype)

    def col(j):
        return pl.BlockSpec((None, s, HEAD_DIM), lambda bi, hi, j=j: (bi, 0, j * h + hi))

    head = pl.BlockSpec((None, s, HEAD_DIM), lambda bi, hi: (bi, 0, hi))
    sd = jax.ShapeDtypeStruct
    r_ins, r_in_specs, r_outs, r_out_specs, r_sems = _ride_args(ride)
    outs = pl.pallas_call(
        _riding(body, 9, 6, 3, ride, 2), name=name, grid=(b, h),
        in_specs=[col(0), col(1), col(2), col(3), col(4),
                  pl.BlockSpec((N_GROUPS, None, blk, 2 * blk), lambda bi, hi: (0, hi, 0, 0)),
                  head, pl.BlockSpec((None, None, s, 1), lambda bi, hi: (bi, hi, 0, 0)), head] + r_in_specs,
        out_specs=[head] * 5 + [pl.BlockSpec((None, None, N_GROUPS, blk, 2 * blk), lambda bi, hi: (bi, hi, 0, 0, 0))]
        + r_out_specs,
        out_shape=[sd((b, s, h * HEAD_DIM), BF16)] * 5 + [sd((b, h, N_GROUPS, blk, 2 * blk), F32)] + r_outs,
        scratch_shapes=[pltpu.VMEM((N_GROUPS, s, HEAD_DIM), F32), pltpu.VMEM((2, s, HEAD_DIM), F32),
                        pltpu.VMEM((s, 1), F32)] + r_sems,
        compiler_params=_params(("arbitrary", "arbitrary")),
    )(qkv, qkv, qkv, qkv, qkv, biasm, o, lse, do, *r_ins)
    return outs[:6], _ride_results(ride, outs[6:])


def bias_table(rel_rows, bucket_f, n_heads, name):
    g, blk, blk2 = bucket_f.shape
    h = n_heads

    def body(rb_ref, bk_ref, o_ref):
        for gi in range(g):
            bk = bk_ref[gi]
            for hi in range(h):
                rb = rb_ref[gi * h + hi]
                acc = jnp.full((blk, blk2), NEG, F32)
                for bucket in range(REL_BUCKETS):
                    acc = jnp.where(bk == float(bucket), rb[:, bucket:bucket + 1], acc)
                o_ref[gi, hi] = acc

    vmem = pl.BlockSpec(memory_space=pltpu.VMEM)
    return pl.pallas_call(
        body, name=name, in_specs=[vmem, vmem], out_specs=vmem,
        out_shape=jax.ShapeDtypeStruct((g, h, blk, blk2), F32),
        compiler_params=_params(),
    )(rel_rows, bucket_f)


def bias_grad(ds_sum, bucket_f, name):
    b, h, g, blk, blk2 = ds_sum.shape

    def body(ds_ref, bk_ref, o_ref):
        lane = lax.broadcasted_iota(jnp.int32, (1, 128), 1)
        for gi in range(g):
            bk = bk_ref[gi]
            for hi in range(h):
                tot = ds_ref[0, hi, gi]
                for bi in range(1, b):
                    tot = tot + ds_ref[bi, hi, gi]
                vec = jnp.zeros((1, 128), F32)
                for bucket in range(REL_BUCKETS):
                    val = jnp.sum(jnp.where(bk == float(bucket), tot, 0.0), keepdims=True)
                    vec = vec + jnp.where(lane == bucket, val, 0.0)
                o_ref[gi * h + hi] = vec

    vmem = pl.BlockSpec(memory_space=pltpu.VMEM)
    return pl.pallas_call(
        body, name=name, in_specs=[vmem, vmem], out_specs=vmem,
        out_shape=jax.ShapeDtypeStruct((g * h, 1, 128), F32),
        compiler_params=_params(),
    )(ds_sum, bucket_f)


def _chip_peers():
    x, y, c = lax.axis_index("x"), lax.axis_index("y"), lax.axis_index("c")
    me = 2 * x + y
    peers = [(1 - x, y, c), (x, 1 - y, c), (1 - x, 1 - y, c)]
    peer_chip = [2 * (1 - x) + y, 2 * x + (1 - y), 2 * (1 - x) + (1 - y)]
    return me, peers, peer_chip


def _any_specs(n):
    return [pl.BlockSpec(memory_space=pl.ANY)] * n


_MID_NUM, _MID_DEN = 3, 4


class _Exchange:
    def start(self, ins, outs, sems):
        local, sends, _ = self._copies(ins, outs, sems)
        for cp in local + sends:
            cp.start()

    def mid(self, ins, outs, sems):
        pass

    def wait(self, ins, outs, sems):
        local, sends, recvs = self._copies(ins, outs, sems)
        for cp in recvs():
            cp.wait_recv()
        for cp in sends:
            cp.wait_send()
        for cp in local:
            cp.wait()


class _Gather(_Exchange):
    HALF_ROWS = 16

    def __init__(self, arrays):
        n = len(arrays)
        self.ins = list(arrays)
        self.split = [a.shape[0] % (2 * self.HALF_ROWS) == 0 for a in arrays]
        self.out_shape = [jax.ShapeDtypeStruct((N_CHIPS,) + a.shape, a.dtype) for a in arrays]
        dma = pltpu.SemaphoreType.DMA
        self.sems = [dma((3 * n,)), dma((3 * n,)), dma((n,)), dma((3 * n,)), dma((3 * n,))]

    def _half(self, i, ref, sibling=False):
        if not self.split[i]:
            return ref
        half = self.ins[i].shape[0] // 2
        c = lax.axis_index("c")
        c = 1 - c if sibling else c
        return ref.at[pl.ds(pl.multiple_of(c * half, self.HALF_ROWS), half)]

    def _plan(self, ins, outs, sems):
        send1, recv1, local_sems, send2, recv2 = sems
        me, peers, peer_chip = _chip_peers()
        x, y, c = lax.axis_index("x"), lax.axis_index("y"), lax.axis_index("c")
        n = len(ins)
        pairs = [(i, k) for k in range(3) for i in range(n)]

        def fetch(i, k, slot):
            return pltpu.make_async_remote_copy(src_ref=self._half(i, ins[i]), dst_ref=self._half(i, outs[i].at[slot]),
                                                send_sem=send1.at[3 * i + k], recv_sem=recv1.at[3 * i + k],
                                                device_id=peers[k], device_id_type=MESH)

        def share(i, k, sibling):
            part = self._half(i, outs[i].at[peer_chip[k]], sibling)
            return pltpu.make_async_remote_copy(src_ref=part, dst_ref=part, send_sem=send2.at[3 * i + k],
                                                recv_sem=recv2.at[3 * i + k], device_id=(x, y, 1 - c),
                                                device_id_type=MESH)

        split_pairs = [(i, k) for i, k in pairs if self.split[i]]
        return dict(
            local=lambda: [pltpu.make_async_copy(ins[i], outs[i].at[me], local_sems.at[i]) for i in range(n)],
            fetch_out=lambda: [fetch(i, k, me) for i, k in pairs],
            fetch_in=lambda: [(fetch(i, k, peer_chip[k]), share(i, k, False) if self.split[i] else None)
                              for i, k in pairs],
            share_out=lambda: [share(i, k, False) for i, k in split_pairs],
            share_in=lambda: [share(i, k, True) for i, k in split_pairs])

    def start(self, ins, outs, sems):
        plan = self._plan(ins, outs, sems)
        for cp in plan["local"]() + plan["fetch_out"]():
            cp.start()

    def mid(self, ins, outs, sems):
        plan = self._plan(ins, outs, sems)
        for arrived, forward in plan["fetch_in"]():
            arrived.wait_recv()
            if forward is not None:
                forward.start()

    def wait(self, ins, outs, sems):
        plan = self._plan(ins, outs, sems)
        for cp in plan["share_in"]():
            cp.wait_recv()
        for cp in plan["fetch_out"]() + plan["share_out"]():
            cp.wait_send()
        for cp in plan["local"]():
            cp.wait()


class _Scatter(_Exchange):
    def __init__(self, slabs, whole=()):
        self.n_slabs = len(slabs)
        self.ins = list(slabs) + list(whole)
        n = len(self.ins)
        self.out_shape = [jax.ShapeDtypeStruct(a.shape, a.dtype) for a in slabs] \
            + [jax.ShapeDtypeStruct((N_CHIPS,) + a.shape, a.dtype) for a in whole]
        self.sems = [pltpu.SemaphoreType.DMA((3 * n,)), pltpu.SemaphoreType.DMA((3 * n,)), pltpu.SemaphoreType.DMA((n,))]

    def _copies(self, ins, outs, sems):
        send_sems, recv_sems, local_sems = sems
        me, peers, peer_chip = _chip_peers()
        n = len(ins)

        def src(i, chip):
            return ins[i].at[chip] if i < self.n_slabs else ins[i]

        def remote(i, k, src_chip, slot):
            return pltpu.make_async_remote_copy(src_ref=src(i, src_chip), dst_ref=outs[i].at[slot],
                                                send_sem=send_sems.at[3 * i + k], recv_sem=recv_sems.at[3 * i + k],
                                                device_id=peers[k], device_id_type=MESH)

        local = [pltpu.make_async_copy(src(i, me), outs[i].at[me], local_sems.at[i]) for i in range(n)]
        sends = [remote(i, k, peer_chip[k], me) for i in range(n) for k in range(3)]
        return local, sends, lambda: [remote(i, k, me, peer_chip[k]) for i in range(n) for k in range(3)]


class _Swap(_Exchange):
    def __init__(self, arrays):
        n = len(arrays)
        self.ins = list(arrays)
        self.out_shape = [jax.ShapeDtypeStruct(a.shape, a.dtype) for a in arrays]
        self.sems = [pltpu.SemaphoreType.DMA((n,)), pltpu.SemaphoreType.DMA((n,))]

    def _copies(self, ins, outs, sems):
        send_sems, recv_sems = sems
        x, y, c = lax.axis_index("x"), lax.axis_index("y"), lax.axis_index("c")
        cps = [pltpu.make_async_remote_copy(src_ref=ins[i], dst_ref=outs[i], send_sem=send_sems.at[i],
                                            recv_sem=recv_sems.at[i], device_id=(x, y, 1 - c), device_id_type=MESH)
               for i in range(len(ins))]
        return [], cps, lambda: cps


def _riding(body, n_in, n_out, n_scratch, ride, rank):
    if not ride:
        return body
    r_in = sum(len(e.ins) for e in ride)
    r_out = sum(len(e.out_shape) for e in ride)

    def split(refs, sizes):
        out, a = [], 0
        for sz in sizes:
            out.append(refs[a:a + sz])
            a += sz
        return out

    def wrapped(*refs):
        a = 0
        parts = []
        for sz in (n_in, r_in, n_out, r_out, n_scratch):
            parts.append(refs[a:a + sz])
            a += sz
        own_in, ex_in, own_out, ex_out, own_scratch = parts
        ex_sems = refs[a:]
        ins = split(ex_in, [len(e.ins) for e in ride])
        outs = split(ex_out, [len(e.out_shape) for e in ride])
        sems = split(ex_sems, [len(e.sems) for e in ride])
        if rank:
            step, total = 0, 1
            for d in range(rank):
                step = step * pl.num_programs(d) + pl.program_id(d)
                total = total * pl.num_programs(d)

            @pl.when(step == 0)
            def _():
                for e, i, o, s in zip(ride, ins, outs, sems):
                    e.start(i, o, s)

            body(*own_in, *own_out, *own_scratch)

            @pl.when(step == (total * _MID_NUM) // _MID_DEN)
            def _():
                for e, i, o, s in zip(ride, ins, outs, sems):
                    e.mid(i, o, s)

            @pl.when(step == total - 1)
            def _():
                for e, i, o, s in zip(ride, ins, outs, sems):
                    e.wait(i, o, s)
        else:
            for phase in ("start", "mid", "wait"):
                for e, i, o, s in zip(ride, ins, outs, sems):
                    getattr(e, phase)(i, o, s)

    return wrapped


def _ride_args(ride):
    ins = [a for e in ride for a in e.ins]
    outs = [s for e in ride for s in e.out_shape]
    sems = [s for e in ride for s in e.sems]
    return ins, _any_specs(len(ins)), outs, _any_specs(len(outs)), sems


def _ride_results(ride, flat):
    out, a = [], 0
    for e in ride:
        out.append(list(flat[a:a + len(e.out_shape)]))
        a += len(e.out_shape)
    return out


def exchange(ride, name):
    ins, in_specs, outs, out_specs, sems = _ride_args(ride)
    res = pl.pallas_call(
        _riding(lambda: None, 0, 0, 0, ride, 0), name=name,
        in_specs=in_specs, out_specs=out_specs, out_shape=outs, scratch_shapes=sems,
    )(*ins)
    return _ride_results(ride, res)


def _sum_slots(ref):
    acc = ref[0].astype(F32)
    for j in range(1, ref.shape[0]):
        acc = acc + ref[j].astype(F32)
    return acc


def sum_pairs(mine, other, name, tr=176):
    n, r, w = mine.shape
    tr = r if r <= tr else _tile(r, tr)

    def body(a_ref, b_ref, o_ref):
        o_ref[...] = _sum_slots(a_ref) + _sum_slots(b_ref)

    spec = pl.BlockSpec((n, tr, w), lambda i: (0, i, 0))
    return pl.pallas_call(
        body, name=name, grid=(r // tr,),
        in_specs=[spec, spec], out_specs=_rows(tr, w),
        out_shape=jax.ShapeDtypeStruct((r, w), F32),
        compiler_params=_params(("parallel",)),
    )(mine, other)


def _adamw_update(w, m, v, g):
    c1 = 1.0 - ADAM_B1 ** ADAM_STEP
    c2 = 1.0 - ADAM_B2 ** ADAM_STEP
    nm = ADAM_B1 * m + (1.0 - ADAM_B1) * g
    nv = ADAM_B2 * v + (1.0 - ADAM_B2) * (g * g)
    return nm, nv, (-ADAM_LR) * ((nm / c1) / (jnp.sqrt(nv / c2) + ADAM_EPS) + ADAM_WD * w)


def adamw(w, m, v, mine, other, name, tr=256):
    r, c = w.shape
    tr = r if r % 8 else _tile(r, tr)

    def body(w_ref, m_ref, v_ref, a_ref, b_ref, g_ref, d_ref, nm_ref, nv_ref):
        g = _sum_slots(a_ref) + _sum_slots(b_ref)
        nm, nv, delta = _adamw_update(w_ref[...], m_ref[...], v_ref[...], g)
        g_ref[...] = g
        nm_ref[...] = nm
        nv_ref[...] = nv
        d_ref[...] = delta

    spec = _rows(tr, c)
    gspec = pl.BlockSpec((N_CHIPS, tr, c), lambda i: (0, i, 0))
    return pl.pallas_call(
        body, name=name, grid=(r // tr,),
        in_specs=[spec] * 3 + [gspec] * 2, out_specs=[spec] * 4,
        out_shape=[jax.ShapeDtypeStruct((r, c), F32)] * 4,
        compiler_params=_params(("parallel",)),
    )(w, m, v, mine, other)


def adamw_small(ws, ms, vs, gs, name):
    n = len(ws)

    def body(*refs):
        ins, outs = refs[:4 * n], refs[4 * n:]
        for i in range(n):
            w_ref, m_ref, v_ref, g_ref = ins[4 * i:4 * i + 4]
            d_ref, nm_ref, nv_ref = outs[3 * i:3 * i + 3]
            nm, nv, delta = _adamw_update(w_ref[...], m_ref[...], v_ref[...], g_ref[...])
            d_ref[...] = delta
            nm_ref[...] = nm
            nv_ref[...] = nv

    flat = [a for quad in zip(ws, ms, vs, gs) for a in quad]
    vmem = pl.BlockSpec(memory_space=pltpu.VMEM)
    outs = pl.pallas_call(
        body, name=name,
        in_specs=[vmem] * (4 * n), out_specs=[vmem] * (3 * n),
        out_shape=[jax.ShapeDtypeStruct(w.shape, F32) for w in ws for _ in range(3)],
        compiler_params=_params(),
    )(*flat)
    return [tuple(outs[3 * i:3 * i + 3]) for i in range(n)]


_PARAMS = (
    ("rel_bias", None), ("norm_mix_pre", None), ("norm_mix_post", None), ("w_in", 1), ("conv_rnn_w", 1),
    ("conv_rnn_b", None), ("w_rg_a", None), ("b_rg_a", None), ("w_rg_x", None), ("b_rg_x", None),
    ("lru_lambda", None), ("w_branch_rnn", 0), ("w_branch_att", 1), ("w_out", 0), ("norm_ffn_pre", None),
    ("norm_ffn_post", None), ("w_ffn_gate", 1), ("w_ffn_up", 1), ("conv_ffn_w", 1), ("conv_ffn_b", None),
    ("w_ffn_down", 0),
)
_SMALL = 65536


def _as2d(a):
    a = a[0] if a.shape[0] == 1 and a.ndim >= 3 else a
    return a.reshape(-1, a.shape[-1]) if a.ndim == 3 else a


def _pack(pieces, dtype):
    flat = jnp.concatenate([p.astype(dtype).reshape(-1) for p in pieces])
    unit = PACK_W * PACK_ROWS
    pad = (-flat.shape[0]) % unit
    flat = jnp.pad(flat, (0, pad))
    return flat.reshape(-1, PACK_W)


def _unpack(buf, shapes):
    flat = buf.reshape(-1)
    out, off = [], 0
    for shp in shapes:
        n = int(np.prod(shp))
        out.append(flat[off:off + n].reshape(shp))
        off += n
    return out


def _join(slots, ax):
    if ax == 0:
        return slots.reshape(-1, slots.shape[-1])
    return jnp.transpose(slots, (1, 0, 2)).reshape(slots.shape[1], -1)


def _cut(full, ax):
    if ax == 0:
        return full.reshape(N_CHIPS, -1, full.shape[-1])
    return jnp.transpose(full.reshape(full.shape[0], N_CHIPS, -1), (1, 0, 2))


def kernel(x, rel_bias, norm_mix_pre, norm_mix_post, w_in, conv_rnn_w, conv_rnn_b, w_rg_a, b_rg_a, w_rg_x, b_rg_x, lru_lambda, w_branch_rnn, w_branch_att, w_out, norm_ffn_pre, norm_ffn_post, w_ffn_gate, w_ffn_up, conv_ffn_w, conv_ffn_b, w_ffn_down, loss_target, m_rel_bias, m_norm_mix_pre, m_norm_mix_post, m_w_in, m_conv_rnn_w, m_conv_rnn_b, m_w_rg_a, m_b_rg_a, m_w_rg_x, m_b_rg_x, m_lru_lambda, m_w_branch_rnn, m_w_branch_att, m_w_out, m_norm_ffn_pre, m_norm_ffn_post, m_w_ffn_gate, m_w_ffn_up, m_conv_ffn_w, m_conv_ffn_b, m_w_ffn_down, v_rel_bias, v_norm_mix_pre, v_norm_mix_post, v_w_in, v_conv_rnn_w, v_conv_rnn_b, v_w_rg_a, v_b_rg_a, v_w_rg_x, v_b_rg_x, v_lru_lambda, v_w_branch_rnn, v_w_branch_att, v_w_out, v_norm_ffn_pre, v_norm_ffn_post, v_w_ffn_gate, v_w_ffn_up, v_conv_ffn_w, v_conv_ffn_b, v_w_ffn_down):
    args = dict(locals())
    names = [n for n, _ in _PARAMS]
    axis = dict(_PARAMS)
    w_loc = {n: args[n] for n in names}
    m_loc = {n: args["m_" + n] for n in names}
    v_loc = {n: args["v_" + n] for n in names}
    sharded = [n for n in names if axis[n] is not None]
    replicated = [n for n in names if axis[n] is None]

    big = [n for n in sharded if w_loc[n].size >= _SMALL]
    small_sharded = [n for n in sharded if n not in big]
    small = replicated + small_sharded

    first = ["w_in"] + small_sharded
    srcs = [_as2d(w_loc[n]).astype(BF16) if n in big else _as2d(w_loc[n]) for n in first]
    (gathered,) = exchange([_Gather(srcs)], "gather_first")
    p = {n: _join(a, axis[n]) for n, a in zip(first, gathered)}
    for n in replicated:
        p[n] = _as2d(w_loc[n])
    shards = {n: _as2d(w_loc[n]).astype(BF16) for n in big if n not in first}

    last = "norm_mix_pre"
    early = [n for n in small if n != last]
    received, sibling, g_small, loss_part = _local_step(x, loss_target, p, shards, early)

    ((received["last"],),) = exchange([_Scatter([], [_pack([g_small[last]], BF16)])], "scatter_last")
    late = [n for n in received if n not in sibling]
    (swapped,) = exchange([_Swap([received[n] for n in late])], "swap_last")
    sibling.update(zip(late, swapped))
    early_sum = sum_pairs(received["small"], sibling["small"], "sum_small")
    last_sum = sum_pairs(received["last"], sibling["last"], "sum_last")
    g_tot = dict(zip(early, _unpack(early_sum, [g_small[n].shape for n in early])))
    (g_tot[last],) = _unpack(last_sum, [g_small[last].shape])
    chip = 2 * lax.axis_index("x") + lax.axis_index("y")
    for n in small_sharded:
        size = g_tot[n].shape[axis[n]] // N_CHIPS
        g_tot[n] = lax.dynamic_slice_in_dim(g_tot[n], chip * size, size, axis=axis[n])

    out_g, out_d, out_m, out_v = {}, {}, {}, {}
    for n in big:
        res = adamw(_as2d(w_loc[n]), _as2d(m_loc[n]), _as2d(v_loc[n]), received[n], sibling[n], "adamw_" + n)
        out_g[n], out_d[n], out_m[n], out_v[n] = (t.reshape(w_loc[n].shape) for t in res)
    res = adamw_small([_as2d(w_loc[n]) for n in small], [_as2d(m_loc[n]) for n in small],
                      [_as2d(v_loc[n]) for n in small], [g_tot[n] for n in small], "adamw_small")
    for n, (d, nm, nv) in zip(small, res):
        out_g[n], out_d[n], out_m[n], out_v[n] = (t.reshape(w_loc[n].shape) for t in (g_tot[n], d, nm, nv))

    d_model = x.shape[-1]
    loss = lax.psum(0.5 * jnp.sum(loss_part) / d_model, ("x", "y", "c"))
    grad_x = g_small["x"]
    return (loss, grad_x, *[out_g[n] for n in names], *[out_d[n] for n in names],
            *[out_m[n] for n in names], *[out_v[n] for n in names])


def _local_step(x, target, p, shards, small_early):
    axis = dict(_PARAMS)
    b, s, d = x.shape
    t = b * s
    rnn = p["b_rg_a"].shape[1]
    ffn = p["conv_ffn_b"].shape[1]
    nbk = rnn // p["w_rg_a"].shape[1]
    hkv = (p["w_in"].shape[1] - rnn - 2 * d) // (N_GROUPS + 2)
    h = hkv // HEAD_DIM
    nq = N_GROUPS * hkv

    x2 = x.reshape(t, d)
    tgt = target.reshape(t, d)
    w_in = p["w_in"]
    in_splits = (rnn, nq + 2 * hkv, 2 * d)
    wa = p["w_rg_a"].reshape(nbk, -1, p["w_rg_a"].shape[1]).astype(BF16)
    wx = p["w_rg_x"].reshape(nbk, -1, p["w_rg_x"].shape[1]).astype(BF16)
    cw_r, cb_r = p["conv_rnn_w"], p["conv_rnn_b"]
    cw_f, cb_f = p["conv_ffn_w"], p["conv_ffn_b"]

    masks, buckets = zip(*[_band(w_, r_) for w_, r_ in DILATED])
    bucket_f = jnp.asarray(np.where(np.stack(masks), np.stack(buckets), -1).astype(np.float32))
    rel_rows = jnp.pad(p["rel_bias"].T, ((0, 0), (0, 128 - REL_BUCKETS)))[:, None, :]
    biasm = bias_table(rel_rows, bucket_f, h, "bias_table")

    early = ["w_branch_rnn", "w_branch_att", "w_out"]
    hn1, (xr, qkv, gts), (got,) = norm_mm(x2, p["norm_mix_pre"], [w_in], [in_splits], "in_proj",
                                          ride=[_Gather([shards[n] for n in early])])
    p.update({n: _join(a, axis[n]) for n, a in zip(early, got)})
    xr3 = xr.reshape(b, s, rnn)
    (y_rnn, a_rnn, xc_rnn), (got,) = rglru_fwd(xr3, cw_r, cb_r, wa, p["b_rg_a"], wx, p["b_rg_x"], p["lru_lambda"], "rglru_fwd",
                              ride=[_Gather([shards[n] for n in ("w_ffn_gate", "w_ffn_up")])])
    p.update(zip(("w_ffn_gate", "w_ffn_up"), got))
    qkv3 = qkv.reshape(b, s, -1)
    o_att, lse, ((got,),) = attn_fwd(qkv3, biasm, h, "attn_fwd", ride=[_Gather([shards["w_ffn_down"]])])
    p["w_ffn_down"] = _join(got, axis["w_ffn_down"])
    merged, br, ba, mix, h1 = merge_out(y_rnn.reshape(t, rnn), o_att.reshape(t, hkv), gts, p["w_branch_rnn"],
                                        p["w_branch_att"], p["w_out"], p["norm_mix_post"], x2, "merge_out")
    hn2, gate_pre, up, act = ffn_in_act(h1, p["norm_ffn_pre"], p["w_ffn_gate"], p["w_ffn_up"], cw_f, cb_f, s, "ffn_in")

    g, gb = {}, {}
    recv, sib = {}, {}

    def rows4(a):
        return a.reshape(N_CHIPS, -1, a.shape[-1])

    dy, dff, dact, g["norm_ffn_post"], loss_part = ffn_down_loss(act, p["w_ffn_down"], p["norm_ffn_post"], h1, tgt,
                                                                  "ffn_down")
    gb["w_ffn_down"] = rows4(mm_tn(act, [dff], "ffn_down_dw"))
    (dgp, dup, dhn2, g["conv_ffn_w"], g["conv_ffn_b"]), ((recv["w_ffn_down"],),) = ffn_in_bwd(
        dact, gate_pre, up, cw_f, cb_f, p["w_ffn_gate"], p["w_ffn_up"], s, "ffn_in_bwd",
        ride=[_Scatter([gb["w_ffn_down"]])])
    gb["w_ffn_gate"] = mm_tn(hn2, [dgp], "ffn_gate_dw", col_shards=N_CHIPS)
    gb["w_ffn_up"] = mm_tn(hn2, [dup], "ffn_up_dw", col_shards=N_CHIPS)
    (dh1, dgts, dy_rnn, do_att, g["norm_ffn_pre"], g["norm_mix_post"], dw_out, dw_br,
     gb["w_branch_att"]) = mid_bwd(dhn2, h1, p["norm_ffn_pre"], dy, mix, p["norm_mix_post"], p["w_out"], gts, br, ba,
                                   p["w_branch_rnn"], p["w_branch_att"], merged, y_rnn.reshape(t, rnn),
                                   o_att.reshape(t, hkv), "mid_bwd")
    gb["w_out"], gb["w_branch_rnn"] = rows4(dw_out), rows4(dw_br)
    ffn_in = ["w_ffn_gate", "w_ffn_up"]
    (dxr, g["conv_rnn_w"], g["conv_rnn_b"], dwa, g["b_rg_a"], dwx, g["b_rg_x"], g["lru_lambda"]), (got,) = rglru_bwd(
        xr3, y_rnn, dy_rnn.reshape(b, s, rnn), a_rnn, xc_rnn, cw_r, wa, p["b_rg_a"], wx, p["b_rg_x"], p["lru_lambda"], "rglru_bwd",
        ride=[_Scatter([gb[n] for n in ffn_in])])
    recv.update(zip(ffn_in, got))
    g["w_rg_a"] = dwa.reshape(p["w_rg_a"].shape)
    g["w_rg_x"] = dwx.reshape(p["w_rg_x"].shape)
    mid = ["w_out", "w_branch_rnn", "w_branch_att"]
    early_recv = ["w_ffn_down"] + ffn_in
    (dq1, dq2, dq3, dk, dv, ds_sum), (got, swapped) = attn_bwd(
        qkv3, biasm, o_att, lse, do_att.reshape(b, s, hkv), h, "attn_bwd",
        ride=[_Scatter([gb[n] for n in mid]), _Swap([recv[n] for n in early_recv])])
    recv.update(zip(mid, got))
    sib.update(zip(early_recv, swapped))
    rows = bias_grad(ds_sum, bucket_f, "bias_grad")
    g["rel_bias"] = rows[:, 0, :REL_BUCKETS].T
    dproj = [dxr.reshape(t, rnn)] + [a.reshape(t, hkv) for a in (dq1, dq2, dq3, dk, dv)] + [dgts]
    lanes = 128
    cut_at = w_in.shape[1] // 2 - (rnn + nq)
    direct = 0 < cut_at < hkv and cut_at % lanes == 0 and hkv % lanes == 0
    if direct:
        first_half = dproj[:4] + [(dproj[4], cut_at, 0)]
        second_half = [(dproj[4], lanes, j) for j in range(cut_at // lanes, hkv // lanes)] + dproj[5:]
    else:
        first_half, second_half = dproj[:4], dproj[4:]
    shards_per_call, calls = (N_CHIPS // 2, 2) if direct else (1, 1)
    dw_a, (got,) = mm_tn(hn1, first_half, "in_proj_dw_a", col_shards=shards_per_call, groups=calls, group=0,
                         ride=[_Swap([recv[n] for n in mid])])
    sib.update(zip(mid, got))
    pack = _pack([g[n] for n in small_early], BF16)
    dw_b, ((recv["small"],),) = mm_tn(hn1, second_half, "in_proj_dw_b", col_shards=shards_per_call, groups=calls,
                                      group=calls - 1, into=dw_a if direct else None, ride=[_Scatter([], [pack])])
    if direct:
        gb["w_in"] = dw_b
    else:
        gb["w_in"] = _cut(jnp.concatenate([dw_a[0], dw_b[0]], axis=1), 1)
    dx, g["norm_mix_pre"], ((recv["w_in"],),) = mm_nt(
        [(dproj, w_in)], "in_proj_dx", norm=(x2, p["norm_mix_pre"], dh1), ride=[_Scatter([gb["w_in"]])])
    g["x"] = dx.reshape(b, s, d)
    return recv, sib, g, loss_part
```
